```python
import jax, jax.numpy as jnp
from jax import lax
import numpy as np

D_MODEL = 1024
BATCH = 8
SEQ = 4096
DEPTH = 4

CHUNK = 64
N_MIXERS = 2
EPS = 1e-6

SSD_EXPAND = 2
SSD_D_INNER = SSD_EXPAND * D_MODEL
SSD_HEAD_DIM = 64
SSD_N_HEADS = SSD_D_INNER // SSD_HEAD_DIM
SSD_N_GROUPS = 8
SSD_HEADS_PER_GROUP = SSD_N_HEADS // SSD_N_GROUPS
SSD_D_STATE = 128
SSD_CONV_W = 4
SSD_BC_DIM = SSD_N_GROUPS * SSD_D_STATE
SSD_CONV_DIM = SSD_D_INNER + 2 * SSD_BC_DIM
SSD_IN_DIM = SSD_D_INNER + SSD_CONV_DIM + SSD_N_HEADS

SC_WIDTH = D_MODEL
SC_CONV_W = 3

FFN_HIDDEN = 2816
FFN_CONV_W = 3

N_SSD_LAYERS = (DEPTH + 1) // 2
N_SC_LAYERS = DEPTH // 2

kernel_name = "hybrid_ssd_shortconv_convffn_sandwich"


def rms_norm(x, g):
    xf = x.astype(jnp.float32)
    y = xf * lax.rsqrt(jnp.mean(xf * xf, axis=-1, keepdims=True) + EPS)
    return (y * g.astype(jnp.float32)).astype(x.dtype)


def causal_dwconv(x, w, b=None):
    k = w.shape[0]
    length = x.shape[1]
    xp = jnp.pad(x, ((0, 0), (k - 1, 0), (0, 0)))
    out = xp[:, 0:length] * w[0]
    for t in range(1, k):
        out = out + xp[:, t:t + length] * w[t]
    if b is not None:
        out = out + b
    return out


def ssd_scan(x, dt, a_head, b_in, c_in):
    bsz, length, h, p = x.shape
    g, n = b_in.shape[2], b_in.shape[3]
    r = h // g
    nc = length // CHUNK
    xf = (x.astype(jnp.float32) * dt[..., None]).reshape(bsz, nc, CHUNK, g, r, p)
    a = (dt * a_head).reshape(bsz, nc, CHUNK, g, r)
    a = jnp.moveaxis(a, 2, -1)
    a_cs = jnp.cumsum(a, axis=-1)
    bc = b_in.astype(jnp.float32).reshape(bsz, nc, CHUNK, g, n)
    cc = c_in.astype(jnp.float32).reshape(bsz, nc, CHUNK, g, n)
    seg = a_cs[..., :, None] - a_cs[..., None, :]
    tri = jnp.tril(jnp.ones((CHUNK, CHUNK), dtype=bool))
    lmat = jnp.exp(jnp.where(tri, seg, -jnp.inf))
    cb = jnp.einsum("bclgn,bcsgn->bcgls", cc, bc)
    y_diag = jnp.einsum("bcgls,bcgrls,bcsgrp->bclgrp", cb, lmat, xf)
    decay_states = jnp.exp(a_cs[..., -1:] - a_cs)
    states = jnp.einsum("bclgn,bcgrl,bclgrp->bcgrpn", bc, decay_states, xf)
    chunk_decay = jnp.exp(a_cs[..., -1])

    def step(s, inp):
        st, dec = inp
        return s * dec[..., None, None] + st, s

    init = jnp.zeros((bsz, g, r, p, n), jnp.float32)
    _, prev = lax.scan(step, init, (jnp.moveaxis(states, 1, 0), jnp.moveaxis(chunk_decay, 1, 0)))
    prev = jnp.moveaxis(prev, 0, 1)
    y_off = jnp.einsum("bclgn,bcgrpn,bcgrl->bclgrp", cc, prev, jnp.exp(a_cs))
    return (y_diag + y_off).reshape(bsz, length, h, p)


def ssd_mixer(h, w_in, conv_w, conv_b, dt_bias, a_log, d_skip, norm_w, w_out):
    bsz, length, _ = h.shape
    zxbcdt = h @ w_in
    z = zxbcdt[..., :SSD_D_INNER]
    xbc = zxbcdt[..., SSD_D_INNER:SSD_D_INNER + SSD_CONV_DIM]
    dt_raw = zxbcdt[..., SSD_D_INNER + SSD_CONV_DIM:]
    xbc = jax.nn.silu(causal_dwconv(xbc, conv_w, conv_b))
    xs = xbc[..., :SSD_D_INNER].reshape(bsz, length, SSD_N_HEADS, SSD_HEAD_DIM)
    bs = xbc[..., SSD_D_INNER:SSD_D_INNER + SSD_BC_DIM].reshape(bsz, length, SSD_N_GROUPS, SSD_D_STATE)
    cs = xbc[..., SSD_D_INNER + SSD_BC_DIM:].reshape(bsz, length, SSD_N_GROUPS, SSD_D_STATE)
    dt = jax.nn.softplus(dt_raw.astype(jnp.float32) + dt_bias.astype(jnp.float32))
    a_head = -jnp.exp(a_log.astype(jnp.float32))
    y = ssd_scan(xs, dt, a_head, bs, cs)
    y = y + xs.astype(jnp.float32) * d_skip.astype(jnp.float32)[:, None]
    y = y.reshape(bsz, length, SSD_D_INNER).astype(h.dtype)
    y = rms_norm(y * jax.nn.silu(z), norm_w)
    return y @ w_out


def shortconv_mixer(h, w_in, conv_w, w_out):
    bcv = h @ w_in
    gb = bcv[..., :SC_WIDTH]
    gc = bcv[..., SC_WIDTH:2 * SC_WIDTH]
    v = bcv[..., 2 * SC_WIDTH:]
    u = causal_dwconv(gc * v, conv_w)
    return (gb * u) @ w_out


def conv_ffn(h, w_up, conv_w, conv_b, w_down):
    up = h @ w_up
    gate = causal_dwconv(up[..., :FFN_HIDDEN], conv_w, conv_b)
    val = up[..., FFN_HIDDEN:]
    return (jax.nn.silu(gate) * val) @ w_down


def _fwd_setup_inputs(seed: int = 0) -> dict:
    key = jax.random.key(seed)
    ks = jax.random.split(key, 20)
    f32 = jnp.float32
    nrm = lambda k, shape, s: jax.random.normal(k, shape, f32) * s
    x = jax.random.normal(ks[0], (BATCH, SEQ, D_MODEL), f32)
    gains = lambda k: 1.0 + nrm(k, (DEPTH, D_MODEL), 0.02)
    dt0 = jnp.exp(jax.random.uniform(ks[8], (N_SSD_LAYERS, SSD_N_HEADS), f32,
                                     np.float32(np.log(1e-3)), np.float32(np.log(1e-1))))
    return {
        "x": x,
        "mix_pre_g": gains(ks[1]),
        "mix_post_g": gains(ks[2]),
        "ffn_pre_g": gains(ks[3]),
        "ffn_post_g": gains(ks[4]),
        "ssd_w_in": nrm(ks[5], (N_SSD_LAYERS, D_MODEL, SSD_IN_DIM), D_MODEL ** -0.5),
        "ssd_conv_w": nrm(ks[6], (N_SSD_LAYERS, SSD_CONV_W, SSD_CONV_DIM), SSD_CONV_W ** -0.5),
        "ssd_conv_b": nrm(ks[7], (N_SSD_LAYERS, SSD_CONV_DIM), 0.01),
        "ssd_dt_bias": dt0 + jnp.log(-jnp.expm1(-dt0)),
        "ssd_A_log": jnp.log(jax.random.uniform(ks[9], (N_SSD_LAYERS, SSD_N_HEADS), f32, 1.0, 16.0)),
        "ssd_D": 1.0 + nrm(ks[10], (N_SSD_LAYERS, SSD_N_HEADS), 0.1),
        "ssd_norm_w": 1.0 + nrm(ks[11], (N_SSD_LAYERS, SSD_D_INNER), 0.02),
        "ssd_w_out": nrm(ks[12], (N_SSD_LAYERS, SSD_D_INNER, D_MODEL), SSD_D_INNER ** -0.5),
        "sc_w_in": nrm(ks[13], (N_SC_LAYERS, D_MODEL, 3 * SC_WIDTH), D_MODEL ** -0.5),
        "sc_conv_w": nrm(ks[14], (N_SC_LAYERS, SC_CONV_W, SC_WIDTH), SC_CONV_W ** -0.5),
        "sc_w_out": nrm(ks[15], (N_SC_LAYERS, SC_WIDTH, D_MODEL), SC_WIDTH ** -0.5),
        "ffn_w_up": nrm(ks[16], (DEPTH, D_MODEL, 2 * FFN_HIDDEN), D_MODEL ** -0.5),
        "ffn_conv_w": nrm(ks[17], (DEPTH, FFN_CONV_W, FFN_HIDDEN), FFN_CONV_W ** -0.5),
        "ffn_conv_b": nrm(ks[18], (DEPTH, FFN_HIDDEN), 0.01),
        "ffn_w_down": nrm(ks[19], (DEPTH, FFN_HIDDEN, D_MODEL), FFN_HIDDEN ** -0.5),
    }


def _fwd_reference(x, mix_pre_g, mix_post_g, ffn_pre_g, ffn_post_g,
              ssd_w_in, ssd_conv_w, ssd_conv_b, ssd_dt_bias, ssd_A_log, ssd_D,
              ssd_norm_w, ssd_w_out, sc_w_in, sc_conv_w, sc_w_out,
              ffn_w_up, ffn_conv_w, ffn_conv_b, ffn_w_down):
    for i in range(DEPTH):
        j = i // N_MIXERS
        h = rms_norm(x, mix_pre_g[i])
        if i % N_MIXERS == 0:
            m = ssd_mixer(h, ssd_w_in[j], ssd_conv_w[j], ssd_conv_b[j], ssd_dt_bias[j],
                          ssd_A_log[j], ssd_D[j], ssd_norm_w[j], ssd_w_out[j])
        else:
            m = shortconv_mixer(h, sc_w_in[j], sc_conv_w[j], sc_w_out[j])
        x = x + rms_norm(m, mix_post_g[i])
        f = conv_ffn(rms_norm(x, ffn_pre_g[i]), ffn_w_up[i], ffn_conv_w[i], ffn_conv_b[i], ffn_w_down[i])
        x = x + rms_norm(f, ffn_post_g[i])
    return x


import jax as _jax
import jax.numpy as _jnp

TWIN_FORMAT = 'train_step'
FWD_PARAMS = ['x', 'mix_pre_g', 'mix_post_g', 'ffn_pre_g', 'ffn_post_g', 'ssd_w_in', 'ssd_conv_w', 'ssd_conv_b', 'ssd_dt_bias', 'ssd_A_log', 'ssd_D', 'ssd_norm_w', 'ssd_w_out', 'sc_w_in', 'sc_conv_w', 'sc_w_out', 'ffn_w_up', 'ffn_conv_w', 'ffn_conv_b', 'ffn_w_down']
TWIN_WEIGHTS = ['mix_pre_g', 'mix_post_g', 'ffn_pre_g', 'ffn_post_g', 'ssd_w_in', 'ssd_conv_w', 'ssd_conv_b', 'ssd_dt_bias', 'ssd_A_log', 'ssd_D', 'ssd_norm_w', 'ssd_w_out', 'sc_w_in', 'sc_conv_w', 'sc_w_out', 'ffn_w_up', 'ffn_conv_w', 'ffn_conv_b', 'ffn_w_down']
TWIN_DIFF_INPUT = 'x'
TWIN_INPUTS = ['x', 'mix_pre_g', 'mix_post_g', 'ffn_pre_g', 'ffn_post_g', 'ssd_w_in', 'ssd_conv_w', 'ssd_conv_b', 'ssd_dt_bias', 'ssd_A_log', 'ssd_D', 'ssd_norm_w', 'ssd_w_out', 'sc_w_in', 'sc_conv_w', 'sc_w_out', 'ffn_w_up', 'ffn_conv_w', 'ffn_conv_b', 'ffn_w_down', 'loss_target', 'm_mix_pre_g', 'm_mix_post_g', 'm_ffn_pre_g', 'm_ffn_post_g', 'm_ssd_w_in', 'm_ssd_conv_w', 'm_ssd_conv_b', 'm_ssd_dt_bias', 'm_ssd_A_log', 'm_ssd_D', 'm_ssd_norm_w', 'm_ssd_w_out', 'm_sc_w_in', 'm_sc_conv_w', 'm_sc_w_out', 'm_ffn_w_up', 'm_ffn_conv_w', 'm_ffn_conv_b', 'm_ffn_w_down', 'v_mix_pre_g', 'v_mix_post_g', 'v_ffn_pre_g', 'v_ffn_post_g', 'v_ssd_w_in', 'v_ssd_conv_w', 'v_ssd_conv_b', 'v_ssd_dt_bias', 'v_ssd_A_log', 'v_ssd_D', 'v_ssd_norm_w', 'v_ssd_w_out', 'v_sc_w_in', 'v_sc_conv_w', 'v_sc_w_out', 'v_ffn_w_up', 'v_ffn_conv_w', 'v_ffn_conv_b', 'v_ffn_w_down']
TWIN_OUTPUTS = ['loss', 'grad_x', 'grad_mix_pre_g', 'grad_mix_post_g', 'grad_ffn_pre_g', 'grad_ffn_post_g', 'grad_ssd_w_in', 'grad_ssd_conv_w', 'grad_ssd_conv_b', 'grad_ssd_dt_bias', 'grad_ssd_A_log', 'grad_ssd_D', 'grad_ssd_norm_w', 'grad_ssd_w_out', 'grad_sc_w_in', 'grad_sc_conv_w', 'grad_sc_w_out', 'grad_ffn_w_up', 'grad_ffn_conv_w', 'grad_ffn_conv_b', 'grad_ffn_w_down', 'delta_mix_pre_g', 'delta_mix_post_g', 'delta_ffn_pre_g', 'delta_ffn_post_g', 'delta_ssd_w_in', 'delta_ssd_conv_w', 'delta_ssd_conv_b', 'delta_ssd_dt_bias', 'delta_ssd_A_log', 'delta_ssd_D', 'delta_ssd_norm_w', 'delta_ssd_w_out', 'delta_sc_w_in', 'delta_sc_conv_w', 'delta_sc_w_out', 'delta_ffn_w_up', 'delta_ffn_conv_w', 'delta_ffn_conv_b', 'delta_ffn_w_down', 'new_m_mix_pre_g', 'new_m_mix_post_g', 'new_m_ffn_pre_g', 'new_m_ffn_post_g', 'new_m_ssd_w_in', 'new_m_ssd_conv_w', 'new_m_ssd_conv_b', 'new_m_ssd_dt_bias', 'new_m_ssd_A_log', 'new_m_ssd_D', 'new_m_ssd_norm_w', 'new_m_ssd_w_out', 'new_m_sc_w_in', 'new_m_sc_conv_w', 'new_m_sc_w_out', 'new_m_ffn_w_up', 'new_m_ffn_conv_w', 'new_m_ffn_conv_b', 'new_m_ffn_w_down', 'new_v_mix_pre_g', 'new_v_mix_post_g', 'new_v_ffn_pre_g', 'new_v_ffn_post_g', 'new_v_ssd_w_in', 'new_v_ssd_conv_w', 'new_v_ssd_conv_b', 'new_v_ssd_dt_bias', 'new_v_ssd_A_log', 'new_v_ssd_D', 'new_v_ssd_norm_w', 'new_v_ssd_w_out', 'new_v_sc_w_in', 'new_v_sc_conv_w', 'new_v_sc_w_out', 'new_v_ffn_w_up', 'new_v_ffn_conv_w', 'new_v_ffn_conv_b', 'new_v_ffn_w_down']
TWIN_LEAF_KINDS = {'loss': 'loss', 'grad_x': 'grad_x', 'grad_mix_pre_g': 'grad_w', 'grad_mix_post_g': 'grad_w', 'grad_ffn_pre_g': 'grad_w', 'grad_ffn_post_g': 'grad_w', 'grad_ssd_w_in': 'grad_w', 'grad_ssd_conv_w': 'grad_w', 'grad_ssd_conv_b': 'grad_w', 'grad_ssd_dt_bias': 'grad_w', 'grad_ssd_A_log': 'grad_w', 'grad_ssd_D': 'grad_w', 'grad_ssd_norm_w': 'grad_w', 'grad_ssd_w_out': 'grad_w', 'grad_sc_w_in': 'grad_w', 'grad_sc_conv_w': 'grad_w', 'grad_sc_w_out': 'grad_w', 'grad_ffn_w_up': 'grad_w', 'grad_ffn_conv_w': 'grad_w', 'grad_ffn_conv_b': 'grad_w', 'grad_ffn_w_down': 'grad_w', 'delta_mix_pre_g': 'delta_w', 'delta_mix_post_g': 'delta_w', 'delta_ffn_pre_g': 'delta_w', 'delta_ffn_post_g': 'delta_w', 'delta_ssd_w_in': 'delta_w', 'delta_ssd_conv_w': 'delta_w', 'delta_ssd_conv_b': 'delta_w', 'delta_ssd_dt_bias': 'delta_w', 'delta_ssd_A_log': 'delta_w', 'delta_ssd_D': 'delta_w', 'delta_ssd_norm_w': 'delta_w', 'delta_ssd_w_out': 'delta_w', 'delta_sc_w_in': 'delta_w', 'delta_sc_conv_w': 'delta_w', 'delta_sc_w_out': 'delta_w', 'delta_ffn_w_up': 'delta_w', 'delta_ffn_conv_w': 'delta_w', 'delta_ffn_conv_b': 'delta_w', 'delta_ffn_w_down': 'delta_w', 'new_m_mix_pre_g': 'new_m', 'new_m_mix_post_g': 'new_m', 'new_m_ffn_pre_g': 'new_m', 'new_m_ffn_post_g': 'new_m', 'new_m_ssd_w_in': 'new_m', 'new_m_ssd_conv_w': 'new_m', 'new_m_ssd_conv_b': 'new_m', 'new_m_ssd_dt_bias': 'new_m', 'new_m_ssd_A_log': 'new_m', 'new_m_ssd_D': 'new_m', 'new_m_ssd_norm_w': 'new_m', 'new_m_ssd_w_out': 'new_m', 'new_m_sc_w_in': 'new_m', 'new_m_sc_conv_w': 'new_m', 'new_m_sc_w_out': 'new_m', 'new_m_ffn_w_up': 'new_m', 'new_m_ffn_conv_w': 'new_m', 'new_m_ffn_conv_b': 'new_m', 'new_m_ffn_w_down': 'new_m', 'new_v_mix_pre_g': 'new_v', 'new_v_mix_post_g': 'new_v', 'new_v_ffn_pre_g': 'new_v', 'new_v_ffn_post_g': 'new_v', 'new_v_ssd_w_in': 'new_v', 'new_v_ssd_conv_w': 'new_v', 'new_v_ssd_conv_b': 'new_v', 'new_v_ssd_dt_bias': 'new_v', 'new_v_ssd_A_log': 'new_v', 'new_v_ssd_D': 'new_v', 'new_v_ssd_norm_w': 'new_v', 'new_v_ssd_w_out': 'new_v', 'new_v_sc_w_in': 'new_v', 'new_v_sc_conv_w': 'new_v', 'new_v_sc_w_out': 'new_v', 'new_v_ffn_w_up': 'new_v', 'new_v_ffn_conv_w': 'new_v', 'new_v_ffn_conv_b': 'new_v', 'new_v_ffn_w_down': 'new_v'}


def _forward(args):
    return _fwd_reference(*[args[k] for k in FWD_PARAMS])


def _output_shape():
    def fwd():
        inp = _fwd_setup_inputs(0)
        return _fwd_reference(*[inp[k] for k in FWD_PARAMS])
    out = _jax.eval_shape(fwd)
    return out.shape, out.dtype

N_MICROBATCH = 1
ADAM_LR = 0.001
ADAM_B1 = 0.9
ADAM_B2 = 0.999
ADAM_EPS = 1e-08
ADAM_WD = 0.01
ADAM_STEP = 10
PER_EXAMPLE_BATCH_AXIS = {'x': 0, 'loss_target': 0}
SHARED_INPUTS = []
_WEIGHT_DTYPES = {'mix_pre_g': _jnp.float32, 'mix_post_g': _jnp.float32, 'ffn_pre_g': _jnp.float32, 'ffn_post_g': _jnp.float32, 'ssd_w_in': _jnp.float32, 'ssd_conv_w': _jnp.float32, 'ssd_conv_b': _jnp.float32, 'ssd_dt_bias': _jnp.float32, 'ssd_A_log': _jnp.float32, 'ssd_D': _jnp.float32, 'ssd_norm_w': _jnp.float32, 'ssd_w_out': _jnp.float32, 'sc_w_in': _jnp.float32, 'sc_conv_w': _jnp.float32, 'sc_w_out': _jnp.float32, 'ffn_w_up': _jnp.float32, 'ffn_conv_w': _jnp.float32, 'ffn_conv_b': _jnp.float32, 'ffn_w_down': _jnp.float32}
MOMENT_SCALE = {'mix_pre_g': 2.345038e+00, 'mix_post_g': 3.185711e+01, 'ffn_pre_g': 1.811805e+00, 'ffn_post_g': 3.196657e+01, 'ssd_w_in': 1.133151e+00, 'ssd_conv_w': 1.052422e+00, 'ssd_conv_b': 2.067868e+00, 'ssd_dt_bias': 3.012793e+00, 'ssd_A_log': 7.019970e+00, 'ssd_D': 8.354920e+00, 'ssd_norm_w': 1.501351e+00, 'ssd_w_out': 2.366016e+00, 'sc_w_in': 9.735301e-01, 'sc_conv_w': 9.905453e-01, 'sc_w_out': 1.029199e+00, 'ffn_w_up': 7.584282e-01, 'ffn_conv_w': 8.075247e-01, 'ffn_conv_b': 1.328256e+00, 'ffn_w_down': 1.375765e+00}


def _to_microbatches(a, axis):
    t = _jnp.moveaxis(a, axis, 0)
    t = t.reshape((N_MICROBATCH, t.shape[0] // N_MICROBATCH) + t.shape[1:])
    return _jnp.moveaxis(t, 1, axis + 1)


def setup_inputs(seed: int = 0) -> dict:
    inp = _fwd_setup_inputs(seed)
    key = _jax.random.fold_in(_jax.random.key(seed), 7919)
    shape, _ = _output_shape()
    out = dict(inp)
    out["loss_target"] = _jax.random.normal(_jax.random.fold_in(key, 0), shape, _jnp.float32)
    for i, name in enumerate(TWIN_WEIGHTS):
        w = inp[name].astype(_jnp.float32)
        if MOMENT_SCALE is None:
            s = _jnp.sqrt(_jnp.mean(_jnp.square(w)) + 1e-30)
        else:
            s = MOMENT_SCALE[name]
        km, kv = _jax.random.split(_jax.random.fold_in(key, i + 1))
        out[name] = w
        out["m_" + name] = s * _jax.random.normal(km, w.shape, _jnp.float32)
        out["v_" + name] = (s * s) * _jax.random.uniform(kv, w.shape, _jnp.float32, 0.5, 1.5)
    if N_MICROBATCH > 1:
        for name, axis in PER_EXAMPLE_BATCH_AXIS.items():
            out[name] = _to_microbatches(out[name], axis)
    return {'x': out['x'], 'mix_pre_g': out['mix_pre_g'], 'mix_post_g': out['mix_post_g'], 'ffn_pre_g': out['ffn_pre_g'], 'ffn_post_g': out['ffn_post_g'], 'ssd_w_in': out['ssd_w_in'], 'ssd_conv_w': out['ssd_conv_w'], 'ssd_conv_b': out['ssd_conv_b'], 'ssd_dt_bias': out['ssd_dt_bias'], 'ssd_A_log': out['ssd_A_log'], 'ssd_D': out['ssd_D'], 'ssd_norm_w': out['ssd_norm_w'], 'ssd_w_out': out['ssd_w_out'], 'sc_w_in': out['sc_w_in'], 'sc_conv_w': out['sc_conv_w'], 'sc_w_out': out['sc_w_out'], 'ffn_w_up': out['ffn_w_up'], 'ffn_conv_w': out['ffn_conv_w'], 'ffn_conv_b': out['ffn_conv_b'], 'ffn_w_down': out['ffn_w_down'], 'loss_target': out['loss_target'], 'm_mix_pre_g': out['m_mix_pre_g'], 'm_mix_post_g': out['m_mix_post_g'], 'm_ffn_pre_g': out['m_ffn_pre_g'], 'm_ffn_post_g': out['m_ffn_post_g'], 'm_ssd_w_in': out['m_ssd_w_in'], 'm_ssd_conv_w': out['m_ssd_conv_w'], 'm_ssd_conv_b': out['m_ssd_conv_b'], 'm_ssd_dt_bias': out['m_ssd_dt_bias'], 'm_ssd_A_log': out['m_ssd_A_log'], 'm_ssd_D': out['m_ssd_D'], 'm_ssd_norm_w': out['m_ssd_norm_w'], 'm_ssd_w_out': out['m_ssd_w_out'], 'm_sc_w_in': out['m_sc_w_in'], 'm_sc_conv_w': out['m_sc_conv_w'], 'm_sc_w_out': out['m_sc_w_out'], 'm_ffn_w_up': out['m_ffn_w_up'], 'm_ffn_conv_w': out['m_ffn_conv_w'], 'm_ffn_conv_b': out['m_ffn_conv_b'], 'm_ffn_w_down': out['m_ffn_w_down'], 'v_mix_pre_g': out['v_mix_pre_g'], 'v_mix_post_g': out['v_mix_post_g'], 'v_ffn_pre_g': out['v_ffn_pre_g'], 'v_ffn_post_g': out['v_ffn_post_g'], 'v_ssd_w_in': out['v_ssd_w_in'], 'v_ssd_conv_w': out['v_ssd_conv_w'], 'v_ssd_conv_b': out['v_ssd_conv_b'], 'v_ssd_dt_bias': out['v_ssd_dt_bias'], 'v_ssd_A_log': out['v_ssd_A_log'], 'v_ssd_D': out['v_ssd_D'], 'v_ssd_norm_w': out['v_ssd_norm_w'], 'v_ssd_w_out': out['v_ssd_w_out'], 'v_sc_w_in': out['v_sc_w_in'], 'v_sc_conv_w': out['v_sc_conv_w'], 'v_sc_w_out': out['v_sc_w_out'], 'v_ffn_w_up': out['v_ffn_w_up'], 'v_ffn_conv_w': out['v_ffn_conv_w'], 'v_ffn_conv_b': out['v_ffn_conv_b'], 'v_ffn_w_down': out['v_ffn_w_down']}


def _loss(weights, diff, rest, loss_target):
    with _jax.named_scope("forward"):
        args = {**rest, TWIN_DIFF_INPUT: diff, **{k: w.astype(_WEIGHT_DTYPES[k]) for k, w in weights.items()}}
        y = _forward(args)
    with _jax.named_scope("loss_head"):
        err = _jnp.square(y.astype(_jnp.float32) - loss_target)
        return 0.5 * _jnp.sum(_jnp.mean(err, axis=-1)) if err.ndim else 0.5 * err


def _adamw(w, g, m, v):
    m = ADAM_B1 * m + (1.0 - ADAM_B1) * g
    v = ADAM_B2 * v + (1.0 - ADAM_B2) * _jnp.square(g)
    m_hat = m / (1.0 - ADAM_B1 ** ADAM_STEP)
    v_hat = v / (1.0 - ADAM_B2 ** ADAM_STEP)
    delta = -ADAM_LR * (m_hat / (_jnp.sqrt(v_hat) + ADAM_EPS) + ADAM_WD * w)
    return delta, m, v


def reference(x, mix_pre_g, mix_post_g, ffn_pre_g, ffn_post_g, ssd_w_in, ssd_conv_w, ssd_conv_b, ssd_dt_bias, ssd_A_log, ssd_D, ssd_norm_w, ssd_w_out, sc_w_in, sc_conv_w, sc_w_out, ffn_w_up, ffn_conv_w, ffn_conv_b, ffn_w_down, loss_target, m_mix_pre_g, m_mix_post_g, m_ffn_pre_g, m_ffn_post_g, m_ssd_w_in, m_ssd_conv_w, m_ssd_conv_b, m_ssd_dt_bias, m_ssd_A_log, m_ssd_D, m_ssd_norm_w, m_ssd_w_out, m_sc_w_in, m_sc_conv_w, m_sc_w_out, m_ffn_w_up, m_ffn_conv_w, m_ffn_conv_b, m_ffn_w_down, v_mix_pre_g, v_mix_post_g, v_ffn_pre_g, v_ffn_post_g, v_ssd_w_in, v_ssd_conv_w, v_ssd_conv_b, v_ssd_dt_bias, v_ssd_A_log, v_ssd_D, v_ssd_norm_w, v_ssd_w_out, v_sc_w_in, v_sc_conv_w, v_sc_w_out, v_ffn_w_up, v_ffn_conv_w, v_ffn_conv_b, v_ffn_w_down):
    given = dict(x=x, mix_pre_g=mix_pre_g, mix_post_g=mix_post_g, ffn_pre_g=ffn_pre_g, ffn_post_g=ffn_post_g, ssd_w_in=ssd_w_in, ssd_conv_w=ssd_conv_w, ssd_conv_b=ssd_conv_b, ssd_dt_bias=ssd_dt_bias, ssd_A_log=ssd_A_log, ssd_D=ssd_D, ssd_norm_w=ssd_norm_w, ssd_w_out=ssd_w_out, sc_w_in=sc_w_in, sc_conv_w=sc_conv_w, sc_w_out=sc_w_out, ffn_w_up=ffn_w_up, ffn_conv_w=ffn_conv_w, ffn_conv_b=ffn_conv_b, ffn_w_down=ffn_w_down, loss_target=loss_target, m_mix_pre_g=m_mix_pre_g, m_mix_post_g=m_mix_post_g, m_ffn_pre_g=m_ffn_pre_g, m_ffn_post_g=m_ffn_post_g, m_ssd_w_in=m_ssd_w_in, m_ssd_conv_w=m_ssd_conv_w, m_ssd_conv_b=m_ssd_conv_b, m_ssd_dt_bias=m_ssd_dt_bias, m_ssd_A_log=m_ssd_A_log, m_ssd_D=m_ssd_D, m_ssd_norm_w=m_ssd_norm_w, m_ssd_w_out=m_ssd_w_out, m_sc_w_in=m_sc_w_in, m_sc_conv_w=m_sc_conv_w, m_sc_w_out=m_sc_w_out, m_ffn_w_up=m_ffn_w_up, m_ffn_conv_w=m_ffn_conv_w, m_ffn_conv_b=m_ffn_conv_b, m_ffn_w_down=m_ffn_w_down, v_mix_pre_g=v_mix_pre_g, v_mix_post_g=v_mix_post_g, v_ffn_pre_g=v_ffn_pre_g, v_ffn_post_g=v_ffn_post_g, v_ssd_w_in=v_ssd_w_in, v_ssd_conv_w=v_ssd_conv_w, v_ssd_conv_b=v_ssd_conv_b, v_ssd_dt_bias=v_ssd_dt_bias, v_ssd_A_log=v_ssd_A_log, v_ssd_D=v_ssd_D, v_ssd_norm_w=v_ssd_norm_w, v_ssd_w_out=v_ssd_w_out, v_sc_w_in=v_sc_w_in, v_sc_conv_w=v_sc_conv_w, v_sc_w_out=v_sc_w_out, v_ffn_w_up=v_ffn_w_up, v_ffn_conv_w=v_ffn_conv_w, v_ffn_conv_b=v_ffn_conv_b, v_ffn_w_down=v_ffn_w_down)
    weights = {n: given[n] for n in TWIN_WEIGHTS}
    shared = {n: given[n] for n in SHARED_INPUTS}
    per_example = {n: given[n] for n in ['x']}
    grad_fn = _jax.value_and_grad(_loss, argnums=(0, 1))

    def one_microbatch(ex, loss_target):
        ex = dict(ex)
        diff = ex.pop(TWIN_DIFF_INPUT)
        return grad_fn(weights, diff, {**shared, **ex}, loss_target)

    if N_MICROBATCH == 1:
        loss, (grad_w, grad_x) = one_microbatch(per_example, given["loss_target"])
    else:
        def body(carry, xs):
            loss_sum, grad_sum = carry
            l_k, (gw_k, gx_k) = one_microbatch(xs[0], xs[1])
            with _jax.named_scope("update"):
                return (loss_sum + l_k, _jax.tree.map(_jnp.add, grad_sum, gw_k)), gx_k

        init = (_jnp.zeros((), _jnp.float32), _jax.tree.map(_jnp.zeros_like, weights))
        (loss, grad_w), grad_x = _jax.lax.scan(body, init, (per_example, given["loss_target"]))
    with _jax.named_scope("update"):
        delta_w, new_m, new_v = {}, {}, {}
        for n in TWIN_WEIGHTS:
            delta_w[n], new_m[n], new_v[n] = _adamw(weights[n], grad_w[n], given["m_" + n], given["v_" + n])
    return (loss, grad_x, *[grad_w[n] for n in TWIN_WEIGHTS], *[delta_w[n] for n in TWIN_WEIGHTS],
            *[new_m[n] for n in TWIN_WEIGHTS], *[new_v[n] for n in TWIN_WEIGHTS])
```

```python
import jax
import jax.numpy as jnp
import numpy as np
from jax import lax
from jax.experimental import pallas as pl
from jax.experimental.pallas import tpu as pltpu

F32 = jnp.float32
BF16 = jnp.bfloat16
SDS = jax.ShapeDtypeStruct
MESH = pl.DeviceIdType.MESH
ANY = pl.BlockSpec(memory_space=pl.ANY)

EPS = 1e-6
CHUNK = 64
HEAD_DIM = 64
N_GROUPS = 8
D_STATE = 128
HEADS_PER_GROUP = 4
GROUP_W = HEADS_PER_GROUP * HEAD_DIM
LANE = 128
ROW_TILE = 128
HALO = 8
PACK_W = 1024
PACK_ROWS = 512
VMEM_LIMIT = 56 * 1024 * 1024

ADAM_LR = 0.001
ADAM_B1 = 0.9
ADAM_B2 = 0.999
ADAM_EPS = 1e-08
ADAM_WD = 0.01
ADAM_STEP = 10

HI = lax.Precision.HIGHEST
NT = (((1,), (1,)), ((), ()))
TN = (((0,), (0,)), ((), ()))


def _cp(*sem):
    return pltpu.CompilerParams(dimension_semantics=sem or None, vmem_limit_bytes=VMEM_LIMIT)


def _sigmoid(x):
    return 1.0 / (1.0 + jnp.exp(-x))


def _dsilu(x, s):
    return s * (1.0 + x * (1.0 - s))


def _rsq(x):
    return lax.rsqrt(jnp.mean(x * x, axis=-1, keepdims=True) + EPS)


def _norm_matmul(x, g, w, tn, name):
    L, D = x.shape
    N = w.shape[1]
    tm = min(512, L)

    def body(x_ref, g_ref, w_ref, o_ref, hn_ref):
        @pl.when(pl.program_id(1) == 0)
        def _():
            xv = x_ref[...]
            hn_ref[...] = (xv * _rsq(xv) * g_ref[...]).astype(BF16)

        o_ref[...] = jnp.dot(hn_ref[...], w_ref[...], preferred_element_type=F32)

    return pl.pallas_call(
        body, grid=(L // tm, N // tn),
        in_specs=[pl.BlockSpec((tm, D), lambda i, j: (i, 0)), pl.BlockSpec((1, D), lambda i, j: (0, 0)),
                  pl.BlockSpec((D, tn), lambda i, j: (0, j))],
        out_specs=[pl.BlockSpec((tm, tn), lambda i, j: (i, j)), pl.BlockSpec((tm, D), lambda i, j: (i, 0))],
        out_shape=[SDS((L, N), F32), SDS((L, D), BF16)],
        name=name, compiler_params=_cp("parallel", "arbitrary"))(x, g, w)


def _matmul_norm_res(a, w, x, g, name):
    L, K = a.shape
    D = w.shape[1]
    tm = min(256, L)

    def body(a_ref, w_ref, x_ref, g_ref, m_ref, xo_ref):
        m = jnp.dot(a_ref[...], w_ref[...], preferred_element_type=F32)
        m_ref[...] = m
        xo_ref[...] = x_ref[...] + m * _rsq(m) * g_ref[...]

    row = lambda i: (i, 0)
    fix = lambda i: (0, 0)
    return pl.pallas_call(
        body, grid=(L // tm,),
        in_specs=[pl.BlockSpec((tm, K), row), pl.BlockSpec((K, D), fix), pl.BlockSpec((tm, D), row),
                  pl.BlockSpec((1, D), fix)],
        out_specs=[pl.BlockSpec((tm, D), row), pl.BlockSpec((tm, D), row)],
        out_shape=[SDS((L, D), F32), SDS((L, D), F32)],
        name=name, compiler_params=_cp("parallel"))(a, w, x, g)


def _postnorm_bwd(dx, m, g, name):
    L, D = dx.shape
    tm = min(512, L)

    def body(dx_ref, m_ref, g_ref, dm_ref, dg_ref):
        @pl.when(pl.program_id(0) == 0)
        def _():
            dg_ref[...] = jnp.zeros_like(dg_ref)

        m = m_ref[...]
        dxv = dx_ref[...]
        r = _rsq(m)
        mh = m * r
        dg_ref[...] += jnp.sum(dxv * mh, axis=0, keepdims=True)
        dyg = dxv * g_ref[...]
        dm_ref[...] = (r * (dyg - mh * jnp.mean(dyg * mh, axis=-1, keepdims=True))).astype(BF16)

    row = lambda i: (i, 0)
    fix = lambda i: (0, 0)
    return pl.pallas_call(
        body, grid=(L // tm,),
        in_specs=[pl.BlockSpec((tm, D), row), pl.BlockSpec((tm, D), row), pl.BlockSpec((1, D), fix)],
        out_specs=[pl.BlockSpec((tm, D), row), pl.BlockSpec((1, D), fix)],
        out_shape=[SDS((L, D), BF16), SDS((1, D), F32)],
        name=name, compiler_params=_cp("arbitrary"))(dx, m, g)


def _matmul_nt(a, w, tn, name):
    L, D = a.shape
    K = w.shape[0]
    tm = min(512, L)

    def body(a_ref, w_ref, o_ref):
        o_ref[...] = lax.dot_general(a_ref[...], w_ref[...], NT, preferred_element_type=F32)

    return pl.pallas_call(
        body, grid=(L // tm, K // tn),
        in_specs=[pl.BlockSpec((tm, D), lambda i, j: (i, 0)), pl.BlockSpec((tn, D), lambda i, j: (j, 0))],
        out_specs=pl.BlockSpec((tm, tn), lambda i, j: (i, j)),
        out_shape=SDS((L, K), F32),
        name=name, compiler_params=_cp("parallel", "parallel"))(a, w)


def _matmul_tn(a, b, ta, tn, name):
    L, Ka = a.shape
    N = b.shape[1]
    tl = min(1024, L)
    n_l = L // tl

    def body(a_ref, b_ref, o_ref, acc_ref):
        l = pl.program_id(2)

        @pl.when(l == 0)
        def _():
            acc_ref[...] = jnp.zeros_like(acc_ref)

        acc_ref[...] += lax.dot_general(a_ref[...], b_ref[...], TN, preferred_element_type=F32)

        @pl.when(l == n_l - 1)
        def _():
            o_ref[...] = acc_ref[...]

    return pl.pallas_call(
        body, grid=(Ka // ta, N // tn, n_l),
        in_specs=[pl.BlockSpec((tl, ta), lambda i, j, l: (l, i)), pl.BlockSpec((tl, tn), lambda i, j, l: (l, j))],
        out_specs=pl.BlockSpec((ta, tn), lambda i, j, l: (i, j)),
        out_shape=SDS((Ka, N), F32),
        scratch_shapes=[pltpu.VMEM((ta, tn), F32)],
        name=name, compiler_params=_cp("parallel", "parallel", "arbitrary"))(a, b)


def _matmul_nt_prenorm_bwd(dy, w, x, g, dres, tk, name):
    L, N = dy.shape
    D = w.shape[0]
    tm = min(512, L)
    n_k = N // tk

    def body(dy_ref, w_ref, x_ref, g_ref, dres_ref, dx_ref, dg_ref, acc_ref):
        i, k = pl.program_id(0), pl.program_id(1)

        @pl.when((i == 0) & (k == 0))
        def _():
            dg_ref[...] = jnp.zeros_like(dg_ref)

        @pl.when(k == 0)
        def _():
            acc_ref[...] = jnp.zeros_like(acc_ref)

        acc_ref[...] += lax.dot_general(dy_ref[...], w_ref[...], NT, preferred_element_type=F32)

        @pl.when(k == n_k - 1)
        def _():
            dh = acc_ref[...]
            xv = x_ref[...]
            r = _rsq(xv)
            xh = xv * r
            dg_ref[...] += jnp.sum(dh * xh, axis=0, keepdims=True)
            dyg = dh * g_ref[...]
            dx_ref[...] = dres_ref[...] + r * (dyg - xh * jnp.mean(dyg * xh, axis=-1, keepdims=True))

    row = lambda i, k: (i, 0)
    fix = lambda i, k: (0, 0)
    return pl.pallas_call(
        body, grid=(L // tm, n_k),
        in_specs=[pl.BlockSpec((tm, tk), lambda i, k: (i, k)), pl.BlockSpec((D, tk), lambda i, k: (0, k)),
                  pl.BlockSpec((tm, D), row), pl.BlockSpec((1, D), fix), pl.BlockSpec((tm, D), row)],
        out_specs=[pl.BlockSpec((tm, D), row), pl.BlockSpec((1, D), fix)],
        out_shape=[SDS((L, D), F32), SDS((1, D), F32)],
        scratch_shapes=[pltpu.VMEM((tm, D), F32)],
        name=name, compiler_params=_cp("arbitrary", "arbitrary"))(dy, w, x, g, dres)


def _loss_head(y, t, name):
    L, D = y.shape
    tm = min(512, L)

    def body(y_ref, t_ref, dy_ref, loss_ref):
        @pl.when(pl.program_id(0) == 0)
        def _():
            loss_ref[...] = jnp.zeros_like(loss_ref)

        e = y_ref[...] - t_ref[...]
        dy_ref[...] = e * (1.0 / D)
        s = jnp.sum(jnp.sum(e * e, axis=1, keepdims=True), axis=0, keepdims=True)
        loss_ref[...] += s * (0.5 / D)

    row = lambda i: (i, 0)
    return pl.pallas_call(
        body, grid=(L // tm,),
        in_specs=[pl.BlockSpec((tm, D), row), pl.BlockSpec((tm, D), row)],
        out_specs=[pl.BlockSpec((tm, D), row), pl.BlockSpec((1, 1), lambda i: (0, 0))],
        out_shape=[SDS((L, D), F32), SDS((1, 1), F32)],
        name=name, compiler_params=_cp("arbitrary"))(y, t)


def _prev_rows(ref, r0, i, cols):
    p0 = pl.multiple_of(jnp.maximum(r0 - HALO, 0), HALO)
    return jnp.where(i > 0, ref[pl.ds(p0, HALO), cols], 0.0)


def _next_rows(ref, r0, i, n_tiles, cols):
    n0 = pl.multiple_of(jnp.minimum(r0 + ROW_TILE, n_tiles * ROW_TILE - HALO), HALO)
    return jnp.where(i < n_tiles - 1, ref[pl.ds(n0, HALO), cols], 0.0)


def _back(ext, s):
    return pltpu.roll(ext, s, axis=0)[HALO:HALO + ROW_TILE]


def _fwd(ext, s):
    n = ext.shape[0]
    return pltpu.roll(ext, n - s, axis=0)[:ROW_TILE]


def _store_rows(ref, rows):
    ref[...] = jnp.zeros_like(ref)
    for k, v in enumerate(rows):
        ref[k:k + 1, :] = v


def _strip_call(body, L, n_strips, ins, outs, name):
    def spec(rows, width, off):
        if off is None:
            return pl.BlockSpec((rows, width), lambda j: (0, 0))
        return pl.BlockSpec((rows, width), lambda j: (0, j + off))

    return pl.pallas_call(
        body, grid=(n_strips,),
        in_specs=[spec(a.shape[0], w, off) for a, w, off in ins],
        out_specs=[spec(s.shape[0], w, off) for s, w, off in outs],
        out_shape=[s for s, _, _ in outs],
        name=name, compiler_params=_cp("parallel"))(*[a for a, _, _ in ins])


def _ffn_mid_fwd(up, cw, cb, name):
    L = up.shape[0]
    C = up.shape[1] // 2
    n_tiles = L // ROW_TILE
    lo, hi = slice(0, LANE), slice(LANE, 2 * LANE)

    def body(up_ref, cw_ref, cb_ref, a_ref, gate_ref):
        w0, w1, w2 = cw_ref[0:1, :], cw_ref[1:2, :], cw_ref[2:3, :]
        b = cb_ref[...]

        def step(i, c):
            r0 = pl.multiple_of(i * ROW_TILE, ROW_TILE)
            rows = pl.ds(r0, ROW_TILE)
            gp = up_ref[rows, lo]
            ext = jnp.concatenate([_prev_rows(up_ref, r0, i, lo), gp], axis=0)
            gate = gp * w2 + _back(ext, 1) * w1 + _back(ext, 2) * w0 + b
            gate_ref[rows, :] = gate
            a_ref[rows, :] = (gate * _sigmoid(gate) * up_ref[rows, hi]).astype(BF16)
            return c

        lax.fori_loop(0, n_tiles, step, 0)

    return _strip_call(body, L, C // LANE,
                       [(up, 2 * LANE, 0), (cw, LANE, 0), (cb, LANE, 0)],
                       [(SDS((L, C), BF16), LANE, 0), (SDS((L, C), F32), LANE, 0)], name)


def _ffn_mid_bwd(da, up, gate, cw, name):
    L, C = da.shape
    n_tiles = L // ROW_TILE
    lo, hi, al = slice(0, LANE), slice(LANE, 2 * LANE), slice(None)

    def body(da_ref, up_ref, gate_ref, cw_ref, dup_ref, st_ref):
        w0, w1, w2 = cw_ref[0:1, :], cw_ref[1:2, :], cw_ref[2:3, :]

        def dgate_of(da_v, val_v, gate_v):
            return da_v * val_v * _dsilu(gate_v, _sigmoid(gate_v))

        def step(i, c):
            r0 = pl.multiple_of(i * ROW_TILE, ROW_TILE)
            rows = pl.ds(r0, ROW_TILE)
            da_v, gate_v, val = da_ref[rows, :], gate_ref[rows, :], up_ref[rows, hi]
            gp = up_ref[rows, lo]
            dgate = dgate_of(da_v, val, gate_v)
            dgate_n = dgate_of(_next_rows(da_ref, r0, i, n_tiles, al), _next_rows(up_ref, r0, i, n_tiles, hi),
                               _next_rows(gate_ref, r0, i, n_tiles, al))
            ext = jnp.concatenate([dgate, dgate_n], axis=0)
            dgp = dgate * w2 + _fwd(ext, 1) * w1 + _fwd(ext, 2) * w0
            dup_ref[rows, lo] = dgp.astype(BF16)
            dup_ref[rows, hi] = (da_v * gate_v * _sigmoid(gate_v)).astype(BF16)
            pext = jnp.concatenate([_prev_rows(up_ref, r0, i, lo), gp], axis=0)
            s = lambda v: jnp.sum(v, axis=0, keepdims=True)
            return (c[0] + s(dgate * _back(pext, 2)), c[1] + s(dgate * _back(pext, 1)), c[2] + s(dgate * gp),
                    c[3] + s(dgate))

        z = jnp.zeros((1, LANE), F32)
        _store_rows(st_ref, lax.fori_loop(0, n_tiles, step, (z, z, z, z)))

    return _strip_call(body, L, C // LANE,
                       [(da, LANE, 0), (up, 2 * LANE, 0), (gate, LANE, 0), (cw, LANE, 0)],
                       [(SDS((L, 2 * C), BF16), 2 * LANE, 0), (SDS((8, C), F32), LANE, 0)], name)


def _sc_mid_fwd(bcv, cw, name):
    L = bcv.shape[0]
    C = bcv.shape[1] // 3
    n_tiles = L // ROW_TILE
    s0, s1, s2 = slice(0, LANE), slice(LANE, 2 * LANE), slice(2 * LANE, 3 * LANE)

    def body(x_ref, cw_ref, q_ref):
        w0, w1, w2 = cw_ref[0:1, :], cw_ref[1:2, :], cw_ref[2:3, :]

        def step(i, c):
            r0 = pl.multiple_of(i * ROW_TILE, ROW_TILE)
            rows = pl.ds(r0, ROW_TILE)
            p = x_ref[rows, s1] * x_ref[rows, s2]
            ext = jnp.concatenate([_prev_rows(x_ref, r0, i, s1) * _prev_rows(x_ref, r0, i, s2), p], axis=0)
            u = p * w2 + _back(ext, 1) * w1 + _back(ext, 2) * w0
            q_ref[rows, :] = (x_ref[rows, s0] * u).astype(BF16)
            return c

        lax.fori_loop(0, n_tiles, step, 0)

    return _strip_call(body, L, C // LANE, [(bcv, 3 * LANE, 0), (cw, LANE, 0)],
                       [(SDS((L, C), BF16), LANE, 0)], name)[0]


def _sc_mid_bwd(dq, bcv, cw, name):
    L, C = dq.shape
    n_tiles = L // ROW_TILE
    s0, s1, s2, al = slice(0, LANE), slice(LANE, 2 * LANE), slice(2 * LANE, 3 * LANE), slice(None)

    def body(dq_ref, x_ref, cw_ref, dx_ref, st_ref):
        w0, w1, w2 = cw_ref[0:1, :], cw_ref[1:2, :], cw_ref[2:3, :]

        def step(i, c):
            r0 = pl.multiple_of(i * ROW_TILE, ROW_TILE)
            rows = pl.ds(r0, ROW_TILE)
            gb, gc, v, dq_v = x_ref[rows, s0], x_ref[rows, s1], x_ref[rows, s2], dq_ref[rows, :]
            p = gc * v
            pext = jnp.concatenate([_prev_rows(x_ref, r0, i, s1) * _prev_rows(x_ref, r0, i, s2), p], axis=0)
            p1, p2 = _back(pext, 1), _back(pext, 2)
            u = p * w2 + p1 * w1 + p2 * w0
            du = dq_v * gb
            du_n = _next_rows(dq_ref, r0, i, n_tiles, al) * _next_rows(x_ref, r0, i, n_tiles, s0)
            ext = jnp.concatenate([du, du_n], axis=0)
            dp = du * w2 + _fwd(ext, 1) * w1 + _fwd(ext, 2) * w0
            dx_ref[rows, s0] = (dq_v * u).astype(BF16)
            dx_ref[rows, s1] = (dp * v).astype(BF16)
            dx_ref[rows, s2] = (dp * gc).astype(BF16)
            s = lambda t: jnp.sum(t, axis=0, keepdims=True)
            return (c[0] + s(du * p2), c[1] + s(du * p1), c[2] + s(du * p))

        z = jnp.zeros((1, LANE), F32)
        _store_rows(st_ref, lax.fori_loop(0, n_tiles, step, (z, z, z)))

    return _strip_call(body, L, C // LANE, [(dq, LANE, 0), (bcv, 3 * LANE, 0), (cw, LANE, 0)],
                       [(SDS((L, 3 * C), BF16), 3 * LANE, 0), (SDS((8, C), F32), LANE, 0)], name)


def _ssd_conv_fwd(zx, cw, cb, col0, C, name):
    L = zx.shape[0]
    n_tiles = L // ROW_TILE
    al = slice(None)

    def body(x_ref, cw_ref, cb_ref, o_ref):
        w0, w1, w2, w3 = cw_ref[0:1, :], cw_ref[1:2, :], cw_ref[2:3, :], cw_ref[3:4, :]
        b = cb_ref[...]

        def step(i, c):
            r0 = pl.multiple_of(i * ROW_TILE, ROW_TILE)
            rows = pl.ds(r0, ROW_TILE)
            xv = x_ref[rows, :]
            ext = jnp.concatenate([_prev_rows(x_ref, r0, i, al), xv], axis=0)
            cv = xv * w3 + _back(ext, 1) * w2 + _back(ext, 2) * w1 + _back(ext, 3) * w0 + b
            o_ref[rows, :] = cv * _sigmoid(cv)
            return c

        lax.fori_loop(0, n_tiles, step, 0)

    return _strip_call(body, L, C // LANE, [(zx, LANE, col0 // LANE), (cw, LANE, 0), (cb, LANE, 0)],
                       [(SDS((L, C), F32), LANE, 0)], name)[0]


def _ssd_conv_bwd(dxbc, zx, cw, cb, col0, name):
    L, C = dxbc.shape
    n_tiles = L // ROW_TILE
    al = slice(None)

    def body(d_ref, x_ref, cw_ref, cb_ref, o_ref, st_ref):
        w0, w1, w2, w3 = cw_ref[0:1, :], cw_ref[1:2, :], cw_ref[2:3, :], cw_ref[3:4, :]
        b = cb_ref[...]

        def step(i, c):
            r0 = pl.multiple_of(i * ROW_TILE, ROW_TILE)
            rows = pl.ds(r0, ROW_TILE)
            xv = x_ref[rows, :]
            xe = jnp.concatenate([_prev_rows(x_ref, r0, i, al), xv, _next_rows(x_ref, r0, i, n_tiles, al)], axis=0)
            x1, x2, x3 = pltpu.roll(xe, 1, axis=0), pltpu.roll(xe, 2, axis=0), pltpu.roll(xe, 3, axis=0)
            cv = (xe * w3 + x1 * w2 + x2 * w1 + x3 * w0 + b)[HALO:]
            de = jnp.concatenate([d_ref[rows, :], _next_rows(d_ref, r0, i, n_tiles, al)], axis=0)
            dc_ext = de * _dsilu(cv, _sigmoid(cv))
            dc = dc_ext[:ROW_TILE]
            o_ref[rows, :] = (dc * w3 + _fwd(dc_ext, 1) * w2 + _fwd(dc_ext, 2) * w1 + _fwd(dc_ext, 3) * w0).astype(BF16)
            s = lambda t: jnp.sum(t, axis=0, keepdims=True)
            t = slice(HALO, HALO + ROW_TILE)
            return (c[0] + s(dc * x3[t]), c[1] + s(dc * x2[t]), c[2] + s(dc * x1[t]), c[3] + s(dc * xv), c[4] + s(dc))

        z = jnp.zeros((1, LANE), F32)
        _store_rows(st_ref, lax.fori_loop(0, n_tiles, step, (z, z, z, z, z)))

    return _strip_call(body, L, C // LANE,
                       [(dxbc, LANE, 0), (zx, LANE, col0 // LANE), (cw, LANE, 0), (cb, LANE, 0)],
                       [(SDS((L, C), BF16), LANE, 0), (SDS((8, C), F32), LANE, 0)], name)


def _scan_constants(n_heads):
    hw = n_heads * HEAD_DIM
    col = np.arange(hw)
    ind = (col[None, :] // HEAD_DIM == np.arange(LANE)[:, None]).astype(np.float32)
    gcol = np.arange(GROUP_W)
    itile = (gcol[None, :] % CHUNK == np.arange(CHUNK)[:, None]).astype(np.float32)
    trit = (gcol[None, :] % CHUNK <= np.arange(CHUNK)[:, None]).astype(np.float32)
    tril = np.tril(np.ones((CHUNK, CHUNK), np.float32))
    bmask = (gcol[:, None] // HEAD_DIM == gcol[None, :] // HEAD_DIM).astype(np.float32)
    return (jnp.asarray(ind), jnp.asarray(ind.T.copy()), jnp.asarray(itile), jnp.asarray(trit), jnp.asarray(tril),
            jnp.asarray(bmask))


def _softplus(x):
    return jnp.maximum(x, 0.0) + jnp.log(1.0 + jnp.exp(-jnp.abs(x)))


def _dot_hi(a, b):
    return jnp.dot(a, b, preferred_element_type=F32, precision=HI)


def _group_terms(g, dt, cs, xbc_ref, ind_ref, itile, trit, bmask, d_inner):
    gl = slice(g * GROUP_W, (g + 1) * GROUP_W)
    indg = ind_ref[:, gl]
    csl = _dot_hi(cs, indg)
    dtx = _dot_hi(dt, indg)
    rr = _dot_hi(jnp.ones((CHUNK, CHUNK), F32), csl * itile)
    lm = jnp.exp(jnp.where(trit > 0.0, csl - rr, -jnp.inf))
    xs = xbc_ref[:, gl]
    b = xbc_ref[:, d_inner + g * D_STATE: d_inner + (g + 1) * D_STATE]
    c = xbc_ref[:, d_inner + (N_GROUPS + g) * D_STATE: d_inner + (N_GROUPS + g + 1) * D_STATE]
    u = xs * dtx
    bb, cb = b.astype(BF16), c.astype(BF16)
    btile = jnp.concatenate([bb] * HEADS_PER_GROUP, axis=0)
    cbt = lax.dot_general(cb, btile, NT, preferred_element_type=F32)
    m = cbt * lm
    ub = u.astype(BF16)
    bdu = jnp.where(bmask > 0.0, jnp.concatenate([ub] * HEADS_PER_GROUP, axis=0), jnp.zeros((), BF16))
    c_last = csl[CHUNK - 1:CHUNK, :]
    return dict(gl=gl, indg=indg, csl=csl, dtx=dtx, lm=lm, xs=xs, bb=bb, cb=cb, u=u, btile=btile, m=m, bdu=bdu,
                e=jnp.exp(csl), dec=jnp.exp(c_last - csl), e_last=jnp.exp(c_last))


def _ssd_scan_fwd(zx, xbc, par, dexp, nw, consts, name):
    L = xbc.shape[0]
    d_inner = dexp.shape[1]
    n_chunks = L // CHUNK
    dt_blk = zx.shape[1] // LANE - 1
    ind, ind_t, itile_c, trit_c, tril_c, bmask_c = consts

    def body(xbc_ref, z_ref, dtr_ref, par_ref, dexp_ref, nw_ref, ind_ref, itile_ref, trit_ref, tril_ref, bmask_ref,
             yn_ref, yf_ref, st_out_ref, st_ref):
        @pl.when(pl.program_id(0) == 0)
        def _():
            st_ref[...] = jnp.zeros_like(st_ref)

        dt = _softplus(dtr_ref[...] + par_ref[0:1, :])
        a_head = -jnp.exp(par_ref[1:2, :])
        cs = _dot_hi(tril_ref[...], dt * a_head)
        itile, trit, bmask = itile_ref[...], trit_ref[...], bmask_ref[...]
        for g in range(N_GROUPS):
            t = _group_terms(g, dt, cs, xbc_ref, ind_ref, itile, trit, bmask, d_inner)
            p = st_ref[g]
            st_out_ref[0, g] = p
            y = jnp.dot(t["m"].astype(BF16), t["bdu"], preferred_element_type=F32)
            y = y + jnp.dot(t["cb"], p.astype(BF16), preferred_element_type=F32) * t["e"]
            st_new = lax.dot_general(t["bb"], (t["u"] * t["dec"]).astype(BF16), TN, preferred_element_type=F32)
            st_ref[g] = p * t["e_last"] + st_new
            yf_ref[:, t["gl"]] = y + t["xs"] * dexp_ref[:, t["gl"]]
        z = z_ref[...]
        y2 = yf_ref[...] * (z * _sigmoid(z))
        yn_ref[...] = (y2 * _rsq(y2) * nw_ref[...]).astype(BF16)

    row = lambda c: (c, 0)
    fix = lambda c: (0, 0)
    cspec = lambda a: pl.BlockSpec(a.shape, fix)
    return pl.pallas_call(
        body, grid=(n_chunks,),
        in_specs=[pl.BlockSpec((CHUNK, xbc.shape[1]), row), pl.BlockSpec((CHUNK, d_inner), row),
                  pl.BlockSpec((CHUNK, LANE), lambda c: (c, dt_blk)), cspec(par), cspec(dexp), cspec(nw),
                  cspec(ind), cspec(itile_c), cspec(trit_c), cspec(tril_c), cspec(bmask_c)],
        out_specs=[pl.BlockSpec((CHUNK, d_inner), row), pl.BlockSpec((CHUNK, d_inner), row),
                   pl.BlockSpec((1, N_GROUPS, D_STATE, GROUP_W), lambda c: (c, 0, 0, 0))],
        out_shape=[SDS((L, d_inner), BF16), SDS((L, d_inner), F32),
                   SDS((n_chunks, N_GROUPS, D_STATE, GROUP_W), F32)],
        scratch_shapes=[pltpu.VMEM((N_GROUPS, D_STATE, GROUP_W), F32)],
        name=name, compiler_params=_cp("arbitrary"))(xbc, zx, zx, par, dexp, nw, ind, itile_c, trit_c, tril_c, bmask_c)


def _ssd_scan_bwd(dyn, yf, zx, xbc, states, par, dexp, nw, consts, name):
    L = xbc.shape[0]
    d_inner = dexp.shape[1]
    n_chunks = L // CHUNK
    dt_blk = zx.shape[1] // LANE - 1
    ind, ind_t, itile_c, trit_c, tril_c, bmask_c = consts
    hslices = [slice(r * HEAD_DIM, (r + 1) * HEAD_DIM) for r in range(HEADS_PER_GROUP)]

    def body(dyn_ref, yf_ref, z_ref, dtr_ref, xbc_ref, st_in_ref, par_ref, dexp_ref, nw_ref, ind_ref, indt_ref,
             itile_ref, trit_ref, tril_ref, bmask_ref,
             dz_ref, dxbc_ref, ddt_ref, dnw_ref, dpar_ref, dq_ref, dyf_ref):
        @pl.when(pl.program_id(0) == 0)
        def _():
            dq_ref[...] = jnp.zeros_like(dq_ref)
            dnw_ref[...] = jnp.zeros_like(dnw_ref)
            dpar_ref[...] = jnp.zeros_like(dpar_ref)

        z, yfv, dynv = z_ref[...], yf_ref[...], dyn_ref[...]
        sz = _sigmoid(z)
        y2 = yfv * (z * sz)
        r = _rsq(y2)
        y2h = y2 * r
        dnw_ref[...] += jnp.sum(dynv * y2h, axis=0, keepdims=True)
        dyg = dynv * nw_ref[...]
        dy2 = r * (dyg - y2h * jnp.mean(dyg * y2h, axis=-1, keepdims=True))
        dz_ref[...] = (dy2 * yfv * _dsilu(z, sz)).astype(BF16)
        dyf_ref[...] = dy2 * (z * sz)

        pre = dtr_ref[...] + par_ref[0:1, :]
        dt = _softplus(pre)
        a_head = -jnp.exp(par_ref[1:2, :])
        cs = _dot_hi(tril_ref[...], dt * a_head)
        itile, trit, bmask = itile_ref[...], trit_ref[...], bmask_ref[...]
        dcs = jnp.zeros((CHUNK, LANE), F32)
        dcs_last = jnp.zeros((1, LANE), F32)
        ddt_u = jnp.zeros((CHUNK, LANE), F32)
        d_skip = jnp.zeros((1, LANE), F32)
        rsum = lambda v: jnp.sum(v, axis=0, keepdims=True)
        row8 = lax.broadcasted_iota(jnp.int32, (8, GROUP_W), 0)
        for g in range(N_GROUPS):
            t = _group_terms(g, dt, cs, xbc_ref, ind_ref, itile, trit, bmask, d_inner)
            gl, m, lm, u, bb, cb, e, dec, xs = (t[k] for k in ("gl", "m", "lm", "u", "bb", "cb", "e", "dec", "xs"))
            indt = indt_ref[gl, :]
            dy = dyf_ref[:, gl]
            dyb = dy.astype(BF16)
            p = st_in_ref[0, g]
            pb = p.astype(BF16)
            q = dq_ref[g]
            qb = q.astype(BF16)
            big = lax.dot_general(m.astype(BF16), dyb, TN, preferred_element_type=F32)
            du = jnp.zeros((CHUNK, GROUP_W), F32)
            for rh in range(HEADS_PER_GROUP):
                du = du + big[hslices[rh], :] * bmask[rh * HEAD_DIM:rh * HEAD_DIM + 1, :]
            dm = lax.dot_general(dyb, t["bdu"], NT, preferred_element_type=F32)
            w = dm * m
            dgt = (dm * lm).astype(BF16)
            dc = jnp.dot(dgt, t["btile"], preferred_element_type=F32)
            db_big = lax.dot_general(dgt, cb, TN, preferred_element_type=F32)
            db = db_big[hslices[0], :] + db_big[hslices[1], :] + db_big[hslices[2], :] + db_big[hslices[3], :]
            cp = jnp.dot(cb, pb, preferred_element_type=F32)
            dye = dy * e
            dyeb = dye.astype(BF16)
            dc = dc + lax.dot_general(dyeb, pb, NT, preferred_element_type=F32)
            dp = lax.dot_general(cb, dyeb, TN, preferred_element_type=F32)
            x2 = dye * cp
            bq = jnp.dot(bb, qb, preferred_element_type=F32)
            ud = u * dec
            du = du + bq * dec
            db = db + lax.dot_general(ud.astype(BF16), qb, NT, preferred_element_type=F32)
            x1 = bq * ud
            dq_ref[g] = dp + t["e_last"] * q
            x3 = rsum(q * p) * t["e_last"]
            red = _dot_hi(jnp.concatenate([w + x2 - x1, du * xs, itile * rsum(w)], axis=0), indt)
            dcs = dcs + red[0:CHUNK] - red[2 * CHUNK:3 * CHUNK]
            ddt_u = ddt_u + red[CHUNK:2 * CHUNK]
            tail = _dot_hi(jnp.where(row8 == 0, rsum(x1) + x3, jnp.where(row8 == 1, rsum(dy * xs), 0.0)), indt)
            dcs_last = dcs_last + tail[0:1]
            d_skip = d_skip + tail[1:2]
            dxbc_ref[:, gl] = du * t["dtx"] + dy * dexp_ref[:, gl]
            dxbc_ref[:, d_inner + g * D_STATE: d_inner + (g + 1) * D_STATE] = db
            dxbc_ref[:, d_inner + (N_GROUPS + g) * D_STATE: d_inner + (N_GROUPS + g + 1) * D_STATE] = dc
        last = lax.broadcasted_iota(jnp.int32, (CHUNK, LANE), 0) == CHUNK - 1
        dcs = dcs + jnp.where(last, dcs_last, 0.0)
        da = lax.dot_general(tril_ref[...], dcs, TN, preferred_element_type=F32, precision=HI)
        ddt = da * a_head + ddt_u
        heads = lax.broadcasted_iota(jnp.int32, (CHUNK, LANE), 1) < d_inner // HEAD_DIM
        ddt_raw = jnp.where(heads, ddt * _sigmoid(pre), 0.0)
        ddt_ref[...] = ddt_raw.astype(BF16)
        dpar_ref[0:1, :] += rsum(ddt_raw)
        dpar_ref[1:2, :] += rsum(da * dt) * a_head
        dpar_ref[2:3, :] += d_skip

    rev = lambda c: (n_chunks - 1 - c, 0)
    fix = lambda c: (0, 0)
    cspec = lambda a: pl.BlockSpec(a.shape, fix)
    nx = xbc.shape[1]
    return pl.pallas_call(
        body, grid=(n_chunks,),
        in_specs=[pl.BlockSpec((CHUNK, d_inner), rev), pl.BlockSpec((CHUNK, d_inner), rev),
                  pl.BlockSpec((CHUNK, d_inner), rev), pl.BlockSpec((CHUNK, LANE), lambda c: (n_chunks - 1 - c, dt_blk)),
                  pl.BlockSpec((CHUNK, nx), rev),
                  pl.BlockSpec((1, N_GROUPS, D_STATE, GROUP_W), lambda c: (n_chunks - 1 - c, 0, 0, 0)),
                  cspec(par), cspec(dexp), cspec(nw), cspec(ind), cspec(ind_t), cspec(itile_c), cspec(trit_c),
                  cspec(tril_c), cspec(bmask_c)],
        out_specs=[pl.BlockSpec((CHUNK, d_inner), rev), pl.BlockSpec((CHUNK, nx), rev),
                   pl.BlockSpec((CHUNK, LANE), rev), pl.BlockSpec((1, d_inner), fix), pl.BlockSpec((8, LANE), fix)],
        out_shape=[SDS((L, d_inner), BF16), SDS((L, nx), F32), SDS((L, LANE), BF16), SDS((1, d_inner), F32),
                   SDS((8, LANE), F32)],
        scratch_shapes=[pltpu.VMEM((N_GROUPS, D_STATE, GROUP_W), F32), pltpu.VMEM((CHUNK, d_inner), F32)],
        name=name, compiler_params=_cp("arbitrary"))(
            dyn, yf, zx, zx, xbc, states, par, dexp, nw, ind, ind_t, itile_c, trit_c, tril_c, bmask_c)


def _adamw(w, m, v, g, name):
    R, C = w.shape
    tr = R
    for cand in (256, 128, 64, 32, 16, 8):
        if R % cand == 0:
            tr = cand
            break

    def body(w_ref, m_ref, v_ref, g_ref, d_ref, mo_ref, vo_ref):
        gv = g_ref[...]
        mn = ADAM_B1 * m_ref[...] + (1.0 - ADAM_B1) * gv
        vn = ADAM_B2 * v_ref[...] + (1.0 - ADAM_B2) * (gv * gv)
        m_hat = mn / (1.0 - ADAM_B1 ** ADAM_STEP)
        v_hat = vn / (1.0 - ADAM_B2 ** ADAM_STEP)
        d_ref[...] = -ADAM_LR * (m_hat / (jnp.sqrt(v_hat) + ADAM_EPS) + ADAM_WD * w_ref[...])
        mo_ref[...] = mn
        vo_ref[...] = vn

    blk = pl.BlockSpec((tr, C), lambda i: (i, 0))
    return pl.pallas_call(
        body, grid=(R // tr,), in_specs=[blk] * 4, out_specs=[blk] * 3, out_shape=[SDS((R, C), F32)] * 3,
        name=name, compiler_params=_cp("parallel"))(w, m, v, g)


def _sum_slots(parts, name):
    n, R, C = parts.shape
    tr = 128 if R % 128 == 0 else R

    def body(p_ref, o_ref):
        acc = p_ref[0]
        for k in range(1, n):
            acc = acc + p_ref[k]
        o_ref[...] = acc

    return pl.pallas_call(
        body, grid=(R // tr,), in_specs=[pl.BlockSpec((n, tr, C), lambda i: (0, i, 0))],
        out_specs=pl.BlockSpec((tr, C), lambda i: (i, 0)), out_shape=SDS((R, C), F32),
        name=name, compiler_params=_cp("parallel"))(parts)


def _add_halves(a, b, name):
    n, R, C = a.shape
    tr = 16
    for cand in (512, 256, 128, 64, 32, 16):
        if R % cand == 0:
            tr = cand
            break

    def body(a_ref, b_ref, o_ref):
        o_ref[...] = (a_ref[...].astype(F32) + b_ref[...].astype(F32)).astype(BF16)

    blk = pl.BlockSpec((1, tr, C), lambda s, i: (s, i, 0))
    return pl.pallas_call(
        body, grid=(n, R // tr), in_specs=[blk, blk], out_specs=blk, out_shape=SDS((n, R, C), BF16),
        name=name, compiler_params=_cp("parallel", "parallel"))(a, b)


def _sum_shard(own_a, own_b, r0, r1, r2, name):
    R, C = own_a.shape
    tr = 16
    for cand in (512, 256, 128, 64, 32, 16):
        if R % cand == 0:
            tr = cand
            break

    def body(a_ref, b_ref, r0_ref, r1_ref, r2_ref, o_ref):
        f = lambda ref: ref[...].astype(F32)
        o_ref[...] = (((f(a_ref) + f(b_ref)) + f(r0_ref)) + f(r1_ref)) + f(r2_ref)

    blk = pl.BlockSpec((tr, C), lambda i: (i, 0))
    return pl.pallas_call(
        body, grid=(R // tr,), in_specs=[blk] * 5, out_specs=blk, out_shape=SDS((R, C), F32),
        name=name, compiler_params=_cp("parallel"))(own_a, own_b, r0, r1, r2)


def _me():
    return lax.axis_index("x"), lax.axis_index("y"), lax.axis_index("c")


def _chip_peers(x, y):
    return [(1 - x, y), (x, 1 - y), (1 - x, 1 - y)]


def _rcopy(src, dst, send_sems, recv_sems, k, to):
    return pltpu.make_async_remote_copy(src_ref=src, dst_ref=dst, send_sem=send_sems.at[k], recv_sem=recv_sems.at[k],
                                        device_id=to, device_id_type=MESH)


def _allgather_weights(mine, small):
    _, R, W = mine.shape

    def body(mine_ref, small_ref, out_ref, sout_ref, send_sems, recv_sems, local_sems):
        x, y, c = _me()
        q = 2 * x + y
        sib = (x, y, 1 - c)
        chips = _chip_peers(x, y)
        own = pltpu.make_async_copy(mine_ref, out_ref.at[q], local_sems.at[0])
        own_s = pltpu.make_async_copy(small_ref, sout_ref.at[q], local_sems.at[1])
        own.start()
        own_s.start()
        first = [_rcopy(mine_ref.at[c], out_ref.at[q, c], send_sems, recv_sems, j, (*chip, c))
                 for j, chip in enumerate(chips)]
        first += [_rcopy(small_ref, sout_ref.at[q], send_sems, recv_sems, 3 + j, (*chip, c))
                  for j, chip in enumerate(chips)]
        for cp in first:
            cp.start()
        passed = []
        for j, (px, py) in enumerate(chips):
            blk = out_ref.at[2 * px + py, c]
            _rcopy(blk, blk, send_sems, recv_sems, j, sib).wait_recv()
            cp = _rcopy(blk, blk, send_sems, recv_sems, 6 + j, sib)
            cp.start()
            passed.append(cp)
        for j, (px, py) in enumerate(chips):
            blk = out_ref.at[2 * px + py, 1 - c]
            _rcopy(blk, blk, send_sems, recv_sems, 6 + j, sib).wait_recv()
            sblk = sout_ref.at[2 * px + py]
            _rcopy(sblk, sblk, send_sems, recv_sems, 3 + j, sib).wait_recv()
        for cp in first + passed:
            cp.wait_send()
        own.wait()
        own_s.wait()

    return pl.pallas_call(
        body, in_specs=[ANY, ANY], out_specs=[ANY, ANY],
        out_shape=[SDS((4, 2, R, W), BF16), SDS((4,) + small.shape, F32)],
        scratch_shapes=[pltpu.SemaphoreType.DMA((9,)), pltpu.SemaphoreType.DMA((9,)), pltpu.SemaphoreType.DMA((2,))],
        name="allgather_weights")(mine, small)


def _send_sibling_halves(grads):
    _, _, R, W = grads.shape

    def body(g_ref, out_ref, send_sems, recv_sems):
        x, y, c = _me()
        sib = (x, y, 1 - c)
        cps = [_rcopy(g_ref.at[s, 1 - c], out_ref.at[s], send_sems, recv_sems, s, sib) for s in range(4)]
        for cp in cps:
            cp.start()
        for cp in cps:
            cp.wait()

    return pl.pallas_call(
        body, in_specs=[ANY], out_specs=ANY, out_shape=SDS((4, R, W), BF16),
        scratch_shapes=[pltpu.SemaphoreType.DMA((4,)), pltpu.SemaphoreType.DMA((4,))],
        name="grads_to_sibling")(grads)


def _send_chip_shards(psum):
    _, R, W = psum.shape

    def body(p_ref, out_ref, send_sems, recv_sems):
        x, y, c = _me()
        cps = [_rcopy(p_ref.at[2 * px + py], out_ref.at[j], send_sems, recv_sems, j, (px, py, c))
               for j, (px, py) in enumerate(_chip_peers(x, y))]
        for cp in cps:
            cp.start()
        for cp in cps:
            cp.wait()

    return pl.pallas_call(
        body, in_specs=[ANY], out_specs=ANY, out_shape=SDS((3, R, W), BF16),
        scratch_shapes=[pltpu.SemaphoreType.DMA((3,)), pltpu.SemaphoreType.DMA((3,))],
        name="grads_to_chips")(psum)


def _swap_halves(half):
    R, W = half.shape

    def body(h_ref, out_ref, send_sems, recv_sems, local_sem):
        x, y, c = _me()
        own = pltpu.make_async_copy(h_ref, out_ref.at[c], local_sem.at[0])
        own.start()
        cp = _rcopy(h_ref, out_ref.at[c], send_sems, recv_sems, 0, (x, y, 1 - c))
        cp.start()
        cp.wait_send()
        other = out_ref.at[1 - c]
        _rcopy(other, other, send_sems, recv_sems, 0, (x, y, 1 - c)).wait_recv()
        own.wait()

    return pl.pallas_call(
        body, in_specs=[ANY], out_specs=ANY, out_shape=SDS((2, R, W), F32),
        scratch_shapes=[pltpu.SemaphoreType.DMA((1,)), pltpu.SemaphoreType.DMA((1,)), pltpu.SemaphoreType.DMA((1,))],
        name="swap_halves")(half)


def _allgather_small(part):
    def body(p_ref, out_ref, send_sems, recv_sems, local_sem):
        x, y, c = _me()
        me = 4 * x + 2 * y + c
        own = pltpu.make_async_copy(p_ref, out_ref.at[me], local_sem.at[0])
        own.start()
        sends = []
        for k in range(1, 8):
            fx, fy, fc = (k >> 2) & 1, (k >> 1) & 1, k & 1
            to = (x ^ fx, y ^ fy, c ^ fc)
            sends.append(_rcopy(p_ref, out_ref.at[me], send_sems, recv_sems, k - 1, to))
        for cp in sends:
            cp.start()
        for k in range(1, 8):
            slot = out_ref.at[me ^ k]
            _rcopy(slot, slot, send_sems, recv_sems, k - 1, (x, y, c)).wait_recv()
        for cp in sends:
            cp.wait_send()
        own.wait()

    return pl.pallas_call(
        body, in_specs=[ANY], out_specs=ANY, out_shape=SDS((8,) + part.shape, F32),
        scratch_shapes=[pltpu.SemaphoreType.DMA((7,)), pltpu.SemaphoreType.DMA((7,)), pltpu.SemaphoreType.DMA((1,))],
        name="allgather_small")(part)


BIG = (("ssd_w_in", 2), ("ssd_w_out", 1), ("sc_w_in", 2), ("sc_w_out", 1), ("ffn_w_up", 2), ("ffn_w_down", 1))


def _to_shards(full, axis):
    n0, a, b = full.shape
    if axis == 2:
        t = full.reshape(n0, a, 4, b // 4).transpose(2, 0, 1, 3)
    else:
        t = full.reshape(n0, 4, a // 4, b).transpose(1, 0, 2, 3)
    return t.reshape(4, -1)


def _from_shards(flat4, shard_shape, axis):
    n0, a, b = shard_shape
    t = flat4.reshape(4, n0, a, b)
    if axis == 2:
        return t.transpose(1, 2, 0, 3).reshape(n0, a, 4 * b)
    return t.transpose(1, 0, 2, 3).reshape(n0, 4 * a, b)


def _interleave(w, parts):
    lead, n = w.shape[:-1], w.shape[-1]
    return w.reshape(*lead, parts, n // (parts * LANE), LANE).swapaxes(-2, -3).reshape(*lead, n)


def _deinterleave(w, parts):
    lead, n = w.shape[:-1], w.shape[-1]
    return w.reshape(*lead, n // (parts * LANE), parts, LANE).swapaxes(-2, -3).reshape(*lead, n)


def _pack_rows(vectors, width, row_multiple):
    flat = jnp.concatenate(vectors, axis=-1)
    n = flat.shape[-1]
    unit = width * row_multiple
    total = -(-n // unit) * unit
    flat = jnp.pad(flat, [(0, 0)] * (flat.ndim - 1) + [(0, total - n)])
    return flat.reshape(*flat.shape[:-1], total // width, width)


def _unpack(flat, shapes):
    out, off = [], 0
    for s in shapes:
        n = int(np.prod(s))
        out.append(flat[..., off:off + n].reshape(*flat.shape[:-1], *s))
        off += n
    return out


def _row(v):
    return v.reshape(1, -1)


def _pad_rows(w, rows=8):
    return jnp.pad(w, ((0, rows - w.shape[0]), (0, 0)))


def _ffn_fwd(x, g_pre, g_post, w_up, cw, cb, w_down, tag):
    up, hn = _norm_matmul(x, g_pre, w_up, 512, "ffn_up" + tag)
    a, gate = _ffn_mid_fwd(up, cw, cb, "ffn_mid_fwd" + tag)
    f, x_new = _matmul_norm_res(a, w_down, x, g_post, "ffn_down" + tag)
    return x_new, (x, hn, up, gate, a, f)


def _ffn_bwd(dx, saved, g_pre, g_post, w_up, cw, w_down, tag):
    x, hn, up, gate, a, f = saved
    df, dg_post = _postnorm_bwd(dx, f, g_post, "ffn_post_bwd" + tag)
    da = _matmul_nt(df, w_down, 256, "ffn_down_dx" + tag)
    dw_down = _matmul_tn(a, df, 256, 1024, "ffn_down_dw" + tag)
    dup, stats = _ffn_mid_bwd(da, up, gate, cw, "ffn_mid_bwd" + tag)
    dx_in, dg_pre = _matmul_nt_prenorm_bwd(dup, w_up, x, g_pre, dx, 512, "ffn_up_dx" + tag)
    dw_up = _matmul_tn(hn, dup, 512, 512, "ffn_up_dw" + tag)
    return dx_in, dict(g_pre=dg_pre, g_post=dg_post, w_up=dw_up, w_down=dw_down, cw=stats[0:3], cb=stats[3])


def _sc_fwd(x, g_pre, g_post, w_in, cw, w_out, tag):
    bcv, hn = _norm_matmul(x, g_pre, w_in, 768, "sc_in" + tag)
    q = _sc_mid_fwd(bcv, cw, "sc_mid_fwd" + tag)
    m, x_new = _matmul_norm_res(q, w_out, x, g_post, "sc_out" + tag)
    return x_new, (x, hn, bcv, q, m)


def _sc_bwd(dx, saved, g_pre, g_post, w_in, cw, w_out, tag):
    x, hn, bcv, q, m = saved
    dm, dg_post = _postnorm_bwd(dx, m, g_post, "sc_post_bwd" + tag)
    dq = _matmul_nt(dm, w_out, 512, "sc_out_dx" + tag)
    dw_out = _matmul_tn(q, dm, 512, 1024, "sc_out_dw" + tag)
    dbcv, stats = _sc_mid_bwd(dq, bcv, cw, "sc_mid_bwd" + tag)
    dx_in, dg_pre = _matmul_nt_prenorm_bwd(dbcv, w_in, x, g_pre, dx, 768, "sc_in_dx" + tag)
    dw_in = _matmul_tn(hn, dbcv, 512, 768, "sc_in_dw" + tag)
    return dx_in, dict(g_pre=dg_pre, g_post=dg_post, w_in=dw_in, w_out=dw_out, cw=stats[0:3])


def _ssd_fwd(x, g_pre, g_post, w_in, cw, cb, par, dexp, nw, w_out, consts, tag):
    d_inner = dexp.shape[1]
    zx, hn = _norm_matmul(x, g_pre, w_in, 896, "ssd_in" + tag)
    xbc = _ssd_conv_fwd(zx, cw, cb, d_inner, cw.shape[1], "ssd_conv_fwd" + tag)
    yn, yf, states = _ssd_scan_fwd(zx, xbc, par, dexp, nw, consts, "ssd_scan_fwd" + tag)
    m, x_new = _matmul_norm_res(yn, w_out, x, g_post, "ssd_out" + tag)
    return x_new, (x, hn, zx, xbc, yn, yf, states, m)


def _ssd_bwd(dx, saved, g_pre, g_post, w_in, cw, cb, par, dexp, nw, w_out, consts, tag):
    x, hn, zx, xbc, yn, yf, states, m = saved
    d_inner = dexp.shape[1]
    dm, dg_post = _postnorm_bwd(dx, m, g_post, "ssd_post_bwd" + tag)
    dyn = _matmul_nt(dm, w_out, 512, "ssd_out_dx" + tag)
    dw_out = _matmul_tn(yn, dm, 512, 1024, "ssd_out_dw" + tag)
    dz, dxbc, ddt, dnw, dpar = _ssd_scan_bwd(dyn, yf, zx, xbc, states, par, dexp, nw, consts, "ssd_scan_bwd" + tag)
    dxp, stats = _ssd_conv_bwd(dxbc, zx, cw, cb, d_inner, "ssd_conv_bwd" + tag)
    dzx = jnp.concatenate([dz, dxp, ddt], axis=1)
    dx_in, dg_pre = _matmul_nt_prenorm_bwd(dzx, w_in, x, g_pre, dx, 896, "ssd_in_dx" + tag)
    dw_in = _matmul_tn(hn, dzx, 512, 896, "ssd_in_dw" + tag)
    n_heads = d_inner // HEAD_DIM
    return dx_in, dict(g_pre=dg_pre, g_post=dg_post, w_in=dw_in, w_out=dw_out, cw=stats[0:4], cb=stats[4],
                       dt_bias=dpar[0, :n_heads], a_log=dpar[1, :n_heads], d=dpar[2, :n_heads], nw=dnw[0])


def kernel(x, mix_pre_g, mix_post_g, ffn_pre_g, ffn_post_g, ssd_w_in, ssd_conv_w, ssd_conv_b, ssd_dt_bias, ssd_A_log, ssd_D, ssd_norm_w, ssd_w_out, sc_w_in, sc_conv_w, sc_w_out, ffn_w_up, ffn_conv_w, ffn_conv_b, ffn_w_down, loss_target, m_mix_pre_g, m_mix_post_g, m_ffn_pre_g, m_ffn_post_g, m_ssd_w_in, m_ssd_conv_w, m_ssd_conv_b, m_ssd_dt_bias, m_ssd_A_log, m_ssd_D, m_ssd_norm_w, m_ssd_w_out, m_sc_w_in, m_sc_conv_w, m_sc_w_out, m_ffn_w_up, m_ffn_conv_w, m_ffn_conv_b, m_ffn_w_down, v_mix_pre_g, v_mix_post_g, v_ffn_pre_g, v_ffn_post_g, v_ssd_w_in, v_ssd_conv_w, v_ssd_conv_b, v_ssd_dt_bias, v_ssd_A_log, v_ssd_D, v_ssd_norm_w, v_ssd_w_out, v_sc_w_in, v_sc_conv_w, v_sc_w_out, v_ffn_w_up, v_ffn_conv_w, v_ffn_conv_b, v_ffn_w_down):
    names = ["mix_pre_g", "mix_post_g", "ffn_pre_g", "ffn_post_g", "ssd_w_in", "ssd_conv_w", "ssd_conv_b",
             "ssd_dt_bias", "ssd_A_log", "ssd_D", "ssd_norm_w", "ssd_w_out", "sc_w_in", "sc_conv_w", "sc_w_out",
             "ffn_w_up", "ffn_conv_w", "ffn_conv_b", "ffn_w_down"]
    env = locals()
    wts = {n: env[n] for n in names}
    mom = {n: env["m_" + n] for n in names}
    var = {n: env["v_" + n] for n in names}

    depth, d_model = mix_pre_g.shape
    n_ssd, n_heads = ssd_dt_bias.shape
    n_sc = sc_conv_w.shape[0]
    d_inner = n_heads * HEAD_DIM
    conv_dim = d_inner + 2 * N_GROUPS * D_STATE
    ssd_in_dim = d_inner + conv_dim + n_heads
    ssd_in_pad = d_inner + conv_dim + LANE
    q_chip = 2 * lax.axis_index("x") + lax.axis_index("y")
    core = lax.axis_index("c")

    big_shapes = [wts[n].shape for n, _ in BIG]
    mine = _pack_rows([wts[n].astype(BF16).reshape(-1) for n, _ in BIG], PACK_W, PACK_ROWS)
    half_rows = mine.shape[0] // 2
    mine = mine.reshape(2, half_rows, PACK_W)
    conv_names = ["ssd_conv_w", "sc_conv_w", "ffn_conv_w"]
    conv_shapes = [wts[n].shape for n in conv_names]
    small_mine = _pack_rows([wts[n].reshape(-1) for n in conv_names], LANE, 8)
    gathered, small_all = _allgather_weights(mine, small_mine)
    flat4 = gathered.reshape(4, -1)
    full = {n: _from_shards(f, s, ax) for (n, ax), f, s in zip(BIG, _unpack(flat4, big_shapes), big_shapes)}
    conv_full = {}
    for n, f, s in zip(conv_names, _unpack(small_all.reshape(4, -1), conv_shapes), conv_shapes):
        conv_full[n] = f.transpose(1, 2, 0, 3).reshape(s[0], s[1], 4 * s[2])

    w_ssd_in = jnp.pad(full["ssd_w_in"], ((0, 0), (0, 0), (0, ssd_in_pad - ssd_in_dim)))
    w_sc_in = _interleave(full["sc_w_in"], 3)
    w_ffn_up = _interleave(full["ffn_w_up"], 2)
    consts = _scan_constants(n_heads)

    def ssd_args(j):
        par = jnp.zeros((8, LANE), F32).at[0, :n_heads].set(ssd_dt_bias[j]).at[1, :n_heads].set(ssd_A_log[j])
        dexp = jnp.repeat(ssd_D[j], HEAD_DIM).reshape(1, d_inner)
        return (w_ssd_in[j], _pad_rows(conv_full["ssd_conv_w"][j]), _row(ssd_conv_b[j]), par, dexp,
                _row(ssd_norm_w[j]), full["ssd_w_out"][j], consts)

    def sc_args(j):
        return (w_sc_in[j], _pad_rows(conv_full["sc_conv_w"][j]), full["sc_w_out"][j])

    def ffn_args(i):
        return (w_ffn_up[i], _pad_rows(conv_full["ffn_conv_w"][i]), _row(ffn_conv_b[i]), full["ffn_w_down"][i])

    h = x[0]
    saved = []
    for i in range(depth):
        j = i // 2
        gp, gq = _row(mix_pre_g[i]), _row(mix_post_g[i])
        if i % 2 == 0:
            h, sv = _ssd_fwd(h, gp, gq, *ssd_args(j), tag="")
        else:
            h, sv = _sc_fwd(h, gp, gq, *sc_args(j), tag="")
        w_up, cw, cb, w_down = ffn_args(i)
        h, sv2 = _ffn_fwd(h, _row(ffn_pre_g[i]), _row(ffn_post_g[i]), w_up, cw, cb, w_down, tag="")
        saved.append((sv, sv2))
    dh, loss_part = _loss_head(h, loss_target[0], "loss_head")

    mix_grads, ffn_grads = [None] * depth, [None] * depth
    for i in reversed(range(depth)):
        j = i // 2
        sv, sv2 = saved[i]
        w_up, cw, cb, w_down = ffn_args(i)
        dh, ffn_grads[i] = _ffn_bwd(dh, sv2, _row(ffn_pre_g[i]), _row(ffn_post_g[i]), w_up, cw, w_down, tag="")
        gp, gq = _row(mix_pre_g[i]), _row(mix_post_g[i])
        if i % 2 == 0:
            w_in, scw, scb, par, dexp, nw, w_out, _ = ssd_args(j)
            dh, mix_grads[i] = _ssd_bwd(dh, sv, gp, gq, w_in, scw, scb, par, dexp, nw, w_out, consts, tag="")
        else:
            w_in, scw, w_out = sc_args(j)
            dh, mix_grads[i] = _sc_bwd(dh, sv, gp, gq, w_in, scw, w_out, tag="")
    grad_x = dh[None]

    ssd_l = [mix_grads[i] for i in range(0, depth, 2)]
    sc_l = [mix_grads[i] for i in range(1, depth, 2)]
    stack = lambda layers, k: jnp.stack([g[k] for g in layers])
    local = {
        "ssd_w_in": stack(ssd_l, "w_in")[:, :, :ssd_in_dim],
        "ssd_w_out": stack(ssd_l, "w_out"),
        "sc_w_in": _deinterleave(stack(sc_l, "w_in"), 3),
        "sc_w_out": stack(sc_l, "w_out"),
        "ffn_w_up": _deinterleave(stack(ffn_grads, "w_up"), 2),
        "ffn_w_down": stack(ffn_grads, "w_down"),
    }
    gpack = _pack_rows([_to_shards(local[n], ax).astype(BF16) for n, ax in BIG], PACK_W, PACK_ROWS)
    gpack = gpack.reshape(4, 2, half_rows, PACK_W)
    from_sib = _send_sibling_halves(gpack)
    own_half = lax.dynamic_index_in_dim(gpack, core, axis=1, keepdims=False)
    chip_sum = _add_halves(own_half, from_sib, "add_core_halves")
    from_chips = _send_chip_shards(chip_sum)
    pick = lambda a: lax.dynamic_index_in_dim(a, q_chip, axis=0, keepdims=False)
    half_sum = _sum_shard(pick(own_half), pick(from_sib), from_chips[0], from_chips[1], from_chips[2], "sum_shard")
    gshard = _swap_halves(half_sum).reshape(-1)
    big_grads = dict(zip([n for n, _ in BIG], _unpack(gshard, big_shapes)))

    small_names = ["mix_pre_g", "mix_post_g", "ffn_pre_g", "ffn_post_g", "ssd_conv_w", "ssd_conv_b", "ssd_dt_bias",
                   "ssd_A_log", "ssd_D", "ssd_norm_w", "sc_conv_w", "ffn_conv_w", "ffn_conv_b"]
    small_local = {
        "mix_pre_g": jnp.concatenate([g["g_pre"] for g in mix_grads]),
        "mix_post_g": jnp.concatenate([g["g_post"] for g in mix_grads]),
        "ffn_pre_g": jnp.concatenate([g["g_pre"] for g in ffn_grads]),
        "ffn_post_g": jnp.concatenate([g["g_post"] for g in ffn_grads]),
        "ssd_conv_w": stack(ssd_l, "cw"), "ssd_conv_b": stack(ssd_l, "cb"), "ssd_dt_bias": stack(ssd_l, "dt_bias"),
        "ssd_A_log": stack(ssd_l, "a_log"), "ssd_D": stack(ssd_l, "d"), "ssd_norm_w": stack(ssd_l, "nw"),
        "sc_conv_w": stack(sc_l, "cw"), "ffn_conv_w": stack(ffn_grads, "cw"), "ffn_conv_b": stack(ffn_grads, "cb"),
    }
    small_full_shapes = [small_local[n].shape for n in small_names]
    spack = _pack_rows([small_local[n].reshape(-1) for n in small_names] + [loss_part.reshape(-1)], LANE, 8)
    stotal = _sum_slots(_allgather_small(spack), "sum_small").reshape(-1)
    small_grads = dict(zip(small_names, _unpack(stotal, small_full_shapes)))
    loss = stotal[sum(int(np.prod(s)) for s in small_full_shapes)]
    for n in conv_names:
        width = wts[n].shape[-1]
        small_grads[n] = lax.dynamic_slice_in_dim(small_grads[n], q_chip * width, width, axis=2)

    grads, delta, new_m, new_v = {}, {}, {}, {}
    for n, _ in BIG:
        s = wts[n].shape
        two_d = lambda a: a.reshape(-1, s[-1])
        grads[n] = big_grads[n]
        d, mn, vn = _adamw(two_d(wts[n]), two_d(mom[n]), two_d(var[n]), two_d(grads[n]), "adamw_" + n)
        delta[n], new_m[n], new_v[n] = d.reshape(s), mn.reshape(s), vn.reshape(s)
    small_shapes = [wts[n].shape for n in small_names]
    pk = lambda d: _pack_rows([d[n].reshape(-1) for n in small_names], LANE, 8)
    for n in small_names:
        grads[n] = small_grads[n].reshape(wts[n].shape)
    d, mn, vn = _adamw(pk(wts), pk(mom), pk(var), pk(grads), "adamw_small")
    for out, packed in ((delta, d), (new_m, mn), (new_v, vn)):
        out.update(zip(small_names, _unpack(packed.reshape(-1), small_shapes)))

    return (loss, grad_x, *[grads[n] for n in names], *[delta[n] for n in names], *[new_m[n] for n in names],
            *[new_v[n] for n in names])
```

```python
import jax
import jax.numpy as jnp
import numpy as np
from jax import lax
from jax.experimental import pallas as pl
from jax.experimental.pallas import tpu as pltpu

F32 = jnp.float32
BF16 = jnp.bfloat16
SDS = jax.ShapeDtypeStruct
MESH = pl.DeviceIdType.MESH
ANY = pl.BlockSpec(memory_space=pl.ANY)

EPS = 1e-6
CHUNK = 64
HEAD_DIM = 64
N_GROUPS = 8
D_STATE = 128
HEADS_PER_GROUP = 4
GROUP_W = HEADS_PER_GROUP * HEAD_DIM
LANE = 128
ROW_TILE = 128
HALO = 8
VMEM_LIMIT = 56 * 1024 * 1024

ADAM_LR = 0.001
ADAM_B1 = 0.9
ADAM_B2 = 0.999
ADAM_EPS = 1e-08
ADAM_WD = 0.01
ADAM_STEP = 10

HI = lax.Precision.HIGHEST
NT = (((1,), (1,)), ((), ()))
TN = (((0,), (0,)), ((), ()))


def _cp(*sem):
    return pltpu.CompilerParams(dimension_semantics=sem or None, vmem_limit_bytes=VMEM_LIMIT)


def _sigmoid(x):
    return 1.0 / (1.0 + jnp.exp(-x))


def _dsilu(x, s):
    return s * (1.0 + x * (1.0 - s))


def _rsq(x):
    return lax.rsqrt(jnp.mean(x * x, axis=-1, keepdims=True) + EPS)


def _norm_matmul(x, g, w, tn, name):
    L, D = x.shape
    N = w.shape[1]
    tm = min(512, L)

    def body(x_ref, g_ref, w_ref, o_ref, hn_ref):
        @pl.when(pl.program_id(1) == 0)
        def _():
            xv = x_ref[...]
            hn_ref[...] = (xv * _rsq(xv) * g_ref[...]).astype(BF16)

        o_ref[...] = jnp.dot(hn_ref[...], w_ref[...], preferred_element_type=F32)

    return pl.pallas_call(
        body, grid=(L // tm, N // tn),
        in_specs=[pl.BlockSpec((tm, D), lambda i, j: (i, 0)), pl.BlockSpec((1, D), lambda i, j: (0, 0)),
                  pl.BlockSpec((D, tn), lambda i, j: (0, j))],
        out_specs=[pl.BlockSpec((tm, tn), lambda i, j: (i, j)), pl.BlockSpec((tm, D), lambda i, j: (i, 0))],
        out_shape=[SDS((L, N), F32), SDS((L, D), BF16)],
        name=name, compiler_params=_cp("parallel", "arbitrary"))(x, g, w)


def _matmul_norm_res(a, w, x, g, name):
    L, K = a.shape
    D = w.shape[1]
    tm = min(256, L)

    def body(a_ref, w_ref, x_ref, g_ref, m_ref, xo_ref):
        m = jnp.dot(a_ref[...], w_ref[...], preferred_element_type=F32)
        m_ref[...] = m
        xo_ref[...] = x_ref[...] + m * _rsq(m) * g_ref[...]

    row = lambda i: (i, 0)
    fix = lambda i: (0, 0)
    return pl.pallas_call(
        body, grid=(L // tm,),
        in_specs=[pl.BlockSpec((tm, K), row), pl.BlockSpec((K, D), fix), pl.BlockSpec((tm, D), row),
                  pl.BlockSpec((1, D), fix)],
        out_specs=[pl.BlockSpec((tm, D), row), pl.BlockSpec((tm, D), row)],
        out_shape=[SDS((L, D), F32), SDS((L, D), F32)],
        name=name, compiler_params=_cp("parallel"))(a, w, x, g)


def _postnorm_bwd(dx, m, g, name):
    L, D = dx.shape
    tm = min(512, L)

    def body(dx_ref, m_ref, g_ref, dm_ref, dg_ref):
        @pl.when(pl.program_id(0) == 0)
        def _():
            dg_ref[...] = jnp.zeros_like(dg_ref)

        m = m_ref[...]
        dxv = dx_ref[...]
        r = _rsq(m)
        mh = m * r
        dg_ref[...] += jnp.sum(dxv * mh, axis=0, keepdims=True)
        dyg = dxv * g_ref[...]
        dm_ref[...] = (r * (dyg - mh * jnp.mean(dyg * mh, axis=-1, keepdims=True))).astype(BF16)

    row = lambda i: (i, 0)
    fix = lambda i: (0, 0)
    return pl.pallas_call(
        body, grid=(L // tm,),
        in_specs=[pl.BlockSpec((tm, D), row), pl.BlockSpec((tm, D), row), pl.BlockSpec((1, D), fix)],
        out_specs=[pl.BlockSpec((tm, D), row), pl.BlockSpec((1, D), fix)],
        out_shape=[SDS((L, D), BF16), SDS((1, D), F32)],
        name=name, compiler_params=_cp("arbitrary"))(dx, m, g)


def _matmul_nt(a, w, tn, name):
    L, D = a.shape
    K = w.shape[0]
    tm = min(512, L)

    def body(a_ref, w_ref, o_ref):
        o_ref[...] = lax.dot_general(a_ref[...], w_ref[...], NT, preferred_element_type=F32)

    return pl.pallas_call(
        body, grid=(L // tm, K // tn),
        in_specs=[pl.BlockSpec((tm, D), lambda i, j: (i, 0)), pl.BlockSpec((tn, D), lambda i, j: (j, 0))],
        out_specs=pl.BlockSpec((tm, tn), lambda i, j: (i, j)),
        out_shape=SDS((L, K), F32),
        name=name, compiler_params=_cp("parallel", "parallel"))(a, w)


def _matmul_tn(a, b, ta, tn, name):
    L, Ka = a.shape
    N = b.shape[1]
    tl = min(1024, L)
    n_l = L // tl

    def body(a_ref, b_ref, o_ref, acc_ref):
        l = pl.program_id(2)

        @pl.when(l == 0)
        def _():
            acc_ref[...] = jnp.zeros_like(acc_ref)

        acc_ref[...] += lax.dot_general(a_ref[...], b_ref[...], TN, preferred_element_type=F32)

        @pl.when(l == n_l - 1)
        def _():
            o_ref[...] = acc_ref[...]

    return pl.pallas_call(
        body, grid=(Ka // ta, N // tn, n_l),
        in_specs=[pl.BlockSpec((tl, ta), lambda i, j, l: (l, i)), pl.BlockSpec((tl, tn), lambda i, j, l: (l, j))],
        out_specs=pl.BlockSpec((ta, tn), lambda i, j, l: (i, j)),
        out_shape=SDS((Ka, N), F32),
        scratch_shapes=[pltpu.VMEM((ta, tn), F32)],
        name=name, compiler_params=_cp("parallel", "parallel", "arbitrary"))(a, b)


def _matmul_nt_prenorm_bwd(dy, w, x, g, dres, tk, name):
    L, N = dy.shape
    D = w.shape[0]
    tm = min(512, L)
    n_k = N // tk

    def body(dy_ref, w_ref, x_ref, g_ref, dres_ref, dx_ref, dg_ref, acc_ref):
        i, k = pl.program_id(0), pl.program_id(1)

        @pl.when((i == 0) & (k == 0))
        def _():
            dg_ref[...] = jnp.zeros_like(dg_ref)

        @pl.when(k == 0)
        def _():
            acc_ref[...] = jnp.zeros_like(acc_ref)

        acc_ref[...] += lax.dot_general(dy_ref[...], w_ref[...], NT, preferred_element_type=F32)

        @pl.when(k == n_k - 1)
        def _():
            dh = acc_ref[...]
            xv = x_ref[...]
            r = _rsq(xv)
            xh = xv * r
            dg_ref[...] += jnp.sum(dh * xh, axis=0, keepdims=True)
            dyg = dh * g_ref[...]
            dx_ref[...] = dres_ref[...] + r * (dyg - xh * jnp.mean(dyg * xh, axis=-1, keepdims=True))

    row = lambda i, k: (i, 0)
    fix = lambda i, k: (0, 0)
    return pl.pallas_call(
        body, grid=(L // tm, n_k),
        in_specs=[pl.BlockSpec((tm, tk), lambda i, k: (i, k)), pl.BlockSpec((D, tk), lambda i, k: (0, k)),
                  pl.BlockSpec((tm, D), row), pl.BlockSpec((1, D), fix), pl.BlockSpec((tm, D), row)],
        out_specs=[pl.BlockSpec((tm, D), row), pl.BlockSpec((1, D), fix)],
        out_shape=[SDS((L, D), F32), SDS((1, D), F32)],
        scratch_shapes=[pltpu.VMEM((tm, D), F32)],
        name=name, compiler_params=_cp("arbitrary", "arbitrary"))(dy, w, x, g, dres)


def _loss_head(y, t, name):
    L, D = y.shape
    tm = min(512, L)

    def body(y_ref, t_ref, dy_ref, loss_ref):
        @pl.when(pl.program_id(0) == 0)
        def _():
            loss_ref[...] = jnp.zeros_like(loss_ref)

        e = y_ref[...] - t_ref[...]
        dy_ref[...] = e * (1.0 / D)
        s = jnp.sum(jnp.sum(e * e, axis=1, keepdims=True), axis=0, keepdims=True)
        loss_ref[...] += s * (0.5 / D)

    row = lambda i: (i, 0)
    return pl.pallas_call(
        body, grid=(L // tm,),
        in_specs=[pl.BlockSpec((tm, D), row), pl.BlockSpec((tm, D), row)],
        out_specs=[pl.BlockSpec((tm, D), row), pl.BlockSpec((1, 1), lambda i: (0, 0))],
        out_shape=[SDS((L, D), F32), SDS((1, 1), F32)],
        name=name, compiler_params=_cp("arbitrary"))(y, t)


def _prev_rows(ref, r0, i, cols):
    p0 = pl.multiple_of(jnp.maximum(r0 - HALO, 0), HALO)
    return jnp.where(i > 0, ref[pl.ds(p0, HALO), cols], 0.0)


def _next_rows(ref, r0, i, n_tiles, cols):
    n0 = pl.multiple_of(jnp.minimum(r0 + ROW_TILE, n_tiles * ROW_TILE - HALO), HALO)
    return jnp.where(i < n_tiles - 1, ref[pl.ds(n0, HALO), cols], 0.0)


def _back(ext, s):
    return pltpu.roll(ext, s, axis=0)[HALO:HALO + ROW_TILE]


def _fwd(ext, s):
    n = ext.shape[0]
    return pltpu.roll(ext, n - s, axis=0)[:ROW_TILE]


def _store_rows(ref, rows):
    ref[...] = jnp.zeros_like(ref)
    for k, v in enumerate(rows):
        ref[k:k + 1, :] = v


def _strip_call(body, L, n_strips, ins, outs, name):
    def spec(rows, width, off):
        if off is None:
            return pl.BlockSpec((rows, width), lambda j: (0, 0))
        return pl.BlockSpec((rows, width), lambda j: (0, j + off))

    return pl.pallas_call(
        body, grid=(n_strips,),
        in_specs=[spec(a.shape[0], w, off) for a, w, off in ins],
        out_specs=[spec(s.shape[0], w, off) for s, w, off in outs],
        out_shape=[s for s, _, _ in outs],
        name=name, compiler_params=_cp("parallel"))(*[a for a, _, _ in ins])


def _ffn_mid_fwd(up, cw, cb, name):
    L = up.shape[0]
    C = up.shape[1] // 2
    n_tiles = L // ROW_TILE
    lo, hi = slice(0, LANE), slice(LANE, 2 * LANE)

    def body(up_ref, cw_ref, cb_ref, a_ref, gate_ref):
        w0, w1, w2 = cw_ref[0:1, :], cw_ref[1:2, :], cw_ref[2:3, :]
        b = cb_ref[...]

        def step(i, c):
            r0 = pl.multiple_of(i * ROW_TILE, ROW_TILE)
            rows = pl.ds(r0, ROW_TILE)
            gp = up_ref[rows, lo]
            ext = jnp.concatenate([_prev_rows(up_ref, r0, i, lo), gp], axis=0)
            gate = gp * w2 + _back(ext, 1) * w1 + _back(ext, 2) * w0 + b
            gate_ref[rows, :] = gate
            a_ref[rows, :] = (gate * _sigmoid(gate) * up_ref[rows, hi]).astype(BF16)
            return c

        lax.fori_loop(0, n_tiles, step, 0)

    return _strip_call(body, L, C // LANE,
                       [(up, 2 * LANE, 0), (cw, LANE, 0), (cb, LANE, 0)],
                       [(SDS((L, C), BF16), LANE, 0), (SDS((L, C), F32), LANE, 0)], name)


def _ffn_mid_bwd(da, up, gate, cw, name):
    L, C = da.shape
    n_tiles = L // ROW_TILE
    lo, hi, al = slice(0, LANE), slice(LANE, 2 * LANE), slice(None)

    def body(da_ref, up_ref, gate_ref, cw_ref, dup_ref, st_ref):
        w0, w1, w2 = cw_ref[0:1, :], cw_ref[1:2, :], cw_ref[2:3, :]

        def dgate_of(da_v, val_v, gate_v):
            return da_v * val_v * _dsilu(gate_v, _sigmoid(gate_v))

        def step(i, c):
            r0 = pl.multiple_of(i * ROW_TILE, ROW_TILE)
            rows = pl.ds(r0, ROW_TILE)
            da_v, gate_v, val = da_ref[rows, :], gate_ref[rows, :], up_ref[rows, hi]
            gp = up_ref[rows, lo]
            dgate = dgate_of(da_v, val, gate_v)
            dgate_n = dgate_of(_next_rows(da_ref, r0, i, n_tiles, al), _next_rows(up_ref, r0, i, n_tiles, hi),
                               _next_rows(gate_ref, r0, i, n_tiles, al))
            ext = jnp.concatenate([dgate, dgate_n], axis=0)
            dgp = dgate * w2 + _fwd(ext, 1) * w1 + _fwd(ext, 2) * w0
            dup_ref[rows, lo] = dgp.astype(BF16)
            dup_ref[rows, hi] = (da_v * gate_v * _sigmoid(gate_v)).astype(BF16)
            pext = jnp.concatenate([_prev_rows(up_ref, r0, i, lo), gp], axis=0)
            s = lambda v: jnp.sum(v, axis=0, keepdims=True)
            return (c[0] + s(dgate * _back(pext, 2)), c[1] + s(dgate * _back(pext, 1)), c[2] + s(dgate * gp),
                    c[3] + s(dgate))

        z = jnp.zeros((1, LANE), F32)
        _store_rows(st_ref, lax.fori_loop(0, n_tiles, step, (z, z, z, z)))

    return _strip_call(body, L, C // LANE,
                       [(da, LANE, 0), (up, 2 * LANE, 0), (gate, LANE, 0), (cw, LANE, 0)],
                       [(SDS((L, 2 * C), BF16), 2 * LANE, 0), (SDS((8, C), F32), LANE, 0)], name)


def _sc_mid_fwd(bcv, cw, name):
    L = bcv.shape[0]
    C = bcv.shape[1] // 3
    n_tiles = L // ROW_TILE
    s0, s1, s2 = slice(0, LANE), slice(LANE, 2 * LANE), slice(2 * LANE, 3 * LANE)

    def body(x_ref, cw_ref, q_ref):
        w0, w1, w2 = cw_ref[0:1, :], cw_ref[1:2, :], cw_ref[2:3, :]

        def step(i, c):
            r0 = pl.multiple_of(i * ROW_TILE, ROW_TILE)
            rows = pl.ds(r0, ROW_TILE)
            p = x_ref[rows, s1] * x_ref[rows, s2]
            ext = jnp.concatenate([_prev_rows(x_ref, r0, i, s1) * _prev_rows(x_ref, r0, i, s2), p], axis=0)
            u = p * w2 + _back(ext, 1) * w1 + _back(ext, 2) * w0
            q_ref[rows, :] = (x_ref[rows, s0] * u).astype(BF16)
            return c

        lax.fori_loop(0, n_tiles, step, 0)

    return _strip_call(body, L, C // LANE, [(bcv, 3 * LANE, 0), (cw, LANE, 0)],
                       [(SDS((L, C), BF16), LANE, 0)], name)[0]


def _sc_mid_bwd(dq, bcv, cw, name):
    L, C = dq.shape
    n_tiles = L // ROW_TILE
    s0, s1, s2, al = slice(0, LANE), slice(LANE, 2 * LANE), slice(2 * LANE, 3 * LANE), slice(None)

    def body(dq_ref, x_ref, cw_ref, dx_ref, st_ref):
        w0, w1, w2 = cw_ref[0:1, :], cw_ref[1:2, :], cw_ref[2:3, :]

        def step(i, c):
            r0 = pl.multiple_of(i * ROW_TILE, ROW_TILE)
            rows = pl.ds(r0, ROW_TILE)
            gb, gc, v, dq_v = x_ref[rows, s0], x_ref[rows, s1], x_ref[rows, s2], dq_ref[rows, :]
            p = gc * v
            pext = jnp.concatenate([_prev_rows(x_ref, r0, i, s1) * _prev_rows(x_ref, r0, i, s2), p], axis=0)
            p1, p2 = _back(pext, 1), _back(pext, 2)
            u = p * w2 + p1 * w1 + p2 * w0
            du = dq_v * gb
            du_n = _next_rows(dq_ref, r0, i, n_tiles, al) * _next_rows(x_ref, r0, i, n_tiles, s0)
            ext = jnp.concatenate([du, du_n], axis=0)
            dp = du * w2 + _fwd(ext, 1) * w1 + _fwd(ext, 2) * w0
            dx_ref[rows, s0] = (dq_v * u).astype(BF16)
            dx_ref[rows, s1] = (dp * v).astype(BF16)
            dx_ref[rows, s2] = (dp * gc).astype(BF16)
            s = lambda t: jnp.sum(t, axis=0, keepdims=True)
            return (c[0] + s(du * p2), c[1] + s(du * p1), c[2] + s(du * p))

        z = jnp.zeros((1, LANE), F32)
        _store_rows(st_ref, lax.fori_loop(0, n_tiles, step, (z, z, z)))

    return _strip_call(body, L, C // LANE, [(dq, LANE, 0), (bcv, 3 * LANE, 0), (cw, LANE, 0)],
                       [(SDS((L, 3 * C), BF16), 3 * LANE, 0), (SDS((8, C), F32), LANE, 0)], name)


def _ssd_conv_fwd(zx, cw, cb, col0, C, name):
    L = zx.shape[0]
    n_tiles = L // ROW_TILE
    al = slice(None)

    def body(x_ref, cw_ref, cb_ref, o_ref):
        w0, w1, w2, w3 = cw_ref[0:1, :], cw_ref[1:2, :], cw_ref[2:3, :], cw_ref[3:4, :]
        b = cb_ref[...]

        def step(i, c):
            r0 = pl.multiple_of(i * ROW_TILE, ROW_TILE)
            rows = pl.ds(r0, ROW_TILE)
            xv = x_ref[rows, :]
            ext = jnp.concatenate([_prev_rows(x_ref, r0, i, al), xv], axis=0)
            cv = xv * w3 + _back(ext, 1) * w2 + _back(ext, 2) * w1 + _back(ext, 3) * w0 + b
            o_ref[rows, :] = cv * _sigmoid(cv)
            return c

        lax.fori_loop(0, n_tiles, step, 0)

    return _strip_call(body, L, C // LANE, [(zx, LANE, col0 // LANE), (cw, LANE, 0), (cb, LANE, 0)],
                       [(SDS((L, C), F32), LANE, 0)], name)[0]


def _ssd_conv_bwd(dxbc, zx, cw, cb, col0, name):
    L, C = dxbc.shape
    n_tiles = L // ROW_TILE
    al = slice(None)

    def body(d_ref, x_ref, cw_ref, cb_ref, o_ref, st_ref):
        w0, w1, w2, w3 = cw_ref[0:1, :], cw_ref[1:2, :], cw_ref[2:3, :], cw_ref[3:4, :]
        b = cb_ref[...]

        def step(i, c):
            r0 = pl.multiple_of(i * ROW_TILE, ROW_TILE)
            rows = pl.ds(r0, ROW_TILE)
            xv = x_ref[rows, :]
            xe = jnp.concatenate([_prev_rows(x_ref, r0, i, al), xv, _next_rows(x_ref, r0, i, n_tiles, al)], axis=0)
            x1, x2, x3 = pltpu.roll(xe, 1, axis=0), pltpu.roll(xe, 2, axis=0), pltpu.roll(xe, 3, axis=0)
            cv = (xe * w3 + x1 * w2 + x2 * w1 + x3 * w0 + b)[HALO:]
            de = jnp.concatenate([d_ref[rows, :], _next_rows(d_ref, r0, i, n_tiles, al)], axis=0)
            dc_ext = de * _dsilu(cv, _sigmoid(cv))
            dc = dc_ext[:ROW_TILE]
            o_ref[rows, :] = (dc * w3 + _fwd(dc_ext, 1) * w2 + _fwd(dc_ext, 2) * w1 + _fwd(dc_ext, 3) * w0).astype(BF16)
            s = lambda t: jnp.sum(t, axis=0, keepdims=True)
            t = slice(HALO, HALO + ROW_TILE)
            return (c[0] + s(dc * x3[t]), c[1] + s(dc * x2[t]), c[2] + s(dc * x1[t]), c[3] + s(dc * xv), c[4] + s(dc))

        z = jnp.zeros((1, LANE), F32)
        _store_rows(st_ref, lax.fori_loop(0, n_tiles, step, (z, z, z, z, z)))

    return _strip_call(body, L, C // LANE,
                       [(dxbc, LANE, 0), (zx, LANE, col0 // LANE), (cw, LANE, 0), (cb, LANE, 0)],
                       [(SDS((L, C), BF16), LANE, 0), (SDS((8, C), F32), LANE, 0)], name)


def _scan_constants(n_heads):
    hw = n_heads * HEAD_DIM
    col = np.arange(hw)
    ind = (col[None, :] // HEAD_DIM == np.arange(LANE)[:, None]).astype(np.float32)
    gcol = np.arange(GROUP_W)
    itile = (gcol[None, :] % CHUNK == np.arange(CHUNK)[:, None]).astype(np.float32)
    trit = (gcol[None, :] % CHUNK <= np.arange(CHUNK)[:, None]).astype(np.float32)
    tril = np.tril(np.ones((CHUNK, CHUNK), np.float32))
    bmask = (gcol[:, None] // HEAD_DIM == gcol[None, :] // HEAD_DIM).astype(np.float32)
    return (jnp.asarray(ind), jnp.asarray(ind.T.copy()), jnp.asarray(itile), jnp.asarray(trit), jnp.asarray(tril),
            jnp.asarray(bmask))


def _softplus(x):
    return jnp.maximum(x, 0.0) + jnp.log(1.0 + jnp.exp(-jnp.abs(x)))


def _dot_hi(a, b):
    return jnp.dot(a, b, preferred_element_type=F32, precision=HI)


def _group_terms(g, dt, cs, xbc_ref, ind_ref, itile, trit, bmask, d_inner):
    gl = slice(g * GROUP_W, (g + 1) * GROUP_W)
    indg = ind_ref[:, gl]
    csl = _dot_hi(cs, indg)
    dtx = _dot_hi(dt, indg)
    rr = _dot_hi(jnp.ones((CHUNK, CHUNK), F32), csl * itile)
    lm = jnp.exp(jnp.where(trit > 0.0, csl - rr, -jnp.inf))
    xs = xbc_ref[:, gl]
    b = xbc_ref[:, d_inner + g * D_STATE: d_inner + (g + 1) * D_STATE]
    c = xbc_ref[:, d_inner + (N_GROUPS + g) * D_STATE: d_inner + (N_GROUPS + g + 1) * D_STATE]
    u = xs * dtx
    bb, cb = b.astype(BF16), c.astype(BF16)
    btile = jnp.concatenate([bb] * HEADS_PER_GROUP, axis=0)
    cbt = lax.dot_general(cb, btile, NT, preferred_element_type=F32)
    m = cbt * lm
    ub = u.astype(BF16)
    bdu = jnp.where(bmask > 0.0, jnp.concatenate([ub] * HEADS_PER_GROUP, axis=0), jnp.zeros((), BF16))
    c_last = csl[CHUNK - 1:CHUNK, :]
    return dict(gl=gl, indg=indg, csl=csl, dtx=dtx, lm=lm, xs=xs, bb=bb, cb=cb, u=u, btile=btile, m=m, bdu=bdu,
                e=jnp.exp(csl), dec=jnp.exp(c_last - csl), e_last=jnp.exp(c_last))


def _ssd_scan_fwd(zx, xbc, par, dexp, nw, consts, name):
    L = xbc.shape[0]
    d_inner = dexp.shape[1]
    n_chunks = L // CHUNK
    dt_blk = zx.shape[1] // LANE - 1
    ind, ind_t, itile_c, trit_c, tril_c, bmask_c = consts

    def body(xbc_ref, z_ref, dtr_ref, par_ref, dexp_ref, nw_ref, ind_ref, itile_ref, trit_ref, tril_ref, bmask_ref,
             yn_ref, yf_ref, st_out_ref, st_ref):
        @pl.when(pl.program_id(0) == 0)
        def _():
            st_ref[...] = jnp.zeros_like(st_ref)

        dt = _softplus(dtr_ref[...] + par_ref[0:1, :])
        a_head = -jnp.exp(par_ref[1:2, :])
        cs = _dot_hi(tril_ref[...], dt * a_head)
        itile, trit, bmask = itile_ref[...], trit_ref[...], bmask_ref[...]
        for g in range(N_GROUPS):
            t = _group_terms(g, dt, cs, xbc_ref, ind_ref, itile, trit, bmask, d_inner)
            p = st_ref[g]
            st_out_ref[0, g] = p
            y = jnp.dot(t["m"].astype(BF16), t["bdu"], preferred_element_type=F32)
            y = y + jnp.dot(t["cb"], p.astype(BF16), preferred_element_type=F32) * t["e"]
            st_new = lax.dot_general(t["bb"], (t["u"] * t["dec"]).astype(BF16), TN, preferred_element_type=F32)
            st_ref[g] = p * t["e_last"] + st_new
            yf_ref[:, t["gl"]] = y + t["xs"] * dexp_ref[:, t["gl"]]
        z = z_ref[...]
        y2 = yf_ref[...] * (z * _sigmoid(z))
        yn_ref[...] = (y2 * _rsq(y2) * nw_ref[...]).astype(BF16)

    row = lambda c: (c, 0)
    fix = lambda c: (0, 0)
    cspec = lambda a: pl.BlockSpec(a.shape, fix)
    return pl.pallas_call(
        body, grid=(n_chunks,),
        in_specs=[pl.BlockSpec((CHUNK, xbc.shape[1]), row), pl.BlockSpec((CHUNK, d_inner), row),
                  pl.BlockSpec((CHUNK, LANE), lambda c: (c, dt_blk)), cspec(par), cspec(dexp), cspec(nw),
                  cspec(ind), cspec(itile_c), cspec(trit_c), cspec(tril_c), cspec(bmask_c)],
        out_specs=[pl.BlockSpec((CHUNK, d_inner), row), pl.BlockSpec((CHUNK, d_inner), row),
                   pl.BlockSpec((1, N_GROUPS, D_STATE, GROUP_W), lambda c: (c, 0, 0, 0))],
        out_shape=[SDS((L, d_inner), BF16), SDS((L, d_inner), F32),
                   SDS((n_chunks, N_GROUPS, D_STATE, GROUP_W), F32)],
        scratch_shapes=[pltpu.VMEM((N_GROUPS, D_STATE, GROUP_W), F32)],
        name=name, compiler_params=_cp("arbitrary"))(xbc, zx, zx, par, dexp, nw, ind, itile_c, trit_c, tril_c, bmask_c)


def _ssd_scan_bwd(dyn, yf, zx, xbc, states, par, dexp, nw, consts, name):
    L = xbc.shape[0]
    d_inner = dexp.shape[1]
    n_chunks = L // CHUNK
    dt_blk = zx.shape[1] // LANE - 1
    ind, ind_t, itile_c, trit_c, tril_c, bmask_c = consts
    hslices = [slice(r * HEAD_DIM, (r + 1) * HEAD_DIM) for r in range(HEADS_PER_GROUP)]

    def body(dyn_ref, yf_ref, z_ref, dtr_ref, xbc_ref, st_in_ref, par_ref, dexp_ref, nw_ref, ind_ref, indt_ref,
             itile_ref, trit_ref, tril_ref, bmask_ref,
             dz_ref, dxbc_ref, ddt_ref, dnw_ref, dpar_ref, dq_ref, dyf_ref):
        @pl.when(pl.program_id(0) == 0)
        def _():
            dq_ref[...] = jnp.zeros_like(dq_ref)
            dnw_ref[...] = jnp.zeros_like(dnw_ref)
            dpar_ref[...] = jnp.zeros_like(dpar_ref)

        z, yfv, dynv = z_ref[...], yf_ref[...], dyn_ref[...]
        sz = _sigmoid(z)
        y2 = yfv * (z * sz)
        r = _rsq(y2)
        y2h = y2 * r
        dnw_ref[...] += jnp.sum(dynv * y2h, axis=0, keepdims=True)
        dyg = dynv * nw_ref[...]
        dy2 = r * (dyg - y2h * jnp.mean(dyg * y2h, axis=-1, keepdims=True))
        dz_ref[...] = (dy2 * yfv * _dsilu(z, sz)).astype(BF16)
        dyf_ref[...] = dy2 * (z * sz)

        pre = dtr_ref[...] + par_ref[0:1, :]
        dt = _softplus(pre)
        a_head = -jnp.exp(par_ref[1:2, :])
        cs = _dot_hi(tril_ref[...], dt * a_head)
        itile, trit, bmask = itile_ref[...], trit_ref[...], bmask_ref[...]
        dcs = jnp.zeros((CHUNK, LANE), F32)
        dcs_last = jnp.zeros((1, LANE), F32)
        ddt_u = jnp.zeros((CHUNK, LANE), F32)
        d_skip = jnp.zeros((1, LANE), F32)
        rsum = lambda v: jnp.sum(v, axis=0, keepdims=True)
        row8 = lax.broadcasted_iota(jnp.int32, (8, GROUP_W), 0)
        for g in range(N_GROUPS):
            t = _group_terms(g, dt, cs, xbc_ref, ind_ref, itile, trit, bmask, d_inner)
            gl, m, lm, u, bb, cb, e, dec, xs = (t[k] for k in ("gl", "m", "lm", "u", "bb", "cb", "e", "dec", "xs"))
            indt = indt_ref[gl, :]
            dy = dyf_ref[:, gl]
            dyb = dy.astype(BF16)
            p = st_in_ref[0, g]
            pb = p.astype(BF16)
            q = dq_ref[g]
            qb = q.astype(BF16)
            big = lax.dot_general(m.astype(BF16), dyb, TN, preferred_element_type=F32)
            du = jnp.zeros((CHUNK, GROUP_W), F32)
            for rh in range(HEADS_PER_GROUP):
                du = du + big[hslices[rh], :] * bmask[rh * HEAD_DIM:rh * HEAD_DIM + 1, :]
            dm = lax.dot_general(dyb, t["bdu"], NT, preferred_element_type=F32)
            w = dm * m
            dgt = (dm * lm).astype(BF16)
            dc = jnp.dot(dgt, t["btile"], preferred_element_type=F32)
            db_big = lax.dot_general(dgt, cb, TN, preferred_element_type=F32)
            db = db_big[hslices[0], :] + db_big[hslices[1], :] + db_big[hslices[2], :] + db_big[hslices[3], :]
            cp = jnp.dot(cb, pb, preferred_element_type=F32)
            dye = dy * e
            dyeb = dye.astype(BF16)
            dc = dc + lax.dot_general(dyeb, pb, NT, preferred_element_type=F32)
            dp = lax.dot_general(cb, dyeb, TN, preferred_element_type=F32)
            x2 = dye * cp
            bq = jnp.dot(bb, qb, preferred_element_type=F32)
            ud = u * dec
            du = du + bq * dec
            db = db + lax.dot_general(ud.astype(BF16), qb, NT, preferred_element_type=F32)
            x1 = bq * ud
            dq_ref[g] = dp + t["e_last"] * q
            x3 = rsum(q * p) * t["e_last"]
            red = _dot_hi(jnp.concatenate([w + x2 - x1, du * xs, itile * rsum(w)], axis=0), indt)
            dcs = dcs + red[0:CHUNK] - red[2 * CHUNK:3 * CHUNK]
            ddt_u = ddt_u + red[CHUNK:2 * CHUNK]
            tail = _dot_hi(jnp.where(row8 == 0, rsum(x1) + x3, jnp.where(row8 == 1, rsum(dy * xs), 0.0)), indt)
            dcs_last = dcs_last + tail[0:1]
            d_skip = d_skip + tail[1:2]
            dxbc_ref[:, gl] = du * t["dtx"] + dy * dexp_ref[:, gl]
            dxbc_ref[:, d_inner + g * D_STATE: d_inner + (g + 1) * D_STATE] = db
            dxbc_ref[:, d_inner + (N_GROUPS + g) * D_STATE: d_inner + (N_GROUPS + g + 1) * D_STATE] = dc
        last = lax.broadcasted_iota(jnp.int32, (CHUNK, LANE), 0) == CHUNK - 1
        dcs = dcs + jnp.where(last, dcs_last, 0.0)
        da = lax.dot_general(tril_ref[...], dcs, TN, preferred_element_type=F32, precision=HI)
        ddt = da * a_head + ddt_u
        heads = lax.broadcasted_iota(jnp.int32, (CHUNK, LANE), 1) < d_inner // HEAD_DIM
        ddt_raw = jnp.where(heads, ddt * _sigmoid(pre), 0.0)
        ddt_ref[...] = ddt_raw.astype(BF16)
        dpar_ref[0:1, :] += rsum(ddt_raw)
        dpar_ref[1:2, :] += rsum(da * dt) * a_head
        dpar_ref[2:3, :] += d_skip

    rev = lambda c: (n_chunks - 1 - c, 0)
    fix = lambda c: (0, 0)
    cspec = lambda a: pl.BlockSpec(a.shape, fix)
    nx = xbc.shape[1]
    return pl.pallas_call(
        body, grid=(n_chunks,),
        in_specs=[pl.BlockSpec((CHUNK, d_inner), rev), pl.BlockSpec((CHUNK, d_inner), rev),
                  pl.BlockSpec((CHUNK, d_inner), rev), pl.BlockSpec((CHUNK, LANE), lambda c: (n_chunks - 1 - c, dt_blk)),
                  pl.BlockSpec((CHUNK, nx), rev),
                  pl.BlockSpec((1, N_GROUPS, D_STATE, GROUP_W), lambda c: (n_chunks - 1 - c, 0, 0, 0)),
                  cspec(par), cspec(dexp), cspec(nw), cspec(ind), cspec(ind_t), cspec(itile_c), cspec(trit_c),
                  cspec(tril_c), cspec(bmask_c)],
        out_specs=[pl.BlockSpec((CHUNK, d_inner), rev), pl.BlockSpec((CHUNK, nx), rev),
                   pl.BlockSpec((CHUNK, LANE), rev), pl.BlockSpec((1, d_inner), fix), pl.BlockSpec((8, LANE), fix)],
        out_shape=[SDS((L, d_inner), BF16), SDS((L, nx), F32), SDS((L, LANE), BF16), SDS((1, d_inner), F32),
                   SDS((8, LANE), F32)],
        scratch_shapes=[pltpu.VMEM((N_GROUPS, D_STATE, GROUP_W), F32), pltpu.VMEM((CHUNK, d_inner), F32)],
        name=name, compiler_params=_cp("arbitrary"))(
            dyn, yf, zx, zx, xbc, states, par, dexp, nw, ind, ind_t, itile_c, trit_c, tril_c, bmask_c)


def _adamw(w, m, v, g, name):
    R, C = w.shape
    tr = R
    for cand in (256, 128, 64, 32, 16, 8):
        if R % cand == 0:
            tr = cand
            break

    def body(w_ref, m_ref, v_ref, g_ref, d_ref, mo_ref, vo_ref):
        gv = g_ref[...]
        mn = ADAM_B1 * m_ref[...] + (1.0 - ADAM_B1) * gv
        vn = ADAM_B2 * v_ref[...] + (1.0 - ADAM_B2) * (gv * gv)
        m_hat = mn / (1.0 - ADAM_B1 ** ADAM_STEP)
        v_hat = vn / (1.0 - ADAM_B2 ** ADAM_STEP)
        d_ref[...] = -ADAM_LR * (m_hat / (jnp.sqrt(v_hat) + ADAM_EPS) + ADAM_WD * w_ref[...])
        mo_ref[...] = mn
        vo_ref[...] = vn

    blk = pl.BlockSpec((tr, C), lambda i: (i, 0))
    return pl.pallas_call(
        body, grid=(R // tr,), in_specs=[blk] * 4, out_specs=[blk] * 3, out_shape=[SDS((R, C), F32)] * 3,
        name=name, compiler_params=_cp("parallel"))(w, m, v, g)


def _sum_slots(parts, name):
    n, R, C = parts.shape
    tr = 128 if R % 128 == 0 else R

    def body(p_ref, o_ref):
        acc = p_ref[0]
        for k in range(1, n):
            acc = acc + p_ref[k]
        o_ref[...] = acc

    return pl.pallas_call(
        body, grid=(R // tr,), in_specs=[pl.BlockSpec((n, tr, C), lambda i: (0, i, 0))],
        out_specs=pl.BlockSpec((tr, C), lambda i: (i, 0)), out_shape=SDS((R, C), F32),
        name=name, compiler_params=_cp("parallel"))(parts)


def _add_core_halves(where, g, r, name):
    _, n0, a, b = g.shape
    h, ta = n0 // 2, a // 2

    def body(w_ref, g_ref, r_ref, o_ref):
        o_ref[...] = (g_ref[...].astype(F32) + r_ref[...].astype(F32)).astype(BF16)

    blk = lambda f: pl.BlockSpec((None, None, ta, b), f)
    mine = lambda s, l, i, w: (s, l, i, 0)
    return pl.pallas_call(
        body, grid_spec=pltpu.PrefetchScalarGridSpec(
            num_scalar_prefetch=1, grid=(4, h, 2),
            in_specs=[blk(lambda s, l, i, w: (s, w[1] * h + l, i, 0)), blk(mine)], out_specs=blk(mine)),
        out_shape=SDS((4, h, a, b), BF16), name=name,
        compiler_params=_cp("parallel", "parallel", "parallel"))(where, g, r)


def _sum_shard(where, g, r, rr, name):
    _, n0, a, b = g.shape
    h, ta = n0 // 2, a // 2

    def body(w_ref, g_ref, r_ref, rr_ref, o_ref):
        f = lambda v: v.astype(F32)
        o_ref[...] = (((f(g_ref[...]) + f(r_ref[...])) + f(rr_ref[0])) + f(rr_ref[1])) + f(rr_ref[2])

    return pl.pallas_call(
        body, grid_spec=pltpu.PrefetchScalarGridSpec(
            num_scalar_prefetch=1, grid=(h, 2),
            in_specs=[pl.BlockSpec((None, None, ta, b), lambda l, i, w: (w[0], w[1] * h + l, i, 0)),
                      pl.BlockSpec((None, None, ta, b), lambda l, i, w: (w[0], l, i, 0)),
                      pl.BlockSpec((3, None, ta, b), lambda l, i, w: (0, l, i, 0))],
            out_specs=pl.BlockSpec((None, ta, b), lambda l, i, w: (w[1] * h + l, i, 0))),
        out_shape=SDS((n0, a, b), F32), name=name,
        compiler_params=_cp("parallel", "parallel"))(where, g, r, rr)


def _me():
    return lax.axis_index("x"), lax.axis_index("y"), lax.axis_index("c")


def _chip_peers(x, y):
    return [(1 - x, y), (x, 1 - y), (1 - x, 1 - y)]


def _rcopy(src, dst, send_sems, recv_sems, k, to):
    return pltpu.make_async_remote_copy(src_ref=src, dst_ref=dst, send_sem=send_sems.at[k], recv_sem=recv_sems.at[k],
                                        device_id=to, device_id_type=MESH)


def _half(ref, c, lead=()):
    h = ref.shape[len(lead)] // 2
    return ref.at[(*lead, pl.ds(c * h, h))]


def _allgather_weights(mine, small):
    n = len(mine)

    def body(*refs):
        mine_refs, small_ref = refs[:n], refs[n]
        out_refs, sout_ref = refs[n + 1:2 * n + 1], refs[2 * n + 1]
        send_sems, recv_sems = refs[2 * n + 2:]
        x, y, c = _me()
        q = 2 * x + y
        sib = (x, y, 1 - c)
        chips = _chip_peers(x, y)
        sem = lambda t, k: 6 * t + k
        first = []
        for j, chip in enumerate(chips):
            for t in range(n):
                first.append(_rcopy(_half(mine_refs[t], c), _half(out_refs[t], c, (q,)), send_sems, recv_sems,
                                    sem(t, j), (*chip, c)))
            first.append(_rcopy(small_ref, sout_ref.at[q], send_sems, recv_sems, sem(n, j), (*chip, c)))
        for cp in first:
            cp.start()
        passed = []
        for j, (px, py) in enumerate(chips):
            for t in range(n):
                blk = _half(out_refs[t], c, (2 * px + py,))
                _rcopy(blk, blk, send_sems, recv_sems, sem(t, j), sib).wait_recv()
                cp = _rcopy(blk, blk, send_sems, recv_sems, sem(t, 3 + j), sib)
                cp.start()
                passed.append(cp)
        for j, (px, py) in enumerate(chips):
            for t in range(n):
                blk = _half(out_refs[t], 1 - c, (2 * px + py,))
                _rcopy(blk, blk, send_sems, recv_sems, sem(t, 3 + j), sib).wait_recv()
            sblk = sout_ref.at[2 * px + py]
            _rcopy(sblk, sblk, send_sems, recv_sems, sem(n, j), sib).wait_recv()
        for cp in first + passed:
            cp.wait_send()

    n_sems = 6 * n + 3
    return pl.pallas_call(
        body, in_specs=[ANY] * (n + 1), out_specs=[ANY] * (n + 1),
        out_shape=[SDS((4,) + m.shape, BF16) for m in mine] + [SDS((4,) + small.shape, F32)],
        scratch_shapes=[pltpu.SemaphoreType.DMA((n_sems,)), pltpu.SemaphoreType.DMA((n_sems,))],
        name="allgather_weights")(*mine, small)


def _grads_to_sibling(grads):
    n = len(grads)

    def body(*refs):
        g_refs, out_refs = refs[:n], refs[n:2 * n]
        send_sems, recv_sems = refs[2 * n:]
        x, y, c = _me()
        cps = [_rcopy(_half(g_refs[t], 1 - c, (slice(None),)), out_refs[t], send_sems, recv_sems, t, (x, y, 1 - c))
               for t in range(n)]
        for cp in cps:
            cp.start()
        for cp in cps:
            cp.wait()

    return pl.pallas_call(
        body, in_specs=[ANY] * n, out_specs=[ANY] * n,
        out_shape=[SDS((4, g.shape[1] // 2) + g.shape[2:], BF16) for g in grads],
        scratch_shapes=[pltpu.SemaphoreType.DMA((n,)), pltpu.SemaphoreType.DMA((n,))],
        name="grads_to_sibling")(*grads)


def _grads_to_chips(psums):
    n = len(psums)

    def body(*refs):
        p_refs, out_refs = refs[:n], refs[n:2 * n]
        send_sems, recv_sems = refs[2 * n:]
        x, y, c = _me()
        cps = [_rcopy(p_refs[t].at[2 * px + py], out_refs[t].at[j], send_sems, recv_sems, 3 * t + j, (px, py, c))
               for j, (px, py) in enumerate(_chip_peers(x, y)) for t in range(n)]
        for cp in cps:
            cp.start()
        for cp in cps:
            cp.wait()

    return pl.pallas_call(
        body, in_specs=[ANY] * n, out_specs=[ANY] * n,
        out_shape=[SDS((3,) + p.shape[1:], BF16) for p in psums],
        scratch_shapes=[pltpu.SemaphoreType.DMA((3 * n,)), pltpu.SemaphoreType.DMA((3 * n,))],
        name="grads_to_chips")(*psums)


def _swap_halves(sums):
    n = len(sums)

    def body(*refs):
        out_refs = refs[n:2 * n]
        send_sems, recv_sems = refs[2 * n:]
        x, y, c = _me()
        sib = (x, y, 1 - c)
        cps = [_rcopy(_half(out_refs[t], c), _half(out_refs[t], c), send_sems, recv_sems, t, sib) for t in range(n)]
        for cp in cps:
            cp.start()
        for t in range(n):
            other = _half(out_refs[t], 1 - c)
            _rcopy(other, other, send_sems, recv_sems, t, sib).wait_recv()
        for cp in cps:
            cp.wait_send()

    return pl.pallas_call(
        body, in_specs=[ANY] * n, out_specs=[ANY] * n, out_shape=[SDS(s.shape, F32) for s in sums],
        input_output_aliases={t: t for t in range(n)},
        scratch_shapes=[pltpu.SemaphoreType.DMA((n,)), pltpu.SemaphoreType.DMA((n,))],
        name="swap_halves")(*sums)


def _allgather_small(part):
    def body(p_ref, out_ref, send_sems, recv_sems, local_sem):
        x, y, c = _me()
        me = 4 * x + 2 * y + c
        own = pltpu.make_async_copy(p_ref, out_ref.at[me], local_sem.at[0])
        own.start()
        sends = []
        for k in range(1, 8):
            fx, fy, fc = (k >> 2) & 1, (k >> 1) & 1, k & 1
            to = (x ^ fx, y ^ fy, c ^ fc)
            sends.append(_rcopy(p_ref, out_ref.at[me], send_sems, recv_sems, k - 1, to))
        for cp in sends:
            cp.start()
        for k in range(1, 8):
            slot = out_ref.at[me ^ k]
            _rcopy(slot, slot, send_sems, recv_sems, k - 1, (x, y, c)).wait_recv()
        for cp in sends:
            cp.wait_send()
        own.wait()

    return pl.pallas_call(
        body, in_specs=[ANY], out_specs=ANY, out_shape=SDS((8,) + part.shape, F32),
        scratch_shapes=[pltpu.SemaphoreType.DMA((7,)), pltpu.SemaphoreType.DMA((7,)), pltpu.SemaphoreType.DMA((1,))],
        name="allgather_small")(part)


BIG = (("ssd_w_in", 2), ("ssd_w_out", 1), ("sc_w_in", 2), ("sc_w_out", 1), ("ffn_w_up", 2), ("ffn_w_down", 1))


def _to_shards(full, axis):
    A, B = full.shape
    if axis == 2:
        return full.reshape(A, 4, B // 4).transpose(1, 0, 2)
    return full.reshape(4, A // 4, B)


def _from_shards(shards, axis):
    _, a, b = shards.shape
    if axis == 2:
        return shards.transpose(1, 0, 2).reshape(a, 4 * b)
    return shards.reshape(4 * a, b)


def _interleave(w, parts):
    lead, n = w.shape[:-1], w.shape[-1]
    return w.reshape(*lead, parts, n // (parts * LANE), LANE).swapaxes(-2, -3).reshape(*lead, n)


def _deinterleave(w, parts):
    lead, n = w.shape[:-1], w.shape[-1]
    return w.reshape(*lead, n // (parts * LANE), parts, LANE).swapaxes(-2, -3).reshape(*lead, n)


def _pack_rows(vectors, width, row_multiple):
    flat = jnp.concatenate(vectors, axis=-1)
    n = flat.shape[-1]
    unit = width * row_multiple
    total = -(-n // unit) * unit
    flat = jnp.pad(flat, [(0, 0)] * (flat.ndim - 1) + [(0, total - n)])
    return flat.reshape(*flat.shape[:-1], total // width, width)


def _unpack(flat, shapes):
    out, off = [], 0
    for s in shapes:
        n = int(np.prod(s))
        out.append(flat[..., off:off + n].reshape(*flat.shape[:-1], *s))
        off += n
    return out


def _memo(fn):
    cache = {}

    def wrapped(k):
        if k not in cache:
            cache[k] = fn(k)
        return cache[k]

    return wrapped


def _row(v):
    return v.reshape(1, -1)


def _pad_rows(w, rows=8):
    return jnp.pad(w, ((0, rows - w.shape[0]), (0, 0)))


def _ffn_fwd(x, g_pre, g_post, w_up, cw, cb, w_down, tag):
    up, hn = _norm_matmul(x, g_pre, w_up, 512, "ffn_up" + tag)
    a, gate = _ffn_mid_fwd(up, cw, cb, "ffn_mid_fwd" + tag)
    f, x_new = _matmul_norm_res(a, w_down, x, g_post, "ffn_down" + tag)
    return x_new, (x, hn, up, gate, a, f)


def _ffn_bwd(dx, saved, g_pre, g_post, w_up, cw, w_down, tag):
    x, hn, up, gate, a, f = saved
    df, dg_post = _postnorm_bwd(dx, f, g_post, "ffn_post_bwd" + tag)
    da = _matmul_nt(df, w_down, 256, "ffn_down_dx" + tag)
    dw_down = _matmul_tn(a, df, 256, 1024, "ffn_down_dw" + tag)
    dup, stats = _ffn_mid_bwd(da, up, gate, cw, "ffn_mid_bwd" + tag)
    dx_in, dg_pre = _matmul_nt_prenorm_bwd(dup, w_up, x, g_pre, dx, 512, "ffn_up_dx" + tag)
    dw_up = _matmul_tn(hn, dup, 512, 512, "ffn_up_dw" + tag)
    return dx_in, dict(g_pre=dg_pre, g_post=dg_post, w_up=dw_up, w_down=dw_down, cw=stats[0:3], cb=stats[3])


def _sc_fwd(x, g_pre, g_post, w_in, cw, w_out, tag):
    bcv, hn = _norm_matmul(x, g_pre, w_in, 768, "sc_in" + tag)
    q = _sc_mid_fwd(bcv, cw, "sc_mid_fwd" + tag)
    m, x_new = _matmul_norm_res(q, w_out, x, g_post, "sc_out" + tag)
    return x_new, (x, hn, bcv, q, m)


def _sc_bwd(dx, saved, g_pre, g_post, w_in, cw, w_out, tag):
    x, hn, bcv, q, m = saved
    dm, dg_post = _postnorm_bwd(dx, m, g_post, "sc_post_bwd" + tag)
    dq = _matmul_nt(dm, w_out, 512, "sc_out_dx" + tag)
    dw_out = _matmul_tn(q, dm, 512, 1024, "sc_out_dw" + tag)
    dbcv, stats = _sc_mid_bwd(dq, bcv, cw, "sc_mid_bwd" + tag)
    dx_in, dg_pre = _matmul_nt_prenorm_bwd(dbcv, w_in, x, g_pre, dx, 768, "sc_in_dx" + tag)
    dw_in = _matmul_tn(hn, dbcv, 512, 768, "sc_in_dw" + tag)
    return dx_in, dict(g_pre=dg_pre, g_post=dg_post, w_in=dw_in, w_out=dw_out, cw=stats[0:3])


def _ssd_fwd(x, g_pre, g_post, w_in, cw, cb, par, dexp, nw, w_out, consts, tag):
    d_inner = dexp.shape[1]
    zx, hn = _norm_matmul(x, g_pre, w_in, 896, "ssd_in" + tag)
    xbc = _ssd_conv_fwd(zx, cw, cb, d_inner, cw.shape[1], "ssd_conv_fwd" + tag)
    yn, yf, states = _ssd_scan_fwd(zx, xbc, par, dexp, nw, consts, "ssd_scan_fwd" + tag)
    m, x_new = _matmul_norm_res(yn, w_out, x, g_post, "ssd_out" + tag)
    return x_new, (x, hn, zx, xbc, yn, yf, states, m)


def _ssd_bwd(dx, saved, g_pre, g_post, w_in, cw, cb, par, dexp, nw, w_out, consts, tag):
    x, hn, zx, xbc, yn, yf, states, m = saved
    d_inner = dexp.shape[1]
    dm, dg_post = _postnorm_bwd(dx, m, g_post, "ssd_post_bwd" + tag)
    dyn = _matmul_nt(dm, w_out, 512, "ssd_out_dx" + tag)
    dw_out = _matmul_tn(yn, dm, 512, 1024, "ssd_out_dw" + tag)
    dz, dxbc, ddt, dnw, dpar = _ssd_scan_bwd(dyn, yf, zx, xbc, states, par, dexp, nw, consts, "ssd_scan_bwd" + tag)
    dxp, stats = _ssd_conv_bwd(dxbc, zx, cw, cb, d_inner, "ssd_conv_bwd" + tag)
    dzx = jnp.concatenate([dz, dxp, ddt], axis=1)
    dx_in, dg_pre = _matmul_nt_prenorm_bwd(dzx, w_in, x, g_pre, dx, 896, "ssd_in_dx" + tag)
    dw_in = _matmul_tn(hn, dzx, 512, 896, "ssd_in_dw" + tag)
    n_heads = d_inner // HEAD_DIM
    return dx_in, dict(g_pre=dg_pre, g_post=dg_post, w_in=dw_in, w_out=dw_out, cw=stats[0:4], cb=stats[4],
                       dt_bias=dpar[0, :n_heads], a_log=dpar[1, :n_heads], d=dpar[2, :n_heads], nw=dnw[0])


def kernel(x, mix_pre_g, mix_post_g, ffn_pre_g, ffn_post_g, ssd_w_in, ssd_conv_w, ssd_conv_b, ssd_dt_bias, ssd_A_log, ssd_D, ssd_norm_w, ssd_w_out, sc_w_in, sc_conv_w, sc_w_out, ffn_w_up, ffn_conv_w, ffn_conv_b, ffn_w_down, loss_target, m_mix_pre_g, m_mix_post_g, m_ffn_pre_g, m_ffn_post_g, m_ssd_w_in, m_ssd_conv_w, m_ssd_conv_b, m_ssd_dt_bias, m_ssd_A_log, m_ssd_D, m_ssd_norm_w, m_ssd_w_out, m_sc_w_in, m_sc_conv_w, m_sc_w_out, m_ffn_w_up, m_ffn_conv_w, m_ffn_conv_b, m_ffn_w_down, v_mix_pre_g, v_mix_post_g, v_ffn_pre_g, v_ffn_post_g, v_ssd_w_in, v_ssd_conv_w, v_ssd_conv_b, v_ssd_dt_bias, v_ssd_A_log, v_ssd_D, v_ssd_norm_w, v_ssd_w_out, v_sc_w_in, v_sc_conv_w, v_sc_w_out, v_ffn_w_up, v_ffn_conv_w, v_ffn_conv_b, v_ffn_w_down):
    names = ["mix_pre_g", "mix_post_g", "ffn_pre_g", "ffn_post_g", "ssd_w_in", "ssd_conv_w", "ssd_conv_b",
             "ssd_dt_bias", "ssd_A_log", "ssd_D", "ssd_norm_w", "ssd_w_out", "sc_w_in", "sc_conv_w", "sc_w_out",
             "ffn_w_up", "ffn_conv_w", "ffn_conv_b", "ffn_w_down"]
    env = locals()
    wts = {n: env[n] for n in names}
    mom = {n: env["m_" + n] for n in names}
    var = {n: env["v_" + n] for n in names}

    depth, d_model = mix_pre_g.shape
    n_ssd, n_heads = ssd_dt_bias.shape
    n_sc = sc_conv_w.shape[0]
    d_inner = n_heads * HEAD_DIM
    conv_dim = d_inner + 2 * N_GROUPS * D_STATE
    ssd_in_dim = d_inner + conv_dim + n_heads
    ssd_in_pad = d_inner + conv_dim + LANE
    q_chip = 2 * lax.axis_index("x") + lax.axis_index("y")
    core = lax.axis_index("c")

    mine = [wts[n].astype(BF16) for n, _ in BIG]
    conv_names = ["ssd_conv_w", "sc_conv_w", "ffn_conv_w"]
    conv_shapes = [wts[n].shape for n in conv_names]
    small_mine = _pack_rows([wts[n].reshape(-1) for n in conv_names], LANE, 8)
    *gathered, small_all = _allgather_weights(mine, small_mine)
    own_slot = lambda buf, own: lax.dynamic_update_index_in_dim(buf, own, q_chip, 0)
    gathered = {n: own_slot(g, m) for (n, _), g, m in zip(BIG, gathered, mine)}
    axis_of = dict(BIG)

    def full(n, layer):
        return _from_shards(gathered[n][:, layer], axis_of[n])

    small_all = own_slot(small_all, small_mine)
    conv_full = {}
    for n, f, s in zip(conv_names, _unpack(small_all.reshape(4, -1), conv_shapes), conv_shapes):
        conv_full[n] = f.transpose(1, 2, 0, 3).reshape(s[0], s[1], 4 * s[2])

    consts = _scan_constants(n_heads)

    def ssd_args(j):
        par = jnp.zeros((8, LANE), F32).at[0, :n_heads].set(ssd_dt_bias[j]).at[1, :n_heads].set(ssd_A_log[j])
        dexp = jnp.repeat(ssd_D[j], HEAD_DIM).reshape(1, d_inner)
        w_in = jnp.pad(full("ssd_w_in", j), ((0, 0), (0, ssd_in_pad - ssd_in_dim)))
        return (w_in, _pad_rows(conv_full["ssd_conv_w"][j]), _row(ssd_conv_b[j]), par, dexp,
                _row(ssd_norm_w[j]), full("ssd_w_out", j), consts)

    def sc_args(j):
        return (_interleave(full("sc_w_in", j), 3), _pad_rows(conv_full["sc_conv_w"][j]), full("sc_w_out", j))

    def ffn_args(i):
        return (_interleave(full("ffn_w_up", i), 2), _pad_rows(conv_full["ffn_conv_w"][i]), _row(ffn_conv_b[i]),
                full("ffn_w_down", i))

    ssd_args, sc_args, ffn_args = _memo(ssd_args), _memo(sc_args), _memo(ffn_args)

    h = x[0]
    saved = []
    for i in range(depth):
        j = i // 2
        gp, gq = _row(mix_pre_g[i]), _row(mix_post_g[i])
        if i % 2 == 0:
            h, sv = _ssd_fwd(h, gp, gq, *ssd_args(j), tag="")
        else:
            h, sv = _sc_fwd(h, gp, gq, *sc_args(j), tag="")
        w_up, cw, cb, w_down = ffn_args(i)
        h, sv2 = _ffn_fwd(h, _row(ffn_pre_g[i]), _row(ffn_post_g[i]), w_up, cw, cb, w_down, tag="")
        saved.append((sv, sv2))
    dh, loss_part = _loss_head(h, loss_target[0], "loss_head")

    mix_grads, ffn_grads = [None] * depth, [None] * depth
    for i in reversed(range(depth)):
        j = i // 2
        sv, sv2 = saved[i]
        w_up, cw, cb, w_down = ffn_args(i)
        dh, ffn_grads[i] = _ffn_bwd(dh, sv2, _row(ffn_pre_g[i]), _row(ffn_post_g[i]), w_up, cw, w_down, tag="")
        gp, gq = _row(mix_pre_g[i]), _row(mix_post_g[i])
        if i % 2 == 0:
            w_in, scw, scb, par, dexp, nw, w_out, _ = ssd_args(j)
            dh, mix_grads[i] = _ssd_bwd(dh, sv, gp, gq, w_in, scw, scb, par, dexp, nw, w_out, consts, tag="")
        else:
            w_in, scw, w_out = sc_args(j)
            dh, mix_grads[i] = _sc_bwd(dh, sv, gp, gq, w_in, scw, w_out, tag="")
    grad_x = dh[None]

    ssd_l = [mix_grads[i] for i in range(0, depth, 2)]
    sc_l = [mix_grads[i] for i in range(1, depth, 2)]
    stack = lambda layers, k: jnp.stack([g[k] for g in layers])
    local = {
        "ssd_w_in": [g["w_in"][:, :ssd_in_dim] for g in ssd_l],
        "ssd_w_out": [g["w_out"] for g in ssd_l],
        "sc_w_in": [_deinterleave(g["w_in"], 3) for g in sc_l],
        "sc_w_out": [g["w_out"] for g in sc_l],
        "ffn_w_up": [_deinterleave(g["w_up"], 2) for g in ffn_grads],
        "ffn_w_down": [g["w_down"] for g in ffn_grads],
    }
    by_shard = [jnp.stack([_to_shards(dw, ax).astype(BF16) for dw in local[n]], axis=1) for n, ax in BIG]
    from_sib = _grads_to_sibling(by_shard)
    where = jnp.stack([q_chip, core]).astype(jnp.int32)
    chip_sums = [_add_core_halves(where, g, r, "add_core_halves_" + n) for (n, _), g, r in zip(BIG, by_shard, from_sib)]
    from_chips = _grads_to_chips(chip_sums)
    half_sums = [_sum_shard(where, g, r, rr, "sum_shard_" + n)
                 for (n, _), g, r, rr in zip(BIG, by_shard, from_sib, from_chips)]
    big_grads = dict(zip([n for n, _ in BIG], _swap_halves(half_sums)))

    small_names = ["mix_pre_g", "mix_post_g", "ffn_pre_g", "ffn_post_g", "ssd_conv_w", "ssd_conv_b", "ssd_dt_bias",
                   "ssd_A_log", "ssd_D", "ssd_norm_w", "sc_conv_w", "ffn_conv_w", "ffn_conv_b"]
    small_local = {
        "mix_pre_g": jnp.concatenate([g["g_pre"] for g in mix_grads]),
        "mix_post_g": jnp.concatenate([g["g_post"] for g in mix_grads]),
        "ffn_pre_g": jnp.concatenate([g["g_pre"] for g in ffn_grads]),
        "ffn_post_g": jnp.concatenate([g["g_post"] for g in ffn_grads]),
        "ssd_conv_w": stack(ssd_l, "cw"), "ssd_conv_b": stack(ssd_l, "cb"), "ssd_dt_bias": stack(ssd_l, "dt_bias"),
        "ssd_A_log": stack(ssd_l, "a_log"), "ssd_D": stack(ssd_l, "d"), "ssd_norm_w": stack(ssd_l, "nw"),
        "sc_conv_w": stack(sc_l, "cw"), "ffn_conv_w": stack(ffn_grads, "cw"), "ffn_conv_b": stack(ffn_grads, "cb"),
    }
    small_full_shapes = [small_local[n].shape for n in small_names]
    spack = _pack_rows([small_local[n].reshape(-1) for n in small_names] + [loss_part.reshape(-1)], LANE, 8)
    stotal = _sum_slots(_allgather_small(spack), "sum_small").reshape(-1)
    small_grads = dict(zip(small_names, _unpack(stotal, small_full_shapes)))
    loss = stotal[sum(int(np.prod(s)) for s in small_full_shapes)]
    for n in conv_names:
        width = wts[n].shape[-1]
        small_grads[n] = lax.dynamic_slice_in_dim(small_grads[n], q_chip * width, width, axis=2)

    grads, delta, new_m, new_v = {}, {}, {}, {}
    for n, _ in BIG:
        s = wts[n].shape
        two_d = lambda a: a.reshape(-1, s[-1])
        grads[n] = big_grads[n]
        d, mn, vn = _adamw(two_d(wts[n]), two_d(mom[n]), two_d(var[n]), two_d(grads[n]), "adamw_" + n)
        delta[n], new_m[n], new_v[n] = d.reshape(s), mn.reshape(s), vn.reshape(s)
    small_shapes = [wts[n].shape for n in small_names]
    pk = lambda d: _pack_rows([d[n].reshape(-1) for n in small_names], LANE, 8)
    for n in small_names:
        grads[n] = small_grads[n].reshape(wts[n].shape)
    d, mn, vn = _adamw(pk(wts), pk(mom), pk(var), pk(grads), "adamw_small")
    for out, packed in ((delta, d), (new_m, mn), (new_v, vn)):
        out.update(zip(small_names, _unpack(packed.reshape(-1), small_shapes)))

    return (loss, grad_x, *[grads[n] for n in names], *[delta[n] for n in names], *[new_m[n] for n in names],
            *[new_v[n] for n in names])
```

```python
import jax
import jax.numpy as jnp
import numpy as np
from jax import lax
from jax.experimental import pallas as pl
from jax.experimental.pallas import tpu as pltpu

F32 = jnp.float32
BF16 = jnp.bfloat16
SDS = jax.ShapeDtypeStruct
MESH = pl.DeviceIdType.MESH
ANY = pl.BlockSpec(memory_space=pl.ANY)

EPS = 1e-6
CHUNK = 64
HEAD_DIM = 64
N_GROUPS = 8
D_STATE = 128
HEADS_PER_GROUP = 4
GROUP_W = HEADS_PER_GROUP * HEAD_DIM
LANE = 128
ROW_TILE = 128
HALO = 8
VMEM_LIMIT = 56 * 1024 * 1024

ADAM_LR = 0.001
ADAM_B1 = 0.9
ADAM_B2 = 0.999
ADAM_EPS = 1e-08
ADAM_WD = 0.01
ADAM_STEP = 10

NT = (((1,), (1,)), ((), ()))
TN = (((0,), (0,)), ((), ()))


def _cp(*sem):
    return pltpu.CompilerParams(dimension_semantics=sem or None, vmem_limit_bytes=VMEM_LIMIT)


def _sigmoid(x):
    return 1.0 / (1.0 + jnp.exp(-x))


def _dsilu(x, s):
    return s * (1.0 + x * (1.0 - s))


def _rsq(x):
    return lax.rsqrt(jnp.mean(x * x, axis=-1, keepdims=True) + EPS)


MM_ROWS = 256


def _norm_matmul(x, g, w, out_dtype, name):
    L, D = x.shape
    N = w.shape[1]
    tm = min(MM_ROWS, L)

    def body(x_ref, g_ref, w_ref, o_ref, hn_ref):
        xv = x_ref[...]
        hn = (xv * _rsq(xv) * g_ref[...]).astype(BF16)
        hn_ref[...] = hn
        o_ref[...] = jnp.dot(hn, w_ref[...], preferred_element_type=F32).astype(out_dtype)

    row = lambda i: (i, 0)
    fix = lambda i: (0, 0)
    return pl.pallas_call(
        body, grid=(L // tm,),
        in_specs=[pl.BlockSpec((tm, D), row), pl.BlockSpec((1, D), fix), pl.BlockSpec((D, N), fix)],
        out_specs=[pl.BlockSpec((tm, N), row), pl.BlockSpec((tm, D), row)],
        out_shape=[SDS((L, N), out_dtype), SDS((L, D), BF16)],
        name=name, compiler_params=_cp("parallel"))(x, g, w)


def _matmul_norm_res(a, w, x, g, name):
    L, K = a.shape
    D = w.shape[1]
    tm = min(256, L)

    def body(a_ref, w_ref, x_ref, g_ref, m_ref, xo_ref):
        m = jnp.dot(a_ref[...], w_ref[...], preferred_element_type=F32)
        m_ref[...] = m
        xo_ref[...] = x_ref[...] + m * _rsq(m) * g_ref[...]

    row = lambda i: (i, 0)
    fix = lambda i: (0, 0)
    return pl.pallas_call(
        body, grid=(L // tm,),
        in_specs=[pl.BlockSpec((tm, K), row), pl.BlockSpec((K, D), fix), pl.BlockSpec((tm, D), row),
                  pl.BlockSpec((1, D), fix)],
        out_specs=[pl.BlockSpec((tm, D), row), pl.BlockSpec((tm, D), row)],
        out_shape=[SDS((L, D), F32), SDS((L, D), F32)],
        name=name, compiler_params=_cp("parallel"))(a, w, x, g)


def _postnorm_bwd(dx, m, g, name):
    L, D = dx.shape
    tm = min(512, L)

    def body(dx_ref, m_ref, g_ref, dm_ref, dg_ref):
        @pl.when(pl.program_id(0) == 0)
        def _():
            dg_ref[...] = jnp.zeros_like(dg_ref)

        m = m_ref[...]
        dxv = dx_ref[...]
        r = _rsq(m)
        mh = m * r
        dg_ref[...] += jnp.sum(dxv * mh, axis=0, keepdims=True)
        dyg = dxv * g_ref[...]
        dm_ref[...] = (r * (dyg - mh * jnp.mean(dyg * mh, axis=-1, keepdims=True))).astype(BF16)

    row = lambda i: (i, 0)
    fix = lambda i: (0, 0)
    return pl.pallas_call(
        body, grid=(L // tm,),
        in_specs=[pl.BlockSpec((tm, D), row), pl.BlockSpec((tm, D), row), pl.BlockSpec((1, D), fix)],
        out_specs=[pl.BlockSpec((tm, D), row), pl.BlockSpec((1, D), fix)],
        out_shape=[SDS((L, D), BF16), SDS((1, D), F32)],
        name=name, compiler_params=_cp("arbitrary"))(dx, m, g)


def _matmul_nt(a, w, out_dtype, name):
    L, D = a.shape
    K = w.shape[0]
    tm = min(MM_ROWS, L)

    def body(a_ref, w_ref, o_ref):
        o_ref[...] = lax.dot_general(a_ref[...], w_ref[...], NT, preferred_element_type=F32).astype(out_dtype)

    return pl.pallas_call(
        body, grid=(L // tm,),
        in_specs=[pl.BlockSpec((tm, D), lambda i: (i, 0)), pl.BlockSpec((K, D), lambda i: (0, 0))],
        out_specs=pl.BlockSpec((tm, K), lambda i: (i, 0)),
        out_shape=SDS((L, K), out_dtype),
        name=name, compiler_params=_cp("parallel"))(a, w)


def _matmul_tn(a, b, ta, tn, name):
    L, Ka = a.shape
    N = b.shape[1]
    tl = min(512, L)
    n_l = L // tl

    def body(a_ref, b_ref, o_ref, acc_ref):
        l = pl.program_id(2)

        @pl.when(l == 0)
        def _():
            acc_ref[...] = jnp.zeros_like(acc_ref)

        acc_ref[...] += lax.dot_general(a_ref[...], b_ref[...], TN, preferred_element_type=F32)

        @pl.when(l == n_l - 1)
        def _():
            o_ref[...] = acc_ref[...]

    return pl.pallas_call(
        body, grid=(Ka // ta, N // tn, n_l),
        in_specs=[pl.BlockSpec((tl, ta), lambda i, j, l: (l, i)), pl.BlockSpec((tl, tn), lambda i, j, l: (l, j))],
        out_specs=pl.BlockSpec((ta, tn), lambda i, j, l: (i, j)),
        out_shape=SDS((Ka, N), F32),
        scratch_shapes=[pltpu.VMEM((ta, tn), F32)],
        name=name, compiler_params=_cp("parallel", "parallel", "arbitrary"))(a, b)


def _matmul_nt_prenorm_bwd(dy, w, x, g, dres, name):
    L, N = dy.shape
    D = w.shape[0]
    tm = min(MM_ROWS, L)

    def body(dy_ref, w_ref, x_ref, g_ref, dres_ref, dx_ref, dg_ref):
        @pl.when(pl.program_id(0) == 0)
        def _():
            dg_ref[...] = jnp.zeros_like(dg_ref)

        dh = lax.dot_general(dy_ref[...], w_ref[...], NT, preferred_element_type=F32)
        xv = x_ref[...]
        r = _rsq(xv)
        xh = xv * r
        dg_ref[...] += jnp.sum(dh * xh, axis=0, keepdims=True)
        dyg = dh * g_ref[...]
        dx_ref[...] = dres_ref[...] + r * (dyg - xh * jnp.mean(dyg * xh, axis=-1, keepdims=True))

    row = lambda i: (i, 0)
    fix = lambda i: (0, 0)
    return pl.pallas_call(
        body, grid=(L // tm,),
        in_specs=[pl.BlockSpec((tm, N), row), pl.BlockSpec((D, N), fix), pl.BlockSpec((tm, D), row),
                  pl.BlockSpec((1, D), fix), pl.BlockSpec((tm, D), row)],
        out_specs=[pl.BlockSpec((tm, D), row), pl.BlockSpec((1, D), fix)],
        out_shape=[SDS((L, D), F32), SDS((1, D), F32)],
        name=name, compiler_params=_cp("arbitrary"))(dy, w, x, g, dres)


def _loss_head(y, t, name):
    L, D = y.shape
    tm = min(512, L)

    def body(y_ref, t_ref, dy_ref, loss_ref):
        @pl.when(pl.program_id(0) == 0)
        def _():
            loss_ref[...] = jnp.zeros_like(loss_ref)

        e = y_ref[...] - t_ref[...]
        dy_ref[...] = e * (1.0 / D)
        s = jnp.sum(jnp.sum(e * e, axis=1, keepdims=True), axis=0, keepdims=True)
        loss_ref[...] += s * (0.5 / D)

    row = lambda i: (i, 0)
    return pl.pallas_call(
        body, grid=(L // tm,),
        in_specs=[pl.BlockSpec((tm, D), row), pl.BlockSpec((tm, D), row)],
        out_specs=[pl.BlockSpec((tm, D), row), pl.BlockSpec((1, 1), lambda i: (0, 0))],
        out_shape=[SDS((L, D), F32), SDS((1, 1), F32)],
        name=name, compiler_params=_cp("arbitrary"))(y, t)


def _tile_rows(ref):
    return HALO * (4 // jnp.dtype(ref.dtype).itemsize)


def _prev_rows(ref, r0, i, cols):
    n = _tile_rows(ref)
    p0 = pl.multiple_of(jnp.maximum(r0 - n, 0), n)
    return jnp.where(i > 0, ref[pl.ds(p0, n), cols].astype(F32)[n - HALO:], 0.0)


def _next_rows(ref, r0, i, n_tiles, cols):
    n = _tile_rows(ref)
    n0 = pl.multiple_of(jnp.minimum(r0 + ROW_TILE, n_tiles * ROW_TILE - n), n)
    return jnp.where(i < n_tiles - 1, ref[pl.ds(n0, n), cols].astype(F32)[:HALO], 0.0)


def _rows_f32(ref, rows, cols):
    return ref[rows, cols].astype(F32)


def _back(ext, s):
    return pltpu.roll(ext, s, axis=0)[HALO:HALO + ROW_TILE]


def _fwd(ext, s):
    n = ext.shape[0]
    return pltpu.roll(ext, n - s, axis=0)[:ROW_TILE]


def _store_rows(ref, rows):
    ref[...] = jnp.zeros_like(ref)
    for k, v in enumerate(rows):
        ref[k:k + 1, :] = v


def _strip_call(body, L, n_strips, ins, outs, name):
    def spec(rows, width, off):
        if off is None:
            return pl.BlockSpec((rows, width), lambda j: (0, 0))
        return pl.BlockSpec((rows, width), lambda j: (0, j + off))

    return pl.pallas_call(
        body, grid=(n_strips,),
        in_specs=[spec(a.shape[0], w, off) for a, w, off in ins],
        out_specs=[spec(s.shape[0], w, off) for s, w, off in outs],
        out_shape=[s for s, _, _ in outs],
        name=name, compiler_params=_cp("parallel"))(*[a for a, _, _ in ins])


def _ffn_mid_fwd(up, cw, cb, name):
    L = up.shape[0]
    C = up.shape[1] // 2
    n_tiles = L // ROW_TILE
    lo, hi = slice(0, LANE), slice(LANE, 2 * LANE)

    def body(up_ref, cw_ref, cb_ref, a_ref):
        w0, w1, w2 = cw_ref[0:1, :], cw_ref[1:2, :], cw_ref[2:3, :]
        b = cb_ref[...]

        def step(i, c):
            r0 = pl.multiple_of(i * ROW_TILE, ROW_TILE)
            rows = pl.ds(r0, ROW_TILE)
            gp = _rows_f32(up_ref, rows, lo)
            ext = jnp.concatenate([_prev_rows(up_ref, r0, i, lo), gp], axis=0)
            gate = gp * w2 + _back(ext, 1) * w1 + _back(ext, 2) * w0 + b
            a_ref[rows, :] = (gate * _sigmoid(gate) * _rows_f32(up_ref, rows, hi)).astype(BF16)
            return c

        lax.fori_loop(0, n_tiles, step, 0)

    return _strip_call(body, L, C // LANE,
                       [(up, 2 * LANE, 0), (cw, LANE, 0), (cb, LANE, 0)],
                       [(SDS((L, C), BF16), LANE, 0)], name)[0]


def _ffn_mid_bwd(da, up, cw, cb, name):
    L, C = da.shape
    n_tiles = L // ROW_TILE
    lo, hi, al = slice(0, LANE), slice(LANE, 2 * LANE), slice(None)

    def body(da_ref, up_ref, cw_ref, cb_ref, dup_ref, st_ref):
        w0, w1, w2 = cw_ref[0:1, :], cw_ref[1:2, :], cw_ref[2:3, :]
        b = cb_ref[...]

        def step(i, c):
            r0 = pl.multiple_of(i * ROW_TILE, ROW_TILE)
            rows = pl.ds(r0, ROW_TILE)
            gp = _rows_f32(up_ref, rows, lo)
            gpe = jnp.concatenate([_prev_rows(up_ref, r0, i, lo), gp, _next_rows(up_ref, r0, i, n_tiles, lo)], axis=0)
            g1, g2 = pltpu.roll(gpe, 1, axis=0), pltpu.roll(gpe, 2, axis=0)
            gate = (gpe * w2 + g1 * w1 + g2 * w0 + b)[HALO:]
            sg = _sigmoid(gate)
            da_e = jnp.concatenate([_rows_f32(da_ref, rows, al), _next_rows(da_ref, r0, i, n_tiles, al)], axis=0)
            val_e = jnp.concatenate([_rows_f32(up_ref, rows, hi), _next_rows(up_ref, r0, i, n_tiles, hi)], axis=0)
            dgate_e = da_e * val_e * _dsilu(gate, sg)
            dgate = dgate_e[:ROW_TILE]
            dgp = dgate * w2 + _fwd(dgate_e, 1) * w1 + _fwd(dgate_e, 2) * w0
            dup_ref[rows, lo] = dgp.astype(BF16)
            dup_ref[rows, hi] = (da_e * gate * sg)[:ROW_TILE].astype(BF16)
            s = lambda v: jnp.sum(v, axis=0, keepdims=True)
            t = slice(HALO, HALO + ROW_TILE)
            return (c[0] + s(dgate * g2[t]), c[1] + s(dgate * g1[t]), c[2] + s(dgate * gp), c[3] + s(dgate))

        z = jnp.zeros((1, LANE), F32)
        _store_rows(st_ref, lax.fori_loop(0, n_tiles, step, (z, z, z, z)))

    return _strip_call(body, L, C // LANE,
                       [(da, LANE, 0), (up, 2 * LANE, 0), (cw, LANE, 0), (cb, LANE, 0)],
                       [(SDS((L, 2 * C), BF16), 2 * LANE, 0), (SDS((8, C), F32), LANE, 0)], name)


def _sc_mid_fwd(bcv, cw, name):
    L = bcv.shape[0]
    C = bcv.shape[1] // 3
    n_tiles = L // ROW_TILE
    s0, s1, s2 = slice(0, LANE), slice(LANE, 2 * LANE), slice(2 * LANE, 3 * LANE)

    def body(x_ref, cw_ref, q_ref):
        w0, w1, w2 = cw_ref[0:1, :], cw_ref[1:2, :], cw_ref[2:3, :]

        def step(i, c):
            r0 = pl.multiple_of(i * ROW_TILE, ROW_TILE)
            rows = pl.ds(r0, ROW_TILE)
            p = _rows_f32(x_ref, rows, s1) * _rows_f32(x_ref, rows, s2)
            ext = jnp.concatenate([_prev_rows(x_ref, r0, i, s1) * _prev_rows(x_ref, r0, i, s2), p], axis=0)
            u = p * w2 + _back(ext, 1) * w1 + _back(ext, 2) * w0
            q_ref[rows, :] = (_rows_f32(x_ref, rows, s0) * u).astype(BF16)
            return c

        lax.fori_loop(0, n_tiles, step, 0)

    return _strip_call(body, L, C // LANE, [(bcv, 3 * LANE, 0), (cw, LANE, 0)],
                       [(SDS((L, C), BF16), LANE, 0)], name)[0]


def _sc_mid_bwd(dq, bcv, cw, name):
    L, C = dq.shape
    n_tiles = L // ROW_TILE
    s0, s1, s2, al = slice(0, LANE), slice(LANE, 2 * LANE), slice(2 * LANE, 3 * LANE), slice(None)

    def body(dq_ref, x_ref, cw_ref, dx_ref, st_ref):
        w0, w1, w2 = cw_ref[0:1, :], cw_ref[1:2, :], cw_ref[2:3, :]

        def step(i, c):
            r0 = pl.multiple_of(i * ROW_TILE, ROW_TILE)
            rows = pl.ds(r0, ROW_TILE)
            gb, gc, v = _rows_f32(x_ref, rows, s0), _rows_f32(x_ref, rows, s1), _rows_f32(x_ref, rows, s2)
            dq_v = _rows_f32(dq_ref, rows, al)
            p = gc * v
            pext = jnp.concatenate([_prev_rows(x_ref, r0, i, s1) * _prev_rows(x_ref, r0, i, s2), p], axis=0)
            p1, p2 = _back(pext, 1), _back(pext, 2)
            u = p * w2 + p1 * w1 + p2 * w0
            du = dq_v * gb
            du_n = _next_rows(dq_ref, r0, i, n_tiles, al) * _next_rows(x_ref, r0, i, n_tiles, s0)
            ext = jnp.concatenate([du, du_n], axis=0)
            dp = du * w2 + _fwd(ext, 1) * w1 + _fwd(ext, 2) * w0
            dx_ref[rows, s0] = (dq_v * u).astype(BF16)
            dx_ref[rows, s1] = (dp * v).astype(BF16)
            dx_ref[rows, s2] = (dp * gc).astype(BF16)
            s = lambda t: jnp.sum(t, axis=0, keepdims=True)
            return (c[0] + s(du * p2), c[1] + s(du * p1), c[2] + s(du * p))

        z = jnp.zeros((1, LANE), F32)
        _store_rows(st_ref, lax.fori_loop(0, n_tiles, step, (z, z, z)))

    return _strip_call(body, L, C // LANE, [(dq, LANE, 0), (bcv, 3 * LANE, 0), (cw, LANE, 0)],
                       [(SDS((L, 3 * C), BF16), 3 * LANE, 0), (SDS((8, C), F32), LANE, 0)], name)


def _ssd_conv_fwd(zx, cw, cb, col0, C, name):
    L = zx.shape[0]
    n_tiles = L // ROW_TILE
    al = slice(None)

    def body(x_ref, cw_ref, cb_ref, o_ref):
        w0, w1, w2, w3 = cw_ref[0:1, :], cw_ref[1:2, :], cw_ref[2:3, :], cw_ref[3:4, :]
        b = cb_ref[...]

        def step(i, c):
            r0 = pl.multiple_of(i * ROW_TILE, ROW_TILE)
            rows = pl.ds(r0, ROW_TILE)
            xv = x_ref[rows, :]
            ext = jnp.concatenate([_prev_rows(x_ref, r0, i, al), xv], axis=0)
            cv = xv * w3 + _back(ext, 1) * w2 + _back(ext, 2) * w1 + _back(ext, 3) * w0 + b
            o_ref[rows, :] = cv * _sigmoid(cv)
            return c

        lax.fori_loop(0, n_tiles, step, 0)

    return _strip_call(body, L, C // LANE, [(zx, LANE, col0 // LANE), (cw, LANE, 0), (cb, LANE, 0)],
                       [(SDS((L, C), F32), LANE, 0)], name)[0]


def _ssd_conv_bwd(dxbc, zx, cw, cb, col0, name):
    L, C = dxbc.shape
    n_tiles = L // ROW_TILE
    al = slice(None)

    def body(d_ref, x_ref, cw_ref, cb_ref, o_ref, st_ref):
        w0, w1, w2, w3 = cw_ref[0:1, :], cw_ref[1:2, :], cw_ref[2:3, :], cw_ref[3:4, :]
        b = cb_ref[...]

        def step(i, c):
            r0 = pl.multiple_of(i * ROW_TILE, ROW_TILE)
            rows = pl.ds(r0, ROW_TILE)
            xv = x_ref[rows, :]
            xe = jnp.concatenate([_prev_rows(x_ref, r0, i, al), xv, _next_rows(x_ref, r0, i, n_tiles, al)], axis=0)
            x1, x2, x3 = pltpu.roll(xe, 1, axis=0), pltpu.roll(xe, 2, axis=0), pltpu.roll(xe, 3, axis=0)
            cv = (xe * w3 + x1 * w2 + x2 * w1 + x3 * w0 + b)[HALO:]
            de = jnp.concatenate([d_ref[rows, :], _next_rows(d_ref, r0, i, n_tiles, al)], axis=0)
            dc_ext = de * _dsilu(cv, _sigmoid(cv))
            dc = dc_ext[:ROW_TILE]
            o_ref[rows, :] = (dc * w3 + _fwd(dc_ext, 1) * w2 + _fwd(dc_ext, 2) * w1 + _fwd(dc_ext, 3) * w0).astype(BF16)
            s = lambda t: jnp.sum(t, axis=0, keepdims=True)
            t = slice(HALO, HALO + ROW_TILE)
            return (c[0] + s(dc * x3[t]), c[1] + s(dc * x2[t]), c[2] + s(dc * x1[t]), c[3] + s(dc * xv), c[4] + s(dc))

        z = jnp.zeros((1, LANE), F32)
        _store_rows(st_ref, lax.fori_loop(0, n_tiles, step, (z, z, z, z, z)))

    return _strip_call(body, L, C // LANE,
                       [(dxbc, LANE, 0), (zx, LANE, col0 // LANE), (cw, LANE, 0), (cb, LANE, 0)],
                       [(SDS((L, C), BF16), LANE, 0), (SDS((8, C), F32), LANE, 0)], name)


def _scan_constants(n_heads):
    hw = n_heads * HEAD_DIM
    col = np.arange(hw)
    ind = (col[None, :] // HEAD_DIM == np.arange(LANE)[:, None]).astype(np.float32)
    gcol = np.arange(GROUP_W)
    itile = (gcol[None, :] % CHUNK == np.arange(CHUNK)[:, None]).astype(np.float32)
    trit = (gcol[None, :] % CHUNK <= np.arange(CHUNK)[:, None]).astype(np.float32)
    tril = np.tril(np.ones((CHUNK, CHUNK), np.float32))
    bmask = (gcol[:, None] // HEAD_DIM == gcol[None, :] // HEAD_DIM).astype(np.float32)
    return (jnp.asarray(ind, BF16), jnp.asarray(ind.T.copy(), BF16), jnp.asarray(itile), jnp.asarray(trit),
            jnp.asarray(tril, BF16), jnp.asarray(bmask))


def _softplus(x):
    return jnp.maximum(x, 0.0) + jnp.log(1.0 + jnp.exp(-jnp.abs(x)))


def _split3(x):
    hi = x.astype(BF16)
    r1 = x - hi.astype(F32)
    mid = r1.astype(BF16)
    return hi, mid, (r1 - mid.astype(F32)).astype(BF16)


def _dot_sel(x, sel, dims=None):
    if dims is None:
        mm = lambda p: jnp.dot(p, sel, preferred_element_type=F32)
    else:
        mm = lambda p: lax.dot_general(sel, p, dims, preferred_element_type=F32)
    hi, mid, lo = _split3(x)
    return (mm(lo) + mm(mid)) + mm(hi)


SEL_X = (((1,), (0,)), ((), ()))


def _group_terms(g, dt, cs, xbc_ref, ind_ref, itile, trit, bmask, d_inner):
    gl = slice(g * GROUP_W, (g + 1) * GROUP_W)
    indg = ind_ref[:, gl]
    csl = _dot_sel(cs, indg)
    dtx = _dot_sel(dt, indg)
    rr = _dot_sel(csl * itile, jnp.ones((CHUNK, CHUNK), BF16), SEL_X)
    lm = jnp.exp(jnp.where(trit > 0.0, csl - rr, -jnp.inf))
    xs = xbc_ref[:, gl]
    b = xbc_ref[:, d_inner + g * D_STATE: d_inner + (g + 1) * D_STATE]
    c = xbc_ref[:, d_inner + (N_GROUPS + g) * D_STATE: d_inner + (N_GROUPS + g + 1) * D_STATE]
    u = xs * dtx
    bb, cb = b.astype(BF16), c.astype(BF16)
    btile = jnp.concatenate([bb] * HEADS_PER_GROUP, axis=0)
    cbt = lax.dot_general(cb, btile, NT, preferred_element_type=F32)
    m = cbt * lm
    ub = u.astype(BF16)
    bdu = jnp.where(bmask > 0.0, jnp.concatenate([ub] * HEADS_PER_GROUP, axis=0), jnp.zeros((), BF16))
    c_last = csl[CHUNK - 1:CHUNK, :]
    return dict(gl=gl, indg=indg, csl=csl, dtx=dtx, lm=lm, xs=xs, bb=bb, cb=cb, u=u, btile=btile, m=m, bdu=bdu,
                e=jnp.exp(csl), dec=jnp.exp(c_last - csl), e_last=jnp.exp(c_last))


def _ssd_scan_fwd(zx, xbc, par, dexp, nw, consts, name):
    L = xbc.shape[0]
    d_inner = dexp.shape[1]
    n_chunks = L // CHUNK
    dt_blk = zx.shape[1] // LANE - 1
    ind, ind_t, itile_c, trit_c, tril_c, bmask_c = consts

    def body(xbc_ref, z_ref, dtr_ref, par_ref, dexp_ref, nw_ref, ind_ref, itile_ref, trit_ref, tril_ref, bmask_ref,
             yn_ref, yf_ref, st_out_ref, st_ref):
        @pl.when(pl.program_id(0) == 0)
        def _():
            st_ref[...] = jnp.zeros_like(st_ref)

        dt = _softplus(dtr_ref[...] + par_ref[0:1, :])
        a_head = -jnp.exp(par_ref[1:2, :])
        cs = _dot_sel(dt * a_head, tril_ref[...], SEL_X)
        itile, trit, bmask = itile_ref[...], trit_ref[...], bmask_ref[...]
        for g in range(N_GROUPS):
            t = _group_terms(g, dt, cs, xbc_ref, ind_ref, itile, trit, bmask, d_inner)
            p = st_ref[g]
            st_out_ref[0, g] = p
            y = jnp.dot(t["m"].astype(BF16), t["bdu"], preferred_element_type=F32)
            y = y + jnp.dot(t["cb"], p.astype(BF16), preferred_element_type=F32) * t["e"]
            st_new = lax.dot_general(t["bb"], (t["u"] * t["dec"]).astype(BF16), TN, preferred_element_type=F32)
            st_ref[g] = p * t["e_last"] + st_new
            yf_ref[:, t["gl"]] = y + t["xs"] * dexp_ref[:, t["gl"]]
        z = z_ref[...]
        y2 = yf_ref[...] * (z * _sigmoid(z))
        yn_ref[...] = (y2 * _rsq(y2) * nw_ref[...]).astype(BF16)

    row = lambda c: (c, 0)
    fix = lambda c: (0, 0)
    cspec = lambda a: pl.BlockSpec(a.shape, fix)
    return pl.pallas_call(
        body, grid=(n_chunks,),
        in_specs=[pl.BlockSpec((CHUNK, xbc.shape[1]), row), pl.BlockSpec((CHUNK, d_inner), row),
                  pl.BlockSpec((CHUNK, LANE), lambda c: (c, dt_blk)), cspec(par), cspec(dexp), cspec(nw),
                  cspec(ind), cspec(itile_c), cspec(trit_c), cspec(tril_c), cspec(bmask_c)],
        out_specs=[pl.BlockSpec((CHUNK, d_inner), row), pl.BlockSpec((CHUNK, d_inner), row),
                   pl.BlockSpec((1, N_GROUPS, D_STATE, GROUP_W), lambda c: (c, 0, 0, 0))],
        out_shape=[SDS((L, d_inner), BF16), SDS((L, d_inner), F32),
                   SDS((n_chunks, N_GROUPS, D_STATE, GROUP_W), F32)],
        scratch_shapes=[pltpu.VMEM((N_GROUPS, D_STATE, GROUP_W), F32)],
        name=name, compiler_params=_cp("arbitrary"))(xbc, zx, zx, par, dexp, nw, ind, itile_c, trit_c, tril_c, bmask_c)


def _ssd_scan_bwd(dyn, yf, zx, xbc, states, par, dexp, nw, consts, name):
    L = xbc.shape[0]
    d_inner = dexp.shape[1]
    n_chunks = L // CHUNK
    dt_blk = zx.shape[1] // LANE - 1
    ind, ind_t, itile_c, trit_c, tril_c, bmask_c = consts
    hslices = [slice(r * HEAD_DIM, (r + 1) * HEAD_DIM) for r in range(HEADS_PER_GROUP)]

    def body(dyn_ref, yf_ref, z_ref, dtr_ref, xbc_ref, st_in_ref, par_ref, dexp_ref, nw_ref, ind_ref, indt_ref,
             itile_ref, trit_ref, tril_ref, bmask_ref,
             dz_ref, dxbc_ref, ddt_ref, dnw_ref, dpar_ref, dq_ref, dyf_ref):
        @pl.when(pl.program_id(0) == 0)
        def _():
            dq_ref[...] = jnp.zeros_like(dq_ref)
            dnw_ref[...] = jnp.zeros_like(dnw_ref)
            dpar_ref[...] = jnp.zeros_like(dpar_ref)

        z, yfv, dynv = z_ref[...], yf_ref[...], dyn_ref[...]
        sz = _sigmoid(z)
        y2 = yfv * (z * sz)
        r = _rsq(y2)
        y2h = y2 * r
        dnw_ref[...] += jnp.sum(dynv * y2h, axis=0, keepdims=True)
        dyg = dynv * nw_ref[...]
        dy2 = r * (dyg - y2h * jnp.mean(dyg * y2h, axis=-1, keepdims=True))
        dz_ref[...] = (dy2 * yfv * _dsilu(z, sz)).astype(BF16)
        dyf_ref[...] = dy2 * (z * sz)

        pre = dtr_ref[...] + par_ref[0:1, :]
        dt = _softplus(pre)
        a_head = -jnp.exp(par_ref[1:2, :])
        cs = _dot_sel(dt * a_head, tril_ref[...], SEL_X)
        itile, trit, bmask = itile_ref[...], trit_ref[...], bmask_ref[...]
        dcs = jnp.zeros((CHUNK, LANE), F32)
        dcs_last = jnp.zeros((1, LANE), F32)
        ddt_u = jnp.zeros((CHUNK, LANE), F32)
        d_skip = jnp.zeros((1, LANE), F32)
        rsum = lambda v: jnp.sum(v, axis=0, keepdims=True)
        row8 = lax.broadcasted_iota(jnp.int32, (8, GROUP_W), 0)
        for g in range(N_GROUPS):
            t = _group_terms(g, dt, cs, xbc_ref, ind_ref, itile, trit, bmask, d_inner)
            gl, m, lm, u, bb, cb, e, dec, xs = (t[k] for k in ("gl", "m", "lm", "u", "bb", "cb", "e", "dec", "xs"))
            indt = indt_ref[gl, :]
            dy = dyf_ref[:, gl]
            dyb = dy.astype(BF16)
            p = st_in_ref[0, g]
            pb = p.astype(BF16)
            q = dq_ref[g]
            qb = q.astype(BF16)
            big = lax.dot_general(m.astype(BF16), dyb, TN, preferred_element_type=F32)
            du = jnp.zeros((CHUNK, GROUP_W), F32)
            for rh in range(HEADS_PER_GROUP):
                du = du + big[hslices[rh], :] * bmask[rh * HEAD_DIM:rh * HEAD_DIM + 1, :]
            dm = lax.dot_general(dyb, t["bdu"], NT, preferred_element_type=F32)
            w = dm * m
            dgt = (dm * lm).astype(BF16)
            dc = jnp.dot(dgt, t["btile"], preferred_element_type=F32)
            db_big = lax.dot_general(dgt, cb, TN, preferred_element_type=F32)
            db = db_big[hslices[0], :] + db_big[hslices[1], :] + db_big[hslices[2], :] + db_big[hslices[3], :]
            cp = jnp.dot(cb, pb, preferred_element_type=F32)
            dye = dy * e
            dyeb = dye.astype(BF16)
            dc = dc + lax.dot_general(dyeb, pb, NT, preferred_element_type=F32)
            dp = lax.dot_general(cb, dyeb, TN, preferred_element_type=F32)
            x2 = dye * cp
            bq = jnp.dot(bb, qb, preferred_element_type=F32)
            ud = u * dec
            du = du + bq * dec
            db = db + lax.dot_general(ud.astype(BF16), qb, NT, preferred_element_type=F32)
            x1 = bq * ud
            dq_ref[g] = dp + t["e_last"] * q
            x3 = rsum(q * p) * t["e_last"]
            red = _dot_sel(jnp.concatenate([w + x2 - x1, du * xs, itile * rsum(w)], axis=0), indt)
            dcs = dcs + red[0:CHUNK] - red[2 * CHUNK:3 * CHUNK]
            ddt_u = ddt_u + red[CHUNK:2 * CHUNK]
            tail = _dot_sel(jnp.where(row8 == 0, rsum(x1) + x3, jnp.where(row8 == 1, rsum(dy * xs), 0.0)), indt)
            dcs_last = dcs_last + tail[0:1]
            d_skip = d_skip + tail[1:2]
            dxbc_ref[:, gl] = du * t["dtx"] + dy * dexp_ref[:, gl]
            dxbc_ref[:, d_inner + g * D_STATE: d_inner + (g + 1) * D_STATE] = db
            dxbc_ref[:, d_inner + (N_GROUPS + g) * D_STATE: d_inner + (N_GROUPS + g + 1) * D_STATE] = dc
        last = lax.broadcasted_iota(jnp.int32, (CHUNK, LANE), 0) == CHUNK - 1
        dcs = dcs + jnp.where(last, dcs_last, 0.0)
        da = _dot_sel(dcs, tril_ref[...], TN)
        ddt = da * a_head + ddt_u
        heads = lax.broadcasted_iota(jnp.int32, (CHUNK, LANE), 1) < d_inner // HEAD_DIM
        ddt_raw = jnp.where(heads, ddt * _sigmoid(pre), 0.0)
        ddt_ref[...] = ddt_raw.astype(BF16)
        dpar_ref[0:1, :] += rsum(ddt_raw)
        dpar_ref[1:2, :] += rsum(da * dt) * a_head
        dpar_ref[2:3, :] += d_skip

    rev = lambda c: (n_chunks - 1 - c, 0)
    fix = lambda c: (0, 0)
    cspec = lambda a: pl.BlockSpec(a.shape, fix)
    nx = xbc.shape[1]
    return pl.pallas_call(
        body, grid=(n_chunks,),
        in_specs=[pl.BlockSpec((CHUNK, d_inner), rev), pl.BlockSpec((CHUNK, d_inner), rev),
                  pl.BlockSpec((CHUNK, d_inner), rev), pl.BlockSpec((CHUNK, LANE), lambda c: (n_chunks - 1 - c, dt_blk)),
                  pl.BlockSpec((CHUNK, nx), rev),
                  pl.BlockSpec((1, N_GROUPS, D_STATE, GROUP_W), lambda c: (n_chunks - 1 - c, 0, 0, 0)),
                  cspec(par), cspec(dexp), cspec(nw), cspec(ind), cspec(ind_t), cspec(itile_c), cspec(trit_c),
                  cspec(tril_c), cspec(bmask_c)],
        out_specs=[pl.BlockSpec((CHUNK, d_inner), rev), pl.BlockSpec((CHUNK, nx), rev),
                   pl.BlockSpec((CHUNK, LANE), rev), pl.BlockSpec((1, d_inner), fix), pl.BlockSpec((8, LANE), fix)],
        out_shape=[SDS((L, d_inner), BF16), SDS((L, nx), F32), SDS((L, LANE), BF16), SDS((1, d_inner), F32),
                   SDS((8, LANE), F32)],
        scratch_shapes=[pltpu.VMEM((N_GROUPS, D_STATE, GROUP_W), F32), pltpu.VMEM((CHUNK, d_inner), F32)],
        name=name, compiler_params=_cp("arbitrary"))(
            dyn, yf, zx, zx, xbc, states, par, dexp, nw, ind, ind_t, itile_c, trit_c, tril_c, bmask_c)


def _adamw(w, m, v, g, name):
    R, C = w.shape
    tr = R
    for cand in (256, 128, 64, 32, 16, 8):
        if R % cand == 0:
            tr = cand
            break

    def body(w_ref, m_ref, v_ref, g_ref, d_ref, mo_ref, vo_ref):
        gv = g_ref[...]
        mn = ADAM_B1 * m_ref[...] + (1.0 - ADAM_B1) * gv
        vn = ADAM_B2 * v_ref[...] + (1.0 - ADAM_B2) * (gv * gv)
        m_hat = mn / (1.0 - ADAM_B1 ** ADAM_STEP)
        v_hat = vn / (1.0 - ADAM_B2 ** ADAM_STEP)
        d_ref[...] = -ADAM_LR * (m_hat / (jnp.sqrt(v_hat) + ADAM_EPS) + ADAM_WD * w_ref[...])
        mo_ref[...] = mn
        vo_ref[...] = vn

    blk = pl.BlockSpec((tr, C), lambda i: (i, 0))
    return pl.pallas_call(
        body, grid=(R // tr,), in_specs=[blk] * 4, out_specs=[blk] * 3, out_shape=[SDS((R, C), F32)] * 3,
        name=name, compiler_params=_cp("parallel"))(w, m, v, g)


def _sum_slots(parts, name):
    n, R, C = parts.shape
    tr = 128 if R % 128 == 0 else R

    def body(p_ref, o_ref):
        acc = p_ref[0]
        for k in range(1, n):
            acc = acc + p_ref[k]
        o_ref[...] = acc

    return pl.pallas_call(
        body, grid=(R // tr,), in_specs=[pl.BlockSpec((n, tr, C), lambda i: (0, i, 0))],
        out_specs=pl.BlockSpec((tr, C), lambda i: (i, 0)), out_shape=SDS((R, C), F32),
        name=name, compiler_params=_cp("parallel"))(parts)


def _add_core_halves(where, g, r, name):
    _, n0, a, b = g.shape
    h, ta = n0 // 2, a // 2

    def body(w_ref, g_ref, r_ref, o_ref):
        o_ref[...] = (g_ref[...].astype(F32) + r_ref[...].astype(F32)).astype(BF16)

    blk = lambda f: pl.BlockSpec((None, None, ta, b), f)
    mine = lambda s, l, i, w: (s, l, i, 0)
    return pl.pallas_call(
        body, grid_spec=pltpu.PrefetchScalarGridSpec(
            num_scalar_prefetch=1, grid=(4, h, 2),
            in_specs=[blk(lambda s, l, i, w: (s, w[1] * h + l, i, 0)), blk(mine)], out_specs=blk(mine)),
        out_shape=SDS((4, h, a, b), BF16), name=name,
        compiler_params=_cp("parallel", "parallel", "parallel"))(where, g, r)


def _sum_shard(where, g, r, rr, name):
    _, n0, a, b = g.shape
    h, ta = n0 // 2, a // 2

    def body(w_ref, g_ref, r_ref, rr_ref, o_ref):
        f = lambda v: v.astype(F32)
        o_ref[...] = (((f(g_ref[...]) + f(r_ref[...])) + f(rr_ref[0])) + f(rr_ref[1])) + f(rr_ref[2])

    return pl.pallas_call(
        body, grid_spec=pltpu.PrefetchScalarGridSpec(
            num_scalar_prefetch=1, grid=(h, 2),
            in_specs=[pl.BlockSpec((None, None, ta, b), lambda l, i, w: (w[0], w[1] * h + l, i, 0)),
                      pl.BlockSpec((None, None, ta, b), lambda l, i, w: (w[0], l, i, 0)),
                      pl.BlockSpec((3, None, ta, b), lambda l, i, w: (0, l, i, 0))],
            out_specs=pl.BlockSpec((None, ta, b), lambda l, i, w: (w[1] * h + l, i, 0))),
        out_shape=SDS((n0, a, b), F32), name=name,
        compiler_params=_cp("parallel", "parallel"))(where, g, r, rr)


def _me():
    return lax.axis_index("x"), lax.axis_index("y"), lax.axis_index("c")


def _chip_peers(x, y):
    return [(1 - x, y), (x, 1 - y), (1 - x, 1 - y)]


def _rcopy(src, dst, send_sems, recv_sems, k, to):
    return pltpu.make_async_remote_copy(src_ref=src, dst_ref=dst, send_sem=send_sems.at[k], recv_sem=recv_sems.at[k],
                                        device_id=to, device_id_type=MESH)


def _half(ref, c, lead=()):
    h = ref.shape[len(lead)] // 2
    return ref.at[(*lead, pl.ds(c * h, h))]


def _allgather_weights(mine, small):
    n = len(mine)

    def body(*refs):
        mine_refs, small_ref = refs[:n], refs[n]
        out_refs, sout_ref = refs[n + 1:2 * n + 1], refs[2 * n + 1]
        send_sems, recv_sems = refs[2 * n + 2:]
        x, y, c = _me()
        q = 2 * x + y
        sib = (x, y, 1 - c)
        chips = _chip_peers(x, y)
        sem = lambda t, k: 6 * t + k
        first = []
        for j, chip in enumerate(chips):
            for t in range(n):
                first.append(_rcopy(_half(mine_refs[t], c), _half(out_refs[t], c, (q,)), send_sems, recv_sems,
                                    sem(t, j), (*chip, c)))
            first.append(_rcopy(small_ref, sout_ref.at[q], send_sems, recv_sems, sem(n, j), (*chip, c)))
        for cp in first:
            cp.start()
        passed = []
        for j, (px, py) in enumerate(chips):
            for t in range(n):
                blk = _half(out_refs[t], c, (2 * px + py,))
                _rcopy(blk, blk, send_sems, recv_sems, sem(t, j), sib).wait_recv()
                cp = _rcopy(blk, blk, send_sems, recv_sems, sem(t, 3 + j), sib)
                cp.start()
                passed.append(cp)
        for j, (px, py) in enumerate(chips):
            for t in range(n):
                blk = _half(out_refs[t], 1 - c, (2 * px + py,))
                _rcopy(blk, blk, send_sems, recv_sems, sem(t, 3 + j), sib).wait_recv()
            sblk = sout_ref.at[2 * px + py]
            _rcopy(sblk, sblk, send_sems, recv_sems, sem(n, j), sib).wait_recv()
        for cp in first + passed:
            cp.wait_send()

    n_sems = 6 * n + 3
    return pl.pallas_call(
        body, in_specs=[ANY] * (n + 1), out_specs=[ANY] * (n + 1),
        out_shape=[SDS((4,) + m.shape, BF16) for m in mine] + [SDS((4,) + small.shape, F32)],
        scratch_shapes=[pltpu.SemaphoreType.DMA((n_sems,)), pltpu.SemaphoreType.DMA((n_sems,))],
        name="allgather_weights")(*mine, small)


def _grads_to_sibling(grads):
    n = len(grads)

    def body(*refs):
        g_refs, out_refs = refs[:n], refs[n:2 * n]
        send_sems, recv_sems = refs[2 * n:]
        x, y, c = _me()
        cps = [_rcopy(_half(g_refs[t], 1 - c, (slice(None),)), out_refs[t], send_sems, recv_sems, t, (x, y, 1 - c))
               for t in range(n)]
        for cp in cps:
            cp.start()
        for cp in cps:
            cp.wait()

    return pl.pallas_call(
        body, in_specs=[ANY] * n, out_specs=[ANY] * n,
        out_shape=[SDS((4, g.shape[1] // 2) + g.shape[2:], BF16) for g in grads],
        scratch_shapes=[pltpu.SemaphoreType.DMA((n,)), pltpu.SemaphoreType.DMA((n,))],
        name="grads_to_sibling")(*grads)


def _grads_to_chips(psums):
    n = len(psums)

    def body(*refs):
        p_refs, out_refs = refs[:n], refs[n:2 * n]
        send_sems, recv_sems = refs[2 * n:]
        x, y, c = _me()
        cps = [_rcopy(p_refs[t].at[2 * px + py], out_refs[t].at[j], send_sems, recv_sems, 3 * t + j, (px, py, c))
               for j, (px, py) in enumerate(_chip_peers(x, y)) for t in range(n)]
        for cp in cps:
            cp.start()
        for cp in cps:
            cp.wait()

    return pl.pallas_call(
        body, in_specs=[ANY] * n, out_specs=[ANY] * n,
        out_shape=[SDS((3,) + p.shape[1:], BF16) for p in psums],
        scratch_shapes=[pltpu.SemaphoreType.DMA((3 * n,)), pltpu.SemaphoreType.DMA((3 * n,))],
        name="grads_to_chips")(*psums)


def _swap_halves(sums):
    n = len(sums)

    def body(*refs):
        out_refs = refs[n:2 * n]
        send_sems, recv_sems = refs[2 * n:]
        x, y, c = _me()
        sib = (x, y, 1 - c)
        cps = [_rcopy(_half(out_refs[t], c), _half(out_refs[t], c), send_sems, recv_sems, t, sib) for t in range(n)]
        for cp in cps:
            cp.start()
        for t in range(n):
            other = _half(out_refs[t], 1 - c)
            _rcopy(other, other, send_sems, recv_sems, t, sib).wait_recv()
        for cp in cps:
            cp.wait_send()

    return pl.pallas_call(
        body, in_specs=[ANY] * n, out_specs=[ANY] * n, out_shape=[SDS(s.shape, F32) for s in sums],
        input_output_aliases={t: t for t in range(n)},
        scratch_shapes=[pltpu.SemaphoreType.DMA((n,)), pltpu.SemaphoreType.DMA((n,))],
        name="swap_halves")(*sums)


def _allgather_small(part):
    def body(p_ref, out_ref, send_sems, recv_sems, local_sem):
        x, y, c = _me()
        me = 4 * x + 2 * y + c
        own = pltpu.make_async_copy(p_ref, out_ref.at[me], local_sem.at[0])
        own.start()
        sends = []
        for k in range(1, 8):
            fx, fy, fc = (k >> 2) & 1, (k >> 1) & 1, k & 1
            to = (x ^ fx, y ^ fy, c ^ fc)
            sends.append(_rcopy(p_ref, out_ref.at[me], send_sems, recv_sems, k - 1, to))
        for cp in sends:
            cp.start()
        for k in range(1, 8):
            slot = out_ref.at[me ^ k]
            _rcopy(slot, slot, send_sems, recv_sems, k - 1, (x, y, c)).wait_recv()
        for cp in sends:
            cp.wait_send()
        own.wait()

    return pl.pallas_call(
        body, in_specs=[ANY], out_specs=ANY, out_shape=SDS((8,) + part.shape, F32),
        scratch_shapes=[pltpu.SemaphoreType.DMA((7,)), pltpu.SemaphoreType.DMA((7,)), pltpu.SemaphoreType.DMA((1,))],
        name="allgather_small")(part)


BIG = (("ssd_w_in", 2), ("ssd_w_out", 1), ("sc_w_in", 2), ("sc_w_out", 1), ("ffn_w_up", 2), ("ffn_w_down", 1))


def _to_shards(full, axis):
    A, B = full.shape
    if axis == 2:
        return full.reshape(A, 4, B // 4).transpose(1, 0, 2)
    return full.reshape(4, A // 4, B)


def _from_shards(shards, axis):
    _, a, b = shards.shape
    if axis == 2:
        return shards.transpose(1, 0, 2).reshape(a, 4 * b)
    return shards.reshape(4 * a, b)


def _interleave(w, parts):
    lead, n = w.shape[:-1], w.shape[-1]
    return w.reshape(*lead, parts, n // (parts * LANE), LANE).swapaxes(-2, -3).reshape(*lead, n)


def _deinterleave(w, parts):
    lead, n = w.shape[:-1], w.shape[-1]
    return w.reshape(*lead, n // (parts * LANE), parts, LANE).swapaxes(-2, -3).reshape(*lead, n)


def _pack_rows(vectors, width, row_multiple):
    flat = jnp.concatenate(vectors, axis=-1)
    n = flat.shape[-1]
    unit = width * row_multiple
    total = -(-n // unit) * unit
    flat = jnp.pad(flat, [(0, 0)] * (flat.ndim - 1) + [(0, total - n)])
    return flat.reshape(*flat.shape[:-1], total // width, width)


def _unpack(flat, shapes):
    out, off = [], 0
    for s in shapes:
        n = int(np.prod(s))
        out.append(flat[..., off:off + n].reshape(*flat.shape[:-1], *s))
        off += n
    return out


def _memo(fn):
    cache = {}

    def wrapped(k):
        if k not in cache:
            cache[k] = fn(k)
        return cache[k]

    return wrapped


def _row(v):
    return v.reshape(1, -1)


def _pad_rows(w, rows=8):
    return jnp.pad(w, ((0, rows - w.shape[0]), (0, 0)))


def _ffn_fwd(x, g_pre, g_post, w_up, cw, cb, w_down, tag):
    up, hn = _norm_matmul(x, g_pre, w_up, BF16, "ffn_up" + tag)
    a = _ffn_mid_fwd(up, cw, cb, "ffn_mid_fwd" + tag)
    f, x_new = _matmul_norm_res(a, w_down, x, g_post, "ffn_down" + tag)
    return x_new, (x, hn, up, a, f)


def _ffn_bwd(dx, saved, g_pre, g_post, w_up, cw, cb, w_down, tag):
    x, hn, up, a, f = saved
    df, dg_post = _postnorm_bwd(dx, f, g_post, "ffn_post_bwd" + tag)
    da = _matmul_nt(df, w_down, BF16, "ffn_down_dx" + tag)
    dw_down = _matmul_tn(a, df, w_down.shape[0] // 2, w_down.shape[1], "ffn_down_dw" + tag)
    dup, stats = _ffn_mid_bwd(da, up, cw, cb, "ffn_mid_bwd" + tag)
    dx_in, dg_pre = _matmul_nt_prenorm_bwd(dup, w_up, x, g_pre, dx, "ffn_up_dx" + tag)
    dw_up = _matmul_tn(hn, dup, w_up.shape[0], w_up.shape[1] // 4, "ffn_up_dw" + tag)
    return dx_in, dict(g_pre=dg_pre, g_post=dg_post, w_up=dw_up, w_down=dw_down, cw=stats[0:3], cb=stats[3])


def _sc_fwd(x, g_pre, g_post, w_in, cw, w_out, tag):
    bcv, hn = _norm_matmul(x, g_pre, w_in, BF16, "sc_in" + tag)
    q = _sc_mid_fwd(bcv, cw, "sc_mid_fwd" + tag)
    m, x_new = _matmul_norm_res(q, w_out, x, g_post, "sc_out" + tag)
    return x_new, (x, hn, bcv, q, m)


def _sc_bwd(dx, saved, g_pre, g_post, w_in, cw, w_out, tag):
    x, hn, bcv, q, m = saved
    dm, dg_post = _postnorm_bwd(dx, m, g_post, "sc_post_bwd" + tag)
    dq = _matmul_nt(dm, w_out, BF16, "sc_out_dx" + tag)
    dw_out = _matmul_tn(q, dm, w_out.shape[0], w_out.shape[1], "sc_out_dw" + tag)
    dbcv, stats = _sc_mid_bwd(dq, bcv, cw, "sc_mid_bwd" + tag)
    dx_in, dg_pre = _matmul_nt_prenorm_bwd(dbcv, w_in, x, g_pre, dx, "sc_in_dx" + tag)
    dw_in = _matmul_tn(hn, dbcv, w_in.shape[0], w_in.shape[1] // 3, "sc_in_dw" + tag)
    return dx_in, dict(g_pre=dg_pre, g_post=dg_post, w_in=dw_in, w_out=dw_out, cw=stats[0:3])


def _ssd_fwd(x, g_pre, g_post, w_in, cw, cb, par, dexp, nw, w_out, consts, tag):
    d_inner = dexp.shape[1]
    zx, hn = _norm_matmul(x, g_pre, w_in, F32, "ssd_in" + tag)
    xbc = _ssd_conv_fwd(zx, cw, cb, d_inner, cw.shape[1], "ssd_conv_fwd" + tag)
    yn, yf, states = _ssd_scan_fwd(zx, xbc, par, dexp, nw, consts, "ssd_scan_fwd" + tag)
    m, x_new = _matmul_norm_res(yn, w_out, x, g_post, "ssd_out" + tag)
    return x_new, (x, hn, zx, xbc, yn, yf, states, m)


def _ssd_bwd(dx, saved, g_pre, g_post, w_in, cw, cb, par, dexp, nw, w_out, consts, tag):
    x, hn, zx, xbc, yn, yf, states, m = saved
    d_inner = dexp.shape[1]
    dm, dg_post = _postnorm_bwd(dx, m, g_post, "ssd_post_bwd" + tag)
    dyn = _matmul_nt(dm, w_out, F32, "ssd_out_dx" + tag)
    dw_out = _matmul_tn(yn, dm, w_out.shape[0] // 2, w_out.shape[1], "ssd_out_dw" + tag)
    dz, dxbc, ddt, dnw, dpar = _ssd_scan_bwd(dyn, yf, zx, xbc, states, par, dexp, nw, consts, "ssd_scan_bwd" + tag)
    dxp, stats = _ssd_conv_bwd(dxbc, zx, cw, cb, d_inner, "ssd_conv_bwd" + tag)
    dzx = jnp.concatenate([dz, dxp, ddt], axis=1)
    dx_in, dg_pre = _matmul_nt_prenorm_bwd(dzx, w_in, x, g_pre, dx, "ssd_in_dx" + tag)
    dw_in = _matmul_tn(hn, dzx, w_in.shape[0], w_in.shape[1] // 7, "ssd_in_dw" + tag)
    n_heads = d_inner // HEAD_DIM
    return dx_in, dict(g_pre=dg_pre, g_post=dg_post, w_in=dw_in, w_out=dw_out, cw=stats[0:4], cb=stats[4],
                       dt_bias=dpar[0, :n_heads], a_log=dpar[1, :n_heads], d=dpar[2, :n_heads], nw=dnw[0])


def kernel(x, mix_pre_g, mix_post_g, ffn_pre_g, ffn_post_g, ssd_w_in, ssd_conv_w, ssd_conv_b, ssd_dt_bias, ssd_A_log, ssd_D, ssd_norm_w, ssd_w_out, sc_w_in, sc_conv_w, sc_w_out, ffn_w_up, ffn_conv_w, ffn_conv_b, ffn_w_down, loss_target, m_mix_pre_g, m_mix_post_g, m_ffn_pre_g, m_ffn_post_g, m_ssd_w_in, m_ssd_conv_w, m_ssd_conv_b, m_ssd_dt_bias, m_ssd_A_log, m_ssd_D, m_ssd_norm_w, m_ssd_w_out, m_sc_w_in, m_sc_conv_w, m_sc_w_out, m_ffn_w_up, m_ffn_conv_w, m_ffn_conv_b, m_ffn_w_down, v_mix_pre_g, v_mix_post_g, v_ffn_pre_g, v_ffn_post_g, v_ssd_w_in, v_ssd_conv_w, v_ssd_conv_b, v_ssd_dt_bias, v_ssd_A_log, v_ssd_D, v_ssd_norm_w, v_ssd_w_out, v_sc_w_in, v_sc_conv_w, v_sc_w_out, v_ffn_w_up, v_ffn_conv_w, v_ffn_conv_b, v_ffn_w_down):
    names = ["mix_pre_g", "mix_post_g", "ffn_pre_g", "ffn_post_g", "ssd_w_in", "ssd_conv_w", "ssd_conv_b",
             "ssd_dt_bias", "ssd_A_log", "ssd_D", "ssd_norm_w", "ssd_w_out", "sc_w_in", "sc_conv_w", "sc_w_out",
             "ffn_w_up", "ffn_conv_w", "ffn_conv_b", "ffn_w_down"]
    env = locals()
    wts = {n: env[n] for n in names}
    mom = {n: env["m_" + n] for n in names}
    var = {n: env["v_" + n] for n in names}

    depth, d_model = mix_pre_g.shape
    n_ssd, n_heads = ssd_dt_bias.shape
    n_sc = sc_conv_w.shape[0]
    d_inner = n_heads * HEAD_DIM
    conv_dim = d_inner + 2 * N_GROUPS * D_STATE
    ssd_in_dim = d_inner + conv_dim + n_heads
    ssd_in_pad = d_inner + conv_dim + LANE
    q_chip = 2 * lax.axis_index("x") + lax.axis_index("y")
    core = lax.axis_index("c")

    mine = [wts[n].astype(BF16) for n, _ in BIG]
    conv_names = ["ssd_conv_w", "sc_conv_w", "ffn_conv_w"]
    conv_shapes = [wts[n].shape for n in conv_names]
    small_mine = _pack_rows([wts[n].reshape(-1) for n in conv_names], LANE, 8)
    *gathered, small_all = _allgather_weights(mine, small_mine)
    own_slot = lambda buf, own: lax.dynamic_update_index_in_dim(buf, own, q_chip, 0)
    gathered = {n: own_slot(g, m) for (n, _), g, m in zip(BIG, gathered, mine)}
    axis_of = dict(BIG)

    def full(n, layer):
        return _from_shards(gathered[n][:, layer], axis_of[n])

    small_all = own_slot(small_all, small_mine)
    conv_full = {}
    for n, f, s in zip(conv_names, _unpack(small_all.reshape(4, -1), conv_shapes), conv_shapes):
        conv_full[n] = f.transpose(1, 2, 0, 3).reshape(s[0], s[1], 4 * s[2])

    consts = _scan_constants(n_heads)

    def ssd_args(j):
        par = jnp.zeros((8, LANE), F32).at[0, :n_heads].set(ssd_dt_bias[j]).at[1, :n_heads].set(ssd_A_log[j])
        dexp = jnp.repeat(ssd_D[j], HEAD_DIM).reshape(1, d_inner)
        w_in = jnp.pad(full("ssd_w_in", j), ((0, 0), (0, ssd_in_pad - ssd_in_dim)))
        return (w_in, _pad_rows(conv_full["ssd_conv_w"][j]), _row(ssd_conv_b[j]), par, dexp,
                _row(ssd_norm_w[j]), full("ssd_w_out", j), consts)

    def sc_args(j):
        return (_interleave(full("sc_w_in", j), 3), _pad_rows(conv_full["sc_conv_w"][j]), full("sc_w_out", j))

    def ffn_args(i):
        return (_interleave(full("ffn_w_up", i), 2), _pad_rows(conv_full["ffn_conv_w"][i]), _row(ffn_conv_b[i]),
                full("ffn_w_down", i))

    ssd_args, sc_args, ffn_args = _memo(ssd_args), _memo(sc_args), _memo(ffn_args)

    h = x[0]
    saved = []
    for i in range(depth):
        j = i // 2
        gp, gq = _row(mix_pre_g[i]), _row(mix_post_g[i])
        if i % 2 == 0:
            h, sv = _ssd_fwd(h, gp, gq, *ssd_args(j), tag="")
        else:
            h, sv = _sc_fwd(h, gp, gq, *sc_args(j), tag="")
        w_up, cw, cb, w_down = ffn_args(i)
        h, sv2 = _ffn_fwd(h, _row(ffn_pre_g[i]), _row(ffn_post_g[i]), w_up, cw, cb, w_down, tag="")
        saved.append((sv, sv2))
    dh, loss_part = _loss_head(h, loss_target[0], "loss_head")

    mix_grads, ffn_grads = [None] * depth, [None] * depth
    for i in reversed(range(depth)):
        j = i // 2
        sv, sv2 = saved[i]
        w_up, cw, cb, w_down = ffn_args(i)
        dh, ffn_grads[i] = _ffn_bwd(dh, sv2, _row(ffn_pre_g[i]), _row(ffn_post_g[i]), w_up, cw, cb, w_down, tag="")
        gp, gq = _row(mix_pre_g[i]), _row(mix_post_g[i])
        if i % 2 == 0:
            w_in, scw, scb, par, dexp, nw, w_out, _ = ssd_args(j)
            dh, mix_grads[i] = _ssd_bwd(dh, sv, gp, gq, w_in, scw, scb, par, dexp, nw, w_out, consts, tag="")
        else:
            w_in, scw, w_out = sc_args(j)
            dh, mix_grads[i] = _sc_bwd(dh, sv, gp, gq, w_in, scw, w_out, tag="")
    grad_x = dh[None]

    ssd_l = [mix_grads[i] for i in range(0, depth, 2)]
    sc_l = [mix_grads[i] for i in range(1, depth, 2)]
    stack = lambda layers, k: jnp.stack([g[k] for g in layers])
    local = {
        "ssd_w_in": [g["w_in"][:, :ssd_in_dim] for g in ssd_l],
        "ssd_w_out": [g["w_out"] for g in ssd_l],
        "sc_w_in": [_deinterleave(g["w_in"], 3) for g in sc_l],
        "sc_w_out": [g["w_out"] for g in sc_l],
        "ffn_w_up": [_deinterleave(g["w_up"], 2) for g in ffn_grads],
        "ffn_w_down": [g["w_down"] for g in ffn_grads],
    }
    by_shard = [jnp.stack([_to_shards(dw, ax).astype(BF16) for dw in local[n]], axis=1) for n, ax in BIG]
    from_sib = _grads_to_sibling(by_shard)
    where = jnp.stack([q_chip, core]).astype(jnp.int32)
    chip_sums = [_add_core_halves(where, g, r, "add_core_halves_" + n) for (n, _), g, r in zip(BIG, by_shard, from_sib)]
    from_chips = _grads_to_chips(chip_sums)
    half_sums = [_sum_shard(where, g, r, rr, "sum_shard_" + n)
                 for (n, _), g, r, rr in zip(BIG, by_shard, from_sib, from_chips)]
    big_grads = dict(zip([n for n, _ in BIG], _swap_halves(half_sums)))

    small_names = ["mix_pre_g", "mix_post_g", "ffn_pre_g", "ffn_post_g", "ssd_conv_w", "ssd_conv_b", "ssd_dt_bias",
                   "ssd_A_log", "ssd_D", "ssd_norm_w", "sc_conv_w", "ffn_conv_w", "ffn_conv_b"]
    small_local = {
        "mix_pre_g": jnp.concatenate([g["g_pre"] for g in mix_grads]),
        "mix_post_g": jnp.concatenate([g["g_post"] for g in mix_grads]),
        "ffn_pre_g": jnp.concatenate([g["g_pre"] for g in ffn_grads]),
        "ffn_post_g": jnp.concatenate([g["g_post"] for g in ffn_grads]),
        "ssd_conv_w": stack(ssd_l, "cw"), "ssd_conv_b": stack(ssd_l, "cb"), "ssd_dt_bias": stack(ssd_l, "dt_bias"),
        "ssd_A_log": stack(ssd_l, "a_log"), "ssd_D": stack(ssd_l, "d"), "ssd_norm_w": stack(ssd_l, "nw"),
        "sc_conv_w": stack(sc_l, "cw"), "ffn_conv_w": stack(ffn_grads, "cw"), "ffn_conv_b": stack(ffn_grads, "cb"),
    }
    small_full_shapes = [small_local[n].shape for n in small_names]
    spack = _pack_rows([small_local[n].reshape(-1) for n in small_names] + [loss_part.reshape(-1)], LANE, 8)
    stotal = _sum_slots(_allgather_small(spack), "sum_small").reshape(-1)
    small_grads = dict(zip(small_names, _unpack(stotal, small_full_shapes)))
    loss = stotal[sum(int(np.prod(s)) for s in small_full_shapes)]
    for n in conv_names:
        width = wts[n].shape[-1]
        small_grads[n] = lax.dynamic_slice_in_dim(small_grads[n], q_chip * width, width, axis=2)

    grads, delta, new_m, new_v = {}, {}, {}, {}
    for n, _ in BIG:
        s = wts[n].shape
        two_d = lambda a: a.reshape(-1, s[-1])
        grads[n] = big_grads[n]
        d, mn, vn = _adamw(two_d(wts[n]), two_d(mom[n]), two_d(var[n]), two_d(grads[n]), "adamw_" + n)
        delta[n], new_m[n], new_v[n] = d.reshape(s), mn.reshape(s), vn.reshape(s)
    small_shapes = [wts[n].shape for n in small_names]
    pk = lambda d: _pack_rows([d[n].reshape(-1) for n in small_names], LANE, 8)
    for n in small_names:
        grads[n] = small_grads[n].reshape(wts[n].shape)
    d, mn, vn = _adamw(pk(wts), pk(mom), pk(var), pk(grads), "adamw_small")
    for out, packed in ((delta, d), (new_m, mn), (new_v, vn)):
        out.update(zip(small_names, _unpack(packed.reshape(-1), small_shapes)))

    return (loss, grad_x, *[grads[n] for n in names], *[delta[n] for n in names], *[new_m[n] for n in names],
            *[new_v[n] for n in names])
```

```python
from typing import Callable, NamedTuple

import jax
import jax.numpy as jnp
import numpy as np
from jax import lax
from jax.experimental import pallas as pl
from jax.experimental.pallas import tpu as pltpu

F32 = jnp.float32
BF16 = jnp.bfloat16
SDS = jax.ShapeDtypeStruct
MESH = pl.DeviceIdType.MESH
ANY = pl.BlockSpec(memory_space=pl.ANY)

EPS = 1e-6
CHUNK = 64
HEAD_DIM = 64
N_GROUPS = 8
D_STATE = 128
HEADS_PER_GROUP = 4
GROUP_W = HEADS_PER_GROUP * HEAD_DIM
LANE = 128
ROW_TILE = 128
HALO = 8
VMEM_LIMIT = 56 * 1024 * 1024

ADAM_LR = 0.001
ADAM_B1 = 0.9
ADAM_B2 = 0.999
ADAM_EPS = 1e-08
ADAM_WD = 0.01
ADAM_STEP = 10

NT = (((1,), (1,)), ((), ()))
TN = (((0,), (0,)), ((), ()))


def _cp(*sem):
    return pltpu.CompilerParams(dimension_semantics=sem or None, vmem_limit_bytes=VMEM_LIMIT)


def _sigmoid(x):
    return 1.0 / (1.0 + jnp.exp(-x))


def _dsilu(x, s):
    return s * (1.0 + x * (1.0 - s))


def _rsq(x):
    return lax.rsqrt(jnp.mean(x * x, axis=-1, keepdims=True) + EPS)


MM_ROWS = 256


def _norm_matmul(x, g, w, out_dtype, name):
    L, D = x.shape
    N = w.shape[1]
    tm = min(MM_ROWS, L)

    def body(x_ref, g_ref, w_ref, o_ref, hn_ref):
        xv = x_ref[...]
        hn = (xv * _rsq(xv) * g_ref[...]).astype(BF16)
        hn_ref[...] = hn
        o_ref[...] = jnp.dot(hn, w_ref[...], preferred_element_type=F32).astype(out_dtype)

    row = lambda i: (i, 0)
    fix = lambda i: (0, 0)
    return pl.pallas_call(
        body, grid=(L // tm,),
        in_specs=[pl.BlockSpec((tm, D), row), pl.BlockSpec((1, D), fix), pl.BlockSpec((D, N), fix)],
        out_specs=[pl.BlockSpec((tm, N), row), pl.BlockSpec((tm, D), row)],
        out_shape=[SDS((L, N), out_dtype), SDS((L, D), BF16)],
        name=name, compiler_params=_cp("parallel"))(x, g, w)


def _matmul_norm_res(a, w, x, g, name):
    L, K = a.shape
    D = w.shape[1]
    tm = min(256, L)

    def body(a_ref, w_ref, x_ref, g_ref, m_ref, xo_ref):
        m = jnp.dot(a_ref[...], w_ref[...], preferred_element_type=F32)
        m_ref[...] = m
        xo_ref[...] = x_ref[...] + m * _rsq(m) * g_ref[...]

    row = lambda i: (i, 0)
    fix = lambda i: (0, 0)
    return pl.pallas_call(
        body, grid=(L // tm,),
        in_specs=[pl.BlockSpec((tm, K), row), pl.BlockSpec((K, D), fix), pl.BlockSpec((tm, D), row),
                  pl.BlockSpec((1, D), fix)],
        out_specs=[pl.BlockSpec((tm, D), row), pl.BlockSpec((tm, D), row)],
        out_shape=[SDS((L, D), F32), SDS((L, D), F32)],
        name=name, compiler_params=_cp("parallel"))(a, w, x, g)


def _postnorm_bwd(dx, m, g, name):
    L, D = dx.shape
    tm = min(512, L)

    def body(dx_ref, m_ref, g_ref, dm_ref, dg_ref):
        @pl.when(pl.program_id(0) == 0)
        def _():
            dg_ref[...] = jnp.zeros_like(dg_ref)

        m = m_ref[...]
        dxv = dx_ref[...]
        r = _rsq(m)
        mh = m * r
        dg_ref[...] += jnp.sum(dxv * mh, axis=0, keepdims=True)
        dyg = dxv * g_ref[...]
        dm_ref[...] = (r * (dyg - mh * jnp.mean(dyg * mh, axis=-1, keepdims=True))).astype(BF16)

    row = lambda i: (i, 0)
    fix = lambda i: (0, 0)
    return pl.pallas_call(
        body, grid=(L // tm,),
        in_specs=[pl.BlockSpec((tm, D), row), pl.BlockSpec((tm, D), row), pl.BlockSpec((1, D), fix)],
        out_specs=[pl.BlockSpec((tm, D), row), pl.BlockSpec((1, D), fix)],
        out_shape=[SDS((L, D), BF16), SDS((1, D), F32)],
        name=name, compiler_params=_cp("arbitrary"))(dx, m, g)


def _matmul_nt(a, w, out_dtype, name):
    L, D = a.shape
    K = w.shape[0]
    tm = min(MM_ROWS, L)

    def body(a_ref, w_ref, o_ref):
        o_ref[...] = lax.dot_general(a_ref[...], w_ref[...], NT, preferred_element_type=F32).astype(out_dtype)

    return pl.pallas_call(
        body, grid=(L // tm,),
        in_specs=[pl.BlockSpec((tm, D), lambda i: (i, 0)), pl.BlockSpec((K, D), lambda i: (0, 0))],
        out_specs=pl.BlockSpec((tm, K), lambda i: (i, 0)),
        out_shape=SDS((L, K), out_dtype),
        name=name, compiler_params=_cp("parallel"))(a, w)


def _matmul_tn(a, b, ta, tn, name):
    L, Ka = a.shape
    N = b.shape[1]
    tl = min(512, L)
    n_l = L // tl

    def body(a_ref, b_ref, o_ref, acc_ref):
        l = pl.program_id(2)

        @pl.when(l == 0)
        def _():
            acc_ref[...] = jnp.zeros_like(acc_ref)

        acc_ref[...] += lax.dot_general(a_ref[...], b_ref[...], TN, preferred_element_type=F32)

        @pl.when(l == n_l - 1)
        def _():
            o_ref[...] = acc_ref[...]

    return pl.pallas_call(
        body, grid=(Ka // ta, N // tn, n_l),
        in_specs=[pl.BlockSpec((tl, ta), lambda i, j, l: (l, i)), pl.BlockSpec((tl, tn), lambda i, j, l: (l, j))],
        out_specs=pl.BlockSpec((ta, tn), lambda i, j, l: (i, j)),
        out_shape=SDS((Ka, N), F32),
        scratch_shapes=[pltpu.VMEM((ta, tn), F32)],
        name=name, compiler_params=_cp("parallel", "parallel", "arbitrary"))(a, b)


def _matmul_nt_prenorm_bwd(dy, w, x, g, dres, name):
    L, N = dy.shape
    D = w.shape[0]
    tm = min(MM_ROWS, L)

    def body(dy_ref, w_ref, x_ref, g_ref, dres_ref, dx_ref, dg_ref):
        @pl.when(pl.program_id(0) == 0)
        def _():
            dg_ref[...] = jnp.zeros_like(dg_ref)

        dh = lax.dot_general(dy_ref[...], w_ref[...], NT, preferred_element_type=F32)
        xv = x_ref[...]
        r = _rsq(xv)
        xh = xv * r
        dg_ref[...] += jnp.sum(dh * xh, axis=0, keepdims=True)
        dyg = dh * g_ref[...]
        dx_ref[...] = dres_ref[...] + r * (dyg - xh * jnp.mean(dyg * xh, axis=-1, keepdims=True))

    row = lambda i: (i, 0)
    fix = lambda i: (0, 0)
    return pl.pallas_call(
        body, grid=(L // tm,),
        in_specs=[pl.BlockSpec((tm, N), row), pl.BlockSpec((D, N), fix), pl.BlockSpec((tm, D), row),
                  pl.BlockSpec((1, D), fix), pl.BlockSpec((tm, D), row)],
        out_specs=[pl.BlockSpec((tm, D), row), pl.BlockSpec((1, D), fix)],
        out_shape=[SDS((L, D), F32), SDS((1, D), F32)],
        name=name, compiler_params=_cp("arbitrary"))(dy, w, x, g, dres)


def _loss_head(y, t, name):
    L, D = y.shape
    tm = min(512, L)

    def body(y_ref, t_ref, dy_ref, loss_ref):
        @pl.when(pl.program_id(0) == 0)
        def _():
            loss_ref[...] = jnp.zeros_like(loss_ref)

        e = y_ref[...] - t_ref[...]
        dy_ref[...] = e * (1.0 / D)
        s = jnp.sum(jnp.sum(e * e, axis=1, keepdims=True), axis=0, keepdims=True)
        loss_ref[...] += s * (0.5 / D)

    row = lambda i: (i, 0)
    return pl.pallas_call(
        body, grid=(L // tm,),
        in_specs=[pl.BlockSpec((tm, D), row), pl.BlockSpec((tm, D), row)],
        out_specs=[pl.BlockSpec((tm, D), row), pl.BlockSpec((1, 1), lambda i: (0, 0))],
        out_shape=[SDS((L, D), F32), SDS((1, 1), F32)],
        name=name, compiler_params=_cp("arbitrary"))(y, t)


def _tile_rows(ref):
    return HALO * (4 // jnp.dtype(ref.dtype).itemsize)


def _prev_rows(ref, r0, i, cols):
    n = _tile_rows(ref)
    p0 = pl.multiple_of(jnp.maximum(r0 - n, 0), n)
    return jnp.where(i > 0, ref[pl.ds(p0, n), cols].astype(F32)[n - HALO:], 0.0)


def _next_rows(ref, r0, i, n_tiles, cols):
    n = _tile_rows(ref)
    n0 = pl.multiple_of(jnp.minimum(r0 + ROW_TILE, n_tiles * ROW_TILE - n), n)
    return jnp.where(i < n_tiles - 1, ref[pl.ds(n0, n), cols].astype(F32)[:HALO], 0.0)


def _rows_f32(ref, rows, cols):
    return ref[rows, cols].astype(F32)


def _back(ext, s):
    return pltpu.roll(ext, s, axis=0)[HALO:HALO + ROW_TILE]


def _fwd(ext, s):
    n = ext.shape[0]
    return pltpu.roll(ext, n - s, axis=0)[:ROW_TILE]


def _store_rows(ref, rows):
    ref[...] = jnp.zeros_like(ref)
    for k, v in enumerate(rows):
        ref[k:k + 1, :] = v


def _strip_call(body, L, n_strips, ins, outs, name):
    def spec(rows, width, off):
        if off is None:
            return pl.BlockSpec((rows, width), lambda j: (0, 0))
        return pl.BlockSpec((rows, width), lambda j: (0, j + off))

    return pl.pallas_call(
        body, grid=(n_strips,),
        in_specs=[spec(a.shape[0], w, off) for a, w, off in ins],
        out_specs=[spec(s.shape[0], w, off) for s, w, off in outs],
        out_shape=[s for s, _, _ in outs],
        name=name, compiler_params=_cp("parallel"))(*[a for a, _, _ in ins])


def _ffn_mid_fwd(up, cw, cb, name):
    L = up.shape[0]
    C = up.shape[1] // 2
    n_tiles = L // ROW_TILE
    lo, hi = slice(0, LANE), slice(LANE, 2 * LANE)

    def body(up_ref, cw_ref, cb_ref, a_ref):
        w0, w1, w2 = cw_ref[0:1, :], cw_ref[1:2, :], cw_ref[2:3, :]
        b = cb_ref[...]

        def step(i, c):
            r0 = pl.multiple_of(i * ROW_TILE, ROW_TILE)
            rows = pl.ds(r0, ROW_TILE)
            gp = _rows_f32(up_ref, rows, lo)
            ext = jnp.concatenate([_prev_rows(up_ref, r0, i, lo), gp], axis=0)
            gate = gp * w2 + _back(ext, 1) * w1 + _back(ext, 2) * w0 + b
            a_ref[rows, :] = (gate * _sigmoid(gate) * _rows_f32(up_ref, rows, hi)).astype(BF16)
            return c

        lax.fori_loop(0, n_tiles, step, 0)

    return _strip_call(body, L, C // LANE,
                       [(up, 2 * LANE, 0), (cw, LANE, 0), (cb, LANE, 0)],
                       [(SDS((L, C), BF16), LANE, 0)], name)[0]


def _ffn_mid_bwd(da, up, cw, cb, name):
    L, C = da.shape
    n_tiles = L // ROW_TILE
    lo, hi, al = slice(0, LANE), slice(LANE, 2 * LANE), slice(None)

    def body(da_ref, up_ref, cw_ref, cb_ref, dup_ref, st_ref):
        w0, w1, w2 = cw_ref[0:1, :], cw_ref[1:2, :], cw_ref[2:3, :]
        b = cb_ref[...]

        def step(i, c):
            r0 = pl.multiple_of(i * ROW_TILE, ROW_TILE)
            rows = pl.ds(r0, ROW_TILE)
            gp = _rows_f32(up_ref, rows, lo)
            gpe = jnp.concatenate([_prev_rows(up_ref, r0, i, lo), gp, _next_rows(up_ref, r0, i, n_tiles, lo)], axis=0)
            g1, g2 = pltpu.roll(gpe, 1, axis=0), pltpu.roll(gpe, 2, axis=0)
            gate = (gpe * w2 + g1 * w1 + g2 * w0 + b)[HALO:]
            sg = _sigmoid(gate)
            da_e = jnp.concatenate([_rows_f32(da_ref, rows, al), _next_rows(da_ref, r0, i, n_tiles, al)], axis=0)
            val_e = jnp.concatenate([_rows_f32(up_ref, rows, hi), _next_rows(up_ref, r0, i, n_tiles, hi)], axis=0)
            dgate_e = da_e * val_e * _dsilu(gate, sg)
            dgate = dgate_e[:ROW_TILE]
            dgp = dgate * w2 + _fwd(dgate_e, 1) * w1 + _fwd(dgate_e, 2) * w0
            dup_ref[rows, lo] = dgp.astype(BF16)
            dup_ref[rows, hi] = (da_e * gate * sg)[:ROW_TILE].astype(BF16)
            s = lambda v: jnp.sum(v, axis=0, keepdims=True)
            t = slice(HALO, HALO + ROW_TILE)
            return (c[0] + s(dgate * g2[t]), c[1] + s(dgate * g1[t]), c[2] + s(dgate * gp), c[3] + s(dgate))

        z = jnp.zeros((1, LANE), F32)
        _store_rows(st_ref, lax.fori_loop(0, n_tiles, step, (z, z, z, z)))

    return _strip_call(body, L, C // LANE,
                       [(da, LANE, 0), (up, 2 * LANE, 0), (cw, LANE, 0), (cb, LANE, 0)],
                       [(SDS((L, 2 * C), BF16), 2 * LANE, 0), (SDS((8, C), F32), LANE, 0)], name)


def _sc_mid_fwd(bcv, cw, name):
    L = bcv.shape[0]
    C = bcv.shape[1] // 3
    n_tiles = L // ROW_TILE
    s0, s1, s2 = slice(0, LANE), slice(LANE, 2 * LANE), slice(2 * LANE, 3 * LANE)

    def body(x_ref, cw_ref, q_ref):
        w0, w1, w2 = cw_ref[0:1, :], cw_ref[1:2, :], cw_ref[2:3, :]

        def step(i, c):
            r0 = pl.multiple_of(i * ROW_TILE, ROW_TILE)
            rows = pl.ds(r0, ROW_TILE)
            p = _rows_f32(x_ref, rows, s1) * _rows_f32(x_ref, rows, s2)
            ext = jnp.concatenate([_prev_rows(x_ref, r0, i, s1) * _prev_rows(x_ref, r0, i, s2), p], axis=0)
            u = p * w2 + _back(ext, 1) * w1 + _back(ext, 2) * w0
            q_ref[rows, :] = (_rows_f32(x_ref, rows, s0) * u).astype(BF16)
            return c

        lax.fori_loop(0, n_tiles, step, 0)

    return _strip_call(body, L, C // LANE, [(bcv, 3 * LANE, 0), (cw, LANE, 0)],
                       [(SDS((L, C), BF16), LANE, 0)], name)[0]


def _sc_mid_bwd(dq, bcv, cw, name):
    L, C = dq.shape
    n_tiles = L // ROW_TILE
    s0, s1, s2, al = slice(0, LANE), slice(LANE, 2 * LANE), slice(2 * LANE, 3 * LANE), slice(None)

    def body(dq_ref, x_ref, cw_ref, dx_ref, st_ref):
        w0, w1, w2 = cw_ref[0:1, :], cw_ref[1:2, :], cw_ref[2:3, :]

        def step(i, c):
            r0 = pl.multiple_of(i * ROW_TILE, ROW_TILE)
            rows = pl.ds(r0, ROW_TILE)
            gb, gc, v = _rows_f32(x_ref, rows, s0), _rows_f32(x_ref, rows, s1), _rows_f32(x_ref, rows, s2)
            dq_v = _rows_f32(dq_ref, rows, al)
            p = gc * v
            pext = jnp.concatenate([_prev_rows(x_ref, r0, i, s1) * _prev_rows(x_ref, r0, i, s2), p], axis=0)
            p1, p2 = _back(pext, 1), _back(pext, 2)
            u = p * w2 + p1 * w1 + p2 * w0
            du = dq_v * gb
            du_n = _next_rows(dq_ref, r0, i, n_tiles, al) * _next_rows(x_ref, r0, i, n_tiles, s0)
            ext = jnp.concatenate([du, du_n], axis=0)
            dp = du * w2 + _fwd(ext, 1) * w1 + _fwd(ext, 2) * w0
            dx_ref[rows, s0] = (dq_v * u).astype(BF16)
            dx_ref[rows, s1] = (dp * v).astype(BF16)
            dx_ref[rows, s2] = (dp * gc).astype(BF16)
            s = lambda t: jnp.sum(t, axis=0, keepdims=True)
            return (c[0] + s(du * p2), c[1] + s(du * p1), c[2] + s(du * p))

        z = jnp.zeros((1, LANE), F32)
        _store_rows(st_ref, lax.fori_loop(0, n_tiles, step, (z, z, z)))

    return _strip_call(body, L, C // LANE, [(dq, LANE, 0), (bcv, 3 * LANE, 0), (cw, LANE, 0)],
                       [(SDS((L, 3 * C), BF16), 3 * LANE, 0), (SDS((8, C), F32), LANE, 0)], name)


def _ssd_conv_fwd(zx, cw, cb, col0, C, name):
    L = zx.shape[0]
    n_tiles = L // ROW_TILE
    al = slice(None)

    def body(x_ref, cw_ref, cb_ref, o_ref):
        w0, w1, w2, w3 = cw_ref[0:1, :], cw_ref[1:2, :], cw_ref[2:3, :], cw_ref[3:4, :]
        b = cb_ref[...]

        def step(i, c):
            r0 = pl.multiple_of(i * ROW_TILE, ROW_TILE)
            rows = pl.ds(r0, ROW_TILE)
            xv = x_ref[rows, :]
            ext = jnp.concatenate([_prev_rows(x_ref, r0, i, al), xv], axis=0)
            cv = xv * w3 + _back(ext, 1) * w2 + _back(ext, 2) * w1 + _back(ext, 3) * w0 + b
            o_ref[rows, :] = cv * _sigmoid(cv)
            return c

        lax.fori_loop(0, n_tiles, step, 0)

    return _strip_call(body, L, C // LANE, [(zx, LANE, col0 // LANE), (cw, LANE, 0), (cb, LANE, 0)],
                       [(SDS((L, C), F32), LANE, 0)], name)[0]


def _ssd_conv_bwd(dxbc, zx, cw, cb, col0, name):
    L, C = dxbc.shape
    n_tiles = L // ROW_TILE
    al = slice(None)

    def body(d_ref, x_ref, cw_ref, cb_ref, o_ref, st_ref):
        w0, w1, w2, w3 = cw_ref[0:1, :], cw_ref[1:2, :], cw_ref[2:3, :], cw_ref[3:4, :]
        b = cb_ref[...]

        def step(i, c):
            r0 = pl.multiple_of(i * ROW_TILE, ROW_TILE)
            rows = pl.ds(r0, ROW_TILE)
            xv = x_ref[rows, :]
            xe = jnp.concatenate([_prev_rows(x_ref, r0, i, al), xv, _next_rows(x_ref, r0, i, n_tiles, al)], axis=0)
            x1, x2, x3 = pltpu.roll(xe, 1, axis=0), pltpu.roll(xe, 2, axis=0), pltpu.roll(xe, 3, axis=0)
            cv = (xe * w3 + x1 * w2 + x2 * w1 + x3 * w0 + b)[HALO:]
            de = jnp.concatenate([d_ref[rows, :], _next_rows(d_ref, r0, i, n_tiles, al)], axis=0)
            dc_ext = de * _dsilu(cv, _sigmoid(cv))
            dc = dc_ext[:ROW_TILE]
            o_ref[rows, :] = (dc * w3 + _fwd(dc_ext, 1) * w2 + _fwd(dc_ext, 2) * w1 + _fwd(dc_ext, 3) * w0).astype(BF16)
            s = lambda t: jnp.sum(t, axis=0, keepdims=True)
            t = slice(HALO, HALO + ROW_TILE)
            return (c[0] + s(dc * x3[t]), c[1] + s(dc * x2[t]), c[2] + s(dc * x1[t]), c[3] + s(dc * xv), c[4] + s(dc))

        z = jnp.zeros((1, LANE), F32)
        _store_rows(st_ref, lax.fori_loop(0, n_tiles, step, (z, z, z, z, z)))

    return _strip_call(body, L, C // LANE,
                       [(dxbc, LANE, 0), (zx, LANE, col0 // LANE), (cw, LANE, 0), (cb, LANE, 0)],
                       [(SDS((L, C), BF16), LANE, 0), (SDS((8, C), F32), LANE, 0)], name)


def _scan_constants(n_heads):
    hw = n_heads * HEAD_DIM
    col = np.arange(hw)
    ind = (col[None, :] // HEAD_DIM == np.arange(LANE)[:, None]).astype(np.float32)
    gcol = np.arange(GROUP_W)
    itile = (gcol[None, :] % CHUNK == np.arange(CHUNK)[:, None]).astype(np.float32)
    trit = (gcol[None, :] % CHUNK <= np.arange(CHUNK)[:, None]).astype(np.float32)
    tril = np.tril(np.ones((CHUNK, CHUNK), np.float32))
    bmask = (gcol[:, None] // HEAD_DIM == gcol[None, :] // HEAD_DIM).astype(np.float32)
    return (jnp.asarray(ind, BF16), jnp.asarray(ind.T.copy(), BF16), jnp.asarray(itile), jnp.asarray(trit),
            jnp.asarray(tril, BF16), jnp.asarray(bmask))


def _softplus(x):
    return jnp.maximum(x, 0.0) + jnp.log(1.0 + jnp.exp(-jnp.abs(x)))


def _split3(x):
    hi = x.astype(BF16)
    r1 = x - hi.astype(F32)
    mid = r1.astype(BF16)
    return hi, mid, (r1 - mid.astype(F32)).astype(BF16)


def _dot_sel(x, sel, dims=None):
    if dims is None:
        mm = lambda p: jnp.dot(p, sel, preferred_element_type=F32)
    else:
        mm = lambda p: lax.dot_general(sel, p, dims, preferred_element_type=F32)
    hi, mid, lo = _split3(x)
    return (mm(lo) + mm(mid)) + mm(hi)


SEL_X = (((1,), (0,)), ((), ()))


def _group_terms(g, dt, cs, xbc_ref, ind_ref, itile, trit, bmask, d_inner):
    gl = slice(g * GROUP_W, (g + 1) * GROUP_W)
    indg = ind_ref[:, gl]
    csl = _dot_sel(cs, indg)
    dtx = _dot_sel(dt, indg)
    rr = _dot_sel(csl * itile, jnp.ones((CHUNK, CHUNK), BF16), SEL_X)
    lm = jnp.exp(jnp.where(trit > 0.0, csl - rr, -jnp.inf))
    xs = xbc_ref[:, gl]
    b = xbc_ref[:, d_inner + g * D_STATE: d_inner + (g + 1) * D_STATE]
    c = xbc_ref[:, d_inner + (N_GROUPS + g) * D_STATE: d_inner + (N_GROUPS + g + 1) * D_STATE]
    u = xs * dtx
    bb, cb = b.astype(BF16), c.astype(BF16)
    btile = jnp.concatenate([bb] * HEADS_PER_GROUP, axis=0)
    cbt = lax.dot_general(cb, btile, NT, preferred_element_type=F32)
    m = cbt * lm
    ub = u.astype(BF16)
    bdu = jnp.where(bmask > 0.0, jnp.concatenate([ub] * HEADS_PER_GROUP, axis=0), jnp.zeros((), BF16))
    c_last = csl[CHUNK - 1:CHUNK, :]
    return dict(gl=gl, indg=indg, csl=csl, dtx=dtx, lm=lm, xs=xs, bb=bb, cb=cb, u=u, btile=btile, m=m, bdu=bdu,
                e=jnp.exp(csl), dec=jnp.exp(c_last - csl), e_last=jnp.exp(c_last))


def _ssd_scan_fwd(zx, xbc, par, dexp, nw, consts, comm, name):
    L = xbc.shape[0]
    d_inner = dexp.shape[1]
    n_chunks = L // CHUNK
    dt_blk = zx.shape[1] // LANE - 1
    ind, ind_t, itile_c, trit_c, tril_c, bmask_c = consts

    def body(xbc_ref, z_ref, dtr_ref, par_ref, dexp_ref, nw_ref, ind_ref, itile_ref, trit_ref, tril_ref, bmask_ref,
             yn_ref, yf_ref, st_out_ref, st_ref):
        @pl.when(pl.program_id(0) == 0)
        def _():
            st_ref[...] = jnp.zeros_like(st_ref)

        dt = _softplus(dtr_ref[...] + par_ref[0:1, :])
        a_head = -jnp.exp(par_ref[1:2, :])
        cs = _dot_sel(dt * a_head, tril_ref[...], SEL_X)
        itile, trit, bmask = itile_ref[...], trit_ref[...], bmask_ref[...]
        for g in range(N_GROUPS):
            t = _group_terms(g, dt, cs, xbc_ref, ind_ref, itile, trit, bmask, d_inner)
            p = st_ref[g]
            st_out_ref[0, g] = p
            y = jnp.dot(t["m"].astype(BF16), t["bdu"], preferred_element_type=F32)
            y = y + jnp.dot(t["cb"], p.astype(BF16), preferred_element_type=F32) * t["e"]
            st_new = lax.dot_general(t["bb"], (t["u"] * t["dec"]).astype(BF16), TN, preferred_element_type=F32)
            st_ref[g] = p * t["e_last"] + st_new
            yf_ref[:, t["gl"]] = y + t["xs"] * dexp_ref[:, t["gl"]]
        z = z_ref[...]
        y2 = yf_ref[...] * (z * _sigmoid(z))
        yn_ref[...] = (y2 * _rsq(y2) * nw_ref[...]).astype(BF16)

    row = lambda c: (c, 0)
    fix = lambda c: (0, 0)
    cspec = lambda a: pl.BlockSpec(a.shape, fix)
    return _carrier_call(
        body, comm, n_chunks,
        in_specs=[pl.BlockSpec((CHUNK, xbc.shape[1]), row), pl.BlockSpec((CHUNK, d_inner), row),
                  pl.BlockSpec((CHUNK, LANE), lambda c: (c, dt_blk)), cspec(par), cspec(dexp), cspec(nw),
                  cspec(ind), cspec(itile_c), cspec(trit_c), cspec(tril_c), cspec(bmask_c)],
        out_specs=[pl.BlockSpec((CHUNK, d_inner), row), pl.BlockSpec((CHUNK, d_inner), row),
                   pl.BlockSpec((1, N_GROUPS, D_STATE, GROUP_W), lambda c: (c, 0, 0, 0))],
        out_shape=[SDS((L, d_inner), BF16), SDS((L, d_inner), F32),
                   SDS((n_chunks, N_GROUPS, D_STATE, GROUP_W), F32)],
        scratch_shapes=[pltpu.VMEM((N_GROUPS, D_STATE, GROUP_W), F32)],
        name=name, args=(xbc, zx, zx, par, dexp, nw, ind, itile_c, trit_c, tril_c, bmask_c))


def _ssd_scan_bwd(dyn, yf, zx, xbc, states, par, dexp, nw, consts, comm, name):
    L = xbc.shape[0]
    d_inner = dexp.shape[1]
    n_chunks = L // CHUNK
    dt_blk = zx.shape[1] // LANE - 1
    ind, ind_t, itile_c, trit_c, tril_c, bmask_c = consts
    hslices = [slice(r * HEAD_DIM, (r + 1) * HEAD_DIM) for r in range(HEADS_PER_GROUP)]

    def body(dyn_ref, yf_ref, z_ref, dtr_ref, xbc_ref, st_in_ref, par_ref, dexp_ref, nw_ref, ind_ref, indt_ref,
             itile_ref, trit_ref, tril_ref, bmask_ref,
             dz_ref, dxbc_ref, ddt_ref, dnw_ref, dpar_ref, dq_ref, dyf_ref):
        @pl.when(pl.program_id(0) == 0)
        def _():
            dq_ref[...] = jnp.zeros_like(dq_ref)
            dnw_ref[...] = jnp.zeros_like(dnw_ref)
            dpar_ref[...] = jnp.zeros_like(dpar_ref)

        z, yfv, dynv = z_ref[...], yf_ref[...], dyn_ref[...]
        sz = _sigmoid(z)
        y2 = yfv * (z * sz)
        r = _rsq(y2)
        y2h = y2 * r
        dnw_ref[...] += jnp.sum(dynv * y2h, axis=0, keepdims=True)
        dyg = dynv * nw_ref[...]
        dy2 = r * (dyg - y2h * jnp.mean(dyg * y2h, axis=-1, keepdims=True))
        dz_ref[...] = (dy2 * yfv * _dsilu(z, sz)).astype(BF16)
        dyf_ref[...] = dy2 * (z * sz)

        pre = dtr_ref[...] + par_ref[0:1, :]
        dt = _softplus(pre)
        a_head = -jnp.exp(par_ref[1:2, :])
        cs = _dot_sel(dt * a_head, tril_ref[...], SEL_X)
        itile, trit, bmask = itile_ref[...], trit_ref[...], bmask_ref[...]
        dcs = jnp.zeros((CHUNK, LANE), F32)
        dcs_last = jnp.zeros((1, LANE), F32)
        ddt_u = jnp.zeros((CHUNK, LANE), F32)
        d_skip = jnp.zeros((1, LANE), F32)
        rsum = lambda v: jnp.sum(v, axis=0, keepdims=True)
        row8 = lax.broadcasted_iota(jnp.int32, (8, GROUP_W), 0)
        for g in range(N_GROUPS):
            t = _group_terms(g, dt, cs, xbc_ref, ind_ref, itile, trit, bmask, d_inner)
            gl, m, lm, u, bb, cb, e, dec, xs = (t[k] for k in ("gl", "m", "lm", "u", "bb", "cb", "e", "dec", "xs"))
            indt = indt_ref[gl, :]
            dy = dyf_ref[:, gl]
            dyb = dy.astype(BF16)
            p = st_in_ref[0, g]
            pb = p.astype(BF16)
            q = dq_ref[g]
            qb = q.astype(BF16)
            big = lax.dot_general(m.astype(BF16), dyb, TN, preferred_element_type=F32)
            du = jnp.zeros((CHUNK, GROUP_W), F32)
            for rh in range(HEADS_PER_GROUP):
                du = du + big[hslices[rh], :] * bmask[rh * HEAD_DIM:rh * HEAD_DIM + 1, :]
            dm = lax.dot_general(dyb, t["bdu"], NT, preferred_element_type=F32)
            w = dm * m
            dgt = (dm * lm).astype(BF16)
            dc = jnp.dot(dgt, t["btile"], preferred_element_type=F32)
            db_big = lax.dot_general(dgt, cb, TN, preferred_element_type=F32)
            db = db_big[hslices[0], :] + db_big[hslices[1], :] + db_big[hslices[2], :] + db_big[hslices[3], :]
            cp = jnp.dot(cb, pb, preferred_element_type=F32)
            dye = dy * e
            dyeb = dye.astype(BF16)
            dc = dc + lax.dot_general(dyeb, pb, NT, preferred_element_type=F32)
            dp = lax.dot_general(cb, dyeb, TN, preferred_element_type=F32)
            x2 = dye * cp
            bq = jnp.dot(bb, qb, preferred_element_type=F32)
            ud = u * dec
            du = du + bq * dec
            db = db + lax.dot_general(ud.astype(BF16), qb, NT, preferred_element_type=F32)
            x1 = bq * ud
            dq_ref[g] = dp + t["e_last"] * q
            x3 = rsum(q * p) * t["e_last"]
            red = _dot_sel(jnp.concatenate([w + x2 - x1, du * xs, itile * rsum(w)], axis=0), indt)
            dcs = dcs + red[0:CHUNK] - red[2 * CHUNK:3 * CHUNK]
            ddt_u = ddt_u + red[CHUNK:2 * CHUNK]
            tail = _dot_sel(jnp.where(row8 == 0, rsum(x1) + x3, jnp.where(row8 == 1, rsum(dy * xs), 0.0)), indt)
            dcs_last = dcs_last + tail[0:1]
            d_skip = d_skip + tail[1:2]
            dxbc_ref[:, gl] = du * t["dtx"] + dy * dexp_ref[:, gl]
            dxbc_ref[:, d_inner + g * D_STATE: d_inner + (g + 1) * D_STATE] = db
            dxbc_ref[:, d_inner + (N_GROUPS + g) * D_STATE: d_inner + (N_GROUPS + g + 1) * D_STATE] = dc
        last = lax.broadcasted_iota(jnp.int32, (CHUNK, LANE), 0) == CHUNK - 1
        dcs = dcs + jnp.where(last, dcs_last, 0.0)
        da = _dot_sel(dcs, tril_ref[...], TN)
        ddt = da * a_head + ddt_u
        heads = lax.broadcasted_iota(jnp.int32, (CHUNK, LANE), 1) < d_inner // HEAD_DIM
        ddt_raw = jnp.where(heads, ddt * _sigmoid(pre), 0.0)
        ddt_ref[...] = ddt_raw.astype(BF16)
        dpar_ref[0:1, :] += rsum(ddt_raw)
        dpar_ref[1:2, :] += rsum(da * dt) * a_head
        dpar_ref[2:3, :] += d_skip

    rev = lambda c: (n_chunks - 1 - c, 0)
    fix = lambda c: (0, 0)
    cspec = lambda a: pl.BlockSpec(a.shape, fix)
    nx = xbc.shape[1]
    return _carrier_call(
        body, comm, n_chunks,
        in_specs=[pl.BlockSpec((CHUNK, d_inner), rev), pl.BlockSpec((CHUNK, d_inner), rev),
                  pl.BlockSpec((CHUNK, d_inner), rev), pl.BlockSpec((CHUNK, LANE), lambda c: (n_chunks - 1 - c, dt_blk)),
                  pl.BlockSpec((CHUNK, nx), rev),
                  pl.BlockSpec((1, N_GROUPS, D_STATE, GROUP_W), lambda c: (n_chunks - 1 - c, 0, 0, 0)),
                  cspec(par), cspec(dexp), cspec(nw), cspec(ind), cspec(ind_t), cspec(itile_c), cspec(trit_c),
                  cspec(tril_c), cspec(bmask_c)],
        out_specs=[pl.BlockSpec((CHUNK, d_inner), rev), pl.BlockSpec((CHUNK, nx), rev),
                   pl.BlockSpec((CHUNK, LANE), rev), pl.BlockSpec((1, d_inner), fix), pl.BlockSpec((8, LANE), fix)],
        out_shape=[SDS((L, d_inner), BF16), SDS((L, nx), F32), SDS((L, LANE), BF16), SDS((1, d_inner), F32),
                   SDS((8, LANE), F32)],
        scratch_shapes=[pltpu.VMEM((N_GROUPS, D_STATE, GROUP_W), F32), pltpu.VMEM((CHUNK, d_inner), F32)],
        name=name, args=(dyn, yf, zx, zx, xbc, states, par, dexp, nw, ind, ind_t, itile_c, trit_c, tril_c, bmask_c))


def _adamw(w, m, v, g, name):
    R, C = w.shape
    tr = R
    for cand in (256, 128, 64, 32, 16, 8):
        if R % cand == 0:
            tr = cand
            break

    def body(w_ref, m_ref, v_ref, g_ref, d_ref, mo_ref, vo_ref):
        gv = g_ref[...]
        mn = ADAM_B1 * m_ref[...] + (1.0 - ADAM_B1) * gv
        vn = ADAM_B2 * v_ref[...] + (1.0 - ADAM_B2) * (gv * gv)
        m_hat = mn / (1.0 - ADAM_B1 ** ADAM_STEP)
        v_hat = vn / (1.0 - ADAM_B2 ** ADAM_STEP)
        d_ref[...] = -ADAM_LR * (m_hat / (jnp.sqrt(v_hat) + ADAM_EPS) + ADAM_WD * w_ref[...])
        mo_ref[...] = mn
        vo_ref[...] = vn

    blk = pl.BlockSpec((tr, C), lambda i: (i, 0))
    return pl.pallas_call(
        body, grid=(R // tr,), in_specs=[blk] * 4, out_specs=[blk] * 3, out_shape=[SDS((R, C), F32)] * 3,
        name=name, compiler_params=_cp("parallel"))(w, m, v, g)


def _sum_slots(parts, name):
    n, R, C = parts.shape
    tr = 128 if R % 128 == 0 else R

    def body(p_ref, o_ref):
        acc = p_ref[0]
        for k in range(1, n):
            acc = acc + p_ref[k]
        o_ref[...] = acc

    return pl.pallas_call(
        body, grid=(R // tr,), in_specs=[pl.BlockSpec((n, tr, C), lambda i: (0, i, 0))],
        out_specs=pl.BlockSpec((tr, C), lambda i: (i, 0)), out_shape=SDS((R, C), F32),
        name=name, compiler_params=_cp("parallel"))(parts)


def _add_core_halves(where, g, r, name):
    _, n, a, b = g.shape
    ta = a // 2

    def body(w_ref, g_ref, r_ref, o_ref):
        o_ref[...] = (g_ref[...].astype(F32) + r_ref[...].astype(F32)).astype(BF16)

    blk = lambda f: pl.BlockSpec((None, None, ta, b), f)
    mine = lambda s, l, w: (s, l, 0, 0)
    return pl.pallas_call(
        body, grid_spec=pltpu.PrefetchScalarGridSpec(
            num_scalar_prefetch=1, grid=(4, n),
            in_specs=[blk(lambda s, l, w: (s, l, w[1], 0)), blk(mine)], out_specs=blk(mine)),
        out_shape=SDS((4, n, ta, b), BF16), name=name,
        compiler_params=_cp("parallel", "parallel"))(where, g, r)


def _sum_shard(where, g, r, rr, name):
    _, n, a, b = g.shape
    ta = a // 2

    def body(w_ref, g_ref, r_ref, rr_ref, o_ref):
        f = lambda v: v.astype(F32)
        o_ref[...] = (((f(g_ref[...]) + f(r_ref[...])) + f(rr_ref[0])) + f(rr_ref[1])) + f(rr_ref[2])

    return pl.pallas_call(
        body, grid_spec=pltpu.PrefetchScalarGridSpec(
            num_scalar_prefetch=1, grid=(n,),
            in_specs=[pl.BlockSpec((None, None, ta, b), lambda l, w: (w[0], l, w[1], 0)),
                      pl.BlockSpec((None, None, ta, b), lambda l, w: (w[0], l, 0, 0)),
                      pl.BlockSpec((3, None, ta, b), lambda l, w: (0, l, 0, 0))],
            out_specs=pl.BlockSpec((None, ta, b), lambda l, w: (l, w[1], 0))),
        out_shape=SDS((n, a, b), F32), name=name,
        compiler_params=_cp("parallel"))(where, g, r, rr)


def _me():
    return lax.axis_index("x"), lax.axis_index("y"), lax.axis_index("c")


def _chip_peers(x, y):
    return [(1 - x, y), (x, 1 - y), (1 - x, 1 - y)]


def _rcopy(src, dst, send_sems, recv_sems, k, to):
    return pltpu.make_async_remote_copy(src_ref=src, dst_ref=dst, send_sem=send_sems.at[k], recv_sem=recv_sems.at[k],
                                        device_id=to, device_id_type=MESH)


def _row_half(ref, c, lead=()):
    a = ref.shape[len(lead) + 1]
    return ref.at[(*lead, slice(None), pl.ds(c * (a // 2), a // 2))]


class _Comm(NamedTuple):
    ins: list
    out_shapes: list
    n_sems: int
    start: Callable
    finish: Callable


def _sem_scratch(comm):
    return [pltpu.SemaphoreType.DMA((comm.n_sems,)), pltpu.SemaphoreType.DMA((comm.n_sems,))]


def _run_comm(comm, name):
    n_in, n_out = len(comm.ins), len(comm.out_shapes)

    def body(*refs):
        ins, outs, sems = refs[:n_in], refs[n_in:n_in + n_out], refs[n_in + n_out:]
        comm.start(ins, outs, *sems)
        comm.finish(ins, outs, *sems)

    return pl.pallas_call(body, in_specs=[ANY] * n_in, out_specs=[ANY] * n_out, out_shape=comm.out_shapes,
                          scratch_shapes=_sem_scratch(comm), name=name)(*comm.ins)


def _carrier_call(compute, comm, n_steps, in_specs, out_specs, out_shape, scratch_shapes, name, args):
    if comm is None:
        return pl.pallas_call(compute, grid=(n_steps,), in_specs=in_specs, out_specs=out_specs, out_shape=out_shape,
                              scratch_shapes=scratch_shapes, name=name, compiler_params=_cp("arbitrary"))(*args), []
    n_in, n_out, n_scr = len(in_specs), len(out_specs), len(scratch_shapes)
    n_ci, n_co = len(comm.ins), len(comm.out_shapes)

    def body(*refs):
        ins, cins = refs[:n_in], refs[n_in:n_in + n_ci]
        o = n_in + n_ci
        outs, couts = refs[o:o + n_out], refs[o + n_out:o + n_out + n_co]
        s = o + n_out + n_co
        scratch, sems = refs[s:s + n_scr], refs[s + n_scr:]

        @pl.when(pl.program_id(0) == 0)
        def _():
            comm.start(cins, couts, *sems)

        compute(*ins, *outs, *scratch)

        @pl.when(pl.program_id(0) == n_steps - 1)
        def _():
            comm.finish(cins, couts, *sems)

    res = pl.pallas_call(
        body, grid=(n_steps,), in_specs=list(in_specs) + [ANY] * n_ci, out_specs=list(out_specs) + [ANY] * n_co,
        out_shape=list(out_shape) + list(comm.out_shapes), scratch_shapes=list(scratch_shapes) + _sem_scratch(comm),
        name=name, compiler_params=_cp("arbitrary"))(*args, *comm.ins)
    return res[:n_out], res[n_out:]


def _allgather_plan(mine, small=None):
    n = len(mine)
    ins = list(mine) + ([] if small is None else [small])
    out_shapes = [SDS((4,) + m.shape, BF16) for m in mine] + ([] if small is None else [SDS((4,) + small.shape, F32)])
    sem = lambda t, k: 6 * t + k

    def first_copies(ins_r, outs_r, send, recv):
        x, y, c = _me()
        q = 2 * x + y
        cps = []
        for j, chip in enumerate(_chip_peers(x, y)):
            for t in range(n):
                cps.append(_rcopy(_row_half(ins_r[t], c), _row_half(outs_r[t], c, (q,)), send, recv, sem(t, j),
                                  (*chip, c)))
            if small is not None:
                cps.append(_rcopy(ins_r[n], outs_r[n].at[q], send, recv, sem(n, j), (*chip, c)))
        return cps

    def start(ins_r, outs_r, send, recv):
        for cp in first_copies(ins_r, outs_r, send, recv):
            cp.start()

    def finish(ins_r, outs_r, send, recv):
        x, y, c = _me()
        sib = (x, y, 1 - c)
        chips = _chip_peers(x, y)
        passed = []
        for j, (px, py) in enumerate(chips):
            for t in range(n):
                blk = _row_half(outs_r[t], c, (2 * px + py,))
                _rcopy(blk, blk, send, recv, sem(t, j), sib).wait_recv()
                cp = _rcopy(blk, blk, send, recv, sem(t, 3 + j), sib)
                cp.start()
                passed.append(cp)
        for j, (px, py) in enumerate(chips):
            for t in range(n):
                blk = _row_half(outs_r[t], 1 - c, (2 * px + py,))
                _rcopy(blk, blk, send, recv, sem(t, 3 + j), sib).wait_recv()
            if small is not None:
                sblk = outs_r[n].at[2 * px + py]
                _rcopy(sblk, sblk, send, recv, sem(n, j), sib).wait_recv()
        for cp in first_copies(ins_r, outs_r, send, recv) + passed:
            cp.wait_send()

    return _Comm(ins, out_shapes, 6 * n + (0 if small is None else 3), start, finish)


def _grads_to_sibling_plan(grads):
    n = len(grads)

    def copies(ins_r, outs_r, send, recv):
        x, y, c = _me()
        return [_rcopy(_row_half(ins_r[t], 1 - c, (slice(None),)), outs_r[t], send, recv, t, (x, y, 1 - c))
                for t in range(n)]

    def start(*a):
        for cp in copies(*a):
            cp.start()

    def finish(*a):
        for cp in copies(*a):
            cp.wait()

    out_shapes = [SDS((4, g.shape[1], g.shape[2] // 2, g.shape[3]), BF16) for g in grads]
    return _Comm(list(grads), out_shapes, n, start, finish)


def _grads_to_chips_plan(psums):
    n = len(psums)

    def copies(ins_r, outs_r, send, recv):
        x, y, c = _me()
        return [_rcopy(ins_r[t].at[2 * px + py], outs_r[t].at[j], send, recv, 3 * t + j, (px, py, c))
                for j, (px, py) in enumerate(_chip_peers(x, y)) for t in range(n)]

    def start(*a):
        for cp in copies(*a):
            cp.start()

    def finish(*a):
        for cp in copies(*a):
            cp.wait()

    return _Comm(list(psums), [SDS((3,) + p.shape[1:], BF16) for p in psums], 3 * n, start, finish)


def _swap_halves(sums):
    n = len(sums)

    def body(*refs):
        out_refs = refs[n:2 * n]
        send_sems, recv_sems = refs[2 * n:]
        x, y, c = _me()
        sib = (x, y, 1 - c)
        cps = [_rcopy(_row_half(out_refs[t], c), _row_half(out_refs[t], c), send_sems, recv_sems, t, sib)
               for t in range(n)]
        for cp in cps:
            cp.start()
        for t in range(n):
            other = _row_half(out_refs[t], 1 - c)
            _rcopy(other, other, send_sems, recv_sems, t, sib).wait_recv()
        for cp in cps:
            cp.wait_send()

    return pl.pallas_call(
        body, in_specs=[ANY] * n, out_specs=[ANY] * n, out_shape=[SDS(s.shape, F32) for s in sums],
        input_output_aliases={t: t for t in range(n)},
        scratch_shapes=[pltpu.SemaphoreType.DMA((n,)), pltpu.SemaphoreType.DMA((n,))],
        name="swap_halves")(*sums)


def _allgather_small(part):
    def body(p_ref, out_ref, send_sems, recv_sems, local_sem):
        x, y, c = _me()
        me = 4 * x + 2 * y + c
        own = pltpu.make_async_copy(p_ref, out_ref.at[me], local_sem.at[0])
        own.start()
        sends = []
        for k in range(1, 8):
            fx, fy, fc = (k >> 2) & 1, (k >> 1) & 1, k & 1
            to = (x ^ fx, y ^ fy, c ^ fc)
            sends.append(_rcopy(p_ref, out_ref.at[me], send_sems, recv_sems, k - 1, to))
        for cp in sends:
            cp.start()
        for k in range(1, 8):
            slot = out_ref.at[me ^ k]
            _rcopy(slot, slot, send_sems, recv_sems, k - 1, (x, y, c)).wait_recv()
        for cp in sends:
            cp.wait_send()
        own.wait()

    return pl.pallas_call(
        body, in_specs=[ANY], out_specs=ANY, out_shape=SDS((8,) + part.shape, F32),
        scratch_shapes=[pltpu.SemaphoreType.DMA((7,)), pltpu.SemaphoreType.DMA((7,)), pltpu.SemaphoreType.DMA((1,))],
        name="allgather_small")(part)


BIG = (("ssd_w_in", 2), ("ssd_w_out", 1), ("sc_w_in", 2), ("sc_w_out", 1), ("ffn_w_up", 2), ("ffn_w_down", 1))


def _to_shards(full, axis):
    A, B = full.shape
    if axis == 2:
        return full.reshape(A, 4, B // 4).transpose(1, 0, 2)
    return full.reshape(4, A // 4, B)


def _from_shards(shards, axis):
    _, a, b = shards.shape
    if axis == 2:
        return shards.transpose(1, 0, 2).reshape(a, 4 * b)
    return shards.reshape(4 * a, b)


def _interleave(w, parts):
    lead, n = w.shape[:-1], w.shape[-1]
    return w.reshape(*lead, parts, n // (parts * LANE), LANE).swapaxes(-2, -3).reshape(*lead, n)


def _deinterleave(w, parts):
    lead, n = w.shape[:-1], w.shape[-1]
    return w.reshape(*lead, n // (parts * LANE), parts, LANE).swapaxes(-2, -3).reshape(*lead, n)


def _pack_rows(vectors, width, row_multiple):
    flat = jnp.concatenate(vectors, axis=-1)
    n = flat.shape[-1]
    unit = width * row_multiple
    total = -(-n // unit) * unit
    flat = jnp.pad(flat, [(0, 0)] * (flat.ndim - 1) + [(0, total - n)])
    return flat.reshape(*flat.shape[:-1], total // width, width)


def _unpack(flat, shapes):
    out, off = [], 0
    for s in shapes:
        n = int(np.prod(s))
        out.append(flat[..., off:off + n].reshape(*flat.shape[:-1], *s))
        off += n
    return out


def _memo(fn):
    cache = {}

    def wrapped(k):
        if k not in cache:
            cache[k] = fn(k)
        return cache[k]

    return wrapped


def _row(v):
    return v.reshape(1, -1)


def _pad_rows(w, rows=8):
    return jnp.pad(w, ((0, rows - w.shape[0]), (0, 0)))


def _ffn_fwd(x, g_pre, g_post, w_up, cw, cb, w_down, tag):
    up, hn = _norm_matmul(x, g_pre, w_up, BF16, "ffn_up" + tag)
    a = _ffn_mid_fwd(up, cw, cb, "ffn_mid_fwd" + tag)
    f, x_new = _matmul_norm_res(a, w_down, x, g_post, "ffn_down" + tag)
    return x_new, (x, hn, up, a, f)


def _ffn_bwd(dx, saved, g_pre, g_post, w_up, cw, cb, w_down, tag):
    x, hn, up, a, f = saved
    df, dg_post = _postnorm_bwd(dx, f, g_post, "ffn_post_bwd" + tag)
    da = _matmul_nt(df, w_down, BF16, "ffn_down_dx" + tag)
    dw_down = _matmul_tn(a, df, w_down.shape[0] // 2, w_down.shape[1], "ffn_down_dw" + tag)
    dup, stats = _ffn_mid_bwd(da, up, cw, cb, "ffn_mid_bwd" + tag)
    dx_in, dg_pre = _matmul_nt_prenorm_bwd(dup, w_up, x, g_pre, dx, "ffn_up_dx" + tag)
    dw_up = _matmul_tn(hn, dup, w_up.shape[0], w_up.shape[1] // 4, "ffn_up_dw" + tag)
    return dx_in, dict(g_pre=dg_pre, g_post=dg_post, w_up=dw_up, w_down=dw_down, cw=stats[0:3], cb=stats[3])


def _sc_fwd(x, g_pre, g_post, w_in, cw, w_out, tag):
    bcv, hn = _norm_matmul(x, g_pre, w_in, BF16, "sc_in" + tag)
    q = _sc_mid_fwd(bcv, cw, "sc_mid_fwd" + tag)
    m, x_new = _matmul_norm_res(q, w_out, x, g_post, "sc_out" + tag)
    return x_new, (x, hn, bcv, q, m)


def _sc_bwd(dx, saved, g_pre, g_post, w_in, cw, w_out, tag):
    x, hn, bcv, q, m = saved
    dm, dg_post = _postnorm_bwd(dx, m, g_post, "sc_post_bwd" + tag)
    dq = _matmul_nt(dm, w_out, BF16, "sc_out_dx" + tag)
    dw_out = _matmul_tn(q, dm, w_out.shape[0], w_out.shape[1], "sc_out_dw" + tag)
    dbcv, stats = _sc_mid_bwd(dq, bcv, cw, "sc_mid_bwd" + tag)
    dx_in, dg_pre = _matmul_nt_prenorm_bwd(dbcv, w_in, x, g_pre, dx, "sc_in_dx" + tag)
    dw_in = _matmul_tn(hn, dbcv, w_in.shape[0], w_in.shape[1] // 3, "sc_in_dw" + tag)
    return dx_in, dict(g_pre=dg_pre, g_post=dg_post, w_in=dw_in, w_out=dw_out, cw=stats[0:3])


def _ssd_fwd(x, g_pre, g_post, w_in, cw, cb, par, dexp, nw, w_out, consts, comm, tag):
    d_inner = dexp.shape[1]
    zx, hn = _norm_matmul(x, g_pre, w_in, F32, "ssd_in" + tag)
    xbc = _ssd_conv_fwd(zx, cw, cb, d_inner, cw.shape[1], "ssd_conv_fwd" + tag)
    (yn, yf, states), carried = _ssd_scan_fwd(zx, xbc, par, dexp, nw, consts, comm, "ssd_scan_fwd" + tag)
    m, x_new = _matmul_norm_res(yn, w_out, x, g_post, "ssd_out" + tag)
    return x_new, (x, hn, zx, xbc, yn, yf, states, m), carried


def _ssd_bwd(dx, saved, g_pre, g_post, w_in, cw, cb, par, dexp, nw, w_out, consts, comm, tag):
    x, hn, zx, xbc, yn, yf, states, m = saved
    d_inner = dexp.shape[1]
    dm, dg_post = _postnorm_bwd(dx, m, g_post, "ssd_post_bwd" + tag)
    dyn = _matmul_nt(dm, w_out, F32, "ssd_out_dx" + tag)
    dw_out = _matmul_tn(yn, dm, w_out.shape[0] // 2, w_out.shape[1], "ssd_out_dw" + tag)
    (dz, dxbc, ddt, dnw, dpar), carried = _ssd_scan_bwd(dyn, yf, zx, xbc, states, par, dexp, nw, consts, comm,
                                                        "ssd_scan_bwd" + tag)
    dxp, stats = _ssd_conv_bwd(dxbc, zx, cw, cb, d_inner, "ssd_conv_bwd" + tag)
    dzx = jnp.concatenate([dz, dxp, ddt], axis=1)
    dx_in, dg_pre = _matmul_nt_prenorm_bwd(dzx, w_in, x, g_pre, dx, "ssd_in_dx" + tag)
    dw_in = _matmul_tn(hn, dzx, w_in.shape[0], w_in.shape[1] // 7, "ssd_in_dw" + tag)
    n_heads = d_inner // HEAD_DIM
    grads = dict(g_pre=dg_pre, g_post=dg_post, w_in=dw_in, w_out=dw_out, cw=stats[0:4], cb=stats[4],
                 dt_bias=dpar[0, :n_heads], a_log=dpar[1, :n_heads], d=dpar[2, :n_heads], nw=dnw[0])
    return dx_in, grads, carried


def kernel(x, mix_pre_g, mix_post_g, ffn_pre_g, ffn_post_g, ssd_w_in, ssd_conv_w, ssd_conv_b, ssd_dt_bias, ssd_A_log, ssd_D, ssd_norm_w, ssd_w_out, sc_w_in, sc_conv_w, sc_w_out, ffn_w_up, ffn_conv_w, ffn_conv_b, ffn_w_down, loss_target, m_mix_pre_g, m_mix_post_g, m_ffn_pre_g, m_ffn_post_g, m_ssd_w_in, m_ssd_conv_w, m_ssd_conv_b, m_ssd_dt_bias, m_ssd_A_log, m_ssd_D, m_ssd_norm_w, m_ssd_w_out, m_sc_w_in, m_sc_conv_w, m_sc_w_out, m_ffn_w_up, m_ffn_conv_w, m_ffn_conv_b, m_ffn_w_down, v_mix_pre_g, v_mix_post_g, v_ffn_pre_g, v_ffn_post_g, v_ssd_w_in, v_ssd_conv_w, v_ssd_conv_b, v_ssd_dt_bias, v_ssd_A_log, v_ssd_D, v_ssd_norm_w, v_ssd_w_out, v_sc_w_in, v_sc_conv_w, v_sc_w_out, v_ffn_w_up, v_ffn_conv_w, v_ffn_conv_b, v_ffn_w_down):
    names = ["mix_pre_g", "mix_post_g", "ffn_pre_g", "ffn_post_g", "ssd_w_in", "ssd_conv_w", "ssd_conv_b",
             "ssd_dt_bias", "ssd_A_log", "ssd_D", "ssd_norm_w", "ssd_w_out", "sc_w_in", "sc_conv_w", "sc_w_out",
             "ffn_w_up", "ffn_conv_w", "ffn_conv_b", "ffn_w_down"]
    env = locals()
    wts = {n: env[n] for n in names}
    mom = {n: env["m_" + n] for n in names}
    var = {n: env["v_" + n] for n in names}

    depth, d_model = mix_pre_g.shape
    n_ssd, n_heads = ssd_dt_bias.shape
    n_sc = sc_conv_w.shape[0]
    d_inner = n_heads * HEAD_DIM
    conv_dim = d_inner + 2 * N_GROUPS * D_STATE
    ssd_in_dim = d_inner + conv_dim + n_heads
    ssd_in_pad = d_inner + conv_dim + LANE
    q_chip = 2 * lax.axis_index("x") + lax.axis_index("y")
    core = lax.axis_index("c")

    assert depth == 4 and n_ssd == 2 and n_sc == 2, "the exchange schedule is written for this trunk"
    ssd_items = lambda j: [("ssd_w_in", j, j + 1), ("ssd_w_out", j, j + 1)]
    sc_items = lambda j: [("sc_w_in", j, j + 1), ("sc_w_out", j, j + 1)]
    ffn_items = lambda lo, hi: [("ffn_w_up", lo, hi), ("ffn_w_down", lo, hi)]
    gather_first = ssd_items(0)
    gather_in_scan = {0: ffn_items(0, 2) + sc_items(0) + ssd_items(1), 2: ffn_items(2, 4) + sc_items(1)}
    reduce_in_scan = {2: ffn_items(2, 4) + sc_items(1), 0: ssd_items(1) + ffn_items(0, 2) + sc_items(0)}
    reduce_last = ssd_items(0)
    axis_of = dict(BIG)
    own_slot = lambda buf, own: lax.dynamic_update_index_in_dim(buf, own, q_chip, 0)
    gathered = {}

    def gather_plan(items, small=None):
        mine = [wts[n][lo:hi].astype(BF16) for n, lo, hi in items]
        return mine, _allgather_plan(mine, small)

    def gather_done(items, mine, results):
        for (n, lo, hi), own, buf in zip(items, mine, results):
            buf = own_slot(buf, own)
            for layer in range(lo, hi):
                gathered[(n, layer)] = (buf, layer - lo)

    def full(n, layer):
        buf, k = gathered[(n, layer)]
        return _from_shards(buf[:, k], axis_of[n])

    conv_names = ["ssd_conv_w", "sc_conv_w", "ffn_conv_w"]
    conv_shapes = [wts[n].shape for n in conv_names]
    small_mine = _pack_rows([wts[n].reshape(-1) for n in conv_names], LANE, 8)
    mine, plan = gather_plan(gather_first, small_mine)
    *results, small_all = _run_comm(plan, "allgather_first")
    gather_done(gather_first, mine, results)
    small_all = own_slot(small_all, small_mine)
    conv_full = {}
    for n, f, s in zip(conv_names, _unpack(small_all.reshape(4, -1), conv_shapes), conv_shapes):
        conv_full[n] = f.transpose(1, 2, 0, 3).reshape(s[0], s[1], 4 * s[2])

    consts = _scan_constants(n_heads)

    def ssd_args(j):
        par = jnp.zeros((8, LANE), F32).at[0, :n_heads].set(ssd_dt_bias[j]).at[1, :n_heads].set(ssd_A_log[j])
        dexp = jnp.repeat(ssd_D[j], HEAD_DIM).reshape(1, d_inner)
        w_in = jnp.pad(full("ssd_w_in", j), ((0, 0), (0, ssd_in_pad - ssd_in_dim)))
        return (w_in, _pad_rows(conv_full["ssd_conv_w"][j]), _row(ssd_conv_b[j]), par, dexp,
                _row(ssd_norm_w[j]), full("ssd_w_out", j), consts)

    def sc_args(j):
        return (_interleave(full("sc_w_in", j), 3), _pad_rows(conv_full["sc_conv_w"][j]), full("sc_w_out", j))

    def ffn_args(i):
        return (_interleave(full("ffn_w_up", i), 2), _pad_rows(conv_full["ffn_conv_w"][i]), _row(ffn_conv_b[i]),
                full("ffn_w_down", i))

    ssd_args, sc_args, ffn_args = _memo(ssd_args), _memo(sc_args), _memo(ffn_args)

    h = x[0]
    saved = []
    for i in range(depth):
        j = i // 2
        gp, gq = _row(mix_pre_g[i]), _row(mix_post_g[i])
        if i % 2 == 0:
            mine, plan = gather_plan(gather_in_scan[i])
            h, sv, results = _ssd_fwd(h, gp, gq, *ssd_args(j), plan, tag="")
            gather_done(gather_in_scan[i], mine, results)
        else:
            h, sv = _sc_fwd(h, gp, gq, *sc_args(j), tag="")
        w_up, cw, cb, w_down = ffn_args(i)
        h, sv2 = _ffn_fwd(h, _row(ffn_pre_g[i]), _row(ffn_post_g[i]), w_up, cw, cb, w_down, tag="")
        saved.append((sv, sv2))
    dh, loss_part = _loss_head(h, loss_target[0], "loss_head")

    mix_grads, ffn_grads = [None] * depth, [None] * depth
    where = jnp.stack([q_chip, core]).astype(jnp.int32)

    def local_grad(n, layer):
        if n == "ssd_w_in":
            return mix_grads[2 * layer]["w_in"][:, :ssd_in_dim]
        if n == "ssd_w_out":
            return mix_grads[2 * layer]["w_out"]
        if n == "sc_w_in":
            return _deinterleave(mix_grads[2 * layer + 1]["w_in"], 3)
        if n == "sc_w_out":
            return mix_grads[2 * layer + 1]["w_out"]
        if n == "ffn_w_up":
            return _deinterleave(ffn_grads[layer]["w_up"], 2)
        return ffn_grads[layer]["w_down"]

    def reduce_begin(items, tag):
        by_shard = [jnp.stack([_to_shards(local_grad(n, layer), axis_of[n]).astype(BF16) for layer in range(lo, hi)],
                              axis=1) for n, lo, hi in items]
        from_sib = _run_comm(_grads_to_sibling_plan(by_shard), "grads_to_sibling" + tag)
        chip_sums = [_add_core_halves(where, g, r, "add_core_halves_" + n + tag)
                     for (n, _, _), g, r in zip(items, by_shard, from_sib)]
        return by_shard, from_sib, _grads_to_chips_plan(chip_sums)

    def reduce_end(items, by_shard, from_sib, from_chips, tag):
        return [_sum_shard(where, g, r, rr, "sum_shard_" + n + tag)
                for (n, _, _), g, r, rr in zip(items, by_shard, from_sib, from_chips)]

    reduced_items, half_sums = [], []
    for i in reversed(range(depth)):
        j = i // 2
        sv, sv2 = saved[i]
        w_up, cw, cb, w_down = ffn_args(i)
        dh, ffn_grads[i] = _ffn_bwd(dh, sv2, _row(ffn_pre_g[i]), _row(ffn_post_g[i]), w_up, cw, cb, w_down, tag="")
        gp, gq = _row(mix_pre_g[i]), _row(mix_post_g[i])
        if i % 2 == 0:
            items, tag = reduce_in_scan[i], "_%d" % i
            by_shard, from_sib, plan = reduce_begin(items, tag)
            dh, mix_grads[i], from_chips = _ssd_bwd(dh, sv, gp, gq, *ssd_args(j), plan, tag="")
            reduced_items += items
            half_sums += reduce_end(items, by_shard, from_sib, from_chips, tag)
        else:
            w_in, scw, w_out = sc_args(j)
            dh, mix_grads[i] = _sc_bwd(dh, sv, gp, gq, w_in, scw, w_out, tag="")
    grad_x = dh[None]
    by_shard, from_sib, plan = reduce_begin(reduce_last, "_last")
    from_chips = _run_comm(plan, "grads_to_chips_last")
    reduced_items += reduce_last
    half_sums += reduce_end(reduce_last, by_shard, from_sib, from_chips, "_last")
    pieces = {}
    for (n, lo, _), g in zip(reduced_items, _swap_halves(half_sums)):
        pieces.setdefault(n, []).append((lo, g))
    big_grads = {n: jnp.concatenate([g for _, g in sorted(p, key=lambda t: t[0])]) for n, p in pieces.items()}
    ssd_l = [mix_grads[i] for i in range(0, depth, 2)]
    sc_l = [mix_grads[i] for i in range(1, depth, 2)]
    stack = lambda layers, k: jnp.stack([g[k] for g in layers])

    small_names = ["mix_pre_g", "mix_post_g", "ffn_pre_g", "ffn_post_g", "ssd_conv_w", "ssd_conv_b", "ssd_dt_bias",
                   "ssd_A_log", "ssd_D", "ssd_norm_w", "sc_conv_w", "ffn_conv_w", "ffn_conv_b"]
    small_local = {
        "mix_pre_g": jnp.concatenate([g["g_pre"] for g in mix_grads]),
        "mix_post_g": jnp.concatenate([g["g_post"] for g in mix_grads]),
        "ffn_pre_g": jnp.concatenate([g["g_pre"] for g in ffn_grads]),
        "ffn_post_g": jnp.concatenate([g["g_post"] for g in ffn_grads]),
        "ssd_conv_w": stack(ssd_l, "cw"), "ssd_conv_b": stack(ssd_l, "cb"), "ssd_dt_bias": stack(ssd_l, "dt_bias"),
        "ssd_A_log": stack(ssd_l, "a_log"), "ssd_D": stack(ssd_l, "d"), "ssd_norm_w": stack(ssd_l, "nw"),
        "sc_conv_w": stack(sc_l, "cw"), "ffn_conv_w": stack(ffn_grads, "cw"), "ffn_conv_b": stack(ffn_grads, "cb"),
    }
    small_full_shapes = [small_local[n].shape for n in small_names]
    spack = _pack_rows([small_local[n].reshape(-1) for n in small_names] + [loss_part.reshape(-1)], LANE, 8)
    stotal = _sum_slots(_allgather_small(spack), "sum_small").reshape(-1)
    small_grads = dict(zip(small_names, _unpack(stotal, small_full_shapes)))
    loss = stotal[sum(int(np.prod(s)) for s in small_full_shapes)]
    for n in conv_names:
        width = wts[n].shape[-1]
        small_grads[n] = lax.dynamic_slice_in_dim(small_grads[n], q_chip * width, width, axis=2)

    grads, delta, new_m, new_v = {}, {}, {}, {}
    for n, _ in BIG:
        s = wts[n].shape
        two_d = lambda a: a.reshape(-1, s[-1])
        grads[n] = big_grads[n]
        d, mn, vn = _adamw(two_d(wts[n]), two_d(mom[n]), two_d(var[n]), two_d(grads[n]), "adamw_" + n)
        delta[n], new_m[n], new_v[n] = d.reshape(s), mn.reshape(s), vn.reshape(s)
    small_shapes = [wts[n].shape for n in small_names]
    pk = lambda d: _pack_rows([d[n].reshape(-1) for n in small_names], LANE, 8)
    for n in small_names:
        grads[n] = small_grads[n].reshape(wts[n].shape)
    d, mn, vn = _adamw(pk(wts), pk(mom), pk(var), pk(grads), "adamw_small")
    for out, packed in ((delta, d), (new_m, mn), (new_v, vn)):
        out.update(zip(small_names, _unpack(packed.reshape(-1), small_shapes)))

    return (loss, grad_x, *[grads[n] for n in names], *[delta[n] for n in names], *[new_m[n] for n in names],
            *[new_v[n] for n in names])
```

```python
from typing import Callable, NamedTuple

import jax
import jax.numpy as jnp
import numpy as np
from jax import lax
from jax.experimental import pallas as pl
from jax.experimental.pallas import tpu as pltpu

F32 = jnp.float32
BF16 = jnp.bfloat16
SDS = jax.ShapeDtypeStruct
MESH = pl.DeviceIdType.MESH
ANY = pl.BlockSpec(memory_space=pl.ANY)

EPS = 1e-6
CHUNK = 64
HEAD_DIM = 64
N_GROUPS = 8
D_STATE = 128
HEADS_PER_GROUP = 4
GROUP_W = HEADS_PER_GROUP * HEAD_DIM
LANE = 128
ROW_TILE = 128
HALO = 8
VMEM_LIMIT = 56 * 1024 * 1024

ADAM_LR = 0.001
ADAM_B1 = 0.9
ADAM_B2 = 0.999
ADAM_EPS = 1e-08
ADAM_WD = 0.01
ADAM_STEP = 10

NT = (((1,), (1,)), ((), ()))
TN = (((0,), (0,)), ((), ()))


def _cp(*sem):
    return pltpu.CompilerParams(dimension_semantics=sem or None, vmem_limit_bytes=VMEM_LIMIT)


def _sigmoid(x):
    return 1.0 / (1.0 + jnp.exp(-x))


def _dsilu(x, s):
    return s * (1.0 + x * (1.0 - s))


def _rsq(x):
    return lax.rsqrt(jnp.mean(x * x, axis=-1, keepdims=True) + EPS)


MM_ROWS = 256


def _norm_matmul(x, g, w, out_dtype, name):
    L, D = x.shape
    N = w.shape[1]
    tm = min(MM_ROWS, L)

    def body(x_ref, g_ref, w_ref, o_ref, hn_ref):
        xv = x_ref[...]
        hn = (xv * _rsq(xv) * g_ref[...]).astype(BF16)
        hn_ref[...] = hn
        o_ref[...] = jnp.dot(hn, w_ref[...], preferred_element_type=F32).astype(out_dtype)

    row = lambda i: (i, 0)
    fix = lambda i: (0, 0)
    return pl.pallas_call(
        body, grid=(L // tm,),
        in_specs=[pl.BlockSpec((tm, D), row), pl.BlockSpec((1, D), fix), pl.BlockSpec((D, N), fix)],
        out_specs=[pl.BlockSpec((tm, N), row), pl.BlockSpec((tm, D), row)],
        out_shape=[SDS((L, N), out_dtype), SDS((L, D), BF16)],
        name=name, compiler_params=_cp("parallel"))(x, g, w)


def _matmul_norm_res(a, w, x, g, name, comm=None):
    L, K = a.shape
    D = w.shape[1]
    tm = min(MM_ROWS, L)

    def body(a_ref, w_ref, x_ref, g_ref, m_ref, xo_ref):
        m = jnp.dot(a_ref[...], w_ref[...], preferred_element_type=F32)
        m_ref[...] = m
        xo_ref[...] = x_ref[...] + m * _rsq(m) * g_ref[...]

    row = lambda i: (i, 0)
    fix = lambda i: (0, 0)
    return _carrier_call(
        body, comm, L // tm,
        in_specs=[pl.BlockSpec((tm, K), row), pl.BlockSpec((K, D), fix), pl.BlockSpec((tm, D), row),
                  pl.BlockSpec((1, D), fix)],
        out_specs=[pl.BlockSpec((tm, D), row), pl.BlockSpec((tm, D), row)],
        out_shape=[SDS((L, D), F32), SDS((L, D), F32)], scratch_shapes=[], name=name, args=(a, w, x, g))


def _postnorm_bwd(dx, m, g, name):
    L, D = dx.shape
    tm = min(512, L)

    def body(dx_ref, m_ref, g_ref, dm_ref, dg_ref):
        @pl.when(pl.program_id(0) == 0)
        def _():
            dg_ref[...] = jnp.zeros_like(dg_ref)

        m = m_ref[...]
        dxv = dx_ref[...]
        r = _rsq(m)
        mh = m * r
        dg_ref[...] += jnp.sum(dxv * mh, axis=0, keepdims=True)
        dyg = dxv * g_ref[...]
        dm_ref[...] = (r * (dyg - mh * jnp.mean(dyg * mh, axis=-1, keepdims=True))).astype(BF16)

    row = lambda i: (i, 0)
    fix = lambda i: (0, 0)
    return pl.pallas_call(
        body, grid=(L // tm,),
        in_specs=[pl.BlockSpec((tm, D), row), pl.BlockSpec((tm, D), row), pl.BlockSpec((1, D), fix)],
        out_specs=[pl.BlockSpec((tm, D), row), pl.BlockSpec((1, D), fix)],
        out_shape=[SDS((L, D), BF16), SDS((1, D), F32)],
        name=name, compiler_params=_cp("arbitrary"))(dx, m, g)


def _matmul_nt(a, w, out_dtype, name):
    L, D = a.shape
    K = w.shape[0]
    tm = min(MM_ROWS, L)

    def body(a_ref, w_ref, o_ref):
        o_ref[...] = lax.dot_general(a_ref[...], w_ref[...], NT, preferred_element_type=F32).astype(out_dtype)

    return pl.pallas_call(
        body, grid=(L // tm,),
        in_specs=[pl.BlockSpec((tm, D), lambda i: (i, 0)), pl.BlockSpec((K, D), lambda i: (0, 0))],
        out_specs=pl.BlockSpec((tm, K), lambda i: (i, 0)),
        out_shape=SDS((L, K), out_dtype),
        name=name, compiler_params=_cp("parallel"))(a, w)


def _matmul_tn(a, b, ta, tn, name):
    L, Ka = a.shape
    N = b.shape[1]
    tl = min(512, L)
    n_l = L // tl

    def body(a_ref, b_ref, o_ref, acc_ref):
        l = pl.program_id(2)

        @pl.when(l == 0)
        def _():
            acc_ref[...] = jnp.zeros_like(acc_ref)

        acc_ref[...] += lax.dot_general(a_ref[...], b_ref[...], TN, preferred_element_type=F32)

        @pl.when(l == n_l - 1)
        def _():
            o_ref[...] = acc_ref[...].astype(BF16)

    return pl.pallas_call(
        body, grid=(Ka // ta, N // tn, n_l),
        in_specs=[pl.BlockSpec((tl, ta), lambda i, j, l: (l, i)), pl.BlockSpec((tl, tn), lambda i, j, l: (l, j))],
        out_specs=pl.BlockSpec((ta, tn), lambda i, j, l: (i, j)),
        out_shape=SDS((Ka, N), BF16),
        scratch_shapes=[pltpu.VMEM((ta, tn), F32)],
        name=name, compiler_params=_cp("parallel", "parallel", "arbitrary"))(a, b)


def _ffn_up(x, g, w4, name, comm=None):
    L, D = x.shape
    b = w4.shape[2]
    tm = min(MM_ROWS, L)

    def body(x_ref, g_ref, w_ref, o_ref, hn_ref):
        xv = x_ref[...]
        hn = (xv * _rsq(xv) * g_ref[...]).astype(BF16)
        hn_ref[...] = hn
        for q in range(4):
            o_ref[q // 2, :, (q % 2) * b:(q % 2 + 1) * b] = jnp.dot(hn, w_ref[q], preferred_element_type=F32).astype(BF16)

    return _carrier_call(
        body, comm, L // tm,
        in_specs=[pl.BlockSpec((tm, D), lambda i: (i, 0)), pl.BlockSpec((1, D), lambda i: (0, 0)),
                  pl.BlockSpec((4, D, b), lambda i: (0, 0, 0))],
        out_specs=[pl.BlockSpec((2, tm, 2 * b), lambda i: (0, i, 0)), pl.BlockSpec((tm, D), lambda i: (i, 0))],
        out_shape=[SDS((2, L, 2 * b), BF16), SDS((L, D), BF16)], scratch_shapes=[], name=name, args=(x, g, w4))


def _ffn_up_dx(dup, w4, x, g, dres, name):
    _, L, _ = dup.shape
    _, D, b = w4.shape
    tm = min(MM_ROWS, L)

    def body(dy_ref, w_ref, x_ref, g_ref, dres_ref, dx_ref, dg_ref):
        @pl.when(pl.program_id(0) == 0)
        def _():
            dg_ref[...] = jnp.zeros_like(dg_ref)

        dh = jnp.zeros((tm, D), F32)
        for q in range(4):
            dh = dh + lax.dot_general(dy_ref[q // 2, :, (q % 2) * b:(q % 2 + 1) * b], w_ref[q], NT,
                                      preferred_element_type=F32)
        xv = x_ref[...]
        r = _rsq(xv)
        xh = xv * r
        dg_ref[...] += jnp.sum(dh * xh, axis=0, keepdims=True)
        dyg = dh * g_ref[...]
        dx_ref[...] = dres_ref[...] + r * (dyg - xh * jnp.mean(dyg * xh, axis=-1, keepdims=True))

    row = lambda i: (i, 0)
    fix = lambda i: (0, 0)
    return pl.pallas_call(
        body, grid=(L // tm,),
        in_specs=[pl.BlockSpec((2, tm, 2 * b), lambda i: (0, i, 0)), pl.BlockSpec((4, D, b), lambda i: (0, 0, 0)),
                  pl.BlockSpec((tm, D), row), pl.BlockSpec((1, D), fix), pl.BlockSpec((tm, D), row)],
        out_specs=[pl.BlockSpec((tm, D), row), pl.BlockSpec((1, D), fix)],
        out_shape=[SDS((L, D), F32), SDS((1, D), F32)],
        name=name, compiler_params=_cp("arbitrary"))(dup, w4, x, g, dres)


def _ffn_up_dw(hn, dup, name):
    L, D = hn.shape
    b = dup.shape[2] // 2
    tl = min(512, L)
    n_l = L // tl

    def body(a_ref, b_ref, o_ref, acc_ref):
        l = pl.program_id(1)

        @pl.when(l == 0)
        def _():
            acc_ref[...] = jnp.zeros_like(acc_ref)

        acc_ref[...] += lax.dot_general(a_ref[...], b_ref[...], TN, preferred_element_type=F32)

        @pl.when(l == n_l - 1)
        def _():
            o_ref[...] = acc_ref[...].astype(BF16)

    return pl.pallas_call(
        body, grid=(4, n_l),
        in_specs=[pl.BlockSpec((tl, D), lambda q, l: (l, 0)),
                  pl.BlockSpec((None, tl, b), lambda q, l: (q // 2, l, q % 2))],
        out_specs=pl.BlockSpec((None, D, b), lambda q, l: (q, 0, 0)),
        out_shape=SDS((4, D, b), BF16),
        scratch_shapes=[pltpu.VMEM((D, b), F32)],
        name=name, compiler_params=_cp("parallel", "arbitrary"))(hn, dup)


def _matmul_nt_prenorm_bwd(dy, w, x, g, dres, name):
    L, N = dy.shape
    D = w.shape[0]
    tm = min(MM_ROWS, L)

    def body(dy_ref, w_ref, x_ref, g_ref, dres_ref, dx_ref, dg_ref):
        @pl.when(pl.program_id(0) == 0)
        def _():
            dg_ref[...] = jnp.zeros_like(dg_ref)

        dh = lax.dot_general(dy_ref[...], w_ref[...], NT, preferred_element_type=F32)
        xv = x_ref[...]
        r = _rsq(xv)
        xh = xv * r
        dg_ref[...] += jnp.sum(dh * xh, axis=0, keepdims=True)
        dyg = dh * g_ref[...]
        dx_ref[...] = dres_ref[...] + r * (dyg - xh * jnp.mean(dyg * xh, axis=-1, keepdims=True))

    row = lambda i: (i, 0)
    fix = lambda i: (0, 0)
    return pl.pallas_call(
        body, grid=(L // tm,),
        in_specs=[pl.BlockSpec((tm, N), row), pl.BlockSpec((D, N), fix), pl.BlockSpec((tm, D), row),
                  pl.BlockSpec((1, D), fix), pl.BlockSpec((tm, D), row)],
        out_specs=[pl.BlockSpec((tm, D), row), pl.BlockSpec((1, D), fix)],
        out_shape=[SDS((L, D), F32), SDS((1, D), F32)],
        name=name, compiler_params=_cp("arbitrary"))(dy, w, x, g, dres)


def _loss_head(y, t, name):
    L, D = y.shape
    tm = min(512, L)

    def body(y_ref, t_ref, dy_ref, loss_ref):
        @pl.when(pl.program_id(0) == 0)
        def _():
            loss_ref[...] = jnp.zeros_like(loss_ref)

        e = y_ref[...] - t_ref[...]
        dy_ref[...] = e * (1.0 / D)
        s = jnp.sum(jnp.sum(e * e, axis=1, keepdims=True), axis=0, keepdims=True)
        loss_ref[...] += s * (0.5 / D)

    row = lambda i: (i, 0)
    return pl.pallas_call(
        body, grid=(L // tm,),
        in_specs=[pl.BlockSpec((tm, D), row), pl.BlockSpec((tm, D), row)],
        out_specs=[pl.BlockSpec((tm, D), row), pl.BlockSpec((1, 1), lambda i: (0, 0))],
        out_shape=[SDS((L, D), F32), SDS((1, 1), F32)],
        name=name, compiler_params=_cp("arbitrary"))(y, t)


def _tile_rows(ref):
    return HALO * (4 // jnp.dtype(ref.dtype).itemsize)


def _prev_rows(ref, r0, i, cols):
    n = _tile_rows(ref)
    p0 = pl.multiple_of(jnp.maximum(r0 - n, 0), n)
    return jnp.where(i > 0, ref[pl.ds(p0, n), cols].astype(F32)[n - HALO:], 0.0)


def _next_rows(ref, r0, i, n_tiles, cols):
    n = _tile_rows(ref)
    n0 = pl.multiple_of(jnp.minimum(r0 + ROW_TILE, n_tiles * ROW_TILE - n), n)
    return jnp.where(i < n_tiles - 1, ref[pl.ds(n0, n), cols].astype(F32)[:HALO], 0.0)


def _rows_f32(ref, rows, cols):
    return ref[rows, cols].astype(F32)


def _back(ext, s):
    return pltpu.roll(ext, s, axis=0)[HALO:HALO + ROW_TILE]


def _fwd(ext, s):
    n = ext.shape[0]
    return pltpu.roll(ext, n - s, axis=0)[:ROW_TILE]


def _store_rows(ref, rows):
    ref[...] = jnp.zeros_like(ref)
    for k, v in enumerate(rows):
        ref[k:k + 1, :] = v


def _strip_call(body, L, n_strips, ins, outs, name):
    def spec(rows, width, off):
        if off is None:
            return pl.BlockSpec((rows, width), lambda j: (0, 0))
        return pl.BlockSpec((rows, width), lambda j: (0, j + off))

    return pl.pallas_call(
        body, grid=(n_strips,),
        in_specs=[spec(a.shape[0], w, off) for a, w, off in ins],
        out_specs=[spec(s.shape[0], w, off) for s, w, off in outs],
        out_shape=[s for s, _, _ in outs],
        name=name, compiler_params=_cp("parallel"))(*[a for a, _, _ in ins])


def _ffn_mid_fwd(up, cw, cb, name):
    _, L, C = up.shape
    n_tiles = L // ROW_TILE
    al = slice(None)

    def body(up_ref, cw_ref, cb_ref, a_ref):
        w0, w1, w2 = cw_ref[0:1, :], cw_ref[1:2, :], cw_ref[2:3, :]
        b = cb_ref[...]
        gate_ref, val_ref = up_ref.at[0], up_ref.at[1]

        def step(i, c):
            r0 = pl.multiple_of(i * ROW_TILE, ROW_TILE)
            rows = pl.ds(r0, ROW_TILE)
            gp = _rows_f32(gate_ref, rows, al)
            ext = jnp.concatenate([_prev_rows(gate_ref, r0, i, al), gp], axis=0)
            gate = gp * w2 + _back(ext, 1) * w1 + _back(ext, 2) * w0 + b
            a_ref[rows, :] = (gate * _sigmoid(gate) * _rows_f32(val_ref, rows, al)).astype(BF16)
            return c

        lax.fori_loop(0, n_tiles, step, 0)

    strip = lambda rows: pl.BlockSpec((rows, LANE), lambda j: (0, j))
    return pl.pallas_call(
        body, grid=(C // LANE,),
        in_specs=[pl.BlockSpec((2, L, LANE), lambda j: (0, 0, j)), strip(cw.shape[0]), strip(1)],
        out_specs=strip(L), out_shape=SDS((L, C), BF16), name=name, compiler_params=_cp("parallel"))(up, cw, cb)


def _ffn_mid_bwd(da, up, cw, cb, name):
    L, C = da.shape
    n_tiles = L // ROW_TILE
    al = slice(None)

    def body(da_ref, up_ref, cw_ref, cb_ref, dup_ref, st_ref):
        w0, w1, w2 = cw_ref[0:1, :], cw_ref[1:2, :], cw_ref[2:3, :]
        b = cb_ref[...]
        gate_ref, val_ref = up_ref.at[0], up_ref.at[1]

        def step(i, c):
            r0 = pl.multiple_of(i * ROW_TILE, ROW_TILE)
            rows = pl.ds(r0, ROW_TILE)
            gp = _rows_f32(gate_ref, rows, al)
            gpe = jnp.concatenate([_prev_rows(gate_ref, r0, i, al), gp, _next_rows(gate_ref, r0, i, n_tiles, al)],
                                  axis=0)
            g1, g2 = pltpu.roll(gpe, 1, axis=0), pltpu.roll(gpe, 2, axis=0)
            gate = (gpe * w2 + g1 * w1 + g2 * w0 + b)[HALO:]
            sg = _sigmoid(gate)
            da_e = jnp.concatenate([_rows_f32(da_ref, rows, al), _next_rows(da_ref, r0, i, n_tiles, al)], axis=0)
            val_e = jnp.concatenate([_rows_f32(val_ref, rows, al), _next_rows(val_ref, r0, i, n_tiles, al)], axis=0)
            dgate_e = da_e * val_e * _dsilu(gate, sg)
            dgate = dgate_e[:ROW_TILE]
            dgp = dgate * w2 + _fwd(dgate_e, 1) * w1 + _fwd(dgate_e, 2) * w0
            dup_ref[0, rows, :] = dgp.astype(BF16)
            dup_ref[1, rows, :] = (da_e * gate * sg)[:ROW_TILE].astype(BF16)
            s = lambda v: jnp.sum(v, axis=0, keepdims=True)
            t = slice(HALO, HALO + ROW_TILE)
            return (c[0] + s(dgate * g2[t]), c[1] + s(dgate * g1[t]), c[2] + s(dgate * gp), c[3] + s(dgate))

        z = jnp.zeros((1, LANE), F32)
        _store_rows(st_ref, lax.fori_loop(0, n_tiles, step, (z, z, z, z)))

    strip = lambda rows: pl.BlockSpec((rows, LANE), lambda j: (0, j))
    pair = pl.BlockSpec((2, L, LANE), lambda j: (0, 0, j))
    return pl.pallas_call(
        body, grid=(C // LANE,), in_specs=[strip(L), pair, strip(cw.shape[0]), strip(1)],
        out_specs=[pair, strip(8)], out_shape=[SDS((2, L, C), BF16), SDS((8, C), F32)],
        name=name, compiler_params=_cp("parallel"))(da, up, cw, cb)


def _sc_mid_fwd(bcv, cw, name):
    L = bcv.shape[0]
    C = bcv.shape[1] // 3
    n_tiles = L // ROW_TILE
    s0, s1, s2 = slice(0, LANE), slice(LANE, 2 * LANE), slice(2 * LANE, 3 * LANE)

    def body(x_ref, cw_ref, q_ref):
        w0, w1, w2 = cw_ref[0:1, :], cw_ref[1:2, :], cw_ref[2:3, :]

        def step(i, c):
            r0 = pl.multiple_of(i * ROW_TILE, ROW_TILE)
            rows = pl.ds(r0, ROW_TILE)
            p = _rows_f32(x_ref, rows, s1) * _rows_f32(x_ref, rows, s2)
            ext = jnp.concatenate([_prev_rows(x_ref, r0, i, s1) * _prev_rows(x_ref, r0, i, s2), p], axis=0)
            u = p * w2 + _back(ext, 1) * w1 + _back(ext, 2) * w0
            q_ref[rows, :] = (_rows_f32(x_ref, rows, s0) * u).astype(BF16)
            return c

        lax.fori_loop(0, n_tiles, step, 0)

    return _strip_call(body, L, C // LANE, [(bcv, 3 * LANE, 0), (cw, LANE, 0)],
                       [(SDS((L, C), BF16), LANE, 0)], name)[0]


def _sc_mid_bwd(dq, bcv, cw, name):
    L, C = dq.shape
    n_tiles = L // ROW_TILE
    s0, s1, s2, al = slice(0, LANE), slice(LANE, 2 * LANE), slice(2 * LANE, 3 * LANE), slice(None)

    def body(dq_ref, x_ref, cw_ref, dx_ref, st_ref):
        w0, w1, w2 = cw_ref[0:1, :], cw_ref[1:2, :], cw_ref[2:3, :]

        def step(i, c):
            r0 = pl.multiple_of(i * ROW_TILE, ROW_TILE)
            rows = pl.ds(r0, ROW_TILE)
            gb, gc, v = _rows_f32(x_ref, rows, s0), _rows_f32(x_ref, rows, s1), _rows_f32(x_ref, rows, s2)
            dq_v = _rows_f32(dq_ref, rows, al)
            p = gc * v
            pext = jnp.concatenate([_prev_rows(x_ref, r0, i, s1) * _prev_rows(x_ref, r0, i, s2), p], axis=0)
            p1, p2 = _back(pext, 1), _back(pext, 2)
            u = p * w2 + p1 * w1 + p2 * w0
            du = dq_v * gb
            du_n = _next_rows(dq_ref, r0, i, n_tiles, al) * _next_rows(x_ref, r0, i, n_tiles, s0)
            ext = jnp.concatenate([du, du_n], axis=0)
            dp = du * w2 + _fwd(ext, 1) * w1 + _fwd(ext, 2) * w0
            dx_ref[rows, s0] = (dq_v * u).astype(BF16)
            dx_ref[rows, s1] = (dp * v).astype(BF16)
            dx_ref[rows, s2] = (dp * gc).astype(BF16)
            s = lambda t: jnp.sum(t, axis=0, keepdims=True)
            return (c[0] + s(du * p2), c[1] + s(du * p1), c[2] + s(du * p))

        z = jnp.zeros((1, LANE), F32)
        _store_rows(st_ref, lax.fori_loop(0, n_tiles, step, (z, z, z)))

    return _strip_call(body, L, C // LANE, [(dq, LANE, 0), (bcv, 3 * LANE, 0), (cw, LANE, 0)],
                       [(SDS((L, 3 * C), BF16), 3 * LANE, 0), (SDS((8, C), F32), LANE, 0)], name)


def _ssd_conv_fwd(zx, cw, cb, col0, C, name):
    L = zx.shape[0]
    n_tiles = L // ROW_TILE
    al = slice(None)

    def body(x_ref, cw_ref, cb_ref, o_ref):
        w0, w1, w2, w3 = cw_ref[0:1, :], cw_ref[1:2, :], cw_ref[2:3, :], cw_ref[3:4, :]
        b = cb_ref[...]

        def step(i, c):
            r0 = pl.multiple_of(i * ROW_TILE, ROW_TILE)
            rows = pl.ds(r0, ROW_TILE)
            xv = x_ref[rows, :]
            ext = jnp.concatenate([_prev_rows(x_ref, r0, i, al), xv], axis=0)
            cv = xv * w3 + _back(ext, 1) * w2 + _back(ext, 2) * w1 + _back(ext, 3) * w0 + b
            o_ref[rows, :] = cv * _sigmoid(cv)
            return c

        lax.fori_loop(0, n_tiles, step, 0)

    return _strip_call(body, L, C // LANE, [(zx, LANE, col0 // LANE), (cw, LANE, 0), (cb, LANE, 0)],
                       [(SDS((L, C), F32), LANE, 0)], name)[0]


def _ssd_conv_bwd(dxbc, zx, cw, cb, col0, name):
    L, C = dxbc.shape
    n_tiles = L // ROW_TILE
    al = slice(None)

    def body(d_ref, x_ref, cw_ref, cb_ref, o_ref, st_ref):
        w0, w1, w2, w3 = cw_ref[0:1, :], cw_ref[1:2, :], cw_ref[2:3, :], cw_ref[3:4, :]
        b = cb_ref[...]

        def step(i, c):
            r0 = pl.multiple_of(i * ROW_TILE, ROW_TILE)
            rows = pl.ds(r0, ROW_TILE)
            xv = x_ref[rows, :]
            xe = jnp.concatenate([_prev_rows(x_ref, r0, i, al), xv, _next_rows(x_ref, r0, i, n_tiles, al)], axis=0)
            x1, x2, x3 = pltpu.roll(xe, 1, axis=0), pltpu.roll(xe, 2, axis=0), pltpu.roll(xe, 3, axis=0)
            cv = (xe * w3 + x1 * w2 + x2 * w1 + x3 * w0 + b)[HALO:]
            de = jnp.concatenate([d_ref[rows, :], _next_rows(d_ref, r0, i, n_tiles, al)], axis=0)
            dc_ext = de * _dsilu(cv, _sigmoid(cv))
            dc = dc_ext[:ROW_TILE]
            o_ref[rows, :] = (dc * w3 + _fwd(dc_ext, 1) * w2 + _fwd(dc_ext, 2) * w1 + _fwd(dc_ext, 3) * w0).astype(BF16)
            s = lambda t: jnp.sum(t, axis=0, keepdims=True)
            t = slice(HALO, HALO + ROW_TILE)
            return (c[0] + s(dc * x3[t]), c[1] + s(dc * x2[t]), c[2] + s(dc * x1[t]), c[3] + s(dc * xv), c[4] + s(dc))

        z = jnp.zeros((1, LANE), F32)
        _store_rows(st_ref, lax.fori_loop(0, n_tiles, step, (z, z, z, z, z)))

    return _strip_call(body, L, C // LANE,
                       [(dxbc, LANE, 0), (zx, LANE, col0 // LANE), (cw, LANE, 0), (cb, LANE, 0)],
                       [(SDS((L, C), BF16), LANE, 0), (SDS((8, C), F32), LANE, 0)], name)


def _scan_constants(n_heads):
    hw = n_heads * HEAD_DIM
    col = np.arange(hw)
    ind = (col[None, :] // HEAD_DIM == np.arange(LANE)[:, None]).astype(np.float32)
    gcol = np.arange(GROUP_W)
    itile = (gcol[None, :] % CHUNK == np.arange(CHUNK)[:, None]).astype(np.float32)
    trit = (gcol[None, :] % CHUNK <= np.arange(CHUNK)[:, None]).astype(np.float32)
    tril = np.tril(np.ones((CHUNK, CHUNK), np.float32))
    bmask = (gcol[:, None] // HEAD_DIM == gcol[None, :] // HEAD_DIM).astype(np.float32)
    return (jnp.asarray(ind, BF16), jnp.asarray(ind.T.copy(), BF16), jnp.asarray(itile), jnp.asarray(trit),
            jnp.asarray(tril, BF16), jnp.asarray(bmask))


def _softplus(x):
    return jnp.maximum(x, 0.0) + jnp.log(1.0 + jnp.exp(-jnp.abs(x)))


def _split3(x):
    hi = x.astype(BF16)
    r1 = x - hi.astype(F32)
    mid = r1.astype(BF16)
    return hi, mid, (r1 - mid.astype(F32)).astype(BF16)


def _dot_sel(x, sel, dims=None):
    if dims is None:
        mm = lambda p: jnp.dot(p, sel, preferred_element_type=F32)
    else:
        mm = lambda p: lax.dot_general(sel, p, dims, preferred_element_type=F32)
    hi, mid, lo = _split3(x)
    return (mm(lo) + mm(mid)) + mm(hi)


SEL_X = (((1,), (0,)), ((), ()))


def _group_terms(g, dt, cs, xbc_ref, ind_ref, itile, trit, bmask, d_inner):
    gl = slice(g * GROUP_W, (g + 1) * GROUP_W)
    indg = ind_ref[:, gl]
    csl = _dot_sel(cs, indg)
    dtx = _dot_sel(dt, indg)
    rr = _dot_sel(csl * itile, jnp.ones((CHUNK, CHUNK), BF16), SEL_X)
    lm = jnp.exp(jnp.where(trit > 0.0, csl - rr, -jnp.inf))
    xs = xbc_ref[:, gl]
    b = xbc_ref[:, d_inner + g * D_STATE: d_inner + (g + 1) * D_STATE]
    c = xbc_ref[:, d_inner + (N_GROUPS + g) * D_STATE: d_inner + (N_GROUPS + g + 1) * D_STATE]
    u = xs * dtx
    bb, cb = b.astype(BF16), c.astype(BF16)
    btile = jnp.concatenate([bb] * HEADS_PER_GROUP, axis=0)
    cbt = lax.dot_general(cb, btile, NT, preferred_element_type=F32)
    m = cbt * lm
    ub = u.astype(BF16)
    bdu = jnp.where(bmask > 0.0, jnp.concatenate([ub] * HEADS_PER_GROUP, axis=0), jnp.zeros((), BF16))
    c_last = csl[CHUNK - 1:CHUNK, :]
    return dict(gl=gl, indg=indg, csl=csl, dtx=dtx, lm=lm, xs=xs, bb=bb, cb=cb, u=u, btile=btile, m=m, bdu=bdu,
                e=jnp.exp(csl), dec=jnp.exp(c_last - csl), e_last=jnp.exp(c_last))


def _ssd_scan_fwd(zx, xbc, par, dexp, nw, consts, comm, name):
    L = xbc.shape[0]
    d_inner = dexp.shape[1]
    n_chunks = L // CHUNK
    dt_blk = zx.shape[1] // LANE - 1
    ind, ind_t, itile_c, trit_c, tril_c, bmask_c = consts

    def body(xbc_ref, z_ref, dtr_ref, par_ref, dexp_ref, nw_ref, ind_ref, itile_ref, trit_ref, tril_ref, bmask_ref,
             yn_ref, yf_ref, st_out_ref, st_ref):
        @pl.when(pl.program_id(0) == 0)
        def _():
            st_ref[...] = jnp.zeros_like(st_ref)

        dt = _softplus(dtr_ref[...] + par_ref[0:1, :])
        a_head = -jnp.exp(par_ref[1:2, :])
        cs = _dot_sel(dt * a_head, tril_ref[...], SEL_X)
        itile, trit, bmask = itile_ref[...], trit_ref[...], bmask_ref[...]
        for g in range(N_GROUPS):
            t = _group_terms(g, dt, cs, xbc_ref, ind_ref, itile, trit, bmask, d_inner)
            p = st_ref[g]
            st_out_ref[0, g] = p
            y = jnp.dot(t["m"].astype(BF16), t["bdu"], preferred_element_type=F32)
            y = y + jnp.dot(t["cb"], p.astype(BF16), preferred_element_type=F32) * t["e"]
            st_new = lax.dot_general(t["bb"], (t["u"] * t["dec"]).astype(BF16), TN, preferred_element_type=F32)
            st_ref[g] = p * t["e_last"] + st_new
            yf_ref[:, t["gl"]] = y + t["xs"] * dexp_ref[:, t["gl"]]
        z = z_ref[...]
        y2 = yf_ref[...] * (z * _sigmoid(z))
        yn_ref[...] = (y2 * _rsq(y2) * nw_ref[...]).astype(BF16)

    row = lambda c: (c, 0)
    fix = lambda c: (0, 0)
    cspec = lambda a: pl.BlockSpec(a.shape, fix)
    return _carrier_call(
        body, comm, n_chunks,
        in_specs=[pl.BlockSpec((CHUNK, xbc.shape[1]), row), pl.BlockSpec((CHUNK, d_inner), row),
                  pl.BlockSpec((CHUNK, LANE), lambda c: (c, dt_blk)), cspec(par), cspec(dexp), cspec(nw),
                  cspec(ind), cspec(itile_c), cspec(trit_c), cspec(tril_c), cspec(bmask_c)],
        out_specs=[pl.BlockSpec((CHUNK, d_inner), row), pl.BlockSpec((CHUNK, d_inner), row),
                   pl.BlockSpec((1, N_GROUPS, D_STATE, GROUP_W), lambda c: (c, 0, 0, 0))],
        out_shape=[SDS((L, d_inner), BF16), SDS((L, d_inner), F32),
                   SDS((n_chunks, N_GROUPS, D_STATE, GROUP_W), F32)],
        scratch_shapes=[pltpu.VMEM((N_GROUPS, D_STATE, GROUP_W), F32)],
        name=name, args=(xbc, zx, zx, par, dexp, nw, ind, itile_c, trit_c, tril_c, bmask_c))


def _ssd_scan_bwd(dyn, yf, zx, xbc, states, par, dexp, nw, consts, comm, name):
    L = xbc.shape[0]
    d_inner = dexp.shape[1]
    n_chunks = L // CHUNK
    dt_blk = zx.shape[1] // LANE - 1
    ind, ind_t, itile_c, trit_c, tril_c, bmask_c = consts
    hslices = [slice(r * HEAD_DIM, (r + 1) * HEAD_DIM) for r in range(HEADS_PER_GROUP)]

    def body(dyn_ref, yf_ref, z_ref, dtr_ref, xbc_ref, st_in_ref, par_ref, dexp_ref, nw_ref, ind_ref, indt_ref,
             itile_ref, trit_ref, tril_ref, bmask_ref,
             dz_ref, dxbc_ref, ddt_ref, dnw_ref, dpar_ref, dq_ref, dyf_ref):
        @pl.when(pl.program_id(0) == 0)
        def _():
            dq_ref[...] = jnp.zeros_like(dq_ref)
            dnw_ref[...] = jnp.zeros_like(dnw_ref)
            dpar_ref[...] = jnp.zeros_like(dpar_ref)

        z, yfv, dynv = z_ref[...], yf_ref[...], dyn_ref[...]
        sz = _sigmoid(z)
        y2 = yfv * (z * sz)
        r = _rsq(y2)
        y2h = y2 * r
        dnw_ref[...] += jnp.sum(dynv * y2h, axis=0, keepdims=True)
        dyg = dynv * nw_ref[...]
        dy2 = r * (dyg - y2h * jnp.mean(dyg * y2h, axis=-1, keepdims=True))
        dz_ref[...] = (dy2 * yfv * _dsilu(z, sz)).astype(BF16)
        dyf_ref[...] = dy2 * (z * sz)

        pre = dtr_ref[...] + par_ref[0:1, :]
        dt = _softplus(pre)
        a_head = -jnp.exp(par_ref[1:2, :])
        cs = _dot_sel(dt * a_head, tril_ref[...], SEL_X)
        itile, trit, bmask = itile_ref[...], trit_ref[...], bmask_ref[...]
        dcs = jnp.zeros((CHUNK, LANE), F32)
        dcs_last = jnp.zeros((1, LANE), F32)
        ddt_u = jnp.zeros((CHUNK, LANE), F32)
        d_skip = jnp.zeros((1, LANE), F32)
        rsum = lambda v: jnp.sum(v, axis=0, keepdims=True)
        row8 = lax.broadcasted_iota(jnp.int32, (8, GROUP_W), 0)
        for g in range(N_GROUPS):
            t = _group_terms(g, dt, cs, xbc_ref, ind_ref, itile, trit, bmask, d_inner)
            gl, m, lm, u, bb, cb, e, dec, xs = (t[k] for k in ("gl", "m", "lm", "u", "bb", "cb", "e", "dec", "xs"))
            indt = indt_ref[gl, :]
            dy = dyf_ref[:, gl]
            dyb = dy.astype(BF16)
            p = st_in_ref[0, g]
            pb = p.astype(BF16)
            q = dq_ref[g]
            qb = q.astype(BF16)
            big = lax.dot_general(m.astype(BF16), dyb, TN, preferred_element_type=F32)
            du = jnp.zeros((CHUNK, GROUP_W), F32)
            for rh in range(HEADS_PER_GROUP):
                du = du + big[hslices[rh], :] * bmask[rh * HEAD_DIM:rh * HEAD_DIM + 1, :]
            dm = lax.dot_general(dyb, t["bdu"], NT, preferred_element_type=F32)
            w = dm * m
            dgt = (dm * lm).astype(BF16)
            dc = jnp.dot(dgt, t["btile"], preferred_element_type=F32)
            db_big = lax.dot_general(dgt, cb, TN, preferred_element_type=F32)
            db = db_big[hslices[0], :] + db_big[hslices[1], :] + db_big[hslices[2], :] + db_big[hslices[3], :]
            cp = jnp.dot(cb, pb, preferred_element_type=F32)
            dye = dy * e
            dyeb = dye.astype(BF16)
            dc = dc + lax.dot_general(dyeb, pb, NT, preferred_element_type=F32)
            dp = lax.dot_general(cb, dyeb, TN, preferred_element_type=F32)
            x2 = dye * cp
            bq = jnp.dot(bb, qb, preferred_element_type=F32)
            ud = u * dec
            du = du + bq * dec
            db = db + lax.dot_general(ud.astype(BF16), qb, NT, preferred_element_type=F32)
            x1 = bq * ud
            dq_ref[g] = dp + t["e_last"] * q
            x3 = rsum(q * p) * t["e_last"]
            red = _dot_sel(jnp.concatenate([w + x2 - x1, du * xs, itile * rsum(w)], axis=0), indt)
            dcs = dcs + red[0:CHUNK] - red[2 * CHUNK:3 * CHUNK]
            ddt_u = ddt_u + red[CHUNK:2 * CHUNK]
            tail = _dot_sel(jnp.where(row8 == 0, rsum(x1) + x3, jnp.where(row8 == 1, rsum(dy * xs), 0.0)), indt)
            dcs_last = dcs_last + tail[0:1]
            d_skip = d_skip + tail[1:2]
            dxbc_ref[:, gl] = du * t["dtx"] + dy * dexp_ref[:, gl]
            dxbc_ref[:, d_inner + g * D_STATE: d_inner + (g + 1) * D_STATE] = db
            dxbc_ref[:, d_inner + (N_GROUPS + g) * D_STATE: d_inner + (N_GROUPS + g + 1) * D_STATE] = dc
        last = lax.broadcasted_iota(jnp.int32, (CHUNK, LANE), 0) == CHUNK - 1
        dcs = dcs + jnp.where(last, dcs_last, 0.0)
        da = _dot_sel(dcs, tril_ref[...], TN)
        ddt = da * a_head + ddt_u
        heads = lax.broadcasted_iota(jnp.int32, (CHUNK, LANE), 1) < d_inner // HEAD_DIM
        ddt_raw = jnp.where(heads, ddt * _sigmoid(pre), 0.0)
        ddt_ref[...] = ddt_raw.astype(BF16)
        dpar_ref[0:1, :] += rsum(ddt_raw)
        dpar_ref[1:2, :] += rsum(da * dt) * a_head
        dpar_ref[2:3, :] += d_skip

    rev = lambda c: (n_chunks - 1 - c, 0)
    fix = lambda c: (0, 0)
    cspec = lambda a: pl.BlockSpec(a.shape, fix)
    nx = xbc.shape[1]
    return _carrier_call(
        body, comm, n_chunks,
        in_specs=[pl.BlockSpec((CHUNK, d_inner), rev), pl.BlockSpec((CHUNK, d_inner), rev),
                  pl.BlockSpec((CHUNK, d_inner), rev), pl.BlockSpec((CHUNK, LANE), lambda c: (n_chunks - 1 - c, dt_blk)),
                  pl.BlockSpec((CHUNK, nx), rev),
                  pl.BlockSpec((1, N_GROUPS, D_STATE, GROUP_W), lambda c: (n_chunks - 1 - c, 0, 0, 0)),
                  cspec(par), cspec(dexp), cspec(nw), cspec(ind), cspec(ind_t), cspec(itile_c), cspec(trit_c),
                  cspec(tril_c), cspec(bmask_c)],
        out_specs=[pl.BlockSpec((CHUNK, d_inner), rev), pl.BlockSpec((CHUNK, nx), rev),
                   pl.BlockSpec((CHUNK, LANE), rev), pl.BlockSpec((1, d_inner), fix), pl.BlockSpec((8, LANE), fix)],
        out_shape=[SDS((L, d_inner), BF16), SDS((L, nx), F32), SDS((L, LANE), BF16), SDS((1, d_inner), F32),
                   SDS((8, LANE), F32)],
        scratch_shapes=[pltpu.VMEM((N_GROUPS, D_STATE, GROUP_W), F32), pltpu.VMEM((CHUNK, d_inner), F32)],
        name=name, args=(dyn, yf, zx, zx, xbc, states, par, dexp, nw, ind, ind_t, itile_c, trit_c, tril_c, bmask_c))


def _adamw(w, m, v, g, name):
    R, C = w.shape
    tr = R
    for cand in (256, 128, 64, 32, 16, 8):
        if R % cand == 0:
            tr = cand
            break

    def body(w_ref, m_ref, v_ref, g_ref, d_ref, mo_ref, vo_ref):
        gv = g_ref[...]
        mn = ADAM_B1 * m_ref[...] + (1.0 - ADAM_B1) * gv
        vn = ADAM_B2 * v_ref[...] + (1.0 - ADAM_B2) * (gv * gv)
        m_hat = mn / (1.0 - ADAM_B1 ** ADAM_STEP)
        v_hat = vn / (1.0 - ADAM_B2 ** ADAM_STEP)
        d_ref[...] = -ADAM_LR * (m_hat / (jnp.sqrt(v_hat) + ADAM_EPS) + ADAM_WD * w_ref[...])
        mo_ref[...] = mn
        vo_ref[...] = vn

    blk = pl.BlockSpec((tr, C), lambda i: (i, 0))
    return pl.pallas_call(
        body, grid=(R // tr,), in_specs=[blk] * 4, out_specs=[blk] * 3, out_shape=[SDS((R, C), F32)] * 3,
        name=name, compiler_params=_cp("parallel"))(w, m, v, g)


def _sum_slots(parts, name):
    n, R, C = parts.shape
    tr = 128 if R % 128 == 0 else R

    def body(p_ref, o_ref):
        acc = p_ref[0]
        for k in range(1, n):
            acc = acc + p_ref[k]
        o_ref[...] = acc

    return pl.pallas_call(
        body, grid=(R // tr,), in_specs=[pl.BlockSpec((n, tr, C), lambda i: (0, i, 0))],
        out_specs=pl.BlockSpec((tr, C), lambda i: (i, 0)), out_shape=SDS((R, C), F32),
        name=name, compiler_params=_cp("parallel"))(parts)


def _add_core_halves(where, g, r, name):
    _, n, a, b = g.shape
    ta = a // 2

    def body(w_ref, g_ref, r_ref, o_ref):
        o_ref[...] = (g_ref[...].astype(F32) + r_ref[...].astype(F32)).astype(BF16)

    blk = lambda f: pl.BlockSpec((None, None, ta, b), f)
    mine = lambda s, l, w: (s, l, 0, 0)
    return pl.pallas_call(
        body, grid_spec=pltpu.PrefetchScalarGridSpec(
            num_scalar_prefetch=1, grid=(4, n),
            in_specs=[blk(lambda s, l, w: (s, l, w[1], 0)), blk(mine)], out_specs=blk(mine)),
        out_shape=SDS((4, n, ta, b), BF16), name=name,
        compiler_params=_cp("parallel", "parallel"))(where, g, r)


def _sum_shard(where, g, r, rr, name):
    _, n, a, b = g.shape
    ta = a // 2

    def body(w_ref, g_ref, r_ref, rr_ref, o_ref):
        f = lambda v: v.astype(F32)
        o_ref[...] = (((f(g_ref[...]) + f(r_ref[...])) + f(rr_ref[0])) + f(rr_ref[1])) + f(rr_ref[2])

    return pl.pallas_call(
        body, grid_spec=pltpu.PrefetchScalarGridSpec(
            num_scalar_prefetch=1, grid=(n,),
            in_specs=[pl.BlockSpec((None, None, ta, b), lambda l, w: (w[0], l, w[1], 0)),
                      pl.BlockSpec((None, None, ta, b), lambda l, w: (w[0], l, 0, 0)),
                      pl.BlockSpec((3, None, ta, b), lambda l, w: (0, l, 0, 0))],
            out_specs=pl.BlockSpec((None, ta, b), lambda l, w: (l, w[1], 0))),
        out_shape=SDS((n, a, b), F32), name=name,
        compiler_params=_cp("parallel"))(where, g, r, rr)


def _me():
    return lax.axis_index("x"), lax.axis_index("y"), lax.axis_index("c")


def _chip_peers(x, y):
    return [(1 - x, y), (x, 1 - y), (1 - x, 1 - y)]


def _rcopy(src, dst, send_sems, recv_sems, k, to):
    return pltpu.make_async_remote_copy(src_ref=src, dst_ref=dst, send_sem=send_sems.at[k], recv_sem=recv_sems.at[k],
                                        device_id=to, device_id_type=MESH)


def _row_half(ref, c, lead=()):
    a = ref.shape[len(lead) + 1]
    return ref.at[(*lead, slice(None), pl.ds(c * (a // 2), a // 2))]


class _Comm(NamedTuple):
    ins: list
    out_shapes: list
    n_sems: int
    start: Callable
    finish: Callable


def _sem_scratch(comm):
    return [pltpu.SemaphoreType.DMA((comm.n_sems,)), pltpu.SemaphoreType.DMA((comm.n_sems,))]


def _run_comm(comm, name):
    n_in, n_out = len(comm.ins), len(comm.out_shapes)

    def body(*refs):
        ins, outs, sems = refs[:n_in], refs[n_in:n_in + n_out], refs[n_in + n_out:]
        comm.start(ins, outs, *sems)
        comm.finish(ins, outs, *sems)

    return pl.pallas_call(body, in_specs=[ANY] * n_in, out_specs=[ANY] * n_out, out_shape=comm.out_shapes,
                          scratch_shapes=_sem_scratch(comm), name=name)(*comm.ins)


def _carrier_call(compute, comm, n_steps, in_specs, out_specs, out_shape, scratch_shapes, name, args):
    if comm is None:
        return pl.pallas_call(compute, grid=(n_steps,), in_specs=in_specs, out_specs=out_specs, out_shape=out_shape,
                              scratch_shapes=scratch_shapes, name=name, compiler_params=_cp("arbitrary"))(*args), []
    n_in, n_out, n_scr = len(in_specs), len(out_specs), len(scratch_shapes)
    n_ci, n_co = len(comm.ins), len(comm.out_shapes)

    def body(*refs):
        ins, cins = refs[:n_in], refs[n_in:n_in + n_ci]
        o = n_in + n_ci
        outs, couts = refs[o:o + n_out], refs[o + n_out:o + n_out + n_co]
        s = o + n_out + n_co
        scratch, sems = refs[s:s + n_scr], refs[s + n_scr:]

        @pl.when(pl.program_id(0) == 0)
        def _():
            comm.start(cins, couts, *sems)

        compute(*ins, *outs, *scratch)

        @pl.when(pl.program_id(0) == n_steps - 1)
        def _():
            comm.finish(cins, couts, *sems)

    res = pl.pallas_call(
        body, grid=(n_steps,), in_specs=list(in_specs) + [ANY] * n_ci, out_specs=list(out_specs) + [ANY] * n_co,
        out_shape=list(out_shape) + list(comm.out_shapes), scratch_shapes=list(scratch_shapes) + _sem_scratch(comm),
        name=name, compiler_params=_cp("arbitrary"))(*args, *comm.ins)
    return res[:n_out], res[n_out:]


def _allgather_plan(mine, small=None):
    n = len(mine)
    ins = list(mine) + ([] if small is None else [small])
    out_shapes = [SDS((4,) + m.shape, BF16) for m in mine] + ([] if small is None else [SDS((4,) + small.shape, F32)])
    sem = lambda t, k: 6 * t + k

    def first_copies(ins_r, outs_r, send, recv):
        x, y, c = _me()
        q = 2 * x + y
        cps = []
        for j, chip in enumerate(_chip_peers(x, y)):
            for t in range(n):
                cps.append(_rcopy(_row_half(ins_r[t], c), _row_half(outs_r[t], c, (q,)), send, recv, sem(t, j),
                                  (*chip, c)))
            if small is not None:
                cps.append(_rcopy(ins_r[n], outs_r[n].at[q], send, recv, sem(n, j), (*chip, c)))
        return cps

    def start(ins_r, outs_r, send, recv):
        for cp in first_copies(ins_r, outs_r, send, recv):
            cp.start()

    def finish(ins_r, outs_r, send, recv):
        x, y, c = _me()
        sib = (x, y, 1 - c)
        chips = _chip_peers(x, y)
        passed = []
        for j, (px, py) in enumerate(chips):
            for t in range(n):
                blk = _row_half(outs_r[t], c, (2 * px + py,))
                _rcopy(blk, blk, send, recv, sem(t, j), sib).wait_recv()
                cp = _rcopy(blk, blk, send, recv, sem(t, 3 + j), sib)
                cp.start()
                passed.append(cp)
        for j, (px, py) in enumerate(chips):
            for t in range(n):
                blk = _row_half(outs_r[t], 1 - c, (2 * px + py,))
                _rcopy(blk, blk, send, recv, sem(t, 3 + j), sib).wait_recv()
            if small is not None:
                sblk = outs_r[n].at[2 * px + py]
                _rcopy(sblk, sblk, send, recv, sem(n, j), sib).wait_recv()
        for cp in first_copies(ins_r, outs_r, send, recv) + passed:
            cp.wait_send()

    return _Comm(ins, out_shapes, 6 * n + (0 if small is None else 3), start, finish)


def _grads_to_sibling_plan(grads):
    n = len(grads)

    def copies(ins_r, outs_r, send, recv):
        x, y, c = _me()
        return [_rcopy(_row_half(ins_r[t], 1 - c, (slice(None),)), outs_r[t], send, recv, t, (x, y, 1 - c))
                for t in range(n)]

    def start(*a):
        for cp in copies(*a):
            cp.start()

    def finish(*a):
        for cp in copies(*a):
            cp.wait()

    out_shapes = [SDS((4, g.shape[1], g.shape[2] // 2, g.shape[3]), BF16) for g in grads]
    return _Comm(list(grads), out_shapes, n, start, finish)


def _grads_to_chips_plan(psums):
    n = len(psums)

    def copies(ins_r, outs_r, send, recv):
        x, y, c = _me()
        return [_rcopy(ins_r[t].at[2 * px + py], outs_r[t].at[j], send, recv, 3 * t + j, (px, py, c))
                for j, (px, py) in enumerate(_chip_peers(x, y)) for t in range(n)]

    def start(*a):
        for cp in copies(*a):
            cp.start()

    def finish(*a):
        for cp in copies(*a):
            cp.wait()

    return _Comm(list(psums), [SDS((3,) + p.shape[1:], BF16) for p in psums], 3 * n, start, finish)


def _swap_halves(sums):
    n = len(sums)

    def body(*refs):
        out_refs = refs[n:2 * n]
        send_sems, recv_sems = refs[2 * n:]
        x, y, c = _me()
        sib = (x, y, 1 - c)
        cps = [_rcopy(_row_half(out_refs[t], c), _row_half(out_refs[t], c), send_sems, recv_sems, t, sib)
               for t in range(n)]
        for cp in cps:
            cp.start()
        for t in range(n):
            other = _row_half(out_refs[t], 1 - c)
            _rcopy(other, other, send_sems, recv_sems, t, sib).wait_recv()
        for cp in cps:
            cp.wait_send()

    return pl.pallas_call(
        body, in_specs=[ANY] * n, out_specs=[ANY] * n, out_shape=[SDS(s.shape, F32) for s in sums],
        input_output_aliases={t: t for t in range(n)},
        scratch_shapes=[pltpu.SemaphoreType.DMA((n,)), pltpu.SemaphoreType.DMA((n,))],
        name="swap_halves")(*sums)


def _allgather_small(part):
    def body(p_ref, out_ref, send_sems, recv_sems, local_sem):
        x, y, c = _me()
        me = 4 * x + 2 * y + c
        own = pltpu.make_async_copy(p_ref, out_ref.at[me], local_sem.at[0])
        own.start()
        sends = []
        for k in range(1, 8):
            fx, fy, fc = (k >> 2) & 1, (k >> 1) & 1, k & 1
            to = (x ^ fx, y ^ fy, c ^ fc)
            sends.append(_rcopy(p_ref, out_ref.at[me], send_sems, recv_sems, k - 1, to))
        for cp in sends:
            cp.start()
        for k in range(1, 8):
            slot = out_ref.at[me ^ k]
            _rcopy(slot, slot, send_sems, recv_sems, k - 1, (x, y, c)).wait_recv()
        for cp in sends:
            cp.wait_send()
        own.wait()

    return pl.pallas_call(
        body, in_specs=[ANY], out_specs=ANY, out_shape=SDS((8,) + part.shape, F32),
        scratch_shapes=[pltpu.SemaphoreType.DMA((7,)), pltpu.SemaphoreType.DMA((7,)), pltpu.SemaphoreType.DMA((1,))],
        name="allgather_small")(part)


BIG = (("ssd_w_in", 2), ("ssd_w_out", 1), ("sc_w_in", 2), ("sc_w_out", 1), ("ffn_w_up", 2), ("ffn_w_down", 1))


def _to_shards(full, axis):
    A, B = full.shape
    if axis == 2:
        return full.reshape(A, 4, B // 4).transpose(1, 0, 2)
    return full.reshape(4, A // 4, B)


def _from_shards(shards, axis):
    _, a, b = shards.shape
    if axis == 2:
        return shards.transpose(1, 0, 2).reshape(a, 4 * b)
    return shards.reshape(4 * a, b)


def _interleave(w, parts):
    lead, n = w.shape[:-1], w.shape[-1]
    return w.reshape(*lead, parts, n // (parts * LANE), LANE).swapaxes(-2, -3).reshape(*lead, n)


def _deinterleave(w, parts):
    lead, n = w.shape[:-1], w.shape[-1]
    return w.reshape(*lead, n // (parts * LANE), parts, LANE).swapaxes(-2, -3).reshape(*lead, n)


def _pack_rows(vectors, width, row_multiple):
    flat = jnp.concatenate(vectors, axis=-1)
    n = flat.shape[-1]
    unit = width * row_multiple
    total = -(-n // unit) * unit
    flat = jnp.pad(flat, [(0, 0)] * (flat.ndim - 1) + [(0, total - n)])
    return flat.reshape(*flat.shape[:-1], total // width, width)


def _unpack(flat, shapes):
    out, off = [], 0
    for s in shapes:
        n = int(np.prod(s))
        out.append(flat[..., off:off + n].reshape(*flat.shape[:-1], *s))
        off += n
    return out


def _memo(fn):
    cache = {}

    def wrapped(k):
        if k not in cache:
            cache[k] = fn(k)
        return cache[k]

    return wrapped


def _row(v):
    return v.reshape(1, -1)


def _pad_rows(w, rows=8):
    return jnp.pad(w, ((0, rows - w.shape[0]), (0, 0)))


def _ffn_fwd(x, g_pre, g_post, w_up, cw, cb, w_down, tag, comm_up=None, comm_down=None):
    (up, hn), got_up = _ffn_up(x, g_pre, w_up, "ffn_up" + tag, comm_up)
    a = _ffn_mid_fwd(up, cw, cb, "ffn_mid_fwd" + tag)
    (f, x_new), got_down = _matmul_norm_res(a, w_down, x, g_post, "ffn_down" + tag, comm_down)
    return x_new, (x, hn, up, a, f), got_up, got_down


def _ffn_bwd(dx, saved, g_pre, g_post, w_up, cw, cb, w_down, tag):
    x, hn, up, a, f = saved
    df, dg_post = _postnorm_bwd(dx, f, g_post, "ffn_post_bwd" + tag)
    da = _matmul_nt(df, w_down, BF16, "ffn_down_dx" + tag)
    dw_down = _matmul_tn(a, df, w_down.shape[0] // 2, w_down.shape[1], "ffn_down_dw" + tag)
    dup, stats = _ffn_mid_bwd(da, up, cw, cb, "ffn_mid_bwd" + tag)
    dx_in, dg_pre = _ffn_up_dx(dup, w_up, x, g_pre, dx, "ffn_up_dx" + tag)
    dw_up = _ffn_up_dw(hn, dup, "ffn_up_dw" + tag)
    return dx_in, dict(g_pre=dg_pre, g_post=dg_post, w_up=dw_up, w_down=dw_down, cw=stats[0:3], cb=stats[3])


def _sc_fwd(x, g_pre, g_post, w_in, cw, w_out, tag):
    bcv, hn = _norm_matmul(x, g_pre, w_in, BF16, "sc_in" + tag)
    q = _sc_mid_fwd(bcv, cw, "sc_mid_fwd" + tag)
    (m, x_new), _ = _matmul_norm_res(q, w_out, x, g_post, "sc_out" + tag)
    return x_new, (x, hn, bcv, q, m)


def _sc_bwd(dx, saved, g_pre, g_post, w_in, cw, w_out, tag):
    x, hn, bcv, q, m = saved
    dm, dg_post = _postnorm_bwd(dx, m, g_post, "sc_post_bwd" + tag)
    dq = _matmul_nt(dm, w_out, BF16, "sc_out_dx" + tag)
    dw_out = _matmul_tn(q, dm, w_out.shape[0], w_out.shape[1], "sc_out_dw" + tag)
    dbcv, stats = _sc_mid_bwd(dq, bcv, cw, "sc_mid_bwd" + tag)
    dx_in, dg_pre = _matmul_nt_prenorm_bwd(dbcv, w_in, x, g_pre, dx, "sc_in_dx" + tag)
    dw_in = _matmul_tn(hn, dbcv, w_in.shape[0], w_in.shape[1] // 3, "sc_in_dw" + tag)
    return dx_in, dict(g_pre=dg_pre, g_post=dg_post, w_in=dw_in, w_out=dw_out, cw=stats[0:3])


def _ssd_fwd(x, g_pre, g_post, w_in, cw, cb, par, dexp, nw, w_out, consts, comm, tag):
    d_inner = dexp.shape[1]
    zx, hn = _norm_matmul(x, g_pre, w_in, F32, "ssd_in" + tag)
    xbc = _ssd_conv_fwd(zx, cw, cb, d_inner, cw.shape[1], "ssd_conv_fwd" + tag)
    (yn, yf, states), carried = _ssd_scan_fwd(zx, xbc, par, dexp, nw, consts, comm, "ssd_scan_fwd" + tag)
    (m, x_new), _ = _matmul_norm_res(yn, w_out, x, g_post, "ssd_out" + tag)
    return x_new, (x, hn, zx, xbc, yn, yf, states, m), carried


def _ssd_bwd(dx, saved, g_pre, g_post, w_in, cw, cb, par, dexp, nw, w_out, consts, comm, tag):
    x, hn, zx, xbc, yn, yf, states, m = saved
    d_inner = dexp.shape[1]
    dm, dg_post = _postnorm_bwd(dx, m, g_post, "ssd_post_bwd" + tag)
    dyn = _matmul_nt(dm, w_out, F32, "ssd_out_dx" + tag)
    dw_out = _matmul_tn(yn, dm, w_out.shape[0] // 2, w_out.shape[1], "ssd_out_dw" + tag)
    (dz, dxbc, ddt, dnw, dpar), carried = _ssd_scan_bwd(dyn, yf, zx, xbc, states, par, dexp, nw, consts, comm,
                                                        "ssd_scan_bwd" + tag)
    dxp, stats = _ssd_conv_bwd(dxbc, zx, cw, cb, d_inner, "ssd_conv_bwd" + tag)
    dzx = jnp.concatenate([dz, dxp, ddt], axis=1)
    dx_in, dg_pre = _matmul_nt_prenorm_bwd(dzx, w_in, x, g_pre, dx, "ssd_in_dx" + tag)
    dw_in = _matmul_tn(hn, dzx, w_in.shape[0], w_in.shape[1] // 7, "ssd_in_dw" + tag)
    n_heads = d_inner // HEAD_DIM
    grads = dict(g_pre=dg_pre, g_post=dg_post, w_in=dw_in, w_out=dw_out, cw=stats[0:4], cb=stats[4],
                 dt_bias=dpar[0, :n_heads], a_log=dpar[1, :n_heads], d=dpar[2, :n_heads], nw=dnw[0])
    return dx_in, grads, carried


def kernel(x, mix_pre_g, mix_post_g, ffn_pre_g, ffn_post_g, ssd_w_in, ssd_conv_w, ssd_conv_b, ssd_dt_bias, ssd_A_log, ssd_D, ssd_norm_w, ssd_w_out, sc_w_in, sc_conv_w, sc_w_out, ffn_w_up, ffn_conv_w, ffn_conv_b, ffn_w_down, loss_target, m_mix_pre_g, m_mix_post_g, m_ffn_pre_g, m_ffn_post_g, m_ssd_w_in, m_ssd_conv_w, m_ssd_conv_b, m_ssd_dt_bias, m_ssd_A_log, m_ssd_D, m_ssd_norm_w, m_ssd_w_out, m_sc_w_in, m_sc_conv_w, m_sc_w_out, m_ffn_w_up, m_ffn_conv_w, m_ffn_conv_b, m_ffn_w_down, v_mix_pre_g, v_mix_post_g, v_ffn_pre_g, v_ffn_post_g, v_ssd_w_in, v_ssd_conv_w, v_ssd_conv_b, v_ssd_dt_bias, v_ssd_A_log, v_ssd_D, v_ssd_norm_w, v_ssd_w_out, v_sc_w_in, v_sc_conv_w, v_sc_w_out, v_ffn_w_up, v_ffn_conv_w, v_ffn_conv_b, v_ffn_w_down):
    names = ["mix_pre_g", "mix_post_g", "ffn_pre_g", "ffn_post_g", "ssd_w_in", "ssd_conv_w", "ssd_conv_b",
             "ssd_dt_bias", "ssd_A_log", "ssd_D", "ssd_norm_w", "ssd_w_out", "sc_w_in", "sc_conv_w", "sc_w_out",
             "ffn_w_up", "ffn_conv_w", "ffn_conv_b", "ffn_w_down"]
    env = locals()
    wts = {n: env[n] for n in names}
    mom = {n: env["m_" + n] for n in names}
    var = {n: env["v_" + n] for n in names}

    depth, d_model = mix_pre_g.shape
    n_ssd, n_heads = ssd_dt_bias.shape
    n_sc = sc_conv_w.shape[0]
    d_inner = n_heads * HEAD_DIM
    conv_dim = d_inner + 2 * N_GROUPS * D_STATE
    ssd_in_dim = d_inner + conv_dim + n_heads
    ssd_in_pad = d_inner + conv_dim + LANE
    q_chip = 2 * lax.axis_index("x") + lax.axis_index("y")
    core = lax.axis_index("c")

    assert depth == 4 and n_ssd == 2 and n_sc == 2, "the exchange schedule is written for this trunk"
    ssd_items = lambda j: [("ssd_w_in", j), ("ssd_w_out", j)]
    sc_items = lambda j: [("sc_w_in", j), ("sc_w_out", j)]
    ffn_items = lambda i: [("ffn_w_up", i), ("ffn_w_down", i)]
    gather_first = ssd_items(0)
    gather_in_scan = {0: ffn_items(0) + sc_items(0) + ffn_items(1), 2: ffn_items(2) + sc_items(1) + ffn_items(3)}
    gather_in_ffn = {0: ([("ssd_w_in", 1)], [("ssd_w_out", 1)])}
    reduce_in_scan = {2: ffn_items(3) + sc_items(1) + ffn_items(2),
                      0: ssd_items(1) + ffn_items(1) + sc_items(0) + ffn_items(0)}
    reduce_last = ssd_items(0)
    axis_of = dict(BIG)
    own_slot = lambda buf, own: lax.dynamic_update_index_in_dim(buf, own, q_chip, 0)
    gathered = {}

    def gather_plan(items, small=None):
        mine = [wts[n][layer:layer + 1].astype(BF16) for n, layer in items]
        return mine, _allgather_plan(mine, small)

    def gather_done(items, mine, results):
        for item, own, buf in zip(items, mine, results):
            gathered[item] = own_slot(buf, own)[:, 0]

    def full(n, layer):
        return _from_shards(gathered[(n, layer)], axis_of[n])

    conv_names = ["ssd_conv_w", "sc_conv_w", "ffn_conv_w"]
    conv_shapes = [wts[n].shape for n in conv_names]
    small_mine = _pack_rows([wts[n].reshape(-1) for n in conv_names], LANE, 8)
    mine, plan = gather_plan(gather_first, small_mine)
    *results, small_all = _run_comm(plan, "allgather_first")
    gather_done(gather_first, mine, results)
    small_all = own_slot(small_all, small_mine)
    conv_full = {}
    for n, f, s in zip(conv_names, _unpack(small_all.reshape(4, -1), conv_shapes), conv_shapes):
        conv_full[n] = f.transpose(1, 2, 0, 3).reshape(s[0], s[1], 4 * s[2])

    consts = _scan_constants(n_heads)

    def ssd_args(j):
        par = jnp.zeros((8, LANE), F32).at[0, :n_heads].set(ssd_dt_bias[j]).at[1, :n_heads].set(ssd_A_log[j])
        dexp = jnp.repeat(ssd_D[j], HEAD_DIM).reshape(1, d_inner)
        w_in = jnp.pad(full("ssd_w_in", j), ((0, 0), (0, ssd_in_pad - ssd_in_dim)))
        return (w_in, _pad_rows(conv_full["ssd_conv_w"][j]), _row(ssd_conv_b[j]), par, dexp,
                _row(ssd_norm_w[j]), full("ssd_w_out", j), consts)

    def sc_args(j):
        return (_interleave(full("sc_w_in", j), 3), _pad_rows(conv_full["sc_conv_w"][j]), full("sc_w_out", j))

    def ffn_args(i):
        return (gathered[("ffn_w_up", i)], _pad_rows(conv_full["ffn_conv_w"][i]), _row(ffn_conv_b[i]),
                full("ffn_w_down", i))

    ssd_args, sc_args, ffn_args = _memo(ssd_args), _memo(sc_args), _memo(ffn_args)

    h = x[0]
    saved = []
    for i in range(depth):
        j = i // 2
        gp, gq = _row(mix_pre_g[i]), _row(mix_post_g[i])
        if i % 2 == 0:
            mine, plan = gather_plan(gather_in_scan[i])
            h, sv, results = _ssd_fwd(h, gp, gq, *ssd_args(j), plan, tag="")
            gather_done(gather_in_scan[i], mine, results)
        else:
            h, sv = _sc_fwd(h, gp, gq, *sc_args(j), tag="")
        items_up, items_down = gather_in_ffn.get(i, ([], []))
        mine_up, plan_up = gather_plan(items_up) if items_up else ([], None)
        mine_down, plan_down = gather_plan(items_down) if items_down else ([], None)
        h, sv2, got_up, got_down = _ffn_fwd(h, _row(ffn_pre_g[i]), _row(ffn_post_g[i]), *ffn_args(i), tag="",
                                            comm_up=plan_up, comm_down=plan_down)
        gather_done(items_up, mine_up, got_up)
        gather_done(items_down, mine_down, got_down)
        saved.append((sv, sv2))
    dh, loss_part = _loss_head(h, loss_target[0], "loss_head")

    mix_grads, ffn_grads = [None] * depth, [None] * depth
    where = jnp.stack([q_chip, core]).astype(jnp.int32)

    def shard_grad(n, layer):
        if n == "ffn_w_up":
            g = ffn_grads[layer]["w_up"]
        elif n == "ffn_w_down":
            g = _to_shards(ffn_grads[layer]["w_down"], 1)
        elif n == "ssd_w_in":
            g = _to_shards(mix_grads[2 * layer]["w_in"][:, :ssd_in_dim], 2)
        elif n == "ssd_w_out":
            g = _to_shards(mix_grads[2 * layer]["w_out"], 1)
        elif n == "sc_w_in":
            g = _to_shards(_deinterleave(mix_grads[2 * layer + 1]["w_in"], 3), 2)
        else:
            g = _to_shards(mix_grads[2 * layer + 1]["w_out"], 1)
        return g[:, None]

    def reduce_begin(items, tag):
        by_shard = [shard_grad(n, layer) for n, layer in items]
        from_sib = _run_comm(_grads_to_sibling_plan(by_shard), "grads_to_sibling" + tag)
        chip_sums = [_add_core_halves(where, g, r, "add_core_halves_%s%d" % item)
                     for item, g, r in zip(items, by_shard, from_sib)]
        return by_shard, from_sib, _grads_to_chips_plan(chip_sums)

    def reduce_end(items, by_shard, from_sib, from_chips):
        return [_sum_shard(where, g, r, rr, "sum_shard_%s%d" % item)
                for item, g, r, rr in zip(items, by_shard, from_sib, from_chips)]

    reduced_items, half_sums = [], []
    for i in reversed(range(depth)):
        j = i // 2
        sv, sv2 = saved[i]
        dh, ffn_grads[i] = _ffn_bwd(dh, sv2, _row(ffn_pre_g[i]), _row(ffn_post_g[i]), *ffn_args(i), tag="")
        gp, gq = _row(mix_pre_g[i]), _row(mix_post_g[i])
        if i % 2 == 0:
            items = reduce_in_scan[i]
            by_shard, from_sib, plan = reduce_begin(items, "_%d" % i)
            dh, mix_grads[i], from_chips = _ssd_bwd(dh, sv, gp, gq, *ssd_args(j), plan, tag="")
            reduced_items += items
            half_sums += reduce_end(items, by_shard, from_sib, from_chips)
        else:
            w_in, scw, w_out = sc_args(j)
            dh, mix_grads[i] = _sc_bwd(dh, sv, gp, gq, w_in, scw, w_out, tag="")
    grad_x = dh[None]
    by_shard, from_sib, plan = reduce_begin(reduce_last, "_last")
    from_chips = _run_comm(plan, "grads_to_chips_last")
    reduced_items += reduce_last
    half_sums += reduce_end(reduce_last, by_shard, from_sib, from_chips)
    pieces = {}
    for (n, layer), g in zip(reduced_items, _swap_halves(half_sums)):
        pieces.setdefault(n, []).append((layer, g))
    big_grads = {n: jnp.concatenate([g for _, g in sorted(p, key=lambda t: t[0])]) for n, p in pieces.items()}
    ssd_l = [mix_grads[i] for i in range(0, depth, 2)]
    sc_l = [mix_grads[i] for i in range(1, depth, 2)]
    stack = lambda layers, k: jnp.stack([g[k] for g in layers])

    small_names = ["mix_pre_g", "mix_post_g", "ffn_pre_g", "ffn_post_g", "ssd_conv_w", "ssd_conv_b", "ssd_dt_bias",
                   "ssd_A_log", "ssd_D", "ssd_norm_w", "sc_conv_w", "ffn_conv_w", "ffn_conv_b"]
    small_local = {
        "mix_pre_g": jnp.concatenate([g["g_pre"] for g in mix_grads]),
        "mix_post_g": jnp.concatenate([g["g_post"] for g in mix_grads]),
        "ffn_pre_g": jnp.concatenate([g["g_pre"] for g in ffn_grads]),
        "ffn_post_g": jnp.concatenate([g["g_post"] for g in ffn_grads]),
        "ssd_conv_w": stack(ssd_l, "cw"), "ssd_conv_b": stack(ssd_l, "cb"), "ssd_dt_bias": stack(ssd_l, "dt_bias"),
        "ssd_A_log": stack(ssd_l, "a_log"), "ssd_D": stack(ssd_l, "d"), "ssd_norm_w": stack(ssd_l, "nw"),
        "sc_conv_w": stack(sc_l, "cw"), "ffn_conv_w": stack(ffn_grads, "cw"), "ffn_conv_b": stack(ffn_grads, "cb"),
    }
    small_full_shapes = [small_local[n].shape for n in small_names]
    spack = _pack_rows([small_local[n].reshape(-1) for n in small_names] + [loss_part.reshape(-1)], LANE, 8)
    stotal = _sum_slots(_allgather_small(spack), "sum_small").reshape(-1)
    small_grads = dict(zip(small_names, _unpack(stotal, small_full_shapes)))
    loss = stotal[sum(int(np.prod(s)) for s in small_full_shapes)]
    for n in conv_names:
        width = wts[n].shape[-1]
        small_grads[n] = lax.dynamic_slice_in_dim(small_grads[n], q_chip * width, width, axis=2)

    grads, delta, new_m, new_v = {}, {}, {}, {}
    for n, _ in BIG:
        s = wts[n].shape
        two_d = lambda a: a.reshape(-1, s[-1])
        grads[n] = big_grads[n]
        d, mn, vn = _adamw(two_d(wts[n]), two_d(mom[n]), two_d(var[n]), two_d(grads[n]), "adamw_" + n)
        delta[n], new_m[n], new_v[n] = d.reshape(s), mn.reshape(s), vn.reshape(s)
    small_shapes = [wts[n].shape for n in small_names]
    pk = lambda d: _pack_rows([d[n].reshape(-1) for n in small_names], LANE, 8)
    for n in small_names:
        grads[n] = small_grads[n].reshape(wts[n].shape)
    d, mn, vn = _adamw(pk(wts), pk(mom), pk(var), pk(grads), "adamw_small")
    for out, packed in ((delta, d), (new_m, mn), (new_v, vn)):
        out.update(zip(small_names, _unpack(packed.reshape(-1), small_shapes)))

    return (loss, grad_x, *[grads[n] for n in names], *[delta[n] for n in names], *[new_m[n] for n in names],
            *[new_v[n] for n in names])
```

```python
from typing import Callable, NamedTuple

import jax
import jax.numpy as jnp
import numpy as np
from jax import lax
from jax.experimental import pallas as pl
from jax.experimental.pallas import tpu as pltpu

F32 = jnp.float32
BF16 = jnp.bfloat16
SDS = jax.ShapeDtypeStruct
MESH = pl.DeviceIdType.MESH
ANY = pl.BlockSpec(memory_space=pl.ANY)

EPS = 1e-6
CHUNK = 64
HEAD_DIM = 64
N_GROUPS = 8
D_STATE = 128
HEADS_PER_GROUP = 4
GROUP_W = HEADS_PER_GROUP * HEAD_DIM
LANE = 128
ROW_TILE = 128
HALO = 8
VMEM_LIMIT = 56 * 1024 * 1024

ADAM_LR = 0.001
ADAM_B1 = 0.9
ADAM_B2 = 0.999
ADAM_EPS = 1e-08
ADAM_WD = 0.01
ADAM_STEP = 10

NT = (((1,), (1,)), ((), ()))
TN = (((0,), (0,)), ((), ()))


def _cp(*sem):
    return pltpu.CompilerParams(dimension_semantics=sem or None, vmem_limit_bytes=VMEM_LIMIT)


def _sigmoid(x):
    return 1.0 / (1.0 + jnp.exp(-x))


def _dsilu(x, s):
    return s * (1.0 + x * (1.0 - s))


def _rsq(x):
    return lax.rsqrt(jnp.mean(x * x, axis=-1, keepdims=True) + EPS)


MM_ROWS = 256


def _norm_matmul(x, g, w, out_dtype, name, comm=None):
    L, D = x.shape
    N = w.shape[1]
    tm = min(MM_ROWS, L)

    def body(x_ref, g_ref, w_ref, o_ref, hn_ref):
        xv = x_ref[...]
        hn = (xv * _rsq(xv) * g_ref[...]).astype(BF16)
        hn_ref[...] = hn
        o_ref[...] = jnp.dot(hn, w_ref[...], preferred_element_type=F32).astype(out_dtype)

    row = lambda i: (i, 0)
    fix = lambda i: (0, 0)
    return _carrier_call(
        body, comm, L // tm,
        in_specs=[pl.BlockSpec((tm, D), row), pl.BlockSpec((1, D), fix), pl.BlockSpec((D, N), fix)],
        out_specs=[pl.BlockSpec((tm, N), row), pl.BlockSpec((tm, D), row)],
        out_shape=[SDS((L, N), out_dtype), SDS((L, D), BF16)], scratch_shapes=[], name=name, args=(x, g, w))


def _matmul_norm_res(a, w, x, g, name, comm=None):
    L, K = a.shape
    D = w.shape[1]
    tm = min(MM_ROWS, L)

    def body(a_ref, w_ref, x_ref, g_ref, m_ref, xo_ref):
        m = jnp.dot(a_ref[...], w_ref[...], preferred_element_type=F32)
        m_ref[...] = m
        xo_ref[...] = x_ref[...] + m * _rsq(m) * g_ref[...]

    row = lambda i: (i, 0)
    fix = lambda i: (0, 0)
    return _carrier_call(
        body, comm, L // tm,
        in_specs=[pl.BlockSpec((tm, K), row), pl.BlockSpec((K, D), fix), pl.BlockSpec((tm, D), row),
                  pl.BlockSpec((1, D), fix)],
        out_specs=[pl.BlockSpec((tm, D), row), pl.BlockSpec((tm, D), row)],
        out_shape=[SDS((L, D), F32), SDS((L, D), F32)], scratch_shapes=[], name=name, args=(a, w, x, g))


def _postnorm_bwd(dx, m, g, name):
    L, D = dx.shape
    tm = min(512, L)

    def body(dx_ref, m_ref, g_ref, dm_ref, dg_ref):
        @pl.when(pl.program_id(0) == 0)
        def _():
            dg_ref[...] = jnp.zeros_like(dg_ref)

        m = m_ref[...]
        dxv = dx_ref[...]
        r = _rsq(m)
        mh = m * r
        dg_ref[...] += jnp.sum(dxv * mh, axis=0, keepdims=True)
        dyg = dxv * g_ref[...]
        dm_ref[...] = (r * (dyg - mh * jnp.mean(dyg * mh, axis=-1, keepdims=True))).astype(BF16)

    row = lambda i: (i, 0)
    fix = lambda i: (0, 0)
    return pl.pallas_call(
        body, grid=(L // tm,),
        in_specs=[pl.BlockSpec((tm, D), row), pl.BlockSpec((tm, D), row), pl.BlockSpec((1, D), fix)],
        out_specs=[pl.BlockSpec((tm, D), row), pl.BlockSpec((1, D), fix)],
        out_shape=[SDS((L, D), BF16), SDS((1, D), F32)],
        name=name, compiler_params=_cp("arbitrary"))(dx, m, g)


def _matmul_nt(a, w, out_dtype, name):
    L, D = a.shape
    K = w.shape[0]
    tm = min(MM_ROWS, L)

    def body(a_ref, w_ref, o_ref):
        o_ref[...] = lax.dot_general(a_ref[...], w_ref[...], NT, preferred_element_type=F32).astype(out_dtype)

    return pl.pallas_call(
        body, grid=(L // tm,),
        in_specs=[pl.BlockSpec((tm, D), lambda i: (i, 0)), pl.BlockSpec((K, D), lambda i: (0, 0))],
        out_specs=pl.BlockSpec((tm, K), lambda i: (i, 0)),
        out_shape=SDS((L, K), out_dtype),
        name=name, compiler_params=_cp("parallel"))(a, w)


def _matmul_tn(a, b, ta, tn, name):
    L, Ka = a.shape
    N = b.shape[1]
    tl = min(512, L)
    n_l = L // tl

    def body(a_ref, b_ref, o_ref, acc_ref):
        l = pl.program_id(2)

        @pl.when(l == 0)
        def _():
            acc_ref[...] = jnp.zeros_like(acc_ref)

        acc_ref[...] += lax.dot_general(a_ref[...], b_ref[...], TN, preferred_element_type=F32)

        @pl.when(l == n_l - 1)
        def _():
            o_ref[...] = acc_ref[...].astype(BF16)

    return pl.pallas_call(
        body, grid=(Ka // ta, N // tn, n_l),
        in_specs=[pl.BlockSpec((tl, ta), lambda i, j, l: (l, i)), pl.BlockSpec((tl, tn), lambda i, j, l: (l, j))],
        out_specs=pl.BlockSpec((ta, tn), lambda i, j, l: (i, j)),
        out_shape=SDS((Ka, N), BF16),
        scratch_shapes=[pltpu.VMEM((ta, tn), F32)],
        name=name, compiler_params=_cp("parallel", "parallel", "arbitrary"))(a, b)


def _ffn_up(x, g, w4, name, comm=None):
    L, D = x.shape
    b = w4.shape[2]
    tm = min(MM_ROWS, L)

    def body(x_ref, g_ref, w_ref, o_ref, hn_ref):
        xv = x_ref[...]
        hn = (xv * _rsq(xv) * g_ref[...]).astype(BF16)
        hn_ref[...] = hn
        for q in range(4):
            o_ref[q // 2, :, (q % 2) * b:(q % 2 + 1) * b] = jnp.dot(hn, w_ref[q], preferred_element_type=F32).astype(BF16)

    return _carrier_call(
        body, comm, L // tm,
        in_specs=[pl.BlockSpec((tm, D), lambda i: (i, 0)), pl.BlockSpec((1, D), lambda i: (0, 0)),
                  pl.BlockSpec((4, D, b), lambda i: (0, 0, 0))],
        out_specs=[pl.BlockSpec((2, tm, 2 * b), lambda i: (0, i, 0)), pl.BlockSpec((tm, D), lambda i: (i, 0))],
        out_shape=[SDS((2, L, 2 * b), BF16), SDS((L, D), BF16)], scratch_shapes=[], name=name, args=(x, g, w4))


def _ffn_up_dx(dup, w4, x, g, dres, name):
    _, L, _ = dup.shape
    _, D, b = w4.shape
    tm = min(MM_ROWS, L)

    def body(dy_ref, w_ref, x_ref, g_ref, dres_ref, dx_ref, dg_ref):
        @pl.when(pl.program_id(0) == 0)
        def _():
            dg_ref[...] = jnp.zeros_like(dg_ref)

        dh = jnp.zeros((tm, D), F32)
        for q in range(4):
            dh = dh + lax.dot_general(dy_ref[q // 2, :, (q % 2) * b:(q % 2 + 1) * b], w_ref[q], NT,
                                      preferred_element_type=F32)
        xv = x_ref[...]
        r = _rsq(xv)
        xh = xv * r
        dg_ref[...] += jnp.sum(dh * xh, axis=0, keepdims=True)
        dyg = dh * g_ref[...]
        dx_ref[...] = dres_ref[...] + r * (dyg - xh * jnp.mean(dyg * xh, axis=-1, keepdims=True))

    row = lambda i: (i, 0)
    fix = lambda i: (0, 0)
    return pl.pallas_call(
        body, grid=(L // tm,),
        in_specs=[pl.BlockSpec((2, tm, 2 * b), lambda i: (0, i, 0)), pl.BlockSpec((4, D, b), lambda i: (0, 0, 0)),
                  pl.BlockSpec((tm, D), row), pl.BlockSpec((1, D), fix), pl.BlockSpec((tm, D), row)],
        out_specs=[pl.BlockSpec((tm, D), row), pl.BlockSpec((1, D), fix)],
        out_shape=[SDS((L, D), F32), SDS((1, D), F32)],
        name=name, compiler_params=_cp("arbitrary"))(dup, w4, x, g, dres)


def _ffn_up_dw(hn, dup, name):
    L, D = hn.shape
    b = dup.shape[2] // 2
    tl = min(512, L)
    n_l = L // tl

    def body(a_ref, b_ref, o_ref, acc_ref):
        l = pl.program_id(1)

        @pl.when(l == 0)
        def _():
            acc_ref[...] = jnp.zeros_like(acc_ref)

        acc_ref[...] += lax.dot_general(a_ref[...], b_ref[...], TN, preferred_element_type=F32)

        @pl.when(l == n_l - 1)
        def _():
            o_ref[...] = acc_ref[...].astype(BF16)

    return pl.pallas_call(
        body, grid=(4, n_l),
        in_specs=[pl.BlockSpec((tl, D), lambda q, l: (l, 0)),
                  pl.BlockSpec((None, tl, b), lambda q, l: (q // 2, l, q % 2))],
        out_specs=pl.BlockSpec((None, D, b), lambda q, l: (q, 0, 0)),
        out_shape=SDS((4, D, b), BF16),
        scratch_shapes=[pltpu.VMEM((D, b), F32)],
        name=name, compiler_params=_cp("parallel", "arbitrary"))(hn, dup)


def _matmul_nt_prenorm_bwd(dy, w, x, g, dres, name, comm=None):
    L, N = dy.shape
    D = w.shape[0]
    tm = min(MM_ROWS, L)

    def body(dy_ref, w_ref, x_ref, g_ref, dres_ref, dx_ref, dg_ref):
        @pl.when(pl.program_id(0) == 0)
        def _():
            dg_ref[...] = jnp.zeros_like(dg_ref)

        dh = lax.dot_general(dy_ref[...], w_ref[...], NT, preferred_element_type=F32)
        xv = x_ref[...]
        r = _rsq(xv)
        xh = xv * r
        dg_ref[...] += jnp.sum(dh * xh, axis=0, keepdims=True)
        dyg = dh * g_ref[...]
        dx_ref[...] = dres_ref[...] + r * (dyg - xh * jnp.mean(dyg * xh, axis=-1, keepdims=True))

    row = lambda i: (i, 0)
    fix = lambda i: (0, 0)
    return _carrier_call(
        body, comm, L // tm,
        in_specs=[pl.BlockSpec((tm, N), row), pl.BlockSpec((D, N), fix), pl.BlockSpec((tm, D), row),
                  pl.BlockSpec((1, D), fix), pl.BlockSpec((tm, D), row)],
        out_specs=[pl.BlockSpec((tm, D), row), pl.BlockSpec((1, D), fix)],
        out_shape=[SDS((L, D), F32), SDS((1, D), F32)], scratch_shapes=[], name=name, args=(dy, w, x, g, dres))


def _loss_head(y, t, name):
    L, D = y.shape
    tm = min(512, L)

    def body(y_ref, t_ref, dy_ref, loss_ref):
        @pl.when(pl.program_id(0) == 0)
        def _():
            loss_ref[...] = jnp.zeros_like(loss_ref)

        e = y_ref[...] - t_ref[...]
        dy_ref[...] = e * (1.0 / D)
        s = jnp.sum(jnp.sum(e * e, axis=1, keepdims=True), axis=0, keepdims=True)
        loss_ref[...] += s * (0.5 / D)

    row = lambda i: (i, 0)
    return pl.pallas_call(
        body, grid=(L // tm,),
        in_specs=[pl.BlockSpec((tm, D), row), pl.BlockSpec((tm, D), row)],
        out_specs=[pl.BlockSpec((tm, D), row), pl.BlockSpec((1, 1), lambda i: (0, 0))],
        out_shape=[SDS((L, D), F32), SDS((1, 1), F32)],
        name=name, compiler_params=_cp("arbitrary"))(y, t)


def _tile_rows(ref):
    return HALO * (4 // jnp.dtype(ref.dtype).itemsize)


def _prev_rows(ref, r0, i, cols):
    n = _tile_rows(ref)
    p0 = pl.multiple_of(jnp.maximum(r0 - n, 0), n)
    return jnp.where(i > 0, ref[pl.ds(p0, n), cols].astype(F32)[n - HALO:], 0.0)


def _next_rows(ref, r0, i, n_tiles, cols):
    n = _tile_rows(ref)
    n0 = pl.multiple_of(jnp.minimum(r0 + ROW_TILE, n_tiles * ROW_TILE - n), n)
    return jnp.where(i < n_tiles - 1, ref[pl.ds(n0, n), cols].astype(F32)[:HALO], 0.0)


def _rows_f32(ref, rows, cols):
    return ref[rows, cols].astype(F32)


def _back(ext, s):
    return pltpu.roll(ext, s, axis=0)[HALO:HALO + ROW_TILE]


def _fwd(ext, s):
    n = ext.shape[0]
    return pltpu.roll(ext, n - s, axis=0)[:ROW_TILE]


def _store_rows(ref, rows):
    ref[...] = jnp.zeros_like(ref)
    for k, v in enumerate(rows):
        ref[k:k + 1, :] = v


def _strip_call(body, L, n_strips, ins, outs, name):
    def spec(rows, width, off):
        if off is None:
            return pl.BlockSpec((rows, width), lambda j: (0, 0))
        return pl.BlockSpec((rows, width), lambda j: (0, j + off))

    return pl.pallas_call(
        body, grid=(n_strips,),
        in_specs=[spec(a.shape[0], w, off) for a, w, off in ins],
        out_specs=[spec(s.shape[0], w, off) for s, w, off in outs],
        out_shape=[s for s, _, _ in outs],
        name=name, compiler_params=_cp("parallel"))(*[a for a, _, _ in ins])


def _ffn_mid_fwd(up, cw, cb, name):
    _, L, C = up.shape
    n_tiles = L // ROW_TILE
    al = slice(None)

    def body(up_ref, cw_ref, cb_ref, a_ref):
        w0, w1, w2 = cw_ref[0:1, :], cw_ref[1:2, :], cw_ref[2:3, :]
        b = cb_ref[...]
        gate_ref, val_ref = up_ref.at[0], up_ref.at[1]

        def step(i, c):
            r0 = pl.multiple_of(i * ROW_TILE, ROW_TILE)
            rows = pl.ds(r0, ROW_TILE)
            gp = _rows_f32(gate_ref, rows, al)
            ext = jnp.concatenate([_prev_rows(gate_ref, r0, i, al), gp], axis=0)
            gate = gp * w2 + _back(ext, 1) * w1 + _back(ext, 2) * w0 + b
            a_ref[rows, :] = (gate * _sigmoid(gate) * _rows_f32(val_ref, rows, al)).astype(BF16)
            return c

        lax.fori_loop(0, n_tiles, step, 0)

    strip = lambda rows: pl.BlockSpec((rows, LANE), lambda j: (0, j))
    return pl.pallas_call(
        body, grid=(C // LANE,),
        in_specs=[pl.BlockSpec((2, L, LANE), lambda j: (0, 0, j)), strip(cw.shape[0]), strip(1)],
        out_specs=strip(L), out_shape=SDS((L, C), BF16), name=name, compiler_params=_cp("parallel"))(up, cw, cb)


def _ffn_mid_bwd(da, up, cw, cb, name):
    L, C = da.shape
    n_tiles = L // ROW_TILE
    al = slice(None)

    def body(da_ref, up_ref, cw_ref, cb_ref, dup_ref, st_ref):
        w0, w1, w2 = cw_ref[0:1, :], cw_ref[1:2, :], cw_ref[2:3, :]
        b = cb_ref[...]
        gate_ref, val_ref = up_ref.at[0], up_ref.at[1]

        def step(i, c):
            r0 = pl.multiple_of(i * ROW_TILE, ROW_TILE)
            rows = pl.ds(r0, ROW_TILE)
            gp = _rows_f32(gate_ref, rows, al)
            gpe = jnp.concatenate([_prev_rows(gate_ref, r0, i, al), gp, _next_rows(gate_ref, r0, i, n_tiles, al)],
                                  axis=0)
            g1, g2 = pltpu.roll(gpe, 1, axis=0), pltpu.roll(gpe, 2, axis=0)
            gate = (gpe * w2 + g1 * w1 + g2 * w0 + b)[HALO:]
            sg = _sigmoid(gate)
            da_e = jnp.concatenate([_rows_f32(da_ref, rows, al), _next_rows(da_ref, r0, i, n_tiles, al)], axis=0)
            val_e = jnp.concatenate([_rows_f32(val_ref, rows, al), _next_rows(val_ref, r0, i, n_tiles, al)], axis=0)
            dgate_e = da_e * val_e * _dsilu(gate, sg)
            dgate = dgate_e[:ROW_TILE]
            dgp = dgate * w2 + _fwd(dgate_e, 1) * w1 + _fwd(dgate_e, 2) * w0
            dup_ref[0, rows, :] = dgp.astype(BF16)
            dup_ref[1, rows, :] = (da_e * gate * sg)[:ROW_TILE].astype(BF16)
            s = lambda v: jnp.sum(v, axis=0, keepdims=True)
            t = slice(HALO, HALO + ROW_TILE)
            return (c[0] + s(dgate * g2[t]), c[1] + s(dgate * g1[t]), c[2] + s(dgate * gp), c[3] + s(dgate))

        z = jnp.zeros((1, LANE), F32)
        _store_rows(st_ref, lax.fori_loop(0, n_tiles, step, (z, z, z, z)))

    strip = lambda rows: pl.BlockSpec((rows, LANE), lambda j: (0, j))
    pair = pl.BlockSpec((2, L, LANE), lambda j: (0, 0, j))
    return pl.pallas_call(
        body, grid=(C // LANE,), in_specs=[strip(L), pair, strip(cw.shape[0]), strip(1)],
        out_specs=[pair, strip(8)], out_shape=[SDS((2, L, C), BF16), SDS((8, C), F32)],
        name=name, compiler_params=_cp("parallel"))(da, up, cw, cb)


def _sc_mid_fwd(bcv, cw, name):
    L = bcv.shape[0]
    C = bcv.shape[1] // 3
    n_tiles = L // ROW_TILE
    s0, s1, s2 = slice(0, LANE), slice(LANE, 2 * LANE), slice(2 * LANE, 3 * LANE)

    def body(x_ref, cw_ref, q_ref):
        w0, w1, w2 = cw_ref[0:1, :], cw_ref[1:2, :], cw_ref[2:3, :]

        def step(i, c):
            r0 = pl.multiple_of(i * ROW_TILE, ROW_TILE)
            rows = pl.ds(r0, ROW_TILE)
            p = _rows_f32(x_ref, rows, s1) * _rows_f32(x_ref, rows, s2)
            ext = jnp.concatenate([_prev_rows(x_ref, r0, i, s1) * _prev_rows(x_ref, r0, i, s2), p], axis=0)
            u = p * w2 + _back(ext, 1) * w1 + _back(ext, 2) * w0
            q_ref[rows, :] = (_rows_f32(x_ref, rows, s0) * u).astype(BF16)
            return c

        lax.fori_loop(0, n_tiles, step, 0)

    return _strip_call(body, L, C // LANE, [(bcv, 3 * LANE, 0), (cw, LANE, 0)],
                       [(SDS((L, C), BF16), LANE, 0)], name)[0]


def _sc_mid_bwd(dq, bcv, cw, name):
    L, C = dq.shape
    n_tiles = L // ROW_TILE
    s0, s1, s2, al = slice(0, LANE), slice(LANE, 2 * LANE), slice(2 * LANE, 3 * LANE), slice(None)

    def body(dq_ref, x_ref, cw_ref, dx_ref, st_ref):
        w0, w1, w2 = cw_ref[0:1, :], cw_ref[1:2, :], cw_ref[2:3, :]

        def step(i, c):
            r0 = pl.multiple_of(i * ROW_TILE, ROW_TILE)
            rows = pl.ds(r0, ROW_TILE)
            gb, gc, v = _rows_f32(x_ref, rows, s0), _rows_f32(x_ref, rows, s1), _rows_f32(x_ref, rows, s2)
            dq_v = _rows_f32(dq_ref, rows, al)
            p = gc * v
            pext = jnp.concatenate([_prev_rows(x_ref, r0, i, s1) * _prev_rows(x_ref, r0, i, s2), p], axis=0)
            p1, p2 = _back(pext, 1), _back(pext, 2)
            u = p * w2 + p1 * w1 + p2 * w0
            du = dq_v * gb
            du_n = _next_rows(dq_ref, r0, i, n_tiles, al) * _next_rows(x_ref, r0, i, n_tiles, s0)
            ext = jnp.concatenate([du, du_n], axis=0)
            dp = du * w2 + _fwd(ext, 1) * w1 + _fwd(ext, 2) * w0
            dx_ref[rows, s0] = (dq_v * u).astype(BF16)
            dx_ref[rows, s1] = (dp * v).astype(BF16)
            dx_ref[rows, s2] = (dp * gc).astype(BF16)
            s = lambda t: jnp.sum(t, axis=0, keepdims=True)
            return (c[0] + s(du * p2), c[1] + s(du * p1), c[2] + s(du * p))

        z = jnp.zeros((1, LANE), F32)
        _store_rows(st_ref, lax.fori_loop(0, n_tiles, step, (z, z, z)))

    return _strip_call(body, L, C // LANE, [(dq, LANE, 0), (bcv, 3 * LANE, 0), (cw, LANE, 0)],
                       [(SDS((L, 3 * C), BF16), 3 * LANE, 0), (SDS((8, C), F32), LANE, 0)], name)


def _ssd_conv_fwd(zx, cw, cb, col0, C, name):
    L = zx.shape[0]
    n_tiles = L // ROW_TILE
    al = slice(None)

    def body(x_ref, cw_ref, cb_ref, o_ref):
        w0, w1, w2, w3 = cw_ref[0:1, :], cw_ref[1:2, :], cw_ref[2:3, :], cw_ref[3:4, :]
        b = cb_ref[...]

        def step(i, c):
            r0 = pl.multiple_of(i * ROW_TILE, ROW_TILE)
            rows = pl.ds(r0, ROW_TILE)
            xv = x_ref[rows, :]
            ext = jnp.concatenate([_prev_rows(x_ref, r0, i, al), xv], axis=0)
            cv = xv * w3 + _back(ext, 1) * w2 + _back(ext, 2) * w1 + _back(ext, 3) * w0 + b
            o_ref[rows, :] = cv * _sigmoid(cv)
            return c

        lax.fori_loop(0, n_tiles, step, 0)

    return _strip_call(body, L, C // LANE, [(zx, LANE, col0 // LANE), (cw, LANE, 0), (cb, LANE, 0)],
                       [(SDS((L, C), F32), LANE, 0)], name)[0]


def _ssd_conv_bwd(dxbc, zx, cw, cb, col0, name):
    L, C = dxbc.shape
    n_tiles = L // ROW_TILE
    al = slice(None)

    def body(d_ref, x_ref, cw_ref, cb_ref, o_ref, st_ref):
        w0, w1, w2, w3 = cw_ref[0:1, :], cw_ref[1:2, :], cw_ref[2:3, :], cw_ref[3:4, :]
        b = cb_ref[...]

        def step(i, c):
            r0 = pl.multiple_of(i * ROW_TILE, ROW_TILE)
            rows = pl.ds(r0, ROW_TILE)
            xv = x_ref[rows, :]
            xe = jnp.concatenate([_prev_rows(x_ref, r0, i, al), xv, _next_rows(x_ref, r0, i, n_tiles, al)], axis=0)
            x1, x2, x3 = pltpu.roll(xe, 1, axis=0), pltpu.roll(xe, 2, axis=0), pltpu.roll(xe, 3, axis=0)
            cv = (xe * w3 + x1 * w2 + x2 * w1 + x3 * w0 + b)[HALO:]
            de = jnp.concatenate([d_ref[rows, :], _next_rows(d_ref, r0, i, n_tiles, al)], axis=0)
            dc_ext = de * _dsilu(cv, _sigmoid(cv))
            dc = dc_ext[:ROW_TILE]
            o_ref[rows, :] = (dc * w3 + _fwd(dc_ext, 1) * w2 + _fwd(dc_ext, 2) * w1 + _fwd(dc_ext, 3) * w0).astype(BF16)
            s = lambda t: jnp.sum(t, axis=0, keepdims=True)
            t = slice(HALO, HALO + ROW_TILE)
            return (c[0] + s(dc * x3[t]), c[1] + s(dc * x2[t]), c[2] + s(dc * x1[t]), c[3] + s(dc * xv), c[4] + s(dc))

        z = jnp.zeros((1, LANE), F32)
        _store_rows(st_ref, lax.fori_loop(0, n_tiles, step, (z, z, z, z, z)))

    return _strip_call(body, L, C // LANE,
                       [(dxbc, LANE, 0), (zx, LANE, col0 // LANE), (cw, LANE, 0), (cb, LANE, 0)],
                       [(SDS((L, C), BF16), LANE, 0), (SDS((8, C), F32), LANE, 0)], name)


def _scan_constants(n_heads):
    hw = n_heads * HEAD_DIM
    col = np.arange(hw)
    ind = (col[None, :] // HEAD_DIM == np.arange(LANE)[:, None]).astype(np.float32)
    gcol = np.arange(GROUP_W)
    itile = (gcol[None, :] % CHUNK == np.arange(CHUNK)[:, None]).astype(np.float32)
    trit = (gcol[None, :] % CHUNK <= np.arange(CHUNK)[:, None]).astype(np.float32)
    tril = np.tril(np.ones((CHUNK, CHUNK), np.float32))
    bmask = (gcol[:, None] // HEAD_DIM == gcol[None, :] // HEAD_DIM).astype(np.float32)
    return (jnp.asarray(ind, BF16), jnp.asarray(ind.T.copy(), BF16), jnp.asarray(itile), jnp.asarray(trit),
            jnp.asarray(tril, BF16), jnp.asarray(bmask))


def _softplus(x):
    return jnp.maximum(x, 0.0) + jnp.log(1.0 + jnp.exp(-jnp.abs(x)))


def _split3(x):
    hi = x.astype(BF16)
    r1 = x - hi.astype(F32)
    mid = r1.astype(BF16)
    return hi, mid, (r1 - mid.astype(F32)).astype(BF16)


def _dot_sel(x, sel, dims=None, parts=3):
    if dims is None:
        mm = lambda p: jnp.dot(p, sel, preferred_element_type=F32)
    else:
        mm = lambda p: lax.dot_general(sel, p, dims, preferred_element_type=F32)
    hi, mid, lo = _split3(x)
    return (mm(lo) + mm(mid)) + mm(hi) if parts == 3 else mm(mid) + mm(hi)


SEL_X = (((1,), (0,)), ((), ()))


def _group_terms(g, dt, cs, xbc_ref, ind_ref, itile, trit, bmask, d_inner):
    gl = slice(g * GROUP_W, (g + 1) * GROUP_W)
    indg = ind_ref[:, gl]
    both = _dot_sel(jnp.concatenate([cs, dt], axis=0), indg)
    csl, dtx = both[:CHUNK], both[CHUNK:]
    rr = _dot_sel(csl * itile, jnp.ones((CHUNK, CHUNK), BF16), SEL_X)
    lm = jnp.exp(jnp.where(trit > 0.0, csl - rr, -jnp.inf))
    xs = xbc_ref[:, gl]
    b = xbc_ref[:, d_inner + g * D_STATE: d_inner + (g + 1) * D_STATE]
    c = xbc_ref[:, d_inner + (N_GROUPS + g) * D_STATE: d_inner + (N_GROUPS + g + 1) * D_STATE]
    u = xs * dtx
    bb, cb = b.astype(BF16), c.astype(BF16)
    btile = jnp.concatenate([bb] * HEADS_PER_GROUP, axis=0)
    cbt = lax.dot_general(cb, btile, NT, preferred_element_type=F32)
    m = cbt * lm
    ub = u.astype(BF16)
    bdu = jnp.where(bmask > 0.0, jnp.concatenate([ub] * HEADS_PER_GROUP, axis=0), jnp.zeros((), BF16))
    c_last = csl[CHUNK - 1:CHUNK, :]
    return dict(gl=gl, indg=indg, csl=csl, dtx=dtx, lm=lm, xs=xs, bb=bb, cb=cb, u=u, btile=btile, m=m, bdu=bdu,
                e=jnp.exp(csl), dec=jnp.exp(c_last - csl), e_last=jnp.exp(c_last))


def _ssd_scan_fwd(zx, xbc, par, dexp, nw, consts, comm, name):
    L = xbc.shape[0]
    d_inner = dexp.shape[1]
    n_chunks = L // CHUNK
    dt_blk = zx.shape[1] // LANE - 1
    ind, ind_t, itile_c, trit_c, tril_c, bmask_c = consts

    def body(xbc_ref, z_ref, dtr_ref, par_ref, dexp_ref, nw_ref, ind_ref, itile_ref, trit_ref, tril_ref, bmask_ref,
             yn_ref, yf_ref, st_out_ref, st_ref):
        @pl.when(pl.program_id(0) == 0)
        def _():
            st_ref[...] = jnp.zeros_like(st_ref)

        dt = _softplus(dtr_ref[...] + par_ref[0:1, :])
        a_head = -jnp.exp(par_ref[1:2, :])
        cs = _dot_sel(dt * a_head, tril_ref[...], SEL_X)
        itile, trit, bmask = itile_ref[...], trit_ref[...], bmask_ref[...]
        for g in range(N_GROUPS):
            t = _group_terms(g, dt, cs, xbc_ref, ind_ref, itile, trit, bmask, d_inner)
            p = st_ref[g]
            st_out_ref[0, g] = p
            y = jnp.dot(t["m"].astype(BF16), t["bdu"], preferred_element_type=F32)
            y = y + jnp.dot(t["cb"], p.astype(BF16), preferred_element_type=F32) * t["e"]
            st_new = lax.dot_general(t["bb"], (t["u"] * t["dec"]).astype(BF16), TN, preferred_element_type=F32)
            st_ref[g] = p * t["e_last"] + st_new
            yf_ref[:, t["gl"]] = y + t["xs"] * dexp_ref[:, t["gl"]]
        z = z_ref[...]
        y2 = yf_ref[...] * (z * _sigmoid(z))
        yn_ref[...] = (y2 * _rsq(y2) * nw_ref[...]).astype(BF16)

    row = lambda c: (c, 0)
    fix = lambda c: (0, 0)
    cspec = lambda a: pl.BlockSpec(a.shape, fix)
    return _carrier_call(
        body, comm, n_chunks,
        in_specs=[pl.BlockSpec((CHUNK, xbc.shape[1]), row), pl.BlockSpec((CHUNK, d_inner), row),
                  pl.BlockSpec((CHUNK, LANE), lambda c: (c, dt_blk)), cspec(par), cspec(dexp), cspec(nw),
                  cspec(ind), cspec(itile_c), cspec(trit_c), cspec(tril_c), cspec(bmask_c)],
        out_specs=[pl.BlockSpec((CHUNK, d_inner), row), pl.BlockSpec((CHUNK, d_inner), row),
                   pl.BlockSpec((1, N_GROUPS, D_STATE, GROUP_W), lambda c: (c, 0, 0, 0))],
        out_shape=[SDS((L, d_inner), BF16), SDS((L, d_inner), F32),
                   SDS((n_chunks, N_GROUPS, D_STATE, GROUP_W), F32)],
        scratch_shapes=[pltpu.VMEM((N_GROUPS, D_STATE, GROUP_W), F32)],
        name=name, args=(xbc, zx, zx, par, dexp, nw, ind, itile_c, trit_c, tril_c, bmask_c))


def _ssd_scan_bwd(dyn, yf, zx, xbc, states, par, dexp, nw, consts, comm, name):
    L = xbc.shape[0]
    d_inner = dexp.shape[1]
    n_chunks = L // CHUNK
    dt_blk = zx.shape[1] // LANE - 1
    ind, ind_t, itile_c, trit_c, tril_c, bmask_c = consts
    hslices = [slice(r * HEAD_DIM, (r + 1) * HEAD_DIM) for r in range(HEADS_PER_GROUP)]

    def body(dyn_ref, yf_ref, z_ref, dtr_ref, xbc_ref, st_in_ref, par_ref, dexp_ref, nw_ref, ind_ref, indt_ref,
             itile_ref, trit_ref, tril_ref, bmask_ref,
             dz_ref, dxbc_ref, ddt_ref, dnw_ref, dpar_ref, dq_ref, dyf_ref):
        @pl.when(pl.program_id(0) == 0)
        def _():
            dq_ref[...] = jnp.zeros_like(dq_ref)
            dnw_ref[...] = jnp.zeros_like(dnw_ref)
            dpar_ref[...] = jnp.zeros_like(dpar_ref)

        z, yfv, dynv = z_ref[...], yf_ref[...], dyn_ref[...]
        sz = _sigmoid(z)
        y2 = yfv * (z * sz)
        r = _rsq(y2)
        y2h = y2 * r
        dnw_ref[...] += jnp.sum(dynv * y2h, axis=0, keepdims=True)
        dyg = dynv * nw_ref[...]
        dy2 = r * (dyg - y2h * jnp.mean(dyg * y2h, axis=-1, keepdims=True))
        dz_ref[...] = (dy2 * yfv * _dsilu(z, sz)).astype(BF16)
        dyf_ref[...] = dy2 * (z * sz)

        pre = dtr_ref[...] + par_ref[0:1, :]
        dt = _softplus(pre)
        a_head = -jnp.exp(par_ref[1:2, :])
        cs = _dot_sel(dt * a_head, tril_ref[...], SEL_X)
        itile, trit, bmask = itile_ref[...], trit_ref[...], bmask_ref[...]
        dcs = jnp.zeros((CHUNK, LANE), F32)
        dcs_last = jnp.zeros((1, LANE), F32)
        ddt_u = jnp.zeros((CHUNK, LANE), F32)
        d_skip = jnp.zeros((1, LANE), F32)
        rsum = lambda v: jnp.sum(v, axis=0, keepdims=True)
        row8 = lax.broadcasted_iota(jnp.int32, (8, GROUP_W), 0)
        for g in range(N_GROUPS):
            t = _group_terms(g, dt, cs, xbc_ref, ind_ref, itile, trit, bmask, d_inner)
            gl, m, lm, u, bb, cb, e, dec, xs = (t[k] for k in ("gl", "m", "lm", "u", "bb", "cb", "e", "dec", "xs"))
            indt = indt_ref[gl, :]
            dy = dyf_ref[:, gl]
            dyb = dy.astype(BF16)
            p = st_in_ref[0, g]
            pb = p.astype(BF16)
            q = dq_ref[g]
            qb = q.astype(BF16)
            big = lax.dot_general(m.astype(BF16), dyb, TN, preferred_element_type=F32)
            du = jnp.zeros((CHUNK, GROUP_W), F32)
            for rh in range(HEADS_PER_GROUP):
                du = du + big[hslices[rh], :] * bmask[rh * HEAD_DIM:rh * HEAD_DIM + 1, :]
            dm = lax.dot_general(dyb, t["bdu"], NT, preferred_element_type=F32)
            w = dm * m
            dgt = (dm * lm).astype(BF16)
            dc = jnp.dot(dgt, t["btile"], preferred_element_type=F32)
            db_big = lax.dot_general(dgt, cb, TN, preferred_element_type=F32)
            db = db_big[hslices[0], :] + db_big[hslices[1], :] + db_big[hslices[2], :] + db_big[hslices[3], :]
            cp = jnp.dot(cb, pb, preferred_element_type=F32)
            dye = dy * e
            dyeb = dye.astype(BF16)
            dc = dc + lax.dot_general(dyeb, pb, NT, preferred_element_type=F32)
            dp = lax.dot_general(cb, dyeb, TN, preferred_element_type=F32)
            x2 = dye * cp
            bq = jnp.dot(bb, qb, preferred_element_type=F32)
            ud = u * dec
            du = du + bq * dec
            db = db + lax.dot_general(ud.astype(BF16), qb, NT, preferred_element_type=F32)
            x1 = bq * ud
            dq_ref[g] = dp + t["e_last"] * q
            x3 = rsum(q * p) * t["e_last"]
            red = _dot_sel(jnp.concatenate([w + x2 - x1, du * xs, itile * rsum(w)], axis=0), indt, parts=2)
            dcs = dcs + red[0:CHUNK] - red[2 * CHUNK:3 * CHUNK]
            ddt_u = ddt_u + red[CHUNK:2 * CHUNK]
            tail = _dot_sel(jnp.where(row8 == 0, rsum(x1) + x3, jnp.where(row8 == 1, rsum(dy * xs), 0.0)), indt,
                            parts=2)
            dcs_last = dcs_last + tail[0:1]
            d_skip = d_skip + tail[1:2]
            dxbc_ref[:, gl] = du * t["dtx"] + dy * dexp_ref[:, gl]
            dxbc_ref[:, d_inner + g * D_STATE: d_inner + (g + 1) * D_STATE] = db
            dxbc_ref[:, d_inner + (N_GROUPS + g) * D_STATE: d_inner + (N_GROUPS + g + 1) * D_STATE] = dc
        last = lax.broadcasted_iota(jnp.int32, (CHUNK, LANE), 0) == CHUNK - 1
        dcs = dcs + jnp.where(last, dcs_last, 0.0)
        da = _dot_sel(dcs, tril_ref[...], TN)
        ddt = da * a_head + ddt_u
        heads = lax.broadcasted_iota(jnp.int32, (CHUNK, LANE), 1) < d_inner // HEAD_DIM
        ddt_raw = jnp.where(heads, ddt * _sigmoid(pre), 0.0)
        ddt_ref[...] = ddt_raw.astype(BF16)
        dpar_ref[0:1, :] += rsum(ddt_raw)
        dpar_ref[1:2, :] += rsum(da * dt) * a_head
        dpar_ref[2:3, :] += d_skip

    rev = lambda c: (n_chunks - 1 - c, 0)
    fix = lambda c: (0, 0)
    cspec = lambda a: pl.BlockSpec(a.shape, fix)
    nx = xbc.shape[1]
    return _carrier_call(
        body, comm, n_chunks,
        in_specs=[pl.BlockSpec((CHUNK, d_inner), rev), pl.BlockSpec((CHUNK, d_inner), rev),
                  pl.BlockSpec((CHUNK, d_inner), rev), pl.BlockSpec((CHUNK, LANE), lambda c: (n_chunks - 1 - c, dt_blk)),
                  pl.BlockSpec((CHUNK, nx), rev),
                  pl.BlockSpec((1, N_GROUPS, D_STATE, GROUP_W), lambda c: (n_chunks - 1 - c, 0, 0, 0)),
                  cspec(par), cspec(dexp), cspec(nw), cspec(ind), cspec(ind_t), cspec(itile_c), cspec(trit_c),
                  cspec(tril_c), cspec(bmask_c)],
        out_specs=[pl.BlockSpec((CHUNK, d_inner), rev), pl.BlockSpec((CHUNK, nx), rev),
                   pl.BlockSpec((CHUNK, LANE), rev), pl.BlockSpec((1, d_inner), fix), pl.BlockSpec((8, LANE), fix)],
        out_shape=[SDS((L, d_inner), BF16), SDS((L, nx), F32), SDS((L, LANE), BF16), SDS((1, d_inner), F32),
                   SDS((8, LANE), F32)],
        scratch_shapes=[pltpu.VMEM((N_GROUPS, D_STATE, GROUP_W), F32), pltpu.VMEM((CHUNK, d_inner), F32)],
        name=name, args=(dyn, yf, zx, zx, xbc, states, par, dexp, nw, ind, ind_t, itile_c, trit_c, tril_c, bmask_c))


def _adamw(w, m, v, g, name):
    R, C = w.shape
    tr = R
    for cand in (256, 128, 64, 32, 16, 8):
        if R % cand == 0:
            tr = cand
            break

    def body(w_ref, m_ref, v_ref, g_ref, d_ref, mo_ref, vo_ref):
        gv = g_ref[...]
        mn = ADAM_B1 * m_ref[...] + (1.0 - ADAM_B1) * gv
        vn = ADAM_B2 * v_ref[...] + (1.0 - ADAM_B2) * (gv * gv)
        m_hat = mn / (1.0 - ADAM_B1 ** ADAM_STEP)
        v_hat = vn / (1.0 - ADAM_B2 ** ADAM_STEP)
        d_ref[...] = -ADAM_LR * (m_hat / (jnp.sqrt(v_hat) + ADAM_EPS) + ADAM_WD * w_ref[...])
        mo_ref[...] = mn
        vo_ref[...] = vn

    blk = pl.BlockSpec((tr, C), lambda i: (i, 0))
    return pl.pallas_call(
        body, grid=(R // tr,), in_specs=[blk] * 4, out_specs=[blk] * 3, out_shape=[SDS((R, C), F32)] * 3,
        name=name, compiler_params=_cp("parallel"))(w, m, v, g)


def _sum_slots(parts, name):
    n, R, C = parts.shape
    tr = 128 if R % 128 == 0 else R

    def body(p_ref, o_ref):
        acc = p_ref[0]
        for k in range(1, n):
            acc = acc + p_ref[k]
        o_ref[...] = acc

    return pl.pallas_call(
        body, grid=(R // tr,), in_specs=[pl.BlockSpec((n, tr, C), lambda i: (0, i, 0))],
        out_specs=pl.BlockSpec((tr, C), lambda i: (i, 0)), out_shape=SDS((R, C), F32),
        name=name, compiler_params=_cp("parallel"))(parts)


def _add_core_halves(where, g, r, name):
    _, n, a, b = g.shape
    ta = a // 2

    def body(w_ref, g_ref, r_ref, o_ref):
        o_ref[...] = (g_ref[...].astype(F32) + r_ref[...].astype(F32)).astype(BF16)

    blk = lambda f: pl.BlockSpec((None, None, ta, b), f)
    mine = lambda s, l, w: (s, l, 0, 0)
    return pl.pallas_call(
        body, grid_spec=pltpu.PrefetchScalarGridSpec(
            num_scalar_prefetch=1, grid=(4, n),
            in_specs=[blk(lambda s, l, w: (s, l, w[1], 0)), blk(mine)], out_specs=blk(mine)),
        out_shape=SDS((4, n, ta, b), BF16), name=name,
        compiler_params=_cp("parallel", "parallel"))(where, g, r)


def _sum_shard(where, g, r, rr, name):
    _, n, a, b = g.shape
    ta = a // 2

    def body(w_ref, g_ref, r_ref, rr_ref, o_ref):
        f = lambda v: v.astype(F32)
        o_ref[...] = (((f(g_ref[...]) + f(r_ref[...])) + f(rr_ref[0])) + f(rr_ref[1])) + f(rr_ref[2])

    return pl.pallas_call(
        body, grid_spec=pltpu.PrefetchScalarGridSpec(
            num_scalar_prefetch=1, grid=(n,),
            in_specs=[pl.BlockSpec((None, None, ta, b), lambda l, w: (w[0], l, w[1], 0)),
                      pl.BlockSpec((None, None, ta, b), lambda l, w: (w[0], l, 0, 0)),
                      pl.BlockSpec((3, None, ta, b), lambda l, w: (0, l, 0, 0))],
            out_specs=pl.BlockSpec((None, ta, b), lambda l, w: (l, w[1], 0))),
        out_shape=SDS((n, a, b), F32), name=name,
        compiler_params=_cp("parallel"))(where, g, r, rr)


def _me():
    return lax.axis_index("x"), lax.axis_index("y"), lax.axis_index("c")


def _chip_peers(x, y):
    return [(1 - x, y), (x, 1 - y), (1 - x, 1 - y)]


def _rcopy(src, dst, send_sems, recv_sems, k, to):
    return pltpu.make_async_remote_copy(src_ref=src, dst_ref=dst, send_sem=send_sems.at[k], recv_sem=recv_sems.at[k],
                                        device_id=to, device_id_type=MESH)


def _row_half(ref, c, lead=()):
    a = ref.shape[len(lead) + 1]
    return ref.at[(*lead, slice(None), pl.ds(c * (a // 2), a // 2))]


class _Comm(NamedTuple):
    ins: list
    out_shapes: list
    n_sems: int
    start: Callable
    finish: Callable


def _sem_scratch(comm):
    return [pltpu.SemaphoreType.DMA((comm.n_sems,)), pltpu.SemaphoreType.DMA((comm.n_sems,))]


def _run_comm(comm, name):
    n_in, n_out = len(comm.ins), len(comm.out_shapes)

    def body(*refs):
        ins, outs, sems = refs[:n_in], refs[n_in:n_in + n_out], refs[n_in + n_out:]
        comm.start(ins, outs, *sems)
        comm.finish(ins, outs, *sems)

    return pl.pallas_call(body, in_specs=[ANY] * n_in, out_specs=[ANY] * n_out, out_shape=comm.out_shapes,
                          scratch_shapes=_sem_scratch(comm), name=name)(*comm.ins)


def _carrier_call(compute, comm, n_steps, in_specs, out_specs, out_shape, scratch_shapes, name, args):
    if comm is None:
        return pl.pallas_call(compute, grid=(n_steps,), in_specs=in_specs, out_specs=out_specs, out_shape=out_shape,
                              scratch_shapes=scratch_shapes, name=name, compiler_params=_cp("arbitrary"))(*args), []
    n_in, n_out, n_scr = len(in_specs), len(out_specs), len(scratch_shapes)
    n_ci, n_co = len(comm.ins), len(comm.out_shapes)

    def body(*refs):
        ins, cins = refs[:n_in], refs[n_in:n_in + n_ci]
        o = n_in + n_ci
        outs, couts = refs[o:o + n_out], refs[o + n_out:o + n_out + n_co]
        s = o + n_out + n_co
        scratch, sems = refs[s:s + n_scr], refs[s + n_scr:]

        @pl.when(pl.program_id(0) == 0)
        def _():
            comm.start(cins, couts, *sems)

        compute(*ins, *outs, *scratch)

        @pl.when(pl.program_id(0) == n_steps - 1)
        def _():
            comm.finish(cins, couts, *sems)

    res = pl.pallas_call(
        body, grid=(n_steps,), in_specs=list(in_specs) + [ANY] * n_ci, out_specs=list(out_specs) + [ANY] * n_co,
        out_shape=list(out_shape) + list(comm.out_shapes), scratch_shapes=list(scratch_shapes) + _sem_scratch(comm),
        name=name, compiler_params=_cp("arbitrary"))(*args, *comm.ins)
    return res[:n_out], res[n_out:]


def _allgather_plan(mine, small=None):
    n = len(mine)
    ins = list(mine) + ([] if small is None else [small])
    out_shapes = [SDS((4,) + m.shape, BF16) for m in mine] + ([] if small is None else [SDS((4,) + small.shape, F32)])
    sem = lambda t, k: 7 * t + k

    def first_copies(ins_r, outs_r, send, recv):
        x, y, c = _me()
        q = 2 * x + y
        cps = []
        for j, chip in enumerate(_chip_peers(x, y)):
            for t in range(n):
                cps.append(_rcopy(_row_half(ins_r[t], c), _row_half(outs_r[t], c, (q,)), send, recv, sem(t, j),
                                  (*chip, c)))
            if small is not None:
                cps.append(_rcopy(ins_r[n], outs_r[n].at[q], send, recv, sem(n, j), (*chip, c)))
        for t in range(len(ins)):
            cps.append(_rcopy(ins_r[t], outs_r[t].at[q], send, recv, sem(t, 6), (x, y, 1 - c)))
        return cps

    def start(ins_r, outs_r, send, recv):
        for cp in first_copies(ins_r, outs_r, send, recv):
            cp.start()

    def finish(ins_r, outs_r, send, recv):
        x, y, c = _me()
        sib = (x, y, 1 - c)
        chips = _chip_peers(x, y)
        passed = []
        for j, (px, py) in enumerate(chips):
            for t in range(n):
                blk = _row_half(outs_r[t], c, (2 * px + py,))
                _rcopy(blk, blk, send, recv, sem(t, j), sib).wait_recv()
                cp = _rcopy(blk, blk, send, recv, sem(t, 3 + j), sib)
                cp.start()
                passed.append(cp)
        for j, (px, py) in enumerate(chips):
            for t in range(n):
                blk = _row_half(outs_r[t], 1 - c, (2 * px + py,))
                _rcopy(blk, blk, send, recv, sem(t, 3 + j), sib).wait_recv()
            if small is not None:
                sblk = outs_r[n].at[2 * px + py]
                _rcopy(sblk, sblk, send, recv, sem(n, j), sib).wait_recv()
        for t in range(len(ins)):
            own = outs_r[t].at[2 * x + y]
            _rcopy(own, own, send, recv, sem(t, 6), sib).wait_recv()
        for cp in first_copies(ins_r, outs_r, send, recv) + passed:
            cp.wait_send()

    return _Comm(ins, out_shapes, 7 * len(ins), start, finish)


def _grads_to_sibling_plan(grads):
    n = len(grads)

    def copies(ins_r, outs_r, send, recv):
        x, y, c = _me()
        return [_rcopy(_row_half(ins_r[t], 1 - c, (slice(None),)), outs_r[t], send, recv, t, (x, y, 1 - c))
                for t in range(n)]

    def start(*a):
        for cp in copies(*a):
            cp.start()

    def finish(*a):
        for cp in copies(*a):
            cp.wait()

    out_shapes = [SDS((4, g.shape[1], g.shape[2] // 2, g.shape[3]), BF16) for g in grads]
    return _Comm(list(grads), out_shapes, n, start, finish)


def _grads_to_chips_plan(psums):
    n = len(psums)

    def copies(ins_r, outs_r, send, recv):
        x, y, c = _me()
        return [_rcopy(ins_r[t].at[2 * px + py], outs_r[t].at[j], send, recv, 3 * t + j, (px, py, c))
                for j, (px, py) in enumerate(_chip_peers(x, y)) for t in range(n)]

    def start(*a):
        for cp in copies(*a):
            cp.start()

    def finish(*a):
        for cp in copies(*a):
            cp.wait()

    return _Comm(list(psums), [SDS((3,) + p.shape[1:], BF16) for p in psums], 3 * n, start, finish)


def _swap_halves(sums):
    n = len(sums)

    def body(*refs):
        out_refs = refs[n:2 * n]
        send_sems, recv_sems = refs[2 * n:]
        x, y, c = _me()
        sib = (x, y, 1 - c)
        cps = [_rcopy(_row_half(out_refs[t], c), _row_half(out_refs[t], c), send_sems, recv_sems, t, sib)
               for t in range(n)]
        for cp in cps:
            cp.start()
        for t in range(n):
            other = _row_half(out_refs[t], 1 - c)
            _rcopy(other, other, send_sems, recv_sems, t, sib).wait_recv()
        for cp in cps:
            cp.wait_send()

    return pl.pallas_call(
        body, in_specs=[ANY] * n, out_specs=[ANY] * n, out_shape=[SDS(s.shape, F32) for s in sums],
        input_output_aliases={t: t for t in range(n)},
        scratch_shapes=[pltpu.SemaphoreType.DMA((n,)), pltpu.SemaphoreType.DMA((n,))],
        name="swap_halves")(*sums)


def _allgather_small(part):
    def body(p_ref, out_ref, send_sems, recv_sems, local_sem):
        x, y, c = _me()
        me = 4 * x + 2 * y + c
        own = pltpu.make_async_copy(p_ref, out_ref.at[me], local_sem.at[0])
        own.start()
        sends = []
        for k in range(1, 8):
            fx, fy, fc = (k >> 2) & 1, (k >> 1) & 1, k & 1
            to = (x ^ fx, y ^ fy, c ^ fc)
            sends.append(_rcopy(p_ref, out_ref.at[me], send_sems, recv_sems, k - 1, to))
        for cp in sends:
            cp.start()
        for k in range(1, 8):
            slot = out_ref.at[me ^ k]
            _rcopy(slot, slot, send_sems, recv_sems, k - 1, (x, y, c)).wait_recv()
        for cp in sends:
            cp.wait_send()
        own.wait()

    return pl.pallas_call(
        body, in_specs=[ANY], out_specs=ANY, out_shape=SDS((8,) + part.shape, F32),
        scratch_shapes=[pltpu.SemaphoreType.DMA((7,)), pltpu.SemaphoreType.DMA((7,)), pltpu.SemaphoreType.DMA((1,))],
        name="allgather_small")(part)


BIG = (("ssd_w_in", 2), ("ssd_w_out", 1), ("sc_w_in", 2), ("sc_w_out", 1), ("ffn_w_up", 2), ("ffn_w_down", 1))


def _to_shards(full, axis):
    A, B = full.shape
    if axis == 2:
        return full.reshape(A, 4, B // 4).transpose(1, 0, 2)
    return full.reshape(4, A // 4, B)


def _from_shards(shards, axis):
    _, a, b = shards.shape
    if axis == 2:
        return shards.transpose(1, 0, 2).reshape(a, 4 * b)
    return shards.reshape(4 * a, b)


def _interleave(w, parts):
    lead, n = w.shape[:-1], w.shape[-1]
    return w.reshape(*lead, parts, n // (parts * LANE), LANE).swapaxes(-2, -3).reshape(*lead, n)


def _deinterleave(w, parts):
    lead, n = w.shape[:-1], w.shape[-1]
    return w.reshape(*lead, n // (parts * LANE), parts, LANE).swapaxes(-2, -3).reshape(*lead, n)


def _pack_rows(vectors, width, row_multiple):
    flat = jnp.concatenate(vectors, axis=-1)
    n = flat.shape[-1]
    unit = width * row_multiple
    total = -(-n // unit) * unit
    flat = jnp.pad(flat, [(0, 0)] * (flat.ndim - 1) + [(0, total - n)])
    return flat.reshape(*flat.shape[:-1], total // width, width)


def _unpack(flat, shapes):
    out, off = [], 0
    for s in shapes:
        n = int(np.prod(s))
        out.append(flat[..., off:off + n].reshape(*flat.shape[:-1], *s))
        off += n
    return out


def _memo(fn):
    cache = {}

    def wrapped(k):
        if k not in cache:
            cache[k] = fn(k)
        return cache[k]

    return wrapped


def _row(v):
    return v.reshape(1, -1)


def _pad_rows(w, rows=8):
    return jnp.pad(w, ((0, rows - w.shape[0]), (0, 0)))


def _ffn_fwd(x, g_pre, g_post, w_up, cw, cb, w_down, tag, comm_up=None, comm_down=None):
    (up, hn), got_up = _ffn_up(x, g_pre, w_up, "ffn_up" + tag, comm_up)
    a = _ffn_mid_fwd(up, cw, cb, "ffn_mid_fwd" + tag)
    (f, x_new), got_down = _matmul_norm_res(a, w_down, x, g_post, "ffn_down" + tag, comm_down)
    return x_new, (x, hn, up, a, f), got_up, got_down


def _ffn_bwd(dx, saved, g_pre, g_post, w_up, cw, cb, w_down, tag):
    x, hn, up, a, f = saved
    df, dg_post = _postnorm_bwd(dx, f, g_post, "ffn_post_bwd" + tag)
    da = _matmul_nt(df, w_down, BF16, "ffn_down_dx" + tag)
    dw_down = _matmul_tn(a, df, w_down.shape[0] // 2, w_down.shape[1], "ffn_down_dw" + tag)
    dup, stats = _ffn_mid_bwd(da, up, cw, cb, "ffn_mid_bwd" + tag)
    dx_in, dg_pre = _ffn_up_dx(dup, w_up, x, g_pre, dx, "ffn_up_dx" + tag)
    dw_up = _ffn_up_dw(hn, dup, "ffn_up_dw" + tag)
    return dx_in, dict(g_pre=dg_pre, g_post=dg_post, w_up=dw_up, w_down=dw_down, cw=stats[0:3], cb=stats[3])


def _sc_fwd(x, g_pre, g_post, w_in, cw, w_out, tag):
    (bcv, hn), _ = _norm_matmul(x, g_pre, w_in, BF16, "sc_in" + tag)
    q = _sc_mid_fwd(bcv, cw, "sc_mid_fwd" + tag)
    (m, x_new), _ = _matmul_norm_res(q, w_out, x, g_post, "sc_out" + tag)
    return x_new, (x, hn, bcv, q, m)


def _sc_bwd(dx, saved, g_pre, g_post, w_in, cw, w_out, tag):
    x, hn, bcv, q, m = saved
    dm, dg_post = _postnorm_bwd(dx, m, g_post, "sc_post_bwd" + tag)
    dq = _matmul_nt(dm, w_out, BF16, "sc_out_dx" + tag)
    dw_out = _matmul_tn(q, dm, w_out.shape[0], w_out.shape[1], "sc_out_dw" + tag)
    dbcv, stats = _sc_mid_bwd(dq, bcv, cw, "sc_mid_bwd" + tag)
    (dx_in, dg_pre), _ = _matmul_nt_prenorm_bwd(dbcv, w_in, x, g_pre, dx, "sc_in_dx" + tag)
    dw_in = _matmul_tn(hn, dbcv, w_in.shape[0], w_in.shape[1] // 3, "sc_in_dw" + tag)
    return dx_in, dict(g_pre=dg_pre, g_post=dg_post, w_in=dw_in, w_out=dw_out, cw=stats[0:3])


def _ssd_fwd(x, g_pre, g_post, w_in, cw, cb, par, dexp, nw, w_out, consts, comm, tag, comm_in=None):
    d_inner = dexp.shape[1]
    (zx, hn), got_in = _norm_matmul(x, g_pre, w_in, F32, "ssd_in" + tag, comm_in)
    xbc = _ssd_conv_fwd(zx, cw, cb, d_inner, cw.shape[1], "ssd_conv_fwd" + tag)
    (yn, yf, states), got = _ssd_scan_fwd(zx, xbc, par, dexp, nw, consts, comm, "ssd_scan_fwd" + tag)
    if callable(w_out):
        w_out = w_out(got_in)
    (m, x_new), _ = _matmul_norm_res(yn, w_out, x, g_post, "ssd_out" + tag)
    return x_new, (x, hn, zx, xbc, yn, yf, states, m), got


def _ssd_bwd(dx, saved, g_pre, g_post, w_in, cw, cb, par, dexp, nw, w_out, consts, comm, tag, comm_dx=None):
    x, hn, zx, xbc, yn, yf, states, m = saved
    d_inner = dexp.shape[1]
    dm, dg_post = _postnorm_bwd(dx, m, g_post, "ssd_post_bwd" + tag)
    dyn = _matmul_nt(dm, w_out, F32, "ssd_out_dx" + tag)
    dw_out = _matmul_tn(yn, dm, w_out.shape[0] // 2, w_out.shape[1], "ssd_out_dw" + tag)
    (dz, dxbc, ddt, dnw, dpar), got = _ssd_scan_bwd(dyn, yf, zx, xbc, states, par, dexp, nw, consts, comm(dw_out),
                                                    "ssd_scan_bwd" + tag)
    dxp, stats = _ssd_conv_bwd(dxbc, zx, cw, cb, d_inner, "ssd_conv_bwd" + tag)
    dzx = jnp.concatenate([dz, dxp, ddt], axis=1)
    dw_in = _matmul_tn(hn, dzx, w_in.shape[0], w_in.shape[1] // 7, "ssd_in_dw" + tag)
    (dx_in, dg_pre), got_dx = _matmul_nt_prenorm_bwd(dzx, w_in, x, g_pre, dx, "ssd_in_dx" + tag,
                                                     None if comm_dx is None else comm_dx(dw_in))
    n_heads = d_inner // HEAD_DIM
    grads = dict(g_pre=dg_pre, g_post=dg_post, w_in=dw_in, w_out=dw_out, cw=stats[0:4], cb=stats[4],
                 dt_bias=dpar[0, :n_heads], a_log=dpar[1, :n_heads], d=dpar[2, :n_heads], nw=dnw[0])
    return dx_in, grads, got, got_dx


def kernel(x, mix_pre_g, mix_post_g, ffn_pre_g, ffn_post_g, ssd_w_in, ssd_conv_w, ssd_conv_b, ssd_dt_bias, ssd_A_log, ssd_D, ssd_norm_w, ssd_w_out, sc_w_in, sc_conv_w, sc_w_out, ffn_w_up, ffn_conv_w, ffn_conv_b, ffn_w_down, loss_target, m_mix_pre_g, m_mix_post_g, m_ffn_pre_g, m_ffn_post_g, m_ssd_w_in, m_ssd_conv_w, m_ssd_conv_b, m_ssd_dt_bias, m_ssd_A_log, m_ssd_D, m_ssd_norm_w, m_ssd_w_out, m_sc_w_in, m_sc_conv_w, m_sc_w_out, m_ffn_w_up, m_ffn_conv_w, m_ffn_conv_b, m_ffn_w_down, v_mix_pre_g, v_mix_post_g, v_ffn_pre_g, v_ffn_post_g, v_ssd_w_in, v_ssd_conv_w, v_ssd_conv_b, v_ssd_dt_bias, v_ssd_A_log, v_ssd_D, v_ssd_norm_w, v_ssd_w_out, v_sc_w_in, v_sc_conv_w, v_sc_w_out, v_ffn_w_up, v_ffn_conv_w, v_ffn_conv_b, v_ffn_w_down):
    names = ["mix_pre_g", "mix_post_g", "ffn_pre_g", "ffn_post_g", "ssd_w_in", "ssd_conv_w", "ssd_conv_b",
             "ssd_dt_bias", "ssd_A_log", "ssd_D", "ssd_norm_w", "ssd_w_out", "sc_w_in", "sc_conv_w", "sc_w_out",
             "ffn_w_up", "ffn_conv_w", "ffn_conv_b", "ffn_w_down"]
    env = locals()
    wts = {n: env[n] for n in names}
    mom = {n: env["m_" + n] for n in names}
    var = {n: env["v_" + n] for n in names}

    depth, d_model = mix_pre_g.shape
    n_ssd, n_heads = ssd_dt_bias.shape
    n_sc = sc_conv_w.shape[0]
    d_inner = n_heads * HEAD_DIM
    conv_dim = d_inner + 2 * N_GROUPS * D_STATE
    ssd_in_dim = d_inner + conv_dim + n_heads
    ssd_in_pad = d_inner + conv_dim + LANE
    q_chip = 2 * lax.axis_index("x") + lax.axis_index("y")
    core = lax.axis_index("c")

    assert depth == 4 and n_ssd == 2 and n_sc == 2, "the exchange schedule is written for this trunk"
    ssd_items = lambda j: [("ssd_w_in", j), ("ssd_w_out", j)]
    sc_items = lambda j: [("sc_w_in", j), ("sc_w_out", j)]
    ffn_items = lambda i: [("ffn_w_up", i), ("ffn_w_down", i)]
    gather_first = [("ssd_w_in", 0)]
    gather_in_ssd_in = {0: [("ssd_w_out", 0)]}
    gather_in_scan = {0: ffn_items(0) + sc_items(0) + ffn_items(1), 2: ffn_items(2) + sc_items(1) + ffn_items(3)}
    gather_in_ffn = {0: ([("ssd_w_in", 1)], [("ssd_w_out", 1)])}
    reduce_in_scan = {2: ffn_items(3) + sc_items(1) + ffn_items(2),
                      0: ssd_items(1) + ffn_items(1) + sc_items(0) + ffn_items(0) + [("ssd_w_out", 0)]}
    reduce_in_dx = {0: [("ssd_w_in", 0)]}
    axis_of = dict(BIG)
    gathered = {}

    def gather_plan(items, small=None):
        mine = [wts[n][layer:layer + 1].astype(BF16) for n, layer in items]
        return _allgather_plan(mine, small) if items else None

    def gather_done(items, results):
        for item, buf in zip(items, results):
            gathered[item] = buf[:, 0]

    def full(n, layer):
        return _from_shards(gathered[(n, layer)], axis_of[n])

    conv_names = ["ssd_conv_w", "sc_conv_w", "ffn_conv_w"]
    conv_shapes = [wts[n].shape for n in conv_names]
    small_mine = _pack_rows([wts[n].reshape(-1) for n in conv_names], LANE, 8)
    *results, small_all = _run_comm(gather_plan(gather_first, small_mine), "allgather_first")
    gather_done(gather_first, results)
    conv_full = {}
    for n, f, s in zip(conv_names, _unpack(small_all.reshape(4, -1), conv_shapes), conv_shapes):
        conv_full[n] = f.transpose(1, 2, 0, 3).reshape(s[0], s[1], 4 * s[2])

    consts = _scan_constants(n_heads)

    def ssd_args(j):
        par = jnp.zeros((8, LANE), F32).at[0, :n_heads].set(ssd_dt_bias[j]).at[1, :n_heads].set(ssd_A_log[j])
        dexp = jnp.repeat(ssd_D[j], HEAD_DIM).reshape(1, d_inner)
        w_in = jnp.pad(full("ssd_w_in", j), ((0, 0), (0, ssd_in_pad - ssd_in_dim)))
        return (w_in, _pad_rows(conv_full["ssd_conv_w"][j]), _row(ssd_conv_b[j]), par, dexp, _row(ssd_norm_w[j]))

    def sc_args(j):
        return (_interleave(full("sc_w_in", j), 3), _pad_rows(conv_full["sc_conv_w"][j]), full("sc_w_out", j))

    def ffn_args(i):
        return (gathered[("ffn_w_up", i)], _pad_rows(conv_full["ffn_conv_w"][i]), _row(ffn_conv_b[i]),
                full("ffn_w_down", i))

    ssd_args, sc_args, ffn_args = _memo(ssd_args), _memo(sc_args), _memo(ffn_args)

    h = x[0]
    saved = []
    for i in range(depth):
        j = i // 2
        gp, gq = _row(mix_pre_g[i]), _row(mix_post_g[i])
        if i % 2 == 0:
            items_in = gather_in_ssd_in.get(i, [])

            def w_out_when_here(got_in, items_in=items_in, j=j):
                gather_done(items_in, got_in)
                return full("ssd_w_out", j)

            h, sv, results = _ssd_fwd(h, gp, gq, *ssd_args(j), w_out_when_here, consts, gather_plan(gather_in_scan[i]),
                                      tag="", comm_in=gather_plan(items_in))
            gather_done(gather_in_scan[i], results)
        else:
            h, sv = _sc_fwd(h, gp, gq, *sc_args(j), tag="")
        items_up, items_down = gather_in_ffn.get(i, ([], []))
        h, sv2, got_up, got_down = _ffn_fwd(h, _row(ffn_pre_g[i]), _row(ffn_post_g[i]), *ffn_args(i), tag="",
                                            comm_up=gather_plan(items_up), comm_down=gather_plan(items_down))
        gather_done(items_up, got_up)
        gather_done(items_down, got_down)
        saved.append((sv, sv2))
    dh, loss_part = _loss_head(h, loss_target[0], "loss_head")

    mix_grads, ffn_grads = [None] * depth, [None] * depth
    where = jnp.stack([q_chip, core]).astype(jnp.int32)

    def shard_grad(n, layer):
        if n == "ffn_w_up":
            g = ffn_grads[layer]["w_up"]
        elif n == "ffn_w_down":
            g = _to_shards(ffn_grads[layer]["w_down"], 1)
        elif n == "ssd_w_in":
            g = _to_shards(early[(n, layer)][:, :ssd_in_dim], 2)
        elif n == "ssd_w_out":
            g = _to_shards(early[(n, layer)], 1)
        elif n == "sc_w_in":
            g = _to_shards(_deinterleave(mix_grads[2 * layer + 1]["w_in"], 3), 2)
        else:
            g = _to_shards(mix_grads[2 * layer + 1]["w_out"], 1)
        return g[:, None]

    def reduce_begin(items, tag):
        by_shard = [shard_grad(n, layer) for n, layer in items]
        from_sib = _run_comm(_grads_to_sibling_plan(by_shard), "grads_to_sibling" + tag)
        chip_sums = [_add_core_halves(where, g, r, "add_core_halves_%s%d" % item)
                     for item, g, r in zip(items, by_shard, from_sib)]
        return by_shard, from_sib, _grads_to_chips_plan(chip_sums)

    def reduce_end(items, by_shard, from_sib, from_chips):
        return [_sum_shard(where, g, r, rr, "sum_shard_%s%d" % item)
                for item, g, r, rr in zip(items, by_shard, from_sib, from_chips)]

    reduced_items, half_sums = [], []
    early = {}

    def riding(items, tag):
        by_shard, from_sib, plan = reduce_begin(items, tag)

        def arrived(from_chips):
            reduced_items.extend(items)
            half_sums.extend(reduce_end(items, by_shard, from_sib, from_chips))

        return plan, arrived

    for i in reversed(range(depth)):
        j = i // 2
        sv, sv2 = saved[i]
        dh, ffn_grads[i] = _ffn_bwd(dh, sv2, _row(ffn_pre_g[i]), _row(ffn_post_g[i]), *ffn_args(i), tag="")
        gp, gq = _row(mix_pre_g[i]), _row(mix_post_g[i])
        if i % 2 == 0:
            then = {}

            def in_scan(dw_out, i=i, j=j, then=then):
                early[("ssd_w_out", j)] = dw_out
                plan, then["scan"] = riding(reduce_in_scan[i], "_%d" % i)
                return plan

            def in_dx(dw_in, i=i, j=j, then=then):
                early[("ssd_w_in", j)] = dw_in
                plan, then["dx"] = riding(reduce_in_dx[i], "_dx%d" % i) if i in reduce_in_dx else (None, None)
                return plan

            dh, mix_grads[i], got, got_dx = _ssd_bwd(dh, sv, gp, gq, *ssd_args(j), full("ssd_w_out", j), consts,
                                                     in_scan, tag="", comm_dx=in_dx)
            then["scan"](got)
            if then["dx"] is not None:
                then["dx"](got_dx)
        else:
            w_in, scw, w_out = sc_args(j)
            dh, mix_grads[i] = _sc_bwd(dh, sv, gp, gq, w_in, scw, w_out, tag="")
    grad_x = dh[None]
    pieces = {}
    for (n, layer), g in zip(reduced_items, _swap_halves(half_sums)):
        pieces.setdefault(n, []).append((layer, g))
    big_grads = {n: jnp.concatenate([g for _, g in sorted(p, key=lambda t: t[0])]) for n, p in pieces.items()}
    ssd_l = [mix_grads[i] for i in range(0, depth, 2)]
    sc_l = [mix_grads[i] for i in range(1, depth, 2)]
    stack = lambda layers, k: jnp.stack([g[k] for g in layers])

    small_names = ["mix_pre_g", "mix_post_g", "ffn_pre_g", "ffn_post_g", "ssd_conv_w", "ssd_conv_b", "ssd_dt_bias",
                   "ssd_A_log", "ssd_D", "ssd_norm_w", "sc_conv_w", "ffn_conv_w", "ffn_conv_b"]
    small_local = {
        "mix_pre_g": jnp.concatenate([g["g_pre"] for g in mix_grads]),
        "mix_post_g": jnp.concatenate([g["g_post"] for g in mix_grads]),
        "ffn_pre_g": jnp.concatenate([g["g_pre"] for g in ffn_grads]),
        "ffn_post_g": jnp.concatenate([g["g_post"] for g in ffn_grads]),
        "ssd_conv_w": stack(ssd_l, "cw"), "ssd_conv_b": stack(ssd_l, "cb"), "ssd_dt_bias": stack(ssd_l, "dt_bias"),
        "ssd_A_log": stack(ssd_l, "a_log"), "ssd_D": stack(ssd_l, "d"), "ssd_norm_w": stack(ssd_l, "nw"),
        "sc_conv_w": stack(sc_l, "cw"), "ffn_conv_w": stack(ffn_grads, "cw"), "ffn_conv_b": stack(ffn_grads, "cb"),
    }
    small_full_shapes = [small_local[n].shape for n in small_names]
    spack = _pack_rows([small_local[n].reshape(-1) for n in small_names] + [loss_part.reshape(-1)], LANE, 8)
    stotal = _sum_slots(_allgather_small(spack), "sum_small").reshape(-1)
    small_grads = dict(zip(small_names, _unpack(stotal, small_full_shapes)))
    loss = stotal[sum(int(np.prod(s)) for s in small_full_shapes)]
    for n in conv_names:
        width = wts[n].shape[-1]
        small_grads[n] = lax.dynamic_slice_in_dim(small_grads[n], q_chip * width, width, axis=2)

    grads, delta, new_m, new_v = {}, {}, {}, {}
    for n, _ in BIG:
        s = wts[n].shape
        two_d = lambda a: a.reshape(-1, s[-1])
        grads[n] = big_grads[n]
        d, mn, vn = _adamw(two_d(wts[n]), two_d(mom[n]), two_d(var[n]), two_d(grads[n]), "adamw_" + n)
        delta[n], new_m[n], new_v[n] = d.reshape(s), mn.reshape(s), vn.reshape(s)
    small_shapes = [wts[n].shape for n in small_names]
    pk = lambda d: _pack_rows([d[n].reshape(-1) for n in small_names], LANE, 8)
    for n in small_names:
        grads[n] = small_grads[n].reshape(wts[n].shape)
    d, mn, vn = _adamw(pk(wts), pk(mom), pk(var), pk(grads), "adamw_small")
    for out, packed in ((delta, d), (new_m, mn), (new_v, vn)):
        out.update(zip(small_names, _unpack(packed.reshape(-1), small_shapes)))

    return (loss, grad_x, *[grads[n] for n in names], *[delta[n] for n in names], *[new_m[n] for n in names],
            *[new_v[n] for n in names])
```

```python
from typing import Callable, NamedTuple

import jax
import jax.numpy as jnp
import numpy as np
from jax import lax
from jax.experimental import pallas as pl
from jax.experimental.pallas import tpu as pltpu

F32 = jnp.float32
BF16 = jnp.bfloat16
SDS = jax.ShapeDtypeStruct
MESH = pl.DeviceIdType.MESH
ANY = pl.BlockSpec(memory_space=pl.ANY)

EPS = 1e-6
CHUNK = 64
HEAD_DIM = 64
N_GROUPS = 8
D_STATE = 128
HEADS_PER_GROUP = 4
GROUP_W = HEADS_PER_GROUP * HEAD_DIM
LANE = 128
ROW_TILE = 128
HALO = 8
VMEM_LIMIT = 56 * 1024 * 1024

ADAM_LR = 0.001
ADAM_B1 = 0.9
ADAM_B2 = 0.999
ADAM_EPS = 1e-08
ADAM_WD = 0.01
ADAM_STEP = 10

NT = (((1,), (1,)), ((), ()))
TN = (((0,), (0,)), ((), ()))


def _cp(*sem):
    return pltpu.CompilerParams(dimension_semantics=sem or None, vmem_limit_bytes=VMEM_LIMIT)


def _sigmoid(x):
    return 1.0 / (1.0 + jnp.exp(-x))


def _dsilu(x, s):
    return s * (1.0 + x * (1.0 - s))


def _rsq(x):
    return lax.rsqrt(jnp.mean(x * x, axis=-1, keepdims=True) + EPS)


MM_ROWS = 256


def _norm_matmul(x, g, w, out_dtype, name, comm=None):
    L, D = x.shape
    N = w.shape[1]
    tm = min(MM_ROWS, L)

    def body(x_ref, g_ref, w_ref, o_ref, hn_ref):
        xv = x_ref[...]
        hn = (xv * _rsq(xv) * g_ref[...]).astype(BF16)
        hn_ref[...] = hn
        o_ref[...] = jnp.dot(hn, w_ref[...], preferred_element_type=F32).astype(out_dtype)

    row = lambda i: (i, 0)
    fix = lambda i: (0, 0)
    return _carrier_call(
        body, comm, L // tm,
        in_specs=[pl.BlockSpec((tm, D), row), pl.BlockSpec((1, D), fix), pl.BlockSpec((D, N), fix)],
        out_specs=[pl.BlockSpec((tm, N), row), pl.BlockSpec((tm, D), row)],
        out_shape=[SDS((L, N), out_dtype), SDS((L, D), BF16)], scratch_shapes=[], name=name, args=(x, g, w))


def _matmul_norm_res(a, w, x, g, name, comm=None):
    L, K = a.shape
    D = w.shape[1]
    tm = min(MM_ROWS, L)

    def body(a_ref, w_ref, x_ref, g_ref, m_ref, xo_ref):
        m = jnp.dot(a_ref[...], w_ref[...], preferred_element_type=F32)
        m_ref[...] = m
        xo_ref[...] = x_ref[...] + m * _rsq(m) * g_ref[...]

    row = lambda i: (i, 0)
    fix = lambda i: (0, 0)
    return _carrier_call(
        body, comm, L // tm,
        in_specs=[pl.BlockSpec((tm, K), row), pl.BlockSpec((K, D), fix), pl.BlockSpec((tm, D), row),
                  pl.BlockSpec((1, D), fix)],
        out_specs=[pl.BlockSpec((tm, D), row), pl.BlockSpec((tm, D), row)],
        out_shape=[SDS((L, D), F32), SDS((L, D), F32)], scratch_shapes=[], name=name, args=(a, w, x, g))


def _postnorm_bwd(dx, m, g, name):
    L, D = dx.shape
    tm = min(512, L)

    def body(dx_ref, m_ref, g_ref, dm_ref, dg_ref):
        @pl.when(pl.program_id(0) == 0)
        def _():
            dg_ref[...] = jnp.zeros_like(dg_ref)

        m = m_ref[...]
        dxv = dx_ref[...]
        r = _rsq(m)
        mh = m * r
        dg_ref[...] += jnp.sum(dxv * mh, axis=0, keepdims=True)
        dyg = dxv * g_ref[...]
        dm_ref[...] = (r * (dyg - mh * jnp.mean(dyg * mh, axis=-1, keepdims=True))).astype(BF16)

    row = lambda i: (i, 0)
    fix = lambda i: (0, 0)
    return pl.pallas_call(
        body, grid=(L // tm,),
        in_specs=[pl.BlockSpec((tm, D), row), pl.BlockSpec((tm, D), row), pl.BlockSpec((1, D), fix)],
        out_specs=[pl.BlockSpec((tm, D), row), pl.BlockSpec((1, D), fix)],
        out_shape=[SDS((L, D), BF16), SDS((1, D), F32)],
        name=name, compiler_params=_cp("arbitrary"))(dx, m, g)


def _matmul_nt(a, w, out_dtype, name):
    L, D = a.shape
    K = w.shape[0]
    tm = min(MM_ROWS, L)

    def body(a_ref, w_ref, o_ref):
        o_ref[...] = lax.dot_general(a_ref[...], w_ref[...], NT, preferred_element_type=F32).astype(out_dtype)

    return pl.pallas_call(
        body, grid=(L // tm,),
        in_specs=[pl.BlockSpec((tm, D), lambda i: (i, 0)), pl.BlockSpec((K, D), lambda i: (0, 0))],
        out_specs=pl.BlockSpec((tm, K), lambda i: (i, 0)),
        out_shape=SDS((L, K), out_dtype),
        name=name, compiler_params=_cp("parallel"))(a, w)


def _matmul_tn(a, b, ta, tn, name):
    L, Ka = a.shape
    N = b.shape[1]
    tl = min(512, L)
    n_l = L // tl

    def body(a_ref, b_ref, o_ref, acc_ref):
        l = pl.program_id(2)

        @pl.when(l == 0)
        def _():
            acc_ref[...] = jnp.zeros_like(acc_ref)

        acc_ref[...] += lax.dot_general(a_ref[...], b_ref[...], TN, preferred_element_type=F32)

        @pl.when(l == n_l - 1)
        def _():
            o_ref[...] = acc_ref[...].astype(BF16)

    return pl.pallas_call(
        body, grid=(Ka // ta, N // tn, n_l),
        in_specs=[pl.BlockSpec((tl, ta), lambda i, j, l: (l, i)), pl.BlockSpec((tl, tn), lambda i, j, l: (l, j))],
        out_specs=pl.BlockSpec((ta, tn), lambda i, j, l: (i, j)),
        out_shape=SDS((Ka, N), BF16),
        scratch_shapes=[pltpu.VMEM((ta, tn), F32)],
        name=name, compiler_params=_cp("parallel", "parallel", "arbitrary"))(a, b)


def _ffn_up(x, g, w4, name, comm=None):
    L, D = x.shape
    b = w4.shape[2]
    tm = min(MM_ROWS, L)

    def body(x_ref, g_ref, w_ref, o_ref, hn_ref):
        xv = x_ref[...]
        hn = (xv * _rsq(xv) * g_ref[...]).astype(BF16)
        hn_ref[...] = hn
        for q in range(4):
            o_ref[q // 2, :, (q % 2) * b:(q % 2 + 1) * b] = jnp.dot(hn, w_ref[q], preferred_element_type=F32).astype(BF16)

    return _carrier_call(
        body, comm, L // tm,
        in_specs=[pl.BlockSpec((tm, D), lambda i: (i, 0)), pl.BlockSpec((1, D), lambda i: (0, 0)),
                  pl.BlockSpec((4, D, b), lambda i: (0, 0, 0))],
        out_specs=[pl.BlockSpec((2, tm, 2 * b), lambda i: (0, i, 0)), pl.BlockSpec((tm, D), lambda i: (i, 0))],
        out_shape=[SDS((2, L, 2 * b), BF16), SDS((L, D), BF16)], scratch_shapes=[], name=name, args=(x, g, w4))


def _ffn_up_dx(dup, w4, x, g, dres, name):
    _, L, _ = dup.shape
    _, D, b = w4.shape
    tm = min(MM_ROWS, L)

    def body(dy_ref, w_ref, x_ref, g_ref, dres_ref, dx_ref, dg_ref):
        @pl.when(pl.program_id(0) == 0)
        def _():
            dg_ref[...] = jnp.zeros_like(dg_ref)

        dh = jnp.zeros((tm, D), F32)
        for q in range(4):
            dh = dh + lax.dot_general(dy_ref[q // 2, :, (q % 2) * b:(q % 2 + 1) * b], w_ref[q], NT,
                                      preferred_element_type=F32)
        xv = x_ref[...]
        r = _rsq(xv)
        xh = xv * r
        dg_ref[...] += jnp.sum(dh * xh, axis=0, keepdims=True)
        dyg = dh * g_ref[...]
        dx_ref[...] = dres_ref[...] + r * (dyg - xh * jnp.mean(dyg * xh, axis=-1, keepdims=True))

    row = lambda i: (i, 0)
    fix = lambda i: (0, 0)
    return pl.pallas_call(
        body, grid=(L // tm,),
        in_specs=[pl.BlockSpec((2, tm, 2 * b), lambda i: (0, i, 0)), pl.BlockSpec((4, D, b), lambda i: (0, 0, 0)),
                  pl.BlockSpec((tm, D), row), pl.BlockSpec((1, D), fix), pl.BlockSpec((tm, D), row)],
        out_specs=[pl.BlockSpec((tm, D), row), pl.BlockSpec((1, D), fix)],
        out_shape=[SDS((L, D), F32), SDS((1, D), F32)],
        name=name, compiler_params=_cp("arbitrary"))(dup, w4, x, g, dres)


def _ffn_up_dw(hn, dup, name):
    L, D = hn.shape
    b = dup.shape[2] // 2
    tl = min(512, L)
    n_l = L // tl

    def body(a_ref, b_ref, o_ref, acc_ref):
        l = pl.program_id(1)

        @pl.when(l == 0)
        def _():
            acc_ref[...] = jnp.zeros_like(acc_ref)

        acc_ref[...] += lax.dot_general(a_ref[...], b_ref[...], TN, preferred_element_type=F32)

        @pl.when(l == n_l - 1)
        def _():
            o_ref[...] = acc_ref[...].astype(BF16)

    return pl.pallas_call(
        body, grid=(4, n_l),
        in_specs=[pl.BlockSpec((tl, D), lambda q, l: (l, 0)),
                  pl.BlockSpec((None, tl, b), lambda q, l: (q // 2, l, q % 2))],
        out_specs=pl.BlockSpec((None, D, b), lambda q, l: (q, 0, 0)),
        out_shape=SDS((4, D, b), BF16),
        scratch_shapes=[pltpu.VMEM((D, b), F32)],
        name=name, compiler_params=_cp("parallel", "arbitrary"))(hn, dup)


def _matmul_nt_prenorm_bwd(dy, w, x, g, dres, name, comm=None):
    L, N = dy.shape
    D = w.shape[0]
    tm = min(MM_ROWS, L)

    def body(dy_ref, w_ref, x_ref, g_ref, dres_ref, dx_ref, dg_ref):
        @pl.when(pl.program_id(0) == 0)
        def _():
            dg_ref[...] = jnp.zeros_like(dg_ref)

        dh = lax.dot_general(dy_ref[...], w_ref[...], NT, preferred_element_type=F32)
        xv = x_ref[...]
        r = _rsq(xv)
        xh = xv * r
        dg_ref[...] += jnp.sum(dh * xh, axis=0, keepdims=True)
        dyg = dh * g_ref[...]
        dx_ref[...] = dres_ref[...] + r * (dyg - xh * jnp.mean(dyg * xh, axis=-1, keepdims=True))

    row = lambda i: (i, 0)
    fix = lambda i: (0, 0)
    return _carrier_call(
        body, comm, L // tm,
        in_specs=[pl.BlockSpec((tm, N), row), pl.BlockSpec((D, N), fix), pl.BlockSpec((tm, D), row),
                  pl.BlockSpec((1, D), fix), pl.BlockSpec((tm, D), row)],
        out_specs=[pl.BlockSpec((tm, D), row), pl.BlockSpec((1, D), fix)],
        out_shape=[SDS((L, D), F32), SDS((1, D), F32)], scratch_shapes=[], name=name, args=(dy, w, x, g, dres))


def _loss_head(y, t, name):
    L, D = y.shape
    tm = min(512, L)

    def body(y_ref, t_ref, dy_ref, loss_ref):
        @pl.when(pl.program_id(0) == 0)
        def _():
            loss_ref[...] = jnp.zeros_like(loss_ref)

        e = y_ref[...] - t_ref[...]
        dy_ref[...] = e * (1.0 / D)
        s = jnp.sum(jnp.sum(e * e, axis=1, keepdims=True), axis=0, keepdims=True)
        loss_ref[...] += s * (0.5 / D)

    row = lambda i: (i, 0)
    return pl.pallas_call(
        body, grid=(L // tm,),
        in_specs=[pl.BlockSpec((tm, D), row), pl.BlockSpec((tm, D), row)],
        out_specs=[pl.BlockSpec((tm, D), row), pl.BlockSpec((1, 1), lambda i: (0, 0))],
        out_shape=[SDS((L, D), F32), SDS((1, 1), F32)],
        name=name, compiler_params=_cp("arbitrary"))(y, t)


def _tile_rows(ref):
    return HALO * (4 // jnp.dtype(ref.dtype).itemsize)


def _prev_rows(ref, r0, i, cols):
    n = _tile_rows(ref)
    p0 = pl.multiple_of(jnp.maximum(r0 - n, 0), n)
    return jnp.where(i > 0, ref[pl.ds(p0, n), cols].astype(F32)[n - HALO:], 0.0)


def _next_rows(ref, r0, i, n_tiles, cols):
    n = _tile_rows(ref)
    n0 = pl.multiple_of(jnp.minimum(r0 + ROW_TILE, n_tiles * ROW_TILE - n), n)
    return jnp.where(i < n_tiles - 1, ref[pl.ds(n0, n), cols].astype(F32)[:HALO], 0.0)


def _rows_f32(ref, rows, cols):
    return ref[rows, cols].astype(F32)


def _back(ext, s):
    return pltpu.roll(ext, s, axis=0)[HALO:HALO + ROW_TILE]


def _fwd(ext, s):
    n = ext.shape[0]
    return pltpu.roll(ext, n - s, axis=0)[:ROW_TILE]


def _store_rows(ref, rows):
    ref[...] = jnp.zeros_like(ref)
    for k, v in enumerate(rows):
        ref[k:k + 1, :] = v


def _strip_call(body, L, n_strips, ins, outs, name):
    def spec(rows, width, off):
        if off is None:
            return pl.BlockSpec((rows, width), lambda j: (0, 0))
        return pl.BlockSpec((rows, width), lambda j: (0, j + off))

    return pl.pallas_call(
        body, grid=(n_strips,),
        in_specs=[spec(a.shape[0], w, off) for a, w, off in ins],
        out_specs=[spec(s.shape[0], w, off) for s, w, off in outs],
        out_shape=[s for s, _, _ in outs],
        name=name, compiler_params=_cp("parallel"))(*[a for a, _, _ in ins])


def _ffn_mid_fwd(up, cw, cb, name):
    _, L, C = up.shape
    n_tiles = L // ROW_TILE
    al = slice(None)

    def body(up_ref, cw_ref, cb_ref, a_ref):
        w0, w1, w2 = cw_ref[0:1, :], cw_ref[1:2, :], cw_ref[2:3, :]
        b = cb_ref[...]
        gate_ref, val_ref = up_ref.at[0], up_ref.at[1]

        def step(i, c):
            r0 = pl.multiple_of(i * ROW_TILE, ROW_TILE)
            rows = pl.ds(r0, ROW_TILE)
            gp = _rows_f32(gate_ref, rows, al)
            ext = jnp.concatenate([_prev_rows(gate_ref, r0, i, al), gp], axis=0)
            gate = gp * w2 + _back(ext, 1) * w1 + _back(ext, 2) * w0 + b
            a_ref[rows, :] = (gate * _sigmoid(gate) * _rows_f32(val_ref, rows, al)).astype(BF16)
            return c

        lax.fori_loop(0, n_tiles, step, 0)

    strip = lambda rows: pl.BlockSpec((rows, LANE), lambda j: (0, j))
    return pl.pallas_call(
        body, grid=(C // LANE,),
        in_specs=[pl.BlockSpec((2, L, LANE), lambda j: (0, 0, j)), strip(cw.shape[0]), strip(1)],
        out_specs=strip(L), out_shape=SDS((L, C), BF16), name=name, compiler_params=_cp("parallel"))(up, cw, cb)


def _ffn_mid_bwd(da, up, cw, cb, name):
    L, C = da.shape
    n_tiles = L // ROW_TILE
    al = slice(None)

    def body(da_ref, up_ref, cw_ref, cb_ref, dup_ref, st_ref):
        w0, w1, w2 = cw_ref[0:1, :], cw_ref[1:2, :], cw_ref[2:3, :]
        b = cb_ref[...]
        gate_ref, val_ref = up_ref.at[0], up_ref.at[1]

        def step(i, c):
            r0 = pl.multiple_of(i * ROW_TILE, ROW_TILE)
            rows = pl.ds(r0, ROW_TILE)
            gp = _rows_f32(gate_ref, rows, al)
            gpe = jnp.concatenate([_prev_rows(gate_ref, r0, i, al), gp, _next_rows(gate_ref, r0, i, n_tiles, al)],
                                  axis=0)
            g1, g2 = pltpu.roll(gpe, 1, axis=0), pltpu.roll(gpe, 2, axis=0)
            gate = (gpe * w2 + g1 * w1 + g2 * w0 + b)[HALO:]
            sg = _sigmoid(gate)
            da_e = jnp.concatenate([_rows_f32(da_ref, rows, al), _next_rows(da_ref, r0, i, n_tiles, al)], axis=0)
            val_e = jnp.concatenate([_rows_f32(val_ref, rows, al), _next_rows(val_ref, r0, i, n_tiles, al)], axis=0)
            dgate_e = da_e * val_e * _dsilu(gate, sg)
            dgate = dgate_e[:ROW_TILE]
            dgp = dgate * w2 + _fwd(dgate_e, 1) * w1 + _fwd(dgate_e, 2) * w0
            dup_ref[0, rows, :] = dgp.astype(BF16)
            dup_ref[1, rows, :] = (da_e * gate * sg)[:ROW_TILE].astype(BF16)
            s = lambda v: jnp.sum(v, axis=0, keepdims=True)
            t = slice(HALO, HALO + ROW_TILE)
            return (c[0] + s(dgate * g2[t]), c[1] + s(dgate * g1[t]), c[2] + s(dgate * gp), c[3] + s(dgate))

        z = jnp.zeros((1, LANE), F32)
        _store_rows(st_ref, lax.fori_loop(0, n_tiles, step, (z, z, z, z)))

    strip = lambda rows: pl.BlockSpec((rows, LANE), lambda j: (0, j))
    pair = pl.BlockSpec((2, L, LANE), lambda j: (0, 0, j))
    return pl.pallas_call(
        body, grid=(C // LANE,), in_specs=[strip(L), pair, strip(cw.shape[0]), strip(1)],
        out_specs=[pair, strip(8)], out_shape=[SDS((2, L, C), BF16), SDS((8, C), F32)],
        name=name, compiler_params=_cp("parallel"))(da, up, cw, cb)


def _sc_mid_fwd(bcv, cw, name):
    L = bcv.shape[0]
    C = bcv.shape[1] // 3
    n_tiles = L // ROW_TILE
    s0, s1, s2 = slice(0, LANE), slice(LANE, 2 * LANE), slice(2 * LANE, 3 * LANE)

    def body(x_ref, cw_ref, q_ref):
        w0, w1, w2 = cw_ref[0:1, :], cw_ref[1:2, :], cw_ref[2:3, :]

        def step(i, c):
            r0 = pl.multiple_of(i * ROW_TILE, ROW_TILE)
            rows = pl.ds(r0, ROW_TILE)
            p = _rows_f32(x_ref, rows, s1) * _rows_f32(x_ref, rows, s2)
            ext = jnp.concatenate([_prev_rows(x_ref, r0, i, s1) * _prev_rows(x_ref, r0, i, s2), p], axis=0)
            u = p * w2 + _back(ext, 1) * w1 + _back(ext, 2) * w0
            q_ref[rows, :] = (_rows_f32(x_ref, rows, s0) * u).astype(BF16)
            return c

        lax.fori_loop(0, n_tiles, step, 0)

    return _strip_call(body, L, C // LANE, [(bcv, 3 * LANE, 0), (cw, LANE, 0)],
                       [(SDS((L, C), BF16), LANE, 0)], name)[0]


def _sc_mid_bwd(dq, bcv, cw, name):
    L, C = dq.shape
    n_tiles = L // ROW_TILE
    s0, s1, s2, al = slice(0, LANE), slice(LANE, 2 * LANE), slice(2 * LANE, 3 * LANE), slice(None)

    def body(dq_ref, x_ref, cw_ref, dx_ref, st_ref):
        w0, w1, w2 = cw_ref[0:1, :], cw_ref[1:2, :], cw_ref[2:3, :]

        def step(i, c):
            r0 = pl.multiple_of(i * ROW_TILE, ROW_TILE)
            rows = pl.ds(r0, ROW_TILE)
            gb, gc, v = _rows_f32(x_ref, rows, s0), _rows_f32(x_ref, rows, s1), _rows_f32(x_ref, rows, s2)
            dq_v = _rows_f32(dq_ref, rows, al)
            p = gc * v
            pext = jnp.concatenate([_prev_rows(x_ref, r0, i, s1) * _prev_rows(x_ref, r0, i, s2), p], axis=0)
            p1, p2 = _back(pext, 1), _back(pext, 2)
            u = p * w2 + p1 * w1 + p2 * w0
            du = dq_v * gb
            du_n = _next_rows(dq_ref, r0, i, n_tiles, al) * _next_rows(x_ref, r0, i, n_tiles, s0)
            ext = jnp.concatenate([du, du_n], axis=0)
            dp = du * w2 + _fwd(ext, 1) * w1 + _fwd(ext, 2) * w0
            dx_ref[rows, s0] = (dq_v * u).astype(BF16)
            dx_ref[rows, s1] = (dp * v).astype(BF16)
            dx_ref[rows, s2] = (dp * gc).astype(BF16)
            s = lambda t: jnp.sum(t, axis=0, keepdims=True)
            return (c[0] + s(du * p2), c[1] + s(du * p1), c[2] + s(du * p))

        z = jnp.zeros((1, LANE), F32)
        _store_rows(st_ref, lax.fori_loop(0, n_tiles, step, (z, z, z)))

    return _strip_call(body, L, C // LANE, [(dq, LANE, 0), (bcv, 3 * LANE, 0), (cw, LANE, 0)],
                       [(SDS((L, 3 * C), BF16), 3 * LANE, 0), (SDS((8, C), F32), LANE, 0)], name)


def _ssd_conv_fwd(zx, cw, cb, col0, C, name):
    L = zx.shape[0]
    n_tiles = L // ROW_TILE
    al = slice(None)

    def body(x_ref, cw_ref, cb_ref, o_ref):
        w0, w1, w2, w3 = cw_ref[0:1, :], cw_ref[1:2, :], cw_ref[2:3, :], cw_ref[3:4, :]
        b = cb_ref[...]

        def step(i, c):
            r0 = pl.multiple_of(i * ROW_TILE, ROW_TILE)
            rows = pl.ds(r0, ROW_TILE)
            xv = x_ref[rows, :]
            ext = jnp.concatenate([_prev_rows(x_ref, r0, i, al), xv], axis=0)
            cv = xv * w3 + _back(ext, 1) * w2 + _back(ext, 2) * w1 + _back(ext, 3) * w0 + b
            o_ref[rows, :] = cv * _sigmoid(cv)
            return c

        lax.fori_loop(0, n_tiles, step, 0)

    return _strip_call(body, L, C // LANE, [(zx, LANE, col0 // LANE), (cw, LANE, 0), (cb, LANE, 0)],
                       [(SDS((L, C), F32), LANE, 0)], name)[0]


def _ssd_conv_bwd(dxbc, zx, cw, cb, col0, dzx, name):
    L, C = dxbc.shape
    n_tiles = L // ROW_TILE
    al = slice(None)

    def body(d_ref, x_ref, cw_ref, cb_ref, dzx_in_ref, o_ref, st_ref):
        w0, w1, w2, w3 = cw_ref[0:1, :], cw_ref[1:2, :], cw_ref[2:3, :], cw_ref[3:4, :]
        b = cb_ref[...]

        def step(i, c):
            r0 = pl.multiple_of(i * ROW_TILE, ROW_TILE)
            rows = pl.ds(r0, ROW_TILE)
            xv = x_ref[rows, :]
            xe = jnp.concatenate([_prev_rows(x_ref, r0, i, al), xv, _next_rows(x_ref, r0, i, n_tiles, al)], axis=0)
            x1, x2, x3 = pltpu.roll(xe, 1, axis=0), pltpu.roll(xe, 2, axis=0), pltpu.roll(xe, 3, axis=0)
            cv = (xe * w3 + x1 * w2 + x2 * w1 + x3 * w0 + b)[HALO:]
            de = jnp.concatenate([d_ref[rows, :], _next_rows(d_ref, r0, i, n_tiles, al)], axis=0)
            dc_ext = de * _dsilu(cv, _sigmoid(cv))
            dc = dc_ext[:ROW_TILE]
            o_ref[rows, :] = (dc * w3 + _fwd(dc_ext, 1) * w2 + _fwd(dc_ext, 2) * w1 + _fwd(dc_ext, 3) * w0).astype(BF16)
            s = lambda t: jnp.sum(t, axis=0, keepdims=True)
            t = slice(HALO, HALO + ROW_TILE)
            return (c[0] + s(dc * x3[t]), c[1] + s(dc * x2[t]), c[2] + s(dc * x1[t]), c[3] + s(dc * xv), c[4] + s(dc))

        z = jnp.zeros((1, LANE), F32)
        _store_rows(st_ref, lax.fori_loop(0, n_tiles, step, (z, z, z, z, z)))

    strip = lambda rows, off=0: pl.BlockSpec((rows, LANE), lambda j: (0, j + off))
    shifted = strip(L, col0 // LANE)
    return pl.pallas_call(
        body, grid=(C // LANE,), in_specs=[strip(L), shifted, strip(cw.shape[0]), strip(1), ANY],
        out_specs=[shifted, strip(8)], out_shape=[SDS(dzx.shape, dzx.dtype), SDS((8, C), F32)],
        input_output_aliases={4: 0}, name=name, compiler_params=_cp("parallel"))(dxbc, zx, cw, cb, dzx)


def _scan_constants(n_heads):
    hw = n_heads * HEAD_DIM
    col = np.arange(hw)
    ind = (col[None, :] // HEAD_DIM == np.arange(LANE)[:, None]).astype(np.float32)
    gcol = np.arange(GROUP_W)
    itile = (gcol[None, :] % CHUNK == np.arange(CHUNK)[:, None]).astype(np.float32)
    trit = (gcol[None, :] % CHUNK <= np.arange(CHUNK)[:, None]).astype(np.float32)
    tril = np.tril(np.ones((CHUNK, CHUNK), np.float32))
    bmask = (gcol[:, None] // HEAD_DIM == gcol[None, :] // HEAD_DIM).astype(np.float32)
    return (jnp.asarray(ind, BF16), jnp.asarray(ind.T.copy(), BF16), jnp.asarray(itile), jnp.asarray(trit),
            jnp.asarray(tril, BF16), jnp.asarray(bmask))


def _softplus(x):
    return jnp.maximum(x, 0.0) + jnp.log(1.0 + jnp.exp(-jnp.abs(x)))


def _split3(x):
    hi = x.astype(BF16)
    r1 = x - hi.astype(F32)
    mid = r1.astype(BF16)
    return hi, mid, (r1 - mid.astype(F32)).astype(BF16)


def _dot_sel(x, sel, dims=None):
    if dims is None:
        mm = lambda p: jnp.dot(p, sel, preferred_element_type=F32)
    else:
        mm = lambda p: lax.dot_general(sel, p, dims, preferred_element_type=F32)
    hi, mid, lo = _split3(x)
    return (mm(lo) + mm(mid)) + mm(hi)


SEL_X = (((1,), (0,)), ((), ()))


def _group_terms(g, dt, cs, xbc_ref, ind_ref, itile, trit, bmask, d_inner):
    gl = slice(g * GROUP_W, (g + 1) * GROUP_W)
    indg = ind_ref[:, gl]
    csl = _dot_sel(cs, indg)
    dtx = _dot_sel(dt, indg)
    rr = _dot_sel(csl * itile, jnp.ones((CHUNK, CHUNK), BF16), SEL_X)
    lm = jnp.exp(jnp.where(trit > 0.0, csl - rr, -jnp.inf))
    xs = xbc_ref[:, gl]
    b = xbc_ref[:, d_inner + g * D_STATE: d_inner + (g + 1) * D_STATE]
    c = xbc_ref[:, d_inner + (N_GROUPS + g) * D_STATE: d_inner + (N_GROUPS + g + 1) * D_STATE]
    u = xs * dtx
    bb, cb = b.astype(BF16), c.astype(BF16)
    btile = jnp.concatenate([bb] * HEADS_PER_GROUP, axis=0)
    cbt = lax.dot_general(cb, btile, NT, preferred_element_type=F32)
    m = cbt * lm
    ub = u.astype(BF16)
    bdu = jnp.where(bmask > 0.0, jnp.concatenate([ub] * HEADS_PER_GROUP, axis=0), jnp.zeros((), BF16))
    c_last = csl[CHUNK - 1:CHUNK, :]
    return dict(gl=gl, indg=indg, csl=csl, dtx=dtx, lm=lm, xs=xs, bb=bb, cb=cb, u=u, btile=btile, m=m, bdu=bdu,
                e=jnp.exp(csl), dec=jnp.exp(c_last - csl), e_last=jnp.exp(c_last))


def _ssd_scan_fwd(zx, xbc, par, dexp, nw, consts, comm, name):
    L = xbc.shape[0]
    d_inner = dexp.shape[1]
    n_chunks = L // CHUNK
    dt_blk = zx.shape[1] // LANE - 1
    ind, ind_t, itile_c, trit_c, tril_c, bmask_c = consts

    def body(xbc_ref, z_ref, dtr_ref, par_ref, dexp_ref, nw_ref, ind_ref, itile_ref, trit_ref, tril_ref, bmask_ref,
             yn_ref, yf_ref, st_out_ref, st_ref):
        @pl.when(pl.program_id(0) == 0)
        def _():
            st_ref[...] = jnp.zeros_like(st_ref)

        dt = _softplus(dtr_ref[...] + par_ref[0:1, :])
        a_head = -jnp.exp(par_ref[1:2, :])
        cs = _dot_sel(dt * a_head, tril_ref[...], SEL_X)
        itile, trit, bmask = itile_ref[...], trit_ref[...], bmask_ref[...]
        for g in range(N_GROUPS):
            t = _group_terms(g, dt, cs, xbc_ref, ind_ref, itile, trit, bmask, d_inner)
            p = st_ref[g]
            st_out_ref[0, g] = p
            y = jnp.dot(t["m"].astype(BF16), t["bdu"], preferred_element_type=F32)
            y = y + jnp.dot(t["cb"], p.astype(BF16), preferred_element_type=F32) * t["e"]
            st_new = lax.dot_general(t["bb"], (t["u"] * t["dec"]).astype(BF16), TN, preferred_element_type=F32)
            st_ref[g] = p * t["e_last"] + st_new
            yf_ref[:, t["gl"]] = y + t["xs"] * dexp_ref[:, t["gl"]]
        z = z_ref[...]
        y2 = yf_ref[...] * (z * _sigmoid(z))
        yn_ref[...] = (y2 * _rsq(y2) * nw_ref[...]).astype(BF16)

    row = lambda c: (c, 0)
    fix = lambda c: (0, 0)
    cspec = lambda a: pl.BlockSpec(a.shape, fix)
    return _carrier_call(
        body, comm, n_chunks,
        in_specs=[pl.BlockSpec((CHUNK, xbc.shape[1]), row), pl.BlockSpec((CHUNK, d_inner), row),
                  pl.BlockSpec((CHUNK, LANE), lambda c: (c, dt_blk)), cspec(par), cspec(dexp), cspec(nw),
                  cspec(ind), cspec(itile_c), cspec(trit_c), cspec(tril_c), cspec(bmask_c)],
        out_specs=[pl.BlockSpec((CHUNK, d_inner), row), pl.BlockSpec((CHUNK, d_inner), row),
                   pl.BlockSpec((1, N_GROUPS, D_STATE, GROUP_W), lambda c: (c, 0, 0, 0))],
        out_shape=[SDS((L, d_inner), BF16), SDS((L, d_inner), F32),
                   SDS((n_chunks, N_GROUPS, D_STATE, GROUP_W), F32)],
        scratch_shapes=[pltpu.VMEM((N_GROUPS, D_STATE, GROUP_W), F32)],
        name=name, args=(xbc, zx, zx, par, dexp, nw, ind, itile_c, trit_c, tril_c, bmask_c))


def _ssd_scan_bwd(dyn, yf, zx, xbc, states, par, dexp, nw, consts, comm, name):
    L = xbc.shape[0]
    d_inner = dexp.shape[1]
    n_chunks = L // CHUNK
    nz = zx.shape[1]
    dt_blk = nz // LANE - 1
    ind, ind_t, itile_c, trit_c, tril_c, bmask_c = consts
    hslices = [slice(r * HEAD_DIM, (r + 1) * HEAD_DIM) for r in range(HEADS_PER_GROUP)]

    def body(dyn_ref, yf_ref, z_ref, dtr_ref, xbc_ref, st_in_ref, par_ref, dexp_ref, nw_ref, ind_ref, indt_ref,
             itile_ref, trit_ref, tril_ref, bmask_ref,
             dzx_ref, dxbc_ref, dnw_ref, dpar_ref, dq_ref, dyf_ref):
        @pl.when(pl.program_id(0) == 0)
        def _():
            dq_ref[...] = jnp.zeros_like(dq_ref)
            dnw_ref[...] = jnp.zeros_like(dnw_ref)
            dpar_ref[...] = jnp.zeros_like(dpar_ref)

        z, yfv, dynv = z_ref[...], yf_ref[...], dyn_ref[...]
        sz = _sigmoid(z)
        y2 = yfv * (z * sz)
        r = _rsq(y2)
        y2h = y2 * r
        dnw_ref[...] += jnp.sum(dynv * y2h, axis=0, keepdims=True)
        dyg = dynv * nw_ref[...]
        dy2 = r * (dyg - y2h * jnp.mean(dyg * y2h, axis=-1, keepdims=True))
        dzx_ref[:, 0:d_inner] = (dy2 * yfv * _dsilu(z, sz)).astype(BF16)
        dyf_ref[...] = dy2 * (z * sz)

        pre = dtr_ref[...] + par_ref[0:1, :]
        dt = _softplus(pre)
        a_head = -jnp.exp(par_ref[1:2, :])
        cs = _dot_sel(dt * a_head, tril_ref[...], SEL_X)
        itile, trit, bmask = itile_ref[...], trit_ref[...], bmask_ref[...]
        dcs = jnp.zeros((CHUNK, LANE), F32)
        dcs_last = jnp.zeros((1, LANE), F32)
        ddt_u = jnp.zeros((CHUNK, LANE), F32)
        d_skip = jnp.zeros((1, LANE), F32)
        rsum = lambda v: jnp.sum(v, axis=0, keepdims=True)
        row8 = lax.broadcasted_iota(jnp.int32, (8, GROUP_W), 0)
        for g in range(N_GROUPS):
            t = _group_terms(g, dt, cs, xbc_ref, ind_ref, itile, trit, bmask, d_inner)
            gl, m, lm, u, bb, cb, e, dec, xs = (t[k] for k in ("gl", "m", "lm", "u", "bb", "cb", "e", "dec", "xs"))
            indt = indt_ref[gl, :]
            dy = dyf_ref[:, gl]
            dyb = dy.astype(BF16)
            p = st_in_ref[0, g]
            pb = p.astype(BF16)
            q = dq_ref[g]
            qb = q.astype(BF16)
            big = lax.dot_general(m.astype(BF16), dyb, TN, preferred_element_type=F32)
            du = jnp.zeros((CHUNK, GROUP_W), F32)
            for rh in range(HEADS_PER_GROUP):
                du = du + big[hslices[rh], :] * bmask[rh * HEAD_DIM:rh * HEAD_DIM + 1, :]
            dm = lax.dot_general(dyb, t["bdu"], NT, preferred_element_type=F32)
            w = dm * m
            dgt = (dm * lm).astype(BF16)
            dc = jnp.dot(dgt, t["btile"], preferred_element_type=F32)
            db_big = lax.dot_general(dgt, cb, TN, preferred_element_type=F32)
            db = db_big[hslices[0], :] + db_big[hslices[1], :] + db_big[hslices[2], :] + db_big[hslices[3], :]
            cp = jnp.dot(cb, pb, preferred_element_type=F32)
            dye = dy * e
            dyeb = dye.astype(BF16)
            dc = dc + lax.dot_general(dyeb, pb, NT, preferred_element_type=F32)
            dp = lax.dot_general(cb, dyeb, TN, preferred_element_type=F32)
            x2 = dye * cp
            bq = jnp.dot(bb, qb, preferred_element_type=F32)
            ud = u * dec
            du = du + bq * dec
            db = db + lax.dot_general(ud.astype(BF16), qb, NT, preferred_element_type=F32)
            x1 = bq * ud
            dq_ref[g] = dp + t["e_last"] * q
            x3 = rsum(q * p) * t["e_last"]
            red = _dot_sel(jnp.concatenate([w + x2 - x1, du * xs, itile * rsum(w)], axis=0), indt)
            dcs = dcs + red[0:CHUNK] - red[2 * CHUNK:3 * CHUNK]
            ddt_u = ddt_u + red[CHUNK:2 * CHUNK]
            tail = _dot_sel(jnp.where(row8 == 0, rsum(x1) + x3, jnp.where(row8 == 1, rsum(dy * xs), 0.0)), indt)
            dcs_last = dcs_last + tail[0:1]
            d_skip = d_skip + tail[1:2]
            dxbc_ref[:, gl] = du * t["dtx"] + dy * dexp_ref[:, gl]
            dxbc_ref[:, d_inner + g * D_STATE: d_inner + (g + 1) * D_STATE] = db
            dxbc_ref[:, d_inner + (N_GROUPS + g) * D_STATE: d_inner + (N_GROUPS + g + 1) * D_STATE] = dc
        last = lax.broadcasted_iota(jnp.int32, (CHUNK, LANE), 0) == CHUNK - 1
        dcs = dcs + jnp.where(last, dcs_last, 0.0)
        da = _dot_sel(dcs, tril_ref[...], TN)
        ddt = da * a_head + ddt_u
        heads = lax.broadcasted_iota(jnp.int32, (CHUNK, LANE), 1) < d_inner // HEAD_DIM
        ddt_raw = jnp.where(heads, ddt * _sigmoid(pre), 0.0)
        dzx_ref[:, nz - LANE:nz] = ddt_raw.astype(BF16)
        dpar_ref[0:1, :] += rsum(ddt_raw)
        dpar_ref[1:2, :] += rsum(da * dt) * a_head
        dpar_ref[2:3, :] += d_skip

    rev = lambda c: (n_chunks - 1 - c, 0)
    fix = lambda c: (0, 0)
    cspec = lambda a: pl.BlockSpec(a.shape, fix)
    nx = xbc.shape[1]
    return _carrier_call(
        body, comm, n_chunks,
        in_specs=[pl.BlockSpec((CHUNK, d_inner), rev), pl.BlockSpec((CHUNK, d_inner), rev),
                  pl.BlockSpec((CHUNK, d_inner), rev), pl.BlockSpec((CHUNK, LANE), lambda c: (n_chunks - 1 - c, dt_blk)),
                  pl.BlockSpec((CHUNK, nx), rev),
                  pl.BlockSpec((1, N_GROUPS, D_STATE, GROUP_W), lambda c: (n_chunks - 1 - c, 0, 0, 0)),
                  cspec(par), cspec(dexp), cspec(nw), cspec(ind), cspec(ind_t), cspec(itile_c), cspec(trit_c),
                  cspec(tril_c), cspec(bmask_c)],
        out_specs=[pl.BlockSpec((CHUNK, nz), rev), pl.BlockSpec((CHUNK, nx), rev),
                   pl.BlockSpec((1, d_inner), fix), pl.BlockSpec((8, LANE), fix)],
        out_shape=[SDS((L, nz), BF16), SDS((L, nx), F32), SDS((1, d_inner), F32), SDS((8, LANE), F32)],
        scratch_shapes=[pltpu.VMEM((N_GROUPS, D_STATE, GROUP_W), F32), pltpu.VMEM((CHUNK, d_inner), F32)],
        name=name, args=(dyn, yf, zx, zx, xbc, states, par, dexp, nw, ind, ind_t, itile_c, trit_c, tril_c, bmask_c))


def _adamw(w, m, v, g, name):
    R, C = w.shape
    tr = R
    for cand in (256, 128, 64, 32, 16, 8):
        if R % cand == 0:
            tr = cand
            break

    def body(w_ref, m_ref, v_ref, g_ref, d_ref, mo_ref, vo_ref):
        gv = g_ref[...]
        mn = ADAM_B1 * m_ref[...] + (1.0 - ADAM_B1) * gv
        vn = ADAM_B2 * v_ref[...] + (1.0 - ADAM_B2) * (gv * gv)
        m_hat = mn / (1.0 - ADAM_B1 ** ADAM_STEP)
        v_hat = vn / (1.0 - ADAM_B2 ** ADAM_STEP)
        d_ref[...] = -ADAM_LR * (m_hat / (jnp.sqrt(v_hat) + ADAM_EPS) + ADAM_WD * w_ref[...])
        mo_ref[...] = mn
        vo_ref[...] = vn

    blk = pl.BlockSpec((tr, C), lambda i: (i, 0))
    return pl.pallas_call(
        body, grid=(R // tr,), in_specs=[blk] * 4, out_specs=[blk] * 3, out_shape=[SDS((R, C), F32)] * 3,
        name=name, compiler_params=_cp("parallel"))(w, m, v, g)


def _sum_slots(parts, name):
    n, R, C = parts.shape
    tr = 128 if R % 128 == 0 else R

    def body(p_ref, o_ref):
        acc = p_ref[0]
        for k in range(1, n):
            acc = acc + p_ref[k]
        o_ref[...] = acc

    return pl.pallas_call(
        body, grid=(R // tr,), in_specs=[pl.BlockSpec((n, tr, C), lambda i: (0, i, 0))],
        out_specs=pl.BlockSpec((tr, C), lambda i: (i, 0)), out_shape=SDS((R, C), F32),
        name=name, compiler_params=_cp("parallel"))(parts)


def _add_core_halves(where, g, r, name):
    _, n, a, b = g.shape
    ta = a // 2

    def body(w_ref, g_ref, r_ref, o_ref):
        o_ref[...] = (g_ref[...].astype(F32) + r_ref[...].astype(F32)).astype(BF16)

    blk = lambda f: pl.BlockSpec((None, None, ta, b), f)
    mine = lambda s, l, w: (s, l, 0, 0)
    return pl.pallas_call(
        body, grid_spec=pltpu.PrefetchScalarGridSpec(
            num_scalar_prefetch=1, grid=(4, n),
            in_specs=[blk(lambda s, l, w: (s, l, w[1], 0)), blk(mine)], out_specs=blk(mine)),
        out_shape=SDS((4, n, ta, b), BF16), name=name,
        compiler_params=_cp("parallel", "parallel"))(where, g, r)


def _sum_shard(where, g, r, rr, into, layer, n_layers, name):
    _, _, a, b = g.shape
    ta = a // 2

    def body(w_ref, g_ref, r_ref, rr_ref, *refs):
        f = lambda v: v.astype(F32)
        refs[-1][...] = (((f(g_ref[...]) + f(r_ref[...])) + f(rr_ref[0])) + f(rr_ref[1])) + f(rr_ref[2])

    more = [] if into is None else [into]
    return pl.pallas_call(
        body, grid_spec=pltpu.PrefetchScalarGridSpec(
            num_scalar_prefetch=1, grid=(1,),
            in_specs=[pl.BlockSpec((None, None, ta, b), lambda l, w: (w[0], 0, w[1], 0)),
                      pl.BlockSpec((None, None, ta, b), lambda l, w: (w[0], 0, 0, 0)),
                      pl.BlockSpec((3, None, ta, b), lambda l, w: (0, 0, 0, 0))] + [ANY] * len(more),
            out_specs=pl.BlockSpec((None, ta, b), lambda l, w: (layer, w[1], 0))),
        out_shape=SDS((n_layers, a, b), F32), name=name, input_output_aliases={4: 0} if more else {},
        compiler_params=_cp("arbitrary"))(where, g, r, rr, *more)


def _me():
    return lax.axis_index("x"), lax.axis_index("y"), lax.axis_index("c")


def _chip_peers(x, y):
    return [(1 - x, y), (x, 1 - y), (1 - x, 1 - y)]


def _rcopy(src, dst, send_sems, recv_sems, k, to):
    return pltpu.make_async_remote_copy(src_ref=src, dst_ref=dst, send_sem=send_sems.at[k], recv_sem=recv_sems.at[k],
                                        device_id=to, device_id_type=MESH)


def _row_half(ref, c, lead=()):
    a = ref.shape[len(lead) + 1]
    return ref.at[(*lead, slice(None), pl.ds(c * (a // 2), a // 2))]


class _Comm(NamedTuple):
    ins: list
    out_shapes: list
    n_sems: int
    start: Callable
    finish: Callable


def _sem_scratch(comm):
    return [pltpu.SemaphoreType.DMA((comm.n_sems,)), pltpu.SemaphoreType.DMA((comm.n_sems,))]


def _run_comm(comm, name):
    n_in, n_out = len(comm.ins), len(comm.out_shapes)

    def body(*refs):
        ins, outs, sems = refs[:n_in], refs[n_in:n_in + n_out], refs[n_in + n_out:]
        comm.start(ins, outs, *sems)
        comm.finish(ins, outs, *sems)

    return pl.pallas_call(body, in_specs=[ANY] * n_in, out_specs=[ANY] * n_out, out_shape=comm.out_shapes,
                          scratch_shapes=_sem_scratch(comm), name=name)(*comm.ins)


def _carrier_call(compute, comm, n_steps, in_specs, out_specs, out_shape, scratch_shapes, name, args):
    if comm is None:
        return pl.pallas_call(compute, grid=(n_steps,), in_specs=in_specs, out_specs=out_specs, out_shape=out_shape,
                              scratch_shapes=scratch_shapes, name=name, compiler_params=_cp("arbitrary"))(*args), []
    n_in, n_out, n_scr = len(in_specs), len(out_specs), len(scratch_shapes)
    n_ci, n_co = len(comm.ins), len(comm.out_shapes)

    def body(*refs):
        ins, cins = refs[:n_in], refs[n_in:n_in + n_ci]
        o = n_in + n_ci
        outs, couts = refs[o:o + n_out], refs[o + n_out:o + n_out + n_co]
        s = o + n_out + n_co
        scratch, sems = refs[s:s + n_scr], refs[s + n_scr:]

        @pl.when(pl.program_id(0) == 0)
        def _():
            comm.start(cins, couts, *sems)

        compute(*ins, *outs, *scratch)

        @pl.when(pl.program_id(0) == n_steps - 1)
        def _():
            comm.finish(cins, couts, *sems)

    res = pl.pallas_call(
        body, grid=(n_steps,), in_specs=list(in_specs) + [ANY] * n_ci, out_specs=list(out_specs) + [ANY] * n_co,
        out_shape=list(out_shape) + list(comm.out_shapes), scratch_shapes=list(scratch_shapes) + _sem_scratch(comm),
        name=name, compiler_params=_cp("arbitrary"))(*args, *comm.ins)
    return res[:n_out], res[n_out:]


def _allgather_plan(mine, small=None):
    n = len(mine)
    ins = list(mine) + ([] if small is None else [small])
    out_shapes = [SDS((4,) + m.shape, BF16) for m in mine] + ([] if small is None else [SDS((4,) + small.shape, F32)])
    sem = lambda t, k: 7 * t + k

    def first_copies(ins_r, outs_r, send, recv):
        x, y, c = _me()
        q = 2 * x + y
        cps = []
        for j, chip in enumerate(_chip_peers(x, y)):
            for t in range(n):
                cps.append(_rcopy(_row_half(ins_r[t], c), _row_half(outs_r[t], c, (q,)), send, recv, sem(t, j),
                                  (*chip, c)))
            if small is not None:
                cps.append(_rcopy(ins_r[n], outs_r[n].at[q], send, recv, sem(n, j), (*chip, c)))
        for t in range(len(ins)):
            cps.append(_rcopy(ins_r[t], outs_r[t].at[q], send, recv, sem(t, 6), (x, y, 1 - c)))
        return cps

    def start(ins_r, outs_r, send, recv):
        for cp in first_copies(ins_r, outs_r, send, recv):
            cp.start()

    def finish(ins_r, outs_r, send, recv):
        x, y, c = _me()
        sib = (x, y, 1 - c)
        chips = _chip_peers(x, y)
        passed = []
        for j, (px, py) in enumerate(chips):
            for t in range(n):
                blk = _row_half(outs_r[t], c, (2 * px + py,))
                _rcopy(blk, blk, send, recv, sem(t, j), sib).wait_recv()
                cp = _rcopy(blk, blk, send, recv, sem(t, 3 + j), sib)
                cp.start()
                passed.append(cp)
        for j, (px, py) in enumerate(chips):
            for t in range(n):
                blk = _row_half(outs_r[t], 1 - c, (2 * px + py,))
                _rcopy(blk, blk, send, recv, sem(t, 3 + j), sib).wait_recv()
            if small is not None:
                sblk = outs_r[n].at[2 * px + py]
                _rcopy(sblk, sblk, send, recv, sem(n, j), sib).wait_recv()
        for t in range(len(ins)):
            own = outs_r[t].at[2 * x + y]
            _rcopy(own, own, send, recv, sem(t, 6), sib).wait_recv()
        for cp in first_copies(ins_r, outs_r, send, recv) + passed:
            cp.wait_send()

    return _Comm(ins, out_shapes, 7 * len(ins), start, finish)


def _grads_to_sibling_plan(grads):
    n = len(grads)

    def copies(ins_r, outs_r, send, recv):
        x, y, c = _me()
        return [_rcopy(_row_half(ins_r[t], 1 - c, (slice(None),)), outs_r[t], send, recv, t, (x, y, 1 - c))
                for t in range(n)]

    def start(*a):
        for cp in copies(*a):
            cp.start()

    def finish(*a):
        for cp in copies(*a):
            cp.wait()

    out_shapes = [SDS((4, g.shape[1], g.shape[2] // 2, g.shape[3]), BF16) for g in grads]
    return _Comm(list(grads), out_shapes, n, start, finish)


def _grads_to_chips_plan(psums):
    n = len(psums)

    def copies(ins_r, outs_r, send, recv):
        x, y, c = _me()
        return [_rcopy(ins_r[t].at[2 * px + py], outs_r[t].at[j], send, recv, 3 * t + j, (px, py, c))
                for j, (px, py) in enumerate(_chip_peers(x, y)) for t in range(n)]

    def start(*a):
        for cp in copies(*a):
            cp.start()

    def finish(*a):
        for cp in copies(*a):
            cp.wait()

    return _Comm(list(psums), [SDS((3,) + p.shape[1:], BF16) for p in psums], 3 * n, start, finish)


def _swap_halves(sums):
    n = len(sums)

    def body(*refs):
        out_refs = refs[n:2 * n]
        send_sems, recv_sems = refs[2 * n:]
        x, y, c = _me()
        sib = (x, y, 1 - c)
        cps = [_rcopy(_row_half(out_refs[t], c), _row_half(out_refs[t], c), send_sems, recv_sems, t, sib)
               for t in range(n)]
        for cp in cps:
            cp.start()
        for t in range(n):
            other = _row_half(out_refs[t], 1 - c)
            _rcopy(other, other, send_sems, recv_sems, t, sib).wait_recv()
        for cp in cps:
            cp.wait_send()

    return pl.pallas_call(
        body, in_specs=[ANY] * n, out_specs=[ANY] * n, out_shape=[SDS(s.shape, F32) for s in sums],
        input_output_aliases={t: t for t in range(n)},
        scratch_shapes=[pltpu.SemaphoreType.DMA((n,)), pltpu.SemaphoreType.DMA((n,))],
        name="swap_halves")(*sums)


def _allgather_small(part):
    def body(p_ref, out_ref, send_sems, recv_sems, local_sem):
        x, y, c = _me()
        me = 4 * x + 2 * y + c
        own = pltpu.make_async_copy(p_ref, out_ref.at[me], local_sem.at[0])
        own.start()
        sends = []
        for k in range(1, 8):
            fx, fy, fc = (k >> 2) & 1, (k >> 1) & 1, k & 1
            to = (x ^ fx, y ^ fy, c ^ fc)
            sends.append(_rcopy(p_ref, out_ref.at[me], send_sems, recv_sems, k - 1, to))
        for cp in sends:
            cp.start()
        for k in range(1, 8):
            slot = out_ref.at[me ^ k]
            _rcopy(slot, slot, send_sems, recv_sems, k - 1, (x, y, c)).wait_recv()
        for cp in sends:
            cp.wait_send()
        own.wait()

    return pl.pallas_call(
        body, in_specs=[ANY], out_specs=ANY, out_shape=SDS((8,) + part.shape, F32),
        scratch_shapes=[pltpu.SemaphoreType.DMA((7,)), pltpu.SemaphoreType.DMA((7,)), pltpu.SemaphoreType.DMA((1,))],
        name="allgather_small")(part)


BIG = (("ssd_w_in", 2), ("ssd_w_out", 1), ("sc_w_in", 2), ("sc_w_out", 1), ("ffn_w_up", 2), ("ffn_w_down", 1))


def _to_shards(full, axis):
    A, B = full.shape
    if axis == 2:
        return full.reshape(A, 4, B // 4).transpose(1, 0, 2)
    return full.reshape(4, A // 4, B)


def _from_shards(shards, axis):
    _, a, b = shards.shape
    if axis == 2:
        return shards.transpose(1, 0, 2).reshape(a, 4 * b)
    return shards.reshape(4 * a, b)


def _interleave(w, parts):
    lead, n = w.shape[:-1], w.shape[-1]
    return w.reshape(*lead, parts, n // (parts * LANE), LANE).swapaxes(-2, -3).reshape(*lead, n)


def _deinterleave(w, parts):
    lead, n = w.shape[:-1], w.shape[-1]
    return w.reshape(*lead, n // (parts * LANE), parts, LANE).swapaxes(-2, -3).reshape(*lead, n)


def _pack_rows(vectors, width, row_multiple):
    flat = jnp.concatenate(vectors, axis=-1)
    n = flat.shape[-1]
    unit = width * row_multiple
    total = -(-n // unit) * unit
    flat = jnp.pad(flat, [(0, 0)] * (flat.ndim - 1) + [(0, total - n)])
    return flat.reshape(*flat.shape[:-1], total // width, width)


def _unpack(flat, shapes):
    out, off = [], 0
    for s in shapes:
        n = int(np.prod(s))
        out.append(flat[..., off:off + n].reshape(*flat.shape[:-1], *s))
        off += n
    return out


def _memo(fn):
    cache = {}

    def wrapped(k):
        if k not in cache:
            cache[k] = fn(k)
        return cache[k]

    return wrapped


def _row(v):
    return v.reshape(1, -1)


def _pad_rows(w, rows=8):
    return jnp.pad(w, ((0, rows - w.shape[0]), (0, 0)))


def _ffn_fwd(x, g_pre, g_post, w_up, cw, cb, w_down, tag, comm_up=None, comm_down=None):
    (up, hn), got_up = _ffn_up(x, g_pre, w_up, "ffn_up" + tag, comm_up)
    a = _ffn_mid_fwd(up, cw, cb, "ffn_mid_fwd" + tag)
    (f, x_new), got_down = _matmul_norm_res(a, w_down, x, g_post, "ffn_down" + tag, comm_down)
    return x_new, (x, hn, up, a, f), got_up, got_down


def _ffn_bwd(dx, saved, g_pre, g_post, w_up, cw, cb, w_down, tag):
    x, hn, up, a, f = saved
    df, dg_post = _postnorm_bwd(dx, f, g_post, "ffn_post_bwd" + tag)
    da = _matmul_nt(df, w_down, BF16, "ffn_down_dx" + tag)
    dw_down = _matmul_tn(a, df, w_down.shape[0] // 2, w_down.shape[1], "ffn_down_dw" + tag)
    dup, stats = _ffn_mid_bwd(da, up, cw, cb, "ffn_mid_bwd" + tag)
    dx_in, dg_pre = _ffn_up_dx(dup, w_up, x, g_pre, dx, "ffn_up_dx" + tag)
    dw_up = _ffn_up_dw(hn, dup, "ffn_up_dw" + tag)
    return dx_in, dict(g_pre=dg_pre, g_post=dg_post, w_up=dw_up, w_down=dw_down, cw=stats[0:3], cb=stats[3])


def _sc_fwd(x, g_pre, g_post, w_in, cw, w_out, tag):
    (bcv, hn), _ = _norm_matmul(x, g_pre, w_in, BF16, "sc_in" + tag)
    q = _sc_mid_fwd(bcv, cw, "sc_mid_fwd" + tag)
    (m, x_new), _ = _matmul_norm_res(q, w_out, x, g_post, "sc_out" + tag)
    return x_new, (x, hn, bcv, q, m)


def _sc_bwd(dx, saved, g_pre, g_post, w_in, cw, w_out, tag):
    x, hn, bcv, q, m = saved
    dm, dg_post = _postnorm_bwd(dx, m, g_post, "sc_post_bwd" + tag)
    dq = _matmul_nt(dm, w_out, BF16, "sc_out_dx" + tag)
    dw_out = _matmul_tn(q, dm, w_out.shape[0], w_out.shape[1], "sc_out_dw" + tag)
    dbcv, stats = _sc_mid_bwd(dq, bcv, cw, "sc_mid_bwd" + tag)
    (dx_in, dg_pre), _ = _matmul_nt_prenorm_bwd(dbcv, w_in, x, g_pre, dx, "sc_in_dx" + tag)
    dw_in = _matmul_tn(hn, dbcv, w_in.shape[0], w_in.shape[1] // 3, "sc_in_dw" + tag)
    return dx_in, dict(g_pre=dg_pre, g_post=dg_post, w_in=dw_in, w_out=dw_out, cw=stats[0:3])


def _ssd_fwd(x, g_pre, g_post, w_in, cw, cb, par, dexp, nw, w_out, consts, comm, tag, comm_in=None):
    d_inner = dexp.shape[1]
    (zx, hn), got_in = _norm_matmul(x, g_pre, w_in, F32, "ssd_in" + tag, comm_in)
    xbc = _ssd_conv_fwd(zx, cw, cb, d_inner, cw.shape[1], "ssd_conv_fwd" + tag)
    (yn, yf, states), got = _ssd_scan_fwd(zx, xbc, par, dexp, nw, consts, comm, "ssd_scan_fwd" + tag)
    if callable(w_out):
        w_out = w_out(got_in)
    (m, x_new), _ = _matmul_norm_res(yn, w_out, x, g_post, "ssd_out" + tag)
    return x_new, (x, hn, zx, xbc, yn, yf, states, m), got


def _ssd_bwd(dx, saved, g_pre, g_post, w_in, cw, cb, par, dexp, nw, w_out, consts, comm, tag, comm_dx=None):
    x, hn, zx, xbc, yn, yf, states, m = saved
    d_inner = dexp.shape[1]
    dm, dg_post = _postnorm_bwd(dx, m, g_post, "ssd_post_bwd" + tag)
    dyn = _matmul_nt(dm, w_out, F32, "ssd_out_dx" + tag)
    dw_out = _matmul_tn(yn, dm, w_out.shape[0] // 2, w_out.shape[1], "ssd_out_dw" + tag)
    (dzx, dxbc, dnw, dpar), got = _ssd_scan_bwd(dyn, yf, zx, xbc, states, par, dexp, nw, consts, comm(dw_out),
                                                "ssd_scan_bwd" + tag)
    dzx, stats = _ssd_conv_bwd(dxbc, zx, cw, cb, d_inner, dzx, "ssd_conv_bwd" + tag)
    dw_in = _matmul_tn(hn, dzx, w_in.shape[0], w_in.shape[1] // 7, "ssd_in_dw" + tag)
    (dx_in, dg_pre), got_dx = _matmul_nt_prenorm_bwd(dzx, w_in, x, g_pre, dx, "ssd_in_dx" + tag,
                                                     None if comm_dx is None else comm_dx(dw_in))
    n_heads = d_inner // HEAD_DIM
    grads = dict(g_pre=dg_pre, g_post=dg_post, w_in=dw_in, w_out=dw_out, cw=stats[0:4], cb=stats[4],
                 dt_bias=dpar[0, :n_heads], a_log=dpar[1, :n_heads], d=dpar[2, :n_heads], nw=dnw[0])
    return dx_in, grads, got, got_dx


def kernel(x, mix_pre_g, mix_post_g, ffn_pre_g, ffn_post_g, ssd_w_in, ssd_conv_w, ssd_conv_b, ssd_dt_bias, ssd_A_log, ssd_D, ssd_norm_w, ssd_w_out, sc_w_in, sc_conv_w, sc_w_out, ffn_w_up, ffn_conv_w, ffn_conv_b, ffn_w_down, loss_target, m_mix_pre_g, m_mix_post_g, m_ffn_pre_g, m_ffn_post_g, m_ssd_w_in, m_ssd_conv_w, m_ssd_conv_b, m_ssd_dt_bias, m_ssd_A_log, m_ssd_D, m_ssd_norm_w, m_ssd_w_out, m_sc_w_in, m_sc_conv_w, m_sc_w_out, m_ffn_w_up, m_ffn_conv_w, m_ffn_conv_b, m_ffn_w_down, v_mix_pre_g, v_mix_post_g, v_ffn_pre_g, v_ffn_post_g, v_ssd_w_in, v_ssd_conv_w, v_ssd_conv_b, v_ssd_dt_bias, v_ssd_A_log, v_ssd_D, v_ssd_norm_w, v_ssd_w_out, v_sc_w_in, v_sc_conv_w, v_sc_w_out, v_ffn_w_up, v_ffn_conv_w, v_ffn_conv_b, v_ffn_w_down):
    names = ["mix_pre_g", "mix_post_g", "ffn_pre_g", "ffn_post_g", "ssd_w_in", "ssd_conv_w", "ssd_conv_b",
             "ssd_dt_bias", "ssd_A_log", "ssd_D", "ssd_norm_w", "ssd_w_out", "sc_w_in", "sc_conv_w", "sc_w_out",
             "ffn_w_up", "ffn_conv_w", "ffn_conv_b", "ffn_w_down"]
    env = locals()
    wts = {n: env[n] for n in names}
    mom = {n: env["m_" + n] for n in names}
    var = {n: env["v_" + n] for n in names}

    depth, d_model = mix_pre_g.shape
    n_ssd, n_heads = ssd_dt_bias.shape
    n_sc = sc_conv_w.shape[0]
    d_inner = n_heads * HEAD_DIM
    conv_dim = d_inner + 2 * N_GROUPS * D_STATE
    ssd_in_dim = d_inner + conv_dim + n_heads
    ssd_in_pad = d_inner + conv_dim + LANE
    q_chip = 2 * lax.axis_index("x") + lax.axis_index("y")
    core = lax.axis_index("c")

    assert depth == 4 and n_ssd == 2 and n_sc == 2, "the exchange schedule is written for this trunk"
    ssd_items = lambda j: [("ssd_w_in", j), ("ssd_w_out", j)]
    sc_items = lambda j: [("sc_w_in", j), ("sc_w_out", j)]
    ffn_items = lambda i: [("ffn_w_up", i), ("ffn_w_down", i)]
    gather_first = [("ssd_w_in", 0)]
    gather_in_ssd_in = {0: [("ssd_w_out", 0)]}
    gather_in_scan = {0: ffn_items(0) + sc_items(0) + ffn_items(1), 2: ffn_items(2) + sc_items(1) + ffn_items(3)}
    gather_in_ffn = {0: ([("ssd_w_in", 1)], [("ssd_w_out", 1)])}
    reduce_in_scan = {2: ffn_items(3) + sc_items(1) + ffn_items(2),
                      0: ssd_items(1) + ffn_items(1) + sc_items(0) + ffn_items(0) + [("ssd_w_out", 0)]}
    reduce_in_dx = {0: [("ssd_w_in", 0)]}
    axis_of = dict(BIG)
    gathered = {}

    def gather_plan(items, small=None):
        mine = [wts[n][layer:layer + 1].astype(BF16) for n, layer in items]
        return _allgather_plan(mine, small) if items else None

    def gather_done(items, results):
        for item, buf in zip(items, results):
            gathered[item] = buf[:, 0]

    def full(n, layer):
        return _from_shards(gathered[(n, layer)], axis_of[n])

    conv_names = ["ssd_conv_w", "sc_conv_w", "ffn_conv_w"]
    conv_shapes = [wts[n].shape for n in conv_names]
    small_mine = _pack_rows([wts[n].reshape(-1) for n in conv_names], LANE, 8)
    *results, small_all = _run_comm(gather_plan(gather_first, small_mine), "allgather_first")
    gather_done(gather_first, results)
    conv_full = {}
    for n, f, s in zip(conv_names, _unpack(small_all.reshape(4, -1), conv_shapes), conv_shapes):
        conv_full[n] = f.transpose(1, 2, 0, 3).reshape(s[0], s[1], 4 * s[2])

    consts = _scan_constants(n_heads)

    def ssd_args(j):
        par = jnp.zeros((8, LANE), F32).at[0, :n_heads].set(ssd_dt_bias[j]).at[1, :n_heads].set(ssd_A_log[j])
        dexp = jnp.repeat(ssd_D[j], HEAD_DIM).reshape(1, d_inner)
        w_in = jnp.pad(full("ssd_w_in", j), ((0, 0), (0, ssd_in_pad - ssd_in_dim)))
        return (w_in, _pad_rows(conv_full["ssd_conv_w"][j]), _row(ssd_conv_b[j]), par, dexp, _row(ssd_norm_w[j]))

    def sc_args(j):
        return (_interleave(full("sc_w_in", j), 3), _pad_rows(conv_full["sc_conv_w"][j]), full("sc_w_out", j))

    def ffn_args(i):
        return (gathered[("ffn_w_up", i)], _pad_rows(conv_full["ffn_conv_w"][i]), _row(ffn_conv_b[i]),
                full("ffn_w_down", i))

    ssd_args, sc_args, ffn_args = _memo(ssd_args), _memo(sc_args), _memo(ffn_args)

    h = x[0]
    saved = []
    for i in range(depth):
        j = i // 2
        gp, gq = _row(mix_pre_g[i]), _row(mix_post_g[i])
        if i % 2 == 0:
            items_in = gather_in_ssd_in.get(i, [])

            def w_out_when_here(got_in, items_in=items_in, j=j):
                gather_done(items_in, got_in)
                return full("ssd_w_out", j)

            h, sv, results = _ssd_fwd(h, gp, gq, *ssd_args(j), w_out_when_here, consts, gather_plan(gather_in_scan[i]),
                                      tag="", comm_in=gather_plan(items_in))
            gather_done(gather_in_scan[i], results)
        else:
            h, sv = _sc_fwd(h, gp, gq, *sc_args(j), tag="")
        items_up, items_down = gather_in_ffn.get(i, ([], []))
        h, sv2, got_up, got_down = _ffn_fwd(h, _row(ffn_pre_g[i]), _row(ffn_post_g[i]), *ffn_args(i), tag="",
                                            comm_up=gather_plan(items_up), comm_down=gather_plan(items_down))
        gather_done(items_up, got_up)
        gather_done(items_down, got_down)
        saved.append((sv, sv2))
    dh, loss_part = _loss_head(h, loss_target[0], "loss_head")

    mix_grads, ffn_grads = [None] * depth, [None] * depth
    where = jnp.stack([q_chip, core]).astype(jnp.int32)

    def shard_grad(n, layer):
        if n == "ffn_w_up":
            g = ffn_grads[layer]["w_up"]
        elif n == "ffn_w_down":
            g = _to_shards(ffn_grads[layer]["w_down"], 1)
        elif n == "ssd_w_in":
            g = _to_shards(early[(n, layer)][:, :ssd_in_dim], 2)
        elif n == "ssd_w_out":
            g = _to_shards(early[(n, layer)], 1)
        elif n == "sc_w_in":
            g = _to_shards(_deinterleave(mix_grads[2 * layer + 1]["w_in"], 3), 2)
        else:
            g = _to_shards(mix_grads[2 * layer + 1]["w_out"], 1)
        return g[:, None]

    def reduce_begin(items, tag):
        by_shard = [shard_grad(n, layer) for n, layer in items]
        from_sib = _run_comm(_grads_to_sibling_plan(by_shard), "grads_to_sibling" + tag)
        chip_sums = [_add_core_halves(where, g, r, "add_core_halves_%s%d" % item)
                     for item, g, r in zip(items, by_shard, from_sib)]
        return by_shard, from_sib, _grads_to_chips_plan(chip_sums)

    sums = {}
    early = {}

    def riding(items, tag):
        by_shard, from_sib, plan = reduce_begin(items, tag)

        def arrived(from_chips):
            for (n, layer), g, r, rr in zip(items, by_shard, from_sib, from_chips):
                sums[n] = _sum_shard(where, g, r, rr, sums.get(n), layer, wts[n].shape[0],
                                     "sum_shard_%s%d" % (n, layer))

        return plan, arrived

    for i in reversed(range(depth)):
        j = i // 2
        sv, sv2 = saved[i]
        dh, ffn_grads[i] = _ffn_bwd(dh, sv2, _row(ffn_pre_g[i]), _row(ffn_post_g[i]), *ffn_args(i), tag="")
        gp, gq = _row(mix_pre_g[i]), _row(mix_post_g[i])
        if i % 2 == 0:
            then = {}

            def in_scan(dw_out, i=i, j=j, then=then):
                early[("ssd_w_out", j)] = dw_out
                plan, then["scan"] = riding(reduce_in_scan[i], "_%d" % i)
                return plan

            def in_dx(dw_in, i=i, j=j, then=then):
                early[("ssd_w_in", j)] = dw_in
                plan, then["dx"] = riding(reduce_in_dx[i], "_dx%d" % i) if i in reduce_in_dx else (None, None)
                return plan

            dh, mix_grads[i], got, got_dx = _ssd_bwd(dh, sv, gp, gq, *ssd_args(j), full("ssd_w_out", j), consts,
                                                     in_scan, tag="", comm_dx=in_dx)
            then["scan"](got)
            if then["dx"] is not None:
                then["dx"](got_dx)
        else:
            w_in, scw, w_out = sc_args(j)
            dh, mix_grads[i] = _sc_bwd(dh, sv, gp, gq, w_in, scw, w_out, tag="")
    grad_x = dh[None]
    big_grads = dict(zip([n for n, _ in BIG], _swap_halves([sums[n] for n, _ in BIG])))
    ssd_l = [mix_grads[i] for i in range(0, depth, 2)]
    sc_l = [mix_grads[i] for i in range(1, depth, 2)]
    stack = lambda layers, k: jnp.stack([g[k] for g in layers])

    small_names = ["mix_pre_g", "mix_post_g", "ffn_pre_g", "ffn_post_g", "ssd_conv_w", "ssd_conv_b", "ssd_dt_bias",
                   "ssd_A_log", "ssd_D", "ssd_norm_w", "sc_conv_w", "ffn_conv_w", "ffn_conv_b"]
    small_local = {
        "mix_pre_g": jnp.concatenate([g["g_pre"] for g in mix_grads]),
        "mix_post_g": jnp.concatenate([g["g_post"] for g in mix_grads]),
        "ffn_pre_g": jnp.concatenate([g["g_pre"] for g in ffn_grads]),
        "ffn_post_g": jnp.concatenate([g["g_post"] for g in ffn_grads]),
        "ssd_conv_w": stack(ssd_l, "cw"), "ssd_conv_b": stack(ssd_l, "cb"), "ssd_dt_bias": stack(ssd_l, "dt_bias"),
        "ssd_A_log": stack(ssd_l, "a_log"), "ssd_D": stack(ssd_l, "d"), "ssd_norm_w": stack(ssd_l, "nw"),
        "sc_conv_w": stack(sc_l, "cw"), "ffn_conv_w": stack(ffn_grads, "cw"), "ffn_conv_b": stack(ffn_grads, "cb"),
    }
    small_full_shapes = [small_local[n].shape for n in small_names]
    spack = _pack_rows([small_local[n].reshape(-1) for n in small_names] + [loss_part.reshape(-1)], LANE, 8)
    stotal = _sum_slots(_allgather_small(spack), "sum_small").reshape(-1)
    small_grads = dict(zip(small_names, _unpack(stotal, small_full_shapes)))
    loss = stotal[sum(int(np.prod(s)) for s in small_full_shapes)]
    for n in conv_names:
        width = wts[n].shape[-1]
        small_grads[n] = lax.dynamic_slice_in_dim(small_grads[n], q_chip * width, width, axis=2)

    grads, delta, new_m, new_v = {}, {}, {}, {}
    for n, _ in BIG:
        s = wts[n].shape
        two_d = lambda a: a.reshape(-1, s[-1])
        grads[n] = big_grads[n]
        d, mn, vn = _adamw(two_d(wts[n]), two_d(mom[n]), two_d(var[n]), two_d(grads[n]), "adamw_" + n)
        delta[n], new_m[n], new_v[n] = d.reshape(s), mn.reshape(s), vn.reshape(s)
    small_shapes = [wts[n].shape for n in small_names]
    pk = lambda d: _pack_rows([d[n].reshape(-1) for n in small_names], LANE, 8)
    for n in small_names:
        grads[n] = small_grads[n].reshape(wts[n].shape)
    d, mn, vn = _adamw(pk(wts), pk(mom), pk(var), pk(grads), "adamw_small")
    for out, packed in ((delta, d), (new_m, mn), (new_v, vn)):
        out.update(zip(small_names, _unpack(packed.reshape(-1), small_shapes)))

    return (loss, grad_x, *[grads[n] for n in names], *[delta[n] for n in names], *[new_m[n] for n in names],
            *[new_v[n] for n in names])
```

```python
from typing import Callable, NamedTuple

import jax
import jax.numpy as jnp
import numpy as np
from jax import lax
from jax.experimental import pallas as pl
from jax.experimental.pallas import tpu as pltpu

F32 = jnp.float32
BF16 = jnp.bfloat16
SDS = jax.ShapeDtypeStruct
MESH = pl.DeviceIdType.MESH
ANY = pl.BlockSpec(memory_space=pl.ANY)

EPS = 1e-6
CHUNK = 64
HEAD_DIM = 64
N_GROUPS = 8
D_STATE = 128
HEADS_PER_GROUP = 4
GROUP_W = HEADS_PER_GROUP * HEAD_DIM
LANE = 128
ROW_TILE = 128
HALO = 8
VMEM_LIMIT = 56 * 1024 * 1024

ADAM_LR = 0.001
ADAM_B1 = 0.9
ADAM_B2 = 0.999
ADAM_EPS = 1e-08
ADAM_WD = 0.01
ADAM_STEP = 10

NT = (((1,), (1,)), ((), ()))
TN = (((0,), (0,)), ((), ()))


def _cp(*sem):
    return pltpu.CompilerParams(dimension_semantics=sem or None, vmem_limit_bytes=VMEM_LIMIT)


def _sigmoid(x):
    return 1.0 / (1.0 + jnp.exp(-x))


def _dsilu(x, s):
    return s * (1.0 + x * (1.0 - s))


def _rsq(x):
    return lax.rsqrt(jnp.mean(x * x, axis=-1, keepdims=True) + EPS)


MM_ROWS = 256


def _norm_matmul(x, g, w, out_dtype, name, comm=None):
    L, D = x.shape
    N = w.shape[1]
    tm = min(MM_ROWS, L)

    def body(x_ref, g_ref, w_ref, o_ref, hn_ref):
        xv = x_ref[...]
        hn = (xv * _rsq(xv) * g_ref[...]).astype(BF16)
        hn_ref[...] = hn
        o_ref[...] = jnp.dot(hn, w_ref[...], preferred_element_type=F32).astype(out_dtype)

    row = lambda i: (i, 0)
    fix = lambda i: (0, 0)
    return _carrier_call(
        body, comm, L // tm,
        in_specs=[pl.BlockSpec((tm, D), row), pl.BlockSpec((1, D), fix), pl.BlockSpec((D, N), fix)],
        out_specs=[pl.BlockSpec((tm, N), row), pl.BlockSpec((tm, D), row)],
        out_shape=[SDS((L, N), out_dtype), SDS((L, D), BF16)], scratch_shapes=[], name=name, args=(x, g, w))


def _matmul_norm_res(a, w, x, g, name, comm=None):
    L, K = a.shape
    D = w.shape[1]
    tm = min(MM_ROWS, L)

    def body(a_ref, w_ref, x_ref, g_ref, m_ref, xo_ref):
        m = jnp.dot(a_ref[...], w_ref[...], preferred_element_type=F32)
        m_ref[...] = m
        xo_ref[...] = x_ref[...] + m * _rsq(m) * g_ref[...]

    row = lambda i: (i, 0)
    fix = lambda i: (0, 0)
    return _carrier_call(
        body, comm, L // tm,
        in_specs=[pl.BlockSpec((tm, K), row), pl.BlockSpec((K, D), fix), pl.BlockSpec((tm, D), row),
                  pl.BlockSpec((1, D), fix)],
        out_specs=[pl.BlockSpec((tm, D), row), pl.BlockSpec((tm, D), row)],
        out_shape=[SDS((L, D), F32), SDS((L, D), F32)], scratch_shapes=[], name=name, args=(a, w, x, g))


def _postnorm_bwd(dx, m, g, name):
    L, D = dx.shape
    tm = min(512, L)

    def body(dx_ref, m_ref, g_ref, dm_ref, dg_ref):
        @pl.when(pl.program_id(0) == 0)
        def _():
            dg_ref[...] = jnp.zeros_like(dg_ref)

        m = m_ref[...]
        dxv = dx_ref[...]
        r = _rsq(m)
        mh = m * r
        dg_ref[...] += jnp.sum(dxv * mh, axis=0, keepdims=True)
        dyg = dxv * g_ref[...]
        dm_ref[...] = (r * (dyg - mh * jnp.mean(dyg * mh, axis=-1, keepdims=True))).astype(BF16)

    row = lambda i: (i, 0)
    fix = lambda i: (0, 0)
    return pl.pallas_call(
        body, grid=(L // tm,),
        in_specs=[pl.BlockSpec((tm, D), row), pl.BlockSpec((tm, D), row), pl.BlockSpec((1, D), fix)],
        out_specs=[pl.BlockSpec((tm, D), row), pl.BlockSpec((1, D), fix)],
        out_shape=[SDS((L, D), BF16), SDS((1, D), F32)],
        name=name, compiler_params=_cp("arbitrary"))(dx, m, g)


def _matmul_nt(a, w, out_dtype, name):
    L, D = a.shape
    K = w.shape[0]
    tm = min(MM_ROWS, L)

    def body(a_ref, w_ref, o_ref):
        o_ref[...] = lax.dot_general(a_ref[...], w_ref[...], NT, preferred_element_type=F32).astype(out_dtype)

    return pl.pallas_call(
        body, grid=(L // tm,),
        in_specs=[pl.BlockSpec((tm, D), lambda i: (i, 0)), pl.BlockSpec((K, D), lambda i: (0, 0))],
        out_specs=pl.BlockSpec((tm, K), lambda i: (i, 0)),
        out_shape=SDS((L, K), out_dtype),
        name=name, compiler_params=_cp("parallel"))(a, w)


def _matmul_tn(a, b, ta, tn, name):
    L, Ka = a.shape
    N = b.shape[1]
    tl = min(512, L)
    n_l = L // tl

    def body(a_ref, b_ref, o_ref, acc_ref):
        l = pl.program_id(2)

        @pl.when(l == 0)
        def _():
            acc_ref[...] = jnp.zeros_like(acc_ref)

        acc_ref[...] += lax.dot_general(a_ref[...], b_ref[...], TN, preferred_element_type=F32)

        @pl.when(l == n_l - 1)
        def _():
            o_ref[...] = acc_ref[...].astype(BF16)

    return pl.pallas_call(
        body, grid=(Ka // ta, N // tn, n_l),
        in_specs=[pl.BlockSpec((tl, ta), lambda i, j, l: (l, i)), pl.BlockSpec((tl, tn), lambda i, j, l: (l, j))],
        out_specs=pl.BlockSpec((ta, tn), lambda i, j, l: (i, j)),
        out_shape=SDS((Ka, N), BF16),
        scratch_shapes=[pltpu.VMEM((ta, tn), F32)],
        name=name, compiler_params=_cp("parallel", "parallel", "arbitrary"))(a, b)


def _ffn_up(x, g, w4, name, comm=None):
    L, D = x.shape
    b = w4.shape[2]
    tm = min(MM_ROWS, L)

    def body(x_ref, g_ref, w_ref, o_ref, hn_ref):
        xv = x_ref[...]
        hn = (xv * _rsq(xv) * g_ref[...]).astype(BF16)
        hn_ref[...] = hn
        for q in range(4):
            o_ref[q // 2, :, (q % 2) * b:(q % 2 + 1) * b] = jnp.dot(hn, w_ref[q], preferred_element_type=F32).astype(BF16)

    return _carrier_call(
        body, comm, L // tm,
        in_specs=[pl.BlockSpec((tm, D), lambda i: (i, 0)), pl.BlockSpec((1, D), lambda i: (0, 0)),
                  pl.BlockSpec((4, D, b), lambda i: (0, 0, 0))],
        out_specs=[pl.BlockSpec((2, tm, 2 * b), lambda i: (0, i, 0)), pl.BlockSpec((tm, D), lambda i: (i, 0))],
        out_shape=[SDS((2, L, 2 * b), BF16), SDS((L, D), BF16)], scratch_shapes=[], name=name, args=(x, g, w4))


def _ffn_up_dx(dup, w4, x, g, dres, name):
    _, L, _ = dup.shape
    _, D, b = w4.shape
    tm = min(MM_ROWS, L)

    def body(dy_ref, w_ref, x_ref, g_ref, dres_ref, dx_ref, dg_ref):
        @pl.when(pl.program_id(0) == 0)
        def _():
            dg_ref[...] = jnp.zeros_like(dg_ref)

        dh = jnp.zeros((tm, D), F32)
        for q in range(4):
            dh = dh + lax.dot_general(dy_ref[q // 2, :, (q % 2) * b:(q % 2 + 1) * b], w_ref[q], NT,
                                      preferred_element_type=F32)
        xv = x_ref[...]
        r = _rsq(xv)
        xh = xv * r
        dg_ref[...] += jnp.sum(dh * xh, axis=0, keepdims=True)
        dyg = dh * g_ref[...]
        dx_ref[...] = dres_ref[...] + r * (dyg - xh * jnp.mean(dyg * xh, axis=-1, keepdims=True))

    row = lambda i: (i, 0)
    fix = lambda i: (0, 0)
    return pl.pallas_call(
        body, grid=(L // tm,),
        in_specs=[pl.BlockSpec((2, tm, 2 * b), lambda i: (0, i, 0)), pl.BlockSpec((4, D, b), lambda i: (0, 0, 0)),
                  pl.BlockSpec((tm, D), row), pl.BlockSpec((1, D), fix), pl.BlockSpec((tm, D), row)],
        out_specs=[pl.BlockSpec((tm, D), row), pl.BlockSpec((1, D), fix)],
        out_shape=[SDS((L, D), F32), SDS((1, D), F32)],
        name=name, compiler_params=_cp("arbitrary"))(dup, w4, x, g, dres)


def _ffn_up_dw(hn, dup, name):
    L, D = hn.shape
    b = dup.shape[2] // 2
    tl = min(512, L)
    n_l = L // tl

    def body(a_ref, b_ref, o_ref, acc_ref):
        l = pl.program_id(1)

        @pl.when(l == 0)
        def _():
            acc_ref[...] = jnp.zeros_like(acc_ref)

        acc_ref[...] += lax.dot_general(a_ref[...], b_ref[...], TN, preferred_element_type=F32)

        @pl.when(l == n_l - 1)
        def _():
            o_ref[...] = acc_ref[...].astype(BF16)

    return pl.pallas_call(
        body, grid=(4, n_l),
        in_specs=[pl.BlockSpec((tl, D), lambda q, l: (l, 0)),
                  pl.BlockSpec((None, tl, b), lambda q, l: (q // 2, l, q % 2))],
        out_specs=pl.BlockSpec((None, D, b), lambda q, l: (q, 0, 0)),
        out_shape=SDS((4, D, b), BF16),
        scratch_shapes=[pltpu.VMEM((D, b), F32)],
        name=name, compiler_params=_cp("parallel", "arbitrary"))(hn, dup)


def _matmul_nt_prenorm_bwd(dy, w, x, g, dres, name, comm=None):
    L, N = dy.shape
    D = w.shape[0]
    tm = min(MM_ROWS, L)

    def body(dy_ref, w_ref, x_ref, g_ref, dres_ref, dx_ref, dg_ref):
        @pl.when(pl.program_id(0) == 0)
        def _():
            dg_ref[...] = jnp.zeros_like(dg_ref)

        dh = lax.dot_general(dy_ref[...], w_ref[...], NT, preferred_element_type=F32)
        xv = x_ref[...]
        r = _rsq(xv)
        xh = xv * r
        dg_ref[...] += jnp.sum(dh * xh, axis=0, keepdims=True)
        dyg = dh * g_ref[...]
        dx_ref[...] = dres_ref[...] + r * (dyg - xh * jnp.mean(dyg * xh, axis=-1, keepdims=True))

    row = lambda i: (i, 0)
    fix = lambda i: (0, 0)
    return _carrier_call(
        body, comm, L // tm,
        in_specs=[pl.BlockSpec((tm, N), row), pl.BlockSpec((D, N), fix), pl.BlockSpec((tm, D), row),
                  pl.BlockSpec((1, D), fix), pl.BlockSpec((tm, D), row)],
        out_specs=[pl.BlockSpec((tm, D), row), pl.BlockSpec((1, D), fix)],
        out_shape=[SDS((L, D), F32), SDS((1, D), F32)], scratch_shapes=[], name=name, args=(dy, w, x, g, dres))


def _loss_head(y, t, name):
    L, D = y.shape
    tm = min(512, L)

    def body(y_ref, t_ref, dy_ref, loss_ref):
        @pl.when(pl.program_id(0) == 0)
        def _():
            loss_ref[...] = jnp.zeros_like(loss_ref)

        e = y_ref[...] - t_ref[...]
        dy_ref[...] = e * (1.0 / D)
        s = jnp.sum(jnp.sum(e * e, axis=1, keepdims=True), axis=0, keepdims=True)
        loss_ref[...] += s * (0.5 / D)

    row = lambda i: (i, 0)
    return pl.pallas_call(
        body, grid=(L // tm,),
        in_specs=[pl.BlockSpec((tm, D), row), pl.BlockSpec((tm, D), row)],
        out_specs=[pl.BlockSpec((tm, D), row), pl.BlockSpec((1, 1), lambda i: (0, 0))],
        out_shape=[SDS((L, D), F32), SDS((1, 1), F32)],
        name=name, compiler_params=_cp("arbitrary"))(y, t)


def _tile_rows(ref):
    return HALO * (4 // jnp.dtype(ref.dtype).itemsize)


def _prev_rows(ref, r0, i, cols):
    n = _tile_rows(ref)
    p0 = pl.multiple_of(jnp.maximum(r0 - n, 0), n)
    return jnp.where(i > 0, ref[pl.ds(p0, n), cols].astype(F32)[n - HALO:], 0.0)


def _next_rows(ref, r0, i, n_tiles, cols):
    n = _tile_rows(ref)
    n0 = pl.multiple_of(jnp.minimum(r0 + ROW_TILE, n_tiles * ROW_TILE - n), n)
    return jnp.where(i < n_tiles - 1, ref[pl.ds(n0, n), cols].astype(F32)[:HALO], 0.0)


def _rows_f32(ref, rows, cols):
    return ref[rows, cols].astype(F32)


def _back(ext, s):
    return pltpu.roll(ext, s, axis=0)[HALO:HALO + ROW_TILE]


def _fwd(ext, s):
    n = ext.shape[0]
    return pltpu.roll(ext, n - s, axis=0)[:ROW_TILE]


def _store_rows(ref, rows):
    ref[...] = jnp.zeros_like(ref)
    for k, v in enumerate(rows):
        ref[k:k + 1, :] = v


def _strip_call(body, L, n_strips, ins, outs, name):
    def spec(rows, width, off):
        if off is None:
            return pl.BlockSpec((rows, width), lambda j: (0, 0))
        return pl.BlockSpec((rows, width), lambda j: (0, j + off))

    return pl.pallas_call(
        body, grid=(n_strips,),
        in_specs=[spec(a.shape[0], w, off) for a, w, off in ins],
        out_specs=[spec(s.shape[0], w, off) for s, w, off in outs],
        out_shape=[s for s, _, _ in outs],
        name=name, compiler_params=_cp("parallel"))(*[a for a, _, _ in ins])


def _ffn_mid_fwd(up, cw, cb, name):
    _, L, C = up.shape
    n_tiles = L // ROW_TILE
    al = slice(None)

    def body(up_ref, cw_ref, cb_ref, a_ref):
        w0, w1, w2 = cw_ref[0:1, :], cw_ref[1:2, :], cw_ref[2:3, :]
        b = cb_ref[...]
        gate_ref, val_ref = up_ref.at[0], up_ref.at[1]

        def step(i, c):
            r0 = pl.multiple_of(i * ROW_TILE, ROW_TILE)
            rows = pl.ds(r0, ROW_TILE)
            gp = _rows_f32(gate_ref, rows, al)
            ext = jnp.concatenate([_prev_rows(gate_ref, r0, i, al), gp], axis=0)
            gate = gp * w2 + _back(ext, 1) * w1 + _back(ext, 2) * w0 + b
            a_ref[rows, :] = (gate * _sigmoid(gate) * _rows_f32(val_ref, rows, al)).astype(BF16)
            return c

        lax.fori_loop(0, n_tiles, step, 0)

    strip = lambda rows: pl.BlockSpec((rows, LANE), lambda j: (0, j))
    return pl.pallas_call(
        body, grid=(C // LANE,),
        in_specs=[pl.BlockSpec((2, L, LANE), lambda j: (0, 0, j)), strip(cw.shape[0]), strip(1)],
        out_specs=strip(L), out_shape=SDS((L, C), BF16), name=name, compiler_params=_cp("parallel"))(up, cw, cb)


def _ffn_mid_bwd(da, up, cw, cb, name):
    L, C = da.shape
    n_tiles = L // ROW_TILE
    al = slice(None)

    def body(da_ref, up_ref, cw_ref, cb_ref, dup_ref, st_ref):
        w0, w1, w2 = cw_ref[0:1, :], cw_ref[1:2, :], cw_ref[2:3, :]
        b = cb_ref[...]
        gate_ref, val_ref = up_ref.at[0], up_ref.at[1]

        def step(i, c):
            r0 = pl.multiple_of(i * ROW_TILE, ROW_TILE)
            rows = pl.ds(r0, ROW_TILE)
            gp = _rows_f32(gate_ref, rows, al)
            gpe = jnp.concatenate([_prev_rows(gate_ref, r0, i, al), gp, _next_rows(gate_ref, r0, i, n_tiles, al)],
                                  axis=0)
            g1, g2 = pltpu.roll(gpe, 1, axis=0), pltpu.roll(gpe, 2, axis=0)
            gate = (gpe * w2 + g1 * w1 + g2 * w0 + b)[HALO:]
            sg = _sigmoid(gate)
            da_e = jnp.concatenate([_rows_f32(da_ref, rows, al), _next_rows(da_ref, r0, i, n_tiles, al)], axis=0)
            val_e = jnp.concatenate([_rows_f32(val_ref, rows, al), _next_rows(val_ref, r0, i, n_tiles, al)], axis=0)
            dgate_e = da_e * val_e * _dsilu(gate, sg)
            dgate = dgate_e[:ROW_TILE]
            dgp = dgate * w2 + _fwd(dgate_e, 1) * w1 + _fwd(dgate_e, 2) * w0
            dup_ref[0, rows, :] = dgp.astype(BF16)
            dup_ref[1, rows, :] = (da_e * gate * sg)[:ROW_TILE].astype(BF16)
            s = lambda v: jnp.sum(v, axis=0, keepdims=True)
            t = slice(HALO, HALO + ROW_TILE)
            return (c[0] + s(dgate * g2[t]), c[1] + s(dgate * g1[t]), c[2] + s(dgate * gp), c[3] + s(dgate))

        z = jnp.zeros((1, LANE), F32)
        _store_rows(st_ref, lax.fori_loop(0, n_tiles, step, (z, z, z, z)))

    strip = lambda rows: pl.BlockSpec((rows, LANE), lambda j: (0, j))
    pair = pl.BlockSpec((2, L, LANE), lambda j: (0, 0, j))
    return pl.pallas_call(
        body, grid=(C // LANE,), in_specs=[strip(L), pair, strip(cw.shape[0]), strip(1)],
        out_specs=[pair, strip(8)], out_shape=[SDS((2, L, C), BF16), SDS((8, C), F32)],
        name=name, compiler_params=_cp("parallel"))(da, up, cw, cb)


def _sc_mid_fwd(bcv, cw, name):
    L = bcv.shape[0]
    C = bcv.shape[1] // 3
    n_tiles = L // ROW_TILE
    s0, s1, s2 = slice(0, LANE), slice(LANE, 2 * LANE), slice(2 * LANE, 3 * LANE)

    def body(x_ref, cw_ref, q_ref):
        w0, w1, w2 = cw_ref[0:1, :], cw_ref[1:2, :], cw_ref[2:3, :]

        def step(i, c):
            r0 = pl.multiple_of(i * ROW_TILE, ROW_TILE)
            rows = pl.ds(r0, ROW_TILE)
            p = _rows_f32(x_ref, rows, s1) * _rows_f32(x_ref, rows, s2)
            ext = jnp.concatenate([_prev_rows(x_ref, r0, i, s1) * _prev_rows(x_ref, r0, i, s2), p], axis=0)
            u = p * w2 + _back(ext, 1) * w1 + _back(ext, 2) * w0
            q_ref[rows, :] = (_rows_f32(x_ref, rows, s0) * u).astype(BF16)
            return c

        lax.fori_loop(0, n_tiles, step, 0)

    return _strip_call(body, L, C // LANE, [(bcv, 3 * LANE, 0), (cw, LANE, 0)],
                       [(SDS((L, C), BF16), LANE, 0)], name)[0]


def _sc_mid_bwd(dq, bcv, cw, name):
    L, C = dq.shape
    n_tiles = L // ROW_TILE
    s0, s1, s2, al = slice(0, LANE), slice(LANE, 2 * LANE), slice(2 * LANE, 3 * LANE), slice(None)

    def body(dq_ref, x_ref, cw_ref, dx_ref, st_ref):
        w0, w1, w2 = cw_ref[0:1, :], cw_ref[1:2, :], cw_ref[2:3, :]

        def step(i, c):
            r0 = pl.multiple_of(i * ROW_TILE, ROW_TILE)
            rows = pl.ds(r0, ROW_TILE)
            gb, gc, v = _rows_f32(x_ref, rows, s0), _rows_f32(x_ref, rows, s1), _rows_f32(x_ref, rows, s2)
            dq_v = _rows_f32(dq_ref, rows, al)
            p = gc * v
            pext = jnp.concatenate([_prev_rows(x_ref, r0, i, s1) * _prev_rows(x_ref, r0, i, s2), p], axis=0)
            p1, p2 = _back(pext, 1), _back(pext, 2)
            u = p * w2 + p1 * w1 + p2 * w0
            du = dq_v * gb
            du_n = _next_rows(dq_ref, r0, i, n_tiles, al) * _next_rows(x_ref, r0, i, n_tiles, s0)
            ext = jnp.concatenate([du, du_n], axis=0)
            dp = du * w2 + _fwd(ext, 1) * w1 + _fwd(ext, 2) * w0
            dx_ref[rows, s0] = (dq_v * u).astype(BF16)
            dx_ref[rows, s1] = (dp * v).astype(BF16)
            dx_ref[rows, s2] = (dp * gc).astype(BF16)
            s = lambda t: jnp.sum(t, axis=0, keepdims=True)
            return (c[0] + s(du * p2), c[1] + s(du * p1), c[2] + s(du * p))

        z = jnp.zeros((1, LANE), F32)
        _store_rows(st_ref, lax.fori_loop(0, n_tiles, step, (z, z, z)))

    return _strip_call(body, L, C // LANE, [(dq, LANE, 0), (bcv, 3 * LANE, 0), (cw, LANE, 0)],
                       [(SDS((L, 3 * C), BF16), 3 * LANE, 0), (SDS((8, C), F32), LANE, 0)], name)


def _ssd_conv_fwd(zx, cw, cb, col0, C, name):
    L = zx.shape[0]
    n_tiles = L // ROW_TILE
    al = slice(None)

    def body(x_ref, cw_ref, cb_ref, o_ref):
        w0, w1, w2, w3 = cw_ref[0:1, :], cw_ref[1:2, :], cw_ref[2:3, :], cw_ref[3:4, :]
        b = cb_ref[...]

        def step(i, c):
            r0 = pl.multiple_of(i * ROW_TILE, ROW_TILE)
            rows = pl.ds(r0, ROW_TILE)
            xv = x_ref[rows, :]
            ext = jnp.concatenate([_prev_rows(x_ref, r0, i, al), xv], axis=0)
            cv = xv * w3 + _back(ext, 1) * w2 + _back(ext, 2) * w1 + _back(ext, 3) * w0 + b
            o_ref[rows, :] = cv * _sigmoid(cv)
            return c

        lax.fori_loop(0, n_tiles, step, 0)

    return _strip_call(body, L, C // LANE, [(zx, LANE, col0 // LANE), (cw, LANE, 0), (cb, LANE, 0)],
                       [(SDS((L, C), F32), LANE, 0)], name)[0]


def _ssd_conv_bwd(dxbc, zx, cw, cb, col0, dzx, name):
    L, C = dxbc.shape
    n_tiles = L // ROW_TILE
    al = slice(None)

    def body(d_ref, x_ref, cw_ref, cb_ref, dzx_in_ref, o_ref, st_ref):
        w0, w1, w2, w3 = cw_ref[0:1, :], cw_ref[1:2, :], cw_ref[2:3, :], cw_ref[3:4, :]
        b = cb_ref[...]

        def step(i, c):
            r0 = pl.multiple_of(i * ROW_TILE, ROW_TILE)
            rows = pl.ds(r0, ROW_TILE)
            xv = x_ref[rows, :]
            xe = jnp.concatenate([_prev_rows(x_ref, r0, i, al), xv, _next_rows(x_ref, r0, i, n_tiles, al)], axis=0)
            x1, x2, x3 = pltpu.roll(xe, 1, axis=0), pltpu.roll(xe, 2, axis=0), pltpu.roll(xe, 3, axis=0)
            cv = (xe * w3 + x1 * w2 + x2 * w1 + x3 * w0 + b)[HALO:]
            de = jnp.concatenate([d_ref[rows, :], _next_rows(d_ref, r0, i, n_tiles, al)], axis=0)
            dc_ext = de * _dsilu(cv, _sigmoid(cv))
            dc = dc_ext[:ROW_TILE]
            o_ref[rows, :] = (dc * w3 + _fwd(dc_ext, 1) * w2 + _fwd(dc_ext, 2) * w1 + _fwd(dc_ext, 3) * w0).astype(BF16)
            s = lambda t: jnp.sum(t, axis=0, keepdims=True)
            t = slice(HALO, HALO + ROW_TILE)
            return (c[0] + s(dc * x3[t]), c[1] + s(dc * x2[t]), c[2] + s(dc * x1[t]), c[3] + s(dc * xv), c[4] + s(dc))

        z = jnp.zeros((1, LANE), F32)
        _store_rows(st_ref, lax.fori_loop(0, n_tiles, step, (z, z, z, z, z)))

    strip = lambda rows, off=0: pl.BlockSpec((rows, LANE), lambda j: (0, j + off))
    shifted = strip(L, col0 // LANE)
    return pl.pallas_call(
        body, grid=(C // LANE,), in_specs=[strip(L), shifted, strip(cw.shape[0]), strip(1), ANY],
        out_specs=[shifted, strip(8)], out_shape=[SDS(dzx.shape, dzx.dtype), SDS((8, C), F32)],
        input_output_aliases={4: 0}, name=name, compiler_params=_cp("parallel"))(dxbc, zx, cw, cb, dzx)


def _scan_constants(n_heads):
    hw = n_heads * HEAD_DIM
    col = np.arange(hw)
    ind = (col[None, :] // HEAD_DIM == np.arange(LANE)[:, None]).astype(np.float32)
    gcol = np.arange(GROUP_W)
    itile = (gcol[None, :] % CHUNK == np.arange(CHUNK)[:, None]).astype(np.float32)
    trit = (gcol[None, :] % CHUNK <= np.arange(CHUNK)[:, None]).astype(np.float32)
    tril = np.tril(np.ones((CHUNK, CHUNK), np.float32))
    bmask = (gcol[:, None] // HEAD_DIM == gcol[None, :] // HEAD_DIM).astype(np.float32)
    return (jnp.asarray(ind, BF16), jnp.asarray(ind.T.copy(), BF16), jnp.asarray(itile), jnp.asarray(trit),
            jnp.asarray(tril, BF16), jnp.asarray(bmask))


def _softplus(x):
    return jnp.maximum(x, 0.0) + jnp.log(1.0 + jnp.exp(-jnp.abs(x)))


def _split3(x):
    hi = x.astype(BF16)
    r1 = x - hi.astype(F32)
    mid = r1.astype(BF16)
    return hi, mid, (r1 - mid.astype(F32)).astype(BF16)


def _dot_sel(x, sel, dims=None):
    if dims is None:
        mm = lambda p: jnp.dot(p, sel, preferred_element_type=F32)
    else:
        mm = lambda p: lax.dot_general(sel, p, dims, preferred_element_type=F32)
    hi, mid, lo = _split3(x)
    return (mm(lo) + mm(mid)) + mm(hi)


SEL_X = (((1,), (0,)), ((), ()))


def _head_lanes(v, g):
    h0 = HEADS_PER_GROUP * g
    return jnp.concatenate([jnp.broadcast_to(v[:, h0 + r:h0 + r + 1], (v.shape[0], HEAD_DIM))
                            for r in range(HEADS_PER_GROUP)], axis=1)


def _group_terms(g, dt, cs, cst, xbc_ref, trit, bmask, d_inner):
    gl = slice(g * GROUP_W, (g + 1) * GROUP_W)
    h0 = HEADS_PER_GROUP * g
    csl = _head_lanes(cs, g)
    dtx = _head_lanes(dt, g)
    rr = jnp.concatenate([cst[h0 + r:h0 + r + 1, :] for r in range(HEADS_PER_GROUP)], axis=1)
    lm = jnp.exp(jnp.where(trit > 0.0, csl - rr, -jnp.inf))
    xs = xbc_ref[:, gl]
    b = xbc_ref[:, d_inner + g * D_STATE: d_inner + (g + 1) * D_STATE]
    c = xbc_ref[:, d_inner + (N_GROUPS + g) * D_STATE: d_inner + (N_GROUPS + g + 1) * D_STATE]
    u = xs * dtx
    bb, cb = b.astype(BF16), c.astype(BF16)
    btile = jnp.concatenate([bb] * HEADS_PER_GROUP, axis=0)
    cbt = lax.dot_general(cb, btile, NT, preferred_element_type=F32)
    m = cbt * lm
    ub = u.astype(BF16)
    bdu = jnp.where(bmask > 0.0, jnp.concatenate([ub] * HEADS_PER_GROUP, axis=0), jnp.zeros((), BF16))
    c_last = csl[CHUNK - 1:CHUNK, :]
    return dict(gl=gl, csl=csl, dtx=dtx, lm=lm, xs=xs, bb=bb, cb=cb, u=u, btile=btile, m=m, bdu=bdu,
                e=jnp.exp(csl), dec=jnp.exp(c_last - csl), e_last=jnp.exp(c_last))


def _ssd_scan_fwd(zx, xbc, par, dexp, nw, consts, comm, name):
    L = xbc.shape[0]
    d_inner = dexp.shape[1]
    n_chunks = L // CHUNK
    dt_blk = zx.shape[1] // LANE - 1
    ind, ind_t, itile_c, trit_c, tril_c, bmask_c = consts

    def body(xbc_ref, z_ref, dtr_ref, par_ref, dexp_ref, nw_ref, trit_ref, tril_ref, bmask_ref,
             yn_ref, yf_ref, st_out_ref, st_ref):
        @pl.when(pl.program_id(0) == 0)
        def _():
            st_ref[...] = jnp.zeros_like(st_ref)

        dt = _softplus(dtr_ref[...] + par_ref[0:1, :])
        a_head = -jnp.exp(par_ref[1:2, :])
        cs = _dot_sel(dt * a_head, tril_ref[...], SEL_X)
        cst = cs.T
        trit, bmask = trit_ref[...], bmask_ref[...]
        for g in range(N_GROUPS):
            t = _group_terms(g, dt, cs, cst, xbc_ref, trit, bmask, d_inner)
            p = st_ref[g]
            st_out_ref[0, g] = p
            y = jnp.dot(t["m"].astype(BF16), t["bdu"], preferred_element_type=F32)
            y = y + jnp.dot(t["cb"], p.astype(BF16), preferred_element_type=F32) * t["e"]
            st_new = lax.dot_general(t["bb"], (t["u"] * t["dec"]).astype(BF16), TN, preferred_element_type=F32)
            st_ref[g] = p * t["e_last"] + st_new
            yf_ref[:, t["gl"]] = y + t["xs"] * dexp_ref[:, t["gl"]]
        z = z_ref[...]
        y2 = yf_ref[...] * (z * _sigmoid(z))
        yn_ref[...] = (y2 * _rsq(y2) * nw_ref[...]).astype(BF16)

    row = lambda c: (c, 0)
    fix = lambda c: (0, 0)
    cspec = lambda a: pl.BlockSpec(a.shape, fix)
    return _carrier_call(
        body, comm, n_chunks,
        in_specs=[pl.BlockSpec((CHUNK, xbc.shape[1]), row), pl.BlockSpec((CHUNK, d_inner), row),
                  pl.BlockSpec((CHUNK, LANE), lambda c: (c, dt_blk)), cspec(par), cspec(dexp), cspec(nw),
                  cspec(trit_c), cspec(tril_c), cspec(bmask_c)],
        out_specs=[pl.BlockSpec((CHUNK, d_inner), row), pl.BlockSpec((CHUNK, d_inner), row),
                   pl.BlockSpec((1, N_GROUPS, D_STATE, GROUP_W), lambda c: (c, 0, 0, 0))],
        out_shape=[SDS((L, d_inner), BF16), SDS((L, d_inner), F32),
                   SDS((n_chunks, N_GROUPS, D_STATE, GROUP_W), F32)],
        scratch_shapes=[pltpu.VMEM((N_GROUPS, D_STATE, GROUP_W), F32)],
        name=name, args=(xbc, zx, zx, par, dexp, nw, trit_c, tril_c, bmask_c))


def _ssd_scan_bwd(dyn, yf, zx, xbc, states, par, dexp, nw, consts, comm, name):
    L = xbc.shape[0]
    d_inner = dexp.shape[1]
    n_chunks = L // CHUNK
    nz = zx.shape[1]
    dt_blk = nz // LANE - 1
    ind, ind_t, itile_c, trit_c, tril_c, bmask_c = consts
    hslices = [slice(r * HEAD_DIM, (r + 1) * HEAD_DIM) for r in range(HEADS_PER_GROUP)]

    def body(dyn_ref, yf_ref, z_ref, dtr_ref, xbc_ref, st_in_ref, par_ref, dexp_ref, nw_ref, indt_ref,
             itile_ref, trit_ref, tril_ref, bmask_ref,
             dzx_ref, dxbc_ref, dnw_ref, dpar_ref, dq_ref, dyf_ref):
        @pl.when(pl.program_id(0) == 0)
        def _():
            dq_ref[...] = jnp.zeros_like(dq_ref)
            dnw_ref[...] = jnp.zeros_like(dnw_ref)
            dpar_ref[...] = jnp.zeros_like(dpar_ref)

        z, yfv, dynv = z_ref[...], yf_ref[...], dyn_ref[...]
        sz = _sigmoid(z)
        y2 = yfv * (z * sz)
        r = _rsq(y2)
        y2h = y2 * r
        dnw_ref[...] += jnp.sum(dynv * y2h, axis=0, keepdims=True)
        dyg = dynv * nw_ref[...]
        dy2 = r * (dyg - y2h * jnp.mean(dyg * y2h, axis=-1, keepdims=True))
        dzx_ref[:, 0:d_inner] = (dy2 * yfv * _dsilu(z, sz)).astype(BF16)
        dyf_ref[...] = dy2 * (z * sz)

        pre = dtr_ref[...] + par_ref[0:1, :]
        dt = _softplus(pre)
        a_head = -jnp.exp(par_ref[1:2, :])
        cs = _dot_sel(dt * a_head, tril_ref[...], SEL_X)
        cst = cs.T
        itile, trit, bmask = itile_ref[...], trit_ref[...], bmask_ref[...]
        dcs = jnp.zeros((CHUNK, LANE), F32)
        dcs_last = jnp.zeros((1, LANE), F32)
        ddt_u = jnp.zeros((CHUNK, LANE), F32)
        d_skip = jnp.zeros((1, LANE), F32)
        rsum = lambda v: jnp.sum(v, axis=0, keepdims=True)
        row8 = lax.broadcasted_iota(jnp.int32, (8, GROUP_W), 0)
        for g in range(N_GROUPS):
            t = _group_terms(g, dt, cs, cst, xbc_ref, trit, bmask, d_inner)
            gl, m, lm, u, bb, cb, e, dec, xs = (t[k] for k in ("gl", "m", "lm", "u", "bb", "cb", "e", "dec", "xs"))
            indt = indt_ref[gl, :]
            dy = dyf_ref[:, gl]
            dyb = dy.astype(BF16)
            p = st_in_ref[0, g]
            pb = p.astype(BF16)
            q = dq_ref[g]
            qb = q.astype(BF16)
            big = lax.dot_general(m.astype(BF16), dyb, TN, preferred_element_type=F32)
            du = jnp.zeros((CHUNK, GROUP_W), F32)
            for rh in range(HEADS_PER_GROUP):
                du = du + big[hslices[rh], :] * bmask[rh * HEAD_DIM:rh * HEAD_DIM + 1, :]
            dm = lax.dot_general(dyb, t["bdu"], NT, preferred_element_type=F32)
            w = dm * m
            dgt = (dm * lm).astype(BF16)
            dc = jnp.dot(dgt, t["btile"], preferred_element_type=F32)
            db_big = lax.dot_general(dgt, cb, TN, preferred_element_type=F32)
            db = db_big[hslices[0], :] + db_big[hslices[1], :] + db_big[hslices[2], :] + db_big[hslices[3], :]
            cp = jnp.dot(cb, pb, preferred_element_type=F32)
            dye = dy * e
            dyeb = dye.astype(BF16)
            dc = dc + lax.dot_general(dyeb, pb, NT, preferred_element_type=F32)
            dp = lax.dot_general(cb, dyeb, TN, preferred_element_type=F32)
            x2 = dye * cp
            bq = jnp.dot(bb, qb, preferred_element_type=F32)
            ud = u * dec
            du = du + bq * dec
            db = db + lax.dot_general(ud.astype(BF16), qb, NT, preferred_element_type=F32)
            x1 = bq * ud
            dq_ref[g] = dp + t["e_last"] * q
            x3 = rsum(q * p) * t["e_last"]
            red = _dot_sel(jnp.concatenate([w + x2 - x1, du * xs, itile * rsum(w)], axis=0), indt)
            dcs = dcs + red[0:CHUNK] - red[2 * CHUNK:3 * CHUNK]
            ddt_u = ddt_u + red[CHUNK:2 * CHUNK]
            tail = _dot_sel(jnp.where(row8 == 0, rsum(x1) + x3, jnp.where(row8 == 1, rsum(dy * xs), 0.0)), indt)
            dcs_last = dcs_last + tail[0:1]
            d_skip = d_skip + tail[1:2]
            dxbc_ref[:, gl] = du * t["dtx"] + dy * dexp_ref[:, gl]
            dxbc_ref[:, d_inner + g * D_STATE: d_inner + (g + 1) * D_STATE] = db
            dxbc_ref[:, d_inner + (N_GROUPS + g) * D_STATE: d_inner + (N_GROUPS + g + 1) * D_STATE] = dc
        last = lax.broadcasted_iota(jnp.int32, (CHUNK, LANE), 0) == CHUNK - 1
        dcs = dcs + jnp.where(last, dcs_last, 0.0)
        da = _dot_sel(dcs, tril_ref[...], TN)
        ddt = da * a_head + ddt_u
        heads = lax.broadcasted_iota(jnp.int32, (CHUNK, LANE), 1) < d_inner // HEAD_DIM
        ddt_raw = jnp.where(heads, ddt * _sigmoid(pre), 0.0)
        dzx_ref[:, nz - LANE:nz] = ddt_raw.astype(BF16)
        dpar_ref[0:1, :] += rsum(ddt_raw)
        dpar_ref[1:2, :] += rsum(da * dt) * a_head
        dpar_ref[2:3, :] += d_skip

    rev = lambda c: (n_chunks - 1 - c, 0)
    fix = lambda c: (0, 0)
    cspec = lambda a: pl.BlockSpec(a.shape, fix)
    nx = xbc.shape[1]
    return _carrier_call(
        body, comm, n_chunks,
        in_specs=[pl.BlockSpec((CHUNK, d_inner), rev), pl.BlockSpec((CHUNK, d_inner), rev),
                  pl.BlockSpec((CHUNK, d_inner), rev), pl.BlockSpec((CHUNK, LANE), lambda c: (n_chunks - 1 - c, dt_blk)),
                  pl.BlockSpec((CHUNK, nx), rev),
                  pl.BlockSpec((1, N_GROUPS, D_STATE, GROUP_W), lambda c: (n_chunks - 1 - c, 0, 0, 0)),
                  cspec(par), cspec(dexp), cspec(nw), cspec(ind_t), cspec(itile_c), cspec(trit_c),
                  cspec(tril_c), cspec(bmask_c)],
        out_specs=[pl.BlockSpec((CHUNK, nz), rev), pl.BlockSpec((CHUNK, nx), rev),
                   pl.BlockSpec((1, d_inner), fix), pl.BlockSpec((8, LANE), fix)],
        out_shape=[SDS((L, nz), BF16), SDS((L, nx), F32), SDS((1, d_inner), F32), SDS((8, LANE), F32)],
        scratch_shapes=[pltpu.VMEM((N_GROUPS, D_STATE, GROUP_W), F32), pltpu.VMEM((CHUNK, d_inner), F32)],
        name=name, args=(dyn, yf, zx, zx, xbc, states, par, dexp, nw, ind_t, itile_c, trit_c, tril_c, bmask_c))


def _adamw(w, m, v, g, name):
    R, C = w.shape
    tr = R
    for cand in (256, 128, 64, 32, 16, 8):
        if R % cand == 0:
            tr = cand
            break

    def body(w_ref, m_ref, v_ref, g_ref, d_ref, mo_ref, vo_ref):
        gv = g_ref[...]
        mn = ADAM_B1 * m_ref[...] + (1.0 - ADAM_B1) * gv
        vn = ADAM_B2 * v_ref[...] + (1.0 - ADAM_B2) * (gv * gv)
        m_hat = mn / (1.0 - ADAM_B1 ** ADAM_STEP)
        v_hat = vn / (1.0 - ADAM_B2 ** ADAM_STEP)
        d_ref[...] = -ADAM_LR * (m_hat / (jnp.sqrt(v_hat) + ADAM_EPS) + ADAM_WD * w_ref[...])
        mo_ref[...] = mn
        vo_ref[...] = vn

    blk = pl.BlockSpec((tr, C), lambda i: (i, 0))
    return pl.pallas_call(
        body, grid=(R // tr,), in_specs=[blk] * 4, out_specs=[blk] * 3, out_shape=[SDS((R, C), F32)] * 3,
        name=name, compiler_params=_cp("parallel"))(w, m, v, g)


def _sum_slots(parts, name):
    n, R, C = parts.shape
    tr = 128 if R % 128 == 0 else R

    def body(p_ref, o_ref):
        acc = p_ref[0]
        for k in range(1, n):
            acc = acc + p_ref[k]
        o_ref[...] = acc

    return pl.pallas_call(
        body, grid=(R // tr,), in_specs=[pl.BlockSpec((n, tr, C), lambda i: (0, i, 0))],
        out_specs=pl.BlockSpec((tr, C), lambda i: (i, 0)), out_shape=SDS((R, C), F32),
        name=name, compiler_params=_cp("parallel"))(parts)


def _add_core_halves(where, g, r, name):
    _, n, a, b = g.shape
    ta = a // 2

    def body(w_ref, g_ref, r_ref, o_ref):
        o_ref[...] = (g_ref[...].astype(F32) + r_ref[...].astype(F32)).astype(BF16)

    blk = lambda f: pl.BlockSpec((None, None, ta, b), f)
    mine = lambda s, l, w: (s, l, 0, 0)
    return pl.pallas_call(
        body, grid_spec=pltpu.PrefetchScalarGridSpec(
            num_scalar_prefetch=1, grid=(4, n),
            in_specs=[blk(lambda s, l, w: (s, l, w[1], 0)), blk(mine)], out_specs=blk(mine)),
        out_shape=SDS((4, n, ta, b), BF16), name=name,
        compiler_params=_cp("parallel", "parallel"))(where, g, r)


def _sum_shard(where, g, r, rr, into, layer, n_layers, name):
    _, _, a, b = g.shape
    ta = a // 2

    def body(w_ref, g_ref, r_ref, rr_ref, *refs):
        f = lambda v: v.astype(F32)
        refs[-1][...] = (((f(g_ref[...]) + f(r_ref[...])) + f(rr_ref[0])) + f(rr_ref[1])) + f(rr_ref[2])

    more = [] if into is None else [into]
    return pl.pallas_call(
        body, grid_spec=pltpu.PrefetchScalarGridSpec(
            num_scalar_prefetch=1, grid=(1,),
            in_specs=[pl.BlockSpec((None, None, ta, b), lambda l, w: (w[0], 0, w[1], 0)),
                      pl.BlockSpec((None, None, ta, b), lambda l, w: (w[0], 0, 0, 0)),
                      pl.BlockSpec((3, None, ta, b), lambda l, w: (0, 0, 0, 0))] + [ANY] * len(more),
            out_specs=pl.BlockSpec((None, ta, b), lambda l, w: (layer, w[1], 0))),
        out_shape=SDS((n_layers, a, b), F32), name=name, input_output_aliases={4: 0} if more else {},
        compiler_params=_cp("arbitrary"))(where, g, r, rr, *more)


def _me():
    return lax.axis_index("x"), lax.axis_index("y"), lax.axis_index("c")


def _chip_peers(x, y):
    return [(1 - x, y), (x, 1 - y), (1 - x, 1 - y)]


def _rcopy(src, dst, send_sems, recv_sems, k, to):
    return pltpu.make_async_remote_copy(src_ref=src, dst_ref=dst, send_sem=send_sems.at[k], recv_sem=recv_sems.at[k],
                                        device_id=to, device_id_type=MESH)


def _row_half(ref, c, lead=()):
    a = ref.shape[len(lead) + 1]
    return ref.at[(*lead, slice(None), pl.ds(c * (a // 2), a // 2))]


class _Comm(NamedTuple):
    ins: list
    out_shapes: list
    n_sems: int
    start: Callable
    finish: Callable


def _sem_scratch(comm):
    return [pltpu.SemaphoreType.DMA((comm.n_sems,)), pltpu.SemaphoreType.DMA((comm.n_sems,))]


def _run_comm(comm, name):
    n_in, n_out = len(comm.ins), len(comm.out_shapes)

    def body(*refs):
        ins, outs, sems = refs[:n_in], refs[n_in:n_in + n_out], refs[n_in + n_out:]
        comm.start(ins, outs, *sems)
        comm.finish(ins, outs, *sems)

    return pl.pallas_call(body, in_specs=[ANY] * n_in, out_specs=[ANY] * n_out, out_shape=comm.out_shapes,
                          scratch_shapes=_sem_scratch(comm), name=name)(*comm.ins)


def _carrier_call(compute, comm, n_steps, in_specs, out_specs, out_shape, scratch_shapes, name, args):
    if comm is None:
        return pl.pallas_call(compute, grid=(n_steps,), in_specs=in_specs, out_specs=out_specs, out_shape=out_shape,
                              scratch_shapes=scratch_shapes, name=name, compiler_params=_cp("arbitrary"))(*args), []
    n_in, n_out, n_scr = len(in_specs), len(out_specs), len(scratch_shapes)
    n_ci, n_co = len(comm.ins), len(comm.out_shapes)

    def body(*refs):
        ins, cins = refs[:n_in], refs[n_in:n_in + n_ci]
        o = n_in + n_ci
        outs, couts = refs[o:o + n_out], refs[o + n_out:o + n_out + n_co]
        s = o + n_out + n_co
        scratch, sems = refs[s:s + n_scr], refs[s + n_scr:]

        @pl.when(pl.program_id(0) == 0)
        def _():
            comm.start(cins, couts, *sems)

        compute(*ins, *outs, *scratch)

        @pl.when(pl.program_id(0) == n_steps - 1)
        def _():
            comm.finish(cins, couts, *sems)

    res = pl.pallas_call(
        body, grid=(n_steps,), in_specs=list(in_specs) + [ANY] * n_ci, out_specs=list(out_specs) + [ANY] * n_co,
        out_shape=list(out_shape) + list(comm.out_shapes), scratch_shapes=list(scratch_shapes) + _sem_scratch(comm),
        name=name, compiler_params=_cp("arbitrary"))(*args, *comm.ins)
    return res[:n_out], res[n_out:]


def _allgather_plan(mine, small=None):
    n = len(mine)
    ins = list(mine) + ([] if small is None else [small])
    out_shapes = [SDS((4,) + m.shape, BF16) for m in mine] + ([] if small is None else [SDS((4,) + small.shape, F32)])
    sem = lambda t, k: 7 * t + k

    def first_copies(ins_r, outs_r, send, recv):
        x, y, c = _me()
        q = 2 * x + y
        cps = []
        for j, chip in enumerate(_chip_peers(x, y)):
            for t in range(n):
                cps.append(_rcopy(_row_half(ins_r[t], c), _row_half(outs_r[t], c, (q,)), send, recv, sem(t, j),
                                  (*chip, c)))
            if small is not None:
                cps.append(_rcopy(ins_r[n], outs_r[n].at[q], send, recv, sem(n, j), (*chip, c)))
        for t in range(len(ins)):
            cps.append(_rcopy(ins_r[t], outs_r[t].at[q], send, recv, sem(t, 6), (x, y, 1 - c)))
        return cps

    def start(ins_r, outs_r, send, recv):
        for cp in first_copies(ins_r, outs_r, send, recv):
            cp.start()

    def finish(ins_r, outs_r, send, recv):
        x, y, c = _me()
        sib = (x, y, 1 - c)
        chips = _chip_peers(x, y)
        passed = []
        for j, (px, py) in enumerate(chips):
            for t in range(n):
                blk = _row_half(outs_r[t], c, (2 * px + py,))
                _rcopy(blk, blk, send, recv, sem(t, j), sib).wait_recv()
                cp = _rcopy(blk, blk, send, recv, sem(t, 3 + j), sib)
                cp.start()
                passed.append(cp)
        for j, (px, py) in enumerate(chips):
            for t in range(n):
                blk = _row_half(outs_r[t], 1 - c, (2 * px + py,))
                _rcopy(blk, blk, send, recv, sem(t, 3 + j), sib).wait_recv()
            if small is not None:
                sblk = outs_r[n].at[2 * px + py]
                _rcopy(sblk, sblk, send, recv, sem(n, j), sib).wait_recv()
        for t in range(len(ins)):
            own = outs_r[t].at[2 * x + y]
            _rcopy(own, own, send, recv, sem(t, 6), sib).wait_recv()
        for cp in first_copies(ins_r, outs_r, send, recv) + passed:
            cp.wait_send()

    return _Comm(ins, out_shapes, 7 * len(ins), start, finish)


def _grads_to_sibling_plan(grads):
    n = len(grads)

    def copies(ins_r, outs_r, send, recv):
        x, y, c = _me()
        return [_rcopy(_row_half(ins_r[t], 1 - c, (slice(None),)), outs_r[t], send, recv, t, (x, y, 1 - c))
                for t in range(n)]

    def start(*a):
        for cp in copies(*a):
            cp.start()

    def finish(*a):
        for cp in copies(*a):
            cp.wait()

    out_shapes = [SDS((4, g.shape[1], g.shape[2] // 2, g.shape[3]), BF16) for g in grads]
    return _Comm(list(grads), out_shapes, n, start, finish)


def _grads_to_chips_plan(psums):
    n = len(psums)

    def copies(ins_r, outs_r, send, recv):
        x, y, c = _me()
        return [_rcopy(ins_r[t].at[2 * px + py], outs_r[t].at[j], send, recv, 3 * t + j, (px, py, c))
                for j, (px, py) in enumerate(_chip_peers(x, y)) for t in range(n)]

    def start(*a):
        for cp in copies(*a):
            cp.start()

    def finish(*a):
        for cp in copies(*a):
            cp.wait()

    return _Comm(list(psums), [SDS((3,) + p.shape[1:], BF16) for p in psums], 3 * n, start, finish)


def _swap_halves(sums):
    n = len(sums)

    def body(*refs):
        out_refs = refs[n:2 * n]
        send_sems, recv_sems = refs[2 * n:]
        x, y, c = _me()
        sib = (x, y, 1 - c)
        cps = [_rcopy(_row_half(out_refs[t], c), _row_half(out_refs[t], c), send_sems, recv_sems, t, sib)
               for t in range(n)]
        for cp in cps:
            cp.start()
        for t in range(n):
            other = _row_half(out_refs[t], 1 - c)
            _rcopy(other, other, send_sems, recv_sems, t, sib).wait_recv()
        for cp in cps:
            cp.wait_send()

    return pl.pallas_call(
        body, in_specs=[ANY] * n, out_specs=[ANY] * n, out_shape=[SDS(s.shape, F32) for s in sums],
        input_output_aliases={t: t for t in range(n)},
        scratch_shapes=[pltpu.SemaphoreType.DMA((n,)), pltpu.SemaphoreType.DMA((n,))],
        name="swap_halves")(*sums)


def _allgather_small(part):
    def body(p_ref, out_ref, send_sems, recv_sems, local_sem):
        x, y, c = _me()
        me = 4 * x + 2 * y + c
        own = pltpu.make_async_copy(p_ref, out_ref.at[me], local_sem.at[0])
        own.start()
        sends = []
        for k in range(1, 8):
            fx, fy, fc = (k >> 2) & 1, (k >> 1) & 1, k & 1
            to = (x ^ fx, y ^ fy, c ^ fc)
            sends.append(_rcopy(p_ref, out_ref.at[me], send_sems, recv_sems, k - 1, to))
        for cp in sends:
            cp.start()
        for k in range(1, 8):
            slot = out_ref.at[me ^ k]
            _rcopy(slot, slot, send_sems, recv_sems, k - 1, (x, y, c)).wait_recv()
        for cp in sends:
            cp.wait_send()
        own.wait()

    return pl.pallas_call(
        body, in_specs=[ANY], out_specs=ANY, out_shape=SDS((8,) + part.shape, F32),
        scratch_shapes=[pltpu.SemaphoreType.DMA((7,)), pltpu.SemaphoreType.DMA((7,)), pltpu.SemaphoreType.DMA((1,))],
        name="allgather_small")(part)


BIG = (("ssd_w_in", 2), ("ssd_w_out", 1), ("sc_w_in", 2), ("sc_w_out", 1), ("ffn_w_up", 2), ("ffn_w_down", 1))


def _to_shards(full, axis):
    A, B = full.shape
    if axis == 2:
        return full.reshape(A, 4, B // 4).transpose(1, 0, 2)
    return full.reshape(4, A // 4, B)


def _from_shards(shards, axis):
    _, a, b = shards.shape
    if axis == 2:
        return shards.transpose(1, 0, 2).reshape(a, 4 * b)
    return shards.reshape(4 * a, b)


def _interleave(w, parts):
    lead, n = w.shape[:-1], w.shape[-1]
    return w.reshape(*lead, parts, n // (parts * LANE), LANE).swapaxes(-2, -3).reshape(*lead, n)


def _deinterleave(w, parts):
    lead, n = w.shape[:-1], w.shape[-1]
    return w.reshape(*lead, n // (parts * LANE), parts, LANE).swapaxes(-2, -3).reshape(*lead, n)


def _pack_rows(vectors, width, row_multiple):
    flat = jnp.concatenate(vectors, axis=-1)
    n = flat.shape[-1]
    unit = width * row_multiple
    total = -(-n // unit) * unit
    flat = jnp.pad(flat, [(0, 0)] * (flat.ndim - 1) + [(0, total - n)])
    return flat.reshape(*flat.shape[:-1], total // width, width)


def _unpack(flat, shapes):
    out, off = [], 0
    for s in shapes:
        n = int(np.prod(s))
        out.append(flat[..., off:off + n].reshape(*flat.shape[:-1], *s))
        off += n
    return out


def _memo(fn):
    cache = {}

    def wrapped(k):
        if k not in cache:
            cache[k] = fn(k)
        return cache[k]

    return wrapped


def _row(v):
    return v.reshape(1, -1)


def _pad_rows(w, rows=8):
    return jnp.pad(w, ((0, rows - w.shape[0]), (0, 0)))


def _ffn_fwd(x, g_pre, g_post, w_up, cw, cb, w_down, tag, comm_up=None, comm_down=None):
    (up, hn), got_up = _ffn_up(x, g_pre, w_up, "ffn_up" + tag, comm_up)
    a = _ffn_mid_fwd(up, cw, cb, "ffn_mid_fwd" + tag)
    (f, x_new), got_down = _matmul_norm_res(a, w_down, x, g_post, "ffn_down" + tag, comm_down)
    return x_new, (x, hn, up, a, f), got_up, got_down


def _ffn_bwd(dx, saved, g_pre, g_post, w_up, cw, cb, w_down, tag):
    x, hn, up, a, f = saved
    df, dg_post = _postnorm_bwd(dx, f, g_post, "ffn_post_bwd" + tag)
    da = _matmul_nt(df, w_down, BF16, "ffn_down_dx" + tag)
    dw_down = _matmul_tn(a, df, w_down.shape[0] // 2, w_down.shape[1], "ffn_down_dw" + tag)
    dup, stats = _ffn_mid_bwd(da, up, cw, cb, "ffn_mid_bwd" + tag)
    dx_in, dg_pre = _ffn_up_dx(dup, w_up, x, g_pre, dx, "ffn_up_dx" + tag)
    dw_up = _ffn_up_dw(hn, dup, "ffn_up_dw" + tag)
    return dx_in, dict(g_pre=dg_pre, g_post=dg_post, w_up=dw_up, w_down=dw_down, cw=stats[0:3], cb=stats[3])


def _sc_fwd(x, g_pre, g_post, w_in, cw, w_out, tag, comm_in=None, comm_out=None):
    (bcv, hn), got_in = _norm_matmul(x, g_pre, w_in, BF16, "sc_in" + tag, comm_in)
    q = _sc_mid_fwd(bcv, cw, "sc_mid_fwd" + tag)
    (m, x_new), got_out = _matmul_norm_res(q, w_out, x, g_post, "sc_out" + tag, comm_out)
    return x_new, (x, hn, bcv, q, m), got_in, got_out


def _sc_bwd(dx, saved, g_pre, g_post, w_in, cw, w_out, tag):
    x, hn, bcv, q, m = saved
    dm, dg_post = _postnorm_bwd(dx, m, g_post, "sc_post_bwd" + tag)
    dq = _matmul_nt(dm, w_out, BF16, "sc_out_dx" + tag)
    dw_out = _matmul_tn(q, dm, w_out.shape[0], w_out.shape[1], "sc_out_dw" + tag)
    dbcv, stats = _sc_mid_bwd(dq, bcv, cw, "sc_mid_bwd" + tag)
    (dx_in, dg_pre), _ = _matmul_nt_prenorm_bwd(dbcv, w_in, x, g_pre, dx, "sc_in_dx" + tag)
    dw_in = _matmul_tn(hn, dbcv, w_in.shape[0], w_in.shape[1] // 3, "sc_in_dw" + tag)
    return dx_in, dict(g_pre=dg_pre, g_post=dg_post, w_in=dw_in, w_out=dw_out, cw=stats[0:3])


def _ssd_fwd(x, g_pre, g_post, w_in, cw, cb, par, dexp, nw, w_out, consts, comm, tag, comm_in=None):
    d_inner = dexp.shape[1]
    (zx, hn), got_in = _norm_matmul(x, g_pre, w_in, F32, "ssd_in" + tag, comm_in)
    xbc = _ssd_conv_fwd(zx, cw, cb, d_inner, cw.shape[1], "ssd_conv_fwd" + tag)
    (yn, yf, states), got = _ssd_scan_fwd(zx, xbc, par, dexp, nw, consts, comm, "ssd_scan_fwd" + tag)
    if callable(w_out):
        w_out = w_out(got_in)
    (m, x_new), _ = _matmul_norm_res(yn, w_out, x, g_post, "ssd_out" + tag)
    return x_new, (x, hn, zx, xbc, yn, yf, states, m), got


def _ssd_bwd(dx, saved, g_pre, g_post, w_in, cw, cb, par, dexp, nw, w_out, consts, comm, tag, comm_dx=None):
    x, hn, zx, xbc, yn, yf, states, m = saved
    d_inner = dexp.shape[1]
    dm, dg_post = _postnorm_bwd(dx, m, g_post, "ssd_post_bwd" + tag)
    dyn = _matmul_nt(dm, w_out, F32, "ssd_out_dx" + tag)
    dw_out = _matmul_tn(yn, dm, w_out.shape[0] // 2, w_out.shape[1], "ssd_out_dw" + tag)
    (dzx, dxbc, dnw, dpar), got = _ssd_scan_bwd(dyn, yf, zx, xbc, states, par, dexp, nw, consts, comm(dw_out),
                                                "ssd_scan_bwd" + tag)
    dzx, stats = _ssd_conv_bwd(dxbc, zx, cw, cb, d_inner, dzx, "ssd_conv_bwd" + tag)
    dw_in = _matmul_tn(hn, dzx, w_in.shape[0], w_in.shape[1] // 7, "ssd_in_dw" + tag)
    (dx_in, dg_pre), got_dx = _matmul_nt_prenorm_bwd(dzx, w_in, x, g_pre, dx, "ssd_in_dx" + tag,
                                                     None if comm_dx is None else comm_dx(dw_in))
    n_heads = d_inner // HEAD_DIM
    grads = dict(g_pre=dg_pre, g_post=dg_post, w_in=dw_in, w_out=dw_out, cw=stats[0:4], cb=stats[4],
                 dt_bias=dpar[0, :n_heads], a_log=dpar[1, :n_heads], d=dpar[2, :n_heads], nw=dnw[0])
    return dx_in, grads, got, got_dx


def kernel(x, mix_pre_g, mix_post_g, ffn_pre_g, ffn_post_g, ssd_w_in, ssd_conv_w, ssd_conv_b, ssd_dt_bias, ssd_A_log, ssd_D, ssd_norm_w, ssd_w_out, sc_w_in, sc_conv_w, sc_w_out, ffn_w_up, ffn_conv_w, ffn_conv_b, ffn_w_down, loss_target, m_mix_pre_g, m_mix_post_g, m_ffn_pre_g, m_ffn_post_g, m_ssd_w_in, m_ssd_conv_w, m_ssd_conv_b, m_ssd_dt_bias, m_ssd_A_log, m_ssd_D, m_ssd_norm_w, m_ssd_w_out, m_sc_w_in, m_sc_conv_w, m_sc_w_out, m_ffn_w_up, m_ffn_conv_w, m_ffn_conv_b, m_ffn_w_down, v_mix_pre_g, v_mix_post_g, v_ffn_pre_g, v_ffn_post_g, v_ssd_w_in, v_ssd_conv_w, v_ssd_conv_b, v_ssd_dt_bias, v_ssd_A_log, v_ssd_D, v_ssd_norm_w, v_ssd_w_out, v_sc_w_in, v_sc_conv_w, v_sc_w_out, v_ffn_w_up, v_ffn_conv_w, v_ffn_conv_b, v_ffn_w_down):
    names = ["mix_pre_g", "mix_post_g", "ffn_pre_g", "ffn_post_g", "ssd_w_in", "ssd_conv_w", "ssd_conv_b",
             "ssd_dt_bias", "ssd_A_log", "ssd_D", "ssd_norm_w", "ssd_w_out", "sc_w_in", "sc_conv_w", "sc_w_out",
             "ffn_w_up", "ffn_conv_w", "ffn_conv_b", "ffn_w_down"]
    env = locals()
    wts = {n: env[n] for n in names}
    mom = {n: env["m_" + n] for n in names}
    var = {n: env["v_" + n] for n in names}

    depth, d_model = mix_pre_g.shape
    n_ssd, n_heads = ssd_dt_bias.shape
    n_sc = sc_conv_w.shape[0]
    d_inner = n_heads * HEAD_DIM
    conv_dim = d_inner + 2 * N_GROUPS * D_STATE
    ssd_in_dim = d_inner + conv_dim + n_heads
    ssd_in_pad = d_inner + conv_dim + LANE
    q_chip = 2 * lax.axis_index("x") + lax.axis_index("y")
    core = lax.axis_index("c")

    assert depth == 4 and n_ssd == 2 and n_sc == 2, "the exchange schedule is written for this trunk"
    ssd_items = lambda j: [("ssd_w_in", j), ("ssd_w_out", j)]
    sc_items = lambda j: [("sc_w_in", j), ("sc_w_out", j)]
    ffn_items = lambda i: [("ffn_w_up", i), ("ffn_w_down", i)]
    gather_first = [("ssd_w_in", 0)]
    gather_in_ssd_in = {0: [("ssd_w_out", 0)]}
    gather_in_scan = {0: ffn_items(0) + sc_items(0), 2: ffn_items(2) + sc_items(1)}
    gather_in_ffn = {0: ([("ffn_w_up", 1)], [("ffn_w_down", 1)]), 1: ([("ssd_w_in", 1)], []),
                     2: ([("ffn_w_up", 3)], [("ffn_w_down", 3)])}
    gather_in_sc = {1: ([("ssd_w_out", 1)], [])}
    reduce_in_scan = {2: ffn_items(3) + sc_items(1) + ffn_items(2),
                      0: ssd_items(1) + ffn_items(1) + sc_items(0) + ffn_items(0) + [("ssd_w_out", 0)]}
    reduce_in_dx = {0: [("ssd_w_in", 0)]}
    axis_of = dict(BIG)
    gathered = {}

    def gather_plan(items, small=None):
        mine = [wts[n][layer:layer + 1].astype(BF16) for n, layer in items]
        return _allgather_plan(mine, small) if items else None

    def gather_done(items, results):
        for item, buf in zip(items, results):
            gathered[item] = buf[:, 0]

    def full(n, layer):
        return _from_shards(gathered[(n, layer)], axis_of[n])

    conv_names = ["ssd_conv_w", "sc_conv_w", "ffn_conv_w"]
    conv_shapes = [wts[n].shape for n in conv_names]
    small_mine = _pack_rows([wts[n].reshape(-1) for n in conv_names], LANE, 8)
    *results, small_all = _run_comm(gather_plan(gather_first, small_mine), "allgather_first")
    gather_done(gather_first, results)
    conv_full = {}
    for n, f, s in zip(conv_names, _unpack(small_all.reshape(4, -1), conv_shapes), conv_shapes):
        conv_full[n] = f.transpose(1, 2, 0, 3).reshape(s[0], s[1], 4 * s[2])

    consts = _scan_constants(n_heads)

    def ssd_args(j):
        par = jnp.zeros((8, LANE), F32).at[0, :n_heads].set(ssd_dt_bias[j]).at[1, :n_heads].set(ssd_A_log[j])
        dexp = jnp.repeat(ssd_D[j], HEAD_DIM).reshape(1, d_inner)
        w_in = jnp.pad(full("ssd_w_in", j), ((0, 0), (0, ssd_in_pad - ssd_in_dim)))
        return (w_in, _pad_rows(conv_full["ssd_conv_w"][j]), _row(ssd_conv_b[j]), par, dexp, _row(ssd_norm_w[j]))

    def sc_args(j):
        return (_interleave(full("sc_w_in", j), 3), _pad_rows(conv_full["sc_conv_w"][j]), full("sc_w_out", j))

    def ffn_args(i):
        return (gathered[("ffn_w_up", i)], _pad_rows(conv_full["ffn_conv_w"][i]), _row(ffn_conv_b[i]),
                full("ffn_w_down", i))

    ssd_args, sc_args, ffn_args = _memo(ssd_args), _memo(sc_args), _memo(ffn_args)

    h = x[0]
    saved = []
    for i in range(depth):
        j = i // 2
        gp, gq = _row(mix_pre_g[i]), _row(mix_post_g[i])
        if i % 2 == 0:
            items_in = gather_in_ssd_in.get(i, [])

            def w_out_when_here(got_in, items_in=items_in, j=j):
                gather_done(items_in, got_in)
                return full("ssd_w_out", j)

            h, sv, results = _ssd_fwd(h, gp, gq, *ssd_args(j), w_out_when_here, consts, gather_plan(gather_in_scan[i]),
                                      tag="", comm_in=gather_plan(items_in))
            gather_done(gather_in_scan[i], results)
        else:
            items_in, items_out = gather_in_sc.get(i, ([], []))
            h, sv, got_in, got_out = _sc_fwd(h, gp, gq, *sc_args(j), tag="", comm_in=gather_plan(items_in),
                                             comm_out=gather_plan(items_out))
            gather_done(items_in, got_in)
            gather_done(items_out, got_out)
        items_up, items_down = gather_in_ffn.get(i, ([], []))
        h, sv2, got_up, got_down = _ffn_fwd(h, _row(ffn_pre_g[i]), _row(ffn_post_g[i]), *ffn_args(i), tag="",
                                            comm_up=gather_plan(items_up), comm_down=gather_plan(items_down))
        gather_done(items_up, got_up)
        gather_done(items_down, got_down)
        saved.append((sv, sv2))
    dh, loss_part = _loss_head(h, loss_target[0], "loss_head")

    mix_grads, ffn_grads = [None] * depth, [None] * depth
    where = jnp.stack([q_chip, core]).astype(jnp.int32)

    def shard_grad(n, layer):
        if n == "ffn_w_up":
            g = ffn_grads[layer]["w_up"]
        elif n == "ffn_w_down":
            g = _to_shards(ffn_grads[layer]["w_down"], 1)
        elif n == "ssd_w_in":
            g = _to_shards(early[(n, layer)][:, :ssd_in_dim], 2)
        elif n == "ssd_w_out":
            g = _to_shards(early[(n, layer)], 1)
        elif n == "sc_w_in":
            g = _to_shards(_deinterleave(mix_grads[2 * layer + 1]["w_in"], 3), 2)
        else:
            g = _to_shards(mix_grads[2 * layer + 1]["w_out"], 1)
        return g[:, None]

    def reduce_begin(items, tag):
        by_shard = [shard_grad(n, layer) for n, layer in items]
        from_sib = _run_comm(_grads_to_sibling_plan(by_shard), "grads_to_sibling" + tag)
        chip_sums = [_add_core_halves(where, g, r, "add_core_halves_%s%d" % item)
                     for item, g, r in zip(items, by_shard, from_sib)]
        return by_shard, from_sib, _grads_to_chips_plan(chip_sums)

    sums = {}
    early = {}

    def riding(items, tag):
        by_shard, from_sib, plan = reduce_begin(items, tag)

        def arrived(from_chips):
            for (n, layer), g, r, rr in zip(items, by_shard, from_sib, from_chips):
                sums[n] = _sum_shard(where, g, r, rr, sums.get(n), layer, wts[n].shape[0],
                                     "sum_shard_%s%d" % (n, layer))

        return plan, arrived

    for i in reversed(range(depth)):
        j = i // 2
        sv, sv2 = saved[i]
        dh, ffn_grads[i] = _ffn_bwd(dh, sv2, _row(ffn_pre_g[i]), _row(ffn_post_g[i]), *ffn_args(i), tag="")
        gp, gq = _row(mix_pre_g[i]), _row(mix_post_g[i])
        if i % 2 == 0:
            then = {}

            def in_scan(dw_out, i=i, j=j, then=then):
                early[("ssd_w_out", j)] = dw_out
                plan, then["scan"] = riding(reduce_in_scan[i], "_%d" % i)
                return plan

            def in_dx(dw_in, i=i, j=j, then=then):
                early[("ssd_w_in", j)] = dw_in
                plan, then["dx"] = riding(reduce_in_dx[i], "_dx%d" % i) if i in reduce_in_dx else (None, None)
                return plan

            dh, mix_grads[i], got, got_dx = _ssd_bwd(dh, sv, gp, gq, *ssd_args(j), full("ssd_w_out", j), consts,
                                                     in_scan, tag="", comm_dx=in_dx)
            then["scan"](got)
            if then["dx"] is not None:
                then["dx"](got_dx)
        else:
            w_in, scw, w_out = sc_args(j)
            dh, mix_grads[i] = _sc_bwd(dh, sv, gp, gq, w_in, scw, w_out, tag="")
    grad_x = dh[None]
    big_grads = dict(zip([n for n, _ in BIG], _swap_halves([sums[n] for n, _ in BIG])))
    ssd_l = [mix_grads[i] for i in range(0, depth, 2)]
    sc_l = [mix_grads[i] for i in range(1, depth, 2)]
    stack = lambda layers, k: jnp.stack([g[k] for g in layers])

    small_names = ["mix_pre_g", "mix_post_g", "ffn_pre_g", "ffn_post_g", "ssd_conv_w", "ssd_conv_b", "ssd_dt_bias",
                   "ssd_A_log", "ssd_D", "ssd_norm_w", "sc_conv_w", "ffn_conv_w", "ffn_conv_b"]
    small_local = {
        "mix_pre_g": jnp.concatenate([g["g_pre"] for g in mix_grads]),
        "mix_post_g": jnp.concatenate([g["g_post"] for g in mix_grads]),
        "ffn_pre_g": jnp.concatenate([g["g_pre"] for g in ffn_grads]),
        "ffn_post_g": jnp.concatenate([g["g_post"] for g in ffn_grads]),
        "ssd_conv_w": stack(ssd_l, "cw"), "ssd_conv_b": stack(ssd_l, "cb"), "ssd_dt_bias": stack(ssd_l, "dt_bias"),
        "ssd_A_log": stack(ssd_l, "a_log"), "ssd_D": stack(ssd_l, "d"), "ssd_norm_w": stack(ssd_l, "nw"),
        "sc_conv_w": stack(sc_l, "cw"), "ffn_conv_w": stack(ffn_grads, "cw"), "ffn_conv_b": stack(ffn_grads, "cb"),
    }
    small_full_shapes = [small_local[n].shape for n in small_names]
    spack = _pack_rows([small_local[n].reshape(-1) for n in small_names] + [loss_part.reshape(-1)], LANE, 8)
    stotal = _sum_slots(_allgather_small(spack), "sum_small").reshape(-1)
    small_grads = dict(zip(small_names, _unpack(stotal, small_full_shapes)))
    loss = stotal[sum(int(np.prod(s)) for s in small_full_shapes)]
    for n in conv_names:
        width = wts[n].shape[-1]
        small_grads[n] = lax.dynamic_slice_in_dim(small_grads[n], q_chip * width, width, axis=2)

    grads, delta, new_m, new_v = {}, {}, {}, {}
    for n, _ in BIG:
        s = wts[n].shape
        two_d = lambda a: a.reshape(-1, s[-1])
        grads[n] = big_grads[n]
        d, mn, vn = _adamw(two_d(wts[n]), two_d(mom[n]), two_d(var[n]), two_d(grads[n]), "adamw_" + n)
        delta[n], new_m[n], new_v[n] = d.reshape(s), mn.reshape(s), vn.reshape(s)
    small_shapes = [wts[n].shape for n in small_names]
    pk = lambda d: _pack_rows([d[n].reshape(-1) for n in small_names], LANE, 8)
    for n in small_names:
        grads[n] = small_grads[n].reshape(wts[n].shape)
    d, mn, vn = _adamw(pk(wts), pk(mom), pk(var), pk(grads), "adamw_small")
    for out, packed in ((delta, d), (new_m, mn), (new_v, vn)):
        out.update(zip(small_names, _unpack(packed.reshape(-1), small_shapes)))

    return (loss, grad_x, *[grads[n] for n in names], *[delta[n] for n in names], *[new_m[n] for n in names],
            *[new_v[n] for n in names])
```

```python
from typing import Callable, NamedTuple

import jax
import jax.numpy as jnp
import numpy as np
from jax import lax
from jax.experimental import pallas as pl
from jax.experimental.pallas import tpu as pltpu

F32 = jnp.float32
BF16 = jnp.bfloat16
SDS = jax.ShapeDtypeStruct
MESH = pl.DeviceIdType.MESH
ANY = pl.BlockSpec(memory_space=pl.ANY)

EPS = 1e-6
CHUNK = 64
HEAD_DIM = 64
N_GROUPS = 8
D_STATE = 128
HEADS_PER_GROUP = 4
GROUP_W = HEADS_PER_GROUP * HEAD_DIM
LANE = 128
ROW_TILE = 128
HALO = 8
VMEM_LIMIT = 56 * 1024 * 1024

ADAM_LR = 0.001
ADAM_B1 = 0.9
ADAM_B2 = 0.999
ADAM_EPS = 1e-08
ADAM_WD = 0.01
ADAM_STEP = 10

NT = (((1,), (1,)), ((), ()))
TN = (((0,), (0,)), ((), ()))


def _cp(*sem):
    return pltpu.CompilerParams(dimension_semantics=sem or None, vmem_limit_bytes=VMEM_LIMIT)


def _sigmoid(x):
    return 1.0 / (1.0 + jnp.exp(-x))


def _dsilu(x, s):
    return s * (1.0 + x * (1.0 - s))


def _rsq(x):
    return lax.rsqrt(jnp.mean(x * x, axis=-1, keepdims=True) + EPS)


MM_ROWS = 256


def _norm_matmul(x, g, w, out_dtype, name, comm=None):
    L, D = x.shape
    N = w.shape[1]
    tm = min(MM_ROWS, L)

    def body(x_ref, g_ref, w_ref, o_ref, hn_ref):
        xv = x_ref[...]
        hn = (xv * _rsq(xv) * g_ref[...]).astype(BF16)
        hn_ref[...] = hn
        o_ref[...] = jnp.dot(hn, w_ref[...], preferred_element_type=F32).astype(out_dtype)

    row = lambda i: (i, 0)
    fix = lambda i: (0, 0)
    return _carrier_call(
        body, comm, L // tm,
        in_specs=[pl.BlockSpec((tm, D), row), pl.BlockSpec((1, D), fix), pl.BlockSpec((D, N), fix)],
        out_specs=[pl.BlockSpec((tm, N), row), pl.BlockSpec((tm, D), row)],
        out_shape=[SDS((L, N), out_dtype), SDS((L, D), BF16)], scratch_shapes=[], name=name, args=(x, g, w))


def _matmul_norm_res(a, w, x, g, name, comm=None):
    L, K = a.shape
    D = w.shape[1]
    tm = min(MM_ROWS, L)

    def body(a_ref, w_ref, x_ref, g_ref, m_ref, xo_ref):
        m = jnp.dot(a_ref[...], w_ref[...], preferred_element_type=F32)
        m_ref[...] = m
        xo_ref[...] = x_ref[...] + m * _rsq(m) * g_ref[...]

    row = lambda i: (i, 0)
    fix = lambda i: (0, 0)
    return _carrier_call(
        body, comm, L // tm,
        in_specs=[pl.BlockSpec((tm, K), row), pl.BlockSpec((K, D), fix), pl.BlockSpec((tm, D), row),
                  pl.BlockSpec((1, D), fix)],
        out_specs=[pl.BlockSpec((tm, D), row), pl.BlockSpec((tm, D), row)],
        out_shape=[SDS((L, D), F32), SDS((L, D), F32)], scratch_shapes=[], name=name, args=(a, w, x, g))


def _postnorm_bwd(dx, m, g, name):
    L, D = dx.shape
    tm = min(512, L)

    def body(dx_ref, m_ref, g_ref, dm_ref, dg_ref):
        @pl.when(pl.program_id(0) == 0)
        def _():
            dg_ref[...] = jnp.zeros_like(dg_ref)

        m = m_ref[...]
        dxv = dx_ref[...]
        r = _rsq(m)
        mh = m * r
        dg_ref[...] += jnp.sum(dxv * mh, axis=0, keepdims=True)
        dyg = dxv * g_ref[...]
        dm_ref[...] = (r * (dyg - mh * jnp.mean(dyg * mh, axis=-1, keepdims=True))).astype(BF16)

    row = lambda i: (i, 0)
    fix = lambda i: (0, 0)
    return pl.pallas_call(
        body, grid=(L // tm,),
        in_specs=[pl.BlockSpec((tm, D), row), pl.BlockSpec((tm, D), row), pl.BlockSpec((1, D), fix)],
        out_specs=[pl.BlockSpec((tm, D), row), pl.BlockSpec((1, D), fix)],
        out_shape=[SDS((L, D), BF16), SDS((1, D), F32)],
        name=name, compiler_params=_cp("arbitrary"))(dx, m, g)


def _matmul_nt(a, w, out_dtype, name):
    L, D = a.shape
    K = w.shape[0]
    tm = min(MM_ROWS, L)

    def body(a_ref, w_ref, o_ref):
        o_ref[...] = lax.dot_general(a_ref[...], w_ref[...], NT, preferred_element_type=F32).astype(out_dtype)

    return pl.pallas_call(
        body, grid=(L // tm,),
        in_specs=[pl.BlockSpec((tm, D), lambda i: (i, 0)), pl.BlockSpec((K, D), lambda i: (0, 0))],
        out_specs=pl.BlockSpec((tm, K), lambda i: (i, 0)),
        out_shape=SDS((L, K), out_dtype),
        name=name, compiler_params=_cp("parallel"))(a, w)


DW_ACC_BYTES = 13 * 512 * 1024


def _dw_tiles(ka, n):
    ta = ka if ka <= 1024 else ka // 2
    fits = [d for d in range(LANE, n + 1, LANE) if n % d == 0 and ta * d * 4 <= DW_ACC_BYTES]
    return ta, max(fits)


def _matmul_tn(a, b, name):
    L, Ka = a.shape
    N = b.shape[1]
    ta, tn = _dw_tiles(Ka, N)
    tl = min(512, L)
    n_l = L // tl

    def body(a_ref, b_ref, o_ref, acc_ref):
        l = pl.program_id(2)

        @pl.when(l == 0)
        def _():
            acc_ref[...] = jnp.zeros_like(acc_ref)

        acc_ref[...] += lax.dot_general(a_ref[...], b_ref[...], TN, preferred_element_type=F32)

        @pl.when(l == n_l - 1)
        def _():
            o_ref[...] = acc_ref[...].astype(BF16)

    return pl.pallas_call(
        body, grid=(Ka // ta, N // tn, n_l),
        in_specs=[pl.BlockSpec((tl, ta), lambda i, j, l: (l, i)), pl.BlockSpec((tl, tn), lambda i, j, l: (l, j))],
        out_specs=pl.BlockSpec((ta, tn), lambda i, j, l: (i, j)),
        out_shape=SDS((Ka, N), BF16),
        scratch_shapes=[pltpu.VMEM((ta, tn), F32)],
        name=name, compiler_params=_cp("parallel", "parallel", "arbitrary"))(a, b)


def _ffn_up(x, g, w4, name, comm=None):
    L, D = x.shape
    b = w4.shape[2]
    tm = min(MM_ROWS, L)

    def body(x_ref, g_ref, w_ref, o_ref, hn_ref):
        xv = x_ref[...]
        hn = (xv * _rsq(xv) * g_ref[...]).astype(BF16)
        hn_ref[...] = hn
        for q in range(4):
            o_ref[q // 2, :, (q % 2) * b:(q % 2 + 1) * b] = jnp.dot(hn, w_ref[q], preferred_element_type=F32).astype(BF16)

    return _carrier_call(
        body, comm, L // tm,
        in_specs=[pl.BlockSpec((tm, D), lambda i: (i, 0)), pl.BlockSpec((1, D), lambda i: (0, 0)),
                  pl.BlockSpec((4, D, b), lambda i: (0, 0, 0))],
        out_specs=[pl.BlockSpec((2, tm, 2 * b), lambda i: (0, i, 0)), pl.BlockSpec((tm, D), lambda i: (i, 0))],
        out_shape=[SDS((2, L, 2 * b), BF16), SDS((L, D), BF16)], scratch_shapes=[], name=name, args=(x, g, w4))


def _ffn_up_dx(dup, w4, x, g, dres, name):
    _, L, _ = dup.shape
    _, D, b = w4.shape
    tm = min(MM_ROWS, L)

    def body(dy_ref, w_ref, x_ref, g_ref, dres_ref, dx_ref, dg_ref):
        @pl.when(pl.program_id(0) == 0)
        def _():
            dg_ref[...] = jnp.zeros_like(dg_ref)

        dh = jnp.zeros((tm, D), F32)
        for q in range(4):
            dh = dh + lax.dot_general(dy_ref[q // 2, :, (q % 2) * b:(q % 2 + 1) * b], w_ref[q], NT,
                                      preferred_element_type=F32)
        xv = x_ref[...]
        r = _rsq(xv)
        xh = xv * r
        dg_ref[...] += jnp.sum(dh * xh, axis=0, keepdims=True)
        dyg = dh * g_ref[...]
        dx_ref[...] = dres_ref[...] + r * (dyg - xh * jnp.mean(dyg * xh, axis=-1, keepdims=True))

    row = lambda i: (i, 0)
    fix = lambda i: (0, 0)
    return pl.pallas_call(
        body, grid=(L // tm,),
        in_specs=[pl.BlockSpec((2, tm, 2 * b), lambda i: (0, i, 0)), pl.BlockSpec((4, D, b), lambda i: (0, 0, 0)),
                  pl.BlockSpec((tm, D), row), pl.BlockSpec((1, D), fix), pl.BlockSpec((tm, D), row)],
        out_specs=[pl.BlockSpec((tm, D), row), pl.BlockSpec((1, D), fix)],
        out_shape=[SDS((L, D), F32), SDS((1, D), F32)],
        name=name, compiler_params=_cp("arbitrary"))(dup, w4, x, g, dres)


def _ffn_up_dw(hn, dup, name):
    L, D = hn.shape
    b = dup.shape[2] // 2
    tl = min(512, L)
    n_l = L // tl

    def body(a_ref, b_ref, o_ref, acc_ref):
        l = pl.program_id(1)

        @pl.when(l == 0)
        def _():
            acc_ref[...] = jnp.zeros_like(acc_ref)

        acc_ref[...] += lax.dot_general(a_ref[...], b_ref[...], TN, preferred_element_type=F32)

        @pl.when(l == n_l - 1)
        def _():
            o_ref[...] = acc_ref[...].astype(BF16)

    return pl.pallas_call(
        body, grid=(4, n_l),
        in_specs=[pl.BlockSpec((tl, D), lambda q, l: (l, 0)),
                  pl.BlockSpec((None, tl, b), lambda q, l: (q // 2, l, q % 2))],
        out_specs=pl.BlockSpec((None, D, b), lambda q, l: (q, 0, 0)),
        out_shape=SDS((4, D, b), BF16),
        scratch_shapes=[pltpu.VMEM((D, b), F32)],
        name=name, compiler_params=_cp("parallel", "arbitrary"))(hn, dup)


def _matmul_nt_prenorm_bwd(dy, w, x, g, dres, name, comm=None):
    L, N = dy.shape
    D = w.shape[0]
    tm = min(MM_ROWS, L)

    def body(dy_ref, w_ref, x_ref, g_ref, dres_ref, dx_ref, dg_ref):
        @pl.when(pl.program_id(0) == 0)
        def _():
            dg_ref[...] = jnp.zeros_like(dg_ref)

        dh = lax.dot_general(dy_ref[...], w_ref[...], NT, preferred_element_type=F32)
        xv = x_ref[...]
        r = _rsq(xv)
        xh = xv * r
        dg_ref[...] += jnp.sum(dh * xh, axis=0, keepdims=True)
        dyg = dh * g_ref[...]
        dx_ref[...] = dres_ref[...] + r * (dyg - xh * jnp.mean(dyg * xh, axis=-1, keepdims=True))

    row = lambda i: (i, 0)
    fix = lambda i: (0, 0)
    return _carrier_call(
        body, comm, L // tm,
        in_specs=[pl.BlockSpec((tm, N), row), pl.BlockSpec((D, N), fix), pl.BlockSpec((tm, D), row),
                  pl.BlockSpec((1, D), fix), pl.BlockSpec((tm, D), row)],
        out_specs=[pl.BlockSpec((tm, D), row), pl.BlockSpec((1, D), fix)],
        out_shape=[SDS((L, D), F32), SDS((1, D), F32)], scratch_shapes=[], name=name, args=(dy, w, x, g, dres))


def _loss_head(y, t, name):
    L, D = y.shape
    tm = min(512, L)

    def body(y_ref, t_ref, dy_ref, loss_ref):
        @pl.when(pl.program_id(0) == 0)
        def _():
            loss_ref[...] = jnp.zeros_like(loss_ref)

        e = y_ref[...] - t_ref[...]
        dy_ref[...] = e * (1.0 / D)
        s = jnp.sum(jnp.sum(e * e, axis=1, keepdims=True), axis=0, keepdims=True)
        loss_ref[...] += s * (0.5 / D)

    row = lambda i: (i, 0)
    return pl.pallas_call(
        body, grid=(L // tm,),
        in_specs=[pl.BlockSpec((tm, D), row), pl.BlockSpec((tm, D), row)],
        out_specs=[pl.BlockSpec((tm, D), row), pl.BlockSpec((1, 1), lambda i: (0, 0))],
        out_shape=[SDS((L, D), F32), SDS((1, 1), F32)],
        name=name, compiler_params=_cp("arbitrary"))(y, t)


def _tile_rows(ref):
    return HALO * (4 // jnp.dtype(ref.dtype).itemsize)


def _prev_rows(ref, r0, i, cols):
    n = _tile_rows(ref)
    p0 = pl.multiple_of(jnp.maximum(r0 - n, 0), n)
    return jnp.where(i > 0, ref[pl.ds(p0, n), cols].astype(F32)[n - HALO:], 0.0)


def _next_rows(ref, r0, i, n_tiles, cols):
    n = _tile_rows(ref)
    n0 = pl.multiple_of(jnp.minimum(r0 + ROW_TILE, n_tiles * ROW_TILE - n), n)
    return jnp.where(i < n_tiles - 1, ref[pl.ds(n0, n), cols].astype(F32)[:HALO], 0.0)


def _rows_f32(ref, rows, cols):
    return ref[rows, cols].astype(F32)


def _back(ext, s):
    return pltpu.roll(ext, s, axis=0)[HALO:HALO + ROW_TILE]


def _fwd(ext, s):
    n = ext.shape[0]
    return pltpu.roll(ext, n - s, axis=0)[:ROW_TILE]


def _store_rows(ref, rows):
    ref[...] = jnp.zeros_like(ref)
    for k, v in enumerate(rows):
        ref[k:k + 1, :] = v


def _strip_call(body, L, n_strips, ins, outs, name):
    def spec(rows, width, off):
        if off is None:
            return pl.BlockSpec((rows, width), lambda j: (0, 0))
        return pl.BlockSpec((rows, width), lambda j: (0, j + off))

    return pl.pallas_call(
        body, grid=(n_strips,),
        in_specs=[spec(a.shape[0], w, off) for a, w, off in ins],
        out_specs=[spec(s.shape[0], w, off) for s, w, off in outs],
        out_shape=[s for s, _, _ in outs],
        name=name, compiler_params=_cp("parallel"))(*[a for a, _, _ in ins])


def _ffn_mid_fwd(up, cw, cb, name):
    _, L, C = up.shape
    n_tiles = L // ROW_TILE
    al = slice(None)

    def body(up_ref, cw_ref, cb_ref, a_ref):
        w0, w1, w2 = cw_ref[0:1, :], cw_ref[1:2, :], cw_ref[2:3, :]
        b = cb_ref[...]
        gate_ref, val_ref = up_ref.at[0], up_ref.at[1]

        def step(i, c):
            r0 = pl.multiple_of(i * ROW_TILE, ROW_TILE)
            rows = pl.ds(r0, ROW_TILE)
            gp = _rows_f32(gate_ref, rows, al)
            ext = jnp.concatenate([_prev_rows(gate_ref, r0, i, al), gp], axis=0)
            gate = gp * w2 + _back(ext, 1) * w1 + _back(ext, 2) * w0 + b
            a_ref[rows, :] = (gate * _sigmoid(gate) * _rows_f32(val_ref, rows, al)).astype(BF16)
            return c

        lax.fori_loop(0, n_tiles, step, 0)

    strip = lambda rows: pl.BlockSpec((rows, LANE), lambda j: (0, j))
    return pl.pallas_call(
        body, grid=(C // LANE,),
        in_specs=[pl.BlockSpec((2, L, LANE), lambda j: (0, 0, j)), strip(cw.shape[0]), strip(1)],
        out_specs=strip(L), out_shape=SDS((L, C), BF16), name=name, compiler_params=_cp("parallel"))(up, cw, cb)


def _ffn_mid_bwd(da, up, cw, cb, name):
    L, C = da.shape
    n_tiles = L // ROW_TILE
    al = slice(None)

    def body(da_ref, up_ref, cw_ref, cb_ref, dup_ref, st_ref):
        w0, w1, w2 = cw_ref[0:1, :], cw_ref[1:2, :], cw_ref[2:3, :]
        b = cb_ref[...]
        gate_ref, val_ref = up_ref.at[0], up_ref.at[1]

        def step(i, c):
            r0 = pl.multiple_of(i * ROW_TILE, ROW_TILE)
            rows = pl.ds(r0, ROW_TILE)
            gp = _rows_f32(gate_ref, rows, al)
            gpe = jnp.concatenate([_prev_rows(gate_ref, r0, i, al), gp, _next_rows(gate_ref, r0, i, n_tiles, al)],
                                  axis=0)
            g1, g2 = pltpu.roll(gpe, 1, axis=0), pltpu.roll(gpe, 2, axis=0)
            gate = (gpe * w2 + g1 * w1 + g2 * w0 + b)[HALO:]
            sg = _sigmoid(gate)
            da_e = jnp.concatenate([_rows_f32(da_ref, rows, al), _next_rows(da_ref, r0, i, n_tiles, al)], axis=0)
            val_e = jnp.concatenate([_rows_f32(val_ref, rows, al), _next_rows(val_ref, r0, i, n_tiles, al)], axis=0)
            dgate_e = da_e * val_e * _dsilu(gate, sg)
            dgate = dgate_e[:ROW_TILE]
            dgp = dgate * w2 + _fwd(dgate_e, 1) * w1 + _fwd(dgate_e, 2) * w0
            dup_ref[0, rows, :] = dgp.astype(BF16)
            dup_ref[1, rows, :] = (da_e * gate * sg)[:ROW_TILE].astype(BF16)
            s = lambda v: jnp.sum(v, axis=0, keepdims=True)
            t = slice(HALO, HALO + ROW_TILE)
            return (c[0] + s(dgate * g2[t]), c[1] + s(dgate * g1[t]), c[2] + s(dgate * gp), c[3] + s(dgate))

        z = jnp.zeros((1, LANE), F32)
        _store_rows(st_ref, lax.fori_loop(0, n_tiles, step, (z, z, z, z)))

    strip = lambda rows: pl.BlockSpec((rows, LANE), lambda j: (0, j))
    pair = pl.BlockSpec((2, L, LANE), lambda j: (0, 0, j))
    return pl.pallas_call(
        body, grid=(C // LANE,), in_specs=[strip(L), pair, strip(cw.shape[0]), strip(1)],
        out_specs=[pair, strip(8)], out_shape=[SDS((2, L, C), BF16), SDS((8, C), F32)],
        name=name, compiler_params=_cp("parallel"))(da, up, cw, cb)


def _sc_mid_fwd(bcv, cw, name):
    L = bcv.shape[0]
    C = bcv.shape[1] // 3
    n_tiles = L // ROW_TILE
    s0, s1, s2 = slice(0, LANE), slice(LANE, 2 * LANE), slice(2 * LANE, 3 * LANE)

    def body(x_ref, cw_ref, q_ref):
        w0, w1, w2 = cw_ref[0:1, :], cw_ref[1:2, :], cw_ref[2:3, :]

        def step(i, c):
            r0 = pl.multiple_of(i * ROW_TILE, ROW_TILE)
            rows = pl.ds(r0, ROW_TILE)
            p = _rows_f32(x_ref, rows, s1) * _rows_f32(x_ref, rows, s2)
            ext = jnp.concatenate([_prev_rows(x_ref, r0, i, s1) * _prev_rows(x_ref, r0, i, s2), p], axis=0)
            u = p * w2 + _back(ext, 1) * w1 + _back(ext, 2) * w0
            q_ref[rows, :] = (_rows_f32(x_ref, rows, s0) * u).astype(BF16)
            return c

        lax.fori_loop(0, n_tiles, step, 0)

    return _strip_call(body, L, C // LANE, [(bcv, 3 * LANE, 0), (cw, LANE, 0)],
                       [(SDS((L, C), BF16), LANE, 0)], name)[0]


def _sc_mid_bwd(dq, bcv, cw, name):
    L, C = dq.shape
    n_tiles = L // ROW_TILE
    s0, s1, s2, al = slice(0, LANE), slice(LANE, 2 * LANE), slice(2 * LANE, 3 * LANE), slice(None)

    def body(dq_ref, x_ref, cw_ref, dx_ref, st_ref):
        w0, w1, w2 = cw_ref[0:1, :], cw_ref[1:2, :], cw_ref[2:3, :]

        def step(i, c):
            r0 = pl.multiple_of(i * ROW_TILE, ROW_TILE)
            rows = pl.ds(r0, ROW_TILE)
            gb, gc, v = _rows_f32(x_ref, rows, s0), _rows_f32(x_ref, rows, s1), _rows_f32(x_ref, rows, s2)
            dq_v = _rows_f32(dq_ref, rows, al)
            p = gc * v
            pext = jnp.concatenate([_prev_rows(x_ref, r0, i, s1) * _prev_rows(x_ref, r0, i, s2), p], axis=0)
            p1, p2 = _back(pext, 1), _back(pext, 2)
            u = p * w2 + p1 * w1 + p2 * w0
            du = dq_v * gb
            du_n = _next_rows(dq_ref, r0, i, n_tiles, al) * _next_rows(x_ref, r0, i, n_tiles, s0)
            ext = jnp.concatenate([du, du_n], axis=0)
            dp = du * w2 + _fwd(ext, 1) * w1 + _fwd(ext, 2) * w0
            dx_ref[rows, s0] = (dq_v * u).astype(BF16)
            dx_ref[rows, s1] = (dp * v).astype(BF16)
            dx_ref[rows, s2] = (dp * gc).astype(BF16)
            s = lambda t: jnp.sum(t, axis=0, keepdims=True)
            return (c[0] + s(du * p2), c[1] + s(du * p1), c[2] + s(du * p))

        z = jnp.zeros((1, LANE), F32)
        _store_rows(st_ref, lax.fori_loop(0, n_tiles, step, (z, z, z)))

    return _strip_call(body, L, C // LANE, [(dq, LANE, 0), (bcv, 3 * LANE, 0), (cw, LANE, 0)],
                       [(SDS((L, 3 * C), BF16), 3 * LANE, 0), (SDS((8, C), F32), LANE, 0)], name)


def _ssd_conv_fwd(zx, cw, cb, col0, C, name):
    L = zx.shape[0]
    n_tiles = L // ROW_TILE
    al = slice(None)

    def body(x_ref, cw_ref, cb_ref, o_ref):
        w0, w1, w2, w3 = cw_ref[0:1, :], cw_ref[1:2, :], cw_ref[2:3, :], cw_ref[3:4, :]
        b = cb_ref[...]

        def step(i, c):
            r0 = pl.multiple_of(i * ROW_TILE, ROW_TILE)
            rows = pl.ds(r0, ROW_TILE)
            xv = x_ref[rows, :]
            ext = jnp.concatenate([_prev_rows(x_ref, r0, i, al), xv], axis=0)
            cv = xv * w3 + _back(ext, 1) * w2 + _back(ext, 2) * w1 + _back(ext, 3) * w0 + b
            o_ref[rows, :] = cv * _sigmoid(cv)
            return c

        lax.fori_loop(0, n_tiles, step, 0)

    return _strip_call(body, L, C // LANE, [(zx, LANE, col0 // LANE), (cw, LANE, 0), (cb, LANE, 0)],
                       [(SDS((L, C), F32), LANE, 0)], name)[0]


def _ssd_conv_bwd(dxbc, zx, cw, cb, col0, dzx, name):
    L, C = dxbc.shape
    n_tiles = L // ROW_TILE
    al = slice(None)

    def body(d_ref, x_ref, cw_ref, cb_ref, dzx_in_ref, o_ref, st_ref):
        w0, w1, w2, w3 = cw_ref[0:1, :], cw_ref[1:2, :], cw_ref[2:3, :], cw_ref[3:4, :]
        b = cb_ref[...]

        def step(i, c):
            r0 = pl.multiple_of(i * ROW_TILE, ROW_TILE)
            rows = pl.ds(r0, ROW_TILE)
            xv = x_ref[rows, :]
            xe = jnp.concatenate([_prev_rows(x_ref, r0, i, al), xv, _next_rows(x_ref, r0, i, n_tiles, al)], axis=0)
            x1, x2, x3 = pltpu.roll(xe, 1, axis=0), pltpu.roll(xe, 2, axis=0), pltpu.roll(xe, 3, axis=0)
            cv = (xe * w3 + x1 * w2 + x2 * w1 + x3 * w0 + b)[HALO:]
            de = jnp.concatenate([d_ref[rows, :], _next_rows(d_ref, r0, i, n_tiles, al)], axis=0)
            dc_ext = de * _dsilu(cv, _sigmoid(cv))
            dc = dc_ext[:ROW_TILE]
            o_ref[rows, :] = (dc * w3 + _fwd(dc_ext, 1) * w2 + _fwd(dc_ext, 2) * w1 + _fwd(dc_ext, 3) * w0).astype(BF16)
            s = lambda t: jnp.sum(t, axis=0, keepdims=True)
            t = slice(HALO, HALO + ROW_TILE)
            return (c[0] + s(dc * x3[t]), c[1] + s(dc * x2[t]), c[2] + s(dc * x1[t]), c[3] + s(dc * xv), c[4] + s(dc))

        z = jnp.zeros((1, LANE), F32)
        _store_rows(st_ref, lax.fori_loop(0, n_tiles, step, (z, z, z, z, z)))

    strip = lambda rows, off=0: pl.BlockSpec((rows, LANE), lambda j: (0, j + off))
    shifted = strip(L, col0 // LANE)
    return pl.pallas_call(
        body, grid=(C // LANE,), in_specs=[strip(L), shifted, strip(cw.shape[0]), strip(1), ANY],
        out_specs=[shifted, strip(8)], out_shape=[SDS(dzx.shape, dzx.dtype), SDS((8, C), F32)],
        input_output_aliases={4: 0}, name=name, compiler_params=_cp("parallel"))(dxbc, zx, cw, cb, dzx)


def _scan_constants(n_heads):
    hw = n_heads * HEAD_DIM
    col = np.arange(hw)
    ind = (col[None, :] // HEAD_DIM == np.arange(LANE)[:, None]).astype(np.float32)
    gcol = np.arange(GROUP_W)
    itile = (gcol[None, :] % CHUNK == np.arange(CHUNK)[:, None]).astype(np.float32)
    trit = (gcol[None, :] % CHUNK <= np.arange(CHUNK)[:, None]).astype(np.float32)
    tril = np.tril(np.ones((CHUNK, CHUNK), np.float32))
    bmask = (gcol[:, None] // HEAD_DIM == gcol[None, :] // HEAD_DIM).astype(np.float32)
    return (jnp.asarray(ind.T.copy(), BF16), jnp.asarray(itile), jnp.asarray(trit), jnp.asarray(tril, BF16),
            jnp.asarray(bmask))


def _softplus(x):
    return jnp.maximum(x, 0.0) + jnp.log(1.0 + jnp.exp(-jnp.abs(x)))


def _split3(x):
    hi = x.astype(BF16)
    r1 = x - hi.astype(F32)
    mid = r1.astype(BF16)
    return hi, mid, (r1 - mid.astype(F32)).astype(BF16)


def _dot_sel(x, sel, dims=None):
    if dims is None:
        mm = lambda p: jnp.dot(p, sel, preferred_element_type=F32)
    else:
        mm = lambda p: lax.dot_general(sel, p, dims, preferred_element_type=F32)
    hi, mid, lo = _split3(x)
    return (mm(lo) + mm(mid)) + mm(hi)


SEL_X = (((1,), (0,)), ((), ()))


def _head_lanes(v, g):
    h0 = HEADS_PER_GROUP * g
    return jnp.concatenate([jnp.broadcast_to(v[:, h0 + r:h0 + r + 1], (v.shape[0], HEAD_DIM))
                            for r in range(HEADS_PER_GROUP)], axis=1)


def _group_terms(g, dt, cs, cst, xbc_ref, trit, bmask, d_inner):
    gl = slice(g * GROUP_W, (g + 1) * GROUP_W)
    h0 = HEADS_PER_GROUP * g
    csl = _head_lanes(cs, g)
    dtx = _head_lanes(dt, g)
    rr = jnp.concatenate([cst[h0 + r:h0 + r + 1, :] for r in range(HEADS_PER_GROUP)], axis=1)
    lm = jnp.exp(jnp.where(trit > 0.0, csl - rr, -jnp.inf))
    xs = xbc_ref[:, gl]
    b = xbc_ref[:, d_inner + g * D_STATE: d_inner + (g + 1) * D_STATE]
    c = xbc_ref[:, d_inner + (N_GROUPS + g) * D_STATE: d_inner + (N_GROUPS + g + 1) * D_STATE]
    u = xs * dtx
    bb, cb = b.astype(BF16), c.astype(BF16)
    btile = jnp.concatenate([bb] * HEADS_PER_GROUP, axis=0)
    cbt = lax.dot_general(cb, btile, NT, preferred_element_type=F32)
    m = cbt * lm
    ub = u.astype(BF16)
    bdu = jnp.where(bmask > 0.0, jnp.concatenate([ub] * HEADS_PER_GROUP, axis=0), jnp.zeros((), BF16))
    c_last = csl[CHUNK - 1:CHUNK, :]
    return dict(gl=gl, csl=csl, dtx=dtx, lm=lm, xs=xs, bb=bb, cb=cb, u=u, btile=btile, m=m, bdu=bdu,
                e=jnp.exp(csl), dec=jnp.exp(c_last - csl), e_last=jnp.exp(c_last))


def _ssd_scan_fwd(zx, xbc, par, dexp, nw, consts, comm, name):
    L = xbc.shape[0]
    d_inner = dexp.shape[1]
    n_chunks = L // CHUNK
    dt_blk = zx.shape[1] // LANE - 1
    ind_t, itile_c, trit_c, tril_c, bmask_c = consts

    def body(xbc_ref, z_ref, dtr_ref, par_ref, dexp_ref, nw_ref, trit_ref, tril_ref, bmask_ref,
             yn_ref, yf_ref, st_out_ref, st_ref):
        @pl.when(pl.program_id(0) == 0)
        def _():
            st_ref[...] = jnp.zeros_like(st_ref)

        dt = _softplus(dtr_ref[...] + par_ref[0:1, :])
        a_head = -jnp.exp(par_ref[1:2, :])
        cs = _dot_sel(dt * a_head, tril_ref[...], SEL_X)
        cst = cs.T
        trit, bmask = trit_ref[...], bmask_ref[...]
        for g in range(N_GROUPS):
            t = _group_terms(g, dt, cs, cst, xbc_ref, trit, bmask, d_inner)
            p = st_ref[g]
            st_out_ref[0, g] = p
            y = jnp.dot(t["m"].astype(BF16), t["bdu"], preferred_element_type=F32)
            y = y + jnp.dot(t["cb"], p.astype(BF16), preferred_element_type=F32) * t["e"]
            st_new = lax.dot_general(t["bb"], (t["u"] * t["dec"]).astype(BF16), TN, preferred_element_type=F32)
            st_ref[g] = p * t["e_last"] + st_new
            yf_ref[:, t["gl"]] = y + t["xs"] * dexp_ref[:, t["gl"]]
        z = z_ref[...]
        y2 = yf_ref[...] * (z * _sigmoid(z))
        yn_ref[...] = (y2 * _rsq(y2) * nw_ref[...]).astype(BF16)

    row = lambda c: (c, 0)
    fix = lambda c: (0, 0)
    cspec = lambda a: pl.BlockSpec(a.shape, fix)
    return _carrier_call(
        body, comm, n_chunks,
        in_specs=[pl.BlockSpec((CHUNK, xbc.shape[1]), row), pl.BlockSpec((CHUNK, d_inner), row),
                  pl.BlockSpec((CHUNK, LANE), lambda c: (c, dt_blk)), cspec(par), cspec(dexp), cspec(nw),
                  cspec(trit_c), cspec(tril_c), cspec(bmask_c)],
        out_specs=[pl.BlockSpec((CHUNK, d_inner), row), pl.BlockSpec((CHUNK, d_inner), row),
                   pl.BlockSpec((1, N_GROUPS, D_STATE, GROUP_W), lambda c: (c, 0, 0, 0))],
        out_shape=[SDS((L, d_inner), BF16), SDS((L, d_inner), F32),
                   SDS((n_chunks, N_GROUPS, D_STATE, GROUP_W), F32)],
        scratch_shapes=[pltpu.VMEM((N_GROUPS, D_STATE, GROUP_W), F32)],
        name=name, args=(xbc, zx, zx, par, dexp, nw, trit_c, tril_c, bmask_c))


def _ssd_scan_bwd(dyn, yf, zx, xbc, states, par, dexp, nw, consts, comm, name):
    L = xbc.shape[0]
    d_inner = dexp.shape[1]
    n_chunks = L // CHUNK
    nz = zx.shape[1]
    dt_blk = nz // LANE - 1
    ind_t, itile_c, trit_c, tril_c, bmask_c = consts
    hslices = [slice(r * HEAD_DIM, (r + 1) * HEAD_DIM) for r in range(HEADS_PER_GROUP)]

    def body(dyn_ref, yf_ref, z_ref, dtr_ref, xbc_ref, st_in_ref, par_ref, dexp_ref, nw_ref, indt_ref,
             itile_ref, trit_ref, tril_ref, bmask_ref,
             dzx_ref, dxbc_ref, dnw_ref, dpar_ref, dq_ref, dyf_ref):
        @pl.when(pl.program_id(0) == 0)
        def _():
            dq_ref[...] = jnp.zeros_like(dq_ref)
            dnw_ref[...] = jnp.zeros_like(dnw_ref)
            dpar_ref[...] = jnp.zeros_like(dpar_ref)

        z, yfv, dynv = z_ref[...], yf_ref[...], dyn_ref[...]
        sz = _sigmoid(z)
        y2 = yfv * (z * sz)
        r = _rsq(y2)
        y2h = y2 * r
        dnw_ref[...] += jnp.sum(dynv * y2h, axis=0, keepdims=True)
        dyg = dynv * nw_ref[...]
        dy2 = r * (dyg - y2h * jnp.mean(dyg * y2h, axis=-1, keepdims=True))
        dzx_ref[:, 0:d_inner] = (dy2 * yfv * _dsilu(z, sz)).astype(BF16)
        dyf_ref[...] = dy2 * (z * sz)

        pre = dtr_ref[...] + par_ref[0:1, :]
        dt = _softplus(pre)
        a_head = -jnp.exp(par_ref[1:2, :])
        cs = _dot_sel(dt * a_head, tril_ref[...], SEL_X)
        cst = cs.T
        itile, trit, bmask = itile_ref[...], trit_ref[...], bmask_ref[...]
        dcs = jnp.zeros((CHUNK, LANE), F32)
        dcs_last = jnp.zeros((1, LANE), F32)
        ddt_u = jnp.zeros((CHUNK, LANE), F32)
        d_skip = jnp.zeros((1, LANE), F32)
        rsum = lambda v: jnp.sum(v, axis=0, keepdims=True)
        row8 = lax.broadcasted_iota(jnp.int32, (8, GROUP_W), 0)
        for g in range(N_GROUPS):
            t = _group_terms(g, dt, cs, cst, xbc_ref, trit, bmask, d_inner)
            gl, m, lm, u, bb, cb, e, dec, xs = (t[k] for k in ("gl", "m", "lm", "u", "bb", "cb", "e", "dec", "xs"))
            indt = indt_ref[gl, :]
            dy = dyf_ref[:, gl]
            dyb = dy.astype(BF16)
            p = st_in_ref[0, g]
            pb = p.astype(BF16)
            q = dq_ref[g]
            qb = q.astype(BF16)
            big = lax.dot_general(m.astype(BF16), dyb, TN, preferred_element_type=F32)
            du = jnp.zeros((CHUNK, GROUP_W), F32)
            for rh in range(HEADS_PER_GROUP):
                du = du + big[hslices[rh], :] * bmask[rh * HEAD_DIM:rh * HEAD_DIM + 1, :]
            dm = lax.dot_general(dyb, t["bdu"], NT, preferred_element_type=F32)
            w = dm * m
            dgt = (dm * lm).astype(BF16)
            dc = jnp.dot(dgt, t["btile"], preferred_element_type=F32)
            db_big = lax.dot_general(dgt, cb, TN, preferred_element_type=F32)
            db = db_big[hslices[0], :] + db_big[hslices[1], :] + db_big[hslices[2], :] + db_big[hslices[3], :]
            cp = jnp.dot(cb, pb, preferred_element_type=F32)
            dye = dy * e
            dyeb = dye.astype(BF16)
            dc = dc + lax.dot_general(dyeb, pb, NT, preferred_element_type=F32)
            dp = lax.dot_general(cb, dyeb, TN, preferred_element_type=F32)
            x2 = dye * cp
            bq = jnp.dot(bb, qb, preferred_element_type=F32)
            ud = u * dec
            du = du + bq * dec
            db = db + lax.dot_general(ud.astype(BF16), qb, NT, preferred_element_type=F32)
            x1 = bq * ud
            dq_ref[g] = dp + t["e_last"] * q
            x3 = rsum(q * p) * t["e_last"]
            red = _dot_sel(jnp.concatenate([w + x2 - x1, du * xs, itile * rsum(w)], axis=0), indt)
            dcs = dcs + red[0:CHUNK] - red[2 * CHUNK:3 * CHUNK]
            ddt_u = ddt_u + red[CHUNK:2 * CHUNK]
            tail = _dot_sel(jnp.where(row8 == 0, rsum(x1) + x3, jnp.where(row8 == 1, rsum(dy * xs), 0.0)), indt)
            dcs_last = dcs_last + tail[0:1]
            d_skip = d_skip + tail[1:2]
            dxbc_ref[:, gl] = du * t["dtx"] + dy * dexp_ref[:, gl]
            dxbc_ref[:, d_inner + g * D_STATE: d_inner + (g + 1) * D_STATE] = db
            dxbc_ref[:, d_inner + (N_GROUPS + g) * D_STATE: d_inner + (N_GROUPS + g + 1) * D_STATE] = dc
        last = lax.broadcasted_iota(jnp.int32, (CHUNK, LANE), 0) == CHUNK - 1
        dcs = dcs + jnp.where(last, dcs_last, 0.0)
        da = _dot_sel(dcs, tril_ref[...], TN)
        ddt = da * a_head + ddt_u
        heads = lax.broadcasted_iota(jnp.int32, (CHUNK, LANE), 1) < d_inner // HEAD_DIM
        ddt_raw = jnp.where(heads, ddt * _sigmoid(pre), 0.0)
        dzx_ref[:, nz - LANE:nz] = ddt_raw.astype(BF16)
        dpar_ref[0:1, :] += rsum(ddt_raw)
        dpar_ref[1:2, :] += rsum(da * dt) * a_head
        dpar_ref[2:3, :] += d_skip

    rev = lambda c: (n_chunks - 1 - c, 0)
    fix = lambda c: (0, 0)
    cspec = lambda a: pl.BlockSpec(a.shape, fix)
    nx = xbc.shape[1]
    return _carrier_call(
        body, comm, n_chunks,
        in_specs=[pl.BlockSpec((CHUNK, d_inner), rev), pl.BlockSpec((CHUNK, d_inner), rev),
                  pl.BlockSpec((CHUNK, d_inner), rev), pl.BlockSpec((CHUNK, LANE), lambda c: (n_chunks - 1 - c, dt_blk)),
                  pl.BlockSpec((CHUNK, nx), rev),
                  pl.BlockSpec((1, N_GROUPS, D_STATE, GROUP_W), lambda c: (n_chunks - 1 - c, 0, 0, 0)),
                  cspec(par), cspec(dexp), cspec(nw), cspec(ind_t), cspec(itile_c), cspec(trit_c),
                  cspec(tril_c), cspec(bmask_c)],
        out_specs=[pl.BlockSpec((CHUNK, nz), rev), pl.BlockSpec((CHUNK, nx), rev),
                   pl.BlockSpec((1, d_inner), fix), pl.BlockSpec((8, LANE), fix)],
        out_shape=[SDS((L, nz), BF16), SDS((L, nx), F32), SDS((1, d_inner), F32), SDS((8, LANE), F32)],
        scratch_shapes=[pltpu.VMEM((N_GROUPS, D_STATE, GROUP_W), F32), pltpu.VMEM((CHUNK, d_inner), F32)],
        name=name, args=(dyn, yf, zx, zx, xbc, states, par, dexp, nw, ind_t, itile_c, trit_c, tril_c, bmask_c))


def _adamw(w, m, v, g, name):
    R, C = w.shape
    tr = R
    for cand in (256, 128, 64, 32, 16, 8):
        if R % cand == 0:
            tr = cand
            break

    def body(w_ref, m_ref, v_ref, g_ref, d_ref, mo_ref, vo_ref):
        gv = g_ref[...]
        mn = ADAM_B1 * m_ref[...] + (1.0 - ADAM_B1) * gv
        vn = ADAM_B2 * v_ref[...] + (1.0 - ADAM_B2) * (gv * gv)
        m_hat = mn / (1.0 - ADAM_B1 ** ADAM_STEP)
        v_hat = vn / (1.0 - ADAM_B2 ** ADAM_STEP)
        d_ref[...] = -ADAM_LR * (m_hat / (jnp.sqrt(v_hat) + ADAM_EPS) + ADAM_WD * w_ref[...])
        mo_ref[...] = mn
        vo_ref[...] = vn

    blk = pl.BlockSpec((tr, C), lambda i: (i, 0))
    return pl.pallas_call(
        body, grid=(R // tr,), in_specs=[blk] * 4, out_specs=[blk] * 3, out_shape=[SDS((R, C), F32)] * 3,
        name=name, compiler_params=_cp("parallel"))(w, m, v, g)


def _sum_slots(parts, name):
    n, R, C = parts.shape
    tr = 128 if R % 128 == 0 else R

    def body(p_ref, o_ref):
        acc = p_ref[0]
        for k in range(1, n):
            acc = acc + p_ref[k]
        o_ref[...] = acc

    return pl.pallas_call(
        body, grid=(R // tr,), in_specs=[pl.BlockSpec((n, tr, C), lambda i: (0, i, 0))],
        out_specs=pl.BlockSpec((tr, C), lambda i: (i, 0)), out_shape=SDS((R, C), F32),
        name=name, compiler_params=_cp("parallel"))(parts)


def _add_core_halves(where, g, r, name):
    _, n, a, b = g.shape
    ta = a // 2

    def body(w_ref, g_ref, r_ref, o_ref):
        o_ref[...] = (g_ref[...].astype(F32) + r_ref[...].astype(F32)).astype(BF16)

    blk = lambda f: pl.BlockSpec((None, None, ta, b), f)
    mine = lambda s, l, w: (s, l, 0, 0)
    return pl.pallas_call(
        body, grid_spec=pltpu.PrefetchScalarGridSpec(
            num_scalar_prefetch=1, grid=(4, n),
            in_specs=[blk(lambda s, l, w: (s, l, w[1], 0)), blk(mine)], out_specs=blk(mine)),
        out_shape=SDS((4, n, ta, b), BF16), name=name,
        compiler_params=_cp("parallel", "parallel"))(where, g, r)


def _sum_shard(where, g, r, rr, into, layer, n_layers, name):
    _, _, a, b = g.shape
    ta = a // 2

    def body(w_ref, g_ref, r_ref, rr_ref, *refs):
        f = lambda v: v.astype(F32)
        refs[-1][...] = (((f(g_ref[...]) + f(r_ref[...])) + f(rr_ref[0])) + f(rr_ref[1])) + f(rr_ref[2])

    more = [] if into is None else [into]
    return pl.pallas_call(
        body, grid_spec=pltpu.PrefetchScalarGridSpec(
            num_scalar_prefetch=1, grid=(1,),
            in_specs=[pl.BlockSpec((None, None, ta, b), lambda l, w: (w[0], 0, w[1], 0)),
                      pl.BlockSpec((None, None, ta, b), lambda l, w: (w[0], 0, 0, 0)),
                      pl.BlockSpec((3, None, ta, b), lambda l, w: (0, 0, 0, 0))] + [ANY] * len(more),
            out_specs=pl.BlockSpec((None, ta, b), lambda l, w: (layer, w[1], 0))),
        out_shape=SDS((n_layers, a, b), F32), name=name, input_output_aliases={4: 0} if more else {},
        compiler_params=_cp("arbitrary"))(where, g, r, rr, *more)


def _me():
    return lax.axis_index("x"), lax.axis_index("y"), lax.axis_index("c")


def _chip_peers(x, y):
    return [(1 - x, y), (x, 1 - y), (1 - x, 1 - y)]


def _rcopy(src, dst, send_sems, recv_sems, k, to):
    return pltpu.make_async_remote_copy(src_ref=src, dst_ref=dst, send_sem=send_sems.at[k], recv_sem=recv_sems.at[k],
                                        device_id=to, device_id_type=MESH)


def _row_half(ref, c, lead=()):
    a = ref.shape[len(lead) + 1]
    return ref.at[(*lead, slice(None), pl.ds(c * (a // 2), a // 2))]


class _Comm(NamedTuple):
    ins: list
    out_shapes: list
    n_sems: int
    start: Callable
    finish: Callable


def _sem_scratch(comm):
    return [pltpu.SemaphoreType.DMA((comm.n_sems,)), pltpu.SemaphoreType.DMA((comm.n_sems,))]


def _run_comm(comm, name):
    n_in, n_out = len(comm.ins), len(comm.out_shapes)

    def body(*refs):
        ins, outs, sems = refs[:n_in], refs[n_in:n_in + n_out], refs[n_in + n_out:]
        comm.start(ins, outs, *sems)
        comm.finish(ins, outs, *sems)

    return pl.pallas_call(body, in_specs=[ANY] * n_in, out_specs=[ANY] * n_out, out_shape=comm.out_shapes,
                          scratch_shapes=_sem_scratch(comm), name=name)(*comm.ins)


def _carrier_call(compute, comm, n_steps, in_specs, out_specs, out_shape, scratch_shapes, name, args):
    if comm is None:
        return pl.pallas_call(compute, grid=(n_steps,), in_specs=in_specs, out_specs=out_specs, out_shape=out_shape,
                              scratch_shapes=scratch_shapes, name=name, compiler_params=_cp("arbitrary"))(*args), []
    n_in, n_out, n_scr = len(in_specs), len(out_specs), len(scratch_shapes)
    n_ci, n_co = len(comm.ins), len(comm.out_shapes)

    def body(*refs):
        ins, cins = refs[:n_in], refs[n_in:n_in + n_ci]
        o = n_in + n_ci
        outs, couts = refs[o:o + n_out], refs[o + n_out:o + n_out + n_co]
        s = o + n_out + n_co
        scratch, sems = refs[s:s + n_scr], refs[s + n_scr:]

        @pl.when(pl.program_id(0) == 0)
        def _():
            comm.start(cins, couts, *sems)

        compute(*ins, *outs, *scratch)

        @pl.when(pl.program_id(0) == n_steps - 1)
        def _():
            comm.finish(cins, couts, *sems)

    res = pl.pallas_call(
        body, grid=(n_steps,), in_specs=list(in_specs) + [ANY] * n_ci, out_specs=list(out_specs) + [ANY] * n_co,
        out_shape=list(out_shape) + list(comm.out_shapes), scratch_shapes=list(scratch_shapes) + _sem_scratch(comm),
        name=name, compiler_params=_cp("arbitrary"))(*args, *comm.ins)
    return res[:n_out], res[n_out:]


def _allgather_plan(mine, small=None):
    n = len(mine)
    ins = list(mine) + ([] if small is None else [small])
    out_shapes = [SDS((4,) + m.shape, BF16) for m in mine] + ([] if small is None else [SDS((4,) + small.shape, F32)])
    sem = lambda t, k: 7 * t + k

    def first_copies(ins_r, outs_r, send, recv):
        x, y, c = _me()
        q = 2 * x + y
        cps = []
        for j, chip in enumerate(_chip_peers(x, y)):
            for t in range(n):
                cps.append(_rcopy(_row_half(ins_r[t], c), _row_half(outs_r[t], c, (q,)), send, recv, sem(t, j),
                                  (*chip, c)))
            if small is not None:
                cps.append(_rcopy(ins_r[n], outs_r[n].at[q], send, recv, sem(n, j), (*chip, c)))
        for t in range(len(ins)):
            cps.append(_rcopy(ins_r[t], outs_r[t].at[q], send, recv, sem(t, 6), (x, y, 1 - c)))
        return cps

    def start(ins_r, outs_r, send, recv):
        for cp in first_copies(ins_r, outs_r, send, recv):
            cp.start()

    def finish(ins_r, outs_r, send, recv):
        x, y, c = _me()
        sib = (x, y, 1 - c)
        chips = _chip_peers(x, y)
        passed = []
        for j, (px, py) in enumerate(chips):
            for t in range(n):
                blk = _row_half(outs_r[t], c, (2 * px + py,))
                _rcopy(blk, blk, send, recv, sem(t, j), sib).wait_recv()
                cp = _rcopy(blk, blk, send, recv, sem(t, 3 + j), sib)
                cp.start()
                passed.append(cp)
        for j, (px, py) in enumerate(chips):
            for t in range(n):
                blk = _row_half(outs_r[t], 1 - c, (2 * px + py,))
                _rcopy(blk, blk, send, recv, sem(t, 3 + j), sib).wait_recv()
            if small is not None:
                sblk = outs_r[n].at[2 * px + py]
                _rcopy(sblk, sblk, send, recv, sem(n, j), sib).wait_recv()
        for t in range(len(ins)):
            own = outs_r[t].at[2 * x + y]
            _rcopy(own, own, send, recv, sem(t, 6), sib).wait_recv()
        for cp in first_copies(ins_r, outs_r, send, recv) + passed:
            cp.wait_send()

    return _Comm(ins, out_shapes, 7 * len(ins), start, finish)


def _grads_to_sibling_plan(grads):
    n = len(grads)

    def copies(ins_r, outs_r, send, recv):
        x, y, c = _me()
        return [_rcopy(_row_half(ins_r[t], 1 - c, (slice(None),)), outs_r[t], send, recv, t, (x, y, 1 - c))
                for t in range(n)]

    def start(*a):
        for cp in copies(*a):
            cp.start()

    def finish(*a):
        for cp in copies(*a):
            cp.wait()

    out_shapes = [SDS((4, g.shape[1], g.shape[2] // 2, g.shape[3]), BF16) for g in grads]
    return _Comm(list(grads), out_shapes, n, start, finish)


def _grads_to_chips_plan(psums):
    n = len(psums)

    def copies(ins_r, outs_r, send, recv):
        x, y, c = _me()
        return [_rcopy(ins_r[t].at[2 * px + py], outs_r[t].at[j], send, recv, 3 * t + j, (px, py, c))
                for j, (px, py) in enumerate(_chip_peers(x, y)) for t in range(n)]

    def start(*a):
        for cp in copies(*a):
            cp.start()

    def finish(*a):
        for cp in copies(*a):
            cp.wait()

    return _Comm(list(psums), [SDS((3,) + p.shape[1:], BF16) for p in psums], 3 * n, start, finish)


def _swap_halves(sums):
    n = len(sums)

    def body(*refs):
        out_refs = refs[n:2 * n]
        send_sems, recv_sems = refs[2 * n:]
        x, y, c = _me()
        sib = (x, y, 1 - c)
        cps = [_rcopy(_row_half(out_refs[t], c), _row_half(out_refs[t], c), send_sems, recv_sems, t, sib)
               for t in range(n)]
        for cp in cps:
            cp.start()
        for t in range(n):
            other = _row_half(out_refs[t], 1 - c)
            _rcopy(other, other, send_sems, recv_sems, t, sib).wait_recv()
        for cp in cps:
            cp.wait_send()

    return pl.pallas_call(
        body, in_specs=[ANY] * n, out_specs=[ANY] * n, out_shape=[SDS(s.shape, F32) for s in sums],
        input_output_aliases={t: t for t in range(n)},
        scratch_shapes=[pltpu.SemaphoreType.DMA((n,)), pltpu.SemaphoreType.DMA((n,))],
        name="swap_halves")(*sums)


def _allgather_small(part):
    def body(p_ref, out_ref, send_sems, recv_sems, local_sem):
        x, y, c = _me()
        me = 4 * x + 2 * y + c
        own = pltpu.make_async_copy(p_ref, out_ref.at[me], local_sem.at[0])
        own.start()
        sends = []
        for k in range(1, 8):
            fx, fy, fc = (k >> 2) & 1, (k >> 1) & 1, k & 1
            to = (x ^ fx, y ^ fy, c ^ fc)
            sends.append(_rcopy(p_ref, out_ref.at[me], send_sems, recv_sems, k - 1, to))
        for cp in sends:
            cp.start()
        for k in range(1, 8):
            slot = out_ref.at[me ^ k]
            _rcopy(slot, slot, send_sems, recv_sems, k - 1, (x, y, c)).wait_recv()
        for cp in sends:
            cp.wait_send()
        own.wait()

    return pl.pallas_call(
        body, in_specs=[ANY], out_specs=ANY, out_shape=SDS((8,) + part.shape, F32),
        scratch_shapes=[pltpu.SemaphoreType.DMA((7,)), pltpu.SemaphoreType.DMA((7,)), pltpu.SemaphoreType.DMA((1,))],
        name="allgather_small")(part)


BIG = (("ssd_w_in", 2), ("ssd_w_out", 1), ("sc_w_in", 2), ("sc_w_out", 1), ("ffn_w_up", 2), ("ffn_w_down", 1))


def _to_shards(full, axis):
    A, B = full.shape
    if axis == 2:
        return full.reshape(A, 4, B // 4).transpose(1, 0, 2)
    return full.reshape(4, A // 4, B)


def _from_shards(shards, axis):
    _, a, b = shards.shape
    if axis == 2:
        return shards.transpose(1, 0, 2).reshape(a, 4 * b)
    return shards.reshape(4 * a, b)


def _interleave(w, parts):
    lead, n = w.shape[:-1], w.shape[-1]
    return w.reshape(*lead, parts, n // (parts * LANE), LANE).swapaxes(-2, -3).reshape(*lead, n)


def _deinterleave(w, parts):
    lead, n = w.shape[:-1], w.shape[-1]
    return w.reshape(*lead, n // (parts * LANE), parts, LANE).swapaxes(-2, -3).reshape(*lead, n)


def _pack_rows(vectors, width, row_multiple):
    flat = jnp.concatenate(vectors, axis=-1)
    n = flat.shape[-1]
    unit = width * row_multiple
    total = -(-n // unit) * unit
    flat = jnp.pad(flat, [(0, 0)] * (flat.ndim - 1) + [(0, total - n)])
    return flat.reshape(*flat.shape[:-1], total // width, width)


def _unpack(flat, shapes):
    out, off = [], 0
    for s in shapes:
        n = int(np.prod(s))
        out.append(flat[..., off:off + n].reshape(*flat.shape[:-1], *s))
        off += n
    return out


def _memo(fn):
    cache = {}

    def wrapped(k):
        if k not in cache:
            cache[k] = fn(k)
        return cache[k]

    return wrapped


def _row(v):
    return v.reshape(1, -1)


def _pad_rows(w, rows=8):
    return jnp.pad(w, ((0, rows - w.shape[0]), (0, 0)))


def _ffn_fwd(x, g_pre, g_post, w_up, cw, cb, w_down, tag, comm_up=None, comm_down=None):
    (up, hn), got_up = _ffn_up(x, g_pre, w_up, "ffn_up" + tag, comm_up)
    a = _ffn_mid_fwd(up, cw, cb, "ffn_mid_fwd" + tag)
    (f, x_new), got_down = _matmul_norm_res(a, w_down, x, g_post, "ffn_down" + tag, comm_down)
    return x_new, (x, hn, up, a, f), got_up, got_down


def _ffn_bwd(dx, saved, g_pre, g_post, w_up, cw, cb, w_down, tag):
    x, hn, up, a, f = saved
    df, dg_post = _postnorm_bwd(dx, f, g_post, "ffn_post_bwd" + tag)
    da = _matmul_nt(df, w_down, BF16, "ffn_down_dx" + tag)
    dw_down = _matmul_tn(a, df, "ffn_down_dw" + tag)
    dup, stats = _ffn_mid_bwd(da, up, cw, cb, "ffn_mid_bwd" + tag)
    dx_in, dg_pre = _ffn_up_dx(dup, w_up, x, g_pre, dx, "ffn_up_dx" + tag)
    dw_up = _ffn_up_dw(hn, dup, "ffn_up_dw" + tag)
    return dx_in, dict(g_pre=dg_pre, g_post=dg_post, w_up=dw_up, w_down=dw_down, cw=stats[0:3], cb=stats[3])


def _sc_fwd(x, g_pre, g_post, w_in, cw, w_out, tag, comm_in=None, comm_out=None):
    (bcv, hn), got_in = _norm_matmul(x, g_pre, w_in, BF16, "sc_in" + tag, comm_in)
    q = _sc_mid_fwd(bcv, cw, "sc_mid_fwd" + tag)
    (m, x_new), got_out = _matmul_norm_res(q, w_out, x, g_post, "sc_out" + tag, comm_out)
    return x_new, (x, hn, bcv, q, m), got_in, got_out


def _sc_bwd(dx, saved, g_pre, g_post, w_in, cw, w_out, tag):
    x, hn, bcv, q, m = saved
    dm, dg_post = _postnorm_bwd(dx, m, g_post, "sc_post_bwd" + tag)
    dq = _matmul_nt(dm, w_out, BF16, "sc_out_dx" + tag)
    dw_out = _matmul_tn(q, dm, "sc_out_dw" + tag)
    dbcv, stats = _sc_mid_bwd(dq, bcv, cw, "sc_mid_bwd" + tag)
    (dx_in, dg_pre), _ = _matmul_nt_prenorm_bwd(dbcv, w_in, x, g_pre, dx, "sc_in_dx" + tag)
    dw_in = _matmul_tn(hn, dbcv, "sc_in_dw" + tag)
    return dx_in, dict(g_pre=dg_pre, g_post=dg_post, w_in=dw_in, w_out=dw_out, cw=stats[0:3])


def _ssd_fwd(x, g_pre, g_post, w_in, cw, cb, par, dexp, nw, w_out, consts, comm, tag, comm_in=None):
    d_inner = dexp.shape[1]
    (zx, hn), got_in = _norm_matmul(x, g_pre, w_in, F32, "ssd_in" + tag, comm_in)
    xbc = _ssd_conv_fwd(zx, cw, cb, d_inner, cw.shape[1], "ssd_conv_fwd" + tag)
    (yn, yf, states), got = _ssd_scan_fwd(zx, xbc, par, dexp, nw, consts, comm, "ssd_scan_fwd" + tag)
    if callable(w_out):
        w_out = w_out(got_in)
    (m, x_new), _ = _matmul_norm_res(yn, w_out, x, g_post, "ssd_out" + tag)
    return x_new, (x, hn, zx, xbc, yn, yf, states, m), got


def _ssd_bwd(dx, saved, g_pre, g_post, w_in, cw, cb, par, dexp, nw, w_out, consts, comm, tag, comm_dx=None):
    x, hn, zx, xbc, yn, yf, states, m = saved
    d_inner = dexp.shape[1]
    dm, dg_post = _postnorm_bwd(dx, m, g_post, "ssd_post_bwd" + tag)
    dyn = _matmul_nt(dm, w_out, F32, "ssd_out_dx" + tag)
    dw_out = _matmul_tn(yn, dm, "ssd_out_dw" + tag)
    (dzx, dxbc, dnw, dpar), got = _ssd_scan_bwd(dyn, yf, zx, xbc, states, par, dexp, nw, consts, comm(dw_out),
                                                "ssd_scan_bwd" + tag)
    dzx, stats = _ssd_conv_bwd(dxbc, zx, cw, cb, d_inner, dzx, "ssd_conv_bwd" + tag)
    dw_in = _matmul_tn(hn, dzx, "ssd_in_dw" + tag)
    (dx_in, dg_pre), got_dx = _matmul_nt_prenorm_bwd(dzx, w_in, x, g_pre, dx, "ssd_in_dx" + tag,
                                                     None if comm_dx is None else comm_dx(dw_in))
    n_heads = d_inner // HEAD_DIM
    grads = dict(g_pre=dg_pre, g_post=dg_post, w_in=dw_in, w_out=dw_out, cw=stats[0:4], cb=stats[4],
                 dt_bias=dpar[0, :n_heads], a_log=dpar[1, :n_heads], d=dpar[2, :n_heads], nw=dnw[0])
    return dx_in, grads, got, got_dx


def kernel(x, mix_pre_g, mix_post_g, ffn_pre_g, ffn_post_g, ssd_w_in, ssd_conv_w, ssd_conv_b, ssd_dt_bias, ssd_A_log, ssd_D, ssd_norm_w, ssd_w_out, sc_w_in, sc_conv_w, sc_w_out, ffn_w_up, ffn_conv_w, ffn_conv_b, ffn_w_down, loss_target, m_mix_pre_g, m_mix_post_g, m_ffn_pre_g, m_ffn_post_g, m_ssd_w_in, m_ssd_conv_w, m_ssd_conv_b, m_ssd_dt_bias, m_ssd_A_log, m_ssd_D, m_ssd_norm_w, m_ssd_w_out, m_sc_w_in, m_sc_conv_w, m_sc_w_out, m_ffn_w_up, m_ffn_conv_w, m_ffn_conv_b, m_ffn_w_down, v_mix_pre_g, v_mix_post_g, v_ffn_pre_g, v_ffn_post_g, v_ssd_w_in, v_ssd_conv_w, v_ssd_conv_b, v_ssd_dt_bias, v_ssd_A_log, v_ssd_D, v_ssd_norm_w, v_ssd_w_out, v_sc_w_in, v_sc_conv_w, v_sc_w_out, v_ffn_w_up, v_ffn_conv_w, v_ffn_conv_b, v_ffn_w_down):
    names = ["mix_pre_g", "mix_post_g", "ffn_pre_g", "ffn_post_g", "ssd_w_in", "ssd_conv_w", "ssd_conv_b",
             "ssd_dt_bias", "ssd_A_log", "ssd_D", "ssd_norm_w", "ssd_w_out", "sc_w_in", "sc_conv_w", "sc_w_out",
             "ffn_w_up", "ffn_conv_w", "ffn_conv_b", "ffn_w_down"]
    env = locals()
    wts = {n: env[n] for n in names}
    mom = {n: env["m_" + n] for n in names}
    var = {n: env["v_" + n] for n in names}

    depth, d_model = mix_pre_g.shape
    n_ssd, n_heads = ssd_dt_bias.shape
    n_sc = sc_conv_w.shape[0]
    d_inner = n_heads * HEAD_DIM
    conv_dim = d_inner + 2 * N_GROUPS * D_STATE
    ssd_in_dim = d_inner + conv_dim + n_heads
    ssd_in_pad = d_inner + conv_dim + LANE
    q_chip = 2 * lax.axis_index("x") + lax.axis_index("y")
    core = lax.axis_index("c")

    assert depth == 4 and n_ssd == 2 and n_sc == 2, "the exchange schedule is written for this trunk"
    ssd_items = lambda j: [("ssd_w_in", j), ("ssd_w_out", j)]
    sc_items = lambda j: [("sc_w_in", j), ("sc_w_out", j)]
    ffn_items = lambda i: [("ffn_w_up", i), ("ffn_w_down", i)]
    gather_first = [("ssd_w_in", 0)]
    gather_in_ssd_in = {0: [("ssd_w_out", 0)]}
    gather_in_scan = {0: ffn_items(0) + sc_items(0), 2: ffn_items(2) + sc_items(1)}
    gather_in_ffn = {0: ([("ffn_w_up", 1)], [("ffn_w_down", 1)]), 1: ([("ssd_w_in", 1)], []),
                     2: ([("ffn_w_up", 3)], [("ffn_w_down", 3)])}
    gather_in_sc = {1: ([("ssd_w_out", 1)], [])}
    reduce_in_scan = {2: ffn_items(3) + sc_items(1) + ffn_items(2),
                      0: ssd_items(1) + ffn_items(1) + sc_items(0) + ffn_items(0) + [("ssd_w_out", 0)]}
    reduce_in_dx = {0: [("ssd_w_in", 0)]}
    axis_of = dict(BIG)
    gathered = {}

    def gather_plan(items, small=None):
        mine = [wts[n][layer:layer + 1].astype(BF16) for n, layer in items]
        return _allgather_plan(mine, small) if items else None

    def gather_done(items, results):
        for item, buf in zip(items, results):
            gathered[item] = buf[:, 0]

    def full(n, layer):
        return _from_shards(gathered[(n, layer)], axis_of[n])

    conv_names = ["ssd_conv_w", "sc_conv_w", "ffn_conv_w"]
    conv_shapes = [wts[n].shape for n in conv_names]
    small_mine = _pack_rows([wts[n].reshape(-1) for n in conv_names], LANE, 8)
    *results, small_all = _run_comm(gather_plan(gather_first, small_mine), "allgather_first")
    gather_done(gather_first, results)
    conv_full = {}
    for n, f, s in zip(conv_names, _unpack(small_all.reshape(4, -1), conv_shapes), conv_shapes):
        conv_full[n] = f.transpose(1, 2, 0, 3).reshape(s[0], s[1], 4 * s[2])

    consts = _scan_constants(n_heads)

    def ssd_args(j):
        par = jnp.zeros((8, LANE), F32).at[0, :n_heads].set(ssd_dt_bias[j]).at[1, :n_heads].set(ssd_A_log[j])
        dexp = jnp.repeat(ssd_D[j], HEAD_DIM).reshape(1, d_inner)
        w_in = jnp.pad(full("ssd_w_in", j), ((0, 0), (0, ssd_in_pad - ssd_in_dim)))
        return (w_in, _pad_rows(conv_full["ssd_conv_w"][j]), _row(ssd_conv_b[j]), par, dexp, _row(ssd_norm_w[j]))

    def sc_args(j):
        return (_interleave(full("sc_w_in", j), 3), _pad_rows(conv_full["sc_conv_w"][j]), full("sc_w_out", j))

    def ffn_args(i):
        return (gathered[("ffn_w_up", i)], _pad_rows(conv_full["ffn_conv_w"][i]), _row(ffn_conv_b[i]),
                full("ffn_w_down", i))

    ssd_args, sc_args, ffn_args = _memo(ssd_args), _memo(sc_args), _memo(ffn_args)

    h = x[0]
    saved = []
    for i in range(depth):
        j = i // 2
        gp, gq = _row(mix_pre_g[i]), _row(mix_post_g[i])
        if i % 2 == 0:
            items_in = gather_in_ssd_in.get(i, [])

            def w_out_when_here(got_in, items_in=items_in, j=j):
                gather_done(items_in, got_in)
                return full("ssd_w_out", j)

            h, sv, results = _ssd_fwd(h, gp, gq, *ssd_args(j), w_out_when_here, consts, gather_plan(gather_in_scan[i]),
                                      tag="", comm_in=gather_plan(items_in))
            gather_done(gather_in_scan[i], results)
        else:
            items_in, items_out = gather_in_sc.get(i, ([], []))
            h, sv, got_in, got_out = _sc_fwd(h, gp, gq, *sc_args(j), tag="", comm_in=gather_plan(items_in),
                                             comm_out=gather_plan(items_out))
            gather_done(items_in, got_in)
            gather_done(items_out, got_out)
        items_up, items_down = gather_in_ffn.get(i, ([], []))
        h, sv2, got_up, got_down = _ffn_fwd(h, _row(ffn_pre_g[i]), _row(ffn_post_g[i]), *ffn_args(i), tag="",
                                            comm_up=gather_plan(items_up), comm_down=gather_plan(items_down))
        gather_done(items_up, got_up)
        gather_done(items_down, got_down)
        saved.append((sv, sv2))
    dh, loss_part = _loss_head(h, loss_target[0], "loss_head")

    mix_grads, ffn_grads = [None] * depth, [None] * depth
    where = jnp.stack([q_chip, core]).astype(jnp.int32)

    def shard_grad(n, layer):
        if n == "ffn_w_up":
            g = ffn_grads[layer]["w_up"]
        elif n == "ffn_w_down":
            g = _to_shards(ffn_grads[layer]["w_down"], 1)
        elif n == "ssd_w_in":
            g = _to_shards(early[(n, layer)][:, :ssd_in_dim], 2)
        elif n == "ssd_w_out":
            g = _to_shards(early[(n, layer)], 1)
        elif n == "sc_w_in":
            g = _to_shards(_deinterleave(mix_grads[2 * layer + 1]["w_in"], 3), 2)
        else:
            g = _to_shards(mix_grads[2 * layer + 1]["w_out"], 1)
        return g[:, None]

    def reduce_begin(items, tag):
        by_shard = [shard_grad(n, layer) for n, layer in items]
        from_sib = _run_comm(_grads_to_sibling_plan(by_shard), "grads_to_sibling" + tag)
        chip_sums = [_add_core_halves(where, g, r, "add_core_halves_%s%d" % item)
                     for item, g, r in zip(items, by_shard, from_sib)]
        return by_shard, from_sib, _grads_to_chips_plan(chip_sums)

    sums = {}
    early = {}

    def riding(items, tag):
        by_shard, from_sib, plan = reduce_begin(items, tag)

        def arrived(from_chips):
            for (n, layer), g, r, rr in zip(items, by_shard, from_sib, from_chips):
                sums[n] = _sum_shard(where, g, r, rr, sums.get(n), layer, wts[n].shape[0],
                                     "sum_shard_%s%d" % (n, layer))

        return plan, arrived

    for i in reversed(range(depth)):
        j = i // 2
        sv, sv2 = saved[i]
        dh, ffn_grads[i] = _ffn_bwd(dh, sv2, _row(ffn_pre_g[i]), _row(ffn_post_g[i]), *ffn_args(i), tag="")
        gp, gq = _row(mix_pre_g[i]), _row(mix_post_g[i])
        if i % 2 == 0:
            then = {}

            def in_scan(dw_out, i=i, j=j, then=then):
                early[("ssd_w_out", j)] = dw_out
                plan, then["scan"] = riding(reduce_in_scan[i], "_%d" % i)
                return plan

            def in_dx(dw_in, i=i, j=j, then=then):
                early[("ssd_w_in", j)] = dw_in
                plan, then["dx"] = riding(reduce_in_dx[i], "_dx%d" % i) if i in reduce_in_dx else (None, None)
                return plan

            dh, mix_grads[i], got, got_dx = _ssd_bwd(dh, sv, gp, gq, *ssd_args(j), full("ssd_w_out", j), consts,
                                                     in_scan, tag="", comm_dx=in_dx)
            then["scan"](got)
            if then["dx"] is not None:
                then["dx"](got_dx)
        else:
            w_in, scw, w_out = sc_args(j)
            dh, mix_grads[i] = _sc_bwd(dh, sv, gp, gq, w_in, scw, w_out, tag="")
    grad_x = dh[None]
    big_grads = dict(zip([n for n, _ in BIG], _swap_halves([sums[n] for n, _ in BIG])))
    ssd_l = [mix_grads[i] for i in range(0, depth, 2)]
    sc_l = [mix_grads[i] for i in range(1, depth, 2)]
    stack = lambda layers, k: jnp.stack([g[k] for g in layers])

    small_names = ["mix_pre_g", "mix_post_g", "ffn_pre_g", "ffn_post_g", "ssd_conv_w", "ssd_conv_b", "ssd_dt_bias",
                   "ssd_A_log", "ssd_D", "ssd_norm_w", "sc_conv_w", "ffn_conv_w", "ffn_conv_b"]
    small_local = {
        "mix_pre_g": jnp.concatenate([g["g_pre"] for g in mix_grads]),
        "mix_post_g": jnp.concatenate([g["g_post"] for g in mix_grads]),
        "ffn_pre_g": jnp.concatenate([g["g_pre"] for g in ffn_grads]),
        "ffn_post_g": jnp.concatenate([g["g_post"] for g in ffn_grads]),
        "ssd_conv_w": stack(ssd_l, "cw"), "ssd_conv_b": stack(ssd_l, "cb"), "ssd_dt_bias": stack(ssd_l, "dt_bias"),
        "ssd_A_log": stack(ssd_l, "a_log"), "ssd_D": stack(ssd_l, "d"), "ssd_norm_w": stack(ssd_l, "nw"),
        "sc_conv_w": stack(sc_l, "cw"), "ffn_conv_w": stack(ffn_grads, "cw"), "ffn_conv_b": stack(ffn_grads, "cb"),
    }
    small_full_shapes = [small_local[n].shape for n in small_names]
    spack = _pack_rows([small_local[n].reshape(-1) for n in small_names] + [loss_part.reshape(-1)], LANE, 8)
    stotal = _sum_slots(_allgather_small(spack), "sum_small").reshape(-1)
    small_grads = dict(zip(small_names, _unpack(stotal, small_full_shapes)))
    loss = stotal[sum(int(np.prod(s)) for s in small_full_shapes)]
    for n in conv_names:
        width = wts[n].shape[-1]
        small_grads[n] = lax.dynamic_slice_in_dim(small_grads[n], q_chip * width, width, axis=2)

    grads, delta, new_m, new_v = {}, {}, {}, {}
    for n, _ in BIG:
        s = wts[n].shape
        two_d = lambda a: a.reshape(-1, s[-1])
        grads[n] = big_grads[n]
        d, mn, vn = _adamw(two_d(wts[n]), two_d(mom[n]), two_d(var[n]), two_d(grads[n]), "adamw_" + n)
        delta[n], new_m[n], new_v[n] = d.reshape(s), mn.reshape(s), vn.reshape(s)
    small_shapes = [wts[n].shape for n in small_names]
    pk = lambda d: _pack_rows([d[n].reshape(-1) for n in small_names], LANE, 8)
    for n in small_names:
        grads[n] = small_grads[n].reshape(wts[n].shape)
    d, mn, vn = _adamw(pk(wts), pk(mom), pk(var), pk(grads), "adamw_small")
    for out, packed in ((delta, d), (new_m, mn), (new_v, vn)):
        out.update(zip(small_names, _unpack(packed.reshape(-1), small_shapes)))

    return (loss, grad_x, *[grads[n] for n in names], *[delta[n] for n in names], *[new_m[n] for n in names],
            *[new_v[n] for n in names])
```

```python
from typing import Callable, NamedTuple

import jax
import jax.numpy as jnp
import numpy as np
from jax import lax
from jax.experimental import pallas as pl
from jax.experimental.pallas import tpu as pltpu

F32 = jnp.float32
BF16 = jnp.bfloat16
SDS = jax.ShapeDtypeStruct
MESH = pl.DeviceIdType.MESH
ANY = pl.BlockSpec(memory_space=pl.ANY)

EPS = 1e-6
CHUNK = 64
HEAD_DIM = 64
N_GROUPS = 8
D_STATE = 128
HEADS_PER_GROUP = 4
GROUP_W = HEADS_PER_GROUP * HEAD_DIM
LANE = 128
ROW_TILE = 128
HALO = 8
VMEM_LIMIT = 56 * 1024 * 1024

ADAM_LR = 0.001
ADAM_B1 = 0.9
ADAM_B2 = 0.999
ADAM_EPS = 1e-08
ADAM_WD = 0.01
ADAM_STEP = 10

NT = (((1,), (1,)), ((), ()))
TN = (((0,), (0,)), ((), ()))


def _cp(*sem):
    return pltpu.CompilerParams(dimension_semantics=sem or None, vmem_limit_bytes=VMEM_LIMIT)


def _sigmoid(x):
    return 1.0 / (1.0 + jnp.exp(-x))


def _dsilu(x, s):
    return s * (1.0 + x * (1.0 - s))


def _rsq(x):
    return lax.rsqrt(jnp.mean(x * x, axis=-1, keepdims=True) + EPS)


MM_ROWS = 256


def _norm_matmul(x, g, w, out_dtype, name, comm=None):
    L, D = x.shape
    N = w.shape[1]
    tm = min(MM_ROWS, L)

    def body(x_ref, g_ref, w_ref, o_ref, hn_ref):
        xv = x_ref[...]
        hn = (xv * _rsq(xv) * g_ref[...]).astype(BF16)
        hn_ref[...] = hn
        o_ref[...] = jnp.dot(hn, w_ref[...], preferred_element_type=F32).astype(out_dtype)

    row = lambda i: (i, 0)
    fix = lambda i: (0, 0)
    return _carrier_call(
        body, comm, L // tm,
        in_specs=[pl.BlockSpec((tm, D), row), pl.BlockSpec((1, D), fix), pl.BlockSpec((D, N), fix)],
        out_specs=[pl.BlockSpec((tm, N), row), pl.BlockSpec((tm, D), row)],
        out_shape=[SDS((L, N), out_dtype), SDS((L, D), BF16)], scratch_shapes=[], name=name, args=(x, g, w))


def _matmul_norm_res(a, w, x, g, name, comm=None):
    L, K = a.shape
    D = w.shape[1]
    tm = min(MM_ROWS, L)

    def body(a_ref, w_ref, x_ref, g_ref, m_ref, xo_ref):
        m = jnp.dot(a_ref[...], w_ref[...], preferred_element_type=F32)
        m_ref[...] = m
        xo_ref[...] = x_ref[...] + m * _rsq(m) * g_ref[...]

    row = lambda i: (i, 0)
    fix = lambda i: (0, 0)
    return _carrier_call(
        body, comm, L // tm,
        in_specs=[pl.BlockSpec((tm, K), row), pl.BlockSpec((K, D), fix), pl.BlockSpec((tm, D), row),
                  pl.BlockSpec((1, D), fix)],
        out_specs=[pl.BlockSpec((tm, D), row), pl.BlockSpec((tm, D), row)],
        out_shape=[SDS((L, D), F32), SDS((L, D), F32)], scratch_shapes=[], name=name, args=(a, w, x, g))


def _postnorm_bwd_matmul_nt(dx, m, g, w, out_dtype, name):
    L, D = dx.shape
    K = w.shape[0]
    tm = min(MM_ROWS, L)

    def body(dx_ref, m_ref, g_ref, w_ref, da_ref, dm_ref, dg_ref):
        @pl.when(pl.program_id(0) == 0)
        def _():
            dg_ref[...] = jnp.zeros_like(dg_ref)

        m = m_ref[...]
        dxv = dx_ref[...]
        r = _rsq(m)
        mh = m * r
        dg_ref[...] += jnp.sum(dxv * mh, axis=0, keepdims=True)
        dyg = dxv * g_ref[...]
        dm = (r * (dyg - mh * jnp.mean(dyg * mh, axis=-1, keepdims=True))).astype(BF16)
        dm_ref[...] = dm
        da_ref[...] = lax.dot_general(dm, w_ref[...], NT, preferred_element_type=F32).astype(out_dtype)

    row = lambda i: (i, 0)
    fix = lambda i: (0, 0)
    return pl.pallas_call(
        body, grid=(L // tm,),
        in_specs=[pl.BlockSpec((tm, D), row), pl.BlockSpec((tm, D), row), pl.BlockSpec((1, D), fix),
                  pl.BlockSpec((K, D), fix)],
        out_specs=[pl.BlockSpec((tm, K), row), pl.BlockSpec((tm, D), row), pl.BlockSpec((1, D), fix)],
        out_shape=[SDS((L, K), out_dtype), SDS((L, D), BF16), SDS((1, D), F32)],
        name=name, compiler_params=_cp("arbitrary"))(dx, m, g, w)


DW_ACC_BYTES = 13 * 512 * 1024


def _dw_tiles(ka, n):
    ta = ka if ka <= 1024 else ka // 2
    fits = [d for d in range(LANE, n + 1, LANE) if n % d == 0 and ta * d * 4 <= DW_ACC_BYTES]
    return ta, max(fits)


def _matmul_tn(a, b, name):
    L, Ka = a.shape
    N = b.shape[1]
    ta, tn = _dw_tiles(Ka, N)
    tl = min(512, L)
    n_l = L // tl

    def body(a_ref, b_ref, o_ref, acc_ref):
        l = pl.program_id(2)

        @pl.when(l == 0)
        def _():
            acc_ref[...] = jnp.zeros_like(acc_ref)

        acc_ref[...] += lax.dot_general(a_ref[...], b_ref[...], TN, preferred_element_type=F32)

        @pl.when(l == n_l - 1)
        def _():
            o_ref[...] = acc_ref[...].astype(BF16)

    return pl.pallas_call(
        body, grid=(Ka // ta, N // tn, n_l),
        in_specs=[pl.BlockSpec((tl, ta), lambda i, j, l: (l, i)), pl.BlockSpec((tl, tn), lambda i, j, l: (l, j))],
        out_specs=pl.BlockSpec((ta, tn), lambda i, j, l: (i, j)),
        out_shape=SDS((Ka, N), BF16),
        scratch_shapes=[pltpu.VMEM((ta, tn), F32)],
        name=name, compiler_params=_cp("parallel", "parallel", "arbitrary"))(a, b)


def _ffn_up(x, g, w4, name, comm=None):
    L, D = x.shape
    b = w4.shape[2]
    tm = min(MM_ROWS, L)

    def body(x_ref, g_ref, w_ref, o_ref, hn_ref):
        xv = x_ref[...]
        hn = (xv * _rsq(xv) * g_ref[...]).astype(BF16)
        hn_ref[...] = hn
        for q in range(4):
            o_ref[q // 2, :, (q % 2) * b:(q % 2 + 1) * b] = jnp.dot(hn, w_ref[q], preferred_element_type=F32).astype(BF16)

    return _carrier_call(
        body, comm, L // tm,
        in_specs=[pl.BlockSpec((tm, D), lambda i: (i, 0)), pl.BlockSpec((1, D), lambda i: (0, 0)),
                  pl.BlockSpec((4, D, b), lambda i: (0, 0, 0))],
        out_specs=[pl.BlockSpec((2, tm, 2 * b), lambda i: (0, i, 0)), pl.BlockSpec((tm, D), lambda i: (i, 0))],
        out_shape=[SDS((2, L, 2 * b), BF16), SDS((L, D), BF16)], scratch_shapes=[], name=name, args=(x, g, w4))


def _ffn_up_dx(dup, w4, x, g, dres, name):
    _, L, _ = dup.shape
    _, D, b = w4.shape
    tm = min(MM_ROWS, L)

    def body(dy_ref, w_ref, x_ref, g_ref, dres_ref, dx_ref, dg_ref):
        @pl.when(pl.program_id(0) == 0)
        def _():
            dg_ref[...] = jnp.zeros_like(dg_ref)

        dh = jnp.zeros((tm, D), F32)
        for q in range(4):
            dh = dh + lax.dot_general(dy_ref[q // 2, :, (q % 2) * b:(q % 2 + 1) * b], w_ref[q], NT,
                                      preferred_element_type=F32)
        xv = x_ref[...]
        r = _rsq(xv)
        xh = xv * r
        dg_ref[...] += jnp.sum(dh * xh, axis=0, keepdims=True)
        dyg = dh * g_ref[...]
        dx_ref[...] = dres_ref[...] + r * (dyg - xh * jnp.mean(dyg * xh, axis=-1, keepdims=True))

    row = lambda i: (i, 0)
    fix = lambda i: (0, 0)
    return pl.pallas_call(
        body, grid=(L // tm,),
        in_specs=[pl.BlockSpec((2, tm, 2 * b), lambda i: (0, i, 0)), pl.BlockSpec((4, D, b), lambda i: (0, 0, 0)),
                  pl.BlockSpec((tm, D), row), pl.BlockSpec((1, D), fix), pl.BlockSpec((tm, D), row)],
        out_specs=[pl.BlockSpec((tm, D), row), pl.BlockSpec((1, D), fix)],
        out_shape=[SDS((L, D), F32), SDS((1, D), F32)],
        name=name, compiler_params=_cp("arbitrary"))(dup, w4, x, g, dres)


def _ffn_up_dw(hn, dup, name):
    L, D = hn.shape
    b = dup.shape[2] // 2
    tl = min(512, L)
    n_l = L // tl

    def body(a_ref, b_ref, o_ref, acc_ref):
        l = pl.program_id(1)

        @pl.when(l == 0)
        def _():
            acc_ref[...] = jnp.zeros_like(acc_ref)

        acc_ref[...] += lax.dot_general(a_ref[...], b_ref[...], TN, preferred_element_type=F32)

        @pl.when(l == n_l - 1)
        def _():
            o_ref[...] = acc_ref[...].astype(BF16)

    return pl.pallas_call(
        body, grid=(4, n_l),
        in_specs=[pl.BlockSpec((tl, D), lambda q, l: (l, 0)),
                  pl.BlockSpec((None, tl, b), lambda q, l: (q // 2, l, q % 2))],
        out_specs=pl.BlockSpec((None, D, b), lambda q, l: (q, 0, 0)),
        out_shape=SDS((4, D, b), BF16),
        scratch_shapes=[pltpu.VMEM((D, b), F32)],
        name=name, compiler_params=_cp("parallel", "arbitrary"))(hn, dup)


def _matmul_nt_prenorm_bwd(dy, w, x, g, dres, name, comm=None):
    L, N = dy.shape
    D = w.shape[0]
    tm = min(MM_ROWS, L)

    def body(dy_ref, w_ref, x_ref, g_ref, dres_ref, dx_ref, dg_ref):
        @pl.when(pl.program_id(0) == 0)
        def _():
            dg_ref[...] = jnp.zeros_like(dg_ref)

        dh = lax.dot_general(dy_ref[...], w_ref[...], NT, preferred_element_type=F32)
        xv = x_ref[...]
        r = _rsq(xv)
        xh = xv * r
        dg_ref[...] += jnp.sum(dh * xh, axis=0, keepdims=True)
        dyg = dh * g_ref[...]
        dx_ref[...] = dres_ref[...] + r * (dyg - xh * jnp.mean(dyg * xh, axis=-1, keepdims=True))

    row = lambda i: (i, 0)
    fix = lambda i: (0, 0)
    return _carrier_call(
        body, comm, L // tm,
        in_specs=[pl.BlockSpec((tm, N), row), pl.BlockSpec((D, N), fix), pl.BlockSpec((tm, D), row),
                  pl.BlockSpec((1, D), fix), pl.BlockSpec((tm, D), row)],
        out_specs=[pl.BlockSpec((tm, D), row), pl.BlockSpec((1, D), fix)],
        out_shape=[SDS((L, D), F32), SDS((1, D), F32)], scratch_shapes=[], name=name, args=(dy, w, x, g, dres))


def _loss_head(y, t, name):
    L, D = y.shape
    tm = min(512, L)

    def body(y_ref, t_ref, dy_ref, loss_ref):
        @pl.when(pl.program_id(0) == 0)
        def _():
            loss_ref[...] = jnp.zeros_like(loss_ref)

        e = y_ref[...] - t_ref[...]
        dy_ref[...] = e * (1.0 / D)
        s = jnp.sum(jnp.sum(e * e, axis=1, keepdims=True), axis=0, keepdims=True)
        loss_ref[...] += s * (0.5 / D)

    row = lambda i: (i, 0)
    return pl.pallas_call(
        body, grid=(L // tm,),
        in_specs=[pl.BlockSpec((tm, D), row), pl.BlockSpec((tm, D), row)],
        out_specs=[pl.BlockSpec((tm, D), row), pl.BlockSpec((1, 1), lambda i: (0, 0))],
        out_shape=[SDS((L, D), F32), SDS((1, 1), F32)],
        name=name, compiler_params=_cp("arbitrary"))(y, t)


def _tile_rows(ref):
    return HALO * (4 // jnp.dtype(ref.dtype).itemsize)


def _prev_rows(ref, r0, i, cols):
    n = _tile_rows(ref)
    p0 = pl.multiple_of(jnp.maximum(r0 - n, 0), n)
    return jnp.where(i > 0, ref[pl.ds(p0, n), cols].astype(F32)[n - HALO:], 0.0)


def _next_rows(ref, r0, i, n_tiles, cols):
    n = _tile_rows(ref)
    n0 = pl.multiple_of(jnp.minimum(r0 + ROW_TILE, n_tiles * ROW_TILE - n), n)
    return jnp.where(i < n_tiles - 1, ref[pl.ds(n0, n), cols].astype(F32)[:HALO], 0.0)


def _rows_f32(ref, rows, cols):
    return ref[rows, cols].astype(F32)


def _back(ext, s):
    return pltpu.roll(ext, s, axis=0)[HALO:HALO + ROW_TILE]


def _fwd(ext, s):
    n = ext.shape[0]
    return pltpu.roll(ext, n - s, axis=0)[:ROW_TILE]


def _store_rows(ref, rows):
    ref[...] = jnp.zeros_like(ref)
    for k, v in enumerate(rows):
        ref[k:k + 1, :] = v


def _strip_call(body, L, n_strips, ins, outs, name):
    def spec(rows, width, off):
        if off is None:
            return pl.BlockSpec((rows, width), lambda j: (0, 0))
        return pl.BlockSpec((rows, width), lambda j: (0, j + off))

    return pl.pallas_call(
        body, grid=(n_strips,),
        in_specs=[spec(a.shape[0], w, off) for a, w, off in ins],
        out_specs=[spec(s.shape[0], w, off) for s, w, off in outs],
        out_shape=[s for s, _, _ in outs],
        name=name, compiler_params=_cp("parallel"))(*[a for a, _, _ in ins])


def _ffn_mid_fwd(up, cw, cb, name):
    _, L, C = up.shape
    n_tiles = L // ROW_TILE
    al = slice(None)

    def body(up_ref, cw_ref, cb_ref, a_ref):
        w0, w1, w2 = cw_ref[0:1, :], cw_ref[1:2, :], cw_ref[2:3, :]
        b = cb_ref[...]
        gate_ref, val_ref = up_ref.at[0], up_ref.at[1]

        def step(i, c):
            r0 = pl.multiple_of(i * ROW_TILE, ROW_TILE)
            rows = pl.ds(r0, ROW_TILE)
            gp = _rows_f32(gate_ref, rows, al)
            ext = jnp.concatenate([_prev_rows(gate_ref, r0, i, al), gp], axis=0)
            gate = gp * w2 + _back(ext, 1) * w1 + _back(ext, 2) * w0 + b
            a_ref[rows, :] = (gate * _sigmoid(gate) * _rows_f32(val_ref, rows, al)).astype(BF16)
            return c

        lax.fori_loop(0, n_tiles, step, 0)

    strip = lambda rows: pl.BlockSpec((rows, LANE), lambda j: (0, j))
    return pl.pallas_call(
        body, grid=(C // LANE,),
        in_specs=[pl.BlockSpec((2, L, LANE), lambda j: (0, 0, j)), strip(cw.shape[0]), strip(1)],
        out_specs=strip(L), out_shape=SDS((L, C), BF16), name=name, compiler_params=_cp("parallel"))(up, cw, cb)


def _ffn_mid_bwd(da, up, cw, cb, name):
    L, C = da.shape
    n_tiles = L // ROW_TILE
    al = slice(None)

    def body(da_ref, up_ref, cw_ref, cb_ref, dup_ref, st_ref):
        w0, w1, w2 = cw_ref[0:1, :], cw_ref[1:2, :], cw_ref[2:3, :]
        b = cb_ref[...]
        gate_ref, val_ref = up_ref.at[0], up_ref.at[1]

        def step(i, c):
            r0 = pl.multiple_of(i * ROW_TILE, ROW_TILE)
            rows = pl.ds(r0, ROW_TILE)
            gp = _rows_f32(gate_ref, rows, al)
            gpe = jnp.concatenate([_prev_rows(gate_ref, r0, i, al), gp, _next_rows(gate_ref, r0, i, n_tiles, al)],
                                  axis=0)
            g1, g2 = pltpu.roll(gpe, 1, axis=0), pltpu.roll(gpe, 2, axis=0)
            gate = (gpe * w2 + g1 * w1 + g2 * w0 + b)[HALO:]
            sg = _sigmoid(gate)
            da_e = jnp.concatenate([_rows_f32(da_ref, rows, al), _next_rows(da_ref, r0, i, n_tiles, al)], axis=0)
            val_e = jnp.concatenate([_rows_f32(val_ref, rows, al), _next_rows(val_ref, r0, i, n_tiles, al)], axis=0)
            dgate_e = da_e * val_e * _dsilu(gate, sg)
            dgate = dgate_e[:ROW_TILE]
            dgp = dgate * w2 + _fwd(dgate_e, 1) * w1 + _fwd(dgate_e, 2) * w0
            dup_ref[0, rows, :] = dgp.astype(BF16)
            dup_ref[1, rows, :] = (da_e * gate * sg)[:ROW_TILE].astype(BF16)
            s = lambda v: jnp.sum(v, axis=0, keepdims=True)
            t = slice(HALO, HALO + ROW_TILE)
            return (c[0] + s(dgate * g2[t]), c[1] + s(dgate * g1[t]), c[2] + s(dgate * gp), c[3] + s(dgate))

        z = jnp.zeros((1, LANE), F32)
        _store_rows(st_ref, lax.fori_loop(0, n_tiles, step, (z, z, z, z)))

    strip = lambda rows: pl.BlockSpec((rows, LANE), lambda j: (0, j))
    pair = pl.BlockSpec((2, L, LANE), lambda j: (0, 0, j))
    return pl.pallas_call(
        body, grid=(C // LANE,), in_specs=[strip(L), pair, strip(cw.shape[0]), strip(1)],
        out_specs=[pair, strip(8)], out_shape=[SDS((2, L, C), BF16), SDS((8, C), F32)],
        name=name, compiler_params=_cp("parallel"))(da, up, cw, cb)


def _sc_mid_fwd(bcv, cw, name):
    L = bcv.shape[0]
    C = bcv.shape[1] // 3
    n_tiles = L // ROW_TILE
    s0, s1, s2 = slice(0, LANE), slice(LANE, 2 * LANE), slice(2 * LANE, 3 * LANE)

    def body(x_ref, cw_ref, q_ref):
        w0, w1, w2 = cw_ref[0:1, :], cw_ref[1:2, :], cw_ref[2:3, :]

        def step(i, c):
            r0 = pl.multiple_of(i * ROW_TILE, ROW_TILE)
            rows = pl.ds(r0, ROW_TILE)
            p = _rows_f32(x_ref, rows, s1) * _rows_f32(x_ref, rows, s2)
            ext = jnp.concatenate([_prev_rows(x_ref, r0, i, s1) * _prev_rows(x_ref, r0, i, s2), p], axis=0)
            u = p * w2 + _back(ext, 1) * w1 + _back(ext, 2) * w0
            q_ref[rows, :] = (_rows_f32(x_ref, rows, s0) * u).astype(BF16)
            return c

        lax.fori_loop(0, n_tiles, step, 0)

    return _strip_call(body, L, C // LANE, [(bcv, 3 * LANE, 0), (cw, LANE, 0)],
                       [(SDS((L, C), BF16), LANE, 0)], name)[0]


def _sc_mid_bwd(dq, bcv, cw, name):
    L, C = dq.shape
    n_tiles = L // ROW_TILE
    s0, s1, s2, al = slice(0, LANE), slice(LANE, 2 * LANE), slice(2 * LANE, 3 * LANE), slice(None)

    def body(dq_ref, x_ref, cw_ref, dx_ref, st_ref):
        w0, w1, w2 = cw_ref[0:1, :], cw_ref[1:2, :], cw_ref[2:3, :]

        def step(i, c):
            r0 = pl.multiple_of(i * ROW_TILE, ROW_TILE)
            rows = pl.ds(r0, ROW_TILE)
            gb, gc, v = _rows_f32(x_ref, rows, s0), _rows_f32(x_ref, rows, s1), _rows_f32(x_ref, rows, s2)
            dq_v = _rows_f32(dq_ref, rows, al)
            p = gc * v
            pext = jnp.concatenate([_prev_rows(x_ref, r0, i, s1) * _prev_rows(x_ref, r0, i, s2), p], axis=0)
            p1, p2 = _back(pext, 1), _back(pext, 2)
            u = p * w2 + p1 * w1 + p2 * w0
            du = dq_v * gb
            du_n = _next_rows(dq_ref, r0, i, n_tiles, al) * _next_rows(x_ref, r0, i, n_tiles, s0)
            ext = jnp.concatenate([du, du_n], axis=0)
            dp = du * w2 + _fwd(ext, 1) * w1 + _fwd(ext, 2) * w0
            dx_ref[rows, s0] = (dq_v * u).astype(BF16)
            dx_ref[rows, s1] = (dp * v).astype(BF16)
            dx_ref[rows, s2] = (dp * gc).astype(BF16)
            s = lambda t: jnp.sum(t, axis=0, keepdims=True)
            return (c[0] + s(du * p2), c[1] + s(du * p1), c[2] + s(du * p))

        z = jnp.zeros((1, LANE), F32)
        _store_rows(st_ref, lax.fori_loop(0, n_tiles, step, (z, z, z)))

    return _strip_call(body, L, C // LANE, [(dq, LANE, 0), (bcv, 3 * LANE, 0), (cw, LANE, 0)],
                       [(SDS((L, 3 * C), BF16), 3 * LANE, 0), (SDS((8, C), F32), LANE, 0)], name)


def _ssd_conv_fwd(zx, cw, cb, col0, C, name):
    L = zx.shape[0]
    n_tiles = L // ROW_TILE
    al = slice(None)

    def body(x_ref, cw_ref, cb_ref, o_ref):
        w0, w1, w2, w3 = cw_ref[0:1, :], cw_ref[1:2, :], cw_ref[2:3, :], cw_ref[3:4, :]
        b = cb_ref[...]

        def step(i, c):
            r0 = pl.multiple_of(i * ROW_TILE, ROW_TILE)
            rows = pl.ds(r0, ROW_TILE)
            xv = x_ref[rows, :]
            ext = jnp.concatenate([_prev_rows(x_ref, r0, i, al), xv], axis=0)
            cv = xv * w3 + _back(ext, 1) * w2 + _back(ext, 2) * w1 + _back(ext, 3) * w0 + b
            o_ref[rows, :] = cv * _sigmoid(cv)
            return c

        lax.fori_loop(0, n_tiles, step, 0)

    return _strip_call(body, L, C // LANE, [(zx, LANE, col0 // LANE), (cw, LANE, 0), (cb, LANE, 0)],
                       [(SDS((L, C), F32), LANE, 0)], name)[0]


def _ssd_conv_bwd(dxbc, zx, cw, cb, col0, dzx, name):
    L, C = dxbc.shape
    n_tiles = L // ROW_TILE
    al = slice(None)

    def body(d_ref, x_ref, cw_ref, cb_ref, dzx_in_ref, o_ref, st_ref):
        w0, w1, w2, w3 = cw_ref[0:1, :], cw_ref[1:2, :], cw_ref[2:3, :], cw_ref[3:4, :]
        b = cb_ref[...]

        def step(i, c):
            r0 = pl.multiple_of(i * ROW_TILE, ROW_TILE)
            rows = pl.ds(r0, ROW_TILE)
            xv = x_ref[rows, :]
            xe = jnp.concatenate([_prev_rows(x_ref, r0, i, al), xv, _next_rows(x_ref, r0, i, n_tiles, al)], axis=0)
            x1, x2, x3 = pltpu.roll(xe, 1, axis=0), pltpu.roll(xe, 2, axis=0), pltpu.roll(xe, 3, axis=0)
            cv = (xe * w3 + x1 * w2 + x2 * w1 + x3 * w0 + b)[HALO:]
            de = jnp.concatenate([d_ref[rows, :], _next_rows(d_ref, r0, i, n_tiles, al)], axis=0)
            dc_ext = de * _dsilu(cv, _sigmoid(cv))
            dc = dc_ext[:ROW_TILE]
            o_ref[rows, :] = (dc * w3 + _fwd(dc_ext, 1) * w2 + _fwd(dc_ext, 2) * w1 + _fwd(dc_ext, 3) * w0).astype(BF16)
            s = lambda t: jnp.sum(t, axis=0, keepdims=True)
            t = slice(HALO, HALO + ROW_TILE)
            return (c[0] + s(dc * x3[t]), c[1] + s(dc * x2[t]), c[2] + s(dc * x1[t]), c[3] + s(dc * xv), c[4] + s(dc))

        z = jnp.zeros((1, LANE), F32)
        _store_rows(st_ref, lax.fori_loop(0, n_tiles, step, (z, z, z, z, z)))

    strip = lambda rows, off=0: pl.BlockSpec((rows, LANE), lambda j: (0, j + off))
    shifted = strip(L, col0 // LANE)
    return pl.pallas_call(
        body, grid=(C // LANE,), in_specs=[strip(L), shifted, strip(cw.shape[0]), strip(1), ANY],
        out_specs=[shifted, strip(8)], out_shape=[SDS(dzx.shape, dzx.dtype), SDS((8, C), F32)],
        input_output_aliases={4: 0}, name=name, compiler_params=_cp("parallel"))(dxbc, zx, cw, cb, dzx)


def _scan_constants(n_heads):
    hw = n_heads * HEAD_DIM
    col = np.arange(hw)
    ind = (col[None, :] // HEAD_DIM == np.arange(LANE)[:, None]).astype(np.float32)
    gcol = np.arange(GROUP_W)
    itile = (gcol[None, :] % CHUNK == np.arange(CHUNK)[:, None]).astype(np.float32)
    trit = (gcol[None, :] % CHUNK <= np.arange(CHUNK)[:, None]).astype(np.float32)
    tril = np.tril(np.ones((CHUNK, CHUNK), np.float32))
    bmask = (gcol[:, None] // HEAD_DIM == gcol[None, :] // HEAD_DIM).astype(np.float32)
    return (jnp.asarray(ind.T.copy(), BF16), jnp.asarray(itile), jnp.asarray(trit), jnp.asarray(tril, BF16),
            jnp.asarray(bmask))


def _softplus(x):
    return jnp.maximum(x, 0.0) + jnp.log(1.0 + jnp.exp(-jnp.abs(x)))


def _split3(x):
    hi = x.astype(BF16)
    r1 = x - hi.astype(F32)
    mid = r1.astype(BF16)
    return hi, mid, (r1 - mid.astype(F32)).astype(BF16)


def _dot_sel(x, sel, dims=None):
    if dims is None:
        mm = lambda p: jnp.dot(p, sel, preferred_element_type=F32)
    else:
        mm = lambda p: lax.dot_general(sel, p, dims, preferred_element_type=F32)
    hi, mid, lo = _split3(x)
    return (mm(lo) + mm(mid)) + mm(hi)


SEL_X = (((1,), (0,)), ((), ()))


def _head_lanes(v, g):
    h0 = HEADS_PER_GROUP * g
    return jnp.concatenate([jnp.broadcast_to(v[:, h0 + r:h0 + r + 1], (v.shape[0], HEAD_DIM))
                            for r in range(HEADS_PER_GROUP)], axis=1)


def _group_terms(g, dt, cs, cst, xbc_ref, trit, bmask, d_inner):
    gl = slice(g * GROUP_W, (g + 1) * GROUP_W)
    h0 = HEADS_PER_GROUP * g
    csl = _head_lanes(cs, g)
    dtx = _head_lanes(dt, g)
    rr = jnp.concatenate([cst[h0 + r:h0 + r + 1, :] for r in range(HEADS_PER_GROUP)], axis=1)
    lm = jnp.exp(jnp.where(trit > 0.0, csl - rr, -jnp.inf))
    xs = xbc_ref[:, gl]
    b = xbc_ref[:, d_inner + g * D_STATE: d_inner + (g + 1) * D_STATE]
    c = xbc_ref[:, d_inner + (N_GROUPS + g) * D_STATE: d_inner + (N_GROUPS + g + 1) * D_STATE]
    u = xs * dtx
    bb, cb = b.astype(BF16), c.astype(BF16)
    btile = jnp.concatenate([bb] * HEADS_PER_GROUP, axis=0)
    cbt = lax.dot_general(cb, btile, NT, preferred_element_type=F32)
    m = cbt * lm
    ub = u.astype(BF16)
    bdu = jnp.where(bmask > 0.0, jnp.concatenate([ub] * HEADS_PER_GROUP, axis=0), jnp.zeros((), BF16))
    c_last = csl[CHUNK - 1:CHUNK, :]
    return dict(gl=gl, csl=csl, dtx=dtx, lm=lm, xs=xs, bb=bb, cb=cb, u=u, btile=btile, m=m, bdu=bdu,
                e=jnp.exp(csl), dec=jnp.exp(c_last - csl), e_last=jnp.exp(c_last))


def _ssd_scan_fwd(zx, xbc, par, dexp, nw, consts, comm, name):
    L = xbc.shape[0]
    d_inner = dexp.shape[1]
    n_chunks = L // CHUNK
    dt_blk = zx.shape[1] // LANE - 1
    ind_t, itile_c, trit_c, tril_c, bmask_c = consts

    def body(xbc_ref, z_ref, dtr_ref, par_ref, dexp_ref, nw_ref, trit_ref, tril_ref, bmask_ref,
             yn_ref, yf_ref, st_out_ref, st_ref):
        @pl.when(pl.program_id(0) == 0)
        def _():
            st_ref[...] = jnp.zeros_like(st_ref)

        dt = _softplus(dtr_ref[...] + par_ref[0:1, :])
        a_head = -jnp.exp(par_ref[1:2, :])
        cs = _dot_sel(dt * a_head, tril_ref[...], SEL_X)
        cst = cs.T
        trit, bmask = trit_ref[...], bmask_ref[...]
        for g in range(N_GROUPS):
            t = _group_terms(g, dt, cs, cst, xbc_ref, trit, bmask, d_inner)
            p = st_ref[g]
            st_out_ref[0, g] = p
            y = jnp.dot(t["m"].astype(BF16), t["bdu"], preferred_element_type=F32)
            y = y + jnp.dot(t["cb"], p.astype(BF16), preferred_element_type=F32) * t["e"]
            st_new = lax.dot_general(t["bb"], (t["u"] * t["dec"]).astype(BF16), TN, preferred_element_type=F32)
            st_ref[g] = p * t["e_last"] + st_new
            yf_ref[:, t["gl"]] = y + t["xs"] * dexp_ref[:, t["gl"]]
        z = z_ref[...]
        y2 = yf_ref[...] * (z * _sigmoid(z))
        yn_ref[...] = (y2 * _rsq(y2) * nw_ref[...]).astype(BF16)

    row = lambda c: (c, 0)
    fix = lambda c: (0, 0)
    cspec = lambda a: pl.BlockSpec(a.shape, fix)
    return _carrier_call(
        body, comm, n_chunks,
        in_specs=[pl.BlockSpec((CHUNK, xbc.shape[1]), row), pl.BlockSpec((CHUNK, d_inner), row),
                  pl.BlockSpec((CHUNK, LANE), lambda c: (c, dt_blk)), cspec(par), cspec(dexp), cspec(nw),
                  cspec(trit_c), cspec(tril_c), cspec(bmask_c)],
        out_specs=[pl.BlockSpec((CHUNK, d_inner), row), pl.BlockSpec((CHUNK, d_inner), row),
                   pl.BlockSpec((1, N_GROUPS, D_STATE, GROUP_W), lambda c: (c, 0, 0, 0))],
        out_shape=[SDS((L, d_inner), BF16), SDS((L, d_inner), F32),
                   SDS((n_chunks, N_GROUPS, D_STATE, GROUP_W), F32)],
        scratch_shapes=[pltpu.VMEM((N_GROUPS, D_STATE, GROUP_W), F32)],
        name=name, args=(xbc, zx, zx, par, dexp, nw, trit_c, tril_c, bmask_c))


def _ssd_scan_bwd(dyn, yf, zx, xbc, states, par, dexp, nw, consts, comm, name):
    L = xbc.shape[0]
    d_inner = dexp.shape[1]
    n_chunks = L // CHUNK
    nz = zx.shape[1]
    dt_blk = nz // LANE - 1
    ind_t, itile_c, trit_c, tril_c, bmask_c = consts
    hslices = [slice(r * HEAD_DIM, (r + 1) * HEAD_DIM) for r in range(HEADS_PER_GROUP)]

    def body(dyn_ref, yf_ref, z_ref, dtr_ref, xbc_ref, st_in_ref, par_ref, dexp_ref, nw_ref, indt_ref,
             itile_ref, trit_ref, tril_ref, bmask_ref,
             dzx_ref, dxbc_ref, dnw_ref, dpar_ref, dq_ref, dyf_ref):
        @pl.when(pl.program_id(0) == 0)
        def _():
            dq_ref[...] = jnp.zeros_like(dq_ref)
            dnw_ref[...] = jnp.zeros_like(dnw_ref)
            dpar_ref[...] = jnp.zeros_like(dpar_ref)

        z, yfv, dynv = z_ref[...], yf_ref[...], dyn_ref[...]
        sz = _sigmoid(z)
        y2 = yfv * (z * sz)
        r = _rsq(y2)
        y2h = y2 * r
        dnw_ref[...] += jnp.sum(dynv * y2h, axis=0, keepdims=True)
        dyg = dynv * nw_ref[...]
        dy2 = r * (dyg - y2h * jnp.mean(dyg * y2h, axis=-1, keepdims=True))
        dzx_ref[:, 0:d_inner] = (dy2 * yfv * _dsilu(z, sz)).astype(BF16)
        dyf_ref[...] = dy2 * (z * sz)

        pre = dtr_ref[...] + par_ref[0:1, :]
        dt = _softplus(pre)
        a_head = -jnp.exp(par_ref[1:2, :])
        cs = _dot_sel(dt * a_head, tril_ref[...], SEL_X)
        cst = cs.T
        itile, trit, bmask = itile_ref[...], trit_ref[...], bmask_ref[...]
        dcs = jnp.zeros((CHUNK, LANE), F32)
        dcs_last = jnp.zeros((1, LANE), F32)
        ddt_u = jnp.zeros((CHUNK, LANE), F32)
        d_skip = jnp.zeros((1, LANE), F32)
        rsum = lambda v: jnp.sum(v, axis=0, keepdims=True)
        row8 = lax.broadcasted_iota(jnp.int32, (8, GROUP_W), 0)
        for g in range(N_GROUPS):
            t = _group_terms(g, dt, cs, cst, xbc_ref, trit, bmask, d_inner)
            gl, m, lm, u, bb, cb, e, dec, xs = (t[k] for k in ("gl", "m", "lm", "u", "bb", "cb", "e", "dec", "xs"))
            indt = indt_ref[gl, :]
            dy = dyf_ref[:, gl]
            dyb = dy.astype(BF16)
            p = st_in_ref[0, g]
            pb = p.astype(BF16)
            q = dq_ref[g]
            qb = q.astype(BF16)
            big = lax.dot_general(m.astype(BF16), dyb, TN, preferred_element_type=F32)
            du = jnp.zeros((CHUNK, GROUP_W), F32)
            for rh in range(HEADS_PER_GROUP):
                du = du + big[hslices[rh], :] * bmask[rh * HEAD_DIM:rh * HEAD_DIM + 1, :]
            dm = lax.dot_general(dyb, t["bdu"], NT, preferred_element_type=F32)
            w = dm * m
            dgt = (dm * lm).astype(BF16)
            dc = jnp.dot(dgt, t["btile"], preferred_element_type=F32)
            db_big = lax.dot_general(dgt, cb, TN, preferred_element_type=F32)
            db = db_big[hslices[0], :] + db_big[hslices[1], :] + db_big[hslices[2], :] + db_big[hslices[3], :]
            cp = jnp.dot(cb, pb, preferred_element_type=F32)
            dye = dy * e
            dyeb = dye.astype(BF16)
            dc = dc + lax.dot_general(dyeb, pb, NT, preferred_element_type=F32)
            dp = lax.dot_general(cb, dyeb, TN, preferred_element_type=F32)
            x2 = dye * cp
            bq = jnp.dot(bb, qb, preferred_element_type=F32)
            ud = u * dec
            du = du + bq * dec
            db = db + lax.dot_general(ud.astype(BF16), qb, NT, preferred_element_type=F32)
            x1 = bq * ud
            dq_ref[g] = dp + t["e_last"] * q
            x3 = rsum(q * p) * t["e_last"]
            red = _dot_sel(jnp.concatenate([w + x2 - x1, du * xs, itile * rsum(w)], axis=0), indt)
            dcs = dcs + red[0:CHUNK] - red[2 * CHUNK:3 * CHUNK]
            ddt_u = ddt_u + red[CHUNK:2 * CHUNK]
            tail = _dot_sel(jnp.where(row8 == 0, rsum(x1) + x3, jnp.where(row8 == 1, rsum(dy * xs), 0.0)), indt)
            dcs_last = dcs_last + tail[0:1]
            d_skip = d_skip + tail[1:2]
            dxbc_ref[:, gl] = du * t["dtx"] + dy * dexp_ref[:, gl]
            dxbc_ref[:, d_inner + g * D_STATE: d_inner + (g + 1) * D_STATE] = db
            dxbc_ref[:, d_inner + (N_GROUPS + g) * D_STATE: d_inner + (N_GROUPS + g + 1) * D_STATE] = dc
        last = lax.broadcasted_iota(jnp.int32, (CHUNK, LANE), 0) == CHUNK - 1
        dcs = dcs + jnp.where(last, dcs_last, 0.0)
        da = _dot_sel(dcs, tril_ref[...], TN)
        ddt = da * a_head + ddt_u
        heads = lax.broadcasted_iota(jnp.int32, (CHUNK, LANE), 1) < d_inner // HEAD_DIM
        ddt_raw = jnp.where(heads, ddt * _sigmoid(pre), 0.0)
        dzx_ref[:, nz - LANE:nz] = ddt_raw.astype(BF16)
        dpar_ref[0:1, :] += rsum(ddt_raw)
        dpar_ref[1:2, :] += rsum(da * dt) * a_head
        dpar_ref[2:3, :] += d_skip

    rev = lambda c: (n_chunks - 1 - c, 0)
    fix = lambda c: (0, 0)
    cspec = lambda a: pl.BlockSpec(a.shape, fix)
    nx = xbc.shape[1]
    return _carrier_call(
        body, comm, n_chunks,
        in_specs=[pl.BlockSpec((CHUNK, d_inner), rev), pl.BlockSpec((CHUNK, d_inner), rev),
                  pl.BlockSpec((CHUNK, d_inner), rev), pl.BlockSpec((CHUNK, LANE), lambda c: (n_chunks - 1 - c, dt_blk)),
                  pl.BlockSpec((CHUNK, nx), rev),
                  pl.BlockSpec((1, N_GROUPS, D_STATE, GROUP_W), lambda c: (n_chunks - 1 - c, 0, 0, 0)),
                  cspec(par), cspec(dexp), cspec(nw), cspec(ind_t), cspec(itile_c), cspec(trit_c),
                  cspec(tril_c), cspec(bmask_c)],
        out_specs=[pl.BlockSpec((CHUNK, nz), rev), pl.BlockSpec((CHUNK, nx), rev),
                   pl.BlockSpec((1, d_inner), fix), pl.BlockSpec((8, LANE), fix)],
        out_shape=[SDS((L, nz), BF16), SDS((L, nx), F32), SDS((1, d_inner), F32), SDS((8, LANE), F32)],
        scratch_shapes=[pltpu.VMEM((N_GROUPS, D_STATE, GROUP_W), F32), pltpu.VMEM((CHUNK, d_inner), F32)],
        name=name, args=(dyn, yf, zx, zx, xbc, states, par, dexp, nw, ind_t, itile_c, trit_c, tril_c, bmask_c))


def _adamw(w, m, v, g, name):
    R, C = w.shape
    tr = R
    for cand in (256, 128, 64, 32, 16, 8):
        if R % cand == 0:
            tr = cand
            break

    def body(w_ref, m_ref, v_ref, g_ref, d_ref, mo_ref, vo_ref):
        gv = g_ref[...]
        mn = ADAM_B1 * m_ref[...] + (1.0 - ADAM_B1) * gv
        vn = ADAM_B2 * v_ref[...] + (1.0 - ADAM_B2) * (gv * gv)
        m_hat = mn / (1.0 - ADAM_B1 ** ADAM_STEP)
        v_hat = vn / (1.0 - ADAM_B2 ** ADAM_STEP)
        d_ref[...] = -ADAM_LR * (m_hat / (jnp.sqrt(v_hat) + ADAM_EPS) + ADAM_WD * w_ref[...])
        mo_ref[...] = mn
        vo_ref[...] = vn

    blk = pl.BlockSpec((tr, C), lambda i: (i, 0))
    return pl.pallas_call(
        body, grid=(R // tr,), in_specs=[blk] * 4, out_specs=[blk] * 3, out_shape=[SDS((R, C), F32)] * 3,
        name=name, compiler_params=_cp("parallel"))(w, m, v, g)


def _sum_slots(parts, name):
    n, R, C = parts.shape
    tr = 128 if R % 128 == 0 else R

    def body(p_ref, o_ref):
        acc = p_ref[0]
        for k in range(1, n):
            acc = acc + p_ref[k]
        o_ref[...] = acc

    return pl.pallas_call(
        body, grid=(R // tr,), in_specs=[pl.BlockSpec((n, tr, C), lambda i: (0, i, 0))],
        out_specs=pl.BlockSpec((tr, C), lambda i: (i, 0)), out_shape=SDS((R, C), F32),
        name=name, compiler_params=_cp("parallel"))(parts)


def _add_core_halves(where, g, r, name):
    _, n, a, b = g.shape
    ta = a // 2

    def body(w_ref, g_ref, r_ref, o_ref):
        o_ref[...] = (g_ref[...].astype(F32) + r_ref[...].astype(F32)).astype(BF16)

    blk = lambda f: pl.BlockSpec((None, None, ta, b), f)
    mine = lambda s, l, w: (s, l, 0, 0)
    return pl.pallas_call(
        body, grid_spec=pltpu.PrefetchScalarGridSpec(
            num_scalar_prefetch=1, grid=(4, n),
            in_specs=[blk(lambda s, l, w: (s, l, w[1], 0)), blk(mine)], out_specs=blk(mine)),
        out_shape=SDS((4, n, ta, b), BF16), name=name,
        compiler_params=_cp("parallel", "parallel"))(where, g, r)


def _sum_shard(where, g, r, rr, into, layer, n_layers, name):
    _, _, a, b = g.shape
    ta = a // 2

    def body(w_ref, g_ref, r_ref, rr_ref, *refs):
        f = lambda v: v.astype(F32)
        refs[-1][...] = (((f(g_ref[...]) + f(r_ref[...])) + f(rr_ref[0])) + f(rr_ref[1])) + f(rr_ref[2])

    more = [] if into is None else [into]
    return pl.pallas_call(
        body, grid_spec=pltpu.PrefetchScalarGridSpec(
            num_scalar_prefetch=1, grid=(1,),
            in_specs=[pl.BlockSpec((None, None, ta, b), lambda l, w: (w[0], 0, w[1], 0)),
                      pl.BlockSpec((None, None, ta, b), lambda l, w: (w[0], 0, 0, 0)),
                      pl.BlockSpec((3, None, ta, b), lambda l, w: (0, 0, 0, 0))] + [ANY] * len(more),
            out_specs=pl.BlockSpec((None, ta, b), lambda l, w: (layer, w[1], 0))),
        out_shape=SDS((n_layers, a, b), F32), name=name, input_output_aliases={4: 0} if more else {},
        compiler_params=_cp("arbitrary"))(where, g, r, rr, *more)


def _me():
    return lax.axis_index("x"), lax.axis_index("y"), lax.axis_index("c")


def _chip_peers(x, y):
    return [(1 - x, y), (x, 1 - y), (1 - x, 1 - y)]


def _rcopy(src, dst, send_sems, recv_sems, k, to):
    return pltpu.make_async_remote_copy(src_ref=src, dst_ref=dst, send_sem=send_sems.at[k], recv_sem=recv_sems.at[k],
                                        device_id=to, device_id_type=MESH)


def _row_half(ref, c, lead=()):
    a = ref.shape[len(lead) + 1]
    return ref.at[(*lead, slice(None), pl.ds(c * (a // 2), a // 2))]


class _Comm(NamedTuple):
    ins: list
    out_shapes: list
    n_sems: int
    start: Callable
    finish: Callable


def _sem_scratch(comm):
    return [pltpu.SemaphoreType.DMA((comm.n_sems,)), pltpu.SemaphoreType.DMA((comm.n_sems,))]


def _run_comm(comm, name):
    n_in, n_out = len(comm.ins), len(comm.out_shapes)

    def body(*refs):
        ins, outs, sems = refs[:n_in], refs[n_in:n_in + n_out], refs[n_in + n_out:]
        comm.start(ins, outs, *sems)
        comm.finish(ins, outs, *sems)

    return pl.pallas_call(body, in_specs=[ANY] * n_in, out_specs=[ANY] * n_out, out_shape=comm.out_shapes,
                          scratch_shapes=_sem_scratch(comm), name=name)(*comm.ins)


def _carrier_call(compute, comm, n_steps, in_specs, out_specs, out_shape, scratch_shapes, name, args):
    if comm is None:
        return pl.pallas_call(compute, grid=(n_steps,), in_specs=in_specs, out_specs=out_specs, out_shape=out_shape,
                              scratch_shapes=scratch_shapes, name=name, compiler_params=_cp("arbitrary"))(*args), []
    n_in, n_out, n_scr = len(in_specs), len(out_specs), len(scratch_shapes)
    n_ci, n_co = len(comm.ins), len(comm.out_shapes)

    def body(*refs):
        ins, cins = refs[:n_in], refs[n_in:n_in + n_ci]
        o = n_in + n_ci
        outs, couts = refs[o:o + n_out], refs[o + n_out:o + n_out + n_co]
        s = o + n_out + n_co
        scratch, sems = refs[s:s + n_scr], refs[s + n_scr:]

        @pl.when(pl.program_id(0) == 0)
        def _():
            comm.start(cins, couts, *sems)

        compute(*ins, *outs, *scratch)

        @pl.when(pl.program_id(0) == n_steps - 1)
        def _():
            comm.finish(cins, couts, *sems)

    res = pl.pallas_call(
        body, grid=(n_steps,), in_specs=list(in_specs) + [ANY] * n_ci, out_specs=list(out_specs) + [ANY] * n_co,
        out_shape=list(out_shape) + list(comm.out_shapes), scratch_shapes=list(scratch_shapes) + _sem_scratch(comm),
        name=name, compiler_params=_cp("arbitrary"))(*args, *comm.ins)
    return res[:n_out], res[n_out:]


def _allgather_plan(mine, small=None):
    n = len(mine)
    ins = list(mine) + ([] if small is None else [small])
    out_shapes = [SDS((4,) + m.shape, BF16) for m in mine] + ([] if small is None else [SDS((4,) + small.shape, F32)])
    sem = lambda t, k: 7 * t + k

    def first_copies(ins_r, outs_r, send, recv):
        x, y, c = _me()
        q = 2 * x + y
        cps = []
        for j, chip in enumerate(_chip_peers(x, y)):
            for t in range(n):
                cps.append(_rcopy(_row_half(ins_r[t], c), _row_half(outs_r[t], c, (q,)), send, recv, sem(t, j),
                                  (*chip, c)))
            if small is not None:
                cps.append(_rcopy(ins_r[n], outs_r[n].at[q], send, recv, sem(n, j), (*chip, c)))
        for t in range(len(ins)):
            cps.append(_rcopy(ins_r[t], outs_r[t].at[q], send, recv, sem(t, 6), (x, y, 1 - c)))
        return cps

    def start(ins_r, outs_r, send, recv):
        for cp in first_copies(ins_r, outs_r, send, recv):
            cp.start()

    def finish(ins_r, outs_r, send, recv):
        x, y, c = _me()
        sib = (x, y, 1 - c)
        chips = _chip_peers(x, y)
        passed = []
        for j, (px, py) in enumerate(chips):
            for t in range(n):
                blk = _row_half(outs_r[t], c, (2 * px + py,))
                _rcopy(blk, blk, send, recv, sem(t, j), sib).wait_recv()
                cp = _rcopy(blk, blk, send, recv, sem(t, 3 + j), sib)
                cp.start()
                passed.append(cp)
        for j, (px, py) in enumerate(chips):
            for t in range(n):
                blk = _row_half(outs_r[t], 1 - c, (2 * px + py,))
                _rcopy(blk, blk, send, recv, sem(t, 3 + j), sib).wait_recv()
            if small is not None:
                sblk = outs_r[n].at[2 * px + py]
                _rcopy(sblk, sblk, send, recv, sem(n, j), sib).wait_recv()
        for t in range(len(ins)):
            own = outs_r[t].at[2 * x + y]
            _rcopy(own, own, send, recv, sem(t, 6), sib).wait_recv()
        for cp in first_copies(ins_r, outs_r, send, recv) + passed:
            cp.wait_send()

    return _Comm(ins, out_shapes, 7 * len(ins), start, finish)


def _grads_to_sibling_plan(grads):
    n = len(grads)

    def copies(ins_r, outs_r, send, recv):
        x, y, c = _me()
        return [_rcopy(_row_half(ins_r[t], 1 - c, (slice(None),)), outs_r[t], send, recv, t, (x, y, 1 - c))
                for t in range(n)]

    def start(*a):
        for cp in copies(*a):
            cp.start()

    def finish(*a):
        for cp in copies(*a):
            cp.wait()

    out_shapes = [SDS((4, g.shape[1], g.shape[2] // 2, g.shape[3]), BF16) for g in grads]
    return _Comm(list(grads), out_shapes, n, start, finish)


def _grads_to_chips_plan(psums):
    n = len(psums)

    def copies(ins_r, outs_r, send, recv):
        x, y, c = _me()
        return [_rcopy(ins_r[t].at[2 * px + py], outs_r[t].at[j], send, recv, 3 * t + j, (px, py, c))
                for j, (px, py) in enumerate(_chip_peers(x, y)) for t in range(n)]

    def start(*a):
        for cp in copies(*a):
            cp.start()

    def finish(*a):
        for cp in copies(*a):
            cp.wait()

    return _Comm(list(psums), [SDS((3,) + p.shape[1:], BF16) for p in psums], 3 * n, start, finish)


def _swap_halves(sums):
    n = len(sums)

    def body(*refs):
        out_refs = refs[n:2 * n]
        send_sems, recv_sems = refs[2 * n:]
        x, y, c = _me()
        sib = (x, y, 1 - c)
        cps = [_rcopy(_row_half(out_refs[t], c), _row_half(out_refs[t], c), send_sems, recv_sems, t, sib)
               for t in range(n)]
        for cp in cps:
            cp.start()
        for t in range(n):
            other = _row_half(out_refs[t], 1 - c)
            _rcopy(other, other, send_sems, recv_sems, t, sib).wait_recv()
        for cp in cps:
            cp.wait_send()

    return pl.pallas_call(
        body, in_specs=[ANY] * n, out_specs=[ANY] * n, out_shape=[SDS(s.shape, F32) for s in sums],
        input_output_aliases={t: t for t in range(n)},
        scratch_shapes=[pltpu.SemaphoreType.DMA((n,)), pltpu.SemaphoreType.DMA((n,))],
        name="swap_halves")(*sums)


def _allgather_small(part):
    def body(p_ref, out_ref, send_sems, recv_sems, local_sem):
        x, y, c = _me()
        me = 4 * x + 2 * y + c
        own = pltpu.make_async_copy(p_ref, out_ref.at[me], local_sem.at[0])
        own.start()
        sends = []
        for k in range(1, 8):
            fx, fy, fc = (k >> 2) & 1, (k >> 1) & 1, k & 1
            to = (x ^ fx, y ^ fy, c ^ fc)
            sends.append(_rcopy(p_ref, out_ref.at[me], send_sems, recv_sems, k - 1, to))
        for cp in sends:
            cp.start()
        for k in range(1, 8):
            slot = out_ref.at[me ^ k]
            _rcopy(slot, slot, send_sems, recv_sems, k - 1, (x, y, c)).wait_recv()
        for cp in sends:
            cp.wait_send()
        own.wait()

    return pl.pallas_call(
        body, in_specs=[ANY], out_specs=ANY, out_shape=SDS((8,) + part.shape, F32),
        scratch_shapes=[pltpu.SemaphoreType.DMA((7,)), pltpu.SemaphoreType.DMA((7,)), pltpu.SemaphoreType.DMA((1,))],
        name="allgather_small")(part)


BIG = (("ssd_w_in", 2), ("ssd_w_out", 1), ("sc_w_in", 2), ("sc_w_out", 1), ("ffn_w_up", 2), ("ffn_w_down", 1))


def _to_shards(full, axis):
    A, B = full.shape
    if axis == 2:
        return full.reshape(A, 4, B // 4).transpose(1, 0, 2)
    return full.reshape(4, A // 4, B)


def _from_shards(shards, axis):
    _, a, b = shards.shape
    if axis == 2:
        return shards.transpose(1, 0, 2).reshape(a, 4 * b)
    return shards.reshape(4 * a, b)


def _interleave(w, parts):
    lead, n = w.shape[:-1], w.shape[-1]
    return w.reshape(*lead, parts, n // (parts * LANE), LANE).swapaxes(-2, -3).reshape(*lead, n)


def _deinterleave(w, parts):
    lead, n = w.shape[:-1], w.shape[-1]
    return w.reshape(*lead, n // (parts * LANE), parts, LANE).swapaxes(-2, -3).reshape(*lead, n)


def _pack_rows(vectors, width, row_multiple):
    flat = jnp.concatenate(vectors, axis=-1)
    n = flat.shape[-1]
    unit = width * row_multiple
    total = -(-n // unit) * unit
    flat = jnp.pad(flat, [(0, 0)] * (flat.ndim - 1) + [(0, total - n)])
    return flat.reshape(*flat.shape[:-1], total // width, width)


def _unpack(flat, shapes):
    out, off = [], 0
    for s in shapes:
        n = int(np.prod(s))
        out.append(flat[..., off:off + n].reshape(*flat.shape[:-1], *s))
        off += n
    return out


def _memo(fn):
    cache = {}

    def wrapped(k):
        if k not in cache:
            cache[k] = fn(k)
        return cache[k]

    return wrapped


def _row(v):
    return v.reshape(1, -1)


def _pad_rows(w, rows=8):
    return jnp.pad(w, ((0, rows - w.shape[0]), (0, 0)))


def _ffn_fwd(x, g_pre, g_post, w_up, cw, cb, w_down, tag, comm_up=None, comm_down=None):
    (up, hn), got_up = _ffn_up(x, g_pre, w_up, "ffn_up" + tag, comm_up)
    a = _ffn_mid_fwd(up, cw, cb, "ffn_mid_fwd" + tag)
    (f, x_new), got_down = _matmul_norm_res(a, w_down, x, g_post, "ffn_down" + tag, comm_down)
    return x_new, (x, hn, up, a, f), got_up, got_down


def _ffn_bwd(dx, saved, g_pre, g_post, w_up, cw, cb, w_down, tag):
    x, hn, up, a, f = saved
    da, df, dg_post = _postnorm_bwd_matmul_nt(dx, f, g_post, w_down, BF16, "ffn_down_dx" + tag)
    dw_down = _matmul_tn(a, df, "ffn_down_dw" + tag)
    dup, stats = _ffn_mid_bwd(da, up, cw, cb, "ffn_mid_bwd" + tag)
    dx_in, dg_pre = _ffn_up_dx(dup, w_up, x, g_pre, dx, "ffn_up_dx" + tag)
    dw_up = _ffn_up_dw(hn, dup, "ffn_up_dw" + tag)
    return dx_in, dict(g_pre=dg_pre, g_post=dg_post, w_up=dw_up, w_down=dw_down, cw=stats[0:3], cb=stats[3])


def _sc_fwd(x, g_pre, g_post, w_in, cw, w_out, tag, comm_in=None, comm_out=None):
    (bcv, hn), got_in = _norm_matmul(x, g_pre, w_in, BF16, "sc_in" + tag, comm_in)
    q = _sc_mid_fwd(bcv, cw, "sc_mid_fwd" + tag)
    (m, x_new), got_out = _matmul_norm_res(q, w_out, x, g_post, "sc_out" + tag, comm_out)
    return x_new, (x, hn, bcv, q, m), got_in, got_out


def _sc_bwd(dx, saved, g_pre, g_post, w_in, cw, w_out, tag):
    x, hn, bcv, q, m = saved
    dq, dm, dg_post = _postnorm_bwd_matmul_nt(dx, m, g_post, w_out, BF16, "sc_out_dx" + tag)
    dw_out = _matmul_tn(q, dm, "sc_out_dw" + tag)
    dbcv, stats = _sc_mid_bwd(dq, bcv, cw, "sc_mid_bwd" + tag)
    (dx_in, dg_pre), _ = _matmul_nt_prenorm_bwd(dbcv, w_in, x, g_pre, dx, "sc_in_dx" + tag)
    dw_in = _matmul_tn(hn, dbcv, "sc_in_dw" + tag)
    return dx_in, dict(g_pre=dg_pre, g_post=dg_post, w_in=dw_in, w_out=dw_out, cw=stats[0:3])


def _ssd_fwd(x, g_pre, g_post, w_in, cw, cb, par, dexp, nw, w_out, consts, comm, tag, comm_in=None):
    d_inner = dexp.shape[1]
    (zx, hn), got_in = _norm_matmul(x, g_pre, w_in, F32, "ssd_in" + tag, comm_in)
    xbc = _ssd_conv_fwd(zx, cw, cb, d_inner, cw.shape[1], "ssd_conv_fwd" + tag)
    (yn, yf, states), got = _ssd_scan_fwd(zx, xbc, par, dexp, nw, consts, comm, "ssd_scan_fwd" + tag)
    if callable(w_out):
        w_out = w_out(got_in)
    (m, x_new), _ = _matmul_norm_res(yn, w_out, x, g_post, "ssd_out" + tag)
    return x_new, (x, hn, zx, xbc, yn, yf, states, m), got


def _ssd_bwd(dx, saved, g_pre, g_post, w_in, cw, cb, par, dexp, nw, w_out, consts, comm, tag, comm_dx=None):
    x, hn, zx, xbc, yn, yf, states, m = saved
    d_inner = dexp.shape[1]
    dyn, dm, dg_post = _postnorm_bwd_matmul_nt(dx, m, g_post, w_out, F32, "ssd_out_dx" + tag)
    dw_out = _matmul_tn(yn, dm, "ssd_out_dw" + tag)
    (dzx, dxbc, dnw, dpar), got = _ssd_scan_bwd(dyn, yf, zx, xbc, states, par, dexp, nw, consts, comm(dw_out),
                                                "ssd_scan_bwd" + tag)
    dzx, stats = _ssd_conv_bwd(dxbc, zx, cw, cb, d_inner, dzx, "ssd_conv_bwd" + tag)
    dw_in = _matmul_tn(hn, dzx, "ssd_in_dw" + tag)
    (dx_in, dg_pre), got_dx = _matmul_nt_prenorm_bwd(dzx, w_in, x, g_pre, dx, "ssd_in_dx" + tag,
                                                     None if comm_dx is None else comm_dx(dw_in))
    n_heads = d_inner // HEAD_DIM
    grads = dict(g_pre=dg_pre, g_post=dg_post, w_in=dw_in, w_out=dw_out, cw=stats[0:4], cb=stats[4],
                 dt_bias=dpar[0, :n_heads], a_log=dpar[1, :n_heads], d=dpar[2, :n_heads], nw=dnw[0])
    return dx_in, grads, got, got_dx


def kernel(x, mix_pre_g, mix_post_g, ffn_pre_g, ffn_post_g, ssd_w_in, ssd_conv_w, ssd_conv_b, ssd_dt_bias, ssd_A_log, ssd_D, ssd_norm_w, ssd_w_out, sc_w_in, sc_conv_w, sc_w_out, ffn_w_up, ffn_conv_w, ffn_conv_b, ffn_w_down, loss_target, m_mix_pre_g, m_mix_post_g, m_ffn_pre_g, m_ffn_post_g, m_ssd_w_in, m_ssd_conv_w, m_ssd_conv_b, m_ssd_dt_bias, m_ssd_A_log, m_ssd_D, m_ssd_norm_w, m_ssd_w_out, m_sc_w_in, m_sc_conv_w, m_sc_w_out, m_ffn_w_up, m_ffn_conv_w, m_ffn_conv_b, m_ffn_w_down, v_mix_pre_g, v_mix_post_g, v_ffn_pre_g, v_ffn_post_g, v_ssd_w_in, v_ssd_conv_w, v_ssd_conv_b, v_ssd_dt_bias, v_ssd_A_log, v_ssd_D, v_ssd_norm_w, v_ssd_w_out, v_sc_w_in, v_sc_conv_w, v_sc_w_out, v_ffn_w_up, v_ffn_conv_w, v_ffn_conv_b, v_ffn_w_down):
    names = ["mix_pre_g", "mix_post_g", "ffn_pre_g", "ffn_post_g", "ssd_w_in", "ssd_conv_w", "ssd_conv_b",
             "ssd_dt_bias", "ssd_A_log", "ssd_D", "ssd_norm_w", "ssd_w_out", "sc_w_in", "sc_conv_w", "sc_w_out",
             "ffn_w_up", "ffn_conv_w", "ffn_conv_b", "ffn_w_down"]
    env = locals()
    wts = {n: env[n] for n in names}
    mom = {n: env["m_" + n] for n in names}
    var = {n: env["v_" + n] for n in names}

    depth, d_model = mix_pre_g.shape
    n_ssd, n_heads = ssd_dt_bias.shape
    n_sc = sc_conv_w.shape[0]
    d_inner = n_heads * HEAD_DIM
    conv_dim = d_inner + 2 * N_GROUPS * D_STATE
    ssd_in_dim = d_inner + conv_dim + n_heads
    ssd_in_pad = d_inner + conv_dim + LANE
    q_chip = 2 * lax.axis_index("x") + lax.axis_index("y")
    core = lax.axis_index("c")

    assert depth == 4 and n_ssd == 2 and n_sc == 2, "the exchange schedule is written for this trunk"
    ssd_items = lambda j: [("ssd_w_in", j), ("ssd_w_out", j)]
    sc_items = lambda j: [("sc_w_in", j), ("sc_w_out", j)]
    ffn_items = lambda i: [("ffn_w_up", i), ("ffn_w_down", i)]
    gather_first = [("ssd_w_in", 0)]
    gather_in_ssd_in = {0: [("ssd_w_out", 0)]}
    gather_in_scan = {0: ffn_items(0) + sc_items(0), 2: ffn_items(2) + sc_items(1)}
    gather_in_ffn = {0: ([("ffn_w_up", 1)], [("ffn_w_down", 1)]), 1: ([("ssd_w_in", 1)], []),
                     2: ([("ffn_w_up", 3)], [("ffn_w_down", 3)])}
    gather_in_sc = {1: ([("ssd_w_out", 1)], [])}
    reduce_in_scan = {2: ffn_items(3) + sc_items(1) + ffn_items(2),
                      0: ssd_items(1) + ffn_items(1) + sc_items(0) + ffn_items(0) + [("ssd_w_out", 0)]}
    reduce_in_dx = {0: [("ssd_w_in", 0)]}
    axis_of = dict(BIG)
    gathered = {}

    def gather_plan(items, small=None):
        mine = [wts[n][layer:layer + 1].astype(BF16) for n, layer in items]
        return _allgather_plan(mine, small) if items else None

    def gather_done(items, results):
        for item, buf in zip(items, results):
            gathered[item] = buf[:, 0]

    def full(n, layer):
        return _from_shards(gathered[(n, layer)], axis_of[n])

    conv_names = ["ssd_conv_w", "sc_conv_w", "ffn_conv_w"]
    conv_shapes = [wts[n].shape for n in conv_names]
    small_mine = _pack_rows([wts[n].reshape(-1) for n in conv_names], LANE, 8)
    *results, small_all = _run_comm(gather_plan(gather_first, small_mine), "allgather_first")
    gather_done(gather_first, results)
    conv_full = {}
    for n, f, s in zip(conv_names, _unpack(small_all.reshape(4, -1), conv_shapes), conv_shapes):
        conv_full[n] = f.transpose(1, 2, 0, 3).reshape(s[0], s[1], 4 * s[2])

    consts = _scan_constants(n_heads)

    def ssd_args(j):
        par = jnp.zeros((8, LANE), F32).at[0, :n_heads].set(ssd_dt_bias[j]).at[1, :n_heads].set(ssd_A_log[j])
        dexp = jnp.repeat(ssd_D[j], HEAD_DIM).reshape(1, d_inner)
        w_in = jnp.pad(full("ssd_w_in", j), ((0, 0), (0, ssd_in_pad - ssd_in_dim)))
        return (w_in, _pad_rows(conv_full["ssd_conv_w"][j]), _row(ssd_conv_b[j]), par, dexp, _row(ssd_norm_w[j]))

    def sc_args(j):
        return (_interleave(full("sc_w_in", j), 3), _pad_rows(conv_full["sc_conv_w"][j]), full("sc_w_out", j))

    def ffn_args(i):
        return (gathered[("ffn_w_up", i)], _pad_rows(conv_full["ffn_conv_w"][i]), _row(ffn_conv_b[i]),
                full("ffn_w_down", i))

    ssd_args, sc_args, ffn_args = _memo(ssd_args), _memo(sc_args), _memo(ffn_args)

    h = x[0]
    saved = []
    for i in range(depth):
        j = i // 2
        gp, gq = _row(mix_pre_g[i]), _row(mix_post_g[i])
        if i % 2 == 0:
            items_in = gather_in_ssd_in.get(i, [])

            def w_out_when_here(got_in, items_in=items_in, j=j):
                gather_done(items_in, got_in)
                return full("ssd_w_out", j)

            h, sv, results = _ssd_fwd(h, gp, gq, *ssd_args(j), w_out_when_here, consts, gather_plan(gather_in_scan[i]),
                                      tag="", comm_in=gather_plan(items_in))
            gather_done(gather_in_scan[i], results)
        else:
            items_in, items_out = gather_in_sc.get(i, ([], []))
            h, sv, got_in, got_out = _sc_fwd(h, gp, gq, *sc_args(j), tag="", comm_in=gather_plan(items_in),
                                             comm_out=gather_plan(items_out))
            gather_done(items_in, got_in)
            gather_done(items_out, got_out)
        items_up, items_down = gather_in_ffn.get(i, ([], []))
        h, sv2, got_up, got_down = _ffn_fwd(h, _row(ffn_pre_g[i]), _row(ffn_post_g[i]), *ffn_args(i), tag="",
                                            comm_up=gather_plan(items_up), comm_down=gather_plan(items_down))
        gather_done(items_up, got_up)
        gather_done(items_down, got_down)
        saved.append((sv, sv2))
    dh, loss_part = _loss_head(h, loss_target[0], "loss_head")

    mix_grads, ffn_grads = [None] * depth, [None] * depth
    where = jnp.stack([q_chip, core]).astype(jnp.int32)

    def shard_grad(n, layer):
        if n == "ffn_w_up":
            g = ffn_grads[layer]["w_up"]
        elif n == "ffn_w_down":
            g = _to_shards(ffn_grads[layer]["w_down"], 1)
        elif n == "ssd_w_in":
            g = _to_shards(early[(n, layer)][:, :ssd_in_dim], 2)
        elif n == "ssd_w_out":
            g = _to_shards(early[(n, layer)], 1)
        elif n == "sc_w_in":
            g = _to_shards(_deinterleave(mix_grads[2 * layer + 1]["w_in"], 3), 2)
        else:
            g = _to_shards(mix_grads[2 * layer + 1]["w_out"], 1)
        return g[:, None]

    def reduce_begin(items, tag):
        by_shard = [shard_grad(n, layer) for n, layer in items]
        from_sib = _run_comm(_grads_to_sibling_plan(by_shard), "grads_to_sibling" + tag)
        chip_sums = [_add_core_halves(where, g, r, "add_core_halves_%s%d" % item)
                     for item, g, r in zip(items, by_shard, from_sib)]
        return by_shard, from_sib, _grads_to_chips_plan(chip_sums)

    sums = {}
    early = {}

    def riding(items, tag):
        by_shard, from_sib, plan = reduce_begin(items, tag)

        def arrived(from_chips):
            for (n, layer), g, r, rr in zip(items, by_shard, from_sib, from_chips):
                sums[n] = _sum_shard(where, g, r, rr, sums.get(n), layer, wts[n].shape[0],
                                     "sum_shard_%s%d" % (n, layer))

        return plan, arrived

    for i in reversed(range(depth)):
        j = i // 2
        sv, sv2 = saved[i]
        dh, ffn_grads[i] = _ffn_bwd(dh, sv2, _row(ffn_pre_g[i]), _row(ffn_post_g[i]), *ffn_args(i), tag="")
        gp, gq = _row(mix_pre_g[i]), _row(mix_post_g[i])
        if i % 2 == 0:
            then = {}

            def in_scan(dw_out, i=i, j=j, then=then):
                early[("ssd_w_out", j)] = dw_out
                plan, then["scan"] = riding(reduce_in_scan[i], "_%d" % i)
                return plan

            def in_dx(dw_in, i=i, j=j, then=then):
                early[("ssd_w_in", j)] = dw_in
                plan, then["dx"] = riding(reduce_in_dx[i], "_dx%d" % i) if i in reduce_in_dx else (None, None)
                return plan

            dh, mix_grads[i], got, got_dx = _ssd_bwd(dh, sv, gp, gq, *ssd_args(j), full("ssd_w_out", j), consts,
                                                     in_scan, tag="", comm_dx=in_dx)
            then["scan"](got)
            if then["dx"] is not None:
                then["dx"](got_dx)
        else:
            w_in, scw, w_out = sc_args(j)
            dh, mix_grads[i] = _sc_bwd(dh, sv, gp, gq, w_in, scw, w_out, tag="")
    grad_x = dh[None]
    big_grads = dict(zip([n for n, _ in BIG], _swap_halves([sums[n] for n, _ in BIG])))
    ssd_l = [mix_grads[i] for i in range(0, depth, 2)]
    sc_l = [mix_grads[i] for i in range(1, depth, 2)]
    stack = lambda layers, k: jnp.stack([g[k] for g in layers])

    small_names = ["mix_pre_g", "mix_post_g", "ffn_pre_g", "ffn_post_g", "ssd_conv_w", "ssd_conv_b", "ssd_dt_bias",
                   "ssd_A_log", "ssd_D", "ssd_norm_w", "sc_conv_w", "ffn_conv_w", "ffn_conv_b"]
    small_local = {
        "mix_pre_g": jnp.concatenate([g["g_pre"] for g in mix_grads]),
        "mix_post_g": jnp.concatenate([g["g_post"] for g in mix_grads]),
        "ffn_pre_g": jnp.concatenate([g["g_pre"] for g in ffn_grads]),
        "ffn_post_g": jnp.concatenate([g["g_post"] for g in ffn_grads]),
        "ssd_conv_w": stack(ssd_l, "cw"), "ssd_conv_b": stack(ssd_l, "cb"), "ssd_dt_bias": stack(ssd_l, "dt_bias"),
        "ssd_A_log": stack(ssd_l, "a_log"), "ssd_D": stack(ssd_l, "d"), "ssd_norm_w": stack(ssd_l, "nw"),
        "sc_conv_w": stack(sc_l, "cw"), "ffn_conv_w": stack(ffn_grads, "cw"), "ffn_conv_b": stack(ffn_grads, "cb"),
    }
    small_full_shapes = [small_local[n].shape for n in small_names]
    spack = _pack_rows([small_local[n].reshape(-1) for n in small_names] + [loss_part.reshape(-1)], LANE, 8)
    stotal = _sum_slots(_allgather_small(spack), "sum_small").reshape(-1)
    small_grads = dict(zip(small_names, _unpack(stotal, small_full_shapes)))
    loss = stotal[sum(int(np.prod(s)) for s in small_full_shapes)]
    for n in conv_names:
        width = wts[n].shape[-1]
        small_grads[n] = lax.dynamic_slice_in_dim(small_grads[n], q_chip * width, width, axis=2)

    grads, delta, new_m, new_v = {}, {}, {}, {}
    for n, _ in BIG:
        s = wts[n].shape
        two_d = lambda a: a.reshape(-1, s[-1])
        grads[n] = big_grads[n]
        d, mn, vn = _adamw(two_d(wts[n]), two_d(mom[n]), two_d(var[n]), two_d(grads[n]), "adamw_" + n)
        delta[n], new_m[n], new_v[n] = d.reshape(s), mn.reshape(s), vn.reshape(s)
    small_shapes = [wts[n].shape for n in small_names]
    pk = lambda d: _pack_rows([d[n].reshape(-1) for n in small_names], LANE, 8)
    for n in small_names:
        grads[n] = small_grads[n].reshape(wts[n].shape)
    d, mn, vn = _adamw(pk(wts), pk(mom), pk(var), pk(grads), "adamw_small")
    for out, packed in ((delta, d), (new_m, mn), (new_v, vn)):
        out.update(zip(small_names, _unpack(packed.reshape(-1), small_shapes)))

    return (loss, grad_x, *[grads[n] for n in names], *[delta[n] for n in names], *[new_m[n] for n in names],
            *[new_v[n] for n in names])
```

```python
from typing import Callable, NamedTuple

import jax
import jax.numpy as jnp
import numpy as np
from jax import lax
from jax.experimental import pallas as pl
from jax.experimental.pallas import tpu as pltpu

F32 = jnp.float32
BF16 = jnp.bfloat16
SDS = jax.ShapeDtypeStruct
MESH = pl.DeviceIdType.MESH
ANY = pl.BlockSpec(memory_space=pl.ANY)

EPS = 1e-6
CHUNK = 64
HEAD_DIM = 64
N_GROUPS = 8
D_STATE = 128
HEADS_PER_GROUP = 4
GROUP_W = HEADS_PER_GROUP * HEAD_DIM
LANE = 128
ROW_TILE = 128
HALO = 8
VMEM_LIMIT = 56 * 1024 * 1024

ADAM_LR = 0.001
ADAM_B1 = 0.9
ADAM_B2 = 0.999
ADAM_EPS = 1e-08
ADAM_WD = 0.01
ADAM_STEP = 10

NT = (((1,), (1,)), ((), ()))
TN = (((0,), (0,)), ((), ()))


def _cp(*sem):
    return pltpu.CompilerParams(dimension_semantics=sem or None, vmem_limit_bytes=VMEM_LIMIT)


def _sigmoid(x):
    return 1.0 / (1.0 + jnp.exp(-x))


def _dsilu(x, s):
    return s * (1.0 + x * (1.0 - s))


def _rsq(x):
    return lax.rsqrt(jnp.mean(x * x, axis=-1, keepdims=True) + EPS)


MM_ROWS = 256


def _norm_matmul(x, g, w, out_dtype, name, comm=None):
    L, D = x.shape
    N = w.shape[1]
    tm = min(MM_ROWS, L)

    def body(x_ref, g_ref, w_ref, o_ref, hn_ref):
        xv = x_ref[...]
        hn = (xv * _rsq(xv) * g_ref[...]).astype(BF16)
        hn_ref[...] = hn
        o_ref[...] = jnp.dot(hn, w_ref[...], preferred_element_type=F32).astype(out_dtype)

    row = lambda i: (i, 0)
    fix = lambda i: (0, 0)
    return _carrier_call(
        body, comm, L // tm,
        in_specs=[pl.BlockSpec((tm, D), row), pl.BlockSpec((1, D), fix), pl.BlockSpec((D, N), fix)],
        out_specs=[pl.BlockSpec((tm, N), row), pl.BlockSpec((tm, D), row)],
        out_shape=[SDS((L, N), out_dtype), SDS((L, D), BF16)], scratch_shapes=[], name=name, args=(x, g, w))


def _matmul_norm_res(a, w, x, g, name, comm=None):
    L, K = a.shape
    D = w.shape[1]
    tm = min(MM_ROWS, L)

    def body(a_ref, w_ref, x_ref, g_ref, m_ref, xo_ref):
        m = jnp.dot(a_ref[...], w_ref[...], preferred_element_type=F32)
        m_ref[...] = m
        xo_ref[...] = x_ref[...] + m * _rsq(m) * g_ref[...]

    row = lambda i: (i, 0)
    fix = lambda i: (0, 0)
    return _carrier_call(
        body, comm, L // tm,
        in_specs=[pl.BlockSpec((tm, K), row), pl.BlockSpec((K, D), fix), pl.BlockSpec((tm, D), row),
                  pl.BlockSpec((1, D), fix)],
        out_specs=[pl.BlockSpec((tm, D), row), pl.BlockSpec((tm, D), row)],
        out_shape=[SDS((L, D), F32), SDS((L, D), F32)], scratch_shapes=[], name=name, args=(a, w, x, g))


def _postnorm_bwd_matmul_nt(dx, m, g, w, out_dtype, name):
    L, D = dx.shape
    K = w.shape[0]
    tm = min(MM_ROWS, L)

    def body(dx_ref, m_ref, g_ref, w_ref, da_ref, dm_ref, dg_ref):
        @pl.when(pl.program_id(0) == 0)
        def _():
            dg_ref[...] = jnp.zeros_like(dg_ref)

        m = m_ref[...]
        dxv = dx_ref[...]
        r = _rsq(m)
        mh = m * r
        dg_ref[...] += jnp.sum(dxv * mh, axis=0, keepdims=True)
        dyg = dxv * g_ref[...]
        dm = (r * (dyg - mh * jnp.mean(dyg * mh, axis=-1, keepdims=True))).astype(BF16)
        dm_ref[...] = dm
        da_ref[...] = lax.dot_general(dm, w_ref[...], NT, preferred_element_type=F32).astype(out_dtype)

    row = lambda i: (i, 0)
    fix = lambda i: (0, 0)
    return pl.pallas_call(
        body, grid=(L // tm,),
        in_specs=[pl.BlockSpec((tm, D), row), pl.BlockSpec((tm, D), row), pl.BlockSpec((1, D), fix),
                  pl.BlockSpec((K, D), fix)],
        out_specs=[pl.BlockSpec((tm, K), row), pl.BlockSpec((tm, D), row), pl.BlockSpec((1, D), fix)],
        out_shape=[SDS((L, K), out_dtype), SDS((L, D), BF16), SDS((1, D), F32)],
        name=name, compiler_params=_cp("arbitrary"))(dx, m, g, w)


DW_ACC_BYTES = 13 * 512 * 1024


def _dw_tiles(ka, n):
    ta = ka if ka <= 1024 else ka // 2
    fits = [d for d in range(LANE, n + 1, LANE) if n % d == 0 and ta * d * 4 <= DW_ACC_BYTES]
    return ta, max(fits)


def _matmul_tn(a, b, name):
    L, Ka = a.shape
    N = b.shape[1]
    ta, tn = _dw_tiles(Ka, N)
    tl = min(512, L)
    n_l = L // tl

    def body(a_ref, b_ref, o_ref, acc_ref):
        l = pl.program_id(2)

        @pl.when(l == 0)
        def _():
            acc_ref[...] = jnp.zeros_like(acc_ref)

        acc_ref[...] += lax.dot_general(a_ref[...], b_ref[...], TN, preferred_element_type=F32)

        @pl.when(l == n_l - 1)
        def _():
            o_ref[...] = acc_ref[...].astype(BF16)

    return pl.pallas_call(
        body, grid=(Ka // ta, N // tn, n_l),
        in_specs=[pl.BlockSpec((tl, ta), lambda i, j, l: (l, i)), pl.BlockSpec((tl, tn), lambda i, j, l: (l, j))],
        out_specs=pl.BlockSpec((ta, tn), lambda i, j, l: (i, j)),
        out_shape=SDS((Ka, N), BF16),
        scratch_shapes=[pltpu.VMEM((ta, tn), F32)],
        name=name, compiler_params=_cp("parallel", "parallel", "arbitrary"))(a, b)


def _ffn_up(x, g, w4, cw, cb, name, comm=None):
    L, D = x.shape
    b = w4.shape[2]
    C = 2 * b
    tm = min(MM_ROWS, L)

    def body(x_ref, g_ref, w_ref, cw_ref, cb_ref, o_ref, hn_ref, a_ref, tail_ref):
        @pl.when(pl.program_id(0) == 0)
        def _():
            tail_ref[...] = jnp.zeros_like(tail_ref)

        xv = x_ref[...]
        hn = (xv * _rsq(xv) * g_ref[...]).astype(BF16)
        hn_ref[...] = hn
        for half in range(2):
            cols = slice(half * b, (half + 1) * b)
            for part in range(2):
                o_ref[part, :, cols] = jnp.dot(hn, w_ref[2 * part + half], preferred_element_type=F32).astype(BF16)
            for s in range(half * b // LANE, (half + 1) * b // LANE):
                sl = slice(s * LANE, (s + 1) * LANE)
                gp = o_ref[0, :, sl].astype(F32)
                ext = jnp.concatenate([tail_ref[:, sl], gp], axis=0)
                back = lambda k: pltpu.roll(ext, k, axis=0)[HALO:HALO + tm]
                gate = gp * cw_ref[2:3, sl] + back(1) * cw_ref[1:2, sl] + back(2) * cw_ref[0:1, sl] + cb_ref[:, sl]
                a_ref[:, sl] = (gate * _sigmoid(gate) * o_ref[1, :, sl].astype(F32)).astype(BF16)
                tail_ref[:, sl] = gp[tm - HALO:tm]

    row = lambda i: (i, 0)
    fix = lambda i: (0, 0)
    return _carrier_call(
        body, comm, L // tm,
        in_specs=[pl.BlockSpec((tm, D), row), pl.BlockSpec((1, D), fix), pl.BlockSpec((4, D, b), lambda i: (0, 0, 0)),
                  pl.BlockSpec(cw.shape, fix), pl.BlockSpec(cb.shape, fix)],
        out_specs=[pl.BlockSpec((2, tm, C), lambda i: (0, i, 0)), pl.BlockSpec((tm, D), row), pl.BlockSpec((tm, C), row)],
        out_shape=[SDS((2, L, C), BF16), SDS((L, D), BF16), SDS((L, C), BF16)],
        scratch_shapes=[pltpu.VMEM((HALO, C), F32)], name=name, args=(x, g, w4, cw, cb))


def _ffn_up_dx(dup, w4, x, g, dres, name):
    _, L, _ = dup.shape
    _, D, b = w4.shape
    tm = min(MM_ROWS, L)

    def body(dy_ref, w_ref, x_ref, g_ref, dres_ref, dx_ref, dg_ref):
        @pl.when(pl.program_id(0) == 0)
        def _():
            dg_ref[...] = jnp.zeros_like(dg_ref)

        dh = jnp.zeros((tm, D), F32)
        for q in range(4):
            dh = dh + lax.dot_general(dy_ref[q // 2, :, (q % 2) * b:(q % 2 + 1) * b], w_ref[q], NT,
                                      preferred_element_type=F32)
        xv = x_ref[...]
        r = _rsq(xv)
        xh = xv * r
        dg_ref[...] += jnp.sum(dh * xh, axis=0, keepdims=True)
        dyg = dh * g_ref[...]
        dx_ref[...] = dres_ref[...] + r * (dyg - xh * jnp.mean(dyg * xh, axis=-1, keepdims=True))

    row = lambda i: (i, 0)
    fix = lambda i: (0, 0)
    return pl.pallas_call(
        body, grid=(L // tm,),
        in_specs=[pl.BlockSpec((2, tm, 2 * b), lambda i: (0, i, 0)), pl.BlockSpec((4, D, b), lambda i: (0, 0, 0)),
                  pl.BlockSpec((tm, D), row), pl.BlockSpec((1, D), fix), pl.BlockSpec((tm, D), row)],
        out_specs=[pl.BlockSpec((tm, D), row), pl.BlockSpec((1, D), fix)],
        out_shape=[SDS((L, D), F32), SDS((1, D), F32)],
        name=name, compiler_params=_cp("arbitrary"))(dup, w4, x, g, dres)


def _ffn_up_dw(hn, dup, name):
    L, D = hn.shape
    b = dup.shape[2] // 2
    tl = min(512, L)
    n_l = L // tl

    def body(a_ref, b_ref, o_ref, acc_ref):
        l = pl.program_id(1)

        @pl.when(l == 0)
        def _():
            acc_ref[...] = jnp.zeros_like(acc_ref)

        acc_ref[...] += lax.dot_general(a_ref[...], b_ref[...], TN, preferred_element_type=F32)

        @pl.when(l == n_l - 1)
        def _():
            o_ref[...] = acc_ref[...].astype(BF16)

    return pl.pallas_call(
        body, grid=(4, n_l),
        in_specs=[pl.BlockSpec((tl, D), lambda q, l: (l, 0)),
                  pl.BlockSpec((None, tl, b), lambda q, l: (q // 2, l, q % 2))],
        out_specs=pl.BlockSpec((None, D, b), lambda q, l: (q, 0, 0)),
        out_shape=SDS((4, D, b), BF16),
        scratch_shapes=[pltpu.VMEM((D, b), F32)],
        name=name, compiler_params=_cp("parallel", "arbitrary"))(hn, dup)


def _matmul_nt_prenorm_bwd(dy, w, x, g, dres, name, comm=None):
    L, N = dy.shape
    D = w.shape[0]
    tm = min(MM_ROWS, L)

    def body(dy_ref, w_ref, x_ref, g_ref, dres_ref, dx_ref, dg_ref):
        @pl.when(pl.program_id(0) == 0)
        def _():
            dg_ref[...] = jnp.zeros_like(dg_ref)

        dh = lax.dot_general(dy_ref[...], w_ref[...], NT, preferred_element_type=F32)
        xv = x_ref[...]
        r = _rsq(xv)
        xh = xv * r
        dg_ref[...] += jnp.sum(dh * xh, axis=0, keepdims=True)
        dyg = dh * g_ref[...]
        dx_ref[...] = dres_ref[...] + r * (dyg - xh * jnp.mean(dyg * xh, axis=-1, keepdims=True))

    row = lambda i: (i, 0)
    fix = lambda i: (0, 0)
    return _carrier_call(
        body, comm, L // tm,
        in_specs=[pl.BlockSpec((tm, N), row), pl.BlockSpec((D, N), fix), pl.BlockSpec((tm, D), row),
                  pl.BlockSpec((1, D), fix), pl.BlockSpec((tm, D), row)],
        out_specs=[pl.BlockSpec((tm, D), row), pl.BlockSpec((1, D), fix)],
        out_shape=[SDS((L, D), F32), SDS((1, D), F32)], scratch_shapes=[], name=name, args=(dy, w, x, g, dres))


def _loss_head(y, t, name):
    L, D = y.shape
    tm = min(512, L)

    def body(y_ref, t_ref, dy_ref, loss_ref):
        @pl.when(pl.program_id(0) == 0)
        def _():
            loss_ref[...] = jnp.zeros_like(loss_ref)

        e = y_ref[...] - t_ref[...]
        dy_ref[...] = e * (1.0 / D)
        s = jnp.sum(jnp.sum(e * e, axis=1, keepdims=True), axis=0, keepdims=True)
        loss_ref[...] += s * (0.5 / D)

    row = lambda i: (i, 0)
    return pl.pallas_call(
        body, grid=(L // tm,),
        in_specs=[pl.BlockSpec((tm, D), row), pl.BlockSpec((tm, D), row)],
        out_specs=[pl.BlockSpec((tm, D), row), pl.BlockSpec((1, 1), lambda i: (0, 0))],
        out_shape=[SDS((L, D), F32), SDS((1, 1), F32)],
        name=name, compiler_params=_cp("arbitrary"))(y, t)


def _tile_rows(ref):
    return HALO * (4 // jnp.dtype(ref.dtype).itemsize)


def _prev_rows(ref, r0, i, cols):
    n = _tile_rows(ref)
    p0 = pl.multiple_of(jnp.maximum(r0 - n, 0), n)
    return jnp.where(i > 0, ref[pl.ds(p0, n), cols].astype(F32)[n - HALO:], 0.0)


def _next_rows(ref, r0, i, n_tiles, cols):
    n = _tile_rows(ref)
    n0 = pl.multiple_of(jnp.minimum(r0 + ROW_TILE, n_tiles * ROW_TILE - n), n)
    return jnp.where(i < n_tiles - 1, ref[pl.ds(n0, n), cols].astype(F32)[:HALO], 0.0)


def _rows_f32(ref, rows, cols):
    return ref[rows, cols].astype(F32)


def _back(ext, s):
    return pltpu.roll(ext, s, axis=0)[HALO:HALO + ROW_TILE]


def _fwd(ext, s):
    n = ext.shape[0]
    return pltpu.roll(ext, n - s, axis=0)[:ROW_TILE]


def _store_rows(ref, rows):
    ref[...] = jnp.zeros_like(ref)
    for k, v in enumerate(rows):
        ref[k:k + 1, :] = v


def _strip_call(body, L, n_strips, ins, outs, name):
    def spec(rows, width, off):
        if off is None:
            return pl.BlockSpec((rows, width), lambda j: (0, 0))
        return pl.BlockSpec((rows, width), lambda j: (0, j + off))

    return pl.pallas_call(
        body, grid=(n_strips,),
        in_specs=[spec(a.shape[0], w, off) for a, w, off in ins],
        out_specs=[spec(s.shape[0], w, off) for s, w, off in outs],
        out_shape=[s for s, _, _ in outs],
        name=name, compiler_params=_cp("parallel"))(*[a for a, _, _ in ins])


def _ffn_mid_bwd(da, up, cw, cb, name):
    L, C = da.shape
    n_tiles = L // ROW_TILE
    al = slice(None)

    def body(da_ref, up_ref, cw_ref, cb_ref, dup_ref, st_ref):
        w0, w1, w2 = cw_ref[0:1, :], cw_ref[1:2, :], cw_ref[2:3, :]
        b = cb_ref[...]
        gate_ref, val_ref = up_ref.at[0], up_ref.at[1]

        def step(i, c):
            r0 = pl.multiple_of(i * ROW_TILE, ROW_TILE)
            rows = pl.ds(r0, ROW_TILE)
            gp = _rows_f32(gate_ref, rows, al)
            gpe = jnp.concatenate([_prev_rows(gate_ref, r0, i, al), gp, _next_rows(gate_ref, r0, i, n_tiles, al)],
                                  axis=0)
            g1, g2 = pltpu.roll(gpe, 1, axis=0), pltpu.roll(gpe, 2, axis=0)
            gate = (gpe * w2 + g1 * w1 + g2 * w0 + b)[HALO:]
            sg = _sigmoid(gate)
            da_e = jnp.concatenate([_rows_f32(da_ref, rows, al), _next_rows(da_ref, r0, i, n_tiles, al)], axis=0)
            val_e = jnp.concatenate([_rows_f32(val_ref, rows, al), _next_rows(val_ref, r0, i, n_tiles, al)], axis=0)
            dgate_e = da_e * val_e * _dsilu(gate, sg)
            dgate = dgate_e[:ROW_TILE]
            dgp = dgate * w2 + _fwd(dgate_e, 1) * w1 + _fwd(dgate_e, 2) * w0
            dup_ref[0, rows, :] = dgp.astype(BF16)
            dup_ref[1, rows, :] = (da_e * gate * sg)[:ROW_TILE].astype(BF16)
            s = lambda v: jnp.sum(v, axis=0, keepdims=True)
            t = slice(HALO, HALO + ROW_TILE)
            return (c[0] + s(dgate * g2[t]), c[1] + s(dgate * g1[t]), c[2] + s(dgate * gp), c[3] + s(dgate))

        z = jnp.zeros((1, LANE), F32)
        _store_rows(st_ref, lax.fori_loop(0, n_tiles, step, (z, z, z, z)))

    strip = lambda rows: pl.BlockSpec((rows, LANE), lambda j: (0, j))
    pair = pl.BlockSpec((2, L, LANE), lambda j: (0, 0, j))
    return pl.pallas_call(
        body, grid=(C // LANE,), in_specs=[strip(L), pair, strip(cw.shape[0]), strip(1)],
        out_specs=[pair, strip(8)], out_shape=[SDS((2, L, C), BF16), SDS((8, C), F32)],
        name=name, compiler_params=_cp("parallel"))(da, up, cw, cb)


def _sc_mid_fwd(bcv, cw, name):
    L = bcv.shape[0]
    C = bcv.shape[1] // 3
    n_tiles = L // ROW_TILE
    s0, s1, s2 = slice(0, LANE), slice(LANE, 2 * LANE), slice(2 * LANE, 3 * LANE)

    def body(x_ref, cw_ref, q_ref):
        w0, w1, w2 = cw_ref[0:1, :], cw_ref[1:2, :], cw_ref[2:3, :]

        def step(i, c):
            r0 = pl.multiple_of(i * ROW_TILE, ROW_TILE)
            rows = pl.ds(r0, ROW_TILE)
            p = _rows_f32(x_ref, rows, s1) * _rows_f32(x_ref, rows, s2)
            ext = jnp.concatenate([_prev_rows(x_ref, r0, i, s1) * _prev_rows(x_ref, r0, i, s2), p], axis=0)
            u = p * w2 + _back(ext, 1) * w1 + _back(ext, 2) * w0
            q_ref[rows, :] = (_rows_f32(x_ref, rows, s0) * u).astype(BF16)
            return c

        lax.fori_loop(0, n_tiles, step, 0)

    return _strip_call(body, L, C // LANE, [(bcv, 3 * LANE, 0), (cw, LANE, 0)],
                       [(SDS((L, C), BF16), LANE, 0)], name)[0]


def _sc_mid_bwd(dq, bcv, cw, name):
    L, C = dq.shape
    n_tiles = L // ROW_TILE
    s0, s1, s2, al = slice(0, LANE), slice(LANE, 2 * LANE), slice(2 * LANE, 3 * LANE), slice(None)

    def body(dq_ref, x_ref, cw_ref, dx_ref, st_ref):
        w0, w1, w2 = cw_ref[0:1, :], cw_ref[1:2, :], cw_ref[2:3, :]

        def step(i, c):
            r0 = pl.multiple_of(i * ROW_TILE, ROW_TILE)
            rows = pl.ds(r0, ROW_TILE)
            gb, gc, v = _rows_f32(x_ref, rows, s0), _rows_f32(x_ref, rows, s1), _rows_f32(x_ref, rows, s2)
            dq_v = _rows_f32(dq_ref, rows, al)
            p = gc * v
            pext = jnp.concatenate([_prev_rows(x_ref, r0, i, s1) * _prev_rows(x_ref, r0, i, s2), p], axis=0)
            p1, p2 = _back(pext, 1), _back(pext, 2)
            u = p * w2 + p1 * w1 + p2 * w0
            du = dq_v * gb
            du_n = _next_rows(dq_ref, r0, i, n_tiles, al) * _next_rows(x_ref, r0, i, n_tiles, s0)
            ext = jnp.concatenate([du, du_n], axis=0)
            dp = du * w2 + _fwd(ext, 1) * w1 + _fwd(ext, 2) * w0
            dx_ref[rows, s0] = (dq_v * u).astype(BF16)
            dx_ref[rows, s1] = (dp * v).astype(BF16)
            dx_ref[rows, s2] = (dp * gc).astype(BF16)
            s = lambda t: jnp.sum(t, axis=0, keepdims=True)
            return (c[0] + s(du * p2), c[1] + s(du * p1), c[2] + s(du * p))

        z = jnp.zeros((1, LANE), F32)
        _store_rows(st_ref, lax.fori_loop(0, n_tiles, step, (z, z, z)))

    return _strip_call(body, L, C // LANE, [(dq, LANE, 0), (bcv, 3 * LANE, 0), (cw, LANE, 0)],
                       [(SDS((L, 3 * C), BF16), 3 * LANE, 0), (SDS((8, C), F32), LANE, 0)], name)


def _ssd_conv_fwd(zx, cw, cb, col0, C, name):
    L = zx.shape[0]
    n_tiles = L // ROW_TILE
    al = slice(None)

    def body(x_ref, cw_ref, cb_ref, o_ref):
        w0, w1, w2, w3 = cw_ref[0:1, :], cw_ref[1:2, :], cw_ref[2:3, :], cw_ref[3:4, :]
        b = cb_ref[...]

        def step(i, c):
            r0 = pl.multiple_of(i * ROW_TILE, ROW_TILE)
            rows = pl.ds(r0, ROW_TILE)
            xv = x_ref[rows, :]
            ext = jnp.concatenate([_prev_rows(x_ref, r0, i, al), xv], axis=0)
            cv = xv * w3 + _back(ext, 1) * w2 + _back(ext, 2) * w1 + _back(ext, 3) * w0 + b
            o_ref[rows, :] = cv * _sigmoid(cv)
            return c

        lax.fori_loop(0, n_tiles, step, 0)

    return _strip_call(body, L, C // LANE, [(zx, LANE, col0 // LANE), (cw, LANE, 0), (cb, LANE, 0)],
                       [(SDS((L, C), F32), LANE, 0)], name)[0]


def _ssd_conv_bwd(dxbc, zx, cw, cb, col0, dzx, name):
    L, C = dxbc.shape
    n_tiles = L // ROW_TILE
    al = slice(None)

    def body(d_ref, x_ref, cw_ref, cb_ref, dzx_in_ref, o_ref, st_ref):
        w0, w1, w2, w3 = cw_ref[0:1, :], cw_ref[1:2, :], cw_ref[2:3, :], cw_ref[3:4, :]
        b = cb_ref[...]

        def step(i, c):
            r0 = pl.multiple_of(i * ROW_TILE, ROW_TILE)
            rows = pl.ds(r0, ROW_TILE)
            xv = x_ref[rows, :]
            xe = jnp.concatenate([_prev_rows(x_ref, r0, i, al), xv, _next_rows(x_ref, r0, i, n_tiles, al)], axis=0)
            x1, x2, x3 = pltpu.roll(xe, 1, axis=0), pltpu.roll(xe, 2, axis=0), pltpu.roll(xe, 3, axis=0)
            cv = (xe * w3 + x1 * w2 + x2 * w1 + x3 * w0 + b)[HALO:]
            de = jnp.concatenate([d_ref[rows, :], _next_rows(d_ref, r0, i, n_tiles, al)], axis=0)
            dc_ext = de * _dsilu(cv, _sigmoid(cv))
            dc = dc_ext[:ROW_TILE]
            o_ref[rows, :] = (dc * w3 + _fwd(dc_ext, 1) * w2 + _fwd(dc_ext, 2) * w1 + _fwd(dc_ext, 3) * w0).astype(BF16)
            s = lambda t: jnp.sum(t, axis=0, keepdims=True)
            t = slice(HALO, HALO + ROW_TILE)
            return (c[0] + s(dc * x3[t]), c[1] + s(dc * x2[t]), c[2] + s(dc * x1[t]), c[3] + s(dc * xv), c[4] + s(dc))

        z = jnp.zeros((1, LANE), F32)
        _store_rows(st_ref, lax.fori_loop(0, n_tiles, step, (z, z, z, z, z)))

    strip = lambda rows, off=0: pl.BlockSpec((rows, LANE), lambda j: (0, j + off))
    shifted = strip(L, col0 // LANE)
    return pl.pallas_call(
        body, grid=(C // LANE,), in_specs=[strip(L), shifted, strip(cw.shape[0]), strip(1), ANY],
        out_specs=[shifted, strip(8)], out_shape=[SDS(dzx.shape, dzx.dtype), SDS((8, C), F32)],
        input_output_aliases={4: 0}, name=name, compiler_params=_cp("parallel"))(dxbc, zx, cw, cb, dzx)


def _scan_constants(n_heads):
    hw = n_heads * HEAD_DIM
    col = np.arange(hw)
    ind = (col[None, :] // HEAD_DIM == np.arange(LANE)[:, None]).astype(np.float32)
    gcol = np.arange(GROUP_W)
    itile = (gcol[None, :] % CHUNK == np.arange(CHUNK)[:, None]).astype(np.float32)
    trit = (gcol[None, :] % CHUNK <= np.arange(CHUNK)[:, None]).astype(np.float32)
    tril = np.tril(np.ones((CHUNK, CHUNK), np.float32))
    bmask = (gcol[:, None] // HEAD_DIM == gcol[None, :] // HEAD_DIM).astype(np.float32)
    return (jnp.asarray(ind.T.copy(), BF16), jnp.asarray(itile), jnp.asarray(trit), jnp.asarray(tril, BF16),
            jnp.asarray(bmask))


def _softplus(x):
    return jnp.maximum(x, 0.0) + jnp.log(1.0 + jnp.exp(-jnp.abs(x)))


def _split3(x):
    hi = x.astype(BF16)
    r1 = x - hi.astype(F32)
    mid = r1.astype(BF16)
    return hi, mid, (r1 - mid.astype(F32)).astype(BF16)


def _dot_sel(x, sel, dims=None):
    if dims is None:
        mm = lambda p: jnp.dot(p, sel, preferred_element_type=F32)
    else:
        mm = lambda p: lax.dot_general(sel, p, dims, preferred_element_type=F32)
    hi, mid, lo = _split3(x)
    return (mm(lo) + mm(mid)) + mm(hi)


SEL_X = (((1,), (0,)), ((), ()))


def _head_lanes(v, g):
    h0 = HEADS_PER_GROUP * g
    return jnp.concatenate([jnp.broadcast_to(v[:, h0 + r:h0 + r + 1], (v.shape[0], HEAD_DIM))
                            for r in range(HEADS_PER_GROUP)], axis=1)


def _group_terms(g, dt, cs, cst, xbc_ref, trit, bmask, d_inner):
    gl = slice(g * GROUP_W, (g + 1) * GROUP_W)
    h0 = HEADS_PER_GROUP * g
    csl = _head_lanes(cs, g)
    dtx = _head_lanes(dt, g)
    rr = jnp.concatenate([cst[h0 + r:h0 + r + 1, :] for r in range(HEADS_PER_GROUP)], axis=1)
    lm = jnp.exp(jnp.where(trit > 0.0, csl - rr, -jnp.inf))
    xs = xbc_ref[:, gl]
    b = xbc_ref[:, d_inner + g * D_STATE: d_inner + (g + 1) * D_STATE]
    c = xbc_ref[:, d_inner + (N_GROUPS + g) * D_STATE: d_inner + (N_GROUPS + g + 1) * D_STATE]
    u = xs * dtx
    bb, cb = b.astype(BF16), c.astype(BF16)
    btile = jnp.concatenate([bb] * HEADS_PER_GROUP, axis=0)
    cbt = lax.dot_general(cb, btile, NT, preferred_element_type=F32)
    m = cbt * lm
    ub = u.astype(BF16)
    bdu = jnp.where(bmask > 0.0, jnp.concatenate([ub] * HEADS_PER_GROUP, axis=0), jnp.zeros((), BF16))
    c_last = csl[CHUNK - 1:CHUNK, :]
    return dict(gl=gl, csl=csl, dtx=dtx, lm=lm, xs=xs, bb=bb, cb=cb, u=u, btile=btile, m=m, bdu=bdu,
                e=jnp.exp(csl), dec=jnp.exp(c_last - csl), e_last=jnp.exp(c_last))


def _ssd_scan_fwd(zx, xbc, par, dexp, nw, consts, comm, name):
    L = xbc.shape[0]
    d_inner = dexp.shape[1]
    n_chunks = L // CHUNK
    dt_blk = zx.shape[1] // LANE - 1
    ind_t, itile_c, trit_c, tril_c, bmask_c = consts

    def body(xbc_ref, z_ref, dtr_ref, par_ref, dexp_ref, nw_ref, trit_ref, tril_ref, bmask_ref,
             yn_ref, yf_ref, st_out_ref, st_ref):
        @pl.when(pl.program_id(0) == 0)
        def _():
            st_ref[...] = jnp.zeros_like(st_ref)

        dt = _softplus(dtr_ref[...] + par_ref[0:1, :])
        a_head = -jnp.exp(par_ref[1:2, :])
        cs = _dot_sel(dt * a_head, tril_ref[...], SEL_X)
        cst = cs.T
        trit, bmask = trit_ref[...], bmask_ref[...]
        for g in range(N_GROUPS):
            t = _group_terms(g, dt, cs, cst, xbc_ref, trit, bmask, d_inner)
            p = st_ref[g]
            st_out_ref[0, g] = p
            y = jnp.dot(t["m"].astype(BF16), t["bdu"], preferred_element_type=F32)
            y = y + jnp.dot(t["cb"], p.astype(BF16), preferred_element_type=F32) * t["e"]
            st_new = lax.dot_general(t["bb"], (t["u"] * t["dec"]).astype(BF16), TN, preferred_element_type=F32)
            st_ref[g] = p * t["e_last"] + st_new
            yf_ref[:, t["gl"]] = y + t["xs"] * dexp_ref[:, t["gl"]]
        z = z_ref[...]
        y2 = yf_ref[...] * (z * _sigmoid(z))
        yn_ref[...] = (y2 * _rsq(y2) * nw_ref[...]).astype(BF16)

    row = lambda c: (c, 0)
    fix = lambda c: (0, 0)
    cspec = lambda a: pl.BlockSpec(a.shape, fix)
    return _carrier_call(
        body, comm, n_chunks,
        in_specs=[pl.BlockSpec((CHUNK, xbc.shape[1]), row), pl.BlockSpec((CHUNK, d_inner), row),
                  pl.BlockSpec((CHUNK, LANE), lambda c: (c, dt_blk)), cspec(par), cspec(dexp), cspec(nw),
                  cspec(trit_c), cspec(tril_c), cspec(bmask_c)],
        out_specs=[pl.BlockSpec((CHUNK, d_inner), row), pl.BlockSpec((CHUNK, d_inner), row),
                   pl.BlockSpec((1, N_GROUPS, D_STATE, GROUP_W), lambda c: (c, 0, 0, 0))],
        out_shape=[SDS((L, d_inner), BF16), SDS((L, d_inner), F32),
                   SDS((n_chunks, N_GROUPS, D_STATE, GROUP_W), F32)],
        scratch_shapes=[pltpu.VMEM((N_GROUPS, D_STATE, GROUP_W), F32)],
        name=name, args=(xbc, zx, zx, par, dexp, nw, trit_c, tril_c, bmask_c))


def _ssd_scan_bwd(dyn, yf, zx, xbc, states, par, dexp, nw, consts, comm, name):
    L = xbc.shape[0]
    d_inner = dexp.shape[1]
    n_chunks = L // CHUNK
    nz = zx.shape[1]
    dt_blk = nz // LANE - 1
    ind_t, itile_c, trit_c, tril_c, bmask_c = consts
    hslices = [slice(r * HEAD_DIM, (r + 1) * HEAD_DIM) for r in range(HEADS_PER_GROUP)]

    def body(dyn_ref, yf_ref, z_ref, dtr_ref, xbc_ref, st_in_ref, par_ref, dexp_ref, nw_ref, indt_ref,
             itile_ref, trit_ref, tril_ref, bmask_ref,
             dzx_ref, dxbc_ref, dnw_ref, dpar_ref, dq_ref, dyf_ref):
        @pl.when(pl.program_id(0) == 0)
        def _():
            dq_ref[...] = jnp.zeros_like(dq_ref)
            dnw_ref[...] = jnp.zeros_like(dnw_ref)
            dpar_ref[...] = jnp.zeros_like(dpar_ref)

        z, yfv, dynv = z_ref[...], yf_ref[...], dyn_ref[...]
        sz = _sigmoid(z)
        y2 = yfv * (z * sz)
        r = _rsq(y2)
        y2h = y2 * r
        dnw_ref[...] += jnp.sum(dynv * y2h, axis=0, keepdims=True)
        dyg = dynv * nw_ref[...]
        dy2 = r * (dyg - y2h * jnp.mean(dyg * y2h, axis=-1, keepdims=True))
        dzx_ref[:, 0:d_inner] = (dy2 * yfv * _dsilu(z, sz)).astype(BF16)
        dyf_ref[...] = dy2 * (z * sz)

        pre = dtr_ref[...] + par_ref[0:1, :]
        dt = _softplus(pre)
        a_head = -jnp.exp(par_ref[1:2, :])
        cs = _dot_sel(dt * a_head, tril_ref[...], SEL_X)
        cst = cs.T
        itile, trit, bmask = itile_ref[...], trit_ref[...], bmask_ref[...]
        dcs = jnp.zeros((CHUNK, LANE), F32)
        dcs_last = jnp.zeros((1, LANE), F32)
        ddt_u = jnp.zeros((CHUNK, LANE), F32)
        d_skip = jnp.zeros((1, LANE), F32)
        rsum = lambda v: jnp.sum(v, axis=0, keepdims=True)
        row8 = lax.broadcasted_iota(jnp.int32, (8, GROUP_W), 0)
        for g in range(N_GROUPS):
            t = _group_terms(g, dt, cs, cst, xbc_ref, trit, bmask, d_inner)
            gl, m, lm, u, bb, cb, e, dec, xs = (t[k] for k in ("gl", "m", "lm", "u", "bb", "cb", "e", "dec", "xs"))
            indt = indt_ref[gl, :]
            dy = dyf_ref[:, gl]
            dyb = dy.astype(BF16)
            p = st_in_ref[0, g]
            pb = p.astype(BF16)
            q = dq_ref[g]
            qb = q.astype(BF16)
            big = lax.dot_general(m.astype(BF16), dyb, TN, preferred_element_type=F32)
            du = jnp.zeros((CHUNK, GROUP_W), F32)
            for rh in range(HEADS_PER_GROUP):
                du = du + big[hslices[rh], :] * bmask[rh * HEAD_DIM:rh * HEAD_DIM + 1, :]
            dm = lax.dot_general(dyb, t["bdu"], NT, preferred_element_type=F32)
            w = dm * m
            dgt = (dm * lm).astype(BF16)
            dc = jnp.dot(dgt, t["btile"], preferred_element_type=F32)
            db_big = lax.dot_general(dgt, cb, TN, preferred_element_type=F32)
            db = db_big[hslices[0], :] + db_big[hslices[1], :] + db_big[hslices[2], :] + db_big[hslices[3], :]
            cp = jnp.dot(cb, pb, preferred_element_type=F32)
            dye = dy * e
            dyeb = dye.astype(BF16)
            dc = dc + lax.dot_general(dyeb, pb, NT, preferred_element_type=F32)
            dp = lax.dot_general(cb, dyeb, TN, preferred_element_type=F32)
            x2 = dye * cp
            bq = jnp.dot(bb, qb, preferred_element_type=F32)
            ud = u * dec
            du = du + bq * dec
            db = db + lax.dot_general(ud.astype(BF16), qb, NT, preferred_element_type=F32)
            x1 = bq * ud
            dq_ref[g] = dp + t["e_last"] * q
            x3 = rsum(q * p) * t["e_last"]
            red = _dot_sel(jnp.concatenate([w + x2 - x1, du * xs, itile * rsum(w)], axis=0), indt)
            dcs = dcs + red[0:CHUNK] - red[2 * CHUNK:3 * CHUNK]
            ddt_u = ddt_u + red[CHUNK:2 * CHUNK]
            tail = _dot_sel(jnp.where(row8 == 0, rsum(x1) + x3, jnp.where(row8 == 1, rsum(dy * xs), 0.0)), indt)
            dcs_last = dcs_last + tail[0:1]
            d_skip = d_skip + tail[1:2]
            dxbc_ref[:, gl] = du * t["dtx"] + dy * dexp_ref[:, gl]
            dxbc_ref[:, d_inner + g * D_STATE: d_inner + (g + 1) * D_STATE] = db
            dxbc_ref[:, d_inner + (N_GROUPS + g) * D_STATE: d_inner + (N_GROUPS + g + 1) * D_STATE] = dc
        last = lax.broadcasted_iota(jnp.int32, (CHUNK, LANE), 0) == CHUNK - 1
        dcs = dcs + jnp.where(last, dcs_last, 0.0)
        da = _dot_sel(dcs, tril_ref[...], TN)
        ddt = da * a_head + ddt_u
        heads = lax.broadcasted_iota(jnp.int32, (CHUNK, LANE), 1) < d_inner // HEAD_DIM
        ddt_raw = jnp.where(heads, ddt * _sigmoid(pre), 0.0)
        dzx_ref[:, nz - LANE:nz] = ddt_raw.astype(BF16)
        dpar_ref[0:1, :] += rsum(ddt_raw)
        dpar_ref[1:2, :] += rsum(da * dt) * a_head
        dpar_ref[2:3, :] += d_skip

    rev = lambda c: (n_chunks - 1 - c, 0)
    fix = lambda c: (0, 0)
    cspec = lambda a: pl.BlockSpec(a.shape, fix)
    nx = xbc.shape[1]
    return _carrier_call(
        body, comm, n_chunks,
        in_specs=[pl.BlockSpec((CHUNK, d_inner), rev), pl.BlockSpec((CHUNK, d_inner), rev),
                  pl.BlockSpec((CHUNK, d_inner), rev), pl.BlockSpec((CHUNK, LANE), lambda c: (n_chunks - 1 - c, dt_blk)),
                  pl.BlockSpec((CHUNK, nx), rev),
                  pl.BlockSpec((1, N_GROUPS, D_STATE, GROUP_W), lambda c: (n_chunks - 1 - c, 0, 0, 0)),
                  cspec(par), cspec(dexp), cspec(nw), cspec(ind_t), cspec(itile_c), cspec(trit_c),
                  cspec(tril_c), cspec(bmask_c)],
        out_specs=[pl.BlockSpec((CHUNK, nz), rev), pl.BlockSpec((CHUNK, nx), rev),
                   pl.BlockSpec((1, d_inner), fix), pl.BlockSpec((8, LANE), fix)],
        out_shape=[SDS((L, nz), BF16), SDS((L, nx), F32), SDS((1, d_inner), F32), SDS((8, LANE), F32)],
        scratch_shapes=[pltpu.VMEM((N_GROUPS, D_STATE, GROUP_W), F32), pltpu.VMEM((CHUNK, d_inner), F32)],
        name=name, args=(dyn, yf, zx, zx, xbc, states, par, dexp, nw, ind_t, itile_c, trit_c, tril_c, bmask_c))


def _adamw(w, m, v, g, name):
    R, C = w.shape
    tr = R
    for cand in (256, 128, 64, 32, 16, 8):
        if R % cand == 0:
            tr = cand
            break

    def body(w_ref, m_ref, v_ref, g_ref, d_ref, mo_ref, vo_ref):
        gv = g_ref[...]
        mn = ADAM_B1 * m_ref[...] + (1.0 - ADAM_B1) * gv
        vn = ADAM_B2 * v_ref[...] + (1.0 - ADAM_B2) * (gv * gv)
        m_hat = mn / (1.0 - ADAM_B1 ** ADAM_STEP)
        v_hat = vn / (1.0 - ADAM_B2 ** ADAM_STEP)
        d_ref[...] = -ADAM_LR * (m_hat / (jnp.sqrt(v_hat) + ADAM_EPS) + ADAM_WD * w_ref[...])
        mo_ref[...] = mn
        vo_ref[...] = vn

    blk = pl.BlockSpec((tr, C), lambda i: (i, 0))
    return pl.pallas_call(
        body, grid=(R // tr,), in_specs=[blk] * 4, out_specs=[blk] * 3, out_shape=[SDS((R, C), F32)] * 3,
        name=name, compiler_params=_cp("parallel"))(w, m, v, g)


def _sum_slots(parts, name):
    n, R, C = parts.shape
    tr = 128 if R % 128 == 0 else R

    def body(p_ref, o_ref):
        acc = p_ref[0]
        for k in range(1, n):
            acc = acc + p_ref[k]
        o_ref[...] = acc

    return pl.pallas_call(
        body, grid=(R // tr,), in_specs=[pl.BlockSpec((n, tr, C), lambda i: (0, i, 0))],
        out_specs=pl.BlockSpec((tr, C), lambda i: (i, 0)), out_shape=SDS((R, C), F32),
        name=name, compiler_params=_cp("parallel"))(parts)


def _add_core_halves(where, g, r, name):
    _, n, a, b = g.shape
    ta = a // 2

    def body(w_ref, g_ref, r_ref, o_ref):
        o_ref[...] = (g_ref[...].astype(F32) + r_ref[...].astype(F32)).astype(BF16)

    blk = lambda f: pl.BlockSpec((None, None, ta, b), f)
    mine = lambda s, l, w: (s, l, 0, 0)
    return pl.pallas_call(
        body, grid_spec=pltpu.PrefetchScalarGridSpec(
            num_scalar_prefetch=1, grid=(4, n),
            in_specs=[blk(lambda s, l, w: (s, l, w[1], 0)), blk(mine)], out_specs=blk(mine)),
        out_shape=SDS((4, n, ta, b), BF16), name=name,
        compiler_params=_cp("parallel", "parallel"))(where, g, r)


def _sum_shard(where, g, r, rr, into, layer, n_layers, name):
    _, _, a, b = g.shape
    ta = a // 2

    def body(w_ref, g_ref, r_ref, rr_ref, *refs):
        f = lambda v: v.astype(F32)
        refs[-1][...] = (((f(g_ref[...]) + f(r_ref[...])) + f(rr_ref[0])) + f(rr_ref[1])) + f(rr_ref[2])

    more = [] if into is None else [into]
    return pl.pallas_call(
        body, grid_spec=pltpu.PrefetchScalarGridSpec(
            num_scalar_prefetch=1, grid=(1,),
            in_specs=[pl.BlockSpec((None, None, ta, b), lambda l, w: (w[0], 0, w[1], 0)),
                      pl.BlockSpec((None, None, ta, b), lambda l, w: (w[0], 0, 0, 0)),
                      pl.BlockSpec((3, None, ta, b), lambda l, w: (0, 0, 0, 0))] + [ANY] * len(more),
            out_specs=pl.BlockSpec((None, ta, b), lambda l, w: (layer, w[1], 0))),
        out_shape=SDS((n_layers, a, b), F32), name=name, input_output_aliases={4: 0} if more else {},
        compiler_params=_cp("arbitrary"))(where, g, r, rr, *more)


def _me():
    return lax.axis_index("x"), lax.axis_index("y"), lax.axis_index("c")


def _chip_peers(x, y):
    return [(1 - x, y), (x, 1 - y), (1 - x, 1 - y)]


def _rcopy(src, dst, send_sems, recv_sems, k, to):
    return pltpu.make_async_remote_copy(src_ref=src, dst_ref=dst, send_sem=send_sems.at[k], recv_sem=recv_sems.at[k],
                                        device_id=to, device_id_type=MESH)


def _row_half(ref, c, lead=()):
    a = ref.shape[len(lead) + 1]
    return ref.at[(*lead, slice(None), pl.ds(c * (a // 2), a // 2))]


class _Comm(NamedTuple):
    ins: list
    out_shapes: list
    n_sems: int
    start: Callable
    finish: Callable


def _sem_scratch(comm):
    return [pltpu.SemaphoreType.DMA((comm.n_sems,)), pltpu.SemaphoreType.DMA((comm.n_sems,))]


def _run_comm(comm, name):
    n_in, n_out = len(comm.ins), len(comm.out_shapes)

    def body(*refs):
        ins, outs, sems = refs[:n_in], refs[n_in:n_in + n_out], refs[n_in + n_out:]
        comm.start(ins, outs, *sems)
        comm.finish(ins, outs, *sems)

    return pl.pallas_call(body, in_specs=[ANY] * n_in, out_specs=[ANY] * n_out, out_shape=comm.out_shapes,
                          scratch_shapes=_sem_scratch(comm), name=name)(*comm.ins)


def _carrier_call(compute, comm, n_steps, in_specs, out_specs, out_shape, scratch_shapes, name, args):
    if comm is None:
        return pl.pallas_call(compute, grid=(n_steps,), in_specs=in_specs, out_specs=out_specs, out_shape=out_shape,
                              scratch_shapes=scratch_shapes, name=name, compiler_params=_cp("arbitrary"))(*args), []
    n_in, n_out, n_scr = len(in_specs), len(out_specs), len(scratch_shapes)
    n_ci, n_co = len(comm.ins), len(comm.out_shapes)

    def body(*refs):
        ins, cins = refs[:n_in], refs[n_in:n_in + n_ci]
        o = n_in + n_ci
        outs, couts = refs[o:o + n_out], refs[o + n_out:o + n_out + n_co]
        s = o + n_out + n_co
        scratch, sems = refs[s:s + n_scr], refs[s + n_scr:]

        @pl.when(pl.program_id(0) == 0)
        def _():
            comm.start(cins, couts, *sems)

        compute(*ins, *outs, *scratch)

        @pl.when(pl.program_id(0) == n_steps - 1)
        def _():
            comm.finish(cins, couts, *sems)

    res = pl.pallas_call(
        body, grid=(n_steps,), in_specs=list(in_specs) + [ANY] * n_ci, out_specs=list(out_specs) + [ANY] * n_co,
        out_shape=list(out_shape) + list(comm.out_shapes), scratch_shapes=list(scratch_shapes) + _sem_scratch(comm),
        name=name, compiler_params=_cp("arbitrary"))(*args, *comm.ins)
    return res[:n_out], res[n_out:]


def _allgather_plan(mine, small=None):
    n = len(mine)
    ins = list(mine) + ([] if small is None else [small])
    out_shapes = [SDS((4,) + m.shape, BF16) for m in mine] + ([] if small is None else [SDS((4,) + small.shape, F32)])
    sem = lambda t, k: 7 * t + k

    def first_copies(ins_r, outs_r, send, recv):
        x, y, c = _me()
        q = 2 * x + y
        cps = []
        for j, chip in enumerate(_chip_peers(x, y)):
            for t in range(n):
                cps.append(_rcopy(_row_half(ins_r[t], c), _row_half(outs_r[t], c, (q,)), send, recv, sem(t, j),
                                  (*chip, c)))
            if small is not None:
                cps.append(_rcopy(ins_r[n], outs_r[n].at[q], send, recv, sem(n, j), (*chip, c)))
        for t in range(len(ins)):
            cps.append(_rcopy(ins_r[t], outs_r[t].at[q], send, recv, sem(t, 6), (x, y, 1 - c)))
        return cps

    def start(ins_r, outs_r, send, recv):
        for cp in first_copies(ins_r, outs_r, send, recv):
            cp.start()

    def finish(ins_r, outs_r, send, recv):
        x, y, c = _me()
        sib = (x, y, 1 - c)
        chips = _chip_peers(x, y)
        passed = []
        for j, (px, py) in enumerate(chips):
            for t in range(n):
                blk = _row_half(outs_r[t], c, (2 * px + py,))
                _rcopy(blk, blk, send, recv, sem(t, j), sib).wait_recv()
                cp = _rcopy(blk, blk, send, recv, sem(t, 3 + j), sib)
                cp.start()
                passed.append(cp)
        for j, (px, py) in enumerate(chips):
            for t in range(n):
                blk = _row_half(outs_r[t], 1 - c, (2 * px + py,))
                _rcopy(blk, blk, send, recv, sem(t, 3 + j), sib).wait_recv()
            if small is not None:
                sblk = outs_r[n].at[2 * px + py]
                _rcopy(sblk, sblk, send, recv, sem(n, j), sib).wait_recv()
        for t in range(len(ins)):
            own = outs_r[t].at[2 * x + y]
            _rcopy(own, own, send, recv, sem(t, 6), sib).wait_recv()
        for cp in first_copies(ins_r, outs_r, send, recv) + passed:
            cp.wait_send()

    return _Comm(ins, out_shapes, 7 * len(ins), start, finish)


def _grads_to_sibling_plan(grads):
    n = len(grads)

    def copies(ins_r, outs_r, send, recv):
        x, y, c = _me()
        return [_rcopy(_row_half(ins_r[t], 1 - c, (slice(None),)), outs_r[t], send, recv, t, (x, y, 1 - c))
                for t in range(n)]

    def start(*a):
        for cp in copies(*a):
            cp.start()

    def finish(*a):
        for cp in copies(*a):
            cp.wait()

    out_shapes = [SDS((4, g.shape[1], g.shape[2] // 2, g.shape[3]), BF16) for g in grads]
    return _Comm(list(grads), out_shapes, n, start, finish)


def _grads_to_chips_plan(psums):
    n = len(psums)

    def copies(ins_r, outs_r, send, recv):
        x, y, c = _me()
        return [_rcopy(ins_r[t].at[2 * px + py], outs_r[t].at[j], send, recv, 3 * t + j, (px, py, c))
                for j, (px, py) in enumerate(_chip_peers(x, y)) for t in range(n)]

    def start(*a):
        for cp in copies(*a):
            cp.start()

    def finish(*a):
        for cp in copies(*a):
            cp.wait()

    return _Comm(list(psums), [SDS((3,) + p.shape[1:], BF16) for p in psums], 3 * n, start, finish)


def _swap_halves(sums):
    n = len(sums)

    def body(*refs):
        out_refs = refs[n:2 * n]
        send_sems, recv_sems = refs[2 * n:]
        x, y, c = _me()
        sib = (x, y, 1 - c)
        cps = [_rcopy(_row_half(out_refs[t], c), _row_half(out_refs[t], c), send_sems, recv_sems, t, sib)
               for t in range(n)]
        for cp in cps:
            cp.start()
        for t in range(n):
            other = _row_half(out_refs[t], 1 - c)
            _rcopy(other, other, send_sems, recv_sems, t, sib).wait_recv()
        for cp in cps:
            cp.wait_send()

    return pl.pallas_call(
        body, in_specs=[ANY] * n, out_specs=[ANY] * n, out_shape=[SDS(s.shape, F32) for s in sums],
        input_output_aliases={t: t for t in range(n)},
        scratch_shapes=[pltpu.SemaphoreType.DMA((n,)), pltpu.SemaphoreType.DMA((n,))],
        name="swap_halves")(*sums)


def _allgather_small(part):
    def body(p_ref, out_ref, send_sems, recv_sems, local_sem):
        x, y, c = _me()
        me = 4 * x + 2 * y + c
        own = pltpu.make_async_copy(p_ref, out_ref.at[me], local_sem.at[0])
        own.start()
        sends = []
        for k in range(1, 8):
            fx, fy, fc = (k >> 2) & 1, (k >> 1) & 1, k & 1
            to = (x ^ fx, y ^ fy, c ^ fc)
            sends.append(_rcopy(p_ref, out_ref.at[me], send_sems, recv_sems, k - 1, to))
        for cp in sends:
            cp.start()
        for k in range(1, 8):
            slot = out_ref.at[me ^ k]
            _rcopy(slot, slot, send_sems, recv_sems, k - 1, (x, y, c)).wait_recv()
        for cp in sends:
            cp.wait_send()
        own.wait()

    return pl.pallas_call(
        body, in_specs=[ANY], out_specs=ANY, out_shape=SDS((8,) + part.shape, F32),
        scratch_shapes=[pltpu.SemaphoreType.DMA((7,)), pltpu.SemaphoreType.DMA((7,)), pltpu.SemaphoreType.DMA((1,))],
        name="allgather_small")(part)


BIG = (("ssd_w_in", 2), ("ssd_w_out", 1), ("sc_w_in", 2), ("sc_w_out", 1), ("ffn_w_up", 2), ("ffn_w_down", 1))


def _to_shards(full, axis):
    A, B = full.shape
    if axis == 2:
        return full.reshape(A, 4, B // 4).transpose(1, 0, 2)
    return full.reshape(4, A // 4, B)


def _from_shards(shards, axis):
    _, a, b = shards.shape
    if axis == 2:
        return shards.transpose(1, 0, 2).reshape(a, 4 * b)
    return shards.reshape(4 * a, b)


def _interleave(w, parts):
    lead, n = w.shape[:-1], w.shape[-1]
    return w.reshape(*lead, parts, n // (parts * LANE), LANE).swapaxes(-2, -3).reshape(*lead, n)


def _deinterleave(w, parts):
    lead, n = w.shape[:-1], w.shape[-1]
    return w.reshape(*lead, n // (parts * LANE), parts, LANE).swapaxes(-2, -3).reshape(*lead, n)


def _pack_rows(vectors, width, row_multiple):
    flat = jnp.concatenate(vectors, axis=-1)
    n = flat.shape[-1]
    unit = width * row_multiple
    total = -(-n // unit) * unit
    flat = jnp.pad(flat, [(0, 0)] * (flat.ndim - 1) + [(0, total - n)])
    return flat.reshape(*flat.shape[:-1], total // width, width)


def _unpack(flat, shapes):
    out, off = [], 0
    for s in shapes:
        n = int(np.prod(s))
        out.append(flat[..., off:off + n].reshape(*flat.shape[:-1], *s))
        off += n
    return out


def _memo(fn):
    cache = {}

    def wrapped(k):
        if k not in cache:
            cache[k] = fn(k)
        return cache[k]

    return wrapped


def _row(v):
    return v.reshape(1, -1)


def _pad_rows(w, rows=8):
    return jnp.pad(w, ((0, rows - w.shape[0]), (0, 0)))


def _ffn_fwd(x, g_pre, g_post, w_up, cw, cb, w_down, tag, comm_up=None, comm_down=None):
    (up, hn, a), got_up = _ffn_up(x, g_pre, w_up, cw, cb, "ffn_up" + tag, comm_up)
    (f, x_new), got_down = _matmul_norm_res(a, w_down, x, g_post, "ffn_down" + tag, comm_down)
    return x_new, (x, hn, up, a, f), got_up, got_down


def _ffn_bwd(dx, saved, g_pre, g_post, w_up, cw, cb, w_down, tag):
    x, hn, up, a, f = saved
    da, df, dg_post = _postnorm_bwd_matmul_nt(dx, f, g_post, w_down, BF16, "ffn_down_dx" + tag)
    dw_down = _matmul_tn(a, df, "ffn_down_dw" + tag)
    dup, stats = _ffn_mid_bwd(da, up, cw, cb, "ffn_mid_bwd" + tag)
    dx_in, dg_pre = _ffn_up_dx(dup, w_up, x, g_pre, dx, "ffn_up_dx" + tag)
    dw_up = _ffn_up_dw(hn, dup, "ffn_up_dw" + tag)
    return dx_in, dict(g_pre=dg_pre, g_post=dg_post, w_up=dw_up, w_down=dw_down, cw=stats[0:3], cb=stats[3])


def _sc_fwd(x, g_pre, g_post, w_in, cw, w_out, tag, comm_in=None, comm_out=None):
    (bcv, hn), got_in = _norm_matmul(x, g_pre, w_in, BF16, "sc_in" + tag, comm_in)
    q = _sc_mid_fwd(bcv, cw, "sc_mid_fwd" + tag)
    (m, x_new), got_out = _matmul_norm_res(q, w_out, x, g_post, "sc_out" + tag, comm_out)
    return x_new, (x, hn, bcv, q, m), got_in, got_out


def _sc_bwd(dx, saved, g_pre, g_post, w_in, cw, w_out, tag):
    x, hn, bcv, q, m = saved
    dq, dm, dg_post = _postnorm_bwd_matmul_nt(dx, m, g_post, w_out, BF16, "sc_out_dx" + tag)
    dw_out = _matmul_tn(q, dm, "sc_out_dw" + tag)
    dbcv, stats = _sc_mid_bwd(dq, bcv, cw, "sc_mid_bwd" + tag)
    (dx_in, dg_pre), _ = _matmul_nt_prenorm_bwd(dbcv, w_in, x, g_pre, dx, "sc_in_dx" + tag)
    dw_in = _matmul_tn(hn, dbcv, "sc_in_dw" + tag)
    return dx_in, dict(g_pre=dg_pre, g_post=dg_post, w_in=dw_in, w_out=dw_out, cw=stats[0:3])


def _ssd_fwd(x, g_pre, g_post, w_in, cw, cb, par, dexp, nw, w_out, consts, comm, tag, comm_in=None):
    d_inner = dexp.shape[1]
    (zx, hn), got_in = _norm_matmul(x, g_pre, w_in, F32, "ssd_in" + tag, comm_in)
    xbc = _ssd_conv_fwd(zx, cw, cb, d_inner, cw.shape[1], "ssd_conv_fwd" + tag)
    (yn, yf, states), got = _ssd_scan_fwd(zx, xbc, par, dexp, nw, consts, comm, "ssd_scan_fwd" + tag)
    if callable(w_out):
        w_out = w_out(got_in)
    (m, x_new), _ = _matmul_norm_res(yn, w_out, x, g_post, "ssd_out" + tag)
    return x_new, (x, hn, zx, xbc, yn, yf, states, m), got


def _ssd_bwd(dx, saved, g_pre, g_post, w_in, cw, cb, par, dexp, nw, w_out, consts, comm, tag, comm_dx=None):
    x, hn, zx, xbc, yn, yf, states, m = saved
    d_inner = dexp.shape[1]
    dyn, dm, dg_post = _postnorm_bwd_matmul_nt(dx, m, g_post, w_out, F32, "ssd_out_dx" + tag)
    dw_out = _matmul_tn(yn, dm, "ssd_out_dw" + tag)
    (dzx, dxbc, dnw, dpar), got = _ssd_scan_bwd(dyn, yf, zx, xbc, states, par, dexp, nw, consts, comm(dw_out),
                                                "ssd_scan_bwd" + tag)
    dzx, stats = _ssd_conv_bwd(dxbc, zx, cw, cb, d_inner, dzx, "ssd_conv_bwd" + tag)
    dw_in = _matmul_tn(hn, dzx, "ssd_in_dw" + tag)
    (dx_in, dg_pre), got_dx = _matmul_nt_prenorm_bwd(dzx, w_in, x, g_pre, dx, "ssd_in_dx" + tag,
                                                     None if comm_dx is None else comm_dx(dw_in))
    n_heads = d_inner // HEAD_DIM
    grads = dict(g_pre=dg_pre, g_post=dg_post, w_in=dw_in, w_out=dw_out, cw=stats[0:4], cb=stats[4],
                 dt_bias=dpar[0, :n_heads], a_log=dpar[1, :n_heads], d=dpar[2, :n_heads], nw=dnw[0])
    return dx_in, grads, got, got_dx


def kernel(x, mix_pre_g, mix_post_g, ffn_pre_g, ffn_post_g, ssd_w_in, ssd_conv_w, ssd_conv_b, ssd_dt_bias, ssd_A_log, ssd_D, ssd_norm_w, ssd_w_out, sc_w_in, sc_conv_w, sc_w_out, ffn_w_up, ffn_conv_w, ffn_conv_b, ffn_w_down, loss_target, m_mix_pre_g, m_mix_post_g, m_ffn_pre_g, m_ffn_post_g, m_ssd_w_in, m_ssd_conv_w, m_ssd_conv_b, m_ssd_dt_bias, m_ssd_A_log, m_ssd_D, m_ssd_norm_w, m_ssd_w_out, m_sc_w_in, m_sc_conv_w, m_sc_w_out, m_ffn_w_up, m_ffn_conv_w, m_ffn_conv_b, m_ffn_w_down, v_mix_pre_g, v_mix_post_g, v_ffn_pre_g, v_ffn_post_g, v_ssd_w_in, v_ssd_conv_w, v_ssd_conv_b, v_ssd_dt_bias, v_ssd_A_log, v_ssd_D, v_ssd_norm_w, v_ssd_w_out, v_sc_w_in, v_sc_conv_w, v_sc_w_out, v_ffn_w_up, v_ffn_conv_w, v_ffn_conv_b, v_ffn_w_down):
    names = ["mix_pre_g", "mix_post_g", "ffn_pre_g", "ffn_post_g", "ssd_w_in", "ssd_conv_w", "ssd_conv_b",
             "ssd_dt_bias", "ssd_A_log", "ssd_D", "ssd_norm_w", "ssd_w_out", "sc_w_in", "sc_conv_w", "sc_w_out",
             "ffn_w_up", "ffn_conv_w", "ffn_conv_b", "ffn_w_down"]
    env = locals()
    wts = {n: env[n] for n in names}
    mom = {n: env["m_" + n] for n in names}
    var = {n: env["v_" + n] for n in names}

    depth, d_model = mix_pre_g.shape
    n_ssd, n_heads = ssd_dt_bias.shape
    n_sc = sc_conv_w.shape[0]
    d_inner = n_heads * HEAD_DIM
    conv_dim = d_inner + 2 * N_GROUPS * D_STATE
    ssd_in_dim = d_inner + conv_dim + n_heads
    ssd_in_pad = d_inner + conv_dim + LANE
    q_chip = 2 * lax.axis_index("x") + lax.axis_index("y")
    core = lax.axis_index("c")

    assert depth == 4 and n_ssd == 2 and n_sc == 2, "the exchange schedule is written for this trunk"
    ssd_items = lambda j: [("ssd_w_in", j), ("ssd_w_out", j)]
    sc_items = lambda j: [("sc_w_in", j), ("sc_w_out", j)]
    ffn_items = lambda i: [("ffn_w_up", i), ("ffn_w_down", i)]
    gather_first = [("ssd_w_in", 0)]
    gather_in_ssd_in = {0: [("ssd_w_out", 0)]}
    gather_in_scan = {0: ffn_items(0) + sc_items(0), 2: ffn_items(2) + sc_items(1)}
    gather_in_ffn = {0: ([("ffn_w_up", 1)], [("ffn_w_down", 1)]), 1: ([("ssd_w_in", 1)], []),
                     2: ([("ffn_w_up", 3)], [("ffn_w_down", 3)])}
    gather_in_sc = {1: ([("ssd_w_out", 1)], [])}
    reduce_in_scan = {2: ffn_items(3) + sc_items(1) + ffn_items(2),
                      0: ssd_items(1) + ffn_items(1) + sc_items(0) + ffn_items(0) + [("ssd_w_out", 0)]}
    reduce_in_dx = {0: [("ssd_w_in", 0)]}
    axis_of = dict(BIG)
    gathered = {}

    def gather_plan(items, small=None):
        mine = [wts[n][layer:layer + 1].astype(BF16) for n, layer in items]
        return _allgather_plan(mine, small) if items else None

    def gather_done(items, results):
        for item, buf in zip(items, results):
            gathered[item] = buf[:, 0]

    def full(n, layer):
        return _from_shards(gathered[(n, layer)], axis_of[n])

    conv_names = ["ssd_conv_w", "sc_conv_w", "ffn_conv_w"]
    conv_shapes = [wts[n].shape for n in conv_names]
    small_mine = _pack_rows([wts[n].reshape(-1) for n in conv_names], LANE, 8)
    *results, small_all = _run_comm(gather_plan(gather_first, small_mine), "allgather_first")
    gather_done(gather_first, results)
    conv_full = {}
    for n, f, s in zip(conv_names, _unpack(small_all.reshape(4, -1), conv_shapes), conv_shapes):
        conv_full[n] = f.transpose(1, 2, 0, 3).reshape(s[0], s[1], 4 * s[2])

    consts = _scan_constants(n_heads)

    def ssd_args(j):
        par = jnp.zeros((8, LANE), F32).at[0, :n_heads].set(ssd_dt_bias[j]).at[1, :n_heads].set(ssd_A_log[j])
        dexp = jnp.repeat(ssd_D[j], HEAD_DIM).reshape(1, d_inner)
        w_in = jnp.pad(full("ssd_w_in", j), ((0, 0), (0, ssd_in_pad - ssd_in_dim)))
        return (w_in, _pad_rows(conv_full["ssd_conv_w"][j]), _row(ssd_conv_b[j]), par, dexp, _row(ssd_norm_w[j]))

    def sc_args(j):
        return (_interleave(full("sc_w_in", j), 3), _pad_rows(conv_full["sc_conv_w"][j]), full("sc_w_out", j))

    def ffn_args(i):
        return (gathered[("ffn_w_up", i)], _pad_rows(conv_full["ffn_conv_w"][i]), _row(ffn_conv_b[i]),
                full("ffn_w_down", i))

    ssd_args, sc_args, ffn_args = _memo(ssd_args), _memo(sc_args), _memo(ffn_args)

    h = x[0]
    saved = []
    for i in range(depth):
        j = i // 2
        gp, gq = _row(mix_pre_g[i]), _row(mix_post_g[i])
        if i % 2 == 0:
            items_in = gather_in_ssd_in.get(i, [])

            def w_out_when_here(got_in, items_in=items_in, j=j):
                gather_done(items_in, got_in)
                return full("ssd_w_out", j)

            h, sv, results = _ssd_fwd(h, gp, gq, *ssd_args(j), w_out_when_here, consts, gather_plan(gather_in_scan[i]),
                                      tag="", comm_in=gather_plan(items_in))
            gather_done(gather_in_scan[i], results)
        else:
            items_in, items_out = gather_in_sc.get(i, ([], []))
            h, sv, got_in, got_out = _sc_fwd(h, gp, gq, *sc_args(j), tag="", comm_in=gather_plan(items_in),
                                             comm_out=gather_plan(items_out))
            gather_done(items_in, got_in)
            gather_done(items_out, got_out)
        items_up, items_down = gather_in_ffn.get(i, ([], []))
        h, sv2, got_up, got_down = _ffn_fwd(h, _row(ffn_pre_g[i]), _row(ffn_post_g[i]), *ffn_args(i), tag="",
                                            comm_up=gather_plan(items_up), comm_down=gather_plan(items_down))
        gather_done(items_up, got_up)
        gather_done(items_down, got_down)
        saved.append((sv, sv2))
    dh, loss_part = _loss_head(h, loss_target[0], "loss_head")

    mix_grads, ffn_grads = [None] * depth, [None] * depth
    where = jnp.stack([q_chip, core]).astype(jnp.int32)

    def shard_grad(n, layer):
        if n == "ffn_w_up":
            g = ffn_grads[layer]["w_up"]
        elif n == "ffn_w_down":
            g = _to_shards(ffn_grads[layer]["w_down"], 1)
        elif n == "ssd_w_in":
            g = _to_shards(early[(n, layer)][:, :ssd_in_dim], 2)
        elif n == "ssd_w_out":
            g = _to_shards(early[(n, layer)], 1)
        elif n == "sc_w_in":
            g = _to_shards(_deinterleave(mix_grads[2 * layer + 1]["w_in"], 3), 2)
        else:
            g = _to_shards(mix_grads[2 * layer + 1]["w_out"], 1)
        return g[:, None]

    def reduce_begin(items, tag):
        by_shard = [shard_grad(n, layer) for n, layer in items]
        from_sib = _run_comm(_grads_to_sibling_plan(by_shard), "grads_to_sibling" + tag)
        chip_sums = [_add_core_halves(where, g, r, "add_core_halves_%s%d" % item)
                     for item, g, r in zip(items, by_shard, from_sib)]
        return by_shard, from_sib, _grads_to_chips_plan(chip_sums)

    sums = {}
    early = {}

    def riding(items, tag):
        by_shard, from_sib, plan = reduce_begin(items, tag)

        def arrived(from_chips):
            for (n, layer), g, r, rr in zip(items, by_shard, from_sib, from_chips):
                sums[n] = _sum_shard(where, g, r, rr, sums.get(n), layer, wts[n].shape[0],
                                     "sum_shard_%s%d" % (n, layer))

        return plan, arrived

    for i in reversed(range(depth)):
        j = i // 2
        sv, sv2 = saved[i]
        dh, ffn_grads[i] = _ffn_bwd(dh, sv2, _row(ffn_pre_g[i]), _row(ffn_post_g[i]), *ffn_args(i), tag="")
        gp, gq = _row(mix_pre_g[i]), _row(mix_post_g[i])
        if i % 2 == 0:
            then = {}

            def in_scan(dw_out, i=i, j=j, then=then):
                early[("ssd_w_out", j)] = dw_out
                plan, then["scan"] = riding(reduce_in_scan[i], "_%d" % i)
                return plan

            def in_dx(dw_in, i=i, j=j, then=then):
                early[("ssd_w_in", j)] = dw_in
                plan, then["dx"] = riding(reduce_in_dx[i], "_dx%d" % i) if i in reduce_in_dx else (None, None)
                return plan

            dh, mix_grads[i], got, got_dx = _ssd_bwd(dh, sv, gp, gq, *ssd_args(j), full("ssd_w_out", j), consts,
                                                     in_scan, tag="", comm_dx=in_dx)
            then["scan"](got)
            if then["dx"] is not None:
                then["dx"](got_dx)
        else:
            w_in, scw, w_out = sc_args(j)
            dh, mix_grads[i] = _sc_bwd(dh, sv, gp, gq, w_in, scw, w_out, tag="")
    grad_x = dh[None]
    big_grads = dict(zip([n for n, _ in BIG], _swap_halves([sums[n] for n, _ in BIG])))
    ssd_l = [mix_grads[i] for i in range(0, depth, 2)]
    sc_l = [mix_grads[i] for i in range(1, depth, 2)]
    stack = lambda layers, k: jnp.stack([g[k] for g in layers])

    small_names = ["mix_pre_g", "mix_post_g", "ffn_pre_g", "ffn_post_g", "ssd_conv_w", "ssd_conv_b", "ssd_dt_bias",
                   "ssd_A_log", "ssd_D", "ssd_norm_w", "sc_conv_w", "ffn_conv_w", "ffn_conv_b"]
    small_local = {
        "mix_pre_g": jnp.concatenate([g["g_pre"] for g in mix_grads]),
        "mix_post_g": jnp.concatenate([g["g_post"] for g in mix_grads]),
        "ffn_pre_g": jnp.concatenate([g["g_pre"] for g in ffn_grads]),
        "ffn_post_g": jnp.concatenate([g["g_post"] for g in ffn_grads]),
        "ssd_conv_w": stack(ssd_l, "cw"), "ssd_conv_b": stack(ssd_l, "cb"), "ssd_dt_bias": stack(ssd_l, "dt_bias"),
        "ssd_A_log": stack(ssd_l, "a_log"), "ssd_D": stack(ssd_l, "d"), "ssd_norm_w": stack(ssd_l, "nw"),
        "sc_conv_w": stack(sc_l, "cw"), "ffn_conv_w": stack(ffn_grads, "cw"), "ffn_conv_b": stack(ffn_grads, "cb"),
    }
    small_full_shapes = [small_local[n].shape for n in small_names]
    spack = _pack_rows([small_local[n].reshape(-1) for n in small_names] + [loss_part.reshape(-1)], LANE, 8)
    stotal = _sum_slots(_allgather_small(spack), "sum_small").reshape(-1)
    small_grads = dict(zip(small_names, _unpack(stotal, small_full_shapes)))
    loss = stotal[sum(int(np.prod(s)) for s in small_full_shapes)]
    for n in conv_names:
        width = wts[n].shape[-1]
        small_grads[n] = lax.dynamic_slice_in_dim(small_grads[n], q_chip * width, width, axis=2)

    grads, delta, new_m, new_v = {}, {}, {}, {}
    for n, _ in BIG:
        s = wts[n].shape
        two_d = lambda a: a.reshape(-1, s[-1])
        grads[n] = big_grads[n]
        d, mn, vn = _adamw(two_d(wts[n]), two_d(mom[n]), two_d(var[n]), two_d(grads[n]), "adamw_" + n)
        delta[n], new_m[n], new_v[n] = d.reshape(s), mn.reshape(s), vn.reshape(s)
    small_shapes = [wts[n].shape for n in small_names]
    pk = lambda d: _pack_rows([d[n].reshape(-1) for n in small_names], LANE, 8)
    for n in small_names:
        grads[n] = small_grads[n].reshape(wts[n].shape)
    d, mn, vn = _adamw(pk(wts), pk(mom), pk(var), pk(grads), "adamw_small")
    for out, packed in ((delta, d), (new_m, mn), (new_v, vn)):
        out.update(zip(small_names, _unpack(packed.reshape(-1), small_shapes)))

    return (loss, grad_x, *[grads[n] for n in names], *[delta[n] for n in names], *[new_m[n] for n in names],
            *[new_v[n] for n in names])
```

```python
from typing import Callable, NamedTuple

import jax
import jax.numpy as jnp
import numpy as np
from jax import lax
from jax.experimental import pallas as pl
from jax.experimental.pallas import tpu as pltpu

F32 = jnp.float32
BF16 = jnp.bfloat16
SDS = jax.ShapeDtypeStruct
MESH = pl.DeviceIdType.MESH
ANY = pl.BlockSpec(memory_space=pl.ANY)

EPS = 1e-6
CHUNK = 64
HEAD_DIM = 64
N_GROUPS = 8
D_STATE = 128
HEADS_PER_GROUP = 4
GROUP_W = HEADS_PER_GROUP * HEAD_DIM
LANE = 128
ROW_TILE = 128
HALO = 8
VMEM_LIMIT = 56 * 1024 * 1024

ADAM_LR = 0.001
ADAM_B1 = 0.9
ADAM_B2 = 0.999
ADAM_EPS = 1e-08
ADAM_WD = 0.01
ADAM_STEP = 10

NT = (((1,), (1,)), ((), ()))
TN = (((0,), (0,)), ((), ()))


def _cp(*sem):
    return pltpu.CompilerParams(dimension_semantics=sem or None, vmem_limit_bytes=VMEM_LIMIT)


def _sigmoid(x):
    return 1.0 / (1.0 + jnp.exp(-x))


def _dsilu(x, s):
    return s * (1.0 + x * (1.0 - s))


def _rsq(x):
    return lax.rsqrt(jnp.mean(x * x, axis=-1, keepdims=True) + EPS)


MM_ROWS = 256


class _Epilogue(NamedTuple):
    ins: list
    cols: int
    dtype: object
    fn: Callable


def _norm_matmul(x, g, w, out_dtype, name, comm=None, epilogue=None):
    L, D = x.shape
    N = w.shape[1]
    tm = min(MM_ROWS, L)
    n_extra = 0 if epilogue is None else len(epilogue.ins)

    def body(x_ref, g_ref, w_ref, *refs):
        extra, (o_ref, hn_ref), rest = refs[:n_extra], refs[n_extra:n_extra + 2], refs[n_extra + 2:]
        xv = x_ref[...]
        hn = (xv * _rsq(xv) * g_ref[...]).astype(BF16)
        hn_ref[...] = hn
        o_ref[...] = jnp.dot(hn, w_ref[...], preferred_element_type=F32).astype(out_dtype)
        if epilogue is not None:
            res_ref, tail_ref = rest

            @pl.when(pl.program_id(0) == 0)
            def _():
                tail_ref[...] = jnp.zeros_like(tail_ref)

            epilogue.fn(o_ref, extra, res_ref, tail_ref, tm)

    row = lambda i: (i, 0)
    fix = lambda i: (0, 0)
    in_specs = [pl.BlockSpec((tm, D), row), pl.BlockSpec((1, D), fix), pl.BlockSpec((D, N), fix)]
    out_specs = [pl.BlockSpec((tm, N), row), pl.BlockSpec((tm, D), row)]
    out_shape = [SDS((L, N), out_dtype), SDS((L, D), BF16)]
    scratch, args = [], (x, g, w)
    if epilogue is not None:
        in_specs += [pl.BlockSpec(a.shape, fix) for a in epilogue.ins]
        out_specs.append(pl.BlockSpec((tm, epilogue.cols), row))
        out_shape.append(SDS((L, epilogue.cols), epilogue.dtype))
        scratch = [pltpu.VMEM((HALO, epilogue.cols), F32)]
        args += tuple(epilogue.ins)
    return _carrier_call(body, comm, L // tm, in_specs=in_specs, out_specs=out_specs, out_shape=out_shape,
                         scratch_shapes=scratch, name=name, args=args)


def _conv_taps(x, tail, taps, tm):
    ext = jnp.concatenate([tail, x], axis=0)
    out = x * taps[0]
    for k in range(1, len(taps)):
        out = out + pltpu.roll(ext, k, axis=0)[HALO:HALO + tm] * taps[k]
    return out


def _ssd_conv_epilogue(cw, cb, col0):
    C = cw.shape[1]

    def fn(o_ref, ins, res_ref, tail_ref, tm):
        cw_ref, cb_ref = ins
        for s in range(C // LANE):
            sl = slice(s * LANE, (s + 1) * LANE)
            xv = o_ref[:, col0 + s * LANE:col0 + (s + 1) * LANE].astype(F32)
            cv = _conv_taps(xv, tail_ref[:, sl], [cw_ref[3 - k:4 - k, sl] for k in range(4)], tm) + cb_ref[:, sl]
            res_ref[:, sl] = cv * _sigmoid(cv)
            tail_ref[:, sl] = xv[tm - HALO:tm]

    return _Epilogue([cw, cb], C, F32, fn)


def _sc_gate_epilogue(cw):
    C = cw.shape[1]

    def fn(o_ref, ins, res_ref, tail_ref, tm):
        (cw_ref,) = ins
        for s in range(C // LANE):
            sl = slice(s * LANE, (s + 1) * LANE)
            gb, gc, v = (o_ref[:, (3 * s + k) * LANE:(3 * s + k + 1) * LANE].astype(F32) for k in range(3))
            p = gc * v
            u = _conv_taps(p, tail_ref[:, sl], [cw_ref[2 - k:3 - k, sl] for k in range(3)], tm)
            res_ref[:, sl] = (gb * u).astype(BF16)
            tail_ref[:, sl] = p[tm - HALO:tm]

    return _Epilogue([cw], C, BF16, fn)


def _matmul_norm_res(a, w, x, g, name, comm=None):
    L, K = a.shape
    D = w.shape[1]
    tm = min(MM_ROWS, L)

    def body(a_ref, w_ref, x_ref, g_ref, m_ref, xo_ref):
        m = jnp.dot(a_ref[...], w_ref[...], preferred_element_type=F32)
        m_ref[...] = m
        xo_ref[...] = x_ref[...] + m * _rsq(m) * g_ref[...]

    row = lambda i: (i, 0)
    fix = lambda i: (0, 0)
    return _carrier_call(
        body, comm, L // tm,
        in_specs=[pl.BlockSpec((tm, K), row), pl.BlockSpec((K, D), fix), pl.BlockSpec((tm, D), row),
                  pl.BlockSpec((1, D), fix)],
        out_specs=[pl.BlockSpec((tm, D), row), pl.BlockSpec((tm, D), row)],
        out_shape=[SDS((L, D), F32), SDS((L, D), F32)], scratch_shapes=[], name=name, args=(a, w, x, g))


def _postnorm_bwd_matmul_nt(dx, m, g, w, out_dtype, name):
    L, D = dx.shape
    K = w.shape[0]
    tm = min(MM_ROWS, L)

    def body(dx_ref, m_ref, g_ref, w_ref, da_ref, dm_ref, dg_ref):
        @pl.when(pl.program_id(0) == 0)
        def _():
            dg_ref[...] = jnp.zeros_like(dg_ref)

        m = m_ref[...]
        dxv = dx_ref[...]
        r = _rsq(m)
        mh = m * r
        dg_ref[...] += jnp.sum(dxv * mh, axis=0, keepdims=True)
        dyg = dxv * g_ref[...]
        dm = (r * (dyg - mh * jnp.mean(dyg * mh, axis=-1, keepdims=True))).astype(BF16)
        dm_ref[...] = dm
        da_ref[...] = lax.dot_general(dm, w_ref[...], NT, preferred_element_type=F32).astype(out_dtype)

    row = lambda i: (i, 0)
    fix = lambda i: (0, 0)
    return pl.pallas_call(
        body, grid=(L // tm,),
        in_specs=[pl.BlockSpec((tm, D), row), pl.BlockSpec((tm, D), row), pl.BlockSpec((1, D), fix),
                  pl.BlockSpec((K, D), fix)],
        out_specs=[pl.BlockSpec((tm, K), row), pl.BlockSpec((tm, D), row), pl.BlockSpec((1, D), fix)],
        out_shape=[SDS((L, K), out_dtype), SDS((L, D), BF16), SDS((1, D), F32)],
        name=name, compiler_params=_cp("arbitrary"))(dx, m, g, w)


DW_ACC_BYTES = 13 * 512 * 1024


def _dw_tiles(ka, n):
    ta = ka if ka <= 1024 else ka // 2
    fits = [d for d in range(LANE, n + 1, LANE) if n % d == 0 and ta * d * 4 <= DW_ACC_BYTES]
    return ta, max(fits)


def _matmul_tn(a, b, name):
    L, Ka = a.shape
    N = b.shape[1]
    ta, tn = _dw_tiles(Ka, N)
    tl = min(512, L)
    n_l = L // tl

    def body(a_ref, b_ref, o_ref, acc_ref):
        l = pl.program_id(2)

        @pl.when(l == 0)
        def _():
            acc_ref[...] = jnp.zeros_like(acc_ref)

        acc_ref[...] += lax.dot_general(a_ref[...], b_ref[...], TN, preferred_element_type=F32)

        @pl.when(l == n_l - 1)
        def _():
            o_ref[...] = acc_ref[...].astype(BF16)

    return pl.pallas_call(
        body, grid=(Ka // ta, N // tn, n_l),
        in_specs=[pl.BlockSpec((tl, ta), lambda i, j, l: (l, i)), pl.BlockSpec((tl, tn), lambda i, j, l: (l, j))],
        out_specs=pl.BlockSpec((ta, tn), lambda i, j, l: (i, j)),
        out_shape=SDS((Ka, N), BF16),
        scratch_shapes=[pltpu.VMEM((ta, tn), F32)],
        name=name, compiler_params=_cp("parallel", "parallel", "arbitrary"))(a, b)


def _ffn_up(x, g, w4, cw, cb, name, comm=None):
    L, D = x.shape
    b = w4.shape[2]
    C = 2 * b
    tm = min(MM_ROWS, L)

    def body(x_ref, g_ref, w_ref, cw_ref, cb_ref, o_ref, hn_ref, a_ref, tail_ref):
        @pl.when(pl.program_id(0) == 0)
        def _():
            tail_ref[...] = jnp.zeros_like(tail_ref)

        xv = x_ref[...]
        hn = (xv * _rsq(xv) * g_ref[...]).astype(BF16)
        hn_ref[...] = hn
        for half in range(2):
            cols = slice(half * b, (half + 1) * b)
            for part in range(2):
                o_ref[part, :, cols] = jnp.dot(hn, w_ref[2 * part + half], preferred_element_type=F32).astype(BF16)
            for s in range(half * b // LANE, (half + 1) * b // LANE):
                sl = slice(s * LANE, (s + 1) * LANE)
                gp = o_ref[0, :, sl].astype(F32)
                ext = jnp.concatenate([tail_ref[:, sl], gp], axis=0)
                back = lambda k: pltpu.roll(ext, k, axis=0)[HALO:HALO + tm]
                gate = gp * cw_ref[2:3, sl] + back(1) * cw_ref[1:2, sl] + back(2) * cw_ref[0:1, sl] + cb_ref[:, sl]
                a_ref[:, sl] = (gate * _sigmoid(gate) * o_ref[1, :, sl].astype(F32)).astype(BF16)
                tail_ref[:, sl] = gp[tm - HALO:tm]

    row = lambda i: (i, 0)
    fix = lambda i: (0, 0)
    return _carrier_call(
        body, comm, L // tm,
        in_specs=[pl.BlockSpec((tm, D), row), pl.BlockSpec((1, D), fix), pl.BlockSpec((4, D, b), lambda i: (0, 0, 0)),
                  pl.BlockSpec(cw.shape, fix), pl.BlockSpec(cb.shape, fix)],
        out_specs=[pl.BlockSpec((2, tm, C), lambda i: (0, i, 0)), pl.BlockSpec((tm, D), row), pl.BlockSpec((tm, C), row)],
        out_shape=[SDS((2, L, C), BF16), SDS((L, D), BF16), SDS((L, C), BF16)],
        scratch_shapes=[pltpu.VMEM((HALO, C), F32)], name=name, args=(x, g, w4, cw, cb))


def _ffn_up_dx(dup, w4, x, g, dres, name):
    _, L, _ = dup.shape
    _, D, b = w4.shape
    tm = min(MM_ROWS, L)

    def body(dy_ref, w_ref, x_ref, g_ref, dres_ref, dx_ref, dg_ref):
        @pl.when(pl.program_id(0) == 0)
        def _():
            dg_ref[...] = jnp.zeros_like(dg_ref)

        dh = jnp.zeros((tm, D), F32)
        for q in range(4):
            dh = dh + lax.dot_general(dy_ref[q // 2, :, (q % 2) * b:(q % 2 + 1) * b], w_ref[q], NT,
                                      preferred_element_type=F32)
        xv = x_ref[...]
        r = _rsq(xv)
        xh = xv * r
        dg_ref[...] += jnp.sum(dh * xh, axis=0, keepdims=True)
        dyg = dh * g_ref[...]
        dx_ref[...] = dres_ref[...] + r * (dyg - xh * jnp.mean(dyg * xh, axis=-1, keepdims=True))

    row = lambda i: (i, 0)
    fix = lambda i: (0, 0)
    return pl.pallas_call(
        body, grid=(L // tm,),
        in_specs=[pl.BlockSpec((2, tm, 2 * b), lambda i: (0, i, 0)), pl.BlockSpec((4, D, b), lambda i: (0, 0, 0)),
                  pl.BlockSpec((tm, D), row), pl.BlockSpec((1, D), fix), pl.BlockSpec((tm, D), row)],
        out_specs=[pl.BlockSpec((tm, D), row), pl.BlockSpec((1, D), fix)],
        out_shape=[SDS((L, D), F32), SDS((1, D), F32)],
        name=name, compiler_params=_cp("arbitrary"))(dup, w4, x, g, dres)


def _ffn_up_dw(hn, dup, name):
    L, D = hn.shape
    b = dup.shape[2] // 2
    tl = min(512, L)
    n_l = L // tl

    def body(a_ref, b_ref, o_ref, acc_ref):
        l = pl.program_id(1)

        @pl.when(l == 0)
        def _():
            acc_ref[...] = jnp.zeros_like(acc_ref)

        acc_ref[...] += lax.dot_general(a_ref[...], b_ref[...], TN, preferred_element_type=F32)

        @pl.when(l == n_l - 1)
        def _():
            o_ref[...] = acc_ref[...].astype(BF16)

    return pl.pallas_call(
        body, grid=(4, n_l),
        in_specs=[pl.BlockSpec((tl, D), lambda q, l: (l, 0)),
                  pl.BlockSpec((None, tl, b), lambda q, l: (q // 2, l, q % 2))],
        out_specs=pl.BlockSpec((None, D, b), lambda q, l: (q, 0, 0)),
        out_shape=SDS((4, D, b), BF16),
        scratch_shapes=[pltpu.VMEM((D, b), F32)],
        name=name, compiler_params=_cp("parallel", "arbitrary"))(hn, dup)


def _matmul_nt_prenorm_bwd(dy, w, x, g, dres, name, comm=None):
    L, N = dy.shape
    D = w.shape[0]
    tm = min(MM_ROWS, L)

    def body(dy_ref, w_ref, x_ref, g_ref, dres_ref, dx_ref, dg_ref):
        @pl.when(pl.program_id(0) == 0)
        def _():
            dg_ref[...] = jnp.zeros_like(dg_ref)

        dh = lax.dot_general(dy_ref[...], w_ref[...], NT, preferred_element_type=F32)
        xv = x_ref[...]
        r = _rsq(xv)
        xh = xv * r
        dg_ref[...] += jnp.sum(dh * xh, axis=0, keepdims=True)
        dyg = dh * g_ref[...]
        dx_ref[...] = dres_ref[...] + r * (dyg - xh * jnp.mean(dyg * xh, axis=-1, keepdims=True))

    row = lambda i: (i, 0)
    fix = lambda i: (0, 0)
    return _carrier_call(
        body, comm, L // tm,
        in_specs=[pl.BlockSpec((tm, N), row), pl.BlockSpec((D, N), fix), pl.BlockSpec((tm, D), row),
                  pl.BlockSpec((1, D), fix), pl.BlockSpec((tm, D), row)],
        out_specs=[pl.BlockSpec((tm, D), row), pl.BlockSpec((1, D), fix)],
        out_shape=[SDS((L, D), F32), SDS((1, D), F32)], scratch_shapes=[], name=name, args=(dy, w, x, g, dres))


def _loss_head(y, t, name):
    L, D = y.shape
    tm = min(512, L)

    def body(y_ref, t_ref, dy_ref, loss_ref):
        @pl.when(pl.program_id(0) == 0)
        def _():
            loss_ref[...] = jnp.zeros_like(loss_ref)

        e = y_ref[...] - t_ref[...]
        dy_ref[...] = e * (1.0 / D)
        s = jnp.sum(jnp.sum(e * e, axis=1, keepdims=True), axis=0, keepdims=True)
        loss_ref[...] += s * (0.5 / D)

    row = lambda i: (i, 0)
    return pl.pallas_call(
        body, grid=(L // tm,),
        in_specs=[pl.BlockSpec((tm, D), row), pl.BlockSpec((tm, D), row)],
        out_specs=[pl.BlockSpec((tm, D), row), pl.BlockSpec((1, 1), lambda i: (0, 0))],
        out_shape=[SDS((L, D), F32), SDS((1, 1), F32)],
        name=name, compiler_params=_cp("arbitrary"))(y, t)


def _tile_rows(ref):
    return HALO * (4 // jnp.dtype(ref.dtype).itemsize)


def _prev_rows(ref, r0, i, cols):
    n = _tile_rows(ref)
    p0 = pl.multiple_of(jnp.maximum(r0 - n, 0), n)
    return jnp.where(i > 0, ref[pl.ds(p0, n), cols].astype(F32)[n - HALO:], 0.0)


def _next_rows(ref, r0, i, n_tiles, cols):
    n = _tile_rows(ref)
    n0 = pl.multiple_of(jnp.minimum(r0 + ROW_TILE, n_tiles * ROW_TILE - n), n)
    return jnp.where(i < n_tiles - 1, ref[pl.ds(n0, n), cols].astype(F32)[:HALO], 0.0)


def _rows_f32(ref, rows, cols):
    return ref[rows, cols].astype(F32)


def _back(ext, s):
    return pltpu.roll(ext, s, axis=0)[HALO:HALO + ROW_TILE]


def _fwd(ext, s):
    n = ext.shape[0]
    return pltpu.roll(ext, n - s, axis=0)[:ROW_TILE]


def _store_rows(ref, rows):
    ref[...] = jnp.zeros_like(ref)
    for k, v in enumerate(rows):
        ref[k:k + 1, :] = v


def _strip_call(body, L, n_strips, ins, outs, name):
    def spec(rows, width, off):
        if off is None:
            return pl.BlockSpec((rows, width), lambda j: (0, 0))
        return pl.BlockSpec((rows, width), lambda j: (0, j + off))

    return pl.pallas_call(
        body, grid=(n_strips,),
        in_specs=[spec(a.shape[0], w, off) for a, w, off in ins],
        out_specs=[spec(s.shape[0], w, off) for s, w, off in outs],
        out_shape=[s for s, _, _ in outs],
        name=name, compiler_params=_cp("parallel"))(*[a for a, _, _ in ins])


def _ffn_mid_bwd(da, up, cw, cb, name):
    L, C = da.shape
    n_tiles = L // ROW_TILE
    al = slice(None)

    def body(da_ref, up_ref, cw_ref, cb_ref, dup_ref, st_ref):
        w0, w1, w2 = cw_ref[0:1, :], cw_ref[1:2, :], cw_ref[2:3, :]
        b = cb_ref[...]
        gate_ref, val_ref = up_ref.at[0], up_ref.at[1]

        def step(i, c):
            r0 = pl.multiple_of(i * ROW_TILE, ROW_TILE)
            rows = pl.ds(r0, ROW_TILE)
            gp = _rows_f32(gate_ref, rows, al)
            gpe = jnp.concatenate([_prev_rows(gate_ref, r0, i, al), gp, _next_rows(gate_ref, r0, i, n_tiles, al)],
                                  axis=0)
            g1, g2 = pltpu.roll(gpe, 1, axis=0), pltpu.roll(gpe, 2, axis=0)
            gate = (gpe * w2 + g1 * w1 + g2 * w0 + b)[HALO:]
            sg = _sigmoid(gate)
            da_e = jnp.concatenate([_rows_f32(da_ref, rows, al), _next_rows(da_ref, r0, i, n_tiles, al)], axis=0)
            val_e = jnp.concatenate([_rows_f32(val_ref, rows, al), _next_rows(val_ref, r0, i, n_tiles, al)], axis=0)
            dgate_e = da_e * val_e * _dsilu(gate, sg)
            dgate = dgate_e[:ROW_TILE]
            dgp = dgate * w2 + _fwd(dgate_e, 1) * w1 + _fwd(dgate_e, 2) * w0
            dup_ref[0, rows, :] = dgp.astype(BF16)
            dup_ref[1, rows, :] = (da_e * gate * sg)[:ROW_TILE].astype(BF16)
            s = lambda v: jnp.sum(v, axis=0, keepdims=True)
            t = slice(HALO, HALO + ROW_TILE)
            return (c[0] + s(dgate * g2[t]), c[1] + s(dgate * g1[t]), c[2] + s(dgate * gp), c[3] + s(dgate))

        z = jnp.zeros((1, LANE), F32)
        _store_rows(st_ref, lax.fori_loop(0, n_tiles, step, (z, z, z, z)))

    strip = lambda rows: pl.BlockSpec((rows, LANE), lambda j: (0, j))
    pair = pl.BlockSpec((2, L, LANE), lambda j: (0, 0, j))
    return pl.pallas_call(
        body, grid=(C // LANE,), in_specs=[strip(L), pair, strip(cw.shape[0]), strip(1)],
        out_specs=[pair, strip(8)], out_shape=[SDS((2, L, C), BF16), SDS((8, C), F32)],
        name=name, compiler_params=_cp("parallel"))(da, up, cw, cb)


def _sc_mid_bwd(dq, bcv, cw, name):
    L, C = dq.shape
    n_tiles = L // ROW_TILE
    s0, s1, s2, al = slice(0, LANE), slice(LANE, 2 * LANE), slice(2 * LANE, 3 * LANE), slice(None)

    def body(dq_ref, x_ref, cw_ref, dx_ref, st_ref):
        w0, w1, w2 = cw_ref[0:1, :], cw_ref[1:2, :], cw_ref[2:3, :]

        def step(i, c):
            r0 = pl.multiple_of(i * ROW_TILE, ROW_TILE)
            rows = pl.ds(r0, ROW_TILE)
            gb, gc, v = _rows_f32(x_ref, rows, s0), _rows_f32(x_ref, rows, s1), _rows_f32(x_ref, rows, s2)
            dq_v = _rows_f32(dq_ref, rows, al)
            p = gc * v
            pext = jnp.concatenate([_prev_rows(x_ref, r0, i, s1) * _prev_rows(x_ref, r0, i, s2), p], axis=0)
            p1, p2 = _back(pext, 1), _back(pext, 2)
            u = p * w2 + p1 * w1 + p2 * w0
            du = dq_v * gb
            du_n = _next_rows(dq_ref, r0, i, n_tiles, al) * _next_rows(x_ref, r0, i, n_tiles, s0)
            ext = jnp.concatenate([du, du_n], axis=0)
            dp = du * w2 + _fwd(ext, 1) * w1 + _fwd(ext, 2) * w0
            dx_ref[rows, s0] = (dq_v * u).astype(BF16)
            dx_ref[rows, s1] = (dp * v).astype(BF16)
            dx_ref[rows, s2] = (dp * gc).astype(BF16)
            s = lambda t: jnp.sum(t, axis=0, keepdims=True)
            return (c[0] + s(du * p2), c[1] + s(du * p1), c[2] + s(du * p))

        z = jnp.zeros((1, LANE), F32)
        _store_rows(st_ref, lax.fori_loop(0, n_tiles, step, (z, z, z)))

    return _strip_call(body, L, C // LANE, [(dq, LANE, 0), (bcv, 3 * LANE, 0), (cw, LANE, 0)],
                       [(SDS((L, 3 * C), BF16), 3 * LANE, 0), (SDS((8, C), F32), LANE, 0)], name)


def _ssd_conv_bwd(dxbc, zx, cw, cb, col0, dzx, name):
    L, C = dxbc.shape
    n_tiles = L // ROW_TILE
    al = slice(None)

    def body(d_ref, x_ref, cw_ref, cb_ref, dzx_in_ref, o_ref, st_ref):
        w0, w1, w2, w3 = cw_ref[0:1, :], cw_ref[1:2, :], cw_ref[2:3, :], cw_ref[3:4, :]
        b = cb_ref[...]

        def step(i, c):
            r0 = pl.multiple_of(i * ROW_TILE, ROW_TILE)
            rows = pl.ds(r0, ROW_TILE)
            xv = x_ref[rows, :]
            xe = jnp.concatenate([_prev_rows(x_ref, r0, i, al), xv, _next_rows(x_ref, r0, i, n_tiles, al)], axis=0)
            x1, x2, x3 = pltpu.roll(xe, 1, axis=0), pltpu.roll(xe, 2, axis=0), pltpu.roll(xe, 3, axis=0)
            cv = (xe * w3 + x1 * w2 + x2 * w1 + x3 * w0 + b)[HALO:]
            de = jnp.concatenate([d_ref[rows, :], _next_rows(d_ref, r0, i, n_tiles, al)], axis=0)
            dc_ext = de * _dsilu(cv, _sigmoid(cv))
            dc = dc_ext[:ROW_TILE]
            o_ref[rows, :] = (dc * w3 + _fwd(dc_ext, 1) * w2 + _fwd(dc_ext, 2) * w1 + _fwd(dc_ext, 3) * w0).astype(BF16)
            s = lambda t: jnp.sum(t, axis=0, keepdims=True)
            t = slice(HALO, HALO + ROW_TILE)
            return (c[0] + s(dc * x3[t]), c[1] + s(dc * x2[t]), c[2] + s(dc * x1[t]), c[3] + s(dc * xv), c[4] + s(dc))

        z = jnp.zeros((1, LANE), F32)
        _store_rows(st_ref, lax.fori_loop(0, n_tiles, step, (z, z, z, z, z)))

    strip = lambda rows, off=0: pl.BlockSpec((rows, LANE), lambda j: (0, j + off))
    shifted = strip(L, col0 // LANE)
    return pl.pallas_call(
        body, grid=(C // LANE,), in_specs=[strip(L), shifted, strip(cw.shape[0]), strip(1), ANY],
        out_specs=[shifted, strip(8)], out_shape=[SDS(dzx.shape, dzx.dtype), SDS((8, C), F32)],
        input_output_aliases={4: 0}, name=name, compiler_params=_cp("parallel"))(dxbc, zx, cw, cb, dzx)


def _scan_constants(n_heads):
    hw = n_heads * HEAD_DIM
    col = np.arange(hw)
    ind = (col[None, :] // HEAD_DIM == np.arange(LANE)[:, None]).astype(np.float32)
    gcol = np.arange(GROUP_W)
    itile = (gcol[None, :] % CHUNK == np.arange(CHUNK)[:, None]).astype(np.float32)
    trit = (gcol[None, :] % CHUNK <= np.arange(CHUNK)[:, None]).astype(np.float32)
    tril = np.tril(np.ones((CHUNK, CHUNK), np.float32))
    bmask = (gcol[:, None] // HEAD_DIM == gcol[None, :] // HEAD_DIM).astype(np.float32)
    return (jnp.asarray(ind.T.copy(), BF16), jnp.asarray(itile), jnp.asarray(trit), jnp.asarray(tril, BF16),
            jnp.asarray(bmask))


def _softplus(x):
    return jnp.maximum(x, 0.0) + jnp.log(1.0 + jnp.exp(-jnp.abs(x)))


def _split3(x):
    hi = x.astype(BF16)
    r1 = x - hi.astype(F32)
    mid = r1.astype(BF16)
    return hi, mid, (r1 - mid.astype(F32)).astype(BF16)


def _dot_sel(x, sel, dims=None):
    if dims is None:
        mm = lambda p: jnp.dot(p, sel, preferred_element_type=F32)
    else:
        mm = lambda p: lax.dot_general(sel, p, dims, preferred_element_type=F32)
    hi, mid, lo = _split3(x)
    return (mm(lo) + mm(mid)) + mm(hi)


SEL_X = (((1,), (0,)), ((), ()))


def _head_lanes(v, g):
    h0 = HEADS_PER_GROUP * g
    return jnp.concatenate([jnp.broadcast_to(v[:, h0 + r:h0 + r + 1], (v.shape[0], HEAD_DIM))
                            for r in range(HEADS_PER_GROUP)], axis=1)


def _group_terms(g, dt, cs, cst, xbc_ref, trit, bmask, d_inner):
    gl = slice(g * GROUP_W, (g + 1) * GROUP_W)
    h0 = HEADS_PER_GROUP * g
    csl = _head_lanes(cs, g)
    dtx = _head_lanes(dt, g)
    rr = jnp.concatenate([cst[h0 + r:h0 + r + 1, :] for r in range(HEADS_PER_GROUP)], axis=1)
    lm = jnp.exp(jnp.where(trit > 0.0, csl - rr, -jnp.inf))
    xs = xbc_ref[:, gl]
    b = xbc_ref[:, d_inner + g * D_STATE: d_inner + (g + 1) * D_STATE]
    c = xbc_ref[:, d_inner + (N_GROUPS + g) * D_STATE: d_inner + (N_GROUPS + g + 1) * D_STATE]
    u = xs * dtx
    bb, cb = b.astype(BF16), c.astype(BF16)
    btile = jnp.concatenate([bb] * HEADS_PER_GROUP, axis=0)
    cbt = lax.dot_general(cb, btile, NT, preferred_element_type=F32)
    m = cbt * lm
    ub = u.astype(BF16)
    bdu = jnp.where(bmask > 0.0, jnp.concatenate([ub] * HEADS_PER_GROUP, axis=0), jnp.zeros((), BF16))
    c_last = csl[CHUNK - 1:CHUNK, :]
    return dict(gl=gl, csl=csl, dtx=dtx, lm=lm, xs=xs, bb=bb, cb=cb, u=u, btile=btile, m=m, bdu=bdu,
                e=jnp.exp(csl), dec=jnp.exp(c_last - csl), e_last=jnp.exp(c_last))


def _ssd_scan_fwd(zx, xbc, par, dexp, nw, consts, comm, name):
    L = xbc.shape[0]
    d_inner = dexp.shape[1]
    n_chunks = L // CHUNK
    dt_blk = zx.shape[1] // LANE - 1
    ind_t, itile_c, trit_c, tril_c, bmask_c = consts

    def body(xbc_ref, z_ref, dtr_ref, par_ref, dexp_ref, nw_ref, trit_ref, tril_ref, bmask_ref,
             yn_ref, yf_ref, st_out_ref, st_ref):
        @pl.when(pl.program_id(0) == 0)
        def _():
            st_ref[...] = jnp.zeros_like(st_ref)

        dt = _softplus(dtr_ref[...] + par_ref[0:1, :])
        a_head = -jnp.exp(par_ref[1:2, :])
        cs = _dot_sel(dt * a_head, tril_ref[...], SEL_X)
        cst = cs.T
        trit, bmask = trit_ref[...], bmask_ref[...]
        for g in range(N_GROUPS):
            t = _group_terms(g, dt, cs, cst, xbc_ref, trit, bmask, d_inner)
            p = st_ref[g]
            st_out_ref[0, g] = p
            y = jnp.dot(t["m"].astype(BF16), t["bdu"], preferred_element_type=F32)
            y = y + jnp.dot(t["cb"], p.astype(BF16), preferred_element_type=F32) * t["e"]
            st_new = lax.dot_general(t["bb"], (t["u"] * t["dec"]).astype(BF16), TN, preferred_element_type=F32)
            st_ref[g] = p * t["e_last"] + st_new
            yf_ref[:, t["gl"]] = y + t["xs"] * dexp_ref[:, t["gl"]]
        z = z_ref[...]
        y2 = yf_ref[...] * (z * _sigmoid(z))
        yn_ref[...] = (y2 * _rsq(y2) * nw_ref[...]).astype(BF16)

    row = lambda c: (c, 0)
    fix = lambda c: (0, 0)
    cspec = lambda a: pl.BlockSpec(a.shape, fix)
    return _carrier_call(
        body, comm, n_chunks,
        in_specs=[pl.BlockSpec((CHUNK, xbc.shape[1]), row), pl.BlockSpec((CHUNK, d_inner), row),
                  pl.BlockSpec((CHUNK, LANE), lambda c: (c, dt_blk)), cspec(par), cspec(dexp), cspec(nw),
                  cspec(trit_c), cspec(tril_c), cspec(bmask_c)],
        out_specs=[pl.BlockSpec((CHUNK, d_inner), row), pl.BlockSpec((CHUNK, d_inner), row),
                   pl.BlockSpec((1, N_GROUPS, D_STATE, GROUP_W), lambda c: (c, 0, 0, 0))],
        out_shape=[SDS((L, d_inner), BF16), SDS((L, d_inner), F32),
                   SDS((n_chunks, N_GROUPS, D_STATE, GROUP_W), F32)],
        scratch_shapes=[pltpu.VMEM((N_GROUPS, D_STATE, GROUP_W), F32)],
        name=name, args=(xbc, zx, zx, par, dexp, nw, trit_c, tril_c, bmask_c))


def _ssd_scan_bwd(dyn, yf, zx, xbc, states, par, dexp, nw, consts, comm, name):
    L = xbc.shape[0]
    d_inner = dexp.shape[1]
    n_chunks = L // CHUNK
    nz = zx.shape[1]
    dt_blk = nz // LANE - 1
    ind_t, itile_c, trit_c, tril_c, bmask_c = consts
    hslices = [slice(r * HEAD_DIM, (r + 1) * HEAD_DIM) for r in range(HEADS_PER_GROUP)]

    def body(dyn_ref, yf_ref, z_ref, dtr_ref, xbc_ref, st_in_ref, par_ref, dexp_ref, nw_ref, indt_ref,
             itile_ref, trit_ref, tril_ref, bmask_ref,
             dzx_ref, dxbc_ref, dnw_ref, dpar_ref, dq_ref, dyf_ref):
        @pl.when(pl.program_id(0) == 0)
        def _():
            dq_ref[...] = jnp.zeros_like(dq_ref)
            dnw_ref[...] = jnp.zeros_like(dnw_ref)
            dpar_ref[...] = jnp.zeros_like(dpar_ref)

        z, yfv, dynv = z_ref[...], yf_ref[...], dyn_ref[...]
        sz = _sigmoid(z)
        y2 = yfv * (z * sz)
        r = _rsq(y2)
        y2h = y2 * r
        dnw_ref[...] += jnp.sum(dynv * y2h, axis=0, keepdims=True)
        dyg = dynv * nw_ref[...]
        dy2 = r * (dyg - y2h * jnp.mean(dyg * y2h, axis=-1, keepdims=True))
        dzx_ref[:, 0:d_inner] = (dy2 * yfv * _dsilu(z, sz)).astype(BF16)
        dyf_ref[...] = dy2 * (z * sz)

        pre = dtr_ref[...] + par_ref[0:1, :]
        dt = _softplus(pre)
        a_head = -jnp.exp(par_ref[1:2, :])
        cs = _dot_sel(dt * a_head, tril_ref[...], SEL_X)
        cst = cs.T
        itile, trit, bmask = itile_ref[...], trit_ref[...], bmask_ref[...]
        dcs = jnp.zeros((CHUNK, LANE), F32)
        dcs_last = jnp.zeros((1, LANE), F32)
        ddt_u = jnp.zeros((CHUNK, LANE), F32)
        d_skip = jnp.zeros((1, LANE), F32)
        rsum = lambda v: jnp.sum(v, axis=0, keepdims=True)
        row8 = lax.broadcasted_iota(jnp.int32, (8, GROUP_W), 0)
        for g in range(N_GROUPS):
            t = _group_terms(g, dt, cs, cst, xbc_ref, trit, bmask, d_inner)
            gl, m, lm, u, bb, cb, e, dec, xs = (t[k] for k in ("gl", "m", "lm", "u", "bb", "cb", "e", "dec", "xs"))
            indt = indt_ref[gl, :]
            dy = dyf_ref[:, gl]
            dyb = dy.astype(BF16)
            p = st_in_ref[0, g]
            pb = p.astype(BF16)
            q = dq_ref[g]
            qb = q.astype(BF16)
            big = lax.dot_general(m.astype(BF16), dyb, TN, preferred_element_type=F32)
            du = jnp.zeros((CHUNK, GROUP_W), F32)
            for rh in range(HEADS_PER_GROUP):
                du = du + big[hslices[rh], :] * bmask[rh * HEAD_DIM:rh * HEAD_DIM + 1, :]
            dm = lax.dot_general(dyb, t["bdu"], NT, preferred_element_type=F32)
            w = dm * m
            dgt = (dm * lm).astype(BF16)
            dc = jnp.dot(dgt, t["btile"], preferred_element_type=F32)
            db_big = lax.dot_general(dgt, cb, TN, preferred_element_type=F32)
            db = db_big[hslices[0], :] + db_big[hslices[1], :] + db_big[hslices[2], :] + db_big[hslices[3], :]
            cp = jnp.dot(cb, pb, preferred_element_type=F32)
            dye = dy * e
            dyeb = dye.astype(BF16)
            dc = dc + lax.dot_general(dyeb, pb, NT, preferred_element_type=F32)
            dp = lax.dot_general(cb, dyeb, TN, preferred_element_type=F32)
            x2 = dye * cp
            bq = jnp.dot(bb, qb, preferred_element_type=F32)
            ud = u * dec
            du = du + bq * dec
            db = db + lax.dot_general(ud.astype(BF16), qb, NT, preferred_element_type=F32)
            x1 = bq * ud
            dq_ref[g] = dp + t["e_last"] * q
            x3 = rsum(q * p) * t["e_last"]
            red = _dot_sel(jnp.concatenate([w + x2 - x1, du * xs, itile * rsum(w)], axis=0), indt)
            dcs = dcs + red[0:CHUNK] - red[2 * CHUNK:3 * CHUNK]
            ddt_u = ddt_u + red[CHUNK:2 * CHUNK]
            tail = _dot_sel(jnp.where(row8 == 0, rsum(x1) + x3, jnp.where(row8 == 1, rsum(dy * xs), 0.0)), indt)
            dcs_last = dcs_last + tail[0:1]
            d_skip = d_skip + tail[1:2]
            dxbc_ref[:, gl] = du * t["dtx"] + dy * dexp_ref[:, gl]
            dxbc_ref[:, d_inner + g * D_STATE: d_inner + (g + 1) * D_STATE] = db
            dxbc_ref[:, d_inner + (N_GROUPS + g) * D_STATE: d_inner + (N_GROUPS + g + 1) * D_STATE] = dc
        last = lax.broadcasted_iota(jnp.int32, (CHUNK, LANE), 0) == CHUNK - 1
        dcs = dcs + jnp.where(last, dcs_last, 0.0)
        da = _dot_sel(dcs, tril_ref[...], TN)
        ddt = da * a_head + ddt_u
        heads = lax.broadcasted_iota(jnp.int32, (CHUNK, LANE), 1) < d_inner // HEAD_DIM
        ddt_raw = jnp.where(heads, ddt * _sigmoid(pre), 0.0)
        dzx_ref[:, nz - LANE:nz] = ddt_raw.astype(BF16)
        dpar_ref[0:1, :] += rsum(ddt_raw)
        dpar_ref[1:2, :] += rsum(da * dt) * a_head
        dpar_ref[2:3, :] += d_skip

    rev = lambda c: (n_chunks - 1 - c, 0)
    fix = lambda c: (0, 0)
    cspec = lambda a: pl.BlockSpec(a.shape, fix)
    nx = xbc.shape[1]
    return _carrier_call(
        body, comm, n_chunks,
        in_specs=[pl.BlockSpec((CHUNK, d_inner), rev), pl.BlockSpec((CHUNK, d_inner), rev),
                  pl.BlockSpec((CHUNK, d_inner), rev), pl.BlockSpec((CHUNK, LANE), lambda c: (n_chunks - 1 - c, dt_blk)),
                  pl.BlockSpec((CHUNK, nx), rev),
                  pl.BlockSpec((1, N_GROUPS, D_STATE, GROUP_W), lambda c: (n_chunks - 1 - c, 0, 0, 0)),
                  cspec(par), cspec(dexp), cspec(nw), cspec(ind_t), cspec(itile_c), cspec(trit_c),
                  cspec(tril_c), cspec(bmask_c)],
        out_specs=[pl.BlockSpec((CHUNK, nz), rev), pl.BlockSpec((CHUNK, nx), rev),
                   pl.BlockSpec((1, d_inner), fix), pl.BlockSpec((8, LANE), fix)],
        out_shape=[SDS((L, nz), BF16), SDS((L, nx), F32), SDS((1, d_inner), F32), SDS((8, LANE), F32)],
        scratch_shapes=[pltpu.VMEM((N_GROUPS, D_STATE, GROUP_W), F32), pltpu.VMEM((CHUNK, d_inner), F32)],
        name=name, args=(dyn, yf, zx, zx, xbc, states, par, dexp, nw, ind_t, itile_c, trit_c, tril_c, bmask_c))


def _adamw(w, m, v, g, name):
    R, C = w.shape
    tr = R
    for cand in (256, 128, 64, 32, 16, 8):
        if R % cand == 0:
            tr = cand
            break

    def body(w_ref, m_ref, v_ref, g_ref, d_ref, mo_ref, vo_ref):
        gv = g_ref[...]
        mn = ADAM_B1 * m_ref[...] + (1.0 - ADAM_B1) * gv
        vn = ADAM_B2 * v_ref[...] + (1.0 - ADAM_B2) * (gv * gv)
        m_hat = mn / (1.0 - ADAM_B1 ** ADAM_STEP)
        v_hat = vn / (1.0 - ADAM_B2 ** ADAM_STEP)
        d_ref[...] = -ADAM_LR * (m_hat / (jnp.sqrt(v_hat) + ADAM_EPS) + ADAM_WD * w_ref[...])
        mo_ref[...] = mn
        vo_ref[...] = vn

    blk = pl.BlockSpec((tr, C), lambda i: (i, 0))
    return pl.pallas_call(
        body, grid=(R // tr,), in_specs=[blk] * 4, out_specs=[blk] * 3, out_shape=[SDS((R, C), F32)] * 3,
        name=name, compiler_params=_cp("parallel"))(w, m, v, g)


def _sum_slots(parts, name):
    n, R, C = parts.shape
    tr = 128 if R % 128 == 0 else R

    def body(p_ref, o_ref):
        acc = p_ref[0]
        for k in range(1, n):
            acc = acc + p_ref[k]
        o_ref[...] = acc

    return pl.pallas_call(
        body, grid=(R // tr,), in_specs=[pl.BlockSpec((n, tr, C), lambda i: (0, i, 0))],
        out_specs=pl.BlockSpec((tr, C), lambda i: (i, 0)), out_shape=SDS((R, C), F32),
        name=name, compiler_params=_cp("parallel"))(parts)


def _add_core_halves(where, g, r, name):
    _, n, a, b = g.shape
    ta = a // 2

    def body(w_ref, g_ref, r_ref, o_ref):
        o_ref[...] = (g_ref[...].astype(F32) + r_ref[...].astype(F32)).astype(BF16)

    blk = lambda f: pl.BlockSpec((None, None, ta, b), f)
    mine = lambda s, l, w: (s, l, 0, 0)
    return pl.pallas_call(
        body, grid_spec=pltpu.PrefetchScalarGridSpec(
            num_scalar_prefetch=1, grid=(4, n),
            in_specs=[blk(lambda s, l, w: (s, l, w[1], 0)), blk(mine)], out_specs=blk(mine)),
        out_shape=SDS((4, n, ta, b), BF16), name=name,
        compiler_params=_cp("parallel", "parallel"))(where, g, r)


def _sum_shard(where, g, r, rr, into, layer, n_layers, name):
    _, _, a, b = g.shape
    ta = a // 2

    def body(w_ref, g_ref, r_ref, rr_ref, *refs):
        f = lambda v: v.astype(F32)
        refs[-1][...] = (((f(g_ref[...]) + f(r_ref[...])) + f(rr_ref[0])) + f(rr_ref[1])) + f(rr_ref[2])

    more = [] if into is None else [into]
    return pl.pallas_call(
        body, grid_spec=pltpu.PrefetchScalarGridSpec(
            num_scalar_prefetch=1, grid=(1,),
            in_specs=[pl.BlockSpec((None, None, ta, b), lambda l, w: (w[0], 0, w[1], 0)),
                      pl.BlockSpec((None, None, ta, b), lambda l, w: (w[0], 0, 0, 0)),
                      pl.BlockSpec((3, None, ta, b), lambda l, w: (0, 0, 0, 0))] + [ANY] * len(more),
            out_specs=pl.BlockSpec((None, ta, b), lambda l, w: (layer, w[1], 0))),
        out_shape=SDS((n_layers, a, b), F32), name=name, input_output_aliases={4: 0} if more else {},
        compiler_params=_cp("arbitrary"))(where, g, r, rr, *more)


def _me():
    return lax.axis_index("x"), lax.axis_index("y"), lax.axis_index("c")


def _chip_peers(x, y):
    return [(1 - x, y), (x, 1 - y), (1 - x, 1 - y)]


def _rcopy(src, dst, send_sems, recv_sems, k, to):
    return pltpu.make_async_remote_copy(src_ref=src, dst_ref=dst, send_sem=send_sems.at[k], recv_sem=recv_sems.at[k],
                                        device_id=to, device_id_type=MESH)


def _row_half(ref, c, lead=()):
    a = ref.shape[len(lead) + 1]
    return ref.at[(*lead, slice(None), pl.ds(c * (a // 2), a // 2))]


class _Comm(NamedTuple):
    ins: list
    out_shapes: list
    n_sems: int
    start: Callable
    finish: Callable


def _sem_scratch(comm):
    return [pltpu.SemaphoreType.DMA((comm.n_sems,)), pltpu.SemaphoreType.DMA((comm.n_sems,))]


def _run_comm(comm, name):
    n_in, n_out = len(comm.ins), len(comm.out_shapes)

    def body(*refs):
        ins, outs, sems = refs[:n_in], refs[n_in:n_in + n_out], refs[n_in + n_out:]
        comm.start(ins, outs, *sems)
        comm.finish(ins, outs, *sems)

    return pl.pallas_call(body, in_specs=[ANY] * n_in, out_specs=[ANY] * n_out, out_shape=comm.out_shapes,
                          scratch_shapes=_sem_scratch(comm), name=name)(*comm.ins)


def _carrier_call(compute, comm, n_steps, in_specs, out_specs, out_shape, scratch_shapes, name, args):
    if comm is None:
        return pl.pallas_call(compute, grid=(n_steps,), in_specs=in_specs, out_specs=out_specs, out_shape=out_shape,
                              scratch_shapes=scratch_shapes, name=name, compiler_params=_cp("arbitrary"))(*args), []
    n_in, n_out, n_scr = len(in_specs), len(out_specs), len(scratch_shapes)
    n_ci, n_co = len(comm.ins), len(comm.out_shapes)

    def body(*refs):
        ins, cins = refs[:n_in], refs[n_in:n_in + n_ci]
        o = n_in + n_ci
        outs, couts = refs[o:o + n_out], refs[o + n_out:o + n_out + n_co]
        s = o + n_out + n_co
        scratch, sems = refs[s:s + n_scr], refs[s + n_scr:]

        @pl.when(pl.program_id(0) == 0)
        def _():
            comm.start(cins, couts, *sems)

        compute(*ins, *outs, *scratch)

        @pl.when(pl.program_id(0) == n_steps - 1)
        def _():
            comm.finish(cins, couts, *sems)

    res = pl.pallas_call(
        body, grid=(n_steps,), in_specs=list(in_specs) + [ANY] * n_ci, out_specs=list(out_specs) + [ANY] * n_co,
        out_shape=list(out_shape) + list(comm.out_shapes), scratch_shapes=list(scratch_shapes) + _sem_scratch(comm),
        name=name, compiler_params=_cp("arbitrary"))(*args, *comm.ins)
    return res[:n_out], res[n_out:]


def _allgather_plan(mine, small=None):
    n = len(mine)
    ins = list(mine) + ([] if small is None else [small])
    out_shapes = [SDS((4,) + m.shape, BF16) for m in mine] + ([] if small is None else [SDS((4,) + small.shape, F32)])
    sem = lambda t, k: 7 * t + k

    def first_copies(ins_r, outs_r, send, recv):
        x, y, c = _me()
        q = 2 * x + y
        cps = []
        for j, chip in enumerate(_chip_peers(x, y)):
            for t in range(n):
                cps.append(_rcopy(_row_half(ins_r[t], c), _row_half(outs_r[t], c, (q,)), send, recv, sem(t, j),
                                  (*chip, c)))
            if small is not None:
                cps.append(_rcopy(ins_r[n], outs_r[n].at[q], send, recv, sem(n, j), (*chip, c)))
        for t in range(len(ins)):
            cps.append(_rcopy(ins_r[t], outs_r[t].at[q], send, recv, sem(t, 6), (x, y, 1 - c)))
        return cps

    def start(ins_r, outs_r, send, recv):
        for cp in first_copies(ins_r, outs_r, send, recv):
            cp.start()

    def finish(ins_r, outs_r, send, recv):
        x, y, c = _me()
        sib = (x, y, 1 - c)
        chips = _chip_peers(x, y)
        passed = []
        for j, (px, py) in enumerate(chips):
            for t in range(n):
                blk = _row_half(outs_r[t], c, (2 * px + py,))
                _rcopy(blk, blk, send, recv, sem(t, j), sib).wait_recv()
                cp = _rcopy(blk, blk, send, recv, sem(t, 3 + j), sib)
                cp.start()
                passed.append(cp)
        for j, (px, py) in enumerate(chips):
            for t in range(n):
                blk = _row_half(outs_r[t], 1 - c, (2 * px + py,))
                _rcopy(blk, blk, send, recv, sem(t, 3 + j), sib).wait_recv()
            if small is not None:
                sblk = outs_r[n].at[2 * px + py]
                _rcopy(sblk, sblk, send, recv, sem(n, j), sib).wait_recv()
        for t in range(len(ins)):
            own = outs_r[t].at[2 * x + y]
            _rcopy(own, own, send, recv, sem(t, 6), sib).wait_recv()
        for cp in first_copies(ins_r, outs_r, send, recv) + passed:
            cp.wait_send()

    return _Comm(ins, out_shapes, 7 * len(ins), start, finish)


def _grads_to_sibling_plan(grads):
    n = len(grads)

    def copies(ins_r, outs_r, send, recv):
        x, y, c = _me()
        return [_rcopy(_row_half(ins_r[t], 1 - c, (slice(None),)), outs_r[t], send, recv, t, (x, y, 1 - c))
                for t in range(n)]

    def start(*a):
        for cp in copies(*a):
            cp.start()

    def finish(*a):
        for cp in copies(*a):
            cp.wait()

    out_shapes = [SDS((4, g.shape[1], g.shape[2] // 2, g.shape[3]), BF16) for g in grads]
    return _Comm(list(grads), out_shapes, n, start, finish)


def _grads_to_chips_plan(psums):
    n = len(psums)

    def copies(ins_r, outs_r, send, recv):
        x, y, c = _me()
        return [_rcopy(ins_r[t].at[2 * px + py], outs_r[t].at[j], send, recv, 3 * t + j, (px, py, c))
                for j, (px, py) in enumerate(_chip_peers(x, y)) for t in range(n)]

    def start(*a):
        for cp in copies(*a):
            cp.start()

    def finish(*a):
        for cp in copies(*a):
            cp.wait()

    return _Comm(list(psums), [SDS((3,) + p.shape[1:], BF16) for p in psums], 3 * n, start, finish)


def _swap_halves(sums):
    n = len(sums)

    def body(*refs):
        out_refs = refs[n:2 * n]
        send_sems, recv_sems = refs[2 * n:]
        x, y, c = _me()
        sib = (x, y, 1 - c)
        cps = [_rcopy(_row_half(out_refs[t], c), _row_half(out_refs[t], c), send_sems, recv_sems, t, sib)
               for t in range(n)]
        for cp in cps:
            cp.start()
        for t in range(n):
            other = _row_half(out_refs[t], 1 - c)
            _rcopy(other, other, send_sems, recv_sems, t, sib).wait_recv()
        for cp in cps:
            cp.wait_send()

    return pl.pallas_call(
        body, in_specs=[ANY] * n, out_specs=[ANY] * n, out_shape=[SDS(s.shape, F32) for s in sums],
        input_output_aliases={t: t for t in range(n)},
        scratch_shapes=[pltpu.SemaphoreType.DMA((n,)), pltpu.SemaphoreType.DMA((n,))],
        name="swap_halves")(*sums)


def _allgather_small(part):
    def body(p_ref, out_ref, send_sems, recv_sems, local_sem):
        x, y, c = _me()
        me = 4 * x + 2 * y + c
        own = pltpu.make_async_copy(p_ref, out_ref.at[me], local_sem.at[0])
        own.start()
        sends = []
        for k in range(1, 8):
            fx, fy, fc = (k >> 2) & 1, (k >> 1) & 1, k & 1
            to = (x ^ fx, y ^ fy, c ^ fc)
            sends.append(_rcopy(p_ref, out_ref.at[me], send_sems, recv_sems, k - 1, to))
        for cp in sends:
            cp.start()
        for k in range(1, 8):
            slot = out_ref.at[me ^ k]
            _rcopy(slot, slot, send_sems, recv_sems, k - 1, (x, y, c)).wait_recv()
        for cp in sends:
            cp.wait_send()
        own.wait()

    return pl.pallas_call(
        body, in_specs=[ANY], out_specs=ANY, out_shape=SDS((8,) + part.shape, F32),
        scratch_shapes=[pltpu.SemaphoreType.DMA((7,)), pltpu.SemaphoreType.DMA((7,)), pltpu.SemaphoreType.DMA((1,))],
        name="allgather_small")(part)


BIG = (("ssd_w_in", 2), ("ssd_w_out", 1), ("sc_w_in", 2), ("sc_w_out", 1), ("ffn_w_up", 2), ("ffn_w_down", 1))


def _to_shards(full, axis):
    A, B = full.shape
    if axis == 2:
        return full.reshape(A, 4, B // 4).transpose(1, 0, 2)
    return full.reshape(4, A // 4, B)


def _from_shards(shards, axis):
    _, a, b = shards.shape
    if axis == 2:
        return shards.transpose(1, 0, 2).reshape(a, 4 * b)
    return shards.reshape(4 * a, b)


def _interleave(w, parts):
    lead, n = w.shape[:-1], w.shape[-1]
    return w.reshape(*lead, parts, n // (parts * LANE), LANE).swapaxes(-2, -3).reshape(*lead, n)


def _deinterleave(w, parts):
    lead, n = w.shape[:-1], w.shape[-1]
    return w.reshape(*lead, n // (parts * LANE), parts, LANE).swapaxes(-2, -3).reshape(*lead, n)


def _pack_rows(vectors, width, row_multiple):
    flat = jnp.concatenate(vectors, axis=-1)
    n = flat.shape[-1]
    unit = width * row_multiple
    total = -(-n // unit) * unit
    flat = jnp.pad(flat, [(0, 0)] * (flat.ndim - 1) + [(0, total - n)])
    return flat.reshape(*flat.shape[:-1], total // width, width)


def _unpack(flat, shapes):
    out, off = [], 0
    for s in shapes:
        n = int(np.prod(s))
        out.append(flat[..., off:off + n].reshape(*flat.shape[:-1], *s))
        off += n
    return out


def _memo(fn):
    cache = {}

    def wrapped(k):
        if k not in cache:
            cache[k] = fn(k)
        return cache[k]

    return wrapped


def _row(v):
    return v.reshape(1, -1)


def _pad_rows(w, rows=8):
    return jnp.pad(w, ((0, rows - w.shape[0]), (0, 0)))


def _ffn_fwd(x, g_pre, g_post, w_up, cw, cb, w_down, tag, comm_up=None, comm_down=None):
    (up, hn, a), got_up = _ffn_up(x, g_pre, w_up, cw, cb, "ffn_up" + tag, comm_up)
    (f, x_new), got_down = _matmul_norm_res(a, w_down, x, g_post, "ffn_down" + tag, comm_down)
    return x_new, (x, hn, up, a, f), got_up, got_down


def _ffn_bwd(dx, saved, g_pre, g_post, w_up, cw, cb, w_down, tag):
    x, hn, up, a, f = saved
    da, df, dg_post = _postnorm_bwd_matmul_nt(dx, f, g_post, w_down, BF16, "ffn_down_dx" + tag)
    dw_down = _matmul_tn(a, df, "ffn_down_dw" + tag)
    dup, stats = _ffn_mid_bwd(da, up, cw, cb, "ffn_mid_bwd" + tag)
    dx_in, dg_pre = _ffn_up_dx(dup, w_up, x, g_pre, dx, "ffn_up_dx" + tag)
    dw_up = _ffn_up_dw(hn, dup, "ffn_up_dw" + tag)
    return dx_in, dict(g_pre=dg_pre, g_post=dg_post, w_up=dw_up, w_down=dw_down, cw=stats[0:3], cb=stats[3])


def _sc_fwd(x, g_pre, g_post, w_in, cw, w_out, tag, comm_in=None, comm_out=None):
    (bcv, hn, q), got_in = _norm_matmul(x, g_pre, w_in, BF16, "sc_in" + tag, comm_in, _sc_gate_epilogue(cw))
    (m, x_new), got_out = _matmul_norm_res(q, w_out, x, g_post, "sc_out" + tag, comm_out)
    return x_new, (x, hn, bcv, q, m), got_in, got_out


def _sc_bwd(dx, saved, g_pre, g_post, w_in, cw, w_out, tag):
    x, hn, bcv, q, m = saved
    dq, dm, dg_post = _postnorm_bwd_matmul_nt(dx, m, g_post, w_out, BF16, "sc_out_dx" + tag)
    dw_out = _matmul_tn(q, dm, "sc_out_dw" + tag)
    dbcv, stats = _sc_mid_bwd(dq, bcv, cw, "sc_mid_bwd" + tag)
    (dx_in, dg_pre), _ = _matmul_nt_prenorm_bwd(dbcv, w_in, x, g_pre, dx, "sc_in_dx" + tag)
    dw_in = _matmul_tn(hn, dbcv, "sc_in_dw" + tag)
    return dx_in, dict(g_pre=dg_pre, g_post=dg_post, w_in=dw_in, w_out=dw_out, cw=stats[0:3])


def _ssd_fwd(x, g_pre, g_post, w_in, cw, cb, par, dexp, nw, w_out, consts, comm, tag, comm_in=None):
    d_inner = dexp.shape[1]
    (zx, hn, xbc), got_in = _norm_matmul(x, g_pre, w_in, F32, "ssd_in" + tag, comm_in,
                                         _ssd_conv_epilogue(cw, cb, d_inner))
    (yn, yf, states), got = _ssd_scan_fwd(zx, xbc, par, dexp, nw, consts, comm, "ssd_scan_fwd" + tag)
    if callable(w_out):
        w_out = w_out(got_in)
    (m, x_new), _ = _matmul_norm_res(yn, w_out, x, g_post, "ssd_out" + tag)
    return x_new, (x, hn, zx, xbc, yn, yf, states, m), got


def _ssd_bwd(dx, saved, g_pre, g_post, w_in, cw, cb, par, dexp, nw, w_out, consts, comm, tag, comm_dx=None):
    x, hn, zx, xbc, yn, yf, states, m = saved
    d_inner = dexp.shape[1]
    dyn, dm, dg_post = _postnorm_bwd_matmul_nt(dx, m, g_post, w_out, F32, "ssd_out_dx" + tag)
    dw_out = _matmul_tn(yn, dm, "ssd_out_dw" + tag)
    (dzx, dxbc, dnw, dpar), got = _ssd_scan_bwd(dyn, yf, zx, xbc, states, par, dexp, nw, consts, comm(dw_out),
                                                "ssd_scan_bwd" + tag)
    dzx, stats = _ssd_conv_bwd(dxbc, zx, cw, cb, d_inner, dzx, "ssd_conv_bwd" + tag)
    dw_in = _matmul_tn(hn, dzx, "ssd_in_dw" + tag)
    (dx_in, dg_pre), got_dx = _matmul_nt_prenorm_bwd(dzx, w_in, x, g_pre, dx, "ssd_in_dx" + tag,
                                                     None if comm_dx is None else comm_dx(dw_in))
    n_heads = d_inner // HEAD_DIM
    grads = dict(g_pre=dg_pre, g_post=dg_post, w_in=dw_in, w_out=dw_out, cw=stats[0:4], cb=stats[4],
                 dt_bias=dpar[0, :n_heads], a_log=dpar[1, :n_heads], d=dpar[2, :n_heads], nw=dnw[0])
    return dx_in, grads, got, got_dx


def kernel(x, mix_pre_g, mix_post_g, ffn_pre_g, ffn_post_g, ssd_w_in, ssd_conv_w, ssd_conv_b, ssd_dt_bias, ssd_A_log, ssd_D, ssd_norm_w, ssd_w_out, sc_w_in, sc_conv_w, sc_w_out, ffn_w_up, ffn_conv_w, ffn_conv_b, ffn_w_down, loss_target, m_mix_pre_g, m_mix_post_g, m_ffn_pre_g, m_ffn_post_g, m_ssd_w_in, m_ssd_conv_w, m_ssd_conv_b, m_ssd_dt_bias, m_ssd_A_log, m_ssd_D, m_ssd_norm_w, m_ssd_w_out, m_sc_w_in, m_sc_conv_w, m_sc_w_out, m_ffn_w_up, m_ffn_conv_w, m_ffn_conv_b, m_ffn_w_down, v_mix_pre_g, v_mix_post_g, v_ffn_pre_g, v_ffn_post_g, v_ssd_w_in, v_ssd_conv_w, v_ssd_conv_b, v_ssd_dt_bias, v_ssd_A_log, v_ssd_D, v_ssd_norm_w, v_ssd_w_out, v_sc_w_in, v_sc_conv_w, v_sc_w_out, v_ffn_w_up, v_ffn_conv_w, v_ffn_conv_b, v_ffn_w_down):
    names = ["mix_pre_g", "mix_post_g", "ffn_pre_g", "ffn_post_g", "ssd_w_in", "ssd_conv_w", "ssd_conv_b",
             "ssd_dt_bias", "ssd_A_log", "ssd_D", "ssd_norm_w", "ssd_w_out", "sc_w_in", "sc_conv_w", "sc_w_out",
             "ffn_w_up", "ffn_conv_w", "ffn_conv_b", "ffn_w_down"]
    env = locals()
    wts = {n: env[n] for n in names}
    mom = {n: env["m_" + n] for n in names}
    var = {n: env["v_" + n] for n in names}

    depth, d_model = mix_pre_g.shape
    n_ssd, n_heads = ssd_dt_bias.shape
    n_sc = sc_conv_w.shape[0]
    d_inner = n_heads * HEAD_DIM
    conv_dim = d_inner + 2 * N_GROUPS * D_STATE
    ssd_in_dim = d_inner + conv_dim + n_heads
    ssd_in_pad = d_inner + conv_dim + LANE
    q_chip = 2 * lax.axis_index("x") + lax.axis_index("y")
    core = lax.axis_index("c")

    assert depth == 4 and n_ssd == 2 and n_sc == 2, "the exchange schedule is written for this trunk"
    ssd_items = lambda j: [("ssd_w_in", j), ("ssd_w_out", j)]
    sc_items = lambda j: [("sc_w_in", j), ("sc_w_out", j)]
    ffn_items = lambda i: [("ffn_w_up", i), ("ffn_w_down", i)]
    gather_first = [("ssd_w_in", 0)]
    gather_in_ssd_in = {0: [("ssd_w_out", 0)]}
    gather_in_scan = {0: ffn_items(0) + sc_items(0), 2: ffn_items(2) + sc_items(1)}
    gather_in_ffn = {0: ([("ffn_w_up", 1)], [("ffn_w_down", 1)]), 1: ([("ssd_w_in", 1)], []),
                     2: ([("ffn_w_up", 3)], [("ffn_w_down", 3)])}
    gather_in_sc = {1: ([("ssd_w_out", 1)], [])}
    reduce_in_scan = {2: ffn_items(3) + sc_items(1) + ffn_items(2),
                      0: ssd_items(1) + ffn_items(1) + sc_items(0) + ffn_items(0) + [("ssd_w_out", 0)]}
    reduce_in_dx = {0: [("ssd_w_in", 0)]}
    axis_of = dict(BIG)
    gathered = {}

    def gather_plan(items, small=None):
        mine = [wts[n][layer:layer + 1].astype(BF16) for n, layer in items]
        return _allgather_plan(mine, small) if items else None

    def gather_done(items, results):
        for item, buf in zip(items, results):
            gathered[item] = buf[:, 0]

    def full(n, layer):
        return _from_shards(gathered[(n, layer)], axis_of[n])

    conv_names = ["ssd_conv_w", "sc_conv_w", "ffn_conv_w"]
    conv_shapes = [wts[n].shape for n in conv_names]
    small_mine = _pack_rows([wts[n].reshape(-1) for n in conv_names], LANE, 8)
    *results, small_all = _run_comm(gather_plan(gather_first, small_mine), "allgather_first")
    gather_done(gather_first, results)
    conv_full = {}
    for n, f, s in zip(conv_names, _unpack(small_all.reshape(4, -1), conv_shapes), conv_shapes):
        conv_full[n] = f.transpose(1, 2, 0, 3).reshape(s[0], s[1], 4 * s[2])

    consts = _scan_constants(n_heads)

    def ssd_args(j):
        par = jnp.zeros((8, LANE), F32).at[0, :n_heads].set(ssd_dt_bias[j]).at[1, :n_heads].set(ssd_A_log[j])
        dexp = jnp.repeat(ssd_D[j], HEAD_DIM).reshape(1, d_inner)
        w_in = jnp.pad(full("ssd_w_in", j), ((0, 0), (0, ssd_in_pad - ssd_in_dim)))
        return (w_in, _pad_rows(conv_full["ssd_conv_w"][j]), _row(ssd_conv_b[j]), par, dexp, _row(ssd_norm_w[j]))

    def sc_args(j):
        return (_interleave(full("sc_w_in", j), 3), _pad_rows(conv_full["sc_conv_w"][j]), full("sc_w_out", j))

    def ffn_args(i):
        return (gathered[("ffn_w_up", i)], _pad_rows(conv_full["ffn_conv_w"][i]), _row(ffn_conv_b[i]),
                full("ffn_w_down", i))

    ssd_args, sc_args, ffn_args = _memo(ssd_args), _memo(sc_args), _memo(ffn_args)

    h = x[0]
    saved = []
    for i in range(depth):
        j = i // 2
        gp, gq = _row(mix_pre_g[i]), _row(mix_post_g[i])
        if i % 2 == 0:
            items_in = gather_in_ssd_in.get(i, [])

            def w_out_when_here(got_in, items_in=items_in, j=j):
                gather_done(items_in, got_in)
                return full("ssd_w_out", j)

            h, sv, results = _ssd_fwd(h, gp, gq, *ssd_args(j), w_out_when_here, consts, gather_plan(gather_in_scan[i]),
                                      tag="", comm_in=gather_plan(items_in))
            gather_done(gather_in_scan[i], results)
        else:
            items_in, items_out = gather_in_sc.get(i, ([], []))
            h, sv, got_in, got_out = _sc_fwd(h, gp, gq, *sc_args(j), tag="", comm_in=gather_plan(items_in),
                                             comm_out=gather_plan(items_out))
            gather_done(items_in, got_in)
            gather_done(items_out, got_out)
        items_up, items_down = gather_in_ffn.get(i, ([], []))
        h, sv2, got_up, got_down = _ffn_fwd(h, _row(ffn_pre_g[i]), _row(ffn_post_g[i]), *ffn_args(i), tag="",
                                            comm_up=gather_plan(items_up), comm_down=gather_plan(items_down))
        gather_done(items_up, got_up)
        gather_done(items_down, got_down)
        saved.append((sv, sv2))
    dh, loss_part = _loss_head(h, loss_target[0], "loss_head")

    mix_grads, ffn_grads = [None] * depth, [None] * depth
    where = jnp.stack([q_chip, core]).astype(jnp.int32)

    def shard_grad(n, layer):
        if n == "ffn_w_up":
            g = ffn_grads[layer]["w_up"]
        elif n == "ffn_w_down":
            g = _to_shards(ffn_grads[layer]["w_down"], 1)
        elif n == "ssd_w_in":
            g = _to_shards(early[(n, layer)][:, :ssd_in_dim], 2)
        elif n == "ssd_w_out":
            g = _to_shards(early[(n, layer)], 1)
        elif n == "sc_w_in":
            g = _to_shards(_deinterleave(mix_grads[2 * layer + 1]["w_in"], 3), 2)
        else:
            g = _to_shards(mix_grads[2 * layer + 1]["w_out"], 1)
        return g[:, None]

    def reduce_begin(items, tag):
        by_shard = [shard_grad(n, layer) for n, layer in items]
        from_sib = _run_comm(_grads_to_sibling_plan(by_shard), "grads_to_sibling" + tag)
        chip_sums = [_add_core_halves(where, g, r, "add_core_halves_%s%d" % item)
                     for item, g, r in zip(items, by_shard, from_sib)]
        return by_shard, from_sib, _grads_to_chips_plan(chip_sums)

    sums = {}
    early = {}

    def riding(items, tag):
        by_shard, from_sib, plan = reduce_begin(items, tag)

        def arrived(from_chips):
            for (n, layer), g, r, rr in zip(items, by_shard, from_sib, from_chips):
                sums[n] = _sum_shard(where, g, r, rr, sums.get(n), layer, wts[n].shape[0],
                                     "sum_shard_%s%d" % (n, layer))

        return plan, arrived

    for i in reversed(range(depth)):
        j = i // 2
        sv, sv2 = saved[i]
        dh, ffn_grads[i] = _ffn_bwd(dh, sv2, _row(ffn_pre_g[i]), _row(ffn_post_g[i]), *ffn_args(i), tag="")
        gp, gq = _row(mix_pre_g[i]), _row(mix_post_g[i])
        if i % 2 == 0:
            then = {}

            def in_scan(dw_out, i=i, j=j, then=then):
                early[("ssd_w_out", j)] = dw_out
                plan, then["scan"] = riding(reduce_in_scan[i], "_%d" % i)
                return plan

            def in_dx(dw_in, i=i, j=j, then=then):
                early[("ssd_w_in", j)] = dw_in
                plan, then["dx"] = riding(reduce_in_dx[i], "_dx%d" % i) if i in reduce_in_dx else (None, None)
                return plan

            dh, mix_grads[i], got, got_dx = _ssd_bwd(dh, sv, gp, gq, *ssd_args(j), full("ssd_w_out", j), consts,
                                                     in_scan, tag="", comm_dx=in_dx)
            then["scan"](got)
            if then["dx"] is not None:
                then["dx"](got_dx)
        else:
            w_in, scw, w_out = sc_args(j)
            dh, mix_grads[i] = _sc_bwd(dh, sv, gp, gq, w_in, scw, w_out, tag="")
    grad_x = dh[None]
    big_grads = dict(zip([n for n, _ in BIG], _swap_halves([sums[n] for n, _ in BIG])))
    ssd_l = [mix_grads[i] for i in range(0, depth, 2)]
    sc_l = [mix_grads[i] for i in range(1, depth, 2)]
    stack = lambda layers, k: jnp.stack([g[k] for g in layers])

    small_names = ["mix_pre_g", "mix_post_g", "ffn_pre_g", "ffn_post_g", "ssd_conv_w", "ssd_conv_b", "ssd_dt_bias",
                   "ssd_A_log", "ssd_D", "ssd_norm_w", "sc_conv_w", "ffn_conv_w", "ffn_conv_b"]
    small_local = {
        "mix_pre_g": jnp.concatenate([g["g_pre"] for g in mix_grads]),
        "mix_post_g": jnp.concatenate([g["g_post"] for g in mix_grads]),
        "ffn_pre_g": jnp.concatenate([g["g_pre"] for g in ffn_grads]),
        "ffn_post_g": jnp.concatenate([g["g_post"] for g in ffn_grads]),
        "ssd_conv_w": stack(ssd_l, "cw"), "ssd_conv_b": stack(ssd_l, "cb"), "ssd_dt_bias": stack(ssd_l, "dt_bias"),
        "ssd_A_log": stack(ssd_l, "a_log"), "ssd_D": stack(ssd_l, "d"), "ssd_norm_w": stack(ssd_l, "nw"),
        "sc_conv_w": stack(sc_l, "cw"), "ffn_conv_w": stack(ffn_grads, "cw"), "ffn_conv_b": stack(ffn_grads, "cb"),
    }
    small_full_shapes = [small_local[n].shape for n in small_names]
    spack = _pack_rows([small_local[n].reshape(-1) for n in small_names] + [loss_part.reshape(-1)], LANE, 8)
    stotal = _sum_slots(_allgather_small(spack), "sum_small").reshape(-1)
    small_grads = dict(zip(small_names, _unpack(stotal, small_full_shapes)))
    loss = stotal[sum(int(np.prod(s)) for s in small_full_shapes)]
    for n in conv_names:
        width = wts[n].shape[-1]
        small_grads[n] = lax.dynamic_slice_in_dim(small_grads[n], q_chip * width, width, axis=2)

    grads, delta, new_m, new_v = {}, {}, {}, {}
    for n, _ in BIG:
        s = wts[n].shape
        two_d = lambda a: a.reshape(-1, s[-1])
        grads[n] = big_grads[n]
        d, mn, vn = _adamw(two_d(wts[n]), two_d(mom[n]), two_d(var[n]), two_d(grads[n]), "adamw_" + n)
        delta[n], new_m[n], new_v[n] = d.reshape(s), mn.reshape(s), vn.reshape(s)
    small_shapes = [wts[n].shape for n in small_names]
    pk = lambda d: _pack_rows([d[n].reshape(-1) for n in small_names], LANE, 8)
    for n in small_names:
        grads[n] = small_grads[n].reshape(wts[n].shape)
    d, mn, vn = _adamw(pk(wts), pk(mom), pk(var), pk(grads), "adamw_small")
    for out, packed in ((delta, d), (new_m, mn), (new_v, vn)):
        out.update(zip(small_names, _unpack(packed.reshape(-1), small_shapes)))

    return (loss, grad_x, *[grads[n] for n in names], *[delta[n] for n in names], *[new_m[n] for n in names],
            *[new_v[n] for n in names])
```

```python
from typing import Callable, NamedTuple

import jax
import jax.numpy as jnp
import numpy as np
from jax import lax
from jax.experimental import pallas as pl
from jax.experimental.pallas import tpu as pltpu

F32 = jnp.float32
BF16 = jnp.bfloat16
SDS = jax.ShapeDtypeStruct
MESH = pl.DeviceIdType.MESH
ANY = pl.BlockSpec(memory_space=pl.ANY)

EPS = 1e-6
CHUNK = 64
HEAD_DIM = 64
N_GROUPS = 8
D_STATE = 128
HEADS_PER_GROUP = 4
GROUP_W = HEADS_PER_GROUP * HEAD_DIM
LANE = 128
ROW_TILE = 128
HALO = 8
VMEM_LIMIT = 56 * 1024 * 1024

ADAM_LR = 0.001
ADAM_B1 = 0.9
ADAM_B2 = 0.999
ADAM_EPS = 1e-08
ADAM_WD = 0.01
ADAM_STEP = 10

NT = (((1,), (1,)), ((), ()))
TN = (((0,), (0,)), ((), ()))


def _cp(*sem):
    return pltpu.CompilerParams(dimension_semantics=sem or None, vmem_limit_bytes=VMEM_LIMIT)


def _sigmoid(x):
    return 1.0 / (1.0 + jnp.exp(-x))


def _dsilu(x, s):
    return s * (1.0 + x * (1.0 - s))


def _rsq(x):
    return lax.rsqrt(jnp.mean(x * x, axis=-1, keepdims=True) + EPS)


MM_ROWS = 256


class _Epilogue(NamedTuple):
    ins: list
    cols: int
    dtype: object
    fn: Callable


def _norm_matmul(x, g, w, out_dtype, name, comm=None, epilogue=None):
    L, D = x.shape
    N = w.shape[1]
    tm = min(MM_ROWS, L)
    n_extra = 0 if epilogue is None else len(epilogue.ins)

    def body(x_ref, g_ref, w_ref, *refs):
        extra, (o_ref, hn_ref), rest = refs[:n_extra], refs[n_extra:n_extra + 2], refs[n_extra + 2:]
        xv = x_ref[...]
        hn = (xv * _rsq(xv) * g_ref[...]).astype(BF16)
        hn_ref[...] = hn
        o_ref[...] = jnp.dot(hn, w_ref[...], preferred_element_type=F32).astype(out_dtype)
        if epilogue is not None:
            res_ref, tail_ref = rest

            @pl.when(pl.program_id(0) == 0)
            def _():
                tail_ref[...] = jnp.zeros_like(tail_ref)

            epilogue.fn(o_ref, extra, res_ref, tail_ref, tm)

    row = lambda i: (i, 0)
    fix = lambda i: (0, 0)
    in_specs = [pl.BlockSpec((tm, D), row), pl.BlockSpec((1, D), fix), pl.BlockSpec((D, N), fix)]
    out_specs = [pl.BlockSpec((tm, N), row), pl.BlockSpec((tm, D), row)]
    out_shape = [SDS((L, N), out_dtype), SDS((L, D), BF16)]
    scratch, args = [], (x, g, w)
    if epilogue is not None:
        in_specs += [pl.BlockSpec(a.shape, fix) for a in epilogue.ins]
        out_specs.append(pl.BlockSpec((tm, epilogue.cols), row))
        out_shape.append(SDS((L, epilogue.cols), epilogue.dtype))
        scratch = [pltpu.VMEM((HALO, epilogue.cols), F32)]
        args += tuple(epilogue.ins)
    return _carrier_call(body, comm, L // tm, in_specs=in_specs, out_specs=out_specs, out_shape=out_shape,
                         scratch_shapes=scratch, name=name, args=args)


def _conv_taps(x, tail, taps, tm):
    ext = jnp.concatenate([tail, x], axis=0)
    out = x * taps[0]
    for k in range(1, len(taps)):
        out = out + pltpu.roll(ext, k, axis=0)[HALO:HALO + tm] * taps[k]
    return out


def _ssd_conv_epilogue(cw, cb, col0):
    C = cw.shape[1]

    def fn(o_ref, ins, res_ref, tail_ref, tm):
        cw_ref, cb_ref = ins
        for s in range(C // LANE):
            sl = slice(s * LANE, (s + 1) * LANE)
            xv = o_ref[:, col0 + s * LANE:col0 + (s + 1) * LANE].astype(F32)
            cv = _conv_taps(xv, tail_ref[:, sl], [cw_ref[3 - k:4 - k, sl] for k in range(4)], tm) + cb_ref[:, sl]
            res_ref[:, sl] = cv * _sigmoid(cv)
            tail_ref[:, sl] = xv[tm - HALO:tm]

    return _Epilogue([cw, cb], C, F32, fn)


def _sc_gate_epilogue(cw):
    C = cw.shape[1]

    def fn(o_ref, ins, res_ref, tail_ref, tm):
        (cw_ref,) = ins
        for s in range(C // LANE):
            sl = slice(s * LANE, (s + 1) * LANE)
            gb, gc, v = (o_ref[:, (3 * s + k) * LANE:(3 * s + k + 1) * LANE].astype(F32) for k in range(3))
            p = gc * v
            u = _conv_taps(p, tail_ref[:, sl], [cw_ref[2 - k:3 - k, sl] for k in range(3)], tm)
            res_ref[:, sl] = (gb * u).astype(BF16)
            tail_ref[:, sl] = p[tm - HALO:tm]

    return _Epilogue([cw], C, BF16, fn)


def _matmul_norm_res(a, w, x, g, name, comm=None):
    L, K = a.shape
    D = w.shape[1]
    tm = min(MM_ROWS, L)

    def body(a_ref, w_ref, x_ref, g_ref, m_ref, xo_ref):
        m = jnp.dot(a_ref[...], w_ref[...], preferred_element_type=F32)
        m_ref[...] = m
        xo_ref[...] = x_ref[...] + m * _rsq(m) * g_ref[...]

    row = lambda i: (i, 0)
    fix = lambda i: (0, 0)
    return _carrier_call(
        body, comm, L // tm,
        in_specs=[pl.BlockSpec((tm, K), row), pl.BlockSpec((K, D), fix), pl.BlockSpec((tm, D), row),
                  pl.BlockSpec((1, D), fix)],
        out_specs=[pl.BlockSpec((tm, D), row), pl.BlockSpec((tm, D), row)],
        out_shape=[SDS((L, D), F32), SDS((L, D), F32)], scratch_shapes=[], name=name, args=(a, w, x, g))


def _postnorm_bwd_matmul_nt(dx, m, g, w, out_dtype, name):
    L, D = dx.shape
    K = w.shape[0]
    tm = min(MM_ROWS, L)

    def body(dx_ref, m_ref, g_ref, w_ref, da_ref, dm_ref, dg_ref):
        @pl.when(pl.program_id(0) == 0)
        def _():
            dg_ref[...] = jnp.zeros_like(dg_ref)

        m = m_ref[...]
        dxv = dx_ref[...]
        r = _rsq(m)
        mh = m * r
        dg_ref[...] += jnp.sum(dxv * mh, axis=0, keepdims=True)
        dyg = dxv * g_ref[...]
        dm = (r * (dyg - mh * jnp.mean(dyg * mh, axis=-1, keepdims=True))).astype(BF16)
        dm_ref[...] = dm
        da_ref[...] = lax.dot_general(dm, w_ref[...], NT, preferred_element_type=F32).astype(out_dtype)

    row = lambda i: (i, 0)
    fix = lambda i: (0, 0)
    return pl.pallas_call(
        body, grid=(L // tm,),
        in_specs=[pl.BlockSpec((tm, D), row), pl.BlockSpec((tm, D), row), pl.BlockSpec((1, D), fix),
                  pl.BlockSpec((K, D), fix)],
        out_specs=[pl.BlockSpec((tm, K), row), pl.BlockSpec((tm, D), row), pl.BlockSpec((1, D), fix)],
        out_shape=[SDS((L, K), out_dtype), SDS((L, D), BF16), SDS((1, D), F32)],
        name=name, compiler_params=_cp("arbitrary"))(dx, m, g, w)


DW_ACC_BYTES = 13 * 512 * 1024


def _dw_tiles(ka, n):
    ta = ka if ka <= 1024 else ka // 2
    fits = [d for d in range(LANE, n + 1, LANE) if n % d == 0 and ta * d * 4 <= DW_ACC_BYTES]
    return ta, max(fits)


def _matmul_tn(a, b, name):
    L, Ka = a.shape
    N = b.shape[1]
    ta, tn = _dw_tiles(Ka, N)
    tl = min(512, L)
    n_l = L // tl

    def body(a_ref, b_ref, o_ref, acc_ref):
        l = pl.program_id(2)

        @pl.when(l == 0)
        def _():
            acc_ref[...] = jnp.zeros_like(acc_ref)

        acc_ref[...] += lax.dot_general(a_ref[...], b_ref[...], TN, preferred_element_type=F32)

        @pl.when(l == n_l - 1)
        def _():
            o_ref[...] = acc_ref[...].astype(BF16)

    return pl.pallas_call(
        body, grid=(Ka // ta, N // tn, n_l),
        in_specs=[pl.BlockSpec((tl, ta), lambda i, j, l: (l, i)), pl.BlockSpec((tl, tn), lambda i, j, l: (l, j))],
        out_specs=pl.BlockSpec((ta, tn), lambda i, j, l: (i, j)),
        out_shape=SDS((Ka, N), BF16),
        scratch_shapes=[pltpu.VMEM((ta, tn), F32)],
        name=name, compiler_params=_cp("parallel", "parallel", "arbitrary"))(a, b)


def _ffn_up(x, g, w4, cw, cb, name, comm=None):
    L, D = x.shape
    b = w4.shape[2]
    C = 2 * b
    tm = min(MM_ROWS, L)

    def body(x_ref, g_ref, w_ref, cw_ref, cb_ref, o_ref, hn_ref, a_ref, tail_ref):
        @pl.when(pl.program_id(0) == 0)
        def _():
            tail_ref[...] = jnp.zeros_like(tail_ref)

        xv = x_ref[...]
        hn = (xv * _rsq(xv) * g_ref[...]).astype(BF16)
        hn_ref[...] = hn
        for half in range(2):
            cols = slice(half * b, (half + 1) * b)
            for part in range(2):
                o_ref[part, :, cols] = jnp.dot(hn, w_ref[2 * part + half], preferred_element_type=F32).astype(BF16)
            for s in range(half * b // LANE, (half + 1) * b // LANE):
                sl = slice(s * LANE, (s + 1) * LANE)
                gp = o_ref[0, :, sl].astype(F32)
                ext = jnp.concatenate([tail_ref[:, sl], gp], axis=0)
                back = lambda k: pltpu.roll(ext, k, axis=0)[HALO:HALO + tm]
                gate = gp * cw_ref[2:3, sl] + back(1) * cw_ref[1:2, sl] + back(2) * cw_ref[0:1, sl] + cb_ref[:, sl]
                a_ref[:, sl] = (gate * _sigmoid(gate) * o_ref[1, :, sl].astype(F32)).astype(BF16)
                tail_ref[:, sl] = gp[tm - HALO:tm]

    row = lambda i: (i, 0)
    fix = lambda i: (0, 0)
    return _carrier_call(
        body, comm, L // tm,
        in_specs=[pl.BlockSpec((tm, D), row), pl.BlockSpec((1, D), fix), pl.BlockSpec((4, D, b), lambda i: (0, 0, 0)),
                  pl.BlockSpec(cw.shape, fix), pl.BlockSpec(cb.shape, fix)],
        out_specs=[pl.BlockSpec((2, tm, C), lambda i: (0, i, 0)), pl.BlockSpec((tm, D), row), pl.BlockSpec((tm, C), row)],
        out_shape=[SDS((2, L, C), BF16), SDS((L, D), BF16), SDS((L, C), BF16)],
        scratch_shapes=[pltpu.VMEM((HALO, C), F32)], name=name, args=(x, g, w4, cw, cb))


def _ffn_up_dx(dup, w4, x, g, dres, name):
    _, L, _ = dup.shape
    _, D, b = w4.shape
    tm = min(MM_ROWS, L)

    def body(dy_ref, w_ref, x_ref, g_ref, dres_ref, dx_ref, dg_ref):
        @pl.when(pl.program_id(0) == 0)
        def _():
            dg_ref[...] = jnp.zeros_like(dg_ref)

        dh = jnp.zeros((tm, D), F32)
        for q in range(4):
            dh = dh + lax.dot_general(dy_ref[q // 2, :, (q % 2) * b:(q % 2 + 1) * b], w_ref[q], NT,
                                      preferred_element_type=F32)
        xv = x_ref[...]
        r = _rsq(xv)
        xh = xv * r
        dg_ref[...] += jnp.sum(dh * xh, axis=0, keepdims=True)
        dyg = dh * g_ref[...]
        dx_ref[...] = dres_ref[...] + r * (dyg - xh * jnp.mean(dyg * xh, axis=-1, keepdims=True))

    row = lambda i: (i, 0)
    fix = lambda i: (0, 0)
    return pl.pallas_call(
        body, grid=(L // tm,),
        in_specs=[pl.BlockSpec((2, tm, 2 * b), lambda i: (0, i, 0)), pl.BlockSpec((4, D, b), lambda i: (0, 0, 0)),
                  pl.BlockSpec((tm, D), row), pl.BlockSpec((1, D), fix), pl.BlockSpec((tm, D), row)],
        out_specs=[pl.BlockSpec((tm, D), row), pl.BlockSpec((1, D), fix)],
        out_shape=[SDS((L, D), F32), SDS((1, D), F32)],
        name=name, compiler_params=_cp("arbitrary"))(dup, w4, x, g, dres)


def _ffn_up_dw(hn, dup, name):
    L, D = hn.shape
    b = dup.shape[2] // 2
    tl = min(512, L)
    n_l = L // tl

    def body(a_ref, b_ref, o_ref, acc_ref):
        l = pl.program_id(1)

        @pl.when(l == 0)
        def _():
            acc_ref[...] = jnp.zeros_like(acc_ref)

        acc_ref[...] += lax.dot_general(a_ref[...], b_ref[...], TN, preferred_element_type=F32)

        @pl.when(l == n_l - 1)
        def _():
            o_ref[...] = acc_ref[...].astype(BF16)

    return pl.pallas_call(
        body, grid=(4, n_l),
        in_specs=[pl.BlockSpec((tl, D), lambda q, l: (l, 0)),
                  pl.BlockSpec((None, tl, b), lambda q, l: (q // 2, l, q % 2))],
        out_specs=pl.BlockSpec((None, D, b), lambda q, l: (q, 0, 0)),
        out_shape=SDS((4, D, b), BF16),
        scratch_shapes=[pltpu.VMEM((D, b), F32)],
        name=name, compiler_params=_cp("parallel", "arbitrary"))(hn, dup)


def _matmul_nt_prenorm_bwd(dy, w, x, g, dres, name, comm=None):
    L, N = dy.shape
    D = w.shape[0]
    tm = min(MM_ROWS, L)

    def body(dy_ref, w_ref, x_ref, g_ref, dres_ref, dx_ref, dg_ref):
        @pl.when(pl.program_id(0) == 0)
        def _():
            dg_ref[...] = jnp.zeros_like(dg_ref)

        dh = lax.dot_general(dy_ref[...], w_ref[...], NT, preferred_element_type=F32)
        xv = x_ref[...]
        r = _rsq(xv)
        xh = xv * r
        dg_ref[...] += jnp.sum(dh * xh, axis=0, keepdims=True)
        dyg = dh * g_ref[...]
        dx_ref[...] = dres_ref[...] + r * (dyg - xh * jnp.mean(dyg * xh, axis=-1, keepdims=True))

    row = lambda i: (i, 0)
    fix = lambda i: (0, 0)
    return _carrier_call(
        body, comm, L // tm,
        in_specs=[pl.BlockSpec((tm, N), row), pl.BlockSpec((D, N), fix), pl.BlockSpec((tm, D), row),
                  pl.BlockSpec((1, D), fix), pl.BlockSpec((tm, D), row)],
        out_specs=[pl.BlockSpec((tm, D), row), pl.BlockSpec((1, D), fix)],
        out_shape=[SDS((L, D), F32), SDS((1, D), F32)], scratch_shapes=[], name=name, args=(dy, w, x, g, dres))


def _loss_head(y, t, name):
    L, D = y.shape
    tm = min(512, L)

    def body(y_ref, t_ref, dy_ref, loss_ref):
        @pl.when(pl.program_id(0) == 0)
        def _():
            loss_ref[...] = jnp.zeros_like(loss_ref)

        e = y_ref[...] - t_ref[...]
        dy_ref[...] = e * (1.0 / D)
        s = jnp.sum(jnp.sum(e * e, axis=1, keepdims=True), axis=0, keepdims=True)
        loss_ref[...] += s * (0.5 / D)

    row = lambda i: (i, 0)
    return pl.pallas_call(
        body, grid=(L // tm,),
        in_specs=[pl.BlockSpec((tm, D), row), pl.BlockSpec((tm, D), row)],
        out_specs=[pl.BlockSpec((tm, D), row), pl.BlockSpec((1, 1), lambda i: (0, 0))],
        out_shape=[SDS((L, D), F32), SDS((1, 1), F32)],
        name=name, compiler_params=_cp("arbitrary"))(y, t)


def _tile_rows(ref):
    return HALO * (4 // jnp.dtype(ref.dtype).itemsize)


def _prev_rows(ref, r0, i, cols):
    n = _tile_rows(ref)
    p0 = pl.multiple_of(jnp.maximum(r0 - n, 0), n)
    return jnp.where(i > 0, ref[pl.ds(p0, n), cols].astype(F32)[n - HALO:], 0.0)


def _next_rows(ref, r0, i, n_tiles, cols):
    n = _tile_rows(ref)
    n0 = pl.multiple_of(jnp.minimum(r0 + ROW_TILE, n_tiles * ROW_TILE - n), n)
    return jnp.where(i < n_tiles - 1, ref[pl.ds(n0, n), cols].astype(F32)[:HALO], 0.0)


def _rows_f32(ref, rows, cols):
    return ref[rows, cols].astype(F32)


def _back(ext, s):
    return pltpu.roll(ext, s, axis=0)[HALO:HALO + ROW_TILE]


def _fwd(ext, s):
    n = ext.shape[0]
    return pltpu.roll(ext, n - s, axis=0)[:ROW_TILE]


def _store_rows(ref, rows):
    ref[...] = jnp.zeros_like(ref)
    for k, v in enumerate(rows):
        ref[k:k + 1, :] = v


def _strip_call(body, L, n_strips, ins, outs, name):
    def spec(rows, width, off):
        if off is None:
            return pl.BlockSpec((rows, width), lambda j: (0, 0))
        return pl.BlockSpec((rows, width), lambda j: (0, j + off))

    return pl.pallas_call(
        body, grid=(n_strips,),
        in_specs=[spec(a.shape[0], w, off) for a, w, off in ins],
        out_specs=[spec(s.shape[0], w, off) for s, w, off in outs],
        out_shape=[s for s, _, _ in outs],
        name=name, compiler_params=_cp("parallel"))(*[a for a, _, _ in ins])


def _sc_mid_bwd(dq, bcv, cw, name):
    L, C = dq.shape
    n_tiles = L // ROW_TILE
    s0, s1, s2, al = slice(0, LANE), slice(LANE, 2 * LANE), slice(2 * LANE, 3 * LANE), slice(None)

    def body(dq_ref, x_ref, cw_ref, dx_ref, st_ref):
        w0, w1, w2 = cw_ref[0:1, :], cw_ref[1:2, :], cw_ref[2:3, :]

        def step(i, c):
            r0 = pl.multiple_of(i * ROW_TILE, ROW_TILE)
            rows = pl.ds(r0, ROW_TILE)
            gb, gc, v = _rows_f32(x_ref, rows, s0), _rows_f32(x_ref, rows, s1), _rows_f32(x_ref, rows, s2)
            dq_v = _rows_f32(dq_ref, rows, al)
            p = gc * v
            pext = jnp.concatenate([_prev_rows(x_ref, r0, i, s1) * _prev_rows(x_ref, r0, i, s2), p], axis=0)
            p1, p2 = _back(pext, 1), _back(pext, 2)
            u = p * w2 + p1 * w1 + p2 * w0
            du = dq_v * gb
            du_n = _next_rows(dq_ref, r0, i, n_tiles, al) * _next_rows(x_ref, r0, i, n_tiles, s0)
            ext = jnp.concatenate([du, du_n], axis=0)
            dp = du * w2 + _fwd(ext, 1) * w1 + _fwd(ext, 2) * w0
            dx_ref[rows, s0] = (dq_v * u).astype(BF16)
            dx_ref[rows, s1] = (dp * v).astype(BF16)
            dx_ref[rows, s2] = (dp * gc).astype(BF16)
            s = lambda t: jnp.sum(t, axis=0, keepdims=True)
            return (c[0] + s(du * p2), c[1] + s(du * p1), c[2] + s(du * p))

        z = jnp.zeros((1, LANE), F32)
        _store_rows(st_ref, lax.fori_loop(0, n_tiles, step, (z, z, z)))

    return _strip_call(body, L, C // LANE, [(dq, LANE, 0), (bcv, 3 * LANE, 0), (cw, LANE, 0)],
                       [(SDS((L, 3 * C), BF16), 3 * LANE, 0), (SDS((8, C), F32), LANE, 0)], name)


def _ssd_conv_bwd(dxbc, zx, cw, cb, col0, dzx, name):
    L, C = dxbc.shape
    n_tiles = L // ROW_TILE
    al = slice(None)

    def body(d_ref, x_ref, cw_ref, cb_ref, dzx_in_ref, o_ref, st_ref):
        w0, w1, w2, w3 = cw_ref[0:1, :], cw_ref[1:2, :], cw_ref[2:3, :], cw_ref[3:4, :]
        b = cb_ref[...]

        def step(i, c):
            r0 = pl.multiple_of(i * ROW_TILE, ROW_TILE)
            rows = pl.ds(r0, ROW_TILE)
            xv = x_ref[rows, :]
            xe = jnp.concatenate([_prev_rows(x_ref, r0, i, al), xv, _next_rows(x_ref, r0, i, n_tiles, al)], axis=0)
            x1, x2, x3 = pltpu.roll(xe, 1, axis=0), pltpu.roll(xe, 2, axis=0), pltpu.roll(xe, 3, axis=0)
            cv = (xe * w3 + x1 * w2 + x2 * w1 + x3 * w0 + b)[HALO:]
            de = jnp.concatenate([d_ref[rows, :], _next_rows(d_ref, r0, i, n_tiles, al)], axis=0)
            dc_ext = de * _dsilu(cv, _sigmoid(cv))
            dc = dc_ext[:ROW_TILE]
            o_ref[rows, :] = (dc * w3 + _fwd(dc_ext, 1) * w2 + _fwd(dc_ext, 2) * w1 + _fwd(dc_ext, 3) * w0).astype(BF16)
            s = lambda t: jnp.sum(t, axis=0, keepdims=True)
            t = slice(HALO, HALO + ROW_TILE)
            return (c[0] + s(dc * x3[t]), c[1] + s(dc * x2[t]), c[2] + s(dc * x1[t]), c[3] + s(dc * xv), c[4] + s(dc))

        z = jnp.zeros((1, LANE), F32)
        _store_rows(st_ref, lax.fori_loop(0, n_tiles, step, (z, z, z, z, z)))

    strip = lambda rows, off=0: pl.BlockSpec((rows, LANE), lambda j: (0, j + off))
    shifted = strip(L, col0 // LANE)
    return pl.pallas_call(
        body, grid=(C // LANE,), in_specs=[strip(L), shifted, strip(cw.shape[0]), strip(1), ANY],
        out_specs=[shifted, strip(8)], out_shape=[SDS(dzx.shape, dzx.dtype), SDS((8, C), F32)],
        input_output_aliases={4: 0}, name=name, compiler_params=_cp("parallel"))(dxbc, zx, cw, cb, dzx)


def _scan_constants(n_heads):
    hw = n_heads * HEAD_DIM
    col = np.arange(hw)
    ind = (col[None, :] // HEAD_DIM == np.arange(LANE)[:, None]).astype(np.float32)
    gcol = np.arange(GROUP_W)
    itile = (gcol[None, :] % CHUNK == np.arange(CHUNK)[:, None]).astype(np.float32)
    trit = (gcol[None, :] % CHUNK <= np.arange(CHUNK)[:, None]).astype(np.float32)
    tril = np.tril(np.ones((CHUNK, CHUNK), np.float32))
    bmask = (gcol[:, None] // HEAD_DIM == gcol[None, :] // HEAD_DIM).astype(np.float32)
    return (jnp.asarray(ind.T.copy(), BF16), jnp.asarray(itile), jnp.asarray(trit), jnp.asarray(tril, BF16),
            jnp.asarray(bmask))


def _softplus(x):
    return jnp.maximum(x, 0.0) + jnp.log(1.0 + jnp.exp(-jnp.abs(x)))


def _split3(x):
    hi = x.astype(BF16)
    r1 = x - hi.astype(F32)
    mid = r1.astype(BF16)
    return hi, mid, (r1 - mid.astype(F32)).astype(BF16)


def _dot_sel(x, sel, dims=None):
    if dims is None:
        mm = lambda p: jnp.dot(p, sel, preferred_element_type=F32)
    else:
        mm = lambda p: lax.dot_general(sel, p, dims, preferred_element_type=F32)
    hi, mid, lo = _split3(x)
    return (mm(lo) + mm(mid)) + mm(hi)


SEL_X = (((1,), (0,)), ((), ()))


def _head_lanes(v, g):
    h0 = HEADS_PER_GROUP * g
    return jnp.concatenate([jnp.broadcast_to(v[:, h0 + r:h0 + r + 1], (v.shape[0], HEAD_DIM))
                            for r in range(HEADS_PER_GROUP)], axis=1)


def _group_terms(g, dt, cs, cst, xbc_ref, trit, bmask, d_inner):
    gl = slice(g * GROUP_W, (g + 1) * GROUP_W)
    h0 = HEADS_PER_GROUP * g
    csl = _head_lanes(cs, g)
    dtx = _head_lanes(dt, g)
    rr = jnp.concatenate([cst[h0 + r:h0 + r + 1, :] for r in range(HEADS_PER_GROUP)], axis=1)
    lm = jnp.exp(jnp.where(trit > 0.0, csl - rr, -jnp.inf))
    xs = xbc_ref[:, gl]
    b = xbc_ref[:, d_inner + g * D_STATE: d_inner + (g + 1) * D_STATE]
    c = xbc_ref[:, d_inner + (N_GROUPS + g) * D_STATE: d_inner + (N_GROUPS + g + 1) * D_STATE]
    u = xs * dtx
    bb, cb = b.astype(BF16), c.astype(BF16)
    btile = jnp.concatenate([bb] * HEADS_PER_GROUP, axis=0)
    cbt = lax.dot_general(cb, btile, NT, preferred_element_type=F32)
    m = cbt * lm
    ub = u.astype(BF16)
    bdu = jnp.where(bmask > 0.0, jnp.concatenate([ub] * HEADS_PER_GROUP, axis=0), jnp.zeros((), BF16))
    c_last = csl[CHUNK - 1:CHUNK, :]
    return dict(gl=gl, csl=csl, dtx=dtx, lm=lm, xs=xs, bb=bb, cb=cb, u=u, btile=btile, m=m, bdu=bdu,
                e=jnp.exp(csl), dec=jnp.exp(c_last - csl), e_last=jnp.exp(c_last))


def _ssd_scan_fwd(zx, xbc, par, dexp, nw, consts, comm, name):
    L = xbc.shape[0]
    d_inner = dexp.shape[1]
    n_chunks = L // CHUNK
    dt_blk = zx.shape[1] // LANE - 1
    ind_t, itile_c, trit_c, tril_c, bmask_c = consts

    def body(xbc_ref, z_ref, dtr_ref, par_ref, dexp_ref, nw_ref, trit_ref, tril_ref, bmask_ref,
             yn_ref, yf_ref, st_out_ref, st_ref):
        @pl.when(pl.program_id(0) == 0)
        def _():
            st_ref[...] = jnp.zeros_like(st_ref)

        dt = _softplus(dtr_ref[...] + par_ref[0:1, :])
        a_head = -jnp.exp(par_ref[1:2, :])
        cs = _dot_sel(dt * a_head, tril_ref[...], SEL_X)
        cst = cs.T
        trit, bmask = trit_ref[...], bmask_ref[...]
        for g in range(N_GROUPS):
            t = _group_terms(g, dt, cs, cst, xbc_ref, trit, bmask, d_inner)
            p = st_ref[g]
            st_out_ref[0, g] = p
            y = jnp.dot(t["m"].astype(BF16), t["bdu"], preferred_element_type=F32)
            y = y + jnp.dot(t["cb"], p.astype(BF16), preferred_element_type=F32) * t["e"]
            st_new = lax.dot_general(t["bb"], (t["u"] * t["dec"]).astype(BF16), TN, preferred_element_type=F32)
            st_ref[g] = p * t["e_last"] + st_new
            yf_ref[:, t["gl"]] = y + t["xs"] * dexp_ref[:, t["gl"]]
        z = z_ref[...]
        y2 = yf_ref[...] * (z * _sigmoid(z))
        yn_ref[...] = (y2 * _rsq(y2) * nw_ref[...]).astype(BF16)

    row = lambda c: (c, 0)
    fix = lambda c: (0, 0)
    cspec = lambda a: pl.BlockSpec(a.shape, fix)
    return _carrier_call(
        body, comm, n_chunks,
        in_specs=[pl.BlockSpec((CHUNK, xbc.shape[1]), row), pl.BlockSpec((CHUNK, d_inner), row),
                  pl.BlockSpec((CHUNK, LANE), lambda c: (c, dt_blk)), cspec(par), cspec(dexp), cspec(nw),
                  cspec(trit_c), cspec(tril_c), cspec(bmask_c)],
        out_specs=[pl.BlockSpec((CHUNK, d_inner), row), pl.BlockSpec((CHUNK, d_inner), row),
                   pl.BlockSpec((1, N_GROUPS, D_STATE, GROUP_W), lambda c: (c, 0, 0, 0))],
        out_shape=[SDS((L, d_inner), BF16), SDS((L, d_inner), F32),
                   SDS((n_chunks, N_GROUPS, D_STATE, GROUP_W), F32)],
        scratch_shapes=[pltpu.VMEM((N_GROUPS, D_STATE, GROUP_W), F32)],
        name=name, args=(xbc, zx, zx, par, dexp, nw, trit_c, tril_c, bmask_c))


def _ssd_scan_bwd(dyn, yf, zx, xbc, states, par, dexp, nw, consts, comm, name):
    L = xbc.shape[0]
    d_inner = dexp.shape[1]
    n_chunks = L // CHUNK
    nz = zx.shape[1]
    dt_blk = nz // LANE - 1
    ind_t, itile_c, trit_c, tril_c, bmask_c = consts
    hslices = [slice(r * HEAD_DIM, (r + 1) * HEAD_DIM) for r in range(HEADS_PER_GROUP)]

    def body(dyn_ref, yf_ref, z_ref, dtr_ref, xbc_ref, st_in_ref, par_ref, dexp_ref, nw_ref, indt_ref,
             itile_ref, trit_ref, tril_ref, bmask_ref,
             dzx_ref, dxbc_ref, dnw_ref, dpar_ref, dq_ref, dyf_ref):
        @pl.when(pl.program_id(0) == 0)
        def _():
            dq_ref[...] = jnp.zeros_like(dq_ref)
            dnw_ref[...] = jnp.zeros_like(dnw_ref)
            dpar_ref[...] = jnp.zeros_like(dpar_ref)

        z, yfv, dynv = z_ref[...], yf_ref[...], dyn_ref[...]
        sz = _sigmoid(z)
        y2 = yfv * (z * sz)
        r = _rsq(y2)
        y2h = y2 * r
        dnw_ref[...] += jnp.sum(dynv * y2h, axis=0, keepdims=True)
        dyg = dynv * nw_ref[...]
        dy2 = r * (dyg - y2h * jnp.mean(dyg * y2h, axis=-1, keepdims=True))
        dzx_ref[:, 0:d_inner] = (dy2 * yfv * _dsilu(z, sz)).astype(BF16)
        dyf_ref[...] = dy2 * (z * sz)

        pre = dtr_ref[...] + par_ref[0:1, :]
        dt = _softplus(pre)
        a_head = -jnp.exp(par_ref[1:2, :])
        cs = _dot_sel(dt * a_head, tril_ref[...], SEL_X)
        cst = cs.T
        itile, trit, bmask = itile_ref[...], trit_ref[...], bmask_ref[...]
        dcs = jnp.zeros((CHUNK, LANE), F32)
        dcs_last = jnp.zeros((1, LANE), F32)
        ddt_u = jnp.zeros((CHUNK, LANE), F32)
        d_skip = jnp.zeros((1, LANE), F32)
        rsum = lambda v: jnp.sum(v, axis=0, keepdims=True)
        row8 = lax.broadcasted_iota(jnp.int32, (8, GROUP_W), 0)
        for g in range(N_GROUPS):
            t = _group_terms(g, dt, cs, cst, xbc_ref, trit, bmask, d_inner)
            gl, m, lm, u, bb, cb, e, dec, xs = (t[k] for k in ("gl", "m", "lm", "u", "bb", "cb", "e", "dec", "xs"))
            indt = indt_ref[gl, :]
            dy = dyf_ref[:, gl]
            dyb = dy.astype(BF16)
            p = st_in_ref[0, g]
            pb = p.astype(BF16)
            q = dq_ref[g]
            qb = q.astype(BF16)
            big = lax.dot_general(m.astype(BF16), dyb, TN, preferred_element_type=F32)
            du = jnp.zeros((CHUNK, GROUP_W), F32)
            for rh in range(HEADS_PER_GROUP):
                du = du + big[hslices[rh], :] * bmask[rh * HEAD_DIM:rh * HEAD_DIM + 1, :]
            dm = lax.dot_general(dyb, t["bdu"], NT, preferred_element_type=F32)
            w = dm * m
            dgt = (dm * lm).astype(BF16)
            dc = jnp.dot(dgt, t["btile"], preferred_element_type=F32)
            db_big = lax.dot_general(dgt, cb, TN, preferred_element_type=F32)
            db = db_big[hslices[0], :] + db_big[hslices[1], :] + db_big[hslices[2], :] + db_big[hslices[3], :]
            cp = jnp.dot(cb, pb, preferred_element_type=F32)
            dye = dy * e
            dyeb = dye.astype(BF16)
            dc = dc + lax.dot_general(dyeb, pb, NT, preferred_element_type=F32)
            dp = lax.dot_general(cb, dyeb, TN, preferred_element_type=F32)
            x2 = dye * cp
            bq = jnp.dot(bb, qb, preferred_element_type=F32)
            ud = u * dec
            du = du + bq * dec
            db = db + lax.dot_general(ud.astype(BF16), qb, NT, preferred_element_type=F32)
            x1 = bq * ud
            dq_ref[g] = dp + t["e_last"] * q
            x3 = rsum(q * p) * t["e_last"]
            red = _dot_sel(jnp.concatenate([w + x2 - x1, du * xs, itile * rsum(w)], axis=0), indt)
            dcs = dcs + red[0:CHUNK] - red[2 * CHUNK:3 * CHUNK]
            ddt_u = ddt_u + red[CHUNK:2 * CHUNK]
            tail = _dot_sel(jnp.where(row8 == 0, rsum(x1) + x3, jnp.where(row8 == 1, rsum(dy * xs), 0.0)), indt)
            dcs_last = dcs_last + tail[0:1]
            d_skip = d_skip + tail[1:2]
            dxbc_ref[:, gl] = du * t["dtx"] + dy * dexp_ref[:, gl]
            dxbc_ref[:, d_inner + g * D_STATE: d_inner + (g + 1) * D_STATE] = db
            dxbc_ref[:, d_inner + (N_GROUPS + g) * D_STATE: d_inner + (N_GROUPS + g + 1) * D_STATE] = dc
        last = lax.broadcasted_iota(jnp.int32, (CHUNK, LANE), 0) == CHUNK - 1
        dcs = dcs + jnp.where(last, dcs_last, 0.0)
        da = _dot_sel(dcs, tril_ref[...], TN)
        ddt = da * a_head + ddt_u
        heads = lax.broadcasted_iota(jnp.int32, (CHUNK, LANE), 1) < d_inner // HEAD_DIM
        ddt_raw = jnp.where(heads, ddt * _sigmoid(pre), 0.0)
        dzx_ref[:, nz - LANE:nz] = ddt_raw.astype(BF16)
        dpar_ref[0:1, :] += rsum(ddt_raw)
        dpar_ref[1:2, :] += rsum(da * dt) * a_head
        dpar_ref[2:3, :] += d_skip

    rev = lambda c: (n_chunks - 1 - c, 0)
    fix = lambda c: (0, 0)
    cspec = lambda a: pl.BlockSpec(a.shape, fix)
    nx = xbc.shape[1]
    return _carrier_call(
        body, comm, n_chunks,
        in_specs=[pl.BlockSpec((CHUNK, d_inner), rev), pl.BlockSpec((CHUNK, d_inner), rev),
                  pl.BlockSpec((CHUNK, d_inner), rev), pl.BlockSpec((CHUNK, LANE), lambda c: (n_chunks - 1 - c, dt_blk)),
                  pl.BlockSpec((CHUNK, nx), rev),
                  pl.BlockSpec((1, N_GROUPS, D_STATE, GROUP_W), lambda c: (n_chunks - 1 - c, 0, 0, 0)),
                  cspec(par), cspec(dexp), cspec(nw), cspec(ind_t), cspec(itile_c), cspec(trit_c),
                  cspec(tril_c), cspec(bmask_c)],
        out_specs=[pl.BlockSpec((CHUNK, nz), rev), pl.BlockSpec((CHUNK, nx), rev),
                   pl.BlockSpec((1, d_inner), fix), pl.BlockSpec((8, LANE), fix)],
        out_shape=[SDS((L, nz), BF16), SDS((L, nx), F32), SDS((1, d_inner), F32), SDS((8, LANE), F32)],
        scratch_shapes=[pltpu.VMEM((N_GROUPS, D_STATE, GROUP_W), F32), pltpu.VMEM((CHUNK, d_inner), F32)],
        name=name, args=(dyn, yf, zx, zx, xbc, states, par, dexp, nw, ind_t, itile_c, trit_c, tril_c, bmask_c))


def _adamw(w, m, v, g, name):
    R, C = w.shape
    tr = R
    for cand in (256, 128, 64, 32, 16, 8):
        if R % cand == 0:
            tr = cand
            break

    def body(w_ref, m_ref, v_ref, g_ref, d_ref, mo_ref, vo_ref):
        gv = g_ref[...]
        mn = ADAM_B1 * m_ref[...] + (1.0 - ADAM_B1) * gv
        vn = ADAM_B2 * v_ref[...] + (1.0 - ADAM_B2) * (gv * gv)
        m_hat = mn / (1.0 - ADAM_B1 ** ADAM_STEP)
        v_hat = vn / (1.0 - ADAM_B2 ** ADAM_STEP)
        d_ref[...] = -ADAM_LR * (m_hat / (jnp.sqrt(v_hat) + ADAM_EPS) + ADAM_WD * w_ref[...])
        mo_ref[...] = mn
        vo_ref[...] = vn

    blk = pl.BlockSpec((tr, C), lambda i: (i, 0))
    return pl.pallas_call(
        body, grid=(R // tr,), in_specs=[blk] * 4, out_specs=[blk] * 3, out_shape=[SDS((R, C), F32)] * 3,
        name=name, compiler_params=_cp("parallel"))(w, m, v, g)


def _sum_slots(parts, name):
    n, R, C = parts.shape
    tr = 128 if R % 128 == 0 else R

    def body(p_ref, o_ref):
        acc = p_ref[0]
        for k in range(1, n):
            acc = acc + p_ref[k]
        o_ref[...] = acc

    return pl.pallas_call(
        body, grid=(R // tr,), in_specs=[pl.BlockSpec((n, tr, C), lambda i: (0, i, 0))],
        out_specs=pl.BlockSpec((tr, C), lambda i: (i, 0)), out_shape=SDS((R, C), F32),
        name=name, compiler_params=_cp("parallel"))(parts)


def _add_core_halves(where, g, r, name):
    _, n, a, b = g.shape
    ta = a // 2

    def body(w_ref, g_ref, r_ref, o_ref):
        o_ref[...] = (g_ref[...].astype(F32) + r_ref[...].astype(F32)).astype(BF16)

    blk = lambda f: pl.BlockSpec((None, None, ta, b), f)
    mine = lambda s, l, w: (s, l, 0, 0)
    return pl.pallas_call(
        body, grid_spec=pltpu.PrefetchScalarGridSpec(
            num_scalar_prefetch=1, grid=(4, n),
            in_specs=[blk(lambda s, l, w: (s, l, w[1], 0)), blk(mine)], out_specs=blk(mine)),
        out_shape=SDS((4, n, ta, b), BF16), name=name,
        compiler_params=_cp("parallel", "parallel"))(where, g, r)


def _sum_shard(where, g, r, rr, into, layer, n_layers, name):
    _, _, a, b = g.shape
    ta = a // 2

    def body(w_ref, g_ref, r_ref, rr_ref, *refs):
        f = lambda v: v.astype(F32)
        refs[-1][...] = (((f(g_ref[...]) + f(r_ref[...])) + f(rr_ref[0])) + f(rr_ref[1])) + f(rr_ref[2])

    more = [] if into is None else [into]
    return pl.pallas_call(
        body, grid_spec=pltpu.PrefetchScalarGridSpec(
            num_scalar_prefetch=1, grid=(1,),
            in_specs=[pl.BlockSpec((None, None, ta, b), lambda l, w: (w[0], 0, w[1], 0)),
                      pl.BlockSpec((None, None, ta, b), lambda l, w: (w[0], 0, 0, 0)),
                      pl.BlockSpec((3, None, ta, b), lambda l, w: (0, 0, 0, 0))] + [ANY] * len(more),
            out_specs=pl.BlockSpec((None, ta, b), lambda l, w: (layer, w[1], 0))),
        out_shape=SDS((n_layers, a, b), F32), name=name, input_output_aliases={4: 0} if more else {},
        compiler_params=_cp("arbitrary"))(where, g, r, rr, *more)


def _me():
    return lax.axis_index("x"), lax.axis_index("y"), lax.axis_index("c")


def _chip_peers(x, y):
    return [(1 - x, y), (x, 1 - y), (1 - x, 1 - y)]


def _rcopy(src, dst, send_sems, recv_sems, k, to):
    return pltpu.make_async_remote_copy(src_ref=src, dst_ref=dst, send_sem=send_sems.at[k], recv_sem=recv_sems.at[k],
                                        device_id=to, device_id_type=MESH)


def _row_half(ref, c, lead=()):
    a = ref.shape[len(lead) + 1]
    return ref.at[(*lead, slice(None), pl.ds(c * (a // 2), a // 2))]


class _Comm(NamedTuple):
    ins: list
    out_shapes: list
    n_sems: int
    start: Callable
    finish: Callable


def _sem_scratch(comm):
    return [pltpu.SemaphoreType.DMA((comm.n_sems,)), pltpu.SemaphoreType.DMA((comm.n_sems,))]


def _run_comm(comm, name):
    n_in, n_out = len(comm.ins), len(comm.out_shapes)

    def body(*refs):
        ins, outs, sems = refs[:n_in], refs[n_in:n_in + n_out], refs[n_in + n_out:]
        comm.start(ins, outs, *sems)
        comm.finish(ins, outs, *sems)

    return pl.pallas_call(body, in_specs=[ANY] * n_in, out_specs=[ANY] * n_out, out_shape=comm.out_shapes,
                          scratch_shapes=_sem_scratch(comm), name=name)(*comm.ins)


def _carrier_call(compute, comm, n_steps, in_specs, out_specs, out_shape, scratch_shapes, name, args):
    if comm is None:
        return pl.pallas_call(compute, grid=(n_steps,), in_specs=in_specs, out_specs=out_specs, out_shape=out_shape,
                              scratch_shapes=scratch_shapes, name=name, compiler_params=_cp("arbitrary"))(*args), []
    n_in, n_out, n_scr = len(in_specs), len(out_specs), len(scratch_shapes)
    n_ci, n_co = len(comm.ins), len(comm.out_shapes)

    def body(*refs):
        ins, cins = refs[:n_in], refs[n_in:n_in + n_ci]
        o = n_in + n_ci
        outs, couts = refs[o:o + n_out], refs[o + n_out:o + n_out + n_co]
        s = o + n_out + n_co
        scratch, sems = refs[s:s + n_scr], refs[s + n_scr:]

        @pl.when(pl.program_id(0) == 0)
        def _():
            comm.start(cins, couts, *sems)

        compute(*ins, *outs, *scratch)

        @pl.when(pl.program_id(0) == n_steps - 1)
        def _():
            comm.finish(cins, couts, *sems)

    res = pl.pallas_call(
        body, grid=(n_steps,), in_specs=list(in_specs) + [ANY] * n_ci, out_specs=list(out_specs) + [ANY] * n_co,
        out_shape=list(out_shape) + list(comm.out_shapes), scratch_shapes=list(scratch_shapes) + _sem_scratch(comm),
        name=name, compiler_params=_cp("arbitrary"))(*args, *comm.ins)
    return res[:n_out], res[n_out:]


def _allgather_plan(mine, small=None):
    n = len(mine)
    ins = list(mine) + ([] if small is None else [small])
    out_shapes = [SDS((4,) + m.shape, BF16) for m in mine] + ([] if small is None else [SDS((4,) + small.shape, F32)])
    sem = lambda t, k: 7 * t + k

    def first_copies(ins_r, outs_r, send, recv):
        x, y, c = _me()
        q = 2 * x + y
        cps = []
        for j, chip in enumerate(_chip_peers(x, y)):
            for t in range(n):
                cps.append(_rcopy(_row_half(ins_r[t], c), _row_half(outs_r[t], c, (q,)), send, recv, sem(t, j),
                                  (*chip, c)))
            if small is not None:
                cps.append(_rcopy(ins_r[n], outs_r[n].at[q], send, recv, sem(n, j), (*chip, c)))
        for t in range(len(ins)):
            cps.append(_rcopy(ins_r[t], outs_r[t].at[q], send, recv, sem(t, 6), (x, y, 1 - c)))
        return cps

    def start(ins_r, outs_r, send, recv):
        for cp in first_copies(ins_r, outs_r, send, recv):
            cp.start()

    def finish(ins_r, outs_r, send, recv):
        x, y, c = _me()
        sib = (x, y, 1 - c)
        chips = _chip_peers(x, y)
        passed = []
        for j, (px, py) in enumerate(chips):
            for t in range(n):
                blk = _row_half(outs_r[t], c, (2 * px + py,))
                _rcopy(blk, blk, send, recv, sem(t, j), sib).wait_recv()
                cp = _rcopy(blk, blk, send, recv, sem(t, 3 + j), sib)
                cp.start()
                passed.append(cp)
        for j, (px, py) in enumerate(chips):
            for t in range(n):
                blk = _row_half(outs_r[t], 1 - c, (2 * px + py,))
                _rcopy(blk, blk, send, recv, sem(t, 3 + j), sib).wait_recv()
            if small is not None:
                sblk = outs_r[n].at[2 * px + py]
                _rcopy(sblk, sblk, send, recv, sem(n, j), sib).wait_recv()
        for t in range(len(ins)):
            own = outs_r[t].at[2 * x + y]
            _rcopy(own, own, send, recv, sem(t, 6), sib).wait_recv()
        for cp in first_copies(ins_r, outs_r, send, recv) + passed:
            cp.wait_send()

    return _Comm(ins, out_shapes, 7 * len(ins), start, finish)


def _grads_to_sibling_plan(grads):
    n = len(grads)

    def copies(ins_r, outs_r, send, recv):
        x, y, c = _me()
        return [_rcopy(_row_half(ins_r[t], 1 - c, (slice(None),)), outs_r[t], send, recv, t, (x, y, 1 - c))
                for t in range(n)]

    def start(*a):
        for cp in copies(*a):
            cp.start()

    def finish(*a):
        for cp in copies(*a):
            cp.wait()

    out_shapes = [SDS((4, g.shape[1], g.shape[2] // 2, g.shape[3]), BF16) for g in grads]
    return _Comm(list(grads), out_shapes, n, start, finish)


def _grads_to_chips_plan(psums):
    n = len(psums)

    def copies(ins_r, outs_r, send, recv):
        x, y, c = _me()
        return [_rcopy(ins_r[t].at[2 * px + py], outs_r[t].at[j], send, recv, 3 * t + j, (px, py, c))
                for j, (px, py) in enumerate(_chip_peers(x, y)) for t in range(n)]

    def start(*a):
        for cp in copies(*a):
            cp.start()

    def finish(*a):
        for cp in copies(*a):
            cp.wait()

    return _Comm(list(psums), [SDS((3,) + p.shape[1:], BF16) for p in psums], 3 * n, start, finish)


def _swap_halves(sums):
    n = len(sums)

    def body(*refs):
        out_refs = refs[n:2 * n]
        send_sems, recv_sems = refs[2 * n:]
        x, y, c = _me()
        sib = (x, y, 1 - c)
        cps = [_rcopy(_row_half(out_refs[t], c), _row_half(out_refs[t], c), send_sems, recv_sems, t, sib)
               for t in range(n)]
        for cp in cps:
            cp.start()
        for t in range(n):
            other = _row_half(out_refs[t], 1 - c)
            _rcopy(other, other, send_sems, recv_sems, t, sib).wait_recv()
        for cp in cps:
            cp.wait_send()

    return pl.pallas_call(
        body, in_specs=[ANY] * n, out_specs=[ANY] * n, out_shape=[SDS(s.shape, F32) for s in sums],
        input_output_aliases={t: t for t in range(n)},
        scratch_shapes=[pltpu.SemaphoreType.DMA((n,)), pltpu.SemaphoreType.DMA((n,))],
        name="swap_halves")(*sums)


def _allgather_small(part):
    def body(p_ref, out_ref, send_sems, recv_sems, local_sem):
        x, y, c = _me()
        me = 4 * x + 2 * y + c
        own = pltpu.make_async_copy(p_ref, out_ref.at[me], local_sem.at[0])
        own.start()
        sends = []
        for k in range(1, 8):
            fx, fy, fc = (k >> 2) & 1, (k >> 1) & 1, k & 1
            to = (x ^ fx, y ^ fy, c ^ fc)
            sends.append(_rcopy(p_ref, out_ref.at[me], send_sems, recv_sems, k - 1, to))
        for cp in sends:
            cp.start()
        for k in range(1, 8):
            slot = out_ref.at[me ^ k]
            _rcopy(slot, slot, send_sems, recv_sems, k - 1, (x, y, c)).wait_recv()
        for cp in sends:
            cp.wait_send()
        own.wait()

    return pl.pallas_call(
        body, in_specs=[ANY], out_specs=ANY, out_shape=SDS((8,) + part.shape, F32),
        scratch_shapes=[pltpu.SemaphoreType.DMA((7,)), pltpu.SemaphoreType.DMA((7,)), pltpu.SemaphoreType.DMA((1,))],
        name="allgather_small")(part)


BIG = (("ssd_w_in", 2), ("ssd_w_out", 1), ("sc_w_in", 2), ("sc_w_out", 1), ("ffn_w_up", 2), ("ffn_w_down", 1))


def _to_shards(full, axis):
    A, B = full.shape
    if axis == 2:
        return full.reshape(A, 4, B // 4).transpose(1, 0, 2)
    return full.reshape(4, A // 4, B)


def _from_shards(shards, axis):
    _, a, b = shards.shape
    if axis == 2:
        return shards.transpose(1, 0, 2).reshape(a, 4 * b)
    return shards.reshape(4 * a, b)


def _interleave(w, parts):
    lead, n = w.shape[:-1], w.shape[-1]
    return w.reshape(*lead, parts, n // (parts * LANE), LANE).swapaxes(-2, -3).reshape(*lead, n)


def _deinterleave(w, parts):
    lead, n = w.shape[:-1], w.shape[-1]
    return w.reshape(*lead, n // (parts * LANE), parts, LANE).swapaxes(-2, -3).reshape(*lead, n)


def _pack_rows(vectors, width, row_multiple):
    flat = jnp.concatenate(vectors, axis=-1)
    n = flat.shape[-1]
    unit = width * row_multiple
    total = -(-n // unit) * unit
    flat = jnp.pad(flat, [(0, 0)] * (flat.ndim - 1) + [(0, total - n)])
    return flat.reshape(*flat.shape[:-1], total // width, width)


def _unpack(flat, shapes):
    out, off = [], 0
    for s in shapes:
        n = int(np.prod(s))
        out.append(flat[..., off:off + n].reshape(*flat.shape[:-1], *s))
        off += n
    return out


def _memo(fn):
    cache = {}

    def wrapped(k):
        if k not in cache:
            cache[k] = fn(k)
        return cache[k]

    return wrapped


def _row(v):
    return v.reshape(1, -1)


def _pad_rows(w, rows=8):
    return jnp.pad(w, ((0, rows - w.shape[0]), (0, 0)))


def _ffn_fwd(x, g_pre, g_post, w_up, cw, cb, w_down, tag, comm_up=None, comm_down=None):
    (up, hn, a), got_up = _ffn_up(x, g_pre, w_up, cw, cb, "ffn_up" + tag, comm_up)
    (f, x_new), got_down = _matmul_norm_res(a, w_down, x, g_post, "ffn_down" + tag, comm_down)
    return x_new, (x, hn, up, a, f), got_up, got_down


def _ffn_down_dx(dx, f, g, w_down, up, cw, cb, name):
    L, D = dx.shape
    C = w_down.shape[0]
    tm = min(MM_ROWS, L)
    n = L // tm
    per = tm // (2 * HALO)

    def body(dx_ref, f_ref, g_ref, w_ref, up_ref, prev_ref, cw_ref, cb_ref, dup_ref, df_ref, dg_ref, st_ref,
             da_ref, head_ref):
        i = pl.program_id(0)
        tile = n - 1 - i

        @pl.when(i == 0)
        def _():
            dg_ref[...] = jnp.zeros_like(dg_ref)
            st_ref[...] = jnp.zeros_like(st_ref)
            head_ref[...] = jnp.zeros_like(head_ref)

        fv = f_ref[...]
        dxv = dx_ref[...]
        r = _rsq(fv)
        fh = fv * r
        dg_ref[...] += jnp.sum(dxv * fh, axis=0, keepdims=True)
        dyg = dxv * g_ref[...]
        df = (r * (dyg - fh * jnp.mean(dyg * fh, axis=-1, keepdims=True))).astype(BF16)
        df_ref[...] = df
        da_ref[...] = lax.dot_general(df, w_ref[...], NT, preferred_element_type=F32)
        rsum = lambda v: jnp.sum(v, axis=0, keepdims=True)
        for s in range(C // LANE):
            sl = slice(s * LANE, (s + 1) * LANE)
            w0, w1, w2 = cw_ref[0:1, sl], cw_ref[1:2, sl], cw_ref[2:3, sl]
            gp = up_ref[0, :, sl].astype(F32)
            prev = jnp.where(tile > 0, prev_ref[0, :, sl].astype(F32)[HALO:], 0.0)
            ext = jnp.concatenate([prev, gp], axis=0)
            g1, g2 = pltpu.roll(ext, 1, axis=0)[HALO:HALO + tm], pltpu.roll(ext, 2, axis=0)[HALO:HALO + tm]
            gate = gp * w2 + g1 * w1 + g2 * w0 + cb_ref[:, sl]
            sg = _sigmoid(gate)
            da = da_ref[:, sl]
            dgate = da * up_ref[1, :, sl].astype(F32) * _dsilu(gate, sg)
            dext = jnp.concatenate([dgate, head_ref[:, sl]], axis=0)
            ahead = lambda k: pltpu.roll(dext, tm + HALO - k, axis=0)[:tm]
            dup_ref[0, :, sl] = (dgate * w2 + ahead(1) * w1 + ahead(2) * w0).astype(BF16)
            dup_ref[1, :, sl] = (da * gate * sg).astype(BF16)
            head_ref[:, sl] = dgate[:HALO]
            st_ref[0:1, sl] += rsum(dgate * g2)
            st_ref[1:2, sl] += rsum(dgate * g1)
            st_ref[2:3, sl] += rsum(dgate * gp)
            st_ref[3:4, sl] += rsum(dgate)

    rev = lambda i: (n - 1 - i, 0)
    fix = lambda i: (0, 0)
    pair = pl.BlockSpec((2, tm, C), lambda i: (0, n - 1 - i, 0))
    halo = pl.BlockSpec((2, 2 * HALO, C), lambda i: (0, jnp.maximum((n - 1 - i) * per - 1, 0), 0))
    return pl.pallas_call(
        body, grid=(n,),
        in_specs=[pl.BlockSpec((tm, D), rev), pl.BlockSpec((tm, D), rev), pl.BlockSpec((1, D), fix),
                  pl.BlockSpec((C, D), fix), pair, halo, pl.BlockSpec(cw.shape, fix), pl.BlockSpec(cb.shape, fix)],
        out_specs=[pair, pl.BlockSpec((tm, D), rev), pl.BlockSpec((1, D), fix), pl.BlockSpec((8, C), fix)],
        out_shape=[SDS((2, L, C), BF16), SDS((L, D), BF16), SDS((1, D), F32), SDS((8, C), F32)],
        scratch_shapes=[pltpu.VMEM((tm, C), F32), pltpu.VMEM((HALO, C), F32)],
        name=name, compiler_params=_cp("arbitrary"))(dx, f, g, w_down, up, up, cw, cb)


def _ffn_bwd(dx, saved, g_pre, g_post, w_up, cw, cb, w_down, tag):
    x, hn, up, a, f = saved
    dup, df, dg_post, stats = _ffn_down_dx(dx, f, g_post, w_down, up, cw, cb, "ffn_down_dx" + tag)
    dw_down = _matmul_tn(a, df, "ffn_down_dw" + tag)
    dx_in, dg_pre = _ffn_up_dx(dup, w_up, x, g_pre, dx, "ffn_up_dx" + tag)
    dw_up = _ffn_up_dw(hn, dup, "ffn_up_dw" + tag)
    return dx_in, dict(g_pre=dg_pre, g_post=dg_post, w_up=dw_up, w_down=dw_down, cw=stats[0:3], cb=stats[3])


def _sc_fwd(x, g_pre, g_post, w_in, cw, w_out, tag, comm_in=None, comm_out=None):
    (bcv, hn, q), got_in = _norm_matmul(x, g_pre, w_in, BF16, "sc_in" + tag, comm_in, _sc_gate_epilogue(cw))
    (m, x_new), got_out = _matmul_norm_res(q, w_out, x, g_post, "sc_out" + tag, comm_out)
    return x_new, (x, hn, bcv, q, m), got_in, got_out


def _sc_bwd(dx, saved, g_pre, g_post, w_in, cw, w_out, tag):
    x, hn, bcv, q, m = saved
    dq, dm, dg_post = _postnorm_bwd_matmul_nt(dx, m, g_post, w_out, BF16, "sc_out_dx" + tag)
    dw_out = _matmul_tn(q, dm, "sc_out_dw" + tag)
    dbcv, stats = _sc_mid_bwd(dq, bcv, cw, "sc_mid_bwd" + tag)
    (dx_in, dg_pre), _ = _matmul_nt_prenorm_bwd(dbcv, w_in, x, g_pre, dx, "sc_in_dx" + tag)
    dw_in = _matmul_tn(hn, dbcv, "sc_in_dw" + tag)
    return dx_in, dict(g_pre=dg_pre, g_post=dg_post, w_in=dw_in, w_out=dw_out, cw=stats[0:3])


def _ssd_fwd(x, g_pre, g_post, w_in, cw, cb, par, dexp, nw, w_out, consts, comm, tag, comm_in=None):
    d_inner = dexp.shape[1]
    (zx, hn, xbc), got_in = _norm_matmul(x, g_pre, w_in, F32, "ssd_in" + tag, comm_in,
                                         _ssd_conv_epilogue(cw, cb, d_inner))
    (yn, yf, states), got = _ssd_scan_fwd(zx, xbc, par, dexp, nw, consts, comm, "ssd_scan_fwd" + tag)
    if callable(w_out):
        w_out = w_out(got_in)
    (m, x_new), _ = _matmul_norm_res(yn, w_out, x, g_post, "ssd_out" + tag)
    return x_new, (x, hn, zx, xbc, yn, yf, states, m), got


def _ssd_bwd(dx, saved, g_pre, g_post, w_in, cw, cb, par, dexp, nw, w_out, consts, comm, tag, comm_dx=None):
    x, hn, zx, xbc, yn, yf, states, m = saved
    d_inner = dexp.shape[1]
    dyn, dm, dg_post = _postnorm_bwd_matmul_nt(dx, m, g_post, w_out, F32, "ssd_out_dx" + tag)
    dw_out = _matmul_tn(yn, dm, "ssd_out_dw" + tag)
    (dzx, dxbc, dnw, dpar), got = _ssd_scan_bwd(dyn, yf, zx, xbc, states, par, dexp, nw, consts, comm(dw_out),
                                                "ssd_scan_bwd" + tag)
    dzx, stats = _ssd_conv_bwd(dxbc, zx, cw, cb, d_inner, dzx, "ssd_conv_bwd" + tag)
    dw_in = _matmul_tn(hn, dzx, "ssd_in_dw" + tag)
    (dx_in, dg_pre), got_dx = _matmul_nt_prenorm_bwd(dzx, w_in, x, g_pre, dx, "ssd_in_dx" + tag,
                                                     None if comm_dx is None else comm_dx(dw_in))
    n_heads = d_inner // HEAD_DIM
    grads = dict(g_pre=dg_pre, g_post=dg_post, w_in=dw_in, w_out=dw_out, cw=stats[0:4], cb=stats[4],
                 dt_bias=dpar[0, :n_heads], a_log=dpar[1, :n_heads], d=dpar[2, :n_heads], nw=dnw[0])
    return dx_in, grads, got, got_dx


def kernel(x, mix_pre_g, mix_post_g, ffn_pre_g, ffn_post_g, ssd_w_in, ssd_conv_w, ssd_conv_b, ssd_dt_bias, ssd_A_log, ssd_D, ssd_norm_w, ssd_w_out, sc_w_in, sc_conv_w, sc_w_out, ffn_w_up, ffn_conv_w, ffn_conv_b, ffn_w_down, loss_target, m_mix_pre_g, m_mix_post_g, m_ffn_pre_g, m_ffn_post_g, m_ssd_w_in, m_ssd_conv_w, m_ssd_conv_b, m_ssd_dt_bias, m_ssd_A_log, m_ssd_D, m_ssd_norm_w, m_ssd_w_out, m_sc_w_in, m_sc_conv_w, m_sc_w_out, m_ffn_w_up, m_ffn_conv_w, m_ffn_conv_b, m_ffn_w_down, v_mix_pre_g, v_mix_post_g, v_ffn_pre_g, v_ffn_post_g, v_ssd_w_in, v_ssd_conv_w, v_ssd_conv_b, v_ssd_dt_bias, v_ssd_A_log, v_ssd_D, v_ssd_norm_w, v_ssd_w_out, v_sc_w_in, v_sc_conv_w, v_sc_w_out, v_ffn_w_up, v_ffn_conv_w, v_ffn_conv_b, v_ffn_w_down):
    names = ["mix_pre_g", "mix_post_g", "ffn_pre_g", "ffn_post_g", "ssd_w_in", "ssd_conv_w", "ssd_conv_b",
             "ssd_dt_bias", "ssd_A_log", "ssd_D", "ssd_norm_w", "ssd_w_out", "sc_w_in", "sc_conv_w", "sc_w_out",
             "ffn_w_up", "ffn_conv_w", "ffn_conv_b", "ffn_w_down"]
    env = locals()
    wts = {n: env[n] for n in names}
    mom = {n: env["m_" + n] for n in names}
    var = {n: env["v_" + n] for n in names}

    depth, d_model = mix_pre_g.shape
    n_ssd, n_heads = ssd_dt_bias.shape
    n_sc = sc_conv_w.shape[0]
    d_inner = n_heads * HEAD_DIM
    conv_dim = d_inner + 2 * N_GROUPS * D_STATE
    ssd_in_dim = d_inner + conv_dim + n_heads
    ssd_in_pad = d_inner + conv_dim + LANE
    q_chip = 2 * lax.axis_index("x") + lax.axis_index("y")
    core = lax.axis_index("c")

    assert depth == 4 and n_ssd == 2 and n_sc == 2, "the exchange schedule is written for this trunk"
    ssd_items = lambda j: [("ssd_w_in", j), ("ssd_w_out", j)]
    sc_items = lambda j: [("sc_w_in", j), ("sc_w_out", j)]
    ffn_items = lambda i: [("ffn_w_up", i), ("ffn_w_down", i)]
    gather_first = [("ssd_w_in", 0)]
    gather_in_ssd_in = {0: [("ssd_w_out", 0)]}
    gather_in_scan = {0: ffn_items(0) + sc_items(0), 2: ffn_items(2) + sc_items(1)}
    gather_in_ffn = {0: ([("ffn_w_up", 1)], [("ffn_w_down", 1)]), 1: ([("ssd_w_in", 1)], []),
                     2: ([("ffn_w_up", 3)], [("ffn_w_down", 3)])}
    gather_in_sc = {1: ([("ssd_w_out", 1)], [])}
    reduce_in_scan = {2: ffn_items(3) + sc_items(1) + ffn_items(2),
                      0: ssd_items(1) + ffn_items(1) + sc_items(0) + ffn_items(0) + [("ssd_w_out", 0)]}
    reduce_in_dx = {0: [("ssd_w_in", 0)]}
    axis_of = dict(BIG)
    gathered = {}

    def gather_plan(items, small=None):
        mine = [wts[n][layer:layer + 1].astype(BF16) for n, layer in items]
        return _allgather_plan(mine, small) if items else None

    def gather_done(items, results):
        for item, buf in zip(items, results):
            gathered[item] = buf[:, 0]

    def full(n, layer):
        return _from_shards(gathered[(n, layer)], axis_of[n])

    conv_names = ["ssd_conv_w", "sc_conv_w", "ffn_conv_w"]
    conv_shapes = [wts[n].shape for n in conv_names]
    small_mine = _pack_rows([wts[n].reshape(-1) for n in conv_names], LANE, 8)
    *results, small_all = _run_comm(gather_plan(gather_first, small_mine), "allgather_first")
    gather_done(gather_first, results)
    conv_full = {}
    for n, f, s in zip(conv_names, _unpack(small_all.reshape(4, -1), conv_shapes), conv_shapes):
        conv_full[n] = f.transpose(1, 2, 0, 3).reshape(s[0], s[1], 4 * s[2])

    consts = _scan_constants(n_heads)

    def ssd_args(j):
        par = jnp.zeros((8, LANE), F32).at[0, :n_heads].set(ssd_dt_bias[j]).at[1, :n_heads].set(ssd_A_log[j])
        dexp = jnp.repeat(ssd_D[j], HEAD_DIM).reshape(1, d_inner)
        w_in = jnp.pad(full("ssd_w_in", j), ((0, 0), (0, ssd_in_pad - ssd_in_dim)))
        return (w_in, _pad_rows(conv_full["ssd_conv_w"][j]), _row(ssd_conv_b[j]), par, dexp, _row(ssd_norm_w[j]))

    def sc_args(j):
        return (_interleave(full("sc_w_in", j), 3), _pad_rows(conv_full["sc_conv_w"][j]), full("sc_w_out", j))

    def ffn_args(i):
        return (gathered[("ffn_w_up", i)], _pad_rows(conv_full["ffn_conv_w"][i]), _row(ffn_conv_b[i]),
                full("ffn_w_down", i))

    ssd_args, sc_args, ffn_args = _memo(ssd_args), _memo(sc_args), _memo(ffn_args)

    h = x[0]
    saved = []
    for i in range(depth):
        j = i // 2
        gp, gq = _row(mix_pre_g[i]), _row(mix_post_g[i])
        if i % 2 == 0:
            items_in = gather_in_ssd_in.get(i, [])

            def w_out_when_here(got_in, items_in=items_in, j=j):
                gather_done(items_in, got_in)
                return full("ssd_w_out", j)

            h, sv, results = _ssd_fwd(h, gp, gq, *ssd_args(j), w_out_when_here, consts, gather_plan(gather_in_scan[i]),
                                      tag="", comm_in=gather_plan(items_in))
            gather_done(gather_in_scan[i], results)
        else:
            items_in, items_out = gather_in_sc.get(i, ([], []))
            h, sv, got_in, got_out = _sc_fwd(h, gp, gq, *sc_args(j), tag="", comm_in=gather_plan(items_in),
                                             comm_out=gather_plan(items_out))
            gather_done(items_in, got_in)
            gather_done(items_out, got_out)
        items_up, items_down = gather_in_ffn.get(i, ([], []))
        h, sv2, got_up, got_down = _ffn_fwd(h, _row(ffn_pre_g[i]), _row(ffn_post_g[i]), *ffn_args(i), tag="",
                                            comm_up=gather_plan(items_up), comm_down=gather_plan(items_down))
        gather_done(items_up, got_up)
        gather_done(items_down, got_down)
        saved.append((sv, sv2))
    dh, loss_part = _loss_head(h, loss_target[0], "loss_head")

    mix_grads, ffn_grads = [None] * depth, [None] * depth
    where = jnp.stack([q_chip, core]).astype(jnp.int32)

    def shard_grad(n, layer):
        if n == "ffn_w_up":
            g = ffn_grads[layer]["w_up"]
        elif n == "ffn_w_down":
            g = _to_shards(ffn_grads[layer]["w_down"], 1)
        elif n == "ssd_w_in":
            g = _to_shards(early[(n, layer)][:, :ssd_in_dim], 2)
        elif n == "ssd_w_out":
            g = _to_shards(early[(n, layer)], 1)
        elif n == "sc_w_in":
            g = _to_shards(_deinterleave(mix_grads[2 * layer + 1]["w_in"], 3), 2)
        else:
            g = _to_shards(mix_grads[2 * layer + 1]["w_out"], 1)
        return g[:, None]

    def reduce_begin(items, tag):
        by_shard = [shard_grad(n, layer) for n, layer in items]
        from_sib = _run_comm(_grads_to_sibling_plan(by_shard), "grads_to_sibling" + tag)
        chip_sums = [_add_core_halves(where, g, r, "add_core_halves_%s%d" % item)
                     for item, g, r in zip(items, by_shard, from_sib)]
        return by_shard, from_sib, _grads_to_chips_plan(chip_sums)

    sums = {}
    early = {}

    def riding(items, tag):
        by_shard, from_sib, plan = reduce_begin(items, tag)

        def arrived(from_chips):
            for (n, layer), g, r, rr in zip(items, by_shard, from_sib, from_chips):
                sums[n] = _sum_shard(where, g, r, rr, sums.get(n), layer, wts[n].shape[0],
                                     "sum_shard_%s%d" % (n, layer))

        return plan, arrived

    for i in reversed(range(depth)):
        j = i // 2
        sv, sv2 = saved[i]
        dh, ffn_grads[i] = _ffn_bwd(dh, sv2, _row(ffn_pre_g[i]), _row(ffn_post_g[i]), *ffn_args(i), tag="")
        gp, gq = _row(mix_pre_g[i]), _row(mix_post_g[i])
        if i % 2 == 0:
            then = {}

            def in_scan(dw_out, i=i, j=j, then=then):
                early[("ssd_w_out", j)] = dw_out
                plan, then["scan"] = riding(reduce_in_scan[i], "_%d" % i)
                return plan

            def in_dx(dw_in, i=i, j=j, then=then):
                early[("ssd_w_in", j)] = dw_in
                plan, then["dx"] = riding(reduce_in_dx[i], "_dx%d" % i) if i in reduce_in_dx else (None, None)
                return plan

            dh, mix_grads[i], got, got_dx = _ssd_bwd(dh, sv, gp, gq, *ssd_args(j), full("ssd_w_out", j), consts,
                                                     in_scan, tag="", comm_dx=in_dx)
            then["scan"](got)
            if then["dx"] is not None:
                then["dx"](got_dx)
        else:
            w_in, scw, w_out = sc_args(j)
            dh, mix_grads[i] = _sc_bwd(dh, sv, gp, gq, w_in, scw, w_out, tag="")
    grad_x = dh[None]
    big_grads = dict(zip([n for n, _ in BIG], _swap_halves([sums[n] for n, _ in BIG])))
    ssd_l = [mix_grads[i] for i in range(0, depth, 2)]
    sc_l = [mix_grads[i] for i in range(1, depth, 2)]
    stack = lambda layers, k: jnp.stack([g[k] for g in layers])

    small_names = ["mix_pre_g", "mix_post_g", "ffn_pre_g", "ffn_post_g", "ssd_conv_w", "ssd_conv_b", "ssd_dt_bias",
                   "ssd_A_log", "ssd_D", "ssd_norm_w", "sc_conv_w", "ffn_conv_w", "ffn_conv_b"]
    small_local = {
        "mix_pre_g": jnp.concatenate([g["g_pre"] for g in mix_grads]),
        "mix_post_g": jnp.concatenate([g["g_post"] for g in mix_grads]),
        "ffn_pre_g": jnp.concatenate([g["g_pre"] for g in ffn_grads]),
        "ffn_post_g": jnp.concatenate([g["g_post"] for g in ffn_grads]),
        "ssd_conv_w": stack(ssd_l, "cw"), "ssd_conv_b": stack(ssd_l, "cb"), "ssd_dt_bias": stack(ssd_l, "dt_bias"),
        "ssd_A_log": stack(ssd_l, "a_log"), "ssd_D": stack(ssd_l, "d"), "ssd_norm_w": stack(ssd_l, "nw"),
        "sc_conv_w": stack(sc_l, "cw"), "ffn_conv_w": stack(ffn_grads, "cw"), "ffn_conv_b": stack(ffn_grads, "cb"),
    }
    small_full_shapes = [small_local[n].shape for n in small_names]
    spack = _pack_rows([small_local[n].reshape(-1) for n in small_names] + [loss_part.reshape(-1)], LANE, 8)
    stotal = _sum_slots(_allgather_small(spack), "sum_small").reshape(-1)
    small_grads = dict(zip(small_names, _unpack(stotal, small_full_shapes)))
    loss = stotal[sum(int(np.prod(s)) for s in small_full_shapes)]
    for n in conv_names:
        width = wts[n].shape[-1]
        small_grads[n] = lax.dynamic_slice_in_dim(small_grads[n], q_chip * width, width, axis=2)

    grads, delta, new_m, new_v = {}, {}, {}, {}
    for n, _ in BIG:
        s = wts[n].shape
        two_d = lambda a: a.reshape(-1, s[-1])
        grads[n] = big_grads[n]
        d, mn, vn = _adamw(two_d(wts[n]), two_d(mom[n]), two_d(var[n]), two_d(grads[n]), "adamw_" + n)
        delta[n], new_m[n], new_v[n] = d.reshape(s), mn.reshape(s), vn.reshape(s)
    small_shapes = [wts[n].shape for n in small_names]
    pk = lambda d: _pack_rows([d[n].reshape(-1) for n in small_names], LANE, 8)
    for n in small_names:
        grads[n] = small_grads[n].reshape(wts[n].shape)
    d, mn, vn = _adamw(pk(wts), pk(mom), pk(var), pk(grads), "adamw_small")
    for out, packed in ((delta, d), (new_m, mn), (new_v, vn)):
        out.update(zip(small_names, _unpack(packed.reshape(-1), small_shapes)))

    return (loss, grad_x, *[grads[n] for n in names], *[delta[n] for n in names], *[new_m[n] for n in names],
            *[new_v[n] for n in names])
```

```python
from typing import Callable, NamedTuple

import jax
import jax.numpy as jnp
import numpy as np
from jax import lax
from jax.experimental import pallas as pl
from jax.experimental.pallas import tpu as pltpu

F32 = jnp.float32
BF16 = jnp.bfloat16
SDS = jax.ShapeDtypeStruct
MESH = pl.DeviceIdType.MESH
ANY = pl.BlockSpec(memory_space=pl.ANY)

EPS = 1e-6
CHUNK = 64
HEAD_DIM = 64
N_GROUPS = 8
D_STATE = 128
HEADS_PER_GROUP = 4
GROUP_W = HEADS_PER_GROUP * HEAD_DIM
LANE = 128
ROW_TILE = 128
HALO = 8
VMEM_LIMIT = 56 * 1024 * 1024

ADAM_LR = 0.001
ADAM_B1 = 0.9
ADAM_B2 = 0.999
ADAM_EPS = 1e-08
ADAM_WD = 0.01
ADAM_STEP = 10

NT = (((1,), (1,)), ((), ()))
TN = (((0,), (0,)), ((), ()))


def _cp(*sem):
    return pltpu.CompilerParams(dimension_semantics=sem or None, vmem_limit_bytes=VMEM_LIMIT)


def _sigmoid(x):
    return 1.0 / (1.0 + jnp.exp(-x))


def _sigmoid_fast(x):
    return pl.reciprocal(1.0 + jnp.exp(-x), approx=True)


def _dsilu(x, s):
    return s * (1.0 + x * (1.0 - s))


def _rsq(x):
    return lax.rsqrt(jnp.mean(x * x, axis=-1, keepdims=True) + EPS)


MM_ROWS = 256


class _Epilogue(NamedTuple):
    ins: list
    cols: int
    dtype: object
    fn: Callable


def _norm_matmul(x, g, w, out_dtype, name, comm=None, epilogue=None):
    L, D = x.shape
    N = w.shape[1]
    tm = min(MM_ROWS, L)
    n_extra = 0 if epilogue is None else len(epilogue.ins)

    def body(x_ref, g_ref, w_ref, *refs):
        extra, (o_ref, hn_ref), rest = refs[:n_extra], refs[n_extra:n_extra + 2], refs[n_extra + 2:]
        xv = x_ref[...]
        hn = (xv * _rsq(xv) * g_ref[...]).astype(BF16)
        hn_ref[...] = hn
        o_ref[...] = jnp.dot(hn, w_ref[...], preferred_element_type=F32).astype(out_dtype)
        if epilogue is not None:
            res_ref, tail_ref = rest

            @pl.when(pl.program_id(0) == 0)
            def _():
                tail_ref[...] = jnp.zeros_like(tail_ref)

            epilogue.fn(o_ref, extra, res_ref, tail_ref, tm)

    row = lambda i: (i, 0)
    fix = lambda i: (0, 0)
    in_specs = [pl.BlockSpec((tm, D), row), pl.BlockSpec((1, D), fix), pl.BlockSpec((D, N), fix)]
    out_specs = [pl.BlockSpec((tm, N), row), pl.BlockSpec((tm, D), row)]
    out_shape = [SDS((L, N), out_dtype), SDS((L, D), BF16)]
    scratch, args = [], (x, g, w)
    if epilogue is not None:
        in_specs += [pl.BlockSpec(a.shape, fix) for a in epilogue.ins]
        out_specs.append(pl.BlockSpec((tm, epilogue.cols), row))
        out_shape.append(SDS((L, epilogue.cols), epilogue.dtype))
        scratch = [pltpu.VMEM((HALO, epilogue.cols), F32)]
        args += tuple(epilogue.ins)
    return _carrier_call(body, comm, L // tm, in_specs=in_specs, out_specs=out_specs, out_shape=out_shape,
                         scratch_shapes=scratch, name=name, args=args)


def _conv_taps(x, tail, taps, tm):
    ext = jnp.concatenate([tail, x], axis=0)
    out = x * taps[0]
    for k in range(1, len(taps)):
        out = out + pltpu.roll(ext, k, axis=0)[HALO:HALO + tm] * taps[k]
    return out


def _ssd_conv_epilogue(cw, cb, col0):
    C = cw.shape[1]

    def fn(o_ref, ins, res_ref, tail_ref, tm):
        cw_ref, cb_ref = ins
        for s in range(C // LANE):
            sl = slice(s * LANE, (s + 1) * LANE)
            xv = o_ref[:, col0 + s * LANE:col0 + (s + 1) * LANE].astype(F32)
            cv = _conv_taps(xv, tail_ref[:, sl], [cw_ref[3 - k:4 - k, sl] for k in range(4)], tm) + cb_ref[:, sl]
            res_ref[:, sl] = cv * _sigmoid(cv)
            tail_ref[:, sl] = xv[tm - HALO:tm]

    return _Epilogue([cw, cb], C, F32, fn)


def _sc_gate_epilogue(cw):
    C = cw.shape[1]

    def fn(o_ref, ins, res_ref, tail_ref, tm):
        (cw_ref,) = ins
        for s in range(C // LANE):
            sl = slice(s * LANE, (s + 1) * LANE)
            gb, gc, v = (o_ref[:, (3 * s + k) * LANE:(3 * s + k + 1) * LANE].astype(F32) for k in range(3))
            p = gc * v
            u = _conv_taps(p, tail_ref[:, sl], [cw_ref[2 - k:3 - k, sl] for k in range(3)], tm)
            res_ref[:, sl] = (gb * u).astype(BF16)
            tail_ref[:, sl] = p[tm - HALO:tm]

    return _Epilogue([cw], C, BF16, fn)


def _matmul_norm_res(a, w, x, g, name, comm=None):
    L, K = a.shape
    D = w.shape[1]
    tm = min(MM_ROWS, L)

    def body(a_ref, w_ref, x_ref, g_ref, m_ref, xo_ref):
        m = jnp.dot(a_ref[...], w_ref[...], preferred_element_type=F32)
        m_ref[...] = m
        xo_ref[...] = x_ref[...] + m * _rsq(m) * g_ref[...]

    row = lambda i: (i, 0)
    fix = lambda i: (0, 0)
    return _carrier_call(
        body, comm, L // tm,
        in_specs=[pl.BlockSpec((tm, K), row), pl.BlockSpec((K, D), fix), pl.BlockSpec((tm, D), row),
                  pl.BlockSpec((1, D), fix)],
        out_specs=[pl.BlockSpec((tm, D), row), pl.BlockSpec((tm, D), row)],
        out_shape=[SDS((L, D), F32), SDS((L, D), F32)], scratch_shapes=[], name=name, args=(a, w, x, g))


def _postnorm_bwd_matmul_nt(dx, m, g, w, out_dtype, name):
    L, D = dx.shape
    K = w.shape[0]
    tm = min(MM_ROWS, L)

    def body(dx_ref, m_ref, g_ref, w_ref, da_ref, dm_ref, dg_ref):
        @pl.when(pl.program_id(0) == 0)
        def _():
            dg_ref[...] = jnp.zeros_like(dg_ref)

        m = m_ref[...]
        dxv = dx_ref[...]
        r = _rsq(m)
        mh = m * r
        dg_ref[...] += jnp.sum(dxv * mh, axis=0, keepdims=True)
        dyg = dxv * g_ref[...]
        dm = (r * (dyg - mh * jnp.mean(dyg * mh, axis=-1, keepdims=True))).astype(BF16)
        dm_ref[...] = dm
        da_ref[...] = lax.dot_general(dm, w_ref[...], NT, preferred_element_type=F32).astype(out_dtype)

    row = lambda i: (i, 0)
    fix = lambda i: (0, 0)
    return pl.pallas_call(
        body, grid=(L // tm,),
        in_specs=[pl.BlockSpec((tm, D), row), pl.BlockSpec((tm, D), row), pl.BlockSpec((1, D), fix),
                  pl.BlockSpec((K, D), fix)],
        out_specs=[pl.BlockSpec((tm, K), row), pl.BlockSpec((tm, D), row), pl.BlockSpec((1, D), fix)],
        out_shape=[SDS((L, K), out_dtype), SDS((L, D), BF16), SDS((1, D), F32)],
        name=name, compiler_params=_cp("arbitrary"))(dx, m, g, w)


DW_ACC_BYTES = 13 * 512 * 1024


def _dw_tiles(ka, n):
    ta = ka if ka <= 1024 else ka // 2
    fits = [d for d in range(LANE, n + 1, LANE) if n % d == 0 and ta * d * 4 <= DW_ACC_BYTES]
    return ta, max(fits)


def _matmul_tn(a, b, name):
    L, Ka = a.shape
    N = b.shape[1]
    ta, tn = _dw_tiles(Ka, N)
    tl = min(512, L)
    n_l = L // tl

    def body(a_ref, b_ref, o_ref, acc_ref):
        l = pl.program_id(2)

        @pl.when(l == 0)
        def _():
            acc_ref[...] = jnp.zeros_like(acc_ref)

        acc_ref[...] += lax.dot_general(a_ref[...], b_ref[...], TN, preferred_element_type=F32)

        @pl.when(l == n_l - 1)
        def _():
            o_ref[...] = acc_ref[...].astype(BF16)

    return pl.pallas_call(
        body, grid=(Ka // ta, N // tn, n_l),
        in_specs=[pl.BlockSpec((tl, ta), lambda i, j, l: (l, i)), pl.BlockSpec((tl, tn), lambda i, j, l: (l, j))],
        out_specs=pl.BlockSpec((ta, tn), lambda i, j, l: (i, j)),
        out_shape=SDS((Ka, N), BF16),
        scratch_shapes=[pltpu.VMEM((ta, tn), F32)],
        name=name, compiler_params=_cp("parallel", "parallel", "arbitrary"))(a, b)


def _ffn_up(x, g, w4, cw, cb, name, comm=None):
    L, D = x.shape
    b = w4.shape[2]
    C = 2 * b
    tm = min(MM_ROWS, L)

    def body(x_ref, g_ref, w_ref, cw_ref, cb_ref, o_ref, hn_ref, a_ref, tail_ref):
        @pl.when(pl.program_id(0) == 0)
        def _():
            tail_ref[...] = jnp.zeros_like(tail_ref)

        xv = x_ref[...]
        hn = (xv * _rsq(xv) * g_ref[...]).astype(BF16)
        hn_ref[...] = hn
        for half in range(2):
            cols = slice(half * b, (half + 1) * b)
            for part in range(2):
                o_ref[part, :, cols] = jnp.dot(hn, w_ref[2 * part + half], preferred_element_type=F32).astype(BF16)
            for s in range(half * b // LANE, (half + 1) * b // LANE):
                sl = slice(s * LANE, (s + 1) * LANE)
                gp = o_ref[0, :, sl].astype(F32)
                ext = jnp.concatenate([tail_ref[:, sl], gp], axis=0)
                back = lambda k: pltpu.roll(ext, k, axis=0)[HALO:HALO + tm]
                gate = gp * cw_ref[2:3, sl] + back(1) * cw_ref[1:2, sl] + back(2) * cw_ref[0:1, sl] + cb_ref[:, sl]
                a_ref[:, sl] = (gate * _sigmoid(gate) * o_ref[1, :, sl].astype(F32)).astype(BF16)
                tail_ref[:, sl] = gp[tm - HALO:tm]

    row = lambda i: (i, 0)
    fix = lambda i: (0, 0)
    return _carrier_call(
        body, comm, L // tm,
        in_specs=[pl.BlockSpec((tm, D), row), pl.BlockSpec((1, D), fix), pl.BlockSpec((4, D, b), lambda i: (0, 0, 0)),
                  pl.BlockSpec(cw.shape, fix), pl.BlockSpec(cb.shape, fix)],
        out_specs=[pl.BlockSpec((2, tm, C), lambda i: (0, i, 0)), pl.BlockSpec((tm, D), row), pl.BlockSpec((tm, C), row)],
        out_shape=[SDS((2, L, C), BF16), SDS((L, D), BF16), SDS((L, C), BF16)],
        scratch_shapes=[pltpu.VMEM((HALO, C), F32)], name=name, args=(x, g, w4, cw, cb))


def _ffn_up_dx(dup, w4, x, g, dres, name):
    _, L, _ = dup.shape
    _, D, b = w4.shape
    tm = min(MM_ROWS, L)

    def body(dy_ref, w_ref, x_ref, g_ref, dres_ref, dx_ref, dg_ref):
        @pl.when(pl.program_id(0) == 0)
        def _():
            dg_ref[...] = jnp.zeros_like(dg_ref)

        dh = jnp.zeros((tm, D), F32)
        for q in range(4):
            dh = dh + lax.dot_general(dy_ref[q // 2, :, (q % 2) * b:(q % 2 + 1) * b], w_ref[q], NT,
                                      preferred_element_type=F32)
        xv = x_ref[...]
        r = _rsq(xv)
        xh = xv * r
        dg_ref[...] += jnp.sum(dh * xh, axis=0, keepdims=True)
        dyg = dh * g_ref[...]
        dx_ref[...] = dres_ref[...] + r * (dyg - xh * jnp.mean(dyg * xh, axis=-1, keepdims=True))

    row = lambda i: (i, 0)
    fix = lambda i: (0, 0)
    return pl.pallas_call(
        body, grid=(L // tm,),
        in_specs=[pl.BlockSpec((2, tm, 2 * b), lambda i: (0, i, 0)), pl.BlockSpec((4, D, b), lambda i: (0, 0, 0)),
                  pl.BlockSpec((tm, D), row), pl.BlockSpec((1, D), fix), pl.BlockSpec((tm, D), row)],
        out_specs=[pl.BlockSpec((tm, D), row), pl.BlockSpec((1, D), fix)],
        out_shape=[SDS((L, D), F32), SDS((1, D), F32)],
        name=name, compiler_params=_cp("arbitrary"))(dup, w4, x, g, dres)


def _ffn_up_dw(hn, dup, name):
    L, D = hn.shape
    b = dup.shape[2] // 2
    tl = min(512, L)
    n_l = L // tl

    def body(a_ref, b_ref, o_ref, acc_ref):
        l = pl.program_id(1)

        @pl.when(l == 0)
        def _():
            acc_ref[...] = jnp.zeros_like(acc_ref)

        acc_ref[...] += lax.dot_general(a_ref[...], b_ref[...], TN, preferred_element_type=F32)

        @pl.when(l == n_l - 1)
        def _():
            o_ref[...] = acc_ref[...].astype(BF16)

    return pl.pallas_call(
        body, grid=(4, n_l),
        in_specs=[pl.BlockSpec((tl, D), lambda q, l: (l, 0)),
                  pl.BlockSpec((None, tl, b), lambda q, l: (q // 2, l, q % 2))],
        out_specs=pl.BlockSpec((None, D, b), lambda q, l: (q, 0, 0)),
        out_shape=SDS((4, D, b), BF16),
        scratch_shapes=[pltpu.VMEM((D, b), F32)],
        name=name, compiler_params=_cp("parallel", "arbitrary"))(hn, dup)


def _matmul_nt_prenorm_bwd(dy, w, x, g, dres, name, comm=None):
    L, N = dy.shape
    D = w.shape[0]
    tm = min(MM_ROWS, L)

    def body(dy_ref, w_ref, x_ref, g_ref, dres_ref, dx_ref, dg_ref):
        @pl.when(pl.program_id(0) == 0)
        def _():
            dg_ref[...] = jnp.zeros_like(dg_ref)

        dh = lax.dot_general(dy_ref[...], w_ref[...], NT, preferred_element_type=F32)
        xv = x_ref[...]
        r = _rsq(xv)
        xh = xv * r
        dg_ref[...] += jnp.sum(dh * xh, axis=0, keepdims=True)
        dyg = dh * g_ref[...]
        dx_ref[...] = dres_ref[...] + r * (dyg - xh * jnp.mean(dyg * xh, axis=-1, keepdims=True))

    row = lambda i: (i, 0)
    fix = lambda i: (0, 0)
    return _carrier_call(
        body, comm, L // tm,
        in_specs=[pl.BlockSpec((tm, N), row), pl.BlockSpec((D, N), fix), pl.BlockSpec((tm, D), row),
                  pl.BlockSpec((1, D), fix), pl.BlockSpec((tm, D), row)],
        out_specs=[pl.BlockSpec((tm, D), row), pl.BlockSpec((1, D), fix)],
        out_shape=[SDS((L, D), F32), SDS((1, D), F32)], scratch_shapes=[], name=name, args=(dy, w, x, g, dres))


def _loss_head(y, t, name):
    L, D = y.shape
    tm = min(512, L)

    def body(y_ref, t_ref, dy_ref, loss_ref):
        @pl.when(pl.program_id(0) == 0)
        def _():
            loss_ref[...] = jnp.zeros_like(loss_ref)

        e = y_ref[...] - t_ref[...]
        dy_ref[...] = e * (1.0 / D)
        s = jnp.sum(jnp.sum(e * e, axis=1, keepdims=True), axis=0, keepdims=True)
        loss_ref[...] += s * (0.5 / D)

    row = lambda i: (i, 0)
    return pl.pallas_call(
        body, grid=(L // tm,),
        in_specs=[pl.BlockSpec((tm, D), row), pl.BlockSpec((tm, D), row)],
        out_specs=[pl.BlockSpec((tm, D), row), pl.BlockSpec((1, 1), lambda i: (0, 0))],
        out_shape=[SDS((L, D), F32), SDS((1, 1), F32)],
        name=name, compiler_params=_cp("arbitrary"))(y, t)


def _tile_rows(ref):
    return HALO * (4 // jnp.dtype(ref.dtype).itemsize)


def _prev_rows(ref, r0, i, cols):
    n = _tile_rows(ref)
    p0 = pl.multiple_of(jnp.maximum(r0 - n, 0), n)
    return jnp.where(i > 0, ref[pl.ds(p0, n), cols].astype(F32)[n - HALO:], 0.0)


def _next_rows(ref, r0, i, n_tiles, cols):
    n = _tile_rows(ref)
    n0 = pl.multiple_of(jnp.minimum(r0 + ROW_TILE, n_tiles * ROW_TILE - n), n)
    return jnp.where(i < n_tiles - 1, ref[pl.ds(n0, n), cols].astype(F32)[:HALO], 0.0)


def _rows_f32(ref, rows, cols):
    return ref[rows, cols].astype(F32)


def _back(ext, s):
    return pltpu.roll(ext, s, axis=0)[HALO:HALO + ROW_TILE]


def _fwd(ext, s):
    n = ext.shape[0]
    return pltpu.roll(ext, n - s, axis=0)[:ROW_TILE]


def _store_rows(ref, rows):
    ref[...] = jnp.zeros_like(ref)
    for k, v in enumerate(rows):
        ref[k:k + 1, :] = v


def _strip_call(body, L, n_strips, ins, outs, name):
    def spec(rows, width, off):
        if off is None:
            return pl.BlockSpec((rows, width), lambda j: (0, 0))
        return pl.BlockSpec((rows, width), lambda j: (0, j + off))

    return pl.pallas_call(
        body, grid=(n_strips,),
        in_specs=[spec(a.shape[0], w, off) for a, w, off in ins],
        out_specs=[spec(s.shape[0], w, off) for s, w, off in outs],
        out_shape=[s for s, _, _ in outs],
        name=name, compiler_params=_cp("parallel"))(*[a for a, _, _ in ins])


def _sc_mid_bwd(dq, bcv, cw, name):
    L, C = dq.shape
    n_tiles = L // ROW_TILE
    s0, s1, s2, al = slice(0, LANE), slice(LANE, 2 * LANE), slice(2 * LANE, 3 * LANE), slice(None)

    def body(dq_ref, x_ref, cw_ref, dx_ref, st_ref):
        w0, w1, w2 = cw_ref[0:1, :], cw_ref[1:2, :], cw_ref[2:3, :]

        def step(i, c):
            r0 = pl.multiple_of(i * ROW_TILE, ROW_TILE)
            rows = pl.ds(r0, ROW_TILE)
            gb, gc, v = _rows_f32(x_ref, rows, s0), _rows_f32(x_ref, rows, s1), _rows_f32(x_ref, rows, s2)
            dq_v = _rows_f32(dq_ref, rows, al)
            p = gc * v
            pext = jnp.concatenate([_prev_rows(x_ref, r0, i, s1) * _prev_rows(x_ref, r0, i, s2), p], axis=0)
            p1, p2 = _back(pext, 1), _back(pext, 2)
            u = p * w2 + p1 * w1 + p2 * w0
            du = dq_v * gb
            du_n = _next_rows(dq_ref, r0, i, n_tiles, al) * _next_rows(x_ref, r0, i, n_tiles, s0)
            ext = jnp.concatenate([du, du_n], axis=0)
            dp = du * w2 + _fwd(ext, 1) * w1 + _fwd(ext, 2) * w0
            dx_ref[rows, s0] = (dq_v * u).astype(BF16)
            dx_ref[rows, s1] = (dp * v).astype(BF16)
            dx_ref[rows, s2] = (dp * gc).astype(BF16)
            s = lambda t: jnp.sum(t, axis=0, keepdims=True)
            return (c[0] + s(du * p2), c[1] + s(du * p1), c[2] + s(du * p))

        z = jnp.zeros((1, LANE), F32)
        _store_rows(st_ref, lax.fori_loop(0, n_tiles, step, (z, z, z)))

    return _strip_call(body, L, C // LANE, [(dq, LANE, 0), (bcv, 3 * LANE, 0), (cw, LANE, 0)],
                       [(SDS((L, 3 * C), BF16), 3 * LANE, 0), (SDS((8, C), F32), LANE, 0)], name)


def _ssd_conv_bwd(dxbc, zx, cw, cb, col0, dzx, name):
    L, C = dxbc.shape
    n_tiles = L // ROW_TILE
    al = slice(None)

    def body(d_ref, x_ref, cw_ref, cb_ref, dzx_in_ref, o_ref, st_ref):
        w0, w1, w2, w3 = cw_ref[0:1, :], cw_ref[1:2, :], cw_ref[2:3, :], cw_ref[3:4, :]
        b = cb_ref[...]

        def step(i, c):
            r0 = pl.multiple_of(i * ROW_TILE, ROW_TILE)
            rows = pl.ds(r0, ROW_TILE)
            xv = x_ref[rows, :]
            xe = jnp.concatenate([_prev_rows(x_ref, r0, i, al), xv, _next_rows(x_ref, r0, i, n_tiles, al)], axis=0)
            x1, x2, x3 = pltpu.roll(xe, 1, axis=0), pltpu.roll(xe, 2, axis=0), pltpu.roll(xe, 3, axis=0)
            cv = (xe * w3 + x1 * w2 + x2 * w1 + x3 * w0 + b)[HALO:]
            de = jnp.concatenate([d_ref[rows, :], _next_rows(d_ref, r0, i, n_tiles, al)], axis=0)
            dc_ext = de * _dsilu(cv, _sigmoid_fast(cv))
            dc = dc_ext[:ROW_TILE]
            o_ref[rows, :] = (dc * w3 + _fwd(dc_ext, 1) * w2 + _fwd(dc_ext, 2) * w1 + _fwd(dc_ext, 3) * w0).astype(BF16)
            s = lambda t: jnp.sum(t, axis=0, keepdims=True)
            t = slice(HALO, HALO + ROW_TILE)
            return (c[0] + s(dc * x3[t]), c[1] + s(dc * x2[t]), c[2] + s(dc * x1[t]), c[3] + s(dc * xv), c[4] + s(dc))

        z = jnp.zeros((1, LANE), F32)
        _store_rows(st_ref, lax.fori_loop(0, n_tiles, step, (z, z, z, z, z)))

    strip = lambda rows, off=0: pl.BlockSpec((rows, LANE), lambda j: (0, j + off))
    shifted = strip(L, col0 // LANE)
    return pl.pallas_call(
        body, grid=(C // LANE,), in_specs=[strip(L), shifted, strip(cw.shape[0]), strip(1), ANY],
        out_specs=[shifted, strip(8)], out_shape=[SDS(dzx.shape, dzx.dtype), SDS((8, C), F32)],
        input_output_aliases={4: 0}, name=name, compiler_params=_cp("parallel"))(dxbc, zx, cw, cb, dzx)


def _scan_constants(n_heads):
    hw = n_heads * HEAD_DIM
    col = np.arange(hw)
    ind = (col[None, :] // HEAD_DIM == np.arange(LANE)[:, None]).astype(np.float32)
    gcol = np.arange(GROUP_W)
    itile = (gcol[None, :] % CHUNK == np.arange(CHUNK)[:, None]).astype(np.float32)
    trit = (gcol[None, :] % CHUNK <= np.arange(CHUNK)[:, None]).astype(np.float32)
    tril = np.tril(np.ones((CHUNK, CHUNK), np.float32))
    bmask = (gcol[:, None] // HEAD_DIM == gcol[None, :] // HEAD_DIM).astype(np.float32)
    return (jnp.asarray(ind.T.copy(), BF16), jnp.asarray(itile), jnp.asarray(trit), jnp.asarray(tril, BF16),
            jnp.asarray(bmask))


def _softplus(x):
    return jnp.maximum(x, 0.0) + jnp.log(1.0 + jnp.exp(-jnp.abs(x)))


def _split3(x):
    hi = x.astype(BF16)
    r1 = x - hi.astype(F32)
    mid = r1.astype(BF16)
    return hi, mid, (r1 - mid.astype(F32)).astype(BF16)


def _dot_sel(x, sel, dims=None):
    if dims is None:
        mm = lambda p: jnp.dot(p, sel, preferred_element_type=F32)
    else:
        mm = lambda p: lax.dot_general(sel, p, dims, preferred_element_type=F32)
    hi, mid, lo = _split3(x)
    return (mm(lo) + mm(mid)) + mm(hi)


SEL_X = (((1,), (0,)), ((), ()))


def _head_lanes(v, g):
    h0 = HEADS_PER_GROUP * g
    return jnp.concatenate([jnp.broadcast_to(v[:, h0 + r:h0 + r + 1], (v.shape[0], HEAD_DIM))
                            for r in range(HEADS_PER_GROUP)], axis=1)


def _group_terms(g, dt, cs, cst, xbc_ref, trit, bmask, d_inner):
    gl = slice(g * GROUP_W, (g + 1) * GROUP_W)
    h0 = HEADS_PER_GROUP * g
    csl = _head_lanes(cs, g)
    dtx = _head_lanes(dt, g)
    rr = jnp.concatenate([cst[h0 + r:h0 + r + 1, :] for r in range(HEADS_PER_GROUP)], axis=1)
    lm = jnp.exp(jnp.where(trit > 0.0, csl - rr, -jnp.inf))
    xs = xbc_ref[:, gl]
    b = xbc_ref[:, d_inner + g * D_STATE: d_inner + (g + 1) * D_STATE]
    c = xbc_ref[:, d_inner + (N_GROUPS + g) * D_STATE: d_inner + (N_GROUPS + g + 1) * D_STATE]
    u = xs * dtx
    bb, cb = b.astype(BF16), c.astype(BF16)
    btile = jnp.concatenate([bb] * HEADS_PER_GROUP, axis=0)
    cbt = lax.dot_general(cb, btile, NT, preferred_element_type=F32)
    m = cbt * lm
    ub = u.astype(BF16)
    bdu = jnp.where(bmask > 0.0, jnp.concatenate([ub] * HEADS_PER_GROUP, axis=0), jnp.zeros((), BF16))
    c_last = csl[CHUNK - 1:CHUNK, :]
    return dict(gl=gl, csl=csl, dtx=dtx, lm=lm, xs=xs, bb=bb, cb=cb, u=u, btile=btile, m=m, bdu=bdu,
                e=jnp.exp(csl), dec=jnp.exp(c_last - csl), e_last=jnp.exp(c_last))


def _ssd_scan_fwd(zx, xbc, par, dexp, nw, consts, comm, name):
    L = xbc.shape[0]
    d_inner = dexp.shape[1]
    n_chunks = L // CHUNK
    dt_blk = zx.shape[1] // LANE - 1
    ind_t, itile_c, trit_c, tril_c, bmask_c = consts

    def body(xbc_ref, z_ref, dtr_ref, par_ref, dexp_ref, nw_ref, trit_ref, tril_ref, bmask_ref,
             yn_ref, yf_ref, st_out_ref, st_ref):
        @pl.when(pl.program_id(0) == 0)
        def _():
            st_ref[...] = jnp.zeros_like(st_ref)

        dt = _softplus(dtr_ref[...] + par_ref[0:1, :])
        a_head = -jnp.exp(par_ref[1:2, :])
        cs = _dot_sel(dt * a_head, tril_ref[...], SEL_X)
        cst = cs.T
        trit, bmask = trit_ref[...], bmask_ref[...]
        for g in range(N_GROUPS):
            t = _group_terms(g, dt, cs, cst, xbc_ref, trit, bmask, d_inner)
            p = st_ref[g]
            st_out_ref[0, g] = p
            y = jnp.dot(t["m"].astype(BF16), t["bdu"], preferred_element_type=F32)
            y = y + jnp.dot(t["cb"], p.astype(BF16), preferred_element_type=F32) * t["e"]
            st_new = lax.dot_general(t["bb"], (t["u"] * t["dec"]).astype(BF16), TN, preferred_element_type=F32)
            st_ref[g] = p * t["e_last"] + st_new
            yf_ref[:, t["gl"]] = y + t["xs"] * dexp_ref[:, t["gl"]]
        z = z_ref[...]
        y2 = yf_ref[...] * (z * _sigmoid(z))
        yn_ref[...] = (y2 * _rsq(y2) * nw_ref[...]).astype(BF16)

    row = lambda c: (c, 0)
    fix = lambda c: (0, 0)
    cspec = lambda a: pl.BlockSpec(a.shape, fix)
    return _carrier_call(
        body, comm, n_chunks,
        in_specs=[pl.BlockSpec((CHUNK, xbc.shape[1]), row), pl.BlockSpec((CHUNK, d_inner), row),
                  pl.BlockSpec((CHUNK, LANE), lambda c: (c, dt_blk)), cspec(par), cspec(dexp), cspec(nw),
                  cspec(trit_c), cspec(tril_c), cspec(bmask_c)],
        out_specs=[pl.BlockSpec((CHUNK, d_inner), row), pl.BlockSpec((CHUNK, d_inner), row),
                   pl.BlockSpec((1, N_GROUPS, D_STATE, GROUP_W), lambda c: (c, 0, 0, 0))],
        out_shape=[SDS((L, d_inner), BF16), SDS((L, d_inner), F32),
                   SDS((n_chunks, N_GROUPS, D_STATE, GROUP_W), F32)],
        scratch_shapes=[pltpu.VMEM((N_GROUPS, D_STATE, GROUP_W), F32)],
        name=name, args=(xbc, zx, zx, par, dexp, nw, trit_c, tril_c, bmask_c))


def _ssd_scan_bwd(dyn, yf, zx, xbc, states, par, dexp, nw, consts, comm, name):
    L = xbc.shape[0]
    d_inner = dexp.shape[1]
    n_chunks = L // CHUNK
    nz = zx.shape[1]
    dt_blk = nz // LANE - 1
    ind_t, itile_c, trit_c, tril_c, bmask_c = consts
    hslices = [slice(r * HEAD_DIM, (r + 1) * HEAD_DIM) for r in range(HEADS_PER_GROUP)]

    def body(dyn_ref, yf_ref, z_ref, dtr_ref, xbc_ref, st_in_ref, par_ref, dexp_ref, nw_ref, indt_ref,
             itile_ref, trit_ref, tril_ref, bmask_ref,
             dzx_ref, dxbc_ref, dnw_ref, dpar_ref, dq_ref, dyf_ref):
        @pl.when(pl.program_id(0) == 0)
        def _():
            dq_ref[...] = jnp.zeros_like(dq_ref)
            dnw_ref[...] = jnp.zeros_like(dnw_ref)
            dpar_ref[...] = jnp.zeros_like(dpar_ref)

        z, yfv, dynv = z_ref[...], yf_ref[...], dyn_ref[...]
        sz = _sigmoid(z)
        y2 = yfv * (z * sz)
        r = _rsq(y2)
        y2h = y2 * r
        dnw_ref[...] += jnp.sum(dynv * y2h, axis=0, keepdims=True)
        dyg = dynv * nw_ref[...]
        dy2 = r * (dyg - y2h * jnp.mean(dyg * y2h, axis=-1, keepdims=True))
        dzx_ref[:, 0:d_inner] = (dy2 * yfv * _dsilu(z, sz)).astype(BF16)
        dyf_ref[...] = dy2 * (z * sz)

        pre = dtr_ref[...] + par_ref[0:1, :]
        dt = _softplus(pre)
        a_head = -jnp.exp(par_ref[1:2, :])
        cs = _dot_sel(dt * a_head, tril_ref[...], SEL_X)
        cst = cs.T
        itile, trit, bmask = itile_ref[...], trit_ref[...], bmask_ref[...]
        dcs = jnp.zeros((CHUNK, LANE), F32)
        dcs_last = jnp.zeros((1, LANE), F32)
        ddt_u = jnp.zeros((CHUNK, LANE), F32)
        d_skip = jnp.zeros((1, LANE), F32)
        rsum = lambda v: jnp.sum(v, axis=0, keepdims=True)
        row8 = lax.broadcasted_iota(jnp.int32, (8, GROUP_W), 0)
        for g in range(N_GROUPS):
            t = _group_terms(g, dt, cs, cst, xbc_ref, trit, bmask, d_inner)
            gl, m, lm, u, bb, cb, e, dec, xs = (t[k] for k in ("gl", "m", "lm", "u", "bb", "cb", "e", "dec", "xs"))
            indt = indt_ref[gl, :]
            dy = dyf_ref[:, gl]
            dyb = dy.astype(BF16)
            p = st_in_ref[0, g]
            pb = p.astype(BF16)
            q = dq_ref[g]
            qb = q.astype(BF16)
            big = lax.dot_general(m.astype(BF16), dyb, TN, preferred_element_type=F32)
            du = jnp.zeros((CHUNK, GROUP_W), F32)
            for rh in range(HEADS_PER_GROUP):
                du = du + big[hslices[rh], :] * bmask[rh * HEAD_DIM:rh * HEAD_DIM + 1, :]
            dm = lax.dot_general(dyb, t["bdu"], NT, preferred_element_type=F32)
            w = dm * m
            dgt = (dm * lm).astype(BF16)
            dc = jnp.dot(dgt, t["btile"], preferred_element_type=F32)
            db_big = lax.dot_general(dgt, cb, TN, preferred_element_type=F32)
            db = db_big[hslices[0], :] + db_big[hslices[1], :] + db_big[hslices[2], :] + db_big[hslices[3], :]
            cp = jnp.dot(cb, pb, preferred_element_type=F32)
            dye = dy * e
            dyeb = dye.astype(BF16)
            dc = dc + lax.dot_general(dyeb, pb, NT, preferred_element_type=F32)
            dp = lax.dot_general(cb, dyeb, TN, preferred_element_type=F32)
            x2 = dye * cp
            bq = jnp.dot(bb, qb, preferred_element_type=F32)
            ud = u * dec
            du = du + bq * dec
            db = db + lax.dot_general(ud.astype(BF16), qb, NT, preferred_element_type=F32)
            x1 = bq * ud
            dq_ref[g] = dp + t["e_last"] * q
            x3 = rsum(q * p) * t["e_last"]
            red = _dot_sel(jnp.concatenate([w + x2 - x1, du * xs, itile * rsum(w)], axis=0), indt)
            dcs = dcs + red[0:CHUNK] - red[2 * CHUNK:3 * CHUNK]
            ddt_u = ddt_u + red[CHUNK:2 * CHUNK]
            tail = _dot_sel(jnp.where(row8 == 0, rsum(x1) + x3, jnp.where(row8 == 1, rsum(dy * xs), 0.0)), indt)
            dcs_last = dcs_last + tail[0:1]
            d_skip = d_skip + tail[1:2]
            dxbc_ref[:, gl] = du * t["dtx"] + dy * dexp_ref[:, gl]
            dxbc_ref[:, d_inner + g * D_STATE: d_inner + (g + 1) * D_STATE] = db
            dxbc_ref[:, d_inner + (N_GROUPS + g) * D_STATE: d_inner + (N_GROUPS + g + 1) * D_STATE] = dc
        last = lax.broadcasted_iota(jnp.int32, (CHUNK, LANE), 0) == CHUNK - 1
        dcs = dcs + jnp.where(last, dcs_last, 0.0)
        da = _dot_sel(dcs, tril_ref[...], TN)
        ddt = da * a_head + ddt_u
        heads = lax.broadcasted_iota(jnp.int32, (CHUNK, LANE), 1) < d_inner // HEAD_DIM
        ddt_raw = jnp.where(heads, ddt * _sigmoid(pre), 0.0)
        dzx_ref[:, nz - LANE:nz] = ddt_raw.astype(BF16)
        dpar_ref[0:1, :] += rsum(ddt_raw)
        dpar_ref[1:2, :] += rsum(da * dt) * a_head
        dpar_ref[2:3, :] += d_skip

    rev = lambda c: (n_chunks - 1 - c, 0)
    fix = lambda c: (0, 0)
    cspec = lambda a: pl.BlockSpec(a.shape, fix)
    nx = xbc.shape[1]
    return _carrier_call(
        body, comm, n_chunks,
        in_specs=[pl.BlockSpec((CHUNK, d_inner), rev), pl.BlockSpec((CHUNK, d_inner), rev),
                  pl.BlockSpec((CHUNK, d_inner), rev), pl.BlockSpec((CHUNK, LANE), lambda c: (n_chunks - 1 - c, dt_blk)),
                  pl.BlockSpec((CHUNK, nx), rev),
                  pl.BlockSpec((1, N_GROUPS, D_STATE, GROUP_W), lambda c: (n_chunks - 1 - c, 0, 0, 0)),
                  cspec(par), cspec(dexp), cspec(nw), cspec(ind_t), cspec(itile_c), cspec(trit_c),
                  cspec(tril_c), cspec(bmask_c)],
        out_specs=[pl.BlockSpec((CHUNK, nz), rev), pl.BlockSpec((CHUNK, nx), rev),
                   pl.BlockSpec((1, d_inner), fix), pl.BlockSpec((8, LANE), fix)],
        out_shape=[SDS((L, nz), BF16), SDS((L, nx), F32), SDS((1, d_inner), F32), SDS((8, LANE), F32)],
        scratch_shapes=[pltpu.VMEM((N_GROUPS, D_STATE, GROUP_W), F32), pltpu.VMEM((CHUNK, d_inner), F32)],
        name=name, args=(dyn, yf, zx, zx, xbc, states, par, dexp, nw, ind_t, itile_c, trit_c, tril_c, bmask_c))


def _adamw(w, m, v, g, name):
    R, C = w.shape
    tr = R
    for cand in (256, 128, 64, 32, 16, 8):
        if R % cand == 0:
            tr = cand
            break

    def body(w_ref, m_ref, v_ref, g_ref, d_ref, mo_ref, vo_ref):
        gv = g_ref[...]
        mn = ADAM_B1 * m_ref[...] + (1.0 - ADAM_B1) * gv
        vn = ADAM_B2 * v_ref[...] + (1.0 - ADAM_B2) * (gv * gv)
        m_hat = mn / (1.0 - ADAM_B1 ** ADAM_STEP)
        v_hat = vn / (1.0 - ADAM_B2 ** ADAM_STEP)
        d_ref[...] = -ADAM_LR * (m_hat / (jnp.sqrt(v_hat) + ADAM_EPS) + ADAM_WD * w_ref[...])
        mo_ref[...] = mn
        vo_ref[...] = vn

    blk = pl.BlockSpec((tr, C), lambda i: (i, 0))
    return pl.pallas_call(
        body, grid=(R // tr,), in_specs=[blk] * 4, out_specs=[blk] * 3, out_shape=[SDS((R, C), F32)] * 3,
        name=name, compiler_params=_cp("parallel"))(w, m, v, g)


def _sum_slots(parts, name):
    n, R, C = parts.shape
    tr = 128 if R % 128 == 0 else R

    def body(p_ref, o_ref):
        acc = p_ref[0]
        for k in range(1, n):
            acc = acc + p_ref[k]
        o_ref[...] = acc

    return pl.pallas_call(
        body, grid=(R // tr,), in_specs=[pl.BlockSpec((n, tr, C), lambda i: (0, i, 0))],
        out_specs=pl.BlockSpec((tr, C), lambda i: (i, 0)), out_shape=SDS((R, C), F32),
        name=name, compiler_params=_cp("parallel"))(parts)


def _add_core_halves(where, g, r, name):
    _, n, a, b = g.shape
    ta = a // 2

    def body(w_ref, g_ref, r_ref, o_ref):
        o_ref[...] = (g_ref[...].astype(F32) + r_ref[...].astype(F32)).astype(BF16)

    blk = lambda f: pl.BlockSpec((None, None, ta, b), f)
    mine = lambda s, l, w: (s, l, 0, 0)
    return pl.pallas_call(
        body, grid_spec=pltpu.PrefetchScalarGridSpec(
            num_scalar_prefetch=1, grid=(4, n),
            in_specs=[blk(lambda s, l, w: (s, l, w[1], 0)), blk(mine)], out_specs=blk(mine)),
        out_shape=SDS((4, n, ta, b), BF16), name=name,
        compiler_params=_cp("parallel", "parallel"))(where, g, r)


def _sum_shard(where, g, r, rr, into, layer, n_layers, name):
    _, _, a, b = g.shape
    ta = a // 2

    def body(w_ref, g_ref, r_ref, rr_ref, *refs):
        f = lambda v: v.astype(F32)
        refs[-1][...] = (((f(g_ref[...]) + f(r_ref[...])) + f(rr_ref[0])) + f(rr_ref[1])) + f(rr_ref[2])

    more = [] if into is None else [into]
    return pl.pallas_call(
        body, grid_spec=pltpu.PrefetchScalarGridSpec(
            num_scalar_prefetch=1, grid=(1,),
            in_specs=[pl.BlockSpec((None, None, ta, b), lambda l, w: (w[0], 0, w[1], 0)),
                      pl.BlockSpec((None, None, ta, b), lambda l, w: (w[0], 0, 0, 0)),
                      pl.BlockSpec((3, None, ta, b), lambda l, w: (0, 0, 0, 0))] + [ANY] * len(more),
            out_specs=pl.BlockSpec((None, ta, b), lambda l, w: (layer, w[1], 0))),
        out_shape=SDS((n_layers, a, b), F32), name=name, input_output_aliases={4: 0} if more else {},
        compiler_params=_cp("arbitrary"))(where, g, r, rr, *more)


def _me():
    return lax.axis_index("x"), lax.axis_index("y"), lax.axis_index("c")


def _chip_peers(x, y):
    return [(1 - x, y), (x, 1 - y), (1 - x, 1 - y)]


def _rcopy(src, dst, send_sems, recv_sems, k, to):
    return pltpu.make_async_remote_copy(src_ref=src, dst_ref=dst, send_sem=send_sems.at[k], recv_sem=recv_sems.at[k],
                                        device_id=to, device_id_type=MESH)


def _row_half(ref, c, lead=()):
    a = ref.shape[len(lead) + 1]
    return ref.at[(*lead, slice(None), pl.ds(c * (a // 2), a // 2))]


class _Comm(NamedTuple):
    ins: list
    out_shapes: list
    n_sems: int
    start: Callable
    finish: Callable


def _sem_scratch(comm):
    return [pltpu.SemaphoreType.DMA((comm.n_sems,)), pltpu.SemaphoreType.DMA((comm.n_sems,))]


def _run_comm(comm, name):
    n_in, n_out = len(comm.ins), len(comm.out_shapes)

    def body(*refs):
        ins, outs, sems = refs[:n_in], refs[n_in:n_in + n_out], refs[n_in + n_out:]
        comm.start(ins, outs, *sems)
        comm.finish(ins, outs, *sems)

    return pl.pallas_call(body, in_specs=[ANY] * n_in, out_specs=[ANY] * n_out, out_shape=comm.out_shapes,
                          scratch_shapes=_sem_scratch(comm), name=name)(*comm.ins)


def _carrier_call(compute, comm, n_steps, in_specs, out_specs, out_shape, scratch_shapes, name, args):
    if comm is None:
        return pl.pallas_call(compute, grid=(n_steps,), in_specs=in_specs, out_specs=out_specs, out_shape=out_shape,
                              scratch_shapes=scratch_shapes, name=name, compiler_params=_cp("arbitrary"))(*args), []
    n_in, n_out, n_scr = len(in_specs), len(out_specs), len(scratch_shapes)
    n_ci, n_co = len(comm.ins), len(comm.out_shapes)

    def body(*refs):
        ins, cins = refs[:n_in], refs[n_in:n_in + n_ci]
        o = n_in + n_ci
        outs, couts = refs[o:o + n_out], refs[o + n_out:o + n_out + n_co]
        s = o + n_out + n_co
        scratch, sems = refs[s:s + n_scr], refs[s + n_scr:]

        @pl.when(pl.program_id(0) == 0)
        def _():
            comm.start(cins, couts, *sems)

        compute(*ins, *outs, *scratch)

        @pl.when(pl.program_id(0) == n_steps - 1)
        def _():
            comm.finish(cins, couts, *sems)

    res = pl.pallas_call(
        body, grid=(n_steps,), in_specs=list(in_specs) + [ANY] * n_ci, out_specs=list(out_specs) + [ANY] * n_co,
        out_shape=list(out_shape) + list(comm.out_shapes), scratch_shapes=list(scratch_shapes) + _sem_scratch(comm),
        name=name, compiler_params=_cp("arbitrary"))(*args, *comm.ins)
    return res[:n_out], res[n_out:]


def _allgather_plan(mine, small=None):
    n = len(mine)
    ins = list(mine) + ([] if small is None else [small])
    out_shapes = [SDS((4,) + m.shape, BF16) for m in mine] + ([] if small is None else [SDS((4,) + small.shape, F32)])
    sem = lambda t, k: 7 * t + k

    def first_copies(ins_r, outs_r, send, recv):
        x, y, c = _me()
        q = 2 * x + y
        cps = []
        for j, chip in enumerate(_chip_peers(x, y)):
            for t in range(n):
                cps.append(_rcopy(_row_half(ins_r[t], c), _row_half(outs_r[t], c, (q,)), send, recv, sem(t, j),
                                  (*chip, c)))
            if small is not None:
                cps.append(_rcopy(ins_r[n], outs_r[n].at[q], send, recv, sem(n, j), (*chip, c)))
        for t in range(len(ins)):
            cps.append(_rcopy(ins_r[t], outs_r[t].at[q], send, recv, sem(t, 6), (x, y, 1 - c)))
        return cps

    def start(ins_r, outs_r, send, recv):
        for cp in first_copies(ins_r, outs_r, send, recv):
            cp.start()

    def finish(ins_r, outs_r, send, recv):
        x, y, c = _me()
        sib = (x, y, 1 - c)
        chips = _chip_peers(x, y)
        passed = []
        for j, (px, py) in enumerate(chips):
            for t in range(n):
                blk = _row_half(outs_r[t], c, (2 * px + py,))
                _rcopy(blk, blk, send, recv, sem(t, j), sib).wait_recv()
                cp = _rcopy(blk, blk, send, recv, sem(t, 3 + j), sib)
                cp.start()
                passed.append(cp)
        for j, (px, py) in enumerate(chips):
            for t in range(n):
                blk = _row_half(outs_r[t], 1 - c, (2 * px + py,))
                _rcopy(blk, blk, send, recv, sem(t, 3 + j), sib).wait_recv()
            if small is not None:
                sblk = outs_r[n].at[2 * px + py]
                _rcopy(sblk, sblk, send, recv, sem(n, j), sib).wait_recv()
        for t in range(len(ins)):
            own = outs_r[t].at[2 * x + y]
            _rcopy(own, own, send, recv, sem(t, 6), sib).wait_recv()
        for cp in first_copies(ins_r, outs_r, send, recv) + passed:
            cp.wait_send()

    return _Comm(ins, out_shapes, 7 * len(ins), start, finish)


def _grads_to_sibling_plan(grads):
    n = len(grads)

    def copies(ins_r, outs_r, send, recv):
        x, y, c = _me()
        return [_rcopy(_row_half(ins_r[t], 1 - c, (slice(None),)), outs_r[t], send, recv, t, (x, y, 1 - c))
                for t in range(n)]

    def start(*a):
        for cp in copies(*a):
            cp.start()

    def finish(*a):
        for cp in copies(*a):
            cp.wait()

    out_shapes = [SDS((4, g.shape[1], g.shape[2] // 2, g.shape[3]), BF16) for g in grads]
    return _Comm(list(grads), out_shapes, n, start, finish)


def _grads_to_chips_plan(psums):
    n = len(psums)

    def copies(ins_r, outs_r, send, recv):
        x, y, c = _me()
        return [_rcopy(ins_r[t].at[2 * px + py], outs_r[t].at[j], send, recv, 3 * t + j, (px, py, c))
                for j, (px, py) in enumerate(_chip_peers(x, y)) for t in range(n)]

    def start(*a):
        for cp in copies(*a):
            cp.start()

    def finish(*a):
        for cp in copies(*a):
            cp.wait()

    return _Comm(list(psums), [SDS((3,) + p.shape[1:], BF16) for p in psums], 3 * n, start, finish)


def _swap_halves(sums):
    n = len(sums)

    def body(*refs):
        out_refs = refs[n:2 * n]
        send_sems, recv_sems = refs[2 * n:]
        x, y, c = _me()
        sib = (x, y, 1 - c)
        cps = [_rcopy(_row_half(out_refs[t], c), _row_half(out_refs[t], c), send_sems, recv_sems, t, sib)
               for t in range(n)]
        for cp in cps:
            cp.start()
        for t in range(n):
            other = _row_half(out_refs[t], 1 - c)
            _rcopy(other, other, send_sems, recv_sems, t, sib).wait_recv()
        for cp in cps:
            cp.wait_send()

    return pl.pallas_call(
        body, in_specs=[ANY] * n, out_specs=[ANY] * n, out_shape=[SDS(s.shape, F32) for s in sums],
        input_output_aliases={t: t for t in range(n)},
        scratch_shapes=[pltpu.SemaphoreType.DMA((n,)), pltpu.SemaphoreType.DMA((n,))],
        name="swap_halves")(*sums)


def _allgather_small(part):
    def body(p_ref, out_ref, send_sems, recv_sems, local_sem):
        x, y, c = _me()
        me = 4 * x + 2 * y + c
        own = pltpu.make_async_copy(p_ref, out_ref.at[me], local_sem.at[0])
        own.start()
        sends = []
        for k in range(1, 8):
            fx, fy, fc = (k >> 2) & 1, (k >> 1) & 1, k & 1
            to = (x ^ fx, y ^ fy, c ^ fc)
            sends.append(_rcopy(p_ref, out_ref.at[me], send_sems, recv_sems, k - 1, to))
        for cp in sends:
            cp.start()
        for k in range(1, 8):
            slot = out_ref.at[me ^ k]
            _rcopy(slot, slot, send_sems, recv_sems, k - 1, (x, y, c)).wait_recv()
        for cp in sends:
            cp.wait_send()
        own.wait()

    return pl.pallas_call(
        body, in_specs=[ANY], out_specs=ANY, out_shape=SDS((8,) + part.shape, F32),
        scratch_shapes=[pltpu.SemaphoreType.DMA((7,)), pltpu.SemaphoreType.DMA((7,)), pltpu.SemaphoreType.DMA((1,))],
        name="allgather_small")(part)


BIG = (("ssd_w_in", 2), ("ssd_w_out", 1), ("sc_w_in", 2), ("sc_w_out", 1), ("ffn_w_up", 2), ("ffn_w_down", 1))


def _to_shards(full, axis):
    A, B = full.shape
    if axis == 2:
        return full.reshape(A, 4, B // 4).transpose(1, 0, 2)
    return full.reshape(4, A // 4, B)


def _from_shards(shards, axis):
    _, a, b = shards.shape
    if axis == 2:
        return shards.transpose(1, 0, 2).reshape(a, 4 * b)
    return shards.reshape(4 * a, b)


def _interleave(w, parts):
    lead, n = w.shape[:-1], w.shape[-1]
    return w.reshape(*lead, parts, n // (parts * LANE), LANE).swapaxes(-2, -3).reshape(*lead, n)


def _deinterleave(w, parts):
    lead, n = w.shape[:-1], w.shape[-1]
    return w.reshape(*lead, n // (parts * LANE), parts, LANE).swapaxes(-2, -3).reshape(*lead, n)


def _pack_rows(vectors, width, row_multiple):
    flat = jnp.concatenate(vectors, axis=-1)
    n = flat.shape[-1]
    unit = width * row_multiple
    total = -(-n // unit) * unit
    flat = jnp.pad(flat, [(0, 0)] * (flat.ndim - 1) + [(0, total - n)])
    return flat.reshape(*flat.shape[:-1], total // width, width)


def _unpack(flat, shapes):
    out, off = [], 0
    for s in shapes:
        n = int(np.prod(s))
        out.append(flat[..., off:off + n].reshape(*flat.shape[:-1], *s))
        off += n
    return out


def _memo(fn):
    cache = {}

    def wrapped(k):
        if k not in cache:
            cache[k] = fn(k)
        return cache[k]

    return wrapped


def _row(v):
    return v.reshape(1, -1)


def _pad_rows(w, rows=8):
    return jnp.pad(w, ((0, rows - w.shape[0]), (0, 0)))


def _ffn_fwd(x, g_pre, g_post, w_up, cw, cb, w_down, tag, comm_up=None, comm_down=None):
    (up, hn, a), got_up = _ffn_up(x, g_pre, w_up, cw, cb, "ffn_up" + tag, comm_up)
    (f, x_new), got_down = _matmul_norm_res(a, w_down, x, g_post, "ffn_down" + tag, comm_down)
    return x_new, (x, hn, up, a, f), got_up, got_down


def _ffn_down_dx(dx, f, g, w_down, up, cw, cb, name):
    L, D = dx.shape
    C = w_down.shape[0]
    tm = min(MM_ROWS, L)
    n = L // tm
    per = tm // (2 * HALO)

    def body(dx_ref, f_ref, g_ref, w_ref, up_ref, prev_ref, cw_ref, cb_ref, dup_ref, df_ref, dg_ref, st_ref,
             da_ref, head_ref):
        i = pl.program_id(0)
        tile = n - 1 - i

        @pl.when(i == 0)
        def _():
            dg_ref[...] = jnp.zeros_like(dg_ref)
            st_ref[...] = jnp.zeros_like(st_ref)
            head_ref[...] = jnp.zeros_like(head_ref)

        fv = f_ref[...]
        dxv = dx_ref[...]
        r = _rsq(fv)
        fh = fv * r
        dg_ref[...] += jnp.sum(dxv * fh, axis=0, keepdims=True)
        dyg = dxv * g_ref[...]
        df = (r * (dyg - fh * jnp.mean(dyg * fh, axis=-1, keepdims=True))).astype(BF16)
        df_ref[...] = df
        da_ref[...] = lax.dot_general(df, w_ref[...], NT, preferred_element_type=F32)
        rsum = lambda v: jnp.sum(v, axis=0, keepdims=True)
        for s in range(C // LANE):
            sl = slice(s * LANE, (s + 1) * LANE)
            w0, w1, w2 = cw_ref[0:1, sl], cw_ref[1:2, sl], cw_ref[2:3, sl]
            gp = up_ref[0, :, sl].astype(F32)
            prev = jnp.where(tile > 0, prev_ref[0, :, sl].astype(F32)[HALO:], 0.0)
            ext = jnp.concatenate([prev, gp], axis=0)
            g1, g2 = pltpu.roll(ext, 1, axis=0)[HALO:HALO + tm], pltpu.roll(ext, 2, axis=0)[HALO:HALO + tm]
            gate = gp * w2 + g1 * w1 + g2 * w0 + cb_ref[:, sl]
            sg = _sigmoid_fast(gate)
            da = da_ref[:, sl]
            dgate = da * up_ref[1, :, sl].astype(F32) * _dsilu(gate, sg)
            dext = jnp.concatenate([dgate, head_ref[:, sl]], axis=0)
            ahead = lambda k: pltpu.roll(dext, tm + HALO - k, axis=0)[:tm]
            dup_ref[0, :, sl] = (dgate * w2 + ahead(1) * w1 + ahead(2) * w0).astype(BF16)
            dup_ref[1, :, sl] = (da * gate * sg).astype(BF16)
            head_ref[:, sl] = dgate[:HALO]
            st_ref[0:1, sl] += rsum(dgate * g2)
            st_ref[1:2, sl] += rsum(dgate * g1)
            st_ref[2:3, sl] += rsum(dgate * gp)
            st_ref[3:4, sl] += rsum(dgate)

    rev = lambda i: (n - 1 - i, 0)
    fix = lambda i: (0, 0)
    pair = pl.BlockSpec((2, tm, C), lambda i: (0, n - 1 - i, 0))
    halo = pl.BlockSpec((2, 2 * HALO, C), lambda i: (0, jnp.maximum((n - 1 - i) * per - 1, 0), 0))
    return pl.pallas_call(
        body, grid=(n,),
        in_specs=[pl.BlockSpec((tm, D), rev), pl.BlockSpec((tm, D), rev), pl.BlockSpec((1, D), fix),
                  pl.BlockSpec((C, D), fix), pair, halo, pl.BlockSpec(cw.shape, fix), pl.BlockSpec(cb.shape, fix)],
        out_specs=[pair, pl.BlockSpec((tm, D), rev), pl.BlockSpec((1, D), fix), pl.BlockSpec((8, C), fix)],
        out_shape=[SDS((2, L, C), BF16), SDS((L, D), BF16), SDS((1, D), F32), SDS((8, C), F32)],
        scratch_shapes=[pltpu.VMEM((tm, C), F32), pltpu.VMEM((HALO, C), F32)],
        name=name, compiler_params=_cp("arbitrary"))(dx, f, g, w_down, up, up, cw, cb)


def _ffn_bwd(dx, saved, g_pre, g_post, w_up, cw, cb, w_down, tag):
    x, hn, up, a, f = saved
    dup, df, dg_post, stats = _ffn_down_dx(dx, f, g_post, w_down, up, cw, cb, "ffn_down_dx" + tag)
    dw_down = _matmul_tn(a, df, "ffn_down_dw" + tag)
    dx_in, dg_pre = _ffn_up_dx(dup, w_up, x, g_pre, dx, "ffn_up_dx" + tag)
    dw_up = _ffn_up_dw(hn, dup, "ffn_up_dw" + tag)
    return dx_in, dict(g_pre=dg_pre, g_post=dg_post, w_up=dw_up, w_down=dw_down, cw=stats[0:3], cb=stats[3])


def _sc_fwd(x, g_pre, g_post, w_in, cw, w_out, tag, comm_in=None, comm_out=None):
    (bcv, hn, q), got_in = _norm_matmul(x, g_pre, w_in, BF16, "sc_in" + tag, comm_in, _sc_gate_epilogue(cw))
    (m, x_new), got_out = _matmul_norm_res(q, w_out, x, g_post, "sc_out" + tag, comm_out)
    return x_new, (x, hn, bcv, q, m), got_in, got_out


def _sc_bwd(dx, saved, g_pre, g_post, w_in, cw, w_out, tag):
    x, hn, bcv, q, m = saved
    dq, dm, dg_post = _postnorm_bwd_matmul_nt(dx, m, g_post, w_out, BF16, "sc_out_dx" + tag)
    dw_out = _matmul_tn(q, dm, "sc_out_dw" + tag)
    dbcv, stats = _sc_mid_bwd(dq, bcv, cw, "sc_mid_bwd" + tag)
    (dx_in, dg_pre), _ = _matmul_nt_prenorm_bwd(dbcv, w_in, x, g_pre, dx, "sc_in_dx" + tag)
    dw_in = _matmul_tn(hn, dbcv, "sc_in_dw" + tag)
    return dx_in, dict(g_pre=dg_pre, g_post=dg_post, w_in=dw_in, w_out=dw_out, cw=stats[0:3])


def _ssd_fwd(x, g_pre, g_post, w_in, cw, cb, par, dexp, nw, w_out, consts, comm, tag, comm_in=None):
    d_inner = dexp.shape[1]
    (zx, hn, xbc), got_in = _norm_matmul(x, g_pre, w_in, F32, "ssd_in" + tag, comm_in,
                                         _ssd_conv_epilogue(cw, cb, d_inner))
    (yn, yf, states), got = _ssd_scan_fwd(zx, xbc, par, dexp, nw, consts, comm, "ssd_scan_fwd" + tag)
    if callable(w_out):
        w_out = w_out(got_in)
    (m, x_new), _ = _matmul_norm_res(yn, w_out, x, g_post, "ssd_out" + tag)
    return x_new, (x, hn, zx, xbc, yn, yf, states, m), got


def _ssd_bwd(dx, saved, g_pre, g_post, w_in, cw, cb, par, dexp, nw, w_out, consts, comm, tag, comm_dx=None):
    x, hn, zx, xbc, yn, yf, states, m = saved
    d_inner = dexp.shape[1]
    dyn, dm, dg_post = _postnorm_bwd_matmul_nt(dx, m, g_post, w_out, F32, "ssd_out_dx" + tag)
    dw_out = _matmul_tn(yn, dm, "ssd_out_dw" + tag)
    (dzx, dxbc, dnw, dpar), got = _ssd_scan_bwd(dyn, yf, zx, xbc, states, par, dexp, nw, consts, comm(dw_out),
                                                "ssd_scan_bwd" + tag)
    dzx, stats = _ssd_conv_bwd(dxbc, zx, cw, cb, d_inner, dzx, "ssd_conv_bwd" + tag)
    dw_in = _matmul_tn(hn, dzx, "ssd_in_dw" + tag)
    (dx_in, dg_pre), got_dx = _matmul_nt_prenorm_bwd(dzx, w_in, x, g_pre, dx, "ssd_in_dx" + tag,
                                                     None if comm_dx is None else comm_dx(dw_in))
    n_heads = d_inner // HEAD_DIM
    grads = dict(g_pre=dg_pre, g_post=dg_post, w_in=dw_in, w_out=dw_out, cw=stats[0:4], cb=stats[4],
                 dt_bias=dpar[0, :n_heads], a_log=dpar[1, :n_heads], d=dpar[2, :n_heads], nw=dnw[0])
    return dx_in, grads, got, got_dx


def kernel(x, mix_pre_g, mix_post_g, ffn_pre_g, ffn_post_g, ssd_w_in, ssd_conv_w, ssd_conv_b, ssd_dt_bias, ssd_A_log, ssd_D, ssd_norm_w, ssd_w_out, sc_w_in, sc_conv_w, sc_w_out, ffn_w_up, ffn_conv_w, ffn_conv_b, ffn_w_down, loss_target, m_mix_pre_g, m_mix_post_g, m_ffn_pre_g, m_ffn_post_g, m_ssd_w_in, m_ssd_conv_w, m_ssd_conv_b, m_ssd_dt_bias, m_ssd_A_log, m_ssd_D, m_ssd_norm_w, m_ssd_w_out, m_sc_w_in, m_sc_conv_w, m_sc_w_out, m_ffn_w_up, m_ffn_conv_w, m_ffn_conv_b, m_ffn_w_down, v_mix_pre_g, v_mix_post_g, v_ffn_pre_g, v_ffn_post_g, v_ssd_w_in, v_ssd_conv_w, v_ssd_conv_b, v_ssd_dt_bias, v_ssd_A_log, v_ssd_D, v_ssd_norm_w, v_ssd_w_out, v_sc_w_in, v_sc_conv_w, v_sc_w_out, v_ffn_w_up, v_ffn_conv_w, v_ffn_conv_b, v_ffn_w_down):
    names = ["mix_pre_g", "mix_post_g", "ffn_pre_g", "ffn_post_g", "ssd_w_in", "ssd_conv_w", "ssd_conv_b",
             "ssd_dt_bias", "ssd_A_log", "ssd_D", "ssd_norm_w", "ssd_w_out", "sc_w_in", "sc_conv_w", "sc_w_out",
             "ffn_w_up", "ffn_conv_w", "ffn_conv_b", "ffn_w_down"]
    env = locals()
    wts = {n: env[n] for n in names}
    mom = {n: env["m_" + n] for n in names}
    var = {n: env["v_" + n] for n in names}

    depth, d_model = mix_pre_g.shape
    n_ssd, n_heads = ssd_dt_bias.shape
    n_sc = sc_conv_w.shape[0]
    d_inner = n_heads * HEAD_DIM
    conv_dim = d_inner + 2 * N_GROUPS * D_STATE
    ssd_in_dim = d_inner + conv_dim + n_heads
    ssd_in_pad = d_inner + conv_dim + LANE
    q_chip = 2 * lax.axis_index("x") + lax.axis_index("y")
    core = lax.axis_index("c")

    assert depth == 4 and n_ssd == 2 and n_sc == 2, "the exchange schedule is written for this trunk"
    ssd_items = lambda j: [("ssd_w_in", j), ("ssd_w_out", j)]
    sc_items = lambda j: [("sc_w_in", j), ("sc_w_out", j)]
    ffn_items = lambda i: [("ffn_w_up", i), ("ffn_w_down", i)]
    gather_first = [("ssd_w_in", 0)]
    gather_in_ssd_in = {0: [("ssd_w_out", 0)]}
    gather_in_scan = {0: ffn_items(0) + sc_items(0), 2: ffn_items(2) + sc_items(1)}
    gather_in_ffn = {0: ([("ffn_w_up", 1)], [("ffn_w_down", 1)]), 1: ([("ssd_w_in", 1)], []),
                     2: ([("ffn_w_up", 3)], [("ffn_w_down", 3)])}
    gather_in_sc = {1: ([("ssd_w_out", 1)], [])}
    reduce_in_scan = {2: ffn_items(3) + sc_items(1) + ffn_items(2),
                      0: ssd_items(1) + ffn_items(1) + sc_items(0) + ffn_items(0) + [("ssd_w_out", 0)]}
    reduce_in_dx = {0: [("ssd_w_in", 0)]}
    axis_of = dict(BIG)
    gathered = {}

    def gather_plan(items, small=None):
        mine = [wts[n][layer:layer + 1].astype(BF16) for n, layer in items]
        return _allgather_plan(mine, small) if items else None

    def gather_done(items, results):
        for item, buf in zip(items, results):
            gathered[item] = buf[:, 0]

    def full(n, layer):
        return _from_shards(gathered[(n, layer)], axis_of[n])

    conv_names = ["ssd_conv_w", "sc_conv_w", "ffn_conv_w"]
    conv_shapes = [wts[n].shape for n in conv_names]
    small_mine = _pack_rows([wts[n].reshape(-1) for n in conv_names], LANE, 8)
    *results, small_all = _run_comm(gather_plan(gather_first, small_mine), "allgather_first")
    gather_done(gather_first, results)
    conv_full = {}
    for n, f, s in zip(conv_names, _unpack(small_all.reshape(4, -1), conv_shapes), conv_shapes):
        conv_full[n] = f.transpose(1, 2, 0, 3).reshape(s[0], s[1], 4 * s[2])

    consts = _scan_constants(n_heads)

    def ssd_args(j):
        par = jnp.zeros((8, LANE), F32).at[0, :n_heads].set(ssd_dt_bias[j]).at[1, :n_heads].set(ssd_A_log[j])
        dexp = jnp.repeat(ssd_D[j], HEAD_DIM).reshape(1, d_inner)
        w_in = jnp.pad(full("ssd_w_in", j), ((0, 0), (0, ssd_in_pad - ssd_in_dim)))
        return (w_in, _pad_rows(conv_full["ssd_conv_w"][j]), _row(ssd_conv_b[j]), par, dexp, _row(ssd_norm_w[j]))

    def sc_args(j):
        return (_interleave(full("sc_w_in", j), 3), _pad_rows(conv_full["sc_conv_w"][j]), full("sc_w_out", j))

    def ffn_args(i):
        return (gathered[("ffn_w_up", i)], _pad_rows(conv_full["ffn_conv_w"][i]), _row(ffn_conv_b[i]),
                full("ffn_w_down", i))

    ssd_args, sc_args, ffn_args = _memo(ssd_args), _memo(sc_args), _memo(ffn_args)

    h = x[0]
    saved = []
    for i in range(depth):
        j = i // 2
        gp, gq = _row(mix_pre_g[i]), _row(mix_post_g[i])
        if i % 2 == 0:
            items_in = gather_in_ssd_in.get(i, [])

            def w_out_when_here(got_in, items_in=items_in, j=j):
                gather_done(items_in, got_in)
                return full("ssd_w_out", j)

            h, sv, results = _ssd_fwd(h, gp, gq, *ssd_args(j), w_out_when_here, consts, gather_plan(gather_in_scan[i]),
                                      tag="", comm_in=gather_plan(items_in))
            gather_done(gather_in_scan[i], results)
        else:
            items_in, items_out = gather_in_sc.get(i, ([], []))
            h, sv, got_in, got_out = _sc_fwd(h, gp, gq, *sc_args(j), tag="", comm_in=gather_plan(items_in),
                                             comm_out=gather_plan(items_out))
            gather_done(items_in, got_in)
            gather_done(items_out, got_out)
        items_up, items_down = gather_in_ffn.get(i, ([], []))
        h, sv2, got_up, got_down = _ffn_fwd(h, _row(ffn_pre_g[i]), _row(ffn_post_g[i]), *ffn_args(i), tag="",
                                            comm_up=gather_plan(items_up), comm_down=gather_plan(items_down))
        gather_done(items_up, got_up)
        gather_done(items_down, got_down)
        saved.append((sv, sv2))
    dh, loss_part = _loss_head(h, loss_target[0], "loss_head")

    mix_grads, ffn_grads = [None] * depth, [None] * depth
    where = jnp.stack([q_chip, core]).astype(jnp.int32)

    def shard_grad(n, layer):
        if n == "ffn_w_up":
            g = ffn_grads[layer]["w_up"]
        elif n == "ffn_w_down":
            g = _to_shards(ffn_grads[layer]["w_down"], 1)
        elif n == "ssd_w_in":
            g = _to_shards(early[(n, layer)][:, :ssd_in_dim], 2)
        elif n == "ssd_w_out":
            g = _to_shards(early[(n, layer)], 1)
        elif n == "sc_w_in":
            g = _to_shards(_deinterleave(mix_grads[2 * layer + 1]["w_in"], 3), 2)
        else:
            g = _to_shards(mix_grads[2 * layer + 1]["w_out"], 1)
        return g[:, None]

    def reduce_begin(items, tag):
        by_shard = [shard_grad(n, layer) for n, layer in items]
        from_sib = _run_comm(_grads_to_sibling_plan(by_shard), "grads_to_sibling" + tag)
        chip_sums = [_add_core_halves(where, g, r, "add_core_halves_%s%d" % item)
                     for item, g, r in zip(items, by_shard, from_sib)]
        return by_shard, from_sib, _grads_to_chips_plan(chip_sums)

    sums = {}
    early = {}

    def riding(items, tag):
        by_shard, from_sib, plan = reduce_begin(items, tag)

        def arrived(from_chips):
            for (n, layer), g, r, rr in zip(items, by_shard, from_sib, from_chips):
                sums[n] = _sum_shard(where, g, r, rr, sums.get(n), layer, wts[n].shape[0],
                                     "sum_shard_%s%d" % (n, layer))

        return plan, arrived

    for i in reversed(range(depth)):
        j = i // 2
        sv, sv2 = saved[i]
        dh, ffn_grads[i] = _ffn_bwd(dh, sv2, _row(ffn_pre_g[i]), _row(ffn_post_g[i]), *ffn_args(i), tag="")
        gp, gq = _row(mix_pre_g[i]), _row(mix_post_g[i])
        if i % 2 == 0:
            then = {}

            def in_scan(dw_out, i=i, j=j, then=then):
                early[("ssd_w_out", j)] = dw_out
                plan, then["scan"] = riding(reduce_in_scan[i], "_%d" % i)
                return plan

            def in_dx(dw_in, i=i, j=j, then=then):
                early[("ssd_w_in", j)] = dw_in
                plan, then["dx"] = riding(reduce_in_dx[i], "_dx%d" % i) if i in reduce_in_dx else (None, None)
                return plan

            dh, mix_grads[i], got, got_dx = _ssd_bwd(dh, sv, gp, gq, *ssd_args(j), full("ssd_w_out", j), consts,
                                                     in_scan, tag="", comm_dx=in_dx)
            then["scan"](got)
            if then["dx"] is not None:
                then["dx"](got_dx)
        else:
            w_in, scw, w_out = sc_args(j)
            dh, mix_grads[i] = _sc_bwd(dh, sv, gp, gq, w_in, scw, w_out, tag="")
    grad_x = dh[None]
    big_grads = dict(zip([n for n, _ in BIG], _swap_halves([sums[n] for n, _ in BIG])))
    ssd_l = [mix_grads[i] for i in range(0, depth, 2)]
    sc_l = [mix_grads[i] for i in range(1, depth, 2)]
    stack = lambda layers, k: jnp.stack([g[k] for g in layers])

    small_names = ["mix_pre_g", "mix_post_g", "ffn_pre_g", "ffn_post_g", "ssd_conv_w", "ssd_conv_b", "ssd_dt_bias",
                   "ssd_A_log", "ssd_D", "ssd_norm_w", "sc_conv_w", "ffn_conv_w", "ffn_conv_b"]
    small_local = {
        "mix_pre_g": jnp.concatenate([g["g_pre"] for g in mix_grads]),
        "mix_post_g": jnp.concatenate([g["g_post"] for g in mix_grads]),
        "ffn_pre_g": jnp.concatenate([g["g_pre"] for g in ffn_grads]),
        "ffn_post_g": jnp.concatenate([g["g_post"] for g in ffn_grads]),
        "ssd_conv_w": stack(ssd_l, "cw"), "ssd_conv_b": stack(ssd_l, "cb"), "ssd_dt_bias": stack(ssd_l, "dt_bias"),
        "ssd_A_log": stack(ssd_l, "a_log"), "ssd_D": stack(ssd_l, "d"), "ssd_norm_w": stack(ssd_l, "nw"),
        "sc_conv_w": stack(sc_l, "cw"), "ffn_conv_w": stack(ffn_grads, "cw"), "ffn_conv_b": stack(ffn_grads, "cb"),
    }
    small_full_shapes = [small_local[n].shape for n in small_names]
    spack = _pack_rows([small_local[n].reshape(-1) for n in small_names] + [loss_part.reshape(-1)], LANE, 8)
    stotal = _sum_slots(_allgather_small(spack), "sum_small").reshape(-1)
    small_grads = dict(zip(small_names, _unpack(stotal, small_full_shapes)))
    loss = stotal[sum(int(np.prod(s)) for s in small_full_shapes)]
    for n in conv_names:
        width = wts[n].shape[-1]
        small_grads[n] = lax.dynamic_slice_in_dim(small_grads[n], q_chip * width, width, axis=2)

    grads, delta, new_m, new_v = {}, {}, {}, {}
    for n, _ in BIG:
        s = wts[n].shape
        two_d = lambda a: a.reshape(-1, s[-1])
        grads[n] = big_grads[n]
        d, mn, vn = _adamw(two_d(wts[n]), two_d(mom[n]), two_d(var[n]), two_d(grads[n]), "adamw_" + n)
        delta[n], new_m[n], new_v[n] = d.reshape(s), mn.reshape(s), vn.reshape(s)
    small_shapes = [wts[n].shape for n in small_names]
    pk = lambda d: _pack_rows([d[n].reshape(-1) for n in small_names], LANE, 8)
    for n in small_names:
        grads[n] = small_grads[n].reshape(wts[n].shape)
    d, mn, vn = _adamw(pk(wts), pk(mom), pk(var), pk(grads), "adamw_small")
    for out, packed in ((delta, d), (new_m, mn), (new_v, vn)):
        out.update(zip(small_names, _unpack(packed.reshape(-1), small_shapes)))

    return (loss, grad_x, *[grads[n] for n in names], *[delta[n] for n in names], *[new_m[n] for n in names],
            *[new_v[n] for n in names])
```

```python
from typing import Callable, NamedTuple

import jax
import jax.numpy as jnp
import numpy as np
from jax import lax
from jax.experimental import pallas as pl
from jax.experimental.pallas import tpu as pltpu

F32 = jnp.float32
BF16 = jnp.bfloat16
SDS = jax.ShapeDtypeStruct
MESH = pl.DeviceIdType.MESH
ANY = pl.BlockSpec(memory_space=pl.ANY)

EPS = 1e-6
CHUNK = 64
HEAD_DIM = 64
N_GROUPS = 8
D_STATE = 128
HEADS_PER_GROUP = 4
GROUP_W = HEADS_PER_GROUP * HEAD_DIM
LANE = 128
ROW_TILE = 128
HALO = 8
VMEM_LIMIT = 56 * 1024 * 1024

ADAM_LR = 0.001
ADAM_B1 = 0.9
ADAM_B2 = 0.999
ADAM_EPS = 1e-08
ADAM_WD = 0.01
ADAM_STEP = 10

NT = (((1,), (1,)), ((), ()))
TN = (((0,), (0,)), ((), ()))


def _cp(*sem):
    return pltpu.CompilerParams(dimension_semantics=sem or None, vmem_limit_bytes=VMEM_LIMIT)


def _sigmoid(x):
    return 1.0 / (1.0 + jnp.exp(-x))


def _sigmoid_fast(x):
    return pl.reciprocal(1.0 + jnp.exp(-x), approx=True)


def _dsilu(x, s):
    return s * (1.0 + x * (1.0 - s))


def _rsq(x):
    return lax.rsqrt(jnp.mean(x * x, axis=-1, keepdims=True) + EPS)


MM_ROWS = 256


class _Epilogue(NamedTuple):
    ins: list
    cols: int
    dtype: object
    fn: Callable


def _norm_matmul(x, g, w, out_dtype, name, comm=None, epilogue=None):
    L, D = x.shape
    N = w.shape[1]
    tm = min(MM_ROWS, L)
    n_extra = 0 if epilogue is None else len(epilogue.ins)

    def body(x_ref, g_ref, w_ref, *refs):
        extra, (o_ref, hn_ref), rest = refs[:n_extra], refs[n_extra:n_extra + 2], refs[n_extra + 2:]
        xv = x_ref[...]
        hn = (xv * _rsq(xv) * g_ref[...]).astype(BF16)
        hn_ref[...] = hn
        o_ref[...] = jnp.dot(hn, w_ref[...], preferred_element_type=F32).astype(out_dtype)
        if epilogue is not None:
            res_ref, tail_ref = rest

            @pl.when(pl.program_id(0) == 0)
            def _():
                tail_ref[...] = jnp.zeros_like(tail_ref)

            epilogue.fn(o_ref, extra, res_ref, tail_ref, tm)

    row = lambda i: (i, 0)
    fix = lambda i: (0, 0)
    in_specs = [pl.BlockSpec((tm, D), row), pl.BlockSpec((1, D), fix), pl.BlockSpec((D, N), fix)]
    out_specs = [pl.BlockSpec((tm, N), row), pl.BlockSpec((tm, D), row)]
    out_shape = [SDS((L, N), out_dtype), SDS((L, D), BF16)]
    scratch, args = [], (x, g, w)
    if epilogue is not None:
        in_specs += [pl.BlockSpec(a.shape, fix) for a in epilogue.ins]
        out_specs.append(pl.BlockSpec((tm, epilogue.cols), row))
        out_shape.append(SDS((L, epilogue.cols), epilogue.dtype))
        scratch = [pltpu.VMEM((HALO, epilogue.cols), F32)]
        args += tuple(epilogue.ins)
    return _carrier_call(body, comm, L // tm, in_specs=in_specs, out_specs=out_specs, out_shape=out_shape,
                         scratch_shapes=scratch, name=name, args=args)


def _conv_taps(x, tail, taps, tm):
    ext = jnp.concatenate([tail, x], axis=0)
    out = x * taps[0]
    for k in range(1, len(taps)):
        out = out + pltpu.roll(ext, k, axis=0)[HALO:HALO + tm] * taps[k]
    return out


def _ssd_conv_epilogue(cw, cb, col0):
    C = cw.shape[1]

    def fn(o_ref, ins, res_ref, tail_ref, tm):
        cw_ref, cb_ref = ins
        for s in range(C // LANE):
            sl = slice(s * LANE, (s + 1) * LANE)
            xv = o_ref[:, col0 + s * LANE:col0 + (s + 1) * LANE].astype(F32)
            cv = _conv_taps(xv, tail_ref[:, sl], [cw_ref[3 - k:4 - k, sl] for k in range(4)], tm) + cb_ref[:, sl]
            res_ref[:, sl] = cv * _sigmoid(cv)
            tail_ref[:, sl] = xv[tm - HALO:tm]

    return _Epilogue([cw, cb], C, F32, fn)


def _sc_gate_epilogue(cw):
    C = cw.shape[1]

    def fn(o_ref, ins, res_ref, tail_ref, tm):
        (cw_ref,) = ins
        for s in range(C // LANE):
            sl = slice(s * LANE, (s + 1) * LANE)
            gb, gc, v = (o_ref[:, (3 * s + k) * LANE:(3 * s + k + 1) * LANE].astype(F32) for k in range(3))
            p = gc * v
            u = _conv_taps(p, tail_ref[:, sl], [cw_ref[2 - k:3 - k, sl] for k in range(3)], tm)
            res_ref[:, sl] = (gb * u).astype(BF16)
            tail_ref[:, sl] = p[tm - HALO:tm]

    return _Epilogue([cw], C, BF16, fn)


def _matmul_norm_res(a, w, x, g, name, comm=None):
    L, K = a.shape
    D = w.shape[1]
    tm = min(MM_ROWS, L)

    def body(a_ref, w_ref, x_ref, g_ref, m_ref, xo_ref):
        m = jnp.dot(a_ref[...], w_ref[...], preferred_element_type=F32)
        m_ref[...] = m
        xo_ref[...] = x_ref[...] + m * _rsq(m) * g_ref[...]

    row = lambda i: (i, 0)
    fix = lambda i: (0, 0)
    return _carrier_call(
        body, comm, L // tm,
        in_specs=[pl.BlockSpec((tm, K), row), pl.BlockSpec((K, D), fix), pl.BlockSpec((tm, D), row),
                  pl.BlockSpec((1, D), fix)],
        out_specs=[pl.BlockSpec((tm, D), row), pl.BlockSpec((tm, D), row)],
        out_shape=[SDS((L, D), F32), SDS((L, D), F32)], scratch_shapes=[], name=name, args=(a, w, x, g))


def _postnorm_bwd_matmul_nt(dx, m, g, w, out_dtype, name):
    L, D = dx.shape
    K = w.shape[0]
    tm = min(MM_ROWS, L)

    def body(dx_ref, m_ref, g_ref, w_ref, da_ref, dm_ref, dg_ref):
        @pl.when(pl.program_id(0) == 0)
        def _():
            dg_ref[...] = jnp.zeros_like(dg_ref)

        m = m_ref[...]
        dxv = dx_ref[...]
        r = _rsq(m)
        mh = m * r
        dg_ref[...] += jnp.sum(dxv * mh, axis=0, keepdims=True)
        dyg = dxv * g_ref[...]
        dm = (r * (dyg - mh * jnp.mean(dyg * mh, axis=-1, keepdims=True))).astype(BF16)
        dm_ref[...] = dm
        da_ref[...] = lax.dot_general(dm, w_ref[...], NT, preferred_element_type=F32).astype(out_dtype)

    row = lambda i: (i, 0)
    fix = lambda i: (0, 0)
    return pl.pallas_call(
        body, grid=(L // tm,),
        in_specs=[pl.BlockSpec((tm, D), row), pl.BlockSpec((tm, D), row), pl.BlockSpec((1, D), fix),
                  pl.BlockSpec((K, D), fix)],
        out_specs=[pl.BlockSpec((tm, K), row), pl.BlockSpec((tm, D), row), pl.BlockSpec((1, D), fix)],
        out_shape=[SDS((L, K), out_dtype), SDS((L, D), BF16), SDS((1, D), F32)],
        name=name, compiler_params=_cp("arbitrary"))(dx, m, g, w)


DW_ACC_BYTES = 13 * 512 * 1024


def _dw_tiles(ka, n):
    ta = ka if ka <= 1024 else ka // 2
    fits = [d for d in range(LANE, n + 1, LANE) if n % d == 0 and ta * d * 4 <= DW_ACC_BYTES]
    return ta, max(fits)


def _matmul_tn(a, b, name):
    L, Ka = a.shape
    N = b.shape[1]
    ta, tn = _dw_tiles(Ka, N)
    tl = min(512, L)
    n_l = L // tl

    def body(a_ref, b_ref, o_ref, acc_ref):
        l = pl.program_id(2)

        @pl.when(l == 0)
        def _():
            acc_ref[...] = jnp.zeros_like(acc_ref)

        acc_ref[...] += lax.dot_general(a_ref[...], b_ref[...], TN, preferred_element_type=F32)

        @pl.when(l == n_l - 1)
        def _():
            o_ref[...] = acc_ref[...].astype(BF16)

    return pl.pallas_call(
        body, grid=(Ka // ta, N // tn, n_l),
        in_specs=[pl.BlockSpec((tl, ta), lambda i, j, l: (l, i)), pl.BlockSpec((tl, tn), lambda i, j, l: (l, j))],
        out_specs=pl.BlockSpec((ta, tn), lambda i, j, l: (i, j)),
        out_shape=SDS((Ka, N), BF16),
        scratch_shapes=[pltpu.VMEM((ta, tn), F32)],
        name=name, compiler_params=_cp("parallel", "parallel", "arbitrary"))(a, b)


def _ffn_up(x, g, w4, cw, cb, name, comm=None):
    L, D = x.shape
    b = w4.shape[2]
    C = 2 * b
    tm = min(MM_ROWS, L)

    def body(x_ref, g_ref, w_ref, cw_ref, cb_ref, o_ref, hn_ref, a_ref, tail_ref):
        @pl.when(pl.program_id(0) == 0)
        def _():
            tail_ref[...] = jnp.zeros_like(tail_ref)

        xv = x_ref[...]
        hn = (xv * _rsq(xv) * g_ref[...]).astype(BF16)
        hn_ref[...] = hn
        for half in range(2):
            cols = slice(half * b, (half + 1) * b)
            for part in range(2):
                o_ref[part, :, cols] = jnp.dot(hn, w_ref[2 * part + half], preferred_element_type=F32).astype(BF16)
            for s in range(half * b // LANE, (half + 1) * b // LANE):
                sl = slice(s * LANE, (s + 1) * LANE)
                gp = o_ref[0, :, sl].astype(F32)
                ext = jnp.concatenate([tail_ref[:, sl], gp], axis=0)
                back = lambda k: pltpu.roll(ext, k, axis=0)[HALO:HALO + tm]
                gate = gp * cw_ref[2:3, sl] + back(1) * cw_ref[1:2, sl] + back(2) * cw_ref[0:1, sl] + cb_ref[:, sl]
                a_ref[:, sl] = (gate * _sigmoid(gate) * o_ref[1, :, sl].astype(F32)).astype(BF16)
                tail_ref[:, sl] = gp[tm - HALO:tm]

    row = lambda i: (i, 0)
    fix = lambda i: (0, 0)
    return _carrier_call(
        body, comm, L // tm,
        in_specs=[pl.BlockSpec((tm, D), row), pl.BlockSpec((1, D), fix), pl.BlockSpec((4, D, b), lambda i: (0, 0, 0)),
                  pl.BlockSpec(cw.shape, fix), pl.BlockSpec(cb.shape, fix)],
        out_specs=[pl.BlockSpec((2, tm, C), lambda i: (0, i, 0)), pl.BlockSpec((tm, D), row), pl.BlockSpec((tm, C), row)],
        out_shape=[SDS((2, L, C), BF16), SDS((L, D), BF16), SDS((L, C), BF16)],
        scratch_shapes=[pltpu.VMEM((HALO, C), F32)], name=name, args=(x, g, w4, cw, cb))


def _ffn_up_dx(dup, w4, x, g, dres, name):
    _, L, _ = dup.shape
    _, D, b = w4.shape
    tm = min(MM_ROWS, L)

    def body(dy_ref, w_ref, x_ref, g_ref, dres_ref, dx_ref, dg_ref):
        @pl.when(pl.program_id(0) == 0)
        def _():
            dg_ref[...] = jnp.zeros_like(dg_ref)

        dh = jnp.zeros((tm, D), F32)
        for q in range(4):
            dh = dh + lax.dot_general(dy_ref[q // 2, :, (q % 2) * b:(q % 2 + 1) * b], w_ref[q], NT,
                                      preferred_element_type=F32)
        xv = x_ref[...]
        r = _rsq(xv)
        xh = xv * r
        dg_ref[...] += jnp.sum(dh * xh, axis=0, keepdims=True)
        dyg = dh * g_ref[...]
        dx_ref[...] = dres_ref[...] + r * (dyg - xh * jnp.mean(dyg * xh, axis=-1, keepdims=True))

    row = lambda i: (i, 0)
    fix = lambda i: (0, 0)
    return pl.pallas_call(
        body, grid=(L // tm,),
        in_specs=[pl.BlockSpec((2, tm, 2 * b), lambda i: (0, i, 0)), pl.BlockSpec((4, D, b), lambda i: (0, 0, 0)),
                  pl.BlockSpec((tm, D), row), pl.BlockSpec((1, D), fix), pl.BlockSpec((tm, D), row)],
        out_specs=[pl.BlockSpec((tm, D), row), pl.BlockSpec((1, D), fix)],
        out_shape=[SDS((L, D), F32), SDS((1, D), F32)],
        name=name, compiler_params=_cp("arbitrary"))(dup, w4, x, g, dres)


def _ffn_up_dw(hn, dup, name):
    L, D = hn.shape
    b = dup.shape[2] // 2
    tl = min(512, L)
    n_l = L // tl

    def body(a_ref, b_ref, o_ref, acc_ref):
        l = pl.program_id(1)

        @pl.when(l == 0)
        def _():
            acc_ref[...] = jnp.zeros_like(acc_ref)

        acc_ref[...] += lax.dot_general(a_ref[...], b_ref[...], TN, preferred_element_type=F32)

        @pl.when(l == n_l - 1)
        def _():
            o_ref[...] = acc_ref[...].astype(BF16)

    return pl.pallas_call(
        body, grid=(4, n_l),
        in_specs=[pl.BlockSpec((tl, D), lambda q, l: (l, 0)),
                  pl.BlockSpec((None, tl, b), lambda q, l: (q // 2, l, q % 2))],
        out_specs=pl.BlockSpec((None, D, b), lambda q, l: (q, 0, 0)),
        out_shape=SDS((4, D, b), BF16),
        scratch_shapes=[pltpu.VMEM((D, b), F32)],
        name=name, compiler_params=_cp("parallel", "arbitrary"))(hn, dup)


def _matmul_nt_prenorm_bwd(dy, w, x, g, dres, name, comm=None):
    L, N = dy.shape
    D = w.shape[0]
    tm = min(MM_ROWS, L)

    def body(dy_ref, w_ref, x_ref, g_ref, dres_ref, dx_ref, dg_ref):
        @pl.when(pl.program_id(0) == 0)
        def _():
            dg_ref[...] = jnp.zeros_like(dg_ref)

        dh = lax.dot_general(dy_ref[...], w_ref[...], NT, preferred_element_type=F32)
        xv = x_ref[...]
        r = _rsq(xv)
        xh = xv * r
        dg_ref[...] += jnp.sum(dh * xh, axis=0, keepdims=True)
        dyg = dh * g_ref[...]
        dx_ref[...] = dres_ref[...] + r * (dyg - xh * jnp.mean(dyg * xh, axis=-1, keepdims=True))

    row = lambda i: (i, 0)
    fix = lambda i: (0, 0)
    return _carrier_call(
        body, comm, L // tm,
        in_specs=[pl.BlockSpec((tm, N), row), pl.BlockSpec((D, N), fix), pl.BlockSpec((tm, D), row),
                  pl.BlockSpec((1, D), fix), pl.BlockSpec((tm, D), row)],
        out_specs=[pl.BlockSpec((tm, D), row), pl.BlockSpec((1, D), fix)],
        out_shape=[SDS((L, D), F32), SDS((1, D), F32)], scratch_shapes=[], name=name, args=(dy, w, x, g, dres))


def _loss_head(y, t, name):
    L, D = y.shape
    tm = min(512, L)

    def body(y_ref, t_ref, dy_ref, loss_ref):
        @pl.when(pl.program_id(0) == 0)
        def _():
            loss_ref[...] = jnp.zeros_like(loss_ref)

        e = y_ref[...] - t_ref[...]
        dy_ref[...] = e * (1.0 / D)
        s = jnp.sum(jnp.sum(e * e, axis=1, keepdims=True), axis=0, keepdims=True)
        loss_ref[...] += s * (0.5 / D)

    row = lambda i: (i, 0)
    return pl.pallas_call(
        body, grid=(L // tm,),
        in_specs=[pl.BlockSpec((tm, D), row), pl.BlockSpec((tm, D), row)],
        out_specs=[pl.BlockSpec((tm, D), row), pl.BlockSpec((1, 1), lambda i: (0, 0))],
        out_shape=[SDS((L, D), F32), SDS((1, 1), F32)],
        name=name, compiler_params=_cp("arbitrary"))(y, t)


def _tile_rows(ref):
    return HALO * (4 // jnp.dtype(ref.dtype).itemsize)


def _prev_rows(ref, r0, i, cols):
    n = _tile_rows(ref)
    p0 = pl.multiple_of(jnp.maximum(r0 - n, 0), n)
    return jnp.where(i > 0, ref[pl.ds(p0, n), cols].astype(F32)[n - HALO:], 0.0)


def _next_rows(ref, r0, i, n_tiles, cols):
    n = _tile_rows(ref)
    n0 = pl.multiple_of(jnp.minimum(r0 + ROW_TILE, n_tiles * ROW_TILE - n), n)
    return jnp.where(i < n_tiles - 1, ref[pl.ds(n0, n), cols].astype(F32)[:HALO], 0.0)


def _rows_f32(ref, rows, cols):
    return ref[rows, cols].astype(F32)


def _back(ext, s):
    return pltpu.roll(ext, s, axis=0)[HALO:HALO + ROW_TILE]


def _fwd(ext, s):
    n = ext.shape[0]
    return pltpu.roll(ext, n - s, axis=0)[:ROW_TILE]


def _store_rows(ref, rows):
    ref[...] = jnp.zeros_like(ref)
    for k, v in enumerate(rows):
        ref[k:k + 1, :] = v


def _strip_call(body, L, n_strips, ins, outs, name):
    def spec(rows, width, off):
        if off is None:
            return pl.BlockSpec((rows, width), lambda j: (0, 0))
        return pl.BlockSpec((rows, width), lambda j: (0, j + off))

    return pl.pallas_call(
        body, grid=(n_strips,),
        in_specs=[spec(a.shape[0], w, off) for a, w, off in ins],
        out_specs=[spec(s.shape[0], w, off) for s, w, off in outs],
        out_shape=[s for s, _, _ in outs],
        name=name, compiler_params=_cp("parallel"))(*[a for a, _, _ in ins])


def _sc_mid_bwd(dq, bcv, cw, name):
    L, C = dq.shape
    n_tiles = L // ROW_TILE
    s0, s1, s2, al = slice(0, LANE), slice(LANE, 2 * LANE), slice(2 * LANE, 3 * LANE), slice(None)

    def body(dq_ref, x_ref, cw_ref, dx_ref, st_ref):
        w0, w1, w2 = cw_ref[0:1, :], cw_ref[1:2, :], cw_ref[2:3, :]

        def step(i, c):
            r0 = pl.multiple_of(i * ROW_TILE, ROW_TILE)
            rows = pl.ds(r0, ROW_TILE)
            gb, gc, v = _rows_f32(x_ref, rows, s0), _rows_f32(x_ref, rows, s1), _rows_f32(x_ref, rows, s2)
            dq_v = _rows_f32(dq_ref, rows, al)
            p = gc * v
            pext = jnp.concatenate([_prev_rows(x_ref, r0, i, s1) * _prev_rows(x_ref, r0, i, s2), p], axis=0)
            p1, p2 = _back(pext, 1), _back(pext, 2)
            u = p * w2 + p1 * w1 + p2 * w0
            du = dq_v * gb
            du_n = _next_rows(dq_ref, r0, i, n_tiles, al) * _next_rows(x_ref, r0, i, n_tiles, s0)
            ext = jnp.concatenate([du, du_n], axis=0)
            dp = du * w2 + _fwd(ext, 1) * w1 + _fwd(ext, 2) * w0
            dx_ref[rows, s0] = (dq_v * u).astype(BF16)
            dx_ref[rows, s1] = (dp * v).astype(BF16)
            dx_ref[rows, s2] = (dp * gc).astype(BF16)
            s = lambda t: jnp.sum(t, axis=0, keepdims=True)
            return (c[0] + s(du * p2), c[1] + s(du * p1), c[2] + s(du * p))

        z = jnp.zeros((1, LANE), F32)
        _store_rows(st_ref, lax.fori_loop(0, n_tiles, step, (z, z, z)))

    return _strip_call(body, L, C // LANE, [(dq, LANE, 0), (bcv, 3 * LANE, 0), (cw, LANE, 0)],
                       [(SDS((L, 3 * C), BF16), 3 * LANE, 0), (SDS((8, C), F32), LANE, 0)], name)


def _ssd_conv_bwd(dxbc, zx, cw, cb, col0, dzx, name):
    L, C = dxbc.shape
    n_tiles = L // ROW_TILE
    al = slice(None)

    def body(d_ref, x_ref, cw_ref, cb_ref, dzx_in_ref, o_ref, st_ref):
        w0, w1, w2, w3 = cw_ref[0:1, :], cw_ref[1:2, :], cw_ref[2:3, :], cw_ref[3:4, :]
        b = cb_ref[...]

        def step(i, c):
            r0 = pl.multiple_of(i * ROW_TILE, ROW_TILE)
            rows = pl.ds(r0, ROW_TILE)
            xv = x_ref[rows, :]
            xe = jnp.concatenate([_prev_rows(x_ref, r0, i, al), xv, _next_rows(x_ref, r0, i, n_tiles, al)], axis=0)
            x1, x2, x3 = pltpu.roll(xe, 1, axis=0), pltpu.roll(xe, 2, axis=0), pltpu.roll(xe, 3, axis=0)
            cv = (xe * w3 + x1 * w2 + x2 * w1 + x3 * w0 + b)[HALO:]
            de = jnp.concatenate([d_ref[rows, :], _next_rows(d_ref, r0, i, n_tiles, al)], axis=0)
            dc_ext = de * _dsilu(cv, _sigmoid_fast(cv))
            dc = dc_ext[:ROW_TILE]
            o_ref[rows, :] = (dc * w3 + _fwd(dc_ext, 1) * w2 + _fwd(dc_ext, 2) * w1 + _fwd(dc_ext, 3) * w0).astype(BF16)
            s = lambda t: jnp.sum(t, axis=0, keepdims=True)
            t = slice(HALO, HALO + ROW_TILE)
            return (c[0] + s(dc * x3[t]), c[1] + s(dc * x2[t]), c[2] + s(dc * x1[t]), c[3] + s(dc * xv), c[4] + s(dc))

        z = jnp.zeros((1, LANE), F32)
        _store_rows(st_ref, lax.fori_loop(0, n_tiles, step, (z, z, z, z, z)))

    strip = lambda rows, off=0: pl.BlockSpec((rows, LANE), lambda j: (0, j + off))
    shifted = strip(L, col0 // LANE)
    return pl.pallas_call(
        body, grid=(C // LANE,), in_specs=[strip(L), shifted, strip(cw.shape[0]), strip(1), ANY],
        out_specs=[shifted, strip(8)], out_shape=[SDS(dzx.shape, dzx.dtype), SDS((8, C), F32)],
        input_output_aliases={4: 0}, name=name, compiler_params=_cp("parallel"))(dxbc, zx, cw, cb, dzx)


def _scan_constants(n_heads):
    hw = n_heads * HEAD_DIM
    col = np.arange(hw)
    ind = (col[None, :] // HEAD_DIM == np.arange(LANE)[:, None]).astype(np.float32)
    gcol = np.arange(GROUP_W)
    itile = (gcol[None, :] % CHUNK == np.arange(CHUNK)[:, None]).astype(np.float32)
    trit = (gcol[None, :] % CHUNK <= np.arange(CHUNK)[:, None]).astype(np.float32)
    tril = np.tril(np.ones((CHUNK, CHUNK), np.float32))
    bmask = (gcol[:, None] // HEAD_DIM == gcol[None, :] // HEAD_DIM).astype(np.float32)
    return (jnp.asarray(ind.T.copy(), BF16), jnp.asarray(itile), jnp.asarray(trit), jnp.asarray(tril, BF16),
            jnp.asarray(bmask))


def _softplus(x):
    return jnp.maximum(x, 0.0) + jnp.log(1.0 + jnp.exp(-jnp.abs(x)))


def _split3(x):
    hi = x.astype(BF16)
    r1 = x - hi.astype(F32)
    mid = r1.astype(BF16)
    return hi, mid, (r1 - mid.astype(F32)).astype(BF16)


def _dot_sel(x, sel, dims=None):
    if dims is None:
        mm = lambda p: jnp.dot(p, sel, preferred_element_type=F32)
    else:
        mm = lambda p: lax.dot_general(sel, p, dims, preferred_element_type=F32)
    hi, mid, lo = _split3(x)
    return (mm(lo) + mm(mid)) + mm(hi)


SEL_X = (((1,), (0,)), ((), ()))


def _head_lanes(v, g):
    h0 = HEADS_PER_GROUP * g
    return jnp.concatenate([jnp.broadcast_to(v[:, h0 + r:h0 + r + 1], (v.shape[0], HEAD_DIM))
                            for r in range(HEADS_PER_GROUP)], axis=1)


def _group_terms(g, dt, cs, cst, xbc_ref, trit, bmask, d_inner):
    gl = slice(g * GROUP_W, (g + 1) * GROUP_W)
    h0 = HEADS_PER_GROUP * g
    csl = _head_lanes(cs, g)
    dtx = _head_lanes(dt, g)
    rr = jnp.concatenate([cst[h0 + r:h0 + r + 1, :] for r in range(HEADS_PER_GROUP)], axis=1)
    lm = jnp.exp(jnp.where(trit > 0.0, csl - rr, -jnp.inf))
    xs = xbc_ref[:, gl]
    b = xbc_ref[:, d_inner + g * D_STATE: d_inner + (g + 1) * D_STATE]
    c = xbc_ref[:, d_inner + (N_GROUPS + g) * D_STATE: d_inner + (N_GROUPS + g + 1) * D_STATE]
    u = xs * dtx
    bb, cb = b.astype(BF16), c.astype(BF16)
    btile = jnp.concatenate([bb] * HEADS_PER_GROUP, axis=0)
    cbt = lax.dot_general(cb, btile, NT, preferred_element_type=F32)
    m = cbt * lm
    ub = u.astype(BF16)
    bdu = jnp.where(bmask > 0.0, jnp.concatenate([ub] * HEADS_PER_GROUP, axis=0), jnp.zeros((), BF16))
    c_last = csl[CHUNK - 1:CHUNK, :]
    return dict(gl=gl, csl=csl, dtx=dtx, lm=lm, xs=xs, bb=bb, cb=cb, u=u, btile=btile, m=m, bdu=bdu,
                e=jnp.exp(csl), dec=jnp.exp(c_last - csl), e_last=jnp.exp(c_last))


def _ssd_scan_fwd(zx, xbc, par, dexp, nw, consts, comm, name):
    L = xbc.shape[0]
    d_inner = dexp.shape[1]
    n_chunks = L // CHUNK
    dt_blk = zx.shape[1] // LANE - 1
    ind_t, itile_c, trit_c, tril_c, bmask_c = consts

    def body(xbc_ref, z_ref, dtr_ref, par_ref, dexp_ref, nw_ref, trit_ref, tril_ref, bmask_ref,
             yn_ref, yf_ref, st_out_ref, st_ref):
        @pl.when(pl.program_id(0) == 0)
        def _():
            st_ref[...] = jnp.zeros_like(st_ref)

        dt = _softplus(dtr_ref[...] + par_ref[0:1, :])
        a_head = -jnp.exp(par_ref[1:2, :])
        cs = _dot_sel(dt * a_head, tril_ref[...], SEL_X)
        cst = cs.T
        trit, bmask = trit_ref[...], bmask_ref[...]
        for g in range(N_GROUPS):
            t = _group_terms(g, dt, cs, cst, xbc_ref, trit, bmask, d_inner)
            p = st_ref[g]
            st_out_ref[0, g] = p
            y = jnp.dot(t["m"].astype(BF16), t["bdu"], preferred_element_type=F32)
            y = y + jnp.dot(t["cb"], p.astype(BF16), preferred_element_type=F32) * t["e"]
            st_new = lax.dot_general(t["bb"], (t["u"] * t["dec"]).astype(BF16), TN, preferred_element_type=F32)
            st_ref[g] = p * t["e_last"] + st_new
            yf_ref[:, t["gl"]] = y + t["xs"] * dexp_ref[:, t["gl"]]
        z = z_ref[...]
        y2 = yf_ref[...] * (z * _sigmoid(z))
        yn_ref[...] = (y2 * _rsq(y2) * nw_ref[...]).astype(BF16)

    row = lambda c: (c, 0)
    fix = lambda c: (0, 0)
    cspec = lambda a: pl.BlockSpec(a.shape, fix)
    return _carrier_call(
        body, comm, n_chunks,
        in_specs=[pl.BlockSpec((CHUNK, xbc.shape[1]), row), pl.BlockSpec((CHUNK, d_inner), row),
                  pl.BlockSpec((CHUNK, LANE), lambda c: (c, dt_blk)), cspec(par), cspec(dexp), cspec(nw),
                  cspec(trit_c), cspec(tril_c), cspec(bmask_c)],
        out_specs=[pl.BlockSpec((CHUNK, d_inner), row), pl.BlockSpec((CHUNK, d_inner), row),
                   pl.BlockSpec((1, N_GROUPS, D_STATE, GROUP_W), lambda c: (c, 0, 0, 0))],
        out_shape=[SDS((L, d_inner), BF16), SDS((L, d_inner), F32),
                   SDS((n_chunks, N_GROUPS, D_STATE, GROUP_W), F32)],
        scratch_shapes=[pltpu.VMEM((N_GROUPS, D_STATE, GROUP_W), F32)],
        name=name, args=(xbc, zx, zx, par, dexp, nw, trit_c, tril_c, bmask_c))


def _ssd_scan_bwd(dyn, yf, zx, xbc, states, par, dexp, nw, consts, comm, name):
    L = xbc.shape[0]
    d_inner = dexp.shape[1]
    n_chunks = L // CHUNK
    nz = zx.shape[1]
    dt_blk = nz // LANE - 1
    ind_t, itile_c, trit_c, tril_c, bmask_c = consts
    hslices = [slice(r * HEAD_DIM, (r + 1) * HEAD_DIM) for r in range(HEADS_PER_GROUP)]

    def body(dyn_ref, yf_ref, z_ref, dtr_ref, xbc_ref, st_in_ref, par_ref, dexp_ref, nw_ref, indt_ref,
             itile_ref, trit_ref, tril_ref, bmask_ref,
             dzx_ref, dxbc_ref, dnw_ref, dpar_ref, dq_ref, dyf_ref):
        @pl.when(pl.program_id(0) == 0)
        def _():
            dq_ref[...] = jnp.zeros_like(dq_ref)
            dnw_ref[...] = jnp.zeros_like(dnw_ref)
            dpar_ref[...] = jnp.zeros_like(dpar_ref)

        lanes = [slice(g * GROUP_W, (g + 1) * GROUP_W) for g in range(N_GROUPS)]
        ssq = jnp.zeros((CHUNK, 1), F32)
        sdy = jnp.zeros((CHUNK, 1), F32)
        for gl in lanes:
            z = z_ref[:, gl]
            y2 = yf_ref[:, gl] * (z * _sigmoid(z))
            ssq = ssq + jnp.sum(y2 * y2, axis=-1, keepdims=True)
            sdy = sdy + jnp.sum(dyn_ref[:, gl] * nw_ref[:, gl] * y2, axis=-1, keepdims=True)
        r = lax.rsqrt(ssq * (1.0 / d_inner) + EPS)
        mdy = sdy * r * (1.0 / d_inner)
        for gl in lanes:
            z, yfv, dynv = z_ref[:, gl], yf_ref[:, gl], dyn_ref[:, gl]
            sz = _sigmoid(z)
            y2h = yfv * (z * sz) * r
            dnw_ref[:, gl] += jnp.sum(dynv * y2h, axis=0, keepdims=True)
            dy2 = r * (dynv * nw_ref[:, gl] - y2h * mdy)
            dzx_ref[:, gl] = (dy2 * yfv * _dsilu(z, sz)).astype(BF16)
            dyf_ref[:, gl] = dy2 * (z * sz)

        pre = dtr_ref[...] + par_ref[0:1, :]
        dt = _softplus(pre)
        a_head = -jnp.exp(par_ref[1:2, :])
        cs = _dot_sel(dt * a_head, tril_ref[...], SEL_X)
        cst = cs.T
        itile, trit, bmask = itile_ref[...], trit_ref[...], bmask_ref[...]
        dcs = jnp.zeros((CHUNK, LANE), F32)
        dcs_last = jnp.zeros((1, LANE), F32)
        ddt_u = jnp.zeros((CHUNK, LANE), F32)
        d_skip = jnp.zeros((1, LANE), F32)
        rsum = lambda v: jnp.sum(v, axis=0, keepdims=True)
        row8 = lax.broadcasted_iota(jnp.int32, (8, GROUP_W), 0)
        for g in range(N_GROUPS):
            t = _group_terms(g, dt, cs, cst, xbc_ref, trit, bmask, d_inner)
            gl, m, lm, u, bb, cb, e, dec, xs = (t[k] for k in ("gl", "m", "lm", "u", "bb", "cb", "e", "dec", "xs"))
            indt = indt_ref[gl, :]
            dy = dyf_ref[:, gl]
            dyb = dy.astype(BF16)
            p = st_in_ref[0, g]
            pb = p.astype(BF16)
            q = dq_ref[g]
            qb = q.astype(BF16)
            big = lax.dot_general(m.astype(BF16), dyb, TN, preferred_element_type=F32)
            du = jnp.zeros((CHUNK, GROUP_W), F32)
            for rh in range(HEADS_PER_GROUP):
                du = du + big[hslices[rh], :] * bmask[rh * HEAD_DIM:rh * HEAD_DIM + 1, :]
            dm = lax.dot_general(dyb, t["bdu"], NT, preferred_element_type=F32)
            w = dm * m
            dgt = (dm * lm).astype(BF16)
            dc = jnp.dot(dgt, t["btile"], preferred_element_type=F32)
            db_big = lax.dot_general(dgt, cb, TN, preferred_element_type=F32)
            db = db_big[hslices[0], :] + db_big[hslices[1], :] + db_big[hslices[2], :] + db_big[hslices[3], :]
            cp = jnp.dot(cb, pb, preferred_element_type=F32)
            dye = dy * e
            dyeb = dye.astype(BF16)
            dc = dc + lax.dot_general(dyeb, pb, NT, preferred_element_type=F32)
            dp = lax.dot_general(cb, dyeb, TN, preferred_element_type=F32)
            x2 = dye * cp
            bq = jnp.dot(bb, qb, preferred_element_type=F32)
            ud = u * dec
            du = du + bq * dec
            db = db + lax.dot_general(ud.astype(BF16), qb, NT, preferred_element_type=F32)
            x1 = bq * ud
            dq_ref[g] = dp + t["e_last"] * q
            x3 = rsum(q * p) * t["e_last"]
            red = _dot_sel(jnp.concatenate([w + x2 - x1, du * xs, itile * rsum(w)], axis=0), indt)
            dcs = dcs + red[0:CHUNK] - red[2 * CHUNK:3 * CHUNK]
            ddt_u = ddt_u + red[CHUNK:2 * CHUNK]
            tail = _dot_sel(jnp.where(row8 == 0, rsum(x1) + x3, jnp.where(row8 == 1, rsum(dy * xs), 0.0)), indt)
            dcs_last = dcs_last + tail[0:1]
            d_skip = d_skip + tail[1:2]
            dxbc_ref[:, gl] = du * t["dtx"] + dy * dexp_ref[:, gl]
            dxbc_ref[:, d_inner + g * D_STATE: d_inner + (g + 1) * D_STATE] = db
            dxbc_ref[:, d_inner + (N_GROUPS + g) * D_STATE: d_inner + (N_GROUPS + g + 1) * D_STATE] = dc
        last = lax.broadcasted_iota(jnp.int32, (CHUNK, LANE), 0) == CHUNK - 1
        dcs = dcs + jnp.where(last, dcs_last, 0.0)
        da = _dot_sel(dcs, tril_ref[...], TN)
        ddt = da * a_head + ddt_u
        heads = lax.broadcasted_iota(jnp.int32, (CHUNK, LANE), 1) < d_inner // HEAD_DIM
        ddt_raw = jnp.where(heads, ddt * _sigmoid(pre), 0.0)
        dzx_ref[:, nz - LANE:nz] = ddt_raw.astype(BF16)
        dpar_ref[0:1, :] += rsum(ddt_raw)
        dpar_ref[1:2, :] += rsum(da * dt) * a_head
        dpar_ref[2:3, :] += d_skip

    rev = lambda c: (n_chunks - 1 - c, 0)
    fix = lambda c: (0, 0)
    cspec = lambda a: pl.BlockSpec(a.shape, fix)
    nx = xbc.shape[1]
    return _carrier_call(
        body, comm, n_chunks,
        in_specs=[pl.BlockSpec((CHUNK, d_inner), rev), pl.BlockSpec((CHUNK, d_inner), rev),
                  pl.BlockSpec((CHUNK, d_inner), rev), pl.BlockSpec((CHUNK, LANE), lambda c: (n_chunks - 1 - c, dt_blk)),
                  pl.BlockSpec((CHUNK, nx), rev),
                  pl.BlockSpec((1, N_GROUPS, D_STATE, GROUP_W), lambda c: (n_chunks - 1 - c, 0, 0, 0)),
                  cspec(par), cspec(dexp), cspec(nw), cspec(ind_t), cspec(itile_c), cspec(trit_c),
                  cspec(tril_c), cspec(bmask_c)],
        out_specs=[pl.BlockSpec((CHUNK, nz), rev), pl.BlockSpec((CHUNK, nx), rev),
                   pl.BlockSpec((1, d_inner), fix), pl.BlockSpec((8, LANE), fix)],
        out_shape=[SDS((L, nz), BF16), SDS((L, nx), F32), SDS((1, d_inner), F32), SDS((8, LANE), F32)],
        scratch_shapes=[pltpu.VMEM((N_GROUPS, D_STATE, GROUP_W), F32), pltpu.VMEM((CHUNK, d_inner), F32)],
        name=name, args=(dyn, yf, zx, zx, xbc, states, par, dexp, nw, ind_t, itile_c, trit_c, tril_c, bmask_c))


def _adamw(w, m, v, g, name):
    R, C = w.shape
    tr = R
    for cand in (256, 128, 64, 32, 16, 8):
        if R % cand == 0:
            tr = cand
            break

    def body(w_ref, m_ref, v_ref, g_ref, d_ref, mo_ref, vo_ref):
        gv = g_ref[...]
        mn = ADAM_B1 * m_ref[...] + (1.0 - ADAM_B1) * gv
        vn = ADAM_B2 * v_ref[...] + (1.0 - ADAM_B2) * (gv * gv)
        m_hat = mn / (1.0 - ADAM_B1 ** ADAM_STEP)
        v_hat = vn / (1.0 - ADAM_B2 ** ADAM_STEP)
        d_ref[...] = -ADAM_LR * (m_hat / (jnp.sqrt(v_hat) + ADAM_EPS) + ADAM_WD * w_ref[...])
        mo_ref[...] = mn
        vo_ref[...] = vn

    blk = pl.BlockSpec((tr, C), lambda i: (i, 0))
    return pl.pallas_call(
        body, grid=(R // tr,), in_specs=[blk] * 4, out_specs=[blk] * 3, out_shape=[SDS((R, C), F32)] * 3,
        name=name, compiler_params=_cp("parallel"))(w, m, v, g)


def _sum_slots(parts, name):
    n, R, C = parts.shape
    tr = 128 if R % 128 == 0 else R

    def body(p_ref, o_ref):
        acc = p_ref[0]
        for k in range(1, n):
            acc = acc + p_ref[k]
        o_ref[...] = acc

    return pl.pallas_call(
        body, grid=(R // tr,), in_specs=[pl.BlockSpec((n, tr, C), lambda i: (0, i, 0))],
        out_specs=pl.BlockSpec((tr, C), lambda i: (i, 0)), out_shape=SDS((R, C), F32),
        name=name, compiler_params=_cp("parallel"))(parts)


def _add_core_halves(where, g, r, name):
    _, n, a, b = g.shape
    ta = a // 2

    def body(w_ref, g_ref, r_ref, o_ref):
        o_ref[...] = (g_ref[...].astype(F32) + r_ref[...].astype(F32)).astype(BF16)

    blk = lambda f: pl.BlockSpec((None, None, ta, b), f)
    mine = lambda s, l, w: (s, l, 0, 0)
    return pl.pallas_call(
        body, grid_spec=pltpu.PrefetchScalarGridSpec(
            num_scalar_prefetch=1, grid=(4, n),
            in_specs=[blk(lambda s, l, w: (s, l, w[1], 0)), blk(mine)], out_specs=blk(mine)),
        out_shape=SDS((4, n, ta, b), BF16), name=name,
        compiler_params=_cp("parallel", "parallel"))(where, g, r)


def _sum_shard(where, g, r, rr, into, layer, n_layers, name):
    _, _, a, b = g.shape
    ta = a // 2

    def body(w_ref, g_ref, r_ref, rr_ref, *refs):
        f = lambda v: v.astype(F32)
        refs[-1][...] = (((f(g_ref[...]) + f(r_ref[...])) + f(rr_ref[0])) + f(rr_ref[1])) + f(rr_ref[2])

    more = [] if into is None else [into]
    return pl.pallas_call(
        body, grid_spec=pltpu.PrefetchScalarGridSpec(
            num_scalar_prefetch=1, grid=(1,),
            in_specs=[pl.BlockSpec((None, None, ta, b), lambda l, w: (w[0], 0, w[1], 0)),
                      pl.BlockSpec((None, None, ta, b), lambda l, w: (w[0], 0, 0, 0)),
                      pl.BlockSpec((3, None, ta, b), lambda l, w: (0, 0, 0, 0))] + [ANY] * len(more),
            out_specs=pl.BlockSpec((None, ta, b), lambda l, w: (layer, w[1], 0))),
        out_shape=SDS((n_layers, a, b), F32), name=name, input_output_aliases={4: 0} if more else {},
        compiler_params=_cp("arbitrary"))(where, g, r, rr, *more)


def _me():
    return lax.axis_index("x"), lax.axis_index("y"), lax.axis_index("c")


def _chip_peers(x, y):
    return [(1 - x, y), (x, 1 - y), (1 - x, 1 - y)]


def _rcopy(src, dst, send_sems, recv_sems, k, to):
    return pltpu.make_async_remote_copy(src_ref=src, dst_ref=dst, send_sem=send_sems.at[k], recv_sem=recv_sems.at[k],
                                        device_id=to, device_id_type=MESH)


def _row_half(ref, c, lead=()):
    a = ref.shape[len(lead) + 1]
    return ref.at[(*lead, slice(None), pl.ds(c * (a // 2), a // 2))]


class _Comm(NamedTuple):
    ins: list
    out_shapes: list
    n_sems: int
    start: Callable
    finish: Callable


def _sem_scratch(comm):
    return [pltpu.SemaphoreType.DMA((comm.n_sems,)), pltpu.SemaphoreType.DMA((comm.n_sems,))]


def _run_comm(comm, name):
    n_in, n_out = len(comm.ins), len(comm.out_shapes)

    def body(*refs):
        ins, outs, sems = refs[:n_in], refs[n_in:n_in + n_out], refs[n_in + n_out:]
        comm.start(ins, outs, *sems)
        comm.finish(ins, outs, *sems)

    return pl.pallas_call(body, in_specs=[ANY] * n_in, out_specs=[ANY] * n_out, out_shape=comm.out_shapes,
                          scratch_shapes=_sem_scratch(comm), name=name)(*comm.ins)


def _carrier_call(compute, comm, n_steps, in_specs, out_specs, out_shape, scratch_shapes, name, args):
    if comm is None:
        return pl.pallas_call(compute, grid=(n_steps,), in_specs=in_specs, out_specs=out_specs, out_shape=out_shape,
                              scratch_shapes=scratch_shapes, name=name, compiler_params=_cp("arbitrary"))(*args), []
    n_in, n_out, n_scr = len(in_specs), len(out_specs), len(scratch_shapes)
    n_ci, n_co = len(comm.ins), len(comm.out_shapes)

    def body(*refs):
        ins, cins = refs[:n_in], refs[n_in:n_in + n_ci]
        o = n_in + n_ci
        outs, couts = refs[o:o + n_out], refs[o + n_out:o + n_out + n_co]
        s = o + n_out + n_co
        scratch, sems = refs[s:s + n_scr], refs[s + n_scr:]

        @pl.when(pl.program_id(0) == 0)
        def _():
            comm.start(cins, couts, *sems)

        compute(*ins, *outs, *scratch)

        @pl.when(pl.program_id(0) == n_steps - 1)
        def _():
            comm.finish(cins, couts, *sems)

    res = pl.pallas_call(
        body, grid=(n_steps,), in_specs=list(in_specs) + [ANY] * n_ci, out_specs=list(out_specs) + [ANY] * n_co,
        out_shape=list(out_shape) + list(comm.out_shapes), scratch_shapes=list(scratch_shapes) + _sem_scratch(comm),
        name=name, compiler_params=_cp("arbitrary"))(*args, *comm.ins)
    return res[:n_out], res[n_out:]


def _allgather_plan(mine, small=None):
    n = len(mine)
    ins = list(mine) + ([] if small is None else [small])
    out_shapes = [SDS((4,) + m.shape, BF16) for m in mine] + ([] if small is None else [SDS((4,) + small.shape, F32)])
    sem = lambda t, k: 7 * t + k

    def first_copies(ins_r, outs_r, send, recv):
        x, y, c = _me()
        q = 2 * x + y
        cps = []
        for j, chip in enumerate(_chip_peers(x, y)):
            for t in range(n):
                cps.append(_rcopy(_row_half(ins_r[t], c), _row_half(outs_r[t], c, (q,)), send, recv, sem(t, j),
                                  (*chip, c)))
            if small is not None:
                cps.append(_rcopy(ins_r[n], outs_r[n].at[q], send, recv, sem(n, j), (*chip, c)))
        for t in range(len(ins)):
            cps.append(_rcopy(ins_r[t], outs_r[t].at[q], send, recv, sem(t, 6), (x, y, 1 - c)))
        return cps

    def start(ins_r, outs_r, send, recv):
        for cp in first_copies(ins_r, outs_r, send, recv):
            cp.start()

    def finish(ins_r, outs_r, send, recv):
        x, y, c = _me()
        sib = (x, y, 1 - c)
        chips = _chip_peers(x, y)
        passed = []
        for j, (px, py) in enumerate(chips):
            for t in range(n):
                blk = _row_half(outs_r[t], c, (2 * px + py,))
                _rcopy(blk, blk, send, recv, sem(t, j), sib).wait_recv()
                cp = _rcopy(blk, blk, send, recv, sem(t, 3 + j), sib)
                cp.start()
                passed.append(cp)
        for j, (px, py) in enumerate(chips):
            for t in range(n):
                blk = _row_half(outs_r[t], 1 - c, (2 * px + py,))
                _rcopy(blk, blk, send, recv, sem(t, 3 + j), sib).wait_recv()
            if small is not None:
                sblk = outs_r[n].at[2 * px + py]
                _rcopy(sblk, sblk, send, recv, sem(n, j), sib).wait_recv()
        for t in range(len(ins)):
            own = outs_r[t].at[2 * x + y]
            _rcopy(own, own, send, recv, sem(t, 6), sib).wait_recv()
        for cp in first_copies(ins_r, outs_r, send, recv) + passed:
            cp.wait_send()

    return _Comm(ins, out_shapes, 7 * len(ins), start, finish)


def _grads_to_sibling_plan(grads):
    n = len(grads)

    def copies(ins_r, outs_r, send, recv):
        x, y, c = _me()
        return [_rcopy(_row_half(ins_r[t], 1 - c, (slice(None),)), outs_r[t], send, recv, t, (x, y, 1 - c))
                for t in range(n)]

    def start(*a):
        for cp in copies(*a):
            cp.start()

    def finish(*a):
        for cp in copies(*a):
            cp.wait()

    out_shapes = [SDS((4, g.shape[1], g.shape[2] // 2, g.shape[3]), BF16) for g in grads]
    return _Comm(list(grads), out_shapes, n, start, finish)


def _grads_to_chips_plan(psums):
    n = len(psums)

    def copies(ins_r, outs_r, send, recv):
        x, y, c = _me()
        return [_rcopy(ins_r[t].at[2 * px + py], outs_r[t].at[j], send, recv, 3 * t + j, (px, py, c))
                for j, (px, py) in enumerate(_chip_peers(x, y)) for t in range(n)]

    def start(*a):
        for cp in copies(*a):
            cp.start()

    def finish(*a):
        for cp in copies(*a):
            cp.wait()

    return _Comm(list(psums), [SDS((3,) + p.shape[1:], BF16) for p in psums], 3 * n, start, finish)


def _swap_halves(sums):
    n = len(sums)

    def body(*refs):
        out_refs = refs[n:2 * n]
        send_sems, recv_sems = refs[2 * n:]
        x, y, c = _me()
        sib = (x, y, 1 - c)
        cps = [_rcopy(_row_half(out_refs[t], c), _row_half(out_refs[t], c), send_sems, recv_sems, t, sib)
               for t in range(n)]
        for cp in cps:
            cp.start()
        for t in range(n):
            other = _row_half(out_refs[t], 1 - c)
            _rcopy(other, other, send_sems, recv_sems, t, sib).wait_recv()
        for cp in cps:
            cp.wait_send()

    return pl.pallas_call(
        body, in_specs=[ANY] * n, out_specs=[ANY] * n, out_shape=[SDS(s.shape, F32) for s in sums],
        input_output_aliases={t: t for t in range(n)},
        scratch_shapes=[pltpu.SemaphoreType.DMA((n,)), pltpu.SemaphoreType.DMA((n,))],
        name="swap_halves")(*sums)


def _allgather_small(part):
    def body(p_ref, out_ref, send_sems, recv_sems, local_sem):
        x, y, c = _me()
        me = 4 * x + 2 * y + c
        own = pltpu.make_async_copy(p_ref, out_ref.at[me], local_sem.at[0])
        own.start()
        sends = []
        for k in range(1, 8):
            fx, fy, fc = (k >> 2) & 1, (k >> 1) & 1, k & 1
            to = (x ^ fx, y ^ fy, c ^ fc)
            sends.append(_rcopy(p_ref, out_ref.at[me], send_sems, recv_sems, k - 1, to))
        for cp in sends:
            cp.start()
        for k in range(1, 8):
            slot = out_ref.at[me ^ k]
            _rcopy(slot, slot, send_sems, recv_sems, k - 1, (x, y, c)).wait_recv()
        for cp in sends:
            cp.wait_send()
        own.wait()

    return pl.pallas_call(
        body, in_specs=[ANY], out_specs=ANY, out_shape=SDS((8,) + part.shape, F32),
        scratch_shapes=[pltpu.SemaphoreType.DMA((7,)), pltpu.SemaphoreType.DMA((7,)), pltpu.SemaphoreType.DMA((1,))],
        name="allgather_small")(part)


BIG = (("ssd_w_in", 2), ("ssd_w_out", 1), ("sc_w_in", 2), ("sc_w_out", 1), ("ffn_w_up", 2), ("ffn_w_down", 1))


def _to_shards(full, axis):
    A, B = full.shape
    if axis == 2:
        return full.reshape(A, 4, B // 4).transpose(1, 0, 2)
    return full.reshape(4, A // 4, B)


def _from_shards(shards, axis):
    _, a, b = shards.shape
    if axis == 2:
        return shards.transpose(1, 0, 2).reshape(a, 4 * b)
    return shards.reshape(4 * a, b)


def _interleave(w, parts):
    lead, n = w.shape[:-1], w.shape[-1]
    return w.reshape(*lead, parts, n // (parts * LANE), LANE).swapaxes(-2, -3).reshape(*lead, n)


def _deinterleave(w, parts):
    lead, n = w.shape[:-1], w.shape[-1]
    return w.reshape(*lead, n // (parts * LANE), parts, LANE).swapaxes(-2, -3).reshape(*lead, n)


def _pack_rows(vectors, width, row_multiple):
    flat = jnp.concatenate(vectors, axis=-1)
    n = flat.shape[-1]
    unit = width * row_multiple
    total = -(-n // unit) * unit
    flat = jnp.pad(flat, [(0, 0)] * (flat.ndim - 1) + [(0, total - n)])
    return flat.reshape(*flat.shape[:-1], total // width, width)


def _unpack(flat, shapes):
    out, off = [], 0
    for s in shapes:
        n = int(np.prod(s))
        out.append(flat[..., off:off + n].reshape(*flat.shape[:-1], *s))
        off += n
    return out


def _memo(fn):
    cache = {}

    def wrapped(k):
        if k not in cache:
            cache[k] = fn(k)
        return cache[k]

    return wrapped


def _row(v):
    return v.reshape(1, -1)


def _pad_rows(w, rows=8):
    return jnp.pad(w, ((0, rows - w.shape[0]), (0, 0)))


def _ffn_fwd(x, g_pre, g_post, w_up, cw, cb, w_down, tag, comm_up=None, comm_down=None):
    (up, hn, a), got_up = _ffn_up(x, g_pre, w_up, cw, cb, "ffn_up" + tag, comm_up)
    (f, x_new), got_down = _matmul_norm_res(a, w_down, x, g_post, "ffn_down" + tag, comm_down)
    return x_new, (x, hn, up, a, f), got_up, got_down


def _ffn_down_dx(dx, f, g, w_down, up, cw, cb, name):
    L, D = dx.shape
    C = w_down.shape[0]
    tm = min(MM_ROWS, L)
    n = L // tm
    per = tm // (2 * HALO)

    def body(dx_ref, f_ref, g_ref, w_ref, up_ref, prev_ref, cw_ref, cb_ref, dup_ref, df_ref, dg_ref, st_ref,
             da_ref, head_ref):
        i = pl.program_id(0)
        tile = n - 1 - i

        @pl.when(i == 0)
        def _():
            dg_ref[...] = jnp.zeros_like(dg_ref)
            st_ref[...] = jnp.zeros_like(st_ref)
            head_ref[...] = jnp.zeros_like(head_ref)

        fv = f_ref[...]
        dxv = dx_ref[...]
        r = _rsq(fv)
        fh = fv * r
        dg_ref[...] += jnp.sum(dxv * fh, axis=0, keepdims=True)
        dyg = dxv * g_ref[...]
        df = (r * (dyg - fh * jnp.mean(dyg * fh, axis=-1, keepdims=True))).astype(BF16)
        df_ref[...] = df
        da_ref[...] = lax.dot_general(df, w_ref[...], NT, preferred_element_type=F32)
        rsum = lambda v: jnp.sum(v, axis=0, keepdims=True)
        for s in range(C // LANE):
            sl = slice(s * LANE, (s + 1) * LANE)
            w0, w1, w2 = cw_ref[0:1, sl], cw_ref[1:2, sl], cw_ref[2:3, sl]
            gp = up_ref[0, :, sl].astype(F32)
            prev = jnp.where(tile > 0, prev_ref[0, :, sl].astype(F32)[HALO:], 0.0)
            ext = jnp.concatenate([prev, gp], axis=0)
            g1, g2 = pltpu.roll(ext, 1, axis=0)[HALO:HALO + tm], pltpu.roll(ext, 2, axis=0)[HALO:HALO + tm]
            gate = gp * w2 + g1 * w1 + g2 * w0 + cb_ref[:, sl]
            sg = _sigmoid_fast(gate)
            da = da_ref[:, sl]
            dgate = da * up_ref[1, :, sl].astype(F32) * _dsilu(gate, sg)
            dext = jnp.concatenate([dgate, head_ref[:, sl]], axis=0)
            ahead = lambda k: pltpu.roll(dext, tm + HALO - k, axis=0)[:tm]
            dup_ref[0, :, sl] = (dgate * w2 + ahead(1) * w1 + ahead(2) * w0).astype(BF16)
            dup_ref[1, :, sl] = (da * gate * sg).astype(BF16)
            head_ref[:, sl] = dgate[:HALO]
            st_ref[0:1, sl] += rsum(dgate * g2)
            st_ref[1:2, sl] += rsum(dgate * g1)
            st_ref[2:3, sl] += rsum(dgate * gp)
            st_ref[3:4, sl] += rsum(dgate)

    rev = lambda i: (n - 1 - i, 0)
    fix = lambda i: (0, 0)
    pair = pl.BlockSpec((2, tm, C), lambda i: (0, n - 1 - i, 0))
    halo = pl.BlockSpec((2, 2 * HALO, C), lambda i: (0, jnp.maximum((n - 1 - i) * per - 1, 0), 0))
    return pl.pallas_call(
        body, grid=(n,),
        in_specs=[pl.BlockSpec((tm, D), rev), pl.BlockSpec((tm, D), rev), pl.BlockSpec((1, D), fix),
                  pl.BlockSpec((C, D), fix), pair, halo, pl.BlockSpec(cw.shape, fix), pl.BlockSpec(cb.shape, fix)],
        out_specs=[pair, pl.BlockSpec((tm, D), rev), pl.BlockSpec((1, D), fix), pl.BlockSpec((8, C), fix)],
        out_shape=[SDS((2, L, C), BF16), SDS((L, D), BF16), SDS((1, D), F32), SDS((8, C), F32)],
        scratch_shapes=[pltpu.VMEM((tm, C), F32), pltpu.VMEM((HALO, C), F32)],
        name=name, compiler_params=_cp("arbitrary"))(dx, f, g, w_down, up, up, cw, cb)


def _ffn_bwd(dx, saved, g_pre, g_post, w_up, cw, cb, w_down, tag):
    x, hn, up, a, f = saved
    dup, df, dg_post, stats = _ffn_down_dx(dx, f, g_post, w_down, up, cw, cb, "ffn_down_dx" + tag)
    dw_down = _matmul_tn(a, df, "ffn_down_dw" + tag)
    dx_in, dg_pre = _ffn_up_dx(dup, w_up, x, g_pre, dx, "ffn_up_dx" + tag)
    dw_up = _ffn_up_dw(hn, dup, "ffn_up_dw" + tag)
    return dx_in, dict(g_pre=dg_pre, g_post=dg_post, w_up=dw_up, w_down=dw_down, cw=stats[0:3], cb=stats[3])


def _sc_fwd(x, g_pre, g_post, w_in, cw, w_out, tag, comm_in=None, comm_out=None):
    (bcv, hn, q), got_in = _norm_matmul(x, g_pre, w_in, BF16, "sc_in" + tag, comm_in, _sc_gate_epilogue(cw))
    (m, x_new), got_out = _matmul_norm_res(q, w_out, x, g_post, "sc_out" + tag, comm_out)
    return x_new, (x, hn, bcv, q, m), got_in, got_out


def _sc_bwd(dx, saved, g_pre, g_post, w_in, cw, w_out, tag):
    x, hn, bcv, q, m = saved
    dq, dm, dg_post = _postnorm_bwd_matmul_nt(dx, m, g_post, w_out, BF16, "sc_out_dx" + tag)
    dw_out = _matmul_tn(q, dm, "sc_out_dw" + tag)
    dbcv, stats = _sc_mid_bwd(dq, bcv, cw, "sc_mid_bwd" + tag)
    (dx_in, dg_pre), _ = _matmul_nt_prenorm_bwd(dbcv, w_in, x, g_pre, dx, "sc_in_dx" + tag)
    dw_in = _matmul_tn(hn, dbcv, "sc_in_dw" + tag)
    return dx_in, dict(g_pre=dg_pre, g_post=dg_post, w_in=dw_in, w_out=dw_out, cw=stats[0:3])


def _ssd_fwd(x, g_pre, g_post, w_in, cw, cb, par, dexp, nw, w_out, consts, comm, tag, comm_in=None):
    d_inner = dexp.shape[1]
    (zx, hn, xbc), got_in = _norm_matmul(x, g_pre, w_in, F32, "ssd_in" + tag, comm_in,
                                         _ssd_conv_epilogue(cw, cb, d_inner))
    (yn, yf, states), got = _ssd_scan_fwd(zx, xbc, par, dexp, nw, consts, comm, "ssd_scan_fwd" + tag)
    if callable(w_out):
        w_out = w_out(got_in)
    (m, x_new), _ = _matmul_norm_res(yn, w_out, x, g_post, "ssd_out" + tag)
    return x_new, (x, hn, zx, xbc, yn, yf, states, m), got


def _ssd_bwd(dx, saved, g_pre, g_post, w_in, cw, cb, par, dexp, nw, w_out, consts, comm, tag, comm_dx=None):
    x, hn, zx, xbc, yn, yf, states, m = saved
    d_inner = dexp.shape[1]
    dyn, dm, dg_post = _postnorm_bwd_matmul_nt(dx, m, g_post, w_out, F32, "ssd_out_dx" + tag)
    dw_out = _matmul_tn(yn, dm, "ssd_out_dw" + tag)
    (dzx, dxbc, dnw, dpar), got = _ssd_scan_bwd(dyn, yf, zx, xbc, states, par, dexp, nw, consts, comm(dw_out),
                                                "ssd_scan_bwd" + tag)
    dzx, stats = _ssd_conv_bwd(dxbc, zx, cw, cb, d_inner, dzx, "ssd_conv_bwd" + tag)
    dw_in = _matmul_tn(hn, dzx, "ssd_in_dw" + tag)
    (dx_in, dg_pre), got_dx = _matmul_nt_prenorm_bwd(dzx, w_in, x, g_pre, dx, "ssd_in_dx" + tag,
                                                     None if comm_dx is None else comm_dx(dw_in))
    n_heads = d_inner // HEAD_DIM
    grads = dict(g_pre=dg_pre, g_post=dg_post, w_in=dw_in, w_out=dw_out, cw=stats[0:4], cb=stats[4],
                 dt_bias=dpar[0, :n_heads], a_log=dpar[1, :n_heads], d=dpar[2, :n_heads], nw=dnw[0])
    return dx_in, grads, got, got_dx


def kernel(x, mix_pre_g, mix_post_g, ffn_pre_g, ffn_post_g, ssd_w_in, ssd_conv_w, ssd_conv_b, ssd_dt_bias, ssd_A_log, ssd_D, ssd_norm_w, ssd_w_out, sc_w_in, sc_conv_w, sc_w_out, ffn_w_up, ffn_conv_w, ffn_conv_b, ffn_w_down, loss_target, m_mix_pre_g, m_mix_post_g, m_ffn_pre_g, m_ffn_post_g, m_ssd_w_in, m_ssd_conv_w, m_ssd_conv_b, m_ssd_dt_bias, m_ssd_A_log, m_ssd_D, m_ssd_norm_w, m_ssd_w_out, m_sc_w_in, m_sc_conv_w, m_sc_w_out, m_ffn_w_up, m_ffn_conv_w, m_ffn_conv_b, m_ffn_w_down, v_mix_pre_g, v_mix_post_g, v_ffn_pre_g, v_ffn_post_g, v_ssd_w_in, v_ssd_conv_w, v_ssd_conv_b, v_ssd_dt_bias, v_ssd_A_log, v_ssd_D, v_ssd_norm_w, v_ssd_w_out, v_sc_w_in, v_sc_conv_w, v_sc_w_out, v_ffn_w_up, v_ffn_conv_w, v_ffn_conv_b, v_ffn_w_down):
    names = ["mix_pre_g", "mix_post_g", "ffn_pre_g", "ffn_post_g", "ssd_w_in", "ssd_conv_w", "ssd_conv_b",
             "ssd_dt_bias", "ssd_A_log", "ssd_D", "ssd_norm_w", "ssd_w_out", "sc_w_in", "sc_conv_w", "sc_w_out",
             "ffn_w_up", "ffn_conv_w", "ffn_conv_b", "ffn_w_down"]
    env = locals()
    wts = {n: env[n] for n in names}
    mom = {n: env["m_" + n] for n in names}
    var = {n: env["v_" + n] for n in names}

    depth, d_model = mix_pre_g.shape
    n_ssd, n_heads = ssd_dt_bias.shape
    n_sc = sc_conv_w.shape[0]
    d_inner = n_heads * HEAD_DIM
    conv_dim = d_inner + 2 * N_GROUPS * D_STATE
    ssd_in_dim = d_inner + conv_dim + n_heads
    ssd_in_pad = d_inner + conv_dim + LANE
    q_chip = 2 * lax.axis_index("x") + lax.axis_index("y")
    core = lax.axis_index("c")

    assert depth == 4 and n_ssd == 2 and n_sc == 2, "the exchange schedule is written for this trunk"
    ssd_items = lambda j: [("ssd_w_in", j), ("ssd_w_out", j)]
    sc_items = lambda j: [("sc_w_in", j), ("sc_w_out", j)]
    ffn_items = lambda i: [("ffn_w_up", i), ("ffn_w_down", i)]
    gather_first = [("ssd_w_in", 0)]
    gather_in_ssd_in = {0: [("ssd_w_out", 0)]}
    gather_in_scan = {0: ffn_items(0) + sc_items(0), 2: ffn_items(2) + sc_items(1)}
    gather_in_ffn = {0: ([("ffn_w_up", 1)], [("ffn_w_down", 1)]), 1: ([("ssd_w_in", 1)], []),
                     2: ([("ffn_w_up", 3)], [("ffn_w_down", 3)])}
    gather_in_sc = {1: ([("ssd_w_out", 1)], [])}
    reduce_in_scan = {2: ffn_items(3) + sc_items(1) + ffn_items(2),
                      0: ssd_items(1) + ffn_items(1) + sc_items(0) + ffn_items(0) + [("ssd_w_out", 0)]}
    reduce_in_dx = {0: [("ssd_w_in", 0)]}
    axis_of = dict(BIG)
    gathered = {}

    def gather_plan(items, small=None):
        mine = [wts[n][layer:layer + 1].astype(BF16) for n, layer in items]
        return _allgather_plan(mine, small) if items else None

    def gather_done(items, results):
        for item, buf in zip(items, results):
            gathered[item] = buf[:, 0]

    def full(n, layer):
        return _from_shards(gathered[(n, layer)], axis_of[n])

    conv_names = ["ssd_conv_w", "sc_conv_w", "ffn_conv_w"]
    conv_shapes = [wts[n].shape for n in conv_names]
    small_mine = _pack_rows([wts[n].reshape(-1) for n in conv_names], LANE, 8)
    *results, small_all = _run_comm(gather_plan(gather_first, small_mine), "allgather_first")
    gather_done(gather_first, results)
    conv_full = {}
    for n, f, s in zip(conv_names, _unpack(small_all.reshape(4, -1), conv_shapes), conv_shapes):
        conv_full[n] = f.transpose(1, 2, 0, 3).reshape(s[0], s[1], 4 * s[2])

    consts = _scan_constants(n_heads)

    def ssd_args(j):
        par = jnp.zeros((8, LANE), F32).at[0, :n_heads].set(ssd_dt_bias[j]).at[1, :n_heads].set(ssd_A_log[j])
        dexp = jnp.repeat(ssd_D[j], HEAD_DIM).reshape(1, d_inner)
        w_in = jnp.pad(full("ssd_w_in", j), ((0, 0), (0, ssd_in_pad - ssd_in_dim)))
        return (w_in, _pad_rows(conv_full["ssd_conv_w"][j]), _row(ssd_conv_b[j]), par, dexp, _row(ssd_norm_w[j]))

    def sc_args(j):
        return (_interleave(full("sc_w_in", j), 3), _pad_rows(conv_full["sc_conv_w"][j]), full("sc_w_out", j))

    def ffn_args(i):
        return (gathered[("ffn_w_up", i)], _pad_rows(conv_full["ffn_conv_w"][i]), _row(ffn_conv_b[i]),
                full("ffn_w_down", i))

    ssd_args, sc_args, ffn_args = _memo(ssd_args), _memo(sc_args), _memo(ffn_args)

    h = x[0]
    saved = []
    for i in range(depth):
        j = i // 2
        gp, gq = _row(mix_pre_g[i]), _row(mix_post_g[i])
        if i % 2 == 0:
            items_in = gather_in_ssd_in.get(i, [])

            def w_out_when_here(got_in, items_in=items_in, j=j):
                gather_done(items_in, got_in)
                return full("ssd_w_out", j)

            h, sv, results = _ssd_fwd(h, gp, gq, *ssd_args(j), w_out_when_here, consts, gather_plan(gather_in_scan[i]),
                                      tag="", comm_in=gather_plan(items_in))
            gather_done(gather_in_scan[i], results)
        else:
            items_in, items_out = gather_in_sc.get(i, ([], []))
            h, sv, got_in, got_out = _sc_fwd(h, gp, gq, *sc_args(j), tag="", comm_in=gather_plan(items_in),
                                             comm_out=gather_plan(items_out))
            gather_done(items_in, got_in)
            gather_done(items_out, got_out)
        items_up, items_down = gather_in_ffn.get(i, ([], []))
        h, sv2, got_up, got_down = _ffn_fwd(h, _row(ffn_pre_g[i]), _row(ffn_post_g[i]), *ffn_args(i), tag="",
                                            comm_up=gather_plan(items_up), comm_down=gather_plan(items_down))
        gather_done(items_up, got_up)
        gather_done(items_down, got_down)
        saved.append((sv, sv2))
    dh, loss_part = _loss_head(h, loss_target[0], "loss_head")

    mix_grads, ffn_grads = [None] * depth, [None] * depth
    where = jnp.stack([q_chip, core]).astype(jnp.int32)

    def shard_grad(n, layer):
        if n == "ffn_w_up":
            g = ffn_grads[layer]["w_up"]
        elif n == "ffn_w_down":
            g = _to_shards(ffn_grads[layer]["w_down"], 1)
        elif n == "ssd_w_in":
            g = _to_shards(early[(n, layer)][:, :ssd_in_dim], 2)
        elif n == "ssd_w_out":
            g = _to_shards(early[(n, layer)], 1)
        elif n == "sc_w_in":
            g = _to_shards(_deinterleave(mix_grads[2 * layer + 1]["w_in"], 3), 2)
        else:
            g = _to_shards(mix_grads[2 * layer + 1]["w_out"], 1)
        return g[:, None]

    def reduce_begin(items, tag):
        by_shard = [shard_grad(n, layer) for n, layer in items]
        from_sib = _run_comm(_grads_to_sibling_plan(by_shard), "grads_to_sibling" + tag)
        chip_sums = [_add_core_halves(where, g, r, "add_core_halves_%s%d" % item)
                     for item, g, r in zip(items, by_shard, from_sib)]
        return by_shard, from_sib, _grads_to_chips_plan(chip_sums)

    sums = {}
    early = {}

    def riding(items, tag):
        by_shard, from_sib, plan = reduce_begin(items, tag)

        def arrived(from_chips):
            for (n, layer), g, r, rr in zip(items, by_shard, from_sib, from_chips):
                sums[n] = _sum_shard(where, g, r, rr, sums.get(n), layer, wts[n].shape[0],
                                     "sum_shard_%s%d" % (n, layer))

        return plan, arrived

    for i in reversed(range(depth)):
        j = i // 2
        sv, sv2 = saved[i]
        dh, ffn_grads[i] = _ffn_bwd(dh, sv2, _row(ffn_pre_g[i]), _row(ffn_post_g[i]), *ffn_args(i), tag="")
        gp, gq = _row(mix_pre_g[i]), _row(mix_post_g[i])
        if i % 2 == 0:
            then = {}

            def in_scan(dw_out, i=i, j=j, then=then):
                early[("ssd_w_out", j)] = dw_out
                plan, then["scan"] = riding(reduce_in_scan[i], "_%d" % i)
                return plan

            def in_dx(dw_in, i=i, j=j, then=then):
                early[("ssd_w_in", j)] = dw_in
                plan, then["dx"] = riding(reduce_in_dx[i], "_dx%d" % i) if i in reduce_in_dx else (None, None)
                return plan

            dh, mix_grads[i], got, got_dx = _ssd_bwd(dh, sv, gp, gq, *ssd_args(j), full("ssd_w_out", j), consts,
                                                     in_scan, tag="", comm_dx=in_dx)
            then["scan"](got)
            if then["dx"] is not None:
                then["dx"](got_dx)
        else:
            w_in, scw, w_out = sc_args(j)
            dh, mix_grads[i] = _sc_bwd(dh, sv, gp, gq, w_in, scw, w_out, tag="")
    grad_x = dh[None]
    big_grads = dict(zip([n for n, _ in BIG], _swap_halves([sums[n] for n, _ in BIG])))
    ssd_l = [mix_grads[i] for i in range(0, depth, 2)]
    sc_l = [mix_grads[i] for i in range(1, depth, 2)]
    stack = lambda layers, k: jnp.stack([g[k] for g in layers])

    small_names = ["mix_pre_g", "mix_post_g", "ffn_pre_g", "ffn_post_g", "ssd_conv_w", "ssd_conv_b", "ssd_dt_bias",
                   "ssd_A_log", "ssd_D", "ssd_norm_w", "sc_conv_w", "ffn_conv_w", "ffn_conv_b"]
    small_local = {
        "mix_pre_g": jnp.concatenate([g["g_pre"] for g in mix_grads]),
        "mix_post_g": jnp.concatenate([g["g_post"] for g in mix_grads]),
        "ffn_pre_g": jnp.concatenate([g["g_pre"] for g in ffn_grads]),
        "ffn_post_g": jnp.concatenate([g["g_post"] for g in ffn_grads]),
        "ssd_conv_w": stack(ssd_l, "cw"), "ssd_conv_b": stack(ssd_l, "cb"), "ssd_dt_bias": stack(ssd_l, "dt_bias"),
        "ssd_A_log": stack(ssd_l, "a_log"), "ssd_D": stack(ssd_l, "d"), "ssd_norm_w": stack(ssd_l, "nw"),
        "sc_conv_w": stack(sc_l, "cw"), "ffn_conv_w": stack(ffn_grads, "cw"), "ffn_conv_b": stack(ffn_grads, "cb"),
    }
    small_full_shapes = [small_local[n].shape for n in small_names]
    spack = _pack_rows([small_local[n].reshape(-1) for n in small_names] + [loss_part.reshape(-1)], LANE, 8)
    stotal = _sum_slots(_allgather_small(spack), "sum_small").reshape(-1)
    small_grads = dict(zip(small_names, _unpack(stotal, small_full_shapes)))
    loss = stotal[sum(int(np.prod(s)) for s in small_full_shapes)]
    for n in conv_names:
        width = wts[n].shape[-1]
        small_grads[n] = lax.dynamic_slice_in_dim(small_grads[n], q_chip * width, width, axis=2)

    grads, delta, new_m, new_v = {}, {}, {}, {}
    for n, _ in BIG:
        s = wts[n].shape
        two_d = lambda a: a.reshape(-1, s[-1])
        grads[n] = big_grads[n]
        d, mn, vn = _adamw(two_d(wts[n]), two_d(mom[n]), two_d(var[n]), two_d(grads[n]), "adamw_" + n)
        delta[n], new_m[n], new_v[n] = d.reshape(s), mn.reshape(s), vn.reshape(s)
    small_shapes = [wts[n].shape for n in small_names]
    pk = lambda d: _pack_rows([d[n].reshape(-1) for n in small_names], LANE, 8)
    for n in small_names:
        grads[n] = small_grads[n].reshape(wts[n].shape)
    d, mn, vn = _adamw(pk(wts), pk(mom), pk(var), pk(grads), "adamw_small")
    for out, packed in ((delta, d), (new_m, mn), (new_v, vn)):
        out.update(zip(small_names, _unpack(packed.reshape(-1), small_shapes)))

    return (loss, grad_x, *[grads[n] for n in names], *[delta[n] for n in names], *[new_m[n] for n in names],
            *[new_v[n] for n in names])
```
